```python
import math
import jax, jax.numpy as jnp
from jax import lax
import numpy as np

D_MODEL = 1024
BATCH = 16
SEQ = 256
DEPTH = 2
DEC_BATCH = 4
DEC_SEQ = 1024
PAST_LEN = 256

GRID_W = 64
N_HEADS = 16
HEAD_DIM = D_MODEL // N_HEADS
WIN_H = 8
WIN_W = 16
Q_BW = WIN_W
KEY_BW = 2 * WIN_W
CTX_QBLOCK = 128
D_FF = ((8 * D_MODEL // 3 + 255) // 256) * 256
N_BANDS = 16
PE_DIM = 1 + 2 * N_BANDS
FILT_W = 64
MOD_SHIFT = 0.05
DECAY_MIN = math.log(1e2) / 1.5
DECAY_MAX = math.log(1e2) / 0.3
N_HY = (DEPTH + 1) // 2
N_NA = DEPTH // 2
ALPHA = (2 * DEPTH) ** 0.25
BETA = (8 * DEPTH) ** -0.25
LN_EPS = 1e-5

kernel_name = "hyena_natten_deepnorm_dit_step"


def layer_norm(x, g, b):
    xf = x.astype(jnp.float32)
    mu = jnp.mean(xf, axis=-1, keepdims=True)
    var = jnp.mean(jnp.square(xf - mu), axis=-1, keepdims=True)
    return ((xf - mu) * lax.rsqrt(var + LN_EPS) * g + b).astype(x.dtype)


def ada_params(cvec, w, b):
    m = (jax.nn.silu(cvec) @ w + b)[:, None, :]
    return jnp.split(m, 6, axis=-1)


def centred_conv3(z, w, b):
    zp = jnp.pad(z, ((0, 0), (1, 1), (0, 0)))
    return zp[:, :-2] * w[0] + zp[:, 1:-1] * w[1] + zp[:, 2:] * w[2] + b


def hyena_filters(L, pe_w1, pe_b1, pe_w2, pe_b2, pe_w3, sin_freq, decay):
    t = jnp.linspace(0.0, 1.0, L, dtype=jnp.float32)[:, None]
    w = (2.0 * math.pi / L) * jnp.arange(L, dtype=jnp.float32)[:, None]
    bands = jnp.linspace(1e-4, N_BANDS - 1, N_BANDS, dtype=jnp.float32)[None, :]
    feats = jnp.concatenate([t, jnp.cos(bands * w), jnp.sin(bands * w)], axis=-1)
    hdn = jnp.sin(sin_freq[0] * (feats @ pe_w1 + pe_b1))
    hdn = jnp.sin(sin_freq[1] * (hdn @ pe_w2 + pe_b2))
    k = (hdn @ pe_w3).astype(jnp.float32) * (jnp.exp(-t * jnp.abs(decay)) + MOD_SHIFT)
    return k[:, :D_MODEL], k[:, D_MODEL:]


def bidir_long_conv(v, k_fwd, k_bwd, bias):
    L = v.shape[1]
    k_circ = jnp.concatenate([k_fwd, jnp.zeros((1, D_MODEL), jnp.float32), k_bwd[:0:-1]], axis=0)
    kf = jnp.fft.rfft(k_circ, axis=0)
    vf = jnp.fft.rfft(v.astype(jnp.float32), n=2 * L, axis=1)
    y = jnp.fft.irfft(vf * kf[None], n=2 * L, axis=1)[:, :L]
    return (y + v.astype(jnp.float32) * bias).astype(v.dtype)


def hyena_mixer(u, w_in, sconv_w, sconv_b, k_fwd, k_bwd, bias, w_out):
    z = centred_conv3(u @ w_in, sconv_w, sconv_b)
    x0, x1, v = jnp.split(z, 3, axis=-1)
    y = bidir_long_conv(v * x1, k_fwd, k_bwd, bias) * x0
    return y @ w_out


def split_heads(qkv):
    B, L, _ = qkv.shape
    qkv = qkv.reshape(B, L, 3, N_HEADS, HEAD_DIM).transpose(2, 0, 3, 1, 4)
    return qkv[0], qkv[1], qkv[2]


def merge_heads(o):
    B, H, L, d = o.shape
    return o.transpose(0, 2, 1, 3).reshape(B, L, H * d)


def na_context(h, w_qkv, w_out):
    B, S, _ = h.shape
    q, k, v = split_heads(h @ w_qkv)
    nb = S // CTX_QBLOCK
    qb = jnp.moveaxis(q.reshape(B, N_HEADS, nb, CTX_QBLOCK, HEAD_DIM), 2, 0) * HEAD_DIM ** -0.5

    def attend(qblk):
        s = jnp.einsum('bhqd,bhkd->bhqk', qblk, k).astype(jnp.float32)
        p = jax.nn.softmax(s, axis=-1).astype(v.dtype)
        return jnp.einsum('bhqk,bhkd->bhqd', p, v)

    o = jnp.moveaxis(lax.map(attend, qb), 0, 2).reshape(B, N_HEADS, S, HEAD_DIM)
    return merge_heads(o) @ w_out, k, v


def na_latent(h, k_ctx, v_ctx, w_qkv, rpb, w_out):
    B, L, _ = h.shape
    rows = L // GRID_W
    kh = min(WIN_H, rows)
    nj = GRID_W // Q_BW
    nk = kh * KEY_BW
    q, k, v = split_heads(h @ w_qkv)
    r = np.arange(rows)
    row0 = np.clip(r - kh // 2, 0, rows - kh)
    key_rows = row0[:, None] + np.arange(kh)[None, :]
    j = np.arange(nj)
    band0 = np.clip(j * Q_BW - WIN_W // 2, 0, GRID_W - KEY_BW)
    key_cols = band0[:, None] + np.arange(KEY_BW)[None, :]
    idx = (key_rows[:, None, :, None] * GRID_W + key_cols[None, :, None, :]).reshape(rows, nj, nk)
    kg = k[:, :, idx]
    vg = v[:, :, idx]
    qb = q.reshape(B, N_HEADS, rows, nj, Q_BW, HEAD_DIM) * HEAD_DIM ** -0.5
    q_cols = j[:, None] * Q_BW + np.arange(Q_BW)[None, :]
    win0 = np.clip(q_cols - WIN_W // 2, 0, GRID_W - WIN_W)
    kc = key_cols[:, None, :]
    col_ok = (kc >= win0[..., None]) & (kc < win0[..., None] + WIN_W)
    valid = np.broadcast_to(col_ok[:, :, None, :], (nj, Q_BW, kh, KEY_BW)).reshape(nj, Q_BW, nk)
    dr_i = (key_rows - r[:, None] + WIN_H - 1)[:, None, None, :, None]
    dc_i = np.clip(kc - q_cols[:, :, None] + WIN_W - 1, 0, 2 * WIN_W - 2)[None, :, :, None, :]
    bias = rpb[:, dr_i, dc_i].reshape(N_HEADS, rows, nj, Q_BW, nk).astype(jnp.float32)
    s_loc = jnp.einsum('bhrjqd,bhrjkd->bhrjqk', qb, kg).astype(jnp.float32) + bias[None]
    s_loc = jnp.where(valid, s_loc, -jnp.inf)
    s_ctx = jnp.einsum('bhrjqd,bhkd->bhrjqk', qb, k_ctx).astype(jnp.float32)
    p = jax.nn.softmax(jnp.concatenate([s_loc, s_ctx], axis=-1), axis=-1).astype(v.dtype)
    o = (jnp.einsum('bhrjqk,bhrjkd->bhrjqd', p[..., :nk], vg)
         + jnp.einsum('bhrjqk,bhkd->bhrjqd', p[..., nk:], v_ctx))
    o = o.reshape(B, N_HEADS, L, HEAD_DIM)
    return merge_heads(o) @ w_out


def swiglu(h, w_in, w_out):
    g, u = jnp.split(h @ w_in, 2, axis=-1)
    return (jax.nn.silu(g) * u) @ w_out


def setup_inputs(seed: int = 0) -> dict:
    key = jax.random.key(seed)
    ks = iter(jax.random.split(key, 40))
    f32 = jnp.float32

    def nrm(shape, scale):
        return jax.random.normal(next(ks), shape, f32) * scale

    D = D_MODEL
    return {
        "x_prompt": nrm((BATCH, SEQ, D), 1.0),
        "x_sample": nrm((DEC_BATCH, DEC_SEQ, D), 1.0),
        "cache_k": nrm((DEC_BATCH, N_NA, N_HEADS, PAST_LEN, HEAD_DIM), 1.0),
        "cache_v": nrm((DEC_BATCH, N_NA, N_HEADS, PAST_LEN, HEAD_DIM), 1.0),
        "c": nrm((DEC_BATCH, D), 1.0),
        "c_ctx": nrm((D,), 1.0),
        "ada_w": nrm((DEPTH, D, 6 * D), 0.5 * D ** -0.5),
        "ada_b": nrm((DEPTH, 6 * D), 0.02),
        "ln1_g": 1.0 + nrm((DEPTH, D), 0.02),
        "ln1_b": nrm((DEPTH, D), 0.02),
        "ln2_g": 1.0 + nrm((DEPTH, D), 0.02),
        "ln2_b": nrm((DEPTH, D), 0.02),
        "ffn_w_in": nrm((DEPTH, D, 2 * D_FF), D ** -0.5),
        "ffn_w_out": nrm((DEPTH, D_FF, D), BETA * D_FF ** -0.5),
        "hy_w_in": nrm((N_HY, D, 3 * D), D ** -0.5),
        "hy_sconv_w": nrm((N_HY, 3, 3 * D), 3 ** -0.5),
        "hy_sconv_b": nrm((N_HY, 3 * D), 0.02),
        "hy_pe_w1": nrm((N_HY, PE_DIM, FILT_W), PE_DIM ** -0.5),
        "hy_pe_b1": nrm((N_HY, FILT_W), 0.02),
        "hy_pe_w2": nrm((N_HY, FILT_W, FILT_W), FILT_W ** -0.5),
        "hy_pe_b2": nrm((N_HY, FILT_W), 0.02),
        "hy_pe_w3": nrm((N_HY, FILT_W, 2 * D), FILT_W ** -0.5),
        "hy_sin_freq": 1.0 + nrm((N_HY, 2, FILT_W), 0.1),
        "hy_decay": jax.random.uniform(next(ks), (N_HY, 2 * D), f32, DECAY_MIN, DECAY_MAX),
        "hy_bias": nrm((N_HY, D), 1.0),
        "hy_w_out": nrm((N_HY, D, D), BETA * D ** -0.5),
        "na_w_qkv": nrm((N_NA, D, 3 * D), D ** -0.5),
        "na_rpb": nrm((N_NA, N_HEADS, 2 * WIN_H - 1, 2 * WIN_W - 1), 0.1),
        "na_w_out": nrm((N_NA, D, D), BETA * D ** -0.5),
    }


def reference(x_prompt, x_sample, cache_k, cache_v, c, c_ctx, ada_w, ada_b, ln1_g, ln1_b, ln2_g, ln2_b,
              ffn_w_in, ffn_w_out, hy_w_in, hy_sconv_w, hy_sconv_b, hy_pe_w1, hy_pe_b1, hy_pe_w2, hy_pe_b2,
              hy_pe_w3, hy_sin_freq, hy_decay, hy_bias, hy_w_out, na_w_qkv, na_rpb, na_w_out):
    xp, xs = x_prompt, x_sample
    new_k, new_v = [], []
    for i in range(DEPTH):
        sh1_p, sc1_p, g1_p, sh2_p, sc2_p, g2_p = ada_params(c_ctx[None, :], ada_w[i], ada_b[i])
        sh1_s, sc1_s, g1_s, sh2_s, sc2_s, g2_s = ada_params(c, ada_w[i], ada_b[i])
        hp = xp * (1.0 + sc1_p) + sh1_p
        hs = xs * (1.0 + sc1_s) + sh1_s
        j = i // 2
        if i % 2 == 0:
            filt = (hy_pe_w1[j], hy_pe_b1[j], hy_pe_w2[j], hy_pe_b2[j], hy_pe_w3[j], hy_sin_freq[j], hy_decay[j])
            kf_p, kb_p = hyena_filters(xp.shape[1], *filt)
            kf_s, kb_s = hyena_filters(xs.shape[1], *filt)
            op = hyena_mixer(hp, hy_w_in[j], hy_sconv_w[j], hy_sconv_b[j], kf_p, kb_p, hy_bias[j], hy_w_out[j])
            os_ = hyena_mixer(hs, hy_w_in[j], hy_sconv_w[j], hy_sconv_b[j], kf_s, kb_s, hy_bias[j], hy_w_out[j])
        else:
            op, k_ctx, v_ctx = na_context(hp, na_w_qkv[j], na_w_out[j])
            new_k.append(k_ctx)
            new_v.append(v_ctx)
            os_ = na_latent(hs, cache_k[:, j], cache_v[:, j], na_w_qkv[j], na_rpb[j], na_w_out[j])
        xp = layer_norm(ALPHA * xp + g1_p * op, ln1_g[i], ln1_b[i])
        xs = layer_norm(ALPHA * xs + g1_s * os_, ln1_g[i], ln1_b[i])
        fp = swiglu(xp * (1.0 + sc2_p) + sh2_p, ffn_w_in[i], ffn_w_out[i])
        fs = swiglu(xs * (1.0 + sc2_s) + sh2_s, ffn_w_in[i], ffn_w_out[i])
        xp = layer_norm(ALPHA * xp + g2_p * fp, ln2_g[i], ln2_b[i])
        xs = layer_norm(ALPHA * xs + g2_s * fs, ln2_g[i], ln2_b[i])
    new_k_arr = jnp.stack(new_k, axis=1)
    new_v_arr = jnp.stack(new_v, axis=1)
    return (xp, xs, new_k_arr, new_v_arr)
```

```python
import functools
import math

import jax
import jax.numpy as jnp
import numpy as np
from jax import lax
from jax.experimental import pallas as pl
from jax.experimental.pallas import tpu as pltpu

D_MODEL = 1024
N_HEADS = 16
HEAD_DIM = D_MODEL // N_HEADS
D_FF = 2816
GRID_W = 64
WIN_H = 8
WIN_W = 16
N_BANDS = 16
PE_DIM = 1 + 2 * N_BANDS
FILT_W = 64
MOD_SHIFT = 0.05
DEPTH = 2
ALPHA = (2 * DEPTH) ** 0.25
LN_EPS = 1e-5
MASK_VALUE = -1e30

LANES = 128
VMEM_LIMIT = 56 * 1024 * 1024

F32 = jnp.float32
BF16 = jnp.bfloat16


def _bf(x):
    return x.astype(BF16)


def _dot(a, b):
    return jnp.dot(a, b, preferred_element_type=F32)


def _dot_nt(a, b):
    return lax.dot_general(a, b, (((1,), (1,)), ((), ())), preferred_element_type=F32)


def _layer_norm(x, g, b):
    mu = jnp.mean(x, axis=-1, keepdims=True)
    xc = x - mu
    var = jnp.mean(xc * xc, axis=-1, keepdims=True)
    return xc * lax.rsqrt(var + LN_EPS) * g + b


def _params(*sem):
    return pltpu.CompilerParams(dimension_semantics=sem, vmem_limit_bytes=VMEM_LIMIT)


@functools.lru_cache(maxsize=None)
def _dft_tables(L):
    n = 2 * L
    f = np.arange(L)[:, None]
    s = np.arange(L)[None, :]
    ang = 2.0 * np.pi * ((f * s) % n) / n
    c = np.cos(ang)
    sn = np.sin(ang)
    sn[0, :] = np.where(np.arange(L) % 2 == 0, 1.0, -1.0)
    fwd = np.concatenate([c, sn], axis=0)
    w = np.full((2 * L, 1), 2.0 / n)
    w[0, 0] = 1.0 / n
    w[L, 0] = 1.0 / n
    inv = (fwd * w).T
    alt = np.where(np.arange(L) % 2 == 0, 1.0, -1.0).astype(np.float32)[:, None]
    return (np.ascontiguousarray(fwd).astype(np.float32), np.ascontiguousarray(inv).astype(np.float32), alt)


@functools.lru_cache(maxsize=None)
def _filter_features(L):
    t = np.linspace(0.0, 1.0, L, dtype=np.float32)[:, None]
    w = (np.float32(2.0 * math.pi / L) * np.arange(L, dtype=np.float32))[:, None]
    bands = np.linspace(1e-4, N_BANDS - 1, N_BANDS, dtype=np.float32)[None, :]
    arg = (bands * w).astype(np.float64)
    feats = np.concatenate([t.astype(np.float64), np.cos(arg), np.sin(arg)], axis=-1)
    out = np.zeros((L, LANES), np.float32)
    out[:, :PE_DIM] = feats
    return out, t


def _ada_kernel(c_ref, w_ref, b_ref, o_ref):
    c = c_ref[...]
    s = c * jax.nn.sigmoid(c)
    s_hi = _bf(s)
    s_lo = _bf(s - s_hi.astype(F32))
    w = w_ref[...]
    w_hi = _bf(w)
    w_lo = _bf(w - w_hi.astype(F32))
    o_ref[...] = _dot(s_hi, w_hi) + _dot(s_lo, w_hi) + _dot(s_hi, w_lo) + b_ref[...]


def _ada(cvec, ada_w, ada_b):
    tn = 1536
    nt = 6 * D_MODEL // tn
    return pl.pallas_call(
        _ada_kernel,
        out_shape=jax.ShapeDtypeStruct((DEPTH, 8, 6 * D_MODEL), F32),
        grid=(DEPTH, nt),
        in_specs=[
            pl.BlockSpec((8, D_MODEL), lambda l, j: (0, 0)),
            pl.BlockSpec((None, D_MODEL, tn), lambda l, j: (l, 0, j)),
            pl.BlockSpec((None, 1, tn), lambda l, j: (l, 0, j)),
        ],
        out_specs=pl.BlockSpec((None, 8, tn), lambda l, j: (l, 0, j)),
        compiler_params=_params("arbitrary", "arbitrary"),
        name="ada_mod",
    )(cvec, ada_w, ada_b.reshape(DEPTH, 1, 6 * D_MODEL))


def _filter_kernel(feats_ref, t_ref, alt_ref, w1_ref, b1_ref, w2_ref, b2_ref, w3f_ref, w3b_ref, freq_ref,
                   decf_ref, decb_ref, fwd_ref, out_ref, *, L):
    hi = lax.Precision.HIGHEST
    h = jnp.sin(freq_ref[0:1, :] * (jnp.dot(feats_ref[...], w1_ref[...], precision=hi,
                                            preferred_element_type=F32) + b1_ref[...]))
    h = jnp.sin(freq_ref[1:2, :] * (jnp.dot(h, w2_ref[...], precision=hi,
                                            preferred_element_type=F32) + b2_ref[...]))
    t = t_ref[...]
    kf = jnp.dot(h, w3f_ref[...], precision=hi, preferred_element_type=F32) * (
        jnp.exp(-t * jnp.abs(decf_ref[...])) + MOD_SHIFT)
    kb = jnp.dot(h, w3b_ref[...], precision=hi, preferred_element_type=F32) * (
        jnp.exp(-t * jnp.abs(decb_ref[...])) + MOD_SHIFT)
    row = lax.broadcasted_iota(jnp.int32, kf.shape, 0)
    kb = jnp.where(row == 0, 0.0, kb)
    ksum = kf + kb
    kdiff = kf - kb
    ksum_hi = _bf(ksum)
    ksum_lo = _bf(ksum - ksum_hi.astype(F32))
    kdiff_hi = _bf(kdiff)
    kdiff_lo = _bf(kdiff - kdiff_hi.astype(F32))
    cos_m = fwd_ref[0:L, :]
    sin_m = fwd_ref[L:2 * L, :]
    k_cos = _dot(cos_m, ksum_hi) + _dot(cos_m, ksum_lo)
    k_sin = _dot(sin_m, kdiff_hi) + _dot(sin_m, kdiff_lo)
    k_nyq = jnp.sum(ksum * alt_ref[...], axis=0, keepdims=True)
    out_ref[0] = k_cos
    out_ref[1] = jnp.where(row == 0, 0.0, k_sin)
    out_ref[2] = jnp.where(row == 0, k_nyq, k_cos)


def _hyena_filter_spectrum(L, pe_w1, pe_b1, pe_w2, pe_b2, pe_w3, sin_freq, decay):
    feats, t = _filter_features(L)
    fwd, _, alt = _dft_tables(L)
    pad = LANES - FILT_W
    w1 = jnp.pad(pe_w1, ((0, LANES - PE_DIM), (0, pad)))
    b1 = jnp.pad(pe_b1, (0, pad)).reshape(1, LANES)
    w2 = jnp.pad(pe_w2, ((0, pad), (0, pad)))
    b2 = jnp.pad(pe_b2, (0, pad)).reshape(1, LANES)
    w3 = jnp.pad(pe_w3, ((0, pad), (0, 0)))
    freq = jnp.pad(sin_freq, ((0, 0), (0, pad)))
    dec = decay.reshape(1, 2 * D_MODEL)
    dt = 256
    nt = D_MODEL // dt
    full = lambda shape: pl.BlockSpec(shape, lambda j: (0,) * len(shape))
    return pl.pallas_call(
        functools.partial(_filter_kernel, L=L),
        out_shape=jax.ShapeDtypeStruct((3, L, D_MODEL), F32),
        grid=(nt,),
        in_specs=[
            full((L, LANES)), full((L, 1)), full((L, 1)),
            full((LANES, LANES)), full((1, LANES)), full((LANES, LANES)), full((1, LANES)),
            pl.BlockSpec((LANES, dt), lambda j: (0, j)),
            pl.BlockSpec((LANES, dt), lambda j: (0, nt + j)),
            full((2, LANES)),
            pl.BlockSpec((1, dt), lambda j: (0, j)),
            pl.BlockSpec((1, dt), lambda j: (0, nt + j)),
            full((2 * L, L)),
        ],
        out_specs=pl.BlockSpec((3, L, dt), lambda j: (0, 0, j)),
        compiler_params=_params("arbitrary"),
        name=f"hyena_filter_{L}",
    )(feats, t, alt, w1, b1, w2, b2, w3, w3, freq, dec, dec, _bf(jnp.asarray(fwd)))


def _hy_in_kernel(x_ref, mod_ref, w_ref, cw_ref, cb_ref, x0_ref, vx_ref, *, L, cn):
    tm = x_ref.shape[0]
    x = x_ref[...]
    h = _bf(x * (1.0 + mod_ref[:, D_MODEL:2 * D_MODEL]) + mod_ref[:, 0:D_MODEL])
    pos = lax.broadcasted_iota(jnp.int32, (tm, cn), 0) % L
    first = pos == 0
    last = pos == L - 1

    def conv(col):
        z = _dot(h, w_ref[:, col:col + cn])
        zm = jnp.where(first, 0.0, pltpu.roll(z, 1, 0))
        zp = jnp.where(last, 0.0, pltpu.roll(z, tm - 1, 0))
        return (zm * cw_ref[0:1, col:col + cn] + z * cw_ref[1:2, col:col + cn]
                + zp * cw_ref[2:3, col:col + cn] + cb_ref[:, col:col + cn])

    for c in range(D_MODEL // cn):
        x0_ref[:, c * cn:(c + 1) * cn] = _bf(conv(c * cn))
        x1 = conv(D_MODEL + c * cn)
        v = conv(2 * D_MODEL + c * cn)
        vx_ref[:, c * cn:(c + 1) * cn] = _bf(v * x1)


def _hy_in(x2d, mods, layer, row_fn, w_in, sconv_w, sconv_b, L, tm):
    n = x2d.shape[0]
    tok = pl.BlockSpec((tm, D_MODEL), lambda i: (i, 0))
    return pl.pallas_call(
        functools.partial(_hy_in_kernel, L=L, cn=512),
        out_shape=(jax.ShapeDtypeStruct((n, D_MODEL), BF16), jax.ShapeDtypeStruct((n, D_MODEL), BF16)),
        grid=(n // tm,),
        in_specs=[
            tok,
            pl.BlockSpec((None, None, 1, 6 * D_MODEL), lambda i: (layer, row_fn(i), 0, 0)),
            pl.BlockSpec((D_MODEL, 3 * D_MODEL), lambda i: (0, 0)),
            pl.BlockSpec((3, 3 * D_MODEL), lambda i: (0, 0)),
            pl.BlockSpec((1, 3 * D_MODEL), lambda i: (0, 0)),
        ],
        out_specs=(tok, tok),
        compiler_params=_params("arbitrary"),
        name=f"hyena_in_{L}",
    )(x2d, mods, w_in, sconv_w, sconv_b)


def _hy_conv_kernel(vx_ref, x0_ref, coef_ref, bias_ref, fwd_ref, inv_ref, u_ref, *, L, nseq):
    k_cos = coef_ref[0]
    k_sin = coef_ref[1]
    k_mix = coef_ref[2]
    for s in range(nseq):
        rows = slice(s * L, (s + 1) * L)
        vx = vx_ref[rows, :]
        spec = _dot(fwd_ref[...], vx)
        v_cos = spec[0:L]
        v_sin = spec[L:2 * L]
        y_spec = jnp.concatenate([v_cos * k_cos - v_sin * k_sin, v_cos * k_sin + v_sin * k_mix], axis=0)
        y = _dot(inv_ref[...], _bf(y_spec))
        u = (y + bias_ref[...] * vx.astype(F32)) * x0_ref[rows, :].astype(F32)
        u_ref[rows, :] = _bf(u)


def _hy_conv(vx, x0, coef, bias, L):
    n = vx.shape[0]
    tm = 1024
    dt = 512
    fwd, inv, _ = _dft_tables(L)
    tok = pl.BlockSpec((tm, dt), lambda j, i: (i, j))
    return pl.pallas_call(
        functools.partial(_hy_conv_kernel, L=L, nseq=tm // L),
        out_shape=jax.ShapeDtypeStruct((n, D_MODEL), BF16),
        grid=(D_MODEL // dt, n // tm),
        in_specs=[
            tok, tok,
            pl.BlockSpec((3, L, dt), lambda j, i: (0, 0, j)),
            pl.BlockSpec((1, dt), lambda j, i: (0, j)),
            pl.BlockSpec((2 * L, L), lambda j, i: (0, 0)),
            pl.BlockSpec((L, 2 * L), lambda j, i: (0, 0)),
        ],
        out_specs=tok,
        compiler_params=_params("arbitrary", "arbitrary"),
        name=f"hyena_conv_{L}",
    )(vx, x0, coef, bias, _bf(jnp.asarray(fwd)), _bf(jnp.asarray(inv)))


def _post_kernel(x_ref, a_ref, mod_ref, wo_ref, ln1g_ref, ln1b_ref, win_ref, wout_ref, ln2g_ref, ln2b_ref,
                 o_ref, act_ref, *, cf):
    d = D_MODEL
    g1 = mod_ref[:, 2 * d:3 * d]
    sh2 = mod_ref[:, 3 * d:4 * d]
    sc2 = mod_ref[:, 4 * d:5 * d]
    g2 = mod_ref[:, 5 * d:6 * d]
    mix = _dot(a_ref[...], wo_ref[...])
    x1 = _layer_norm(ALPHA * x_ref[...] + g1 * mix, ln1g_ref[...], ln1b_ref[...])
    h = _bf(x1 * (1.0 + sc2) + sh2)
    for c in range(D_FF // cf):
        gate = _dot(h, win_ref[:, c * cf:(c + 1) * cf])
        up = _dot(h, win_ref[:, D_FF + c * cf:D_FF + (c + 1) * cf])
        act_ref[:, c * cf:(c + 1) * cf] = _bf(gate * jax.nn.sigmoid(gate) * up)
    f = _dot(act_ref[...], wout_ref[...])
    o_ref[...] = _layer_norm(ALPHA * x1 + g2 * f, ln2g_ref[...], ln2b_ref[...])


def _post(x2d, a2d, mods, layer, row_fn, w_o, ln1_g, ln1_b, w_in, w_out, ln2_g, ln2_b, tm, name):
    n = x2d.shape[0]
    tok = pl.BlockSpec((tm, D_MODEL), lambda i: (i, 0))
    const = lambda shape: pl.BlockSpec(shape, lambda i: (0,) * len(shape), pipeline_mode=pl.Buffered(1))
    vec = pl.BlockSpec((1, D_MODEL), lambda i: (0, 0))
    return pl.pallas_call(
        functools.partial(_post_kernel, cf=256),
        out_shape=jax.ShapeDtypeStruct((n, D_MODEL), F32),
        grid=(n // tm,),
        in_specs=[
            tok, tok,
            pl.BlockSpec((None, None, 1, 6 * D_MODEL), lambda i: (layer, row_fn(i), 0, 0)),
            const((D_MODEL, D_MODEL)), vec, vec,
            const((D_MODEL, 2 * D_FF)), const((D_FF, D_MODEL)), vec, vec,
        ],
        out_specs=tok,
        scratch_shapes=[pltpu.VMEM((tm, D_FF), BF16)],
        compiler_params=_params("arbitrary"),
        name=name,
    )(x2d, a2d, mods, w_o, ln1_g.reshape(1, -1), ln1_b.reshape(1, -1), w_in, w_out,
      ln2_g.reshape(1, -1), ln2_b.reshape(1, -1))


def _qkv_kernel(x_ref, mod_ref, w_ref, qkv_ref, *kv_refs):
    d = D_MODEL
    h = _bf(x_ref[...] * (1.0 + mod_ref[:, d:2 * d]) + mod_ref[:, 0:d])
    qkv_ref[:, 0:d] = _bf(_dot(h, w_ref[:, 0:d]) * HEAD_DIM ** -0.5)
    k = _dot(h, w_ref[:, d:2 * d])
    qkv_ref[:, d:2 * d] = _bf(k)
    v = _dot(h, w_ref[:, 2 * d:3 * d])
    qkv_ref[:, 2 * d:3 * d] = _bf(v)
    if kv_refs:
        nk_ref, nv_ref = kv_refs
        for hh in range(N_HEADS):
            nk_ref[0, 0, hh] = k[:, hh * HEAD_DIM:(hh + 1) * HEAD_DIM]
            nv_ref[0, 0, hh] = v[:, hh * HEAD_DIM:(hh + 1) * HEAD_DIM]


def _qkv(x2d, mods, layer, row_fn, w_qkv, tm, seq_kv):
    n = x2d.shape[0]
    out_shape = [jax.ShapeDtypeStruct((n, 3 * D_MODEL), BF16)]
    out_specs = [pl.BlockSpec((tm, 3 * D_MODEL), lambda i: (i, 0))]
    if seq_kv:
        nb = n // tm
        kv_shape = jax.ShapeDtypeStruct((nb, 1, N_HEADS, tm, HEAD_DIM), F32)
        kv_spec = pl.BlockSpec((1, 1, N_HEADS, tm, HEAD_DIM), lambda i: (i, 0, 0, 0, 0))
        out_shape += [kv_shape, kv_shape]
        out_specs += [kv_spec, kv_spec]
    return pl.pallas_call(
        _qkv_kernel,
        out_shape=tuple(out_shape),
        grid=(n // tm,),
        in_specs=[
            pl.BlockSpec((tm, D_MODEL), lambda i: (i, 0)),
            pl.BlockSpec((None, None, 1, 6 * D_MODEL), lambda i: (layer, row_fn(i), 0, 0)),
            pl.BlockSpec((D_MODEL, 3 * D_MODEL), lambda i: (0, 0)),
        ],
        out_specs=tuple(out_specs),
        compiler_params=_params("arbitrary"),
        name="qkv_ctx" if seq_kv else "qkv_lat",
    )(x2d, mods, w_qkv)


def _ctx_attn_kernel(q_ref, k_ref, v_ref, o_ref):
    for hp in range(N_HEADS // 2):
        outs = []
        for hh in (2 * hp, 2 * hp + 1):
            sl = slice(hh * HEAD_DIM, (hh + 1) * HEAD_DIM)
            s = _dot_nt(q_ref[:, sl], k_ref[:, sl])
            m = jnp.max(s, axis=-1, keepdims=True)
            p = jnp.exp(s - m)
            l = jnp.sum(p, axis=-1, keepdims=True)
            outs.append(_dot(_bf(p), v_ref[:, sl]) * (1.0 / l))
        o_ref[:, 2 * hp * HEAD_DIM:(2 * hp + 2) * HEAD_DIM] = _bf(jnp.concatenate(outs, axis=1))


def _ctx_attn(qkv, seq):
    n = qkv.shape[0]
    return pl.pallas_call(
        _ctx_attn_kernel,
        out_shape=jax.ShapeDtypeStruct((n, D_MODEL), BF16),
        grid=(n // seq,),
        in_specs=[pl.BlockSpec((seq, D_MODEL), lambda b, c=c: (b, c)) for c in range(3)],
        out_specs=pl.BlockSpec((seq, D_MODEL), lambda b: (b, 0)),
        compiler_params=_params("arbitrary"),
        name="ctx_attn",
    )(qkv, qkv, qkv)


LAT_ROWS = 16
HALF_ROWS = LAT_ROWS // 2
LOCAL_ROWS = 12
HALF_KEY_ROW0 = (0, LAT_ROWS - LOCAL_ROWS)


def _bias_kernel(rpb_ref, tab_ref):
    shape = (GRID_W, LANES)
    qc = lax.broadcasted_iota(jnp.int32, shape, 0)
    lane = lax.broadcasted_iota(jnp.int32, shape, 1)
    kc = lane % GRID_W
    win0 = jnp.clip(qc - WIN_W // 2, 0, GRID_W - WIN_W)
    col_ok = (kc >= win0) & (kc < win0 + WIN_W)
    masked = jnp.full(shape, MASK_VALUE, F32)
    tiles = []
    for dr in range(2 * WIN_H - 1):
        row = jnp.broadcast_to(rpb_ref[dr:dr + 1, :], shape)
        t = pltpu.roll(row, LANES - (WIN_W - 1), 1, stride=1, stride_axis=0)
        t = jnp.where(lane < GRID_W, t, pltpu.roll(t, GRID_W, 1))
        tiles.append(jnp.where(col_ok, t, MASK_VALUE))

    def tile(qr, kr):
        row0 = min(max(qr - WIN_H // 2, 0), LAT_ROWS - WIN_H)
        if row0 <= kr < row0 + WIN_H:
            return tiles[kr - qr + WIN_H - 1]
        return masked

    for qr in range(LAT_ROWS):
        kr0 = HALF_KEY_ROW0[qr // HALF_ROWS]
        for i in range(0, LOCAL_ROWS, 2):
            both = jnp.where(lane < GRID_W, tile(qr, kr0 + i), tile(qr, kr0 + i + 1))
            tab_ref[qr * GRID_W:(qr + 1) * GRID_W, i * GRID_W:(i + 2) * GRID_W] = both


def _bias_table(rpb):
    rpb = jnp.pad(rpb, ((0, 0), (0, 1), (0, LANES - (2 * WIN_W - 1))))
    nq = LAT_ROWS * GRID_W
    nk = LOCAL_ROWS * GRID_W
    return pl.pallas_call(
        _bias_kernel,
        out_shape=jax.ShapeDtypeStruct((N_HEADS, nq, nk), F32),
        grid=(N_HEADS,),
        in_specs=[pl.BlockSpec((None, 2 * WIN_H, LANES), lambda h: (h, 0, 0))],
        out_specs=pl.BlockSpec((None, nq, nk), lambda h: (h, 0, 0)),
        compiler_params=_params("arbitrary"),
        name="na_bias_table",
    )(rpb)


def _lat_attn_kernel(q_ref, k_ref, v_ref, ck_ref, cv_ref, tab_ref, o_ref):
    nq = HALF_ROWS * GRID_W
    nk = LOCAL_ROWS * GRID_W
    outs = []
    for hh in range(2):
        sl = slice(hh * HEAD_DIM, (hh + 1) * HEAD_DIM)
        k_ctx = _bf(ck_ref[0, 0, hh])
        v_ctx = _bf(cv_ref[0, 0, hh])
        halves = []
        for half in range(2):
            key0 = HALF_KEY_ROW0[half] * GRID_W
            q = q_ref[half * nq:(half + 1) * nq, sl]
            s_loc = _dot_nt(q, k_ref[key0:key0 + nk, sl]) + tab_ref[hh, half * nq:(half + 1) * nq, :]
            s_ctx = _dot_nt(q, k_ctx)
            m = jnp.maximum(jnp.max(s_loc, axis=-1, keepdims=True), jnp.max(s_ctx, axis=-1, keepdims=True))
            p_loc = jnp.exp(s_loc - m)
            p_ctx = jnp.exp(s_ctx - m)
            l = jnp.sum(p_loc, axis=-1, keepdims=True) + jnp.sum(p_ctx, axis=-1, keepdims=True)
            o = _dot(_bf(p_loc), v_ref[key0:key0 + nk, sl]) + _dot(_bf(p_ctx), v_ctx)
            halves.append(o * (1.0 / l))
        outs.append(jnp.concatenate(halves, axis=0))
    o_ref[...] = _bf(jnp.concatenate(outs, axis=1))


def _lat_attn(qkv, cache_k, cache_v, table, layer_idx):
    n = qkv.shape[0]
    seq = LAT_ROWS * GRID_W
    nb = n // seq
    nhp = N_HEADS // 2
    past = cache_k.shape[3]
    tok = lambda c: pl.BlockSpec((seq, 2 * HEAD_DIM), lambda hp, b, c=c: (b, c * nhp + hp))
    cache = pl.BlockSpec((1, 1, 2, past, HEAD_DIM), lambda hp, b: (b, layer_idx, hp, 0, 0))
    return pl.pallas_call(
        _lat_attn_kernel,
        out_shape=jax.ShapeDtypeStruct((n, D_MODEL), BF16),
        grid=(nhp, nb),
        in_specs=[tok(0), tok(1), tok(2), cache, cache,
                  pl.BlockSpec((2, seq, LOCAL_ROWS * GRID_W), lambda hp, b: (hp, 0, 0))],
        out_specs=pl.BlockSpec((seq, 2 * HEAD_DIM), lambda hp, b: (b, hp)),
        compiler_params=_params("arbitrary", "arbitrary"),
        name="lat_attn",
    )(qkv, qkv, qkv, cache_k, cache_v, table)


def kernel(x_prompt, x_sample, cache_k, cache_v, c, c_ctx, ada_w, ada_b, ln1_g, ln1_b, ln2_g, ln2_b, ffn_w_in, ffn_w_out, hy_w_in, hy_sconv_w, hy_sconv_b, hy_pe_w1, hy_pe_b1, hy_pe_w2, hy_pe_b2, hy_pe_w3, hy_sin_freq, hy_decay, hy_bias, hy_w_out, na_w_qkv, na_rpb, na_w_out):
    nbp, lp, d = x_prompt.shape
    nbs, ls, _ = x_sample.shape
    assert d == D_MODEL and ls == LAT_ROWS * GRID_W and c.shape[0] == nbs
    xp = x_prompt.reshape(nbp * lp, d)
    xs = x_sample.reshape(nbs * ls, d)

    cvec = jnp.concatenate([c_ctx[None, :], c, jnp.zeros((8 - 1 - nbs, d), F32)], axis=0)
    mods = _ada(cvec, ada_w, ada_b).reshape(DEPTH, 8, 1, 6 * d)
    row_p = lambda i: 0
    tm = 512
    row_s = lambda i: 1 + (i * tm) // ls

    filt = (hy_pe_w1[0], hy_pe_b1[0], hy_pe_w2[0], hy_pe_b2[0], hy_pe_w3[0], hy_sin_freq[0], hy_decay[0])
    coef_p = _hyena_filter_spectrum(lp, *filt)
    coef_s = _hyena_filter_spectrum(ls, *filt)
    w_in = _bf(hy_w_in[0])
    sconv_b = hy_sconv_b[0].reshape(1, -1)
    hy_bias0 = hy_bias[0].reshape(1, -1)
    x0_p, vx_p = _hy_in(xp, mods, 0, row_p, w_in, hy_sconv_w[0], sconv_b, lp, tm)
    x0_s, vx_s = _hy_in(xs, mods, 0, lambda i: 1 + i, w_in, hy_sconv_w[0], sconv_b, ls, ls)
    u_p = _hy_conv(vx_p, x0_p, coef_p, hy_bias0, lp)
    u_s = _hy_conv(vx_s, x0_s, coef_s, hy_bias0, ls)
    post0 = (_bf(hy_w_out[0]), ln1_g[0], ln1_b[0], _bf(ffn_w_in[0]), _bf(ffn_w_out[0]), ln2_g[0], ln2_b[0])
    xp = _post(xp, u_p, mods, 0, row_p, *post0, tm=tm, name="post0_ctx")
    xs = _post(xs, u_s, mods, 0, row_s, *post0, tm=tm, name="post0_lat")

    w_qkv = _bf(na_w_qkv[0])
    qkv_p, new_k, new_v = _qkv(xp, mods, 1, row_p, w_qkv, lp, True)
    (qkv_s,) = _qkv(xs, mods, 1, row_s, w_qkv, tm, False)
    a_p = _ctx_attn(qkv_p, lp)
    a_s = _lat_attn(qkv_s, cache_k, cache_v, _bias_table(na_rpb[0]), 0)
    post1 = (_bf(na_w_out[0]), ln1_g[1], ln1_b[1], _bf(ffn_w_in[1]), _bf(ffn_w_out[1]), ln2_g[1], ln2_b[1])
    xp = _post(xp, a_p, mods, 1, row_p, *post1, tm=tm, name="post1_ctx")
    xs = _post(xs, a_s, mods, 1, row_s, *post1, tm=tm, name="post1_lat")

    return (xp.reshape(nbp, lp, d), xs.reshape(nbs, ls, d), new_k, new_v)
```

```python
import functools
import math

import jax
import jax.numpy as jnp
import numpy as np
from jax import lax
from jax.experimental import pallas as pl
from jax.experimental.pallas import tpu as pltpu

D_MODEL = 1024
N_HEADS = 16
HEAD_DIM = D_MODEL // N_HEADS
D_FF = 2816
GRID_W = 64
WIN_H = 8
WIN_W = 16
N_BANDS = 16
PE_DIM = 1 + 2 * N_BANDS
FILT_W = 64
MOD_SHIFT = 0.05
DEPTH = 2
ALPHA = (2 * DEPTH) ** 0.25
LN_EPS = 1e-5
MASK_VALUE = -1e30

LANES = 128
SUBLANES = 8
VMEM_LIMIT = 56 * 1024 * 1024

F32 = jnp.float32
BF16 = jnp.bfloat16


def _bf(x):
    return x.astype(BF16)


def _dot(a, b):
    return jnp.dot(a, b, preferred_element_type=F32)


def _dot_nt(a, b):
    return lax.dot_general(a, b, (((1,), (1,)), ((), ())), preferred_element_type=F32)


def _layer_norm(x, g, b):
    mu = jnp.mean(x, axis=-1, keepdims=True)
    xc = x - mu
    var = jnp.mean(xc * xc, axis=-1, keepdims=True)
    return xc * lax.rsqrt(var + LN_EPS) * g + b


def _params(*sem):
    return pltpu.CompilerParams(dimension_semantics=sem, vmem_limit_bytes=VMEM_LIMIT)


@functools.lru_cache(maxsize=None)
def _dft_tables(L):
    n = 2 * L
    f = np.arange(L)[:, None]
    s = np.arange(L)[None, :]
    ang = 2.0 * np.pi * ((f * s) % n) / n
    c = np.cos(ang)
    sn = np.sin(ang)
    sn[0, :] = np.where(np.arange(L) % 2 == 0, 1.0, -1.0)
    fwd = np.concatenate([c, sn], axis=0)
    w = np.full((2 * L, 1), 2.0 / n)
    w[0, 0] = 1.0 / n
    w[L, 0] = 1.0 / n
    inv = (fwd * w).T
    alt = np.where(np.arange(L) % 2 == 0, 1.0, -1.0).astype(np.float32)[:, None]
    return (np.ascontiguousarray(fwd).astype(np.float32), np.ascontiguousarray(inv).astype(np.float32), alt)


@functools.lru_cache(maxsize=None)
def _filter_features(L):
    t = np.linspace(0.0, 1.0, L, dtype=np.float32)[:, None]
    w = (np.float32(2.0 * math.pi / L) * np.arange(L, dtype=np.float32))[:, None]
    bands = np.linspace(1e-4, N_BANDS - 1, N_BANDS, dtype=np.float32)[None, :]
    arg = (bands * w).astype(np.float64)
    feats = np.concatenate([t.astype(np.float64), np.cos(arg), np.sin(arg)], axis=-1)
    out = np.zeros((L, LANES), np.float32)
    out[:, :PE_DIM] = feats
    return out, t


def _ada_kernel(c_ref, w_ref, b_ref, o_ref):
    c = c_ref[...]
    s = c * jax.nn.sigmoid(c)
    s_hi = _bf(s)
    s_lo = _bf(s - s_hi.astype(F32))
    w = w_ref[...]
    w_hi = _bf(w)
    w_lo = _bf(w - w_hi.astype(F32))
    o_ref[...] = _dot(s_hi, w_hi) + _dot(s_lo, w_hi) + _dot(s_hi, w_lo) + b_ref[...]


def _ada(cvec, ada_w, ada_b):
    tn = 1536
    nt = 6 * D_MODEL // tn
    return pl.pallas_call(
        _ada_kernel,
        out_shape=jax.ShapeDtypeStruct((DEPTH, 8, 6 * D_MODEL), F32),
        grid=(DEPTH, nt),
        in_specs=[
            pl.BlockSpec((8, D_MODEL), lambda l, j: (0, 0)),
            pl.BlockSpec((None, D_MODEL, tn), lambda l, j: (l, 0, j)),
            pl.BlockSpec((None, 1, tn), lambda l, j: (l, 0, j)),
        ],
        out_specs=pl.BlockSpec((None, 8, tn), lambda l, j: (l, 0, j)),
        compiler_params=_params("arbitrary", "arbitrary"),
        name="ada_mod",
    )(cvec, ada_w, ada_b.reshape(DEPTH, 1, 6 * D_MODEL))


def _filter_kernel(feats_ref, t_ref, alt_ref, w1_ref, b1_ref, w2_ref, b2_ref, w3f_ref, w3b_ref, freq_ref,
                   decf_ref, decb_ref, fwd_ref, out_ref, h_ref, *, L):
    hi = lax.Precision.HIGHEST

    @pl.when(pl.program_id(0) == 0)
    def _():
        h1 = jnp.sin(freq_ref[0:1, :] * (jnp.dot(feats_ref[...], w1_ref[...], precision=hi,
                                                 preferred_element_type=F32) + b1_ref[...]))
        h_ref[...] = jnp.sin(freq_ref[1:2, :] * (jnp.dot(h1, w2_ref[...], precision=hi,
                                                         preferred_element_type=F32) + b2_ref[...]))

    h = h_ref[...]
    t = t_ref[...]
    kf = jnp.dot(h, w3f_ref[...], precision=hi, preferred_element_type=F32) * (
        jnp.exp(-t * jnp.abs(decf_ref[...])) + MOD_SHIFT)
    kb = jnp.dot(h, w3b_ref[...], precision=hi, preferred_element_type=F32) * (
        jnp.exp(-t * jnp.abs(decb_ref[...])) + MOD_SHIFT)
    row = lax.broadcasted_iota(jnp.int32, kf.shape, 0)
    kb = jnp.where(row == 0, 0.0, kb)
    ksum = kf + kb
    kdiff = kf - kb
    ksum_hi = _bf(ksum)
    ksum_lo = _bf(ksum - ksum_hi.astype(F32))
    kdiff_hi = _bf(kdiff)
    kdiff_lo = _bf(kdiff - kdiff_hi.astype(F32))
    cos_m = fwd_ref[0:L, :]
    sin_m = fwd_ref[L:2 * L, :]
    k_cos = _dot(cos_m, ksum_hi) + _dot(cos_m, ksum_lo)
    k_sin = _dot(sin_m, kdiff_hi) + _dot(sin_m, kdiff_lo)
    k_nyq = jnp.sum(ksum * alt_ref[...], axis=0, keepdims=True)
    out_ref[0] = k_cos
    out_ref[1] = jnp.where(row == 0, 0.0, k_sin)
    out_ref[2] = jnp.where(row == 0, k_nyq, k_cos)


def _hyena_filter_spectrum(L, pe_w1, pe_b1, pe_w2, pe_b2, pe_w3, sin_freq, decay):
    feats, t = _filter_features(L)
    fwd, _, alt = _dft_tables(L)
    pad = LANES - FILT_W
    w1 = jnp.pad(pe_w1, ((0, LANES - PE_DIM), (0, pad)))
    b1 = jnp.pad(pe_b1, (0, pad)).reshape(1, LANES)
    w2 = jnp.pad(pe_w2, ((0, pad), (0, pad)))
    b2 = jnp.pad(pe_b2, (0, pad)).reshape(1, LANES)
    w3 = jnp.pad(pe_w3, ((0, pad), (0, 0)))
    freq = jnp.pad(sin_freq, ((0, 0), (0, pad)))
    dec = decay.reshape(1, 2 * D_MODEL)
    dt = 256
    nt = D_MODEL // dt
    full = lambda shape: pl.BlockSpec(shape, lambda j: (0,) * len(shape))
    return pl.pallas_call(
        functools.partial(_filter_kernel, L=L),
        out_shape=jax.ShapeDtypeStruct((3, L, D_MODEL), F32),
        grid=(nt,),
        in_specs=[
            full((L, LANES)), full((L, 1)), full((L, 1)),
            full((LANES, LANES)), full((1, LANES)), full((LANES, LANES)), full((1, LANES)),
            pl.BlockSpec((LANES, dt), lambda j: (0, j)),
            pl.BlockSpec((LANES, dt), lambda j: (0, nt + j)),
            full((2, LANES)),
            pl.BlockSpec((1, dt), lambda j: (0, j)),
            pl.BlockSpec((1, dt), lambda j: (0, nt + j)),
            full((2 * L, L)),
        ],
        out_specs=pl.BlockSpec((3, L, dt), lambda j: (0, 0, j)),
        scratch_shapes=[pltpu.VMEM((L, LANES), F32)],
        compiler_params=_params("arbitrary"),
        name=f"hyena_filter_{L}",
    )(feats, t, alt, w1, b1, w2, b2, w3, w3, freq, dec, dec, _bf(jnp.asarray(fwd)))


def _hy_in_kernel(x_ref, mod_ref, w_ref, cw_ref, cb_ref, x0_ref, vx_ref, *, L, cn):
    tm = x_ref.shape[0]
    x = x_ref[...]
    h = _bf(x * (1.0 + mod_ref[:, D_MODEL:2 * D_MODEL]) + mod_ref[:, 0:D_MODEL])
    row = lax.broadcasted_iota(jnp.int32, (SUBLANES, cn), 0)

    def conv(col):
        z = _dot(h, w_ref[:, col:col + cn])
        zm = pltpu.roll(z, 1, 0)
        zp = pltpu.roll(z, tm - 1, 0)
        w0 = cw_ref[0:1, col:col + cn]
        w1 = cw_ref[1:2, col:col + cn]
        w2 = cw_ref[2:3, col:col + cn]
        b = cb_ref[:, col:col + cn]
        out = zm * w0 + z * w1 + zp * w2 + b
        pieces = []
        for s in range(tm // L):
            lo = s * L
            hi = lo + L - SUBLANES
            head = (jnp.where(row == 0, 0.0, zm[lo:lo + SUBLANES]) * w0 + z[lo:lo + SUBLANES] * w1
                    + zp[lo:lo + SUBLANES] * w2 + b)
            tail = (zm[hi:hi + SUBLANES] * w0 + z[hi:hi + SUBLANES] * w1
                    + jnp.where(row == SUBLANES - 1, 0.0, zp[hi:hi + SUBLANES]) * w2 + b)
            pieces += [head, out[lo + SUBLANES:hi], tail]
        return jnp.concatenate(pieces, axis=0)

    for c in range(D_MODEL // cn):
        x0_ref[:, c * cn:(c + 1) * cn] = _bf(conv(c * cn))
        x1 = conv(D_MODEL + c * cn)
        v = conv(2 * D_MODEL + c * cn)
        vx_ref[:, c * cn:(c + 1) * cn] = _bf(v * x1)


def _hy_in(x2d, mods, layer, row_fn, w_in, sconv_w, sconv_b, L, tm):
    n = x2d.shape[0]
    cn = 256
    tok = pl.BlockSpec((tm, D_MODEL), lambda i: (i, 0))
    return pl.pallas_call(
        functools.partial(_hy_in_kernel, L=L, cn=cn),
        out_shape=(jax.ShapeDtypeStruct((n, D_MODEL), BF16), jax.ShapeDtypeStruct((n, D_MODEL), BF16)),
        grid=(n // tm,),
        in_specs=[
            tok,
            pl.BlockSpec((None, None, 1, 6 * D_MODEL), lambda i: (layer, row_fn(i), 0, 0)),
            pl.BlockSpec((D_MODEL, 3 * D_MODEL), lambda i: (0, 0)),
            pl.BlockSpec((3, 3 * D_MODEL), lambda i: (0, 0)),
            pl.BlockSpec((1, 3 * D_MODEL), lambda i: (0, 0)),
        ],
        out_specs=(tok, tok),
        compiler_params=_params("arbitrary"),
        name=f"hyena_in_{L}",
    )(x2d, mods, w_in, sconv_w, sconv_b)


def _hy_conv_kernel(vx_ref, x0_ref, coef_ref, bias_ref, fwd_ref, inv_ref, u_ref, *, L, nseq, cn, mt):
    for c in range(vx_ref.shape[1] // cn):
        cols = slice(c * cn, (c + 1) * cn)
        k_cos = coef_ref[0, :, cols]
        k_sin = coef_ref[1, :, cols]
        k_mix = coef_ref[2, :, cols]
        for s in range(nseq):
            vx = vx_ref[s * L:(s + 1) * L, cols]
            spec = _dot(fwd_ref[...], vx)
            v_cos = spec[0:L]
            v_sin = spec[L:2 * L]
            y_spec = _bf(jnp.concatenate([v_cos * k_cos - v_sin * k_sin, v_cos * k_sin + v_sin * k_mix],
                                         axis=0))
            for m in range(L // mt):
                rows = slice(s * L + m * mt, s * L + (m + 1) * mt)
                y = _dot(inv_ref[m * mt:(m + 1) * mt, :], y_spec)
                u = (y + bias_ref[:, cols] * vx_ref[rows, cols].astype(F32)) * x0_ref[rows, cols].astype(F32)
                u_ref[rows, cols] = _bf(u)


def _hy_conv(vx, x0, coef, bias, L):
    n = vx.shape[0]
    tm = 1024
    dt = 512
    fwd, inv, _ = _dft_tables(L)
    tok = pl.BlockSpec((tm, dt), lambda j, i: (i, j))
    return pl.pallas_call(
        functools.partial(_hy_conv_kernel, L=L, nseq=tm // L, cn=256 if L <= 256 else dt, mt=L),
        out_shape=jax.ShapeDtypeStruct((n, D_MODEL), BF16),
        grid=(D_MODEL // dt, n // tm),
        in_specs=[
            tok, tok,
            pl.BlockSpec((3, L, dt), lambda j, i: (0, 0, j)),
            pl.BlockSpec((1, dt), lambda j, i: (0, j)),
            pl.BlockSpec((2 * L, L), lambda j, i: (0, 0)),
            pl.BlockSpec((L, 2 * L), lambda j, i: (0, 0)),
        ],
        out_specs=tok,
        compiler_params=_params("arbitrary", "arbitrary"),
        name=f"hyena_conv_{L}",
    )(vx, x0, coef, bias, _bf(jnp.asarray(fwd)), _bf(jnp.asarray(inv)))


def _post_kernel(x_ref, a_ref, mod_ref, wo_ref, ln1g_ref, ln1b_ref, win_ref, wout_ref, ln2g_ref, ln2b_ref,
                 o_ref, act_ref, *, cf):
    d = D_MODEL
    g1 = mod_ref[:, 2 * d:3 * d]
    sh2 = mod_ref[:, 3 * d:4 * d]
    sc2 = mod_ref[:, 4 * d:5 * d]
    g2 = mod_ref[:, 5 * d:6 * d]
    mix = _dot(a_ref[...], wo_ref[...])
    x1 = _layer_norm(ALPHA * x_ref[...] + g1 * mix, ln1g_ref[...], ln1b_ref[...])
    h = _bf(x1 * (1.0 + sc2) + sh2)
    for c in range(D_FF // cf):
        gate = _dot(h, win_ref[:, c * cf:(c + 1) * cf])
        up = _dot(h, win_ref[:, D_FF + c * cf:D_FF + (c + 1) * cf])
        act_ref[:, c * cf:(c + 1) * cf] = _bf(gate * jax.nn.sigmoid(gate) * up)
    f = _dot(act_ref[...], wout_ref[...])
    o_ref[...] = _layer_norm(ALPHA * x1 + g2 * f, ln2g_ref[...], ln2b_ref[...])


def _post(x2d, a2d, mods, layer, row_fn, w_o, mixer, ln1_g, ln1_b, w_in, w_out, ln2_g, ln2_b, tm, name):
    n = x2d.shape[0]
    tok = pl.BlockSpec((tm, D_MODEL), lambda i: (i, 0))
    const = lambda shape, idx: pl.BlockSpec((None,) + shape, lambda i: (idx, 0, 0),
                                            pipeline_mode=pl.Buffered(1))
    vec = pl.BlockSpec((None, 1, D_MODEL), lambda i: (layer, 0, 0))
    return pl.pallas_call(
        functools.partial(_post_kernel, cf=256),
        out_shape=jax.ShapeDtypeStruct((n, D_MODEL), F32),
        grid=(n // tm,),
        in_specs=[
            tok, tok,
            pl.BlockSpec((None, None, 1, 6 * D_MODEL), lambda i: (layer, row_fn(i), 0, 0)),
            const((D_MODEL, D_MODEL), mixer), vec, vec,
            const((D_MODEL, 2 * D_FF), layer), const((D_FF, D_MODEL), layer), vec, vec,
        ],
        out_specs=tok,
        scratch_shapes=[pltpu.VMEM((tm, D_FF), BF16)],
        compiler_params=_params("arbitrary"),
        name=name,
    )(x2d, a2d, mods, w_o, ln1_g, ln1_b, w_in, w_out, ln2_g, ln2_b)


def _qkv_kernel(x_ref, mod_ref, w_ref, qkv_ref, *kv_refs):
    d = D_MODEL
    h = _bf(x_ref[...] * (1.0 + mod_ref[:, d:2 * d]) + mod_ref[:, 0:d])
    qkv_ref[:, 0:d] = _bf(_dot(h, w_ref[:, 0:d]) * HEAD_DIM ** -0.5)
    k = _dot(h, w_ref[:, d:2 * d])
    qkv_ref[:, d:2 * d] = _bf(k)
    v = _dot(h, w_ref[:, 2 * d:3 * d])
    qkv_ref[:, 2 * d:3 * d] = _bf(v)
    if kv_refs:
        nk_ref, nv_ref = kv_refs
        for hh in range(N_HEADS):
            nk_ref[0, 0, hh] = k[:, hh * HEAD_DIM:(hh + 1) * HEAD_DIM]
            nv_ref[0, 0, hh] = v[:, hh * HEAD_DIM:(hh + 1) * HEAD_DIM]


def _qkv(x2d, mods, layer, row_fn, w_qkv, tm, seq_kv):
    n = x2d.shape[0]
    out_shape = [jax.ShapeDtypeStruct((n, 3 * D_MODEL), BF16)]
    out_specs = [pl.BlockSpec((tm, 3 * D_MODEL), lambda i: (i, 0))]
    if seq_kv:
        nb = n // tm
        kv_shape = jax.ShapeDtypeStruct((nb, 1, N_HEADS, tm, HEAD_DIM), F32)
        kv_spec = pl.BlockSpec((1, 1, N_HEADS, tm, HEAD_DIM), lambda i: (i, 0, 0, 0, 0))
        out_shape += [kv_shape, kv_shape]
        out_specs += [kv_spec, kv_spec]
    return pl.pallas_call(
        _qkv_kernel,
        out_shape=tuple(out_shape),
        grid=(n // tm,),
        in_specs=[
            pl.BlockSpec((tm, D_MODEL), lambda i: (i, 0)),
            pl.BlockSpec((None, None, 1, 6 * D_MODEL), lambda i: (layer, row_fn(i), 0, 0)),
            pl.BlockSpec((D_MODEL, 3 * D_MODEL), lambda i: (0, 0)),
        ],
        out_specs=tuple(out_specs),
        compiler_params=_params("arbitrary"),
        name="qkv_ctx" if seq_kv else "qkv_lat",
    )(x2d, mods, w_qkv)


def _ctx_attn_kernel(q_ref, k_ref, v_ref, o_ref):
    for hp in range(N_HEADS // 2):
        outs = []
        for hh in (2 * hp, 2 * hp + 1):
            sl = slice(hh * HEAD_DIM, (hh + 1) * HEAD_DIM)
            s = _dot_nt(q_ref[:, sl], k_ref[:, sl])
            m = jnp.max(s, axis=-1, keepdims=True)
            p = jnp.exp(s - m)
            l = jnp.sum(p, axis=-1, keepdims=True)
            outs.append(_dot(_bf(p), v_ref[:, sl]) * (1.0 / l))
        o_ref[:, 2 * hp * HEAD_DIM:(2 * hp + 2) * HEAD_DIM] = _bf(jnp.concatenate(outs, axis=1))


def _ctx_attn(qkv, seq):
    n = qkv.shape[0]
    return pl.pallas_call(
        _ctx_attn_kernel,
        out_shape=jax.ShapeDtypeStruct((n, D_MODEL), BF16),
        grid=(n // seq,),
        in_specs=[pl.BlockSpec((seq, D_MODEL), lambda b, c=c: (b, c)) for c in range(3)],
        out_specs=pl.BlockSpec((seq, D_MODEL), lambda b: (b, 0)),
        compiler_params=_params("arbitrary"),
        name="ctx_attn",
    )(qkv, qkv, qkv)


LAT_ROWS = 16
HALF_ROWS = LAT_ROWS // 2
LOCAL_ROWS = 12
HALF_KEY_ROW0 = (0, LAT_ROWS - LOCAL_ROWS)


def _bias_kernel(rpb_ref, tab_ref):
    shape = (GRID_W, LANES)
    qc = lax.broadcasted_iota(jnp.int32, shape, 0)
    lane = lax.broadcasted_iota(jnp.int32, shape, 1)
    kc = lane % GRID_W
    win0 = jnp.clip(qc - WIN_W // 2, 0, GRID_W - WIN_W)
    col_ok = (kc >= win0) & (kc < win0 + WIN_W)
    masked = jnp.full(shape, MASK_VALUE, F32)
    tiles = []
    for dr in range(2 * WIN_H - 1):
        row = jnp.broadcast_to(rpb_ref[dr:dr + 1, :], shape)
        t = pltpu.roll(row, LANES - (WIN_W - 1), 1, stride=1, stride_axis=0)
        t = jnp.where(lane < GRID_W, t, pltpu.roll(t, GRID_W, 1))
        tiles.append(jnp.where(col_ok, t, MASK_VALUE))

    def tile(qr, kr):
        row0 = min(max(qr - WIN_H // 2, 0), LAT_ROWS - WIN_H)
        if row0 <= kr < row0 + WIN_H:
            return tiles[kr - qr + WIN_H - 1]
        return masked

    for qr in range(LAT_ROWS):
        kr0 = HALF_KEY_ROW0[qr // HALF_ROWS]
        for i in range(0, LOCAL_ROWS, 2):
            both = jnp.where(lane < GRID_W, tile(qr, kr0 + i), tile(qr, kr0 + i + 1))
            tab_ref[qr * GRID_W:(qr + 1) * GRID_W, i * GRID_W:(i + 2) * GRID_W] = both


def _bias_table(rpb):
    rpb = jnp.pad(rpb, ((0, 0), (0, 1), (0, LANES - (2 * WIN_W - 1))))
    nq = LAT_ROWS * GRID_W
    nk = LOCAL_ROWS * GRID_W
    return pl.pallas_call(
        _bias_kernel,
        out_shape=jax.ShapeDtypeStruct((N_HEADS, nq, nk), F32),
        grid=(N_HEADS,),
        in_specs=[pl.BlockSpec((None, 2 * WIN_H, LANES), lambda h: (h, 0, 0))],
        out_specs=pl.BlockSpec((None, nq, nk), lambda h: (h, 0, 0)),
        compiler_params=_params("arbitrary"),
        name="na_bias_table",
    )(rpb)


def _lat_attn_kernel(q_ref, k_ref, v_ref, ck_ref, cv_ref, tab_ref, o_ref):
    nq = HALF_ROWS * GRID_W
    nk = LOCAL_ROWS * GRID_W
    outs = []
    for hh in range(2):
        sl = slice(hh * HEAD_DIM, (hh + 1) * HEAD_DIM)
        k_ctx = _bf(ck_ref[0, 0, hh])
        v_ctx = _bf(cv_ref[0, 0, hh])
        halves = []
        for half in range(2):
            key0 = HALF_KEY_ROW0[half] * GRID_W
            q = q_ref[half * nq:(half + 1) * nq, sl]
            s_loc = _dot_nt(q, k_ref[key0:key0 + nk, sl]) + tab_ref[hh, half * nq:(half + 1) * nq, :]
            s_ctx = _dot_nt(q, k_ctx)
            m = jnp.maximum(jnp.max(s_loc, axis=-1, keepdims=True), jnp.max(s_ctx, axis=-1, keepdims=True))
            p_loc = jnp.exp(s_loc - m)
            p_ctx = jnp.exp(s_ctx - m)
            l = jnp.sum(p_loc, axis=-1, keepdims=True) + jnp.sum(p_ctx, axis=-1, keepdims=True)
            o = _dot(_bf(p_loc), v_ref[key0:key0 + nk, sl]) + _dot(_bf(p_ctx), v_ctx)
            halves.append(o * (1.0 / l))
        outs.append(jnp.concatenate(halves, axis=0))
    o_ref[...] = _bf(jnp.concatenate(outs, axis=1))


def _lat_attn(qkv, cache_k, cache_v, table, layer_idx):
    n = qkv.shape[0]
    seq = LAT_ROWS * GRID_W
    nb = n // seq
    nhp = N_HEADS // 2
    past = cache_k.shape[3]
    tok = lambda c: pl.BlockSpec((seq, 2 * HEAD_DIM), lambda hp, b, c=c: (b, c * nhp + hp))
    cache = pl.BlockSpec((1, 1, 2, past, HEAD_DIM), lambda hp, b: (b, layer_idx, hp, 0, 0))
    return pl.pallas_call(
        _lat_attn_kernel,
        out_shape=jax.ShapeDtypeStruct((n, D_MODEL), BF16),
        grid=(nhp, nb),
        in_specs=[tok(0), tok(1), tok(2), cache, cache,
                  pl.BlockSpec((2, seq, LOCAL_ROWS * GRID_W), lambda hp, b: (hp, 0, 0))],
        out_specs=pl.BlockSpec((seq, 2 * HEAD_DIM), lambda hp, b: (b, hp)),
        compiler_params=_params("arbitrary", "arbitrary"),
        name="lat_attn",
    )(qkv, qkv, qkv, cache_k, cache_v, table)


def kernel(x_prompt, x_sample, cache_k, cache_v, c, c_ctx, ada_w, ada_b, ln1_g, ln1_b, ln2_g, ln2_b, ffn_w_in, ffn_w_out, hy_w_in, hy_sconv_w, hy_sconv_b, hy_pe_w1, hy_pe_b1, hy_pe_w2, hy_pe_b2, hy_pe_w3, hy_sin_freq, hy_decay, hy_bias, hy_w_out, na_w_qkv, na_rpb, na_w_out):
    nbp, lp, d = x_prompt.shape
    nbs, ls, _ = x_sample.shape
    assert d == D_MODEL and ls == LAT_ROWS * GRID_W and c.shape[0] == nbs
    xp = x_prompt.reshape(nbp * lp, d)
    xs = x_sample.reshape(nbs * ls, d)

    cvec = jnp.concatenate([c_ctx[None, :], c, jnp.zeros((8 - 1 - nbs, d), F32)], axis=0)
    mods = _ada(cvec, ada_w, ada_b).reshape(DEPTH, 8, 1, 6 * d)
    row_p = lambda i: 0
    tm = 512
    row_s = lambda i: 1 + (i * tm) // ls

    filt = (hy_pe_w1[0], hy_pe_b1[0], hy_pe_w2[0], hy_pe_b2[0], hy_pe_w3[0], hy_sin_freq[0], hy_decay[0])
    coef_p = _hyena_filter_spectrum(lp, *filt)
    coef_s = _hyena_filter_spectrum(ls, *filt)
    w_in = _bf(hy_w_in[0])
    sconv_b = hy_sconv_b[0].reshape(1, -1)
    hy_bias0 = hy_bias[0].reshape(1, -1)
    x0_p, vx_p = _hy_in(xp, mods, 0, row_p, w_in, hy_sconv_w[0], sconv_b, lp, tm)
    x0_s, vx_s = _hy_in(xs, mods, 0, lambda i: 1 + i, w_in, hy_sconv_w[0], sconv_b, ls, ls)
    u_p = _hy_conv(vx_p, x0_p, coef_p, hy_bias0, lp)
    u_s = _hy_conv(vx_s, x0_s, coef_s, hy_bias0, ls)
    ln = [a.reshape(DEPTH, 1, d) for a in (ln1_g, ln1_b, ln2_g, ln2_b)]
    ffn = (ln[0], ln[1], _bf(ffn_w_in), _bf(ffn_w_out), ln[2], ln[3])
    hy_w_out_bf = _bf(hy_w_out)
    xp = _post(xp, u_p, mods, 0, row_p, hy_w_out_bf, 0, *ffn, tm=tm, name="post0_ctx")
    xs = _post(xs, u_s, mods, 0, row_s, hy_w_out_bf, 0, *ffn, tm=tm, name="post0_lat")

    w_qkv = _bf(na_w_qkv[0])
    qkv_p, new_k, new_v = _qkv(xp, mods, 1, row_p, w_qkv, lp, True)
    (qkv_s,) = _qkv(xs, mods, 1, row_s, w_qkv, tm, False)
    a_p = _ctx_attn(qkv_p, lp)
    a_s = _lat_attn(qkv_s, cache_k, cache_v, _bias_table(na_rpb[0]), 0)
    na_w_out_bf = _bf(na_w_out)
    xp = _post(xp, a_p, mods, 1, row_p, na_w_out_bf, 0, *ffn, tm=tm, name="post1_ctx")
    xs = _post(xs, a_s, mods, 1, row_s, na_w_out_bf, 0, *ffn, tm=tm, name="post1_lat")

    return (xp.reshape(nbp, lp, d), xs.reshape(nbs, ls, d), new_k, new_v)
```

```python
import functools
import math

import jax
import jax.numpy as jnp
import numpy as np
from jax import lax
from jax.experimental import pallas as pl
from jax.experimental.pallas import tpu as pltpu

D_MODEL = 1024
N_HEADS = 16
HEAD_DIM = D_MODEL // N_HEADS
D_FF = 2816
GRID_W = 64
WIN_H = 8
WIN_W = 16
N_BANDS = 16
PE_DIM = 1 + 2 * N_BANDS
FILT_W = 64
MOD_SHIFT = 0.05
DEPTH = 2
ALPHA = (2 * DEPTH) ** 0.25
LN_EPS = 1e-5
MASK_VALUE = -1e30

LANES = 128
SUBLANES = 8
BF16_ROWS = 2 * SUBLANES
ATTN_LOOKAHEAD = 2
LOG2E = math.log2(math.e)
VMEM_LIMIT = 56 * 1024 * 1024

F32 = jnp.float32
BF16 = jnp.bfloat16


def _bf(x):
    return x.astype(BF16)


def _dot(a, b):
    return jnp.dot(a, b, preferred_element_type=F32)


def _dot_nt(a, b):
    return lax.dot_general(a, b, (((1,), (1,)), ((), ())), preferred_element_type=F32)


def _dot_tn(a, b):
    return lax.dot_general(a, b, (((0,), (0,)), ((), ())), preferred_element_type=F32)


def _layer_norm(x, g, b):
    mu = jnp.mean(x, axis=-1, keepdims=True)
    xc = x - mu
    var = jnp.mean(xc * xc, axis=-1, keepdims=True)
    return xc * lax.rsqrt(var + LN_EPS) * g + b


def _params(*sem):
    return pltpu.CompilerParams(dimension_semantics=sem, vmem_limit_bytes=VMEM_LIMIT)


@functools.lru_cache(maxsize=None)
def _dft_tables(L):
    n = 2 * L
    f = np.arange(L)[:, None]
    s = np.arange(L)[None, :]
    ang = 2.0 * np.pi * ((f * s) % n) / n
    c = np.cos(ang)
    sn = np.sin(ang)
    sn[0, :] = np.where(np.arange(L) % 2 == 0, 1.0, -1.0)
    fwd = np.concatenate([c, sn], axis=0)
    w = np.full((2 * L, 1), 2.0 / n)
    w[0, 0] = 1.0 / n
    w[L, 0] = 1.0 / n
    inv = (fwd * w).T
    alt = np.where(np.arange(L) % 2 == 0, 1.0, -1.0).astype(np.float32)[:, None]
    return (np.ascontiguousarray(fwd).astype(np.float32), np.ascontiguousarray(inv).astype(np.float32), alt)


@functools.lru_cache(maxsize=None)
def _filter_features(L):
    t = np.linspace(0.0, 1.0, L, dtype=np.float32)[:, None]
    w = (np.float32(2.0 * math.pi / L) * np.arange(L, dtype=np.float32))[:, None]
    bands = np.linspace(1e-4, N_BANDS - 1, N_BANDS, dtype=np.float32)[None, :]
    arg = (bands * w).astype(np.float64)
    feats = np.concatenate([t.astype(np.float64), np.cos(arg), np.sin(arg)], axis=-1)
    out = np.zeros((L, LANES), np.float32)
    out[:, :PE_DIM] = feats
    return out, t


def _ada_kernel(c_ref, w_ref, b_ref, o_ref):
    c = c_ref[...]
    s = c * jax.nn.sigmoid(c)
    s_hi = _bf(s)
    s_lo = _bf(s - s_hi.astype(F32))
    w = w_ref[...]
    w_hi = _bf(w)
    w_lo = _bf(w - w_hi.astype(F32))
    o_ref[...] = _dot(s_hi, w_hi) + _dot(s_lo, w_hi) + _dot(s_hi, w_lo) + b_ref[...]


def _ada(cvec, ada_w, ada_b):
    tn = 1536
    nt = 6 * D_MODEL // tn
    return pl.pallas_call(
        _ada_kernel,
        out_shape=jax.ShapeDtypeStruct((DEPTH, 8, 6 * D_MODEL), F32),
        grid=(DEPTH, nt),
        in_specs=[
            pl.BlockSpec((8, D_MODEL), lambda l, j: (0, 0)),
            pl.BlockSpec((None, D_MODEL, tn), lambda l, j: (l, 0, j)),
            pl.BlockSpec((None, 1, tn), lambda l, j: (l, 0, j)),
        ],
        out_specs=pl.BlockSpec((None, 8, tn), lambda l, j: (l, 0, j)),
        compiler_params=_params("arbitrary", "arbitrary"),
        name="ada_mod",
    )(cvec, ada_w, ada_b.reshape(DEPTH, 1, 6 * D_MODEL))


def _filter_kernel(feats_ref, t_ref, alt_ref, w1_ref, b1_ref, w2_ref, b2_ref, w3f_ref, w3b_ref, freq_ref,
                   decf_ref, decb_ref, fwd_ref, out_ref, h_ref, *, L):
    hi = lax.Precision.HIGHEST

    @pl.when(pl.program_id(0) == 0)
    def _():
        h1 = jnp.sin(freq_ref[0:1, :] * (jnp.dot(feats_ref[...], w1_ref[...], precision=hi,
                                                 preferred_element_type=F32) + b1_ref[...]))
        h_ref[...] = jnp.sin(freq_ref[1:2, :] * (jnp.dot(h1, w2_ref[...], precision=hi,
                                                         preferred_element_type=F32) + b2_ref[...]))

    h = h_ref[...]
    t = t_ref[...]
    kf = jnp.dot(h, w3f_ref[...], precision=hi, preferred_element_type=F32) * (
        jnp.exp(-t * jnp.abs(decf_ref[...])) + MOD_SHIFT)
    kb = jnp.dot(h, w3b_ref[...], precision=hi, preferred_element_type=F32) * (
        jnp.exp(-t * jnp.abs(decb_ref[...])) + MOD_SHIFT)
    row = lax.broadcasted_iota(jnp.int32, kf.shape, 0)
    kb = jnp.where(row == 0, 0.0, kb)
    ksum = kf + kb
    kdiff = kf - kb
    ksum_hi = _bf(ksum)
    ksum_lo = _bf(ksum - ksum_hi.astype(F32))
    kdiff_hi = _bf(kdiff)
    kdiff_lo = _bf(kdiff - kdiff_hi.astype(F32))
    cos_m = fwd_ref[0:L, :]
    sin_m = fwd_ref[L:2 * L, :]
    k_cos = _dot(cos_m, ksum_hi) + _dot(cos_m, ksum_lo)
    k_sin = _dot(sin_m, kdiff_hi) + _dot(sin_m, kdiff_lo)
    k_nyq = jnp.sum(ksum * alt_ref[...], axis=0, keepdims=True)
    out_ref[0] = k_cos
    out_ref[1] = jnp.where(row == 0, 0.0, k_sin)
    out_ref[2] = jnp.where(row == 0, k_nyq, k_cos)


def _hyena_filter_spectrum(L, pe_w1, pe_b1, pe_w2, pe_b2, pe_w3, sin_freq, decay):
    feats, t = _filter_features(L)
    fwd, _, alt = _dft_tables(L)
    pad = LANES - FILT_W
    w1 = jnp.pad(pe_w1, ((0, LANES - PE_DIM), (0, pad)))
    b1 = jnp.pad(pe_b1, (0, pad)).reshape(1, LANES)
    w2 = jnp.pad(pe_w2, ((0, pad), (0, pad)))
    b2 = jnp.pad(pe_b2, (0, pad)).reshape(1, LANES)
    w3 = jnp.pad(pe_w3, ((0, pad), (0, 0)))
    freq = jnp.pad(sin_freq, ((0, 0), (0, pad)))
    dec = decay.reshape(1, 2 * D_MODEL)
    dt = 256
    nt = D_MODEL // dt
    full = lambda shape: pl.BlockSpec(shape, lambda j: (0,) * len(shape))
    return pl.pallas_call(
        functools.partial(_filter_kernel, L=L),
        out_shape=jax.ShapeDtypeStruct((3, L, D_MODEL), F32),
        grid=(nt,),
        in_specs=[
            full((L, LANES)), full((L, 1)), full((L, 1)),
            full((LANES, LANES)), full((1, LANES)), full((LANES, LANES)), full((1, LANES)),
            pl.BlockSpec((LANES, dt), lambda j: (0, j)),
            pl.BlockSpec((LANES, dt), lambda j: (0, nt + j)),
            full((2, LANES)),
            pl.BlockSpec((1, dt), lambda j: (0, j)),
            pl.BlockSpec((1, dt), lambda j: (0, nt + j)),
            full((2 * L, L)),
        ],
        out_specs=pl.BlockSpec((3, L, dt), lambda j: (0, 0, j)),
        scratch_shapes=[pltpu.VMEM((L, LANES), F32)],
        compiler_params=_params("arbitrary"),
        name=f"hyena_filter_{L}",
    )(feats, t, alt, w1, b1, w2, b2, w3, w3, freq, dec, dec, _bf(jnp.asarray(fwd)))


def _hy_in_kernel(x_ref, mod_ref, w_ref, cw_ref, cb_ref, x0_ref, vx_ref, *, L, cn):
    tm = x_ref.shape[0]
    x = x_ref[...]
    h = _bf(x * (1.0 + mod_ref[:, D_MODEL:2 * D_MODEL]) + mod_ref[:, 0:D_MODEL])
    row = lax.broadcasted_iota(jnp.int32, (SUBLANES, cn), 0)

    def conv(col):
        z = _dot(h, w_ref[:, col:col + cn])
        zm = pltpu.roll(z, 1, 0)
        zp = pltpu.roll(z, tm - 1, 0)
        w0 = cw_ref[0:1, col:col + cn]
        w1 = cw_ref[1:2, col:col + cn]
        w2 = cw_ref[2:3, col:col + cn]
        b = cb_ref[:, col:col + cn]
        out = zm * w0 + z * w1 + zp * w2 + b
        pieces = []
        for s in range(tm // L):
            lo = s * L
            hi = lo + L - SUBLANES
            head = (jnp.where(row == 0, 0.0, zm[lo:lo + SUBLANES]) * w0 + z[lo:lo + SUBLANES] * w1
                    + zp[lo:lo + SUBLANES] * w2 + b)
            tail = (zm[hi:hi + SUBLANES] * w0 + z[hi:hi + SUBLANES] * w1
                    + jnp.where(row == SUBLANES - 1, 0.0, zp[hi:hi + SUBLANES]) * w2 + b)
            pieces += [head, out[lo + SUBLANES:hi], tail]
        return jnp.concatenate(pieces, axis=0)

    for c in range(D_MODEL // cn):
        x0_ref[:, c * cn:(c + 1) * cn] = _bf(conv(c * cn))
        x1 = conv(D_MODEL + c * cn)
        v = conv(2 * D_MODEL + c * cn)
        vx_ref[:, c * cn:(c + 1) * cn] = _bf(v * x1)


def _hy_in(x2d, mods, layer, row_fn, w_in, sconv_w, sconv_b, L, tm):
    n = x2d.shape[0]
    cn = 256
    tok = pl.BlockSpec((tm, D_MODEL), lambda i: (i, 0))
    return pl.pallas_call(
        functools.partial(_hy_in_kernel, L=L, cn=cn),
        out_shape=(jax.ShapeDtypeStruct((n, D_MODEL), BF16), jax.ShapeDtypeStruct((n, D_MODEL), BF16)),
        grid=(n // tm,),
        in_specs=[
            tok,
            pl.BlockSpec((None, None, 1, 6 * D_MODEL), lambda i: (layer, row_fn(i), 0, 0)),
            pl.BlockSpec((D_MODEL, 3 * D_MODEL), lambda i: (0, 0)),
            pl.BlockSpec((3, 3 * D_MODEL), lambda i: (0, 0)),
            pl.BlockSpec((1, 3 * D_MODEL), lambda i: (0, 0)),
        ],
        out_specs=(tok, tok),
        compiler_params=_params("arbitrary"),
        name=f"hyena_in_{L}",
    )(x2d, mods, w_in, sconv_w, sconv_b)


def _hy_conv_kernel(vx_ref, x0_ref, coef_ref, bias_ref, fwd_ref, inv_ref, u_ref, *, L, nseq, cn, mt):
    for c in range(vx_ref.shape[1] // cn):
        cols = slice(c * cn, (c + 1) * cn)
        k_cos = coef_ref[0, :, cols]
        k_sin = coef_ref[1, :, cols]
        k_mix = coef_ref[2, :, cols]
        for s in range(nseq):
            vx = vx_ref[s * L:(s + 1) * L, cols]
            spec = _dot(fwd_ref[...], vx)
            v_cos = spec[0:L]
            v_sin = spec[L:2 * L]
            y_spec = _bf(jnp.concatenate([v_cos * k_cos - v_sin * k_sin, v_cos * k_sin + v_sin * k_mix],
                                         axis=0))
            for m in range(L // mt):
                rows = slice(s * L + m * mt, s * L + (m + 1) * mt)
                y = _dot(inv_ref[m * mt:(m + 1) * mt, :], y_spec)
                u = (y + bias_ref[:, cols] * vx_ref[rows, cols].astype(F32)) * x0_ref[rows, cols].astype(F32)
                u_ref[rows, cols] = _bf(u)


def _hy_conv(vx, x0, coef, bias, L):
    n = vx.shape[0]
    tm = 1024
    dt = 512
    fwd, inv, _ = _dft_tables(L)
    tok = pl.BlockSpec((tm, dt), lambda j, i: (i, j))
    return pl.pallas_call(
        functools.partial(_hy_conv_kernel, L=L, nseq=tm // L, cn=256 if L <= 256 else dt, mt=L),
        out_shape=jax.ShapeDtypeStruct((n, D_MODEL), BF16),
        grid=(D_MODEL // dt, n // tm),
        in_specs=[
            tok, tok,
            pl.BlockSpec((3, L, dt), lambda j, i: (0, 0, j)),
            pl.BlockSpec((1, dt), lambda j, i: (0, j)),
            pl.BlockSpec((2 * L, L), lambda j, i: (0, 0)),
            pl.BlockSpec((L, 2 * L), lambda j, i: (0, 0)),
        ],
        out_specs=tok,
        compiler_params=_params("arbitrary", "arbitrary"),
        name=f"hyena_conv_{L}",
    )(vx, x0, coef, bias, _bf(jnp.asarray(fwd)), _bf(jnp.asarray(inv)))


def _post_kernel(x_ref, a_ref, mod_ref, wo_ref, ln1g_ref, ln1b_ref, win_ref, wout_ref, ln2g_ref, ln2b_ref,
                 o_ref, act_ref, *, cf, a_feature_major):
    d = D_MODEL
    g1 = mod_ref[:, 2 * d:3 * d]
    sh2 = mod_ref[:, 3 * d:4 * d]
    sc2 = mod_ref[:, 4 * d:5 * d]
    g2 = mod_ref[:, 5 * d:6 * d]
    mix = (_dot_tn if a_feature_major else _dot)(a_ref[...], wo_ref[...])
    x1 = _layer_norm(ALPHA * x_ref[...] + g1 * mix, ln1g_ref[...], ln1b_ref[...])
    h = _bf(x1 * (1.0 + sc2) + sh2)
    for c in range(D_FF // cf):
        gate = _dot(h, win_ref[:, c * cf:(c + 1) * cf])
        up = _dot(h, win_ref[:, D_FF + c * cf:D_FF + (c + 1) * cf])
        act_ref[:, c * cf:(c + 1) * cf] = _bf(gate * jax.nn.sigmoid(gate) * up)
    f = _dot(act_ref[...], wout_ref[...])
    o_ref[...] = _layer_norm(ALPHA * x1 + g2 * f, ln2g_ref[...], ln2b_ref[...])


def _post(x2d, a2d, mods, layer, row_fn, w_o, mixer, ln1_g, ln1_b, w_in, w_out, ln2_g, ln2_b, tm, name,
          a_feature_major=False):
    n = x2d.shape[0]
    tok = pl.BlockSpec((tm, D_MODEL), lambda i: (i, 0))
    a_spec = pl.BlockSpec((D_MODEL, tm), lambda i: (0, i)) if a_feature_major else tok
    const = lambda shape, idx: pl.BlockSpec((None,) + shape, lambda i: (idx, 0, 0),
                                            pipeline_mode=pl.Buffered(1))
    vec = pl.BlockSpec((None, 1, D_MODEL), lambda i: (layer, 0, 0))
    return pl.pallas_call(
        functools.partial(_post_kernel, cf=256, a_feature_major=a_feature_major),
        out_shape=jax.ShapeDtypeStruct((n, D_MODEL), F32),
        grid=(n // tm,),
        in_specs=[
            tok, a_spec,
            pl.BlockSpec((None, None, 1, 6 * D_MODEL), lambda i: (layer, row_fn(i), 0, 0)),
            const((D_MODEL, D_MODEL), mixer), vec, vec,
            const((D_MODEL, 2 * D_FF), layer), const((D_FF, D_MODEL), layer), vec, vec,
        ],
        out_specs=tok,
        scratch_shapes=[pltpu.VMEM((tm, D_FF), BF16)],
        compiler_params=_params("arbitrary"),
        name=name,
    )(x2d, a2d, mods, w_o, ln1_g, ln1_b, w_in, w_out, ln2_g, ln2_b)


def _qkv_kernel(x_ref, mod_ref, wt_ref, qkvt_ref, *kv_refs):
    d = D_MODEL
    tm = x_ref.shape[0]
    h = _bf(x_ref[...] * (1.0 + mod_ref[:, d:2 * d]) + mod_ref[:, 0:d])
    qkvt_ref[0:d, :] = _bf(_dot_nt(wt_ref[0:d, :], h) * (HEAD_DIM ** -0.5 * LOG2E))
    kt = _dot_nt(wt_ref[d:2 * d, :], h)
    qkvt_ref[d:2 * d, :] = _bf(kt)
    vt = _dot_nt(wt_ref[2 * d:3 * d, :], h)
    qkvt_ref[2 * d:3 * d, :] = _bf(vt)
    if kv_refs:
        nk_ref, nv_ref = kv_refs
        nk_ref[0, 0] = kt.reshape(N_HEADS, HEAD_DIM, tm)
        nv_ref[0, 0] = vt.reshape(N_HEADS, HEAD_DIM, tm)


def _qkv(x2d, mods, layer, row_fn, w_qkv_t, tm, seq_kv):
    n = x2d.shape[0]
    out_shape = [jax.ShapeDtypeStruct((3 * D_MODEL, n), BF16)]
    out_specs = [pl.BlockSpec((3 * D_MODEL, tm), lambda i: (0, i))]
    if seq_kv:
        nb = n // tm
        kv_shape = jax.ShapeDtypeStruct((nb, 1, N_HEADS, HEAD_DIM, tm), F32)
        kv_spec = pl.BlockSpec((1, 1, N_HEADS, HEAD_DIM, tm), lambda i: (i, 0, 0, 0, 0))
        out_shape += [kv_shape, kv_shape]
        out_specs += [kv_spec, kv_spec]
    return pl.pallas_call(
        _qkv_kernel,
        out_shape=tuple(out_shape),
        grid=(n // tm,),
        in_specs=[
            pl.BlockSpec((tm, D_MODEL), lambda i: (i, 0)),
            pl.BlockSpec((None, None, 1, 6 * D_MODEL), lambda i: (layer, row_fn(i), 0, 0)),
            pl.BlockSpec((3 * D_MODEL, D_MODEL), lambda i: (0, 0)),
        ],
        out_specs=tuple(out_specs),
        compiler_params=_params("arbitrary"),
        name="qkv_ctx" if seq_kv else "qkv_lat",
    )(x2d, mods, w_qkv_t)


def _softmax_weights(*scores):
    m = functools.reduce(jnp.maximum, [jnp.max(s, axis=0, keepdims=True) for s in scores])
    return [_bf(jnp.exp2(s - m)) for s in scores]


def _weighted_values(vt, p):
    ones = jnp.ones((BF16_ROWS, vt.shape[1]), BF16)
    return _dot(jnp.concatenate([vt, ones], axis=0), p)


def _normalise(acc):
    return _bf(acc[0:HEAD_DIM] * (1.0 / acc[HEAD_DIM:HEAD_DIM + 1]))


def _ctx_attn_kernel(qt_ref, kt_ref, vt_ref, ot_ref):
    def rows(hh):
        return slice(hh * HEAD_DIM, (hh + 1) * HEAD_DIM)

    def scores(hh):
        return _dot_tn(kt_ref[rows(hh), :], qt_ref[rows(hh), :])

    pending = [scores(hh) for hh in range(ATTN_LOOKAHEAD)]
    for hh in range(N_HEADS):
        if hh + ATTN_LOOKAHEAD < N_HEADS:
            pending.append(scores(hh + ATTN_LOOKAHEAD))
        (p,) = _softmax_weights(pending.pop(0))
        ot_ref[rows(hh), :] = _normalise(_weighted_values(vt_ref[rows(hh), :], p))


def _ctx_attn(qkvt, seq):
    n = qkvt.shape[1]
    return pl.pallas_call(
        _ctx_attn_kernel,
        out_shape=jax.ShapeDtypeStruct((D_MODEL, n), BF16),
        grid=(n // seq,),
        in_specs=[pl.BlockSpec((D_MODEL, seq), lambda b, c=c: (c, b)) for c in range(3)],
        out_specs=pl.BlockSpec((D_MODEL, seq), lambda b: (0, b)),
        compiler_params=_params("arbitrary"),
        name="ctx_attn",
    )(qkvt, qkvt, qkvt)


LAT_ROWS = 16
HALF_ROWS = LAT_ROWS // 2
LOCAL_ROWS = 12
HALF_KEY_ROW0 = (0, LAT_ROWS - LOCAL_ROWS)


def _bias_kernel(rpb_ref, tab_ref):
    shape = (GRID_W, LANES)
    kc = lax.broadcasted_iota(jnp.int32, shape, 0)
    lane = lax.broadcasted_iota(jnp.int32, shape, 1)
    qc = lane % GRID_W
    win0 = jnp.clip(qc - WIN_W // 2, 0, GRID_W - WIN_W)
    col_ok = (kc >= win0) & (kc < win0 + WIN_W)
    masked = jnp.full(shape, MASK_VALUE, F32)
    tiles = []
    for dr in range(2 * WIN_H - 1):
        row = jnp.broadcast_to(rpb_ref[dr:dr + 1, :] * LOG2E, shape)
        t = pltpu.roll(row, LANES - (WIN_W - 1), 1, stride=1, stride_axis=0)
        t = jnp.where(lane < GRID_W, t, pltpu.roll(t, GRID_W, 1))
        tiles.append(jnp.where(col_ok, t, MASK_VALUE))

    def tile(qr, kr):
        row0 = min(max(qr - WIN_H // 2, 0), LAT_ROWS - WIN_H)
        if row0 <= kr < row0 + WIN_H:
            return tiles[kr - qr + WIN_H - 1]
        return masked

    for qr in range(0, LAT_ROWS, 2):
        kr0 = HALF_KEY_ROW0[qr // HALF_ROWS]
        for i in range(LOCAL_ROWS):
            both = jnp.where(lane < GRID_W, tile(qr, kr0 + i), tile(qr + 1, kr0 + i))
            tab_ref[i * GRID_W:(i + 1) * GRID_W, qr * GRID_W:(qr + 2) * GRID_W] = both


def _bias_table(rpb):
    rpb = jnp.pad(rpb[:, :, ::-1], ((0, 0), (0, 1), (0, LANES - (2 * WIN_W - 1))))
    nq = LAT_ROWS * GRID_W
    nk = LOCAL_ROWS * GRID_W
    return pl.pallas_call(
        _bias_kernel,
        out_shape=jax.ShapeDtypeStruct((N_HEADS, nk, nq), F32),
        grid=(N_HEADS,),
        in_specs=[pl.BlockSpec((None, 2 * WIN_H, LANES), lambda h: (h, 0, 0))],
        out_specs=pl.BlockSpec((None, nk, nq), lambda h: (h, 0, 0)),
        compiler_params=_params("arbitrary"),
        name="na_bias_table",
    )(rpb)


def _lat_attn_kernel(qt_ref, kt_ref, vt_ref, ckt_ref, cvt_ref, tab_ref, ot_ref):
    seq = LAT_ROWS * GRID_W
    nq = HALF_ROWS * GRID_W
    nk = LOCAL_ROWS * GRID_W
    heads = qt_ref.shape[0] // HEAD_DIM
    units = [(b, hh, half) for b in range(qt_ref.shape[1] // seq) for hh in range(heads) for half in range(2)]

    def rows(hh):
        return slice(hh * HEAD_DIM, (hh + 1) * HEAD_DIM)

    def scores(unit):
        b, hh, half = unit
        key0 = b * seq + HALF_KEY_ROW0[half] * GRID_W
        qt = qt_ref[rows(hh), b * seq + half * nq:b * seq + (half + 1) * nq]
        s_loc = _dot_tn(kt_ref[rows(hh), key0:key0 + nk], qt) + tab_ref[hh, :, half * nq:(half + 1) * nq]
        s_ctx = _dot_tn(_bf(ckt_ref[b, hh]), qt)
        return s_loc, s_ctx

    pending = [scores(u) for u in units[:ATTN_LOOKAHEAD]]
    for i, (b, hh, half) in enumerate(units):
        if i + ATTN_LOOKAHEAD < len(units):
            pending.append(scores(units[i + ATTN_LOOKAHEAD]))
        s_loc, s_ctx = pending.pop(0)
        key0 = b * seq + HALF_KEY_ROW0[half] * GRID_W
        p_loc, p_ctx = _softmax_weights(s_loc, s_ctx)
        acc = (_weighted_values(vt_ref[rows(hh), key0:key0 + nk], p_loc)
               + _weighted_values(_bf(cvt_ref[b, hh]), p_ctx))
        ot_ref[rows(hh), b * seq + half * nq:b * seq + (half + 1) * nq] = _normalise(acc)


def _lat_attn(qkvt, cache_kt, cache_vt, table, layer_idx):
    n = qkvt.shape[1]
    seq = LAT_ROWS * GRID_W
    nb = n // seq
    nhp = N_HEADS // 2
    past = cache_kt.shape[4]
    feat = lambda c: pl.BlockSpec((2 * HEAD_DIM, n), lambda hp, c=c: (c * nhp + hp, 0))
    cache = pl.BlockSpec((nb, None, 2, HEAD_DIM, past), lambda hp: (0, layer_idx, hp, 0, 0))
    return pl.pallas_call(
        _lat_attn_kernel,
        out_shape=jax.ShapeDtypeStruct((D_MODEL, n), BF16),
        grid=(nhp,),
        in_specs=[feat(0), feat(1), feat(2), cache, cache,
                  pl.BlockSpec((2, LOCAL_ROWS * GRID_W, seq), lambda hp: (hp, 0, 0))],
        out_specs=pl.BlockSpec((2 * HEAD_DIM, n), lambda hp: (hp, 0)),
        compiler_params=_params("arbitrary"),
        name="lat_attn",
    )(qkvt, qkvt, qkvt, cache_kt, cache_vt, table)


def kernel(x_prompt, x_sample, cache_k, cache_v, c, c_ctx, ada_w, ada_b, ln1_g, ln1_b, ln2_g, ln2_b, ffn_w_in, ffn_w_out, hy_w_in, hy_sconv_w, hy_sconv_b, hy_pe_w1, hy_pe_b1, hy_pe_w2, hy_pe_b2, hy_pe_w3, hy_sin_freq, hy_decay, hy_bias, hy_w_out, na_w_qkv, na_rpb, na_w_out):
    nbp, lp, d = x_prompt.shape
    nbs, ls, _ = x_sample.shape
    assert d == D_MODEL and ls == LAT_ROWS * GRID_W and c.shape[0] == nbs
    xp = x_prompt.reshape(nbp * lp, d)
    xs = x_sample.reshape(nbs * ls, d)

    cvec = jnp.concatenate([c_ctx[None, :], c, jnp.zeros((8 - 1 - nbs, d), F32)], axis=0)
    mods = _ada(cvec, ada_w, ada_b).reshape(DEPTH, 8, 1, 6 * d)
    row_p = lambda i: 0
    tm = 512
    row_s = lambda i: 1 + (i * tm) // ls

    filt = (hy_pe_w1[0], hy_pe_b1[0], hy_pe_w2[0], hy_pe_b2[0], hy_pe_w3[0], hy_sin_freq[0], hy_decay[0])
    coef_p = _hyena_filter_spectrum(lp, *filt)
    coef_s = _hyena_filter_spectrum(ls, *filt)
    w_in = _bf(hy_w_in[0])
    sconv_b = hy_sconv_b[0].reshape(1, -1)
    hy_bias0 = hy_bias[0].reshape(1, -1)
    x0_p, vx_p = _hy_in(xp, mods, 0, row_p, w_in, hy_sconv_w[0], sconv_b, lp, tm)
    x0_s, vx_s = _hy_in(xs, mods, 0, lambda i: 1 + i, w_in, hy_sconv_w[0], sconv_b, ls, ls)
    u_p = _hy_conv(vx_p, x0_p, coef_p, hy_bias0, lp)
    u_s = _hy_conv(vx_s, x0_s, coef_s, hy_bias0, ls)
    ln = [a.reshape(DEPTH, 1, d) for a in (ln1_g, ln1_b, ln2_g, ln2_b)]
    ffn = (ln[0], ln[1], _bf(ffn_w_in), _bf(ffn_w_out), ln[2], ln[3])
    hy_w_out_bf = _bf(hy_w_out)
    xp = _post(xp, u_p, mods, 0, row_p, hy_w_out_bf, 0, *ffn, tm=tm, name="post0_ctx")
    xs = _post(xs, u_s, mods, 0, row_s, hy_w_out_bf, 0, *ffn, tm=tm, name="post0_lat")

    w_qkv_t = _bf(na_w_qkv[0]).T
    qkvt_p, new_kt, new_vt = _qkv(xp, mods, 1, row_p, w_qkv_t, lp, True)
    (qkvt_s,) = _qkv(xs, mods, 1, row_s, w_qkv_t, tm, False)
    at_p = _ctx_attn(qkvt_p, lp)
    at_s = _lat_attn(qkvt_s, jnp.swapaxes(cache_k, 3, 4), jnp.swapaxes(cache_v, 3, 4),
                     _bias_table(na_rpb[0]), 0)
    na_w_out_bf = _bf(na_w_out)
    xp = _post(xp, at_p, mods, 1, row_p, na_w_out_bf, 0, *ffn, tm=tm, name="post1_ctx", a_feature_major=True)
    xs = _post(xs, at_s, mods, 1, row_s, na_w_out_bf, 0, *ffn, tm=tm, name="post1_lat", a_feature_major=True)

    return (xp.reshape(nbp, lp, d), xs.reshape(nbs, ls, d),
            jnp.swapaxes(new_kt, 3, 4), jnp.swapaxes(new_vt, 3, 4))
```

```python
import functools
import math

import jax
import jax.numpy as jnp
import numpy as np
from jax import lax
from jax.experimental import pallas as pl
from jax.experimental.pallas import tpu as pltpu

D_MODEL = 1024
N_HEADS = 16
HEAD_DIM = D_MODEL // N_HEADS
D_FF = 2816
GRID_W = 64
WIN_H = 8
WIN_W = 16
N_BANDS = 16
PE_DIM = 1 + 2 * N_BANDS
FILT_W = 64
MOD_SHIFT = 0.05
DEPTH = 2
ALPHA = (2 * DEPTH) ** 0.25
LN_EPS = 1e-5
MASK_VALUE = -1e30

LANES = 128
SUBLANES = 8
BF16_ROWS = 2 * SUBLANES
ATTN_LOOKAHEAD = 2
LOG2E = math.log2(math.e)
VMEM_LIMIT = 56 * 1024 * 1024

F32 = jnp.float32
BF16 = jnp.bfloat16


def _bf(x):
    return x.astype(BF16)


def _dot(a, b):
    return jnp.dot(a, b, preferred_element_type=F32)


def _dot_nt(a, b):
    return lax.dot_general(a, b, (((1,), (1,)), ((), ())), preferred_element_type=F32)


def _dot_tn(a, b):
    return lax.dot_general(a, b, (((0,), (0,)), ((), ())), preferred_element_type=F32)


def _layer_norm(x, g, b):
    mu = jnp.mean(x, axis=-1, keepdims=True)
    xc = x - mu
    var = jnp.mean(xc * xc, axis=-1, keepdims=True)
    return xc * lax.rsqrt(var + LN_EPS) * g + b


def _params(*sem):
    return pltpu.CompilerParams(dimension_semantics=sem, vmem_limit_bytes=VMEM_LIMIT)


class _Cast:
    def __init__(self, w, layer, steps, step_fn, transpose=False):
        _, r, c = w.shape
        assert r % steps == 0
        rows = r // steps
        self.w = w
        self.transpose = transpose
        self.in_spec = pl.BlockSpec((None, rows, c), lambda *g: (layer, step_fn(*g), 0))
        if transpose:
            self.out_shape = jax.ShapeDtypeStruct((c, r), BF16)
            self.out_spec = pl.BlockSpec((c, rows), lambda *g: (0, step_fn(*g)))
        else:
            self.out_shape = jax.ShapeDtypeStruct((r, c), BF16)
            self.out_spec = pl.BlockSpec((rows, c), lambda *g: (step_fn(*g), 0))


def _with_casts(body, n_in, n_out, casts):
    nc = len(casts)

    def kernel_fn(*refs):
        ins, src = refs[:n_in], refs[n_in:n_in + nc]
        outs, dst = refs[n_in + nc:n_in + nc + n_out], refs[n_in + nc + n_out:n_in + 2 * nc + n_out]
        for cast, s, d in zip(casts, src, dst):
            d[...] = _bf(s[...].T if cast.transpose else s[...])
        body(*ins, *outs, *refs[n_in + 2 * nc + n_out:])

    return kernel_fn


@functools.lru_cache(maxsize=None)
def _dft_tables(L):
    n = 2 * L
    f = np.arange(L)[:, None]
    s = np.arange(L)[None, :]
    ang = 2.0 * np.pi * ((f * s) % n) / n
    c = np.cos(ang)
    sn = np.sin(ang)
    sn[0, :] = np.where(np.arange(L) % 2 == 0, 1.0, -1.0)
    fwd = np.concatenate([c, sn], axis=0)
    w = np.full((2 * L, 1), 2.0 / n)
    w[0, 0] = 1.0 / n
    w[L, 0] = 1.0 / n
    inv = (fwd * w).T
    alt = np.where(np.arange(L) % 2 == 0, 1.0, -1.0).astype(np.float32)[:, None]
    return (np.ascontiguousarray(fwd).astype(np.float32), np.ascontiguousarray(inv).astype(np.float32), alt)


@functools.lru_cache(maxsize=None)
def _filter_features(L):
    t = np.linspace(0.0, 1.0, L, dtype=np.float32)[:, None]
    w = (np.float32(2.0 * math.pi / L) * np.arange(L, dtype=np.float32))[:, None]
    bands = np.linspace(1e-4, N_BANDS - 1, N_BANDS, dtype=np.float32)[None, :]
    arg = (bands * w).astype(np.float64)
    feats = np.concatenate([t.astype(np.float64), np.cos(arg), np.sin(arg)], axis=-1)
    out = np.zeros((L, LANES), np.float32)
    out[:, :PE_DIM] = feats
    return out, t


def _ada_kernel(c_ref, w_ref, b_ref, o_ref):
    c = c_ref[...]
    s = c * jax.nn.sigmoid(c)
    s_hi = _bf(s)
    s_lo = _bf(s - s_hi.astype(F32))
    w = w_ref[...]
    w_hi = _bf(w)
    w_lo = _bf(w - w_hi.astype(F32))
    o_ref[...] = _dot(s_hi, w_hi) + _dot(s_lo, w_hi) + _dot(s_hi, w_lo) + b_ref[...]


def _ada(cvec, ada_w, ada_b):
    tn = 1536
    nt = 6 * D_MODEL // tn
    return pl.pallas_call(
        _ada_kernel,
        out_shape=jax.ShapeDtypeStruct((DEPTH, 8, 6 * D_MODEL), F32),
        grid=(DEPTH, nt),
        in_specs=[
            pl.BlockSpec((8, D_MODEL), lambda l, j: (0, 0)),
            pl.BlockSpec((None, D_MODEL, tn), lambda l, j: (l, 0, j)),
            pl.BlockSpec((None, 1, tn), lambda l, j: (l, 0, j)),
        ],
        out_specs=pl.BlockSpec((None, 8, tn), lambda l, j: (l, 0, j)),
        compiler_params=_params("arbitrary", "arbitrary"),
        name="ada_mod",
    )(cvec, ada_w, ada_b.reshape(DEPTH, 1, 6 * D_MODEL))


def _filter_kernel(feats_ref, t_ref, alt_ref, w1_ref, b1_ref, w2_ref, b2_ref, w3f_ref, w3b_ref, freq_ref,
                   decf_ref, decb_ref, fwd_ref, out_ref, h_ref, *, L):
    hi = lax.Precision.HIGHEST

    @pl.when(pl.program_id(0) == 0)
    def _():
        h1 = jnp.sin(freq_ref[0:1, :] * (jnp.dot(feats_ref[...], w1_ref[...], precision=hi,
                                                 preferred_element_type=F32) + b1_ref[...]))
        h_ref[...] = jnp.sin(freq_ref[1:2, :] * (jnp.dot(h1, w2_ref[...], precision=hi,
                                                         preferred_element_type=F32) + b2_ref[...]))

    h = h_ref[...]
    t = t_ref[...]
    kf = jnp.dot(h, w3f_ref[...], precision=hi, preferred_element_type=F32) * (
        jnp.exp(-t * jnp.abs(decf_ref[...])) + MOD_SHIFT)
    kb = jnp.dot(h, w3b_ref[...], precision=hi, preferred_element_type=F32) * (
        jnp.exp(-t * jnp.abs(decb_ref[...])) + MOD_SHIFT)
    row = lax.broadcasted_iota(jnp.int32, kf.shape, 0)
    kb = jnp.where(row == 0, 0.0, kb)
    ksum = kf + kb
    kdiff = kf - kb
    ksum_hi = _bf(ksum)
    ksum_lo = _bf(ksum - ksum_hi.astype(F32))
    kdiff_hi = _bf(kdiff)
    kdiff_lo = _bf(kdiff - kdiff_hi.astype(F32))
    cos_m = fwd_ref[0:L, :]
    sin_m = fwd_ref[L:2 * L, :]
    k_cos = _dot(cos_m, ksum_hi) + _dot(cos_m, ksum_lo)
    k_sin = _dot(sin_m, kdiff_hi) + _dot(sin_m, kdiff_lo)
    k_nyq = jnp.sum(ksum * alt_ref[...], axis=0, keepdims=True)
    out_ref[0] = k_cos
    out_ref[1] = jnp.where(row == 0, 0.0, k_sin)
    out_ref[2] = jnp.where(row == 0, k_nyq, k_cos)


def _hyena_filter_spectrum(L, pe_w1, pe_b1, pe_w2, pe_b2, pe_w3, sin_freq, decay):
    feats, t = _filter_features(L)
    fwd, _, alt = _dft_tables(L)
    pad = LANES - FILT_W
    w1 = jnp.pad(pe_w1, ((0, LANES - PE_DIM), (0, pad)))
    b1 = jnp.pad(pe_b1, (0, pad)).reshape(1, LANES)
    w2 = jnp.pad(pe_w2, ((0, pad), (0, pad)))
    b2 = jnp.pad(pe_b2, (0, pad)).reshape(1, LANES)
    w3 = jnp.pad(pe_w3, ((0, pad), (0, 0)))
    freq = jnp.pad(sin_freq, ((0, 0), (0, pad)))
    dec = decay.reshape(1, 2 * D_MODEL)
    dt = 256
    nt = D_MODEL // dt
    full = lambda shape: pl.BlockSpec(shape, lambda j: (0,) * len(shape))
    return pl.pallas_call(
        functools.partial(_filter_kernel, L=L),
        out_shape=jax.ShapeDtypeStruct((3, L, D_MODEL), F32),
        grid=(nt,),
        in_specs=[
            full((L, LANES)), full((L, 1)), full((L, 1)),
            full((LANES, LANES)), full((1, LANES)), full((LANES, LANES)), full((1, LANES)),
            pl.BlockSpec((LANES, dt), lambda j: (0, j)),
            pl.BlockSpec((LANES, dt), lambda j: (0, nt + j)),
            full((2, LANES)),
            pl.BlockSpec((1, dt), lambda j: (0, j)),
            pl.BlockSpec((1, dt), lambda j: (0, nt + j)),
            full((2 * L, L)),
        ],
        out_specs=pl.BlockSpec((3, L, dt), lambda j: (0, 0, j)),
        scratch_shapes=[pltpu.VMEM((L, LANES), F32)],
        compiler_params=_params("arbitrary"),
        name=f"hyena_filter_{L}",
    )(feats, t, alt, w1, b1, w2, b2, w3, w3, freq, dec, dec, _bf(jnp.asarray(fwd)))


def _hy_in_kernel(x_ref, mod_ref, w_ref, cw_ref, cb_ref, x0_ref, vx_ref, *, L, cn):
    tm = x_ref.shape[0]
    x = x_ref[...]
    h = _bf(x * (1.0 + mod_ref[:, D_MODEL:2 * D_MODEL]) + mod_ref[:, 0:D_MODEL])
    row = lax.broadcasted_iota(jnp.int32, (SUBLANES, cn), 0)

    def conv(col):
        z = _dot(h, w_ref[:, col:col + cn])
        zm = pltpu.roll(z, 1, 0)
        zp = pltpu.roll(z, tm - 1, 0)
        w0 = cw_ref[0:1, col:col + cn]
        w1 = cw_ref[1:2, col:col + cn]
        w2 = cw_ref[2:3, col:col + cn]
        b = cb_ref[:, col:col + cn]
        out = zm * w0 + z * w1 + zp * w2 + b
        pieces = []
        for s in range(tm // L):
            lo = s * L
            hi = lo + L - SUBLANES
            head = (jnp.where(row == 0, 0.0, zm[lo:lo + SUBLANES]) * w0 + z[lo:lo + SUBLANES] * w1
                    + zp[lo:lo + SUBLANES] * w2 + b)
            tail = (zm[hi:hi + SUBLANES] * w0 + z[hi:hi + SUBLANES] * w1
                    + jnp.where(row == SUBLANES - 1, 0.0, zp[hi:hi + SUBLANES]) * w2 + b)
            pieces += [head, out[lo + SUBLANES:hi], tail]
        return jnp.concatenate(pieces, axis=0)

    for c in range(D_MODEL // cn):
        x0_ref[:, c * cn:(c + 1) * cn] = _bf(conv(c * cn))
        x1 = conv(D_MODEL + c * cn)
        v = conv(2 * D_MODEL + c * cn)
        vx_ref[:, c * cn:(c + 1) * cn] = _bf(v * x1)


def _hy_in(x2d, mods, layer, row_fn, w_in, sconv_w, sconv_b, L, tm, casts=()):
    n = x2d.shape[0]
    cn = 256
    tok = pl.BlockSpec((tm, D_MODEL), lambda i: (i, 0))
    act = jax.ShapeDtypeStruct((n, D_MODEL), BF16)
    return pl.pallas_call(
        _with_casts(functools.partial(_hy_in_kernel, L=L, cn=cn), 5, 2, casts),
        out_shape=(act, act) + tuple(c.out_shape for c in casts),
        grid=(n // tm,),
        in_specs=[
            tok,
            pl.BlockSpec((None, None, 1, 6 * D_MODEL), lambda i: (layer, row_fn(i), 0, 0)),
            pl.BlockSpec((D_MODEL, 3 * D_MODEL), lambda i: (0, 0)),
            pl.BlockSpec((3, 3 * D_MODEL), lambda i: (0, 0)),
            pl.BlockSpec((1, 3 * D_MODEL), lambda i: (0, 0)),
        ] + [c.in_spec for c in casts],
        out_specs=(tok, tok) + tuple(c.out_spec for c in casts),
        compiler_params=_params("arbitrary"),
        name=f"hyena_in_{L}",
    )(x2d, mods, w_in, sconv_w, sconv_b, *[c.w for c in casts])


def _hy_conv_kernel(vx_ref, x0_ref, coef_ref, bias_ref, fwd_ref, inv_ref, u_ref, *, L, nseq, cn, mt):
    for c in range(vx_ref.shape[1] // cn):
        cols = slice(c * cn, (c + 1) * cn)
        k_cos = coef_ref[0, :, cols]
        k_sin = coef_ref[1, :, cols]
        k_mix = coef_ref[2, :, cols]
        for s in range(nseq):
            vx = vx_ref[s * L:(s + 1) * L, cols]
            spec = _dot(fwd_ref[...], vx)
            v_cos = spec[0:L]
            v_sin = spec[L:2 * L]
            y_spec = _bf(jnp.concatenate([v_cos * k_cos - v_sin * k_sin, v_cos * k_sin + v_sin * k_mix],
                                         axis=0))
            for m in range(L // mt):
                rows = slice(s * L + m * mt, s * L + (m + 1) * mt)
                y = _dot(inv_ref[m * mt:(m + 1) * mt, :], y_spec)
                u = (y + bias_ref[:, cols] * vx_ref[rows, cols].astype(F32)) * x0_ref[rows, cols].astype(F32)
                u_ref[rows, cols] = _bf(u)


HY_CONV_TOKENS = 1024
HY_CONV_CHANNELS = 512


def _hy_conv_steps(n):
    nt = n // HY_CONV_TOKENS
    return (D_MODEL // HY_CONV_CHANNELS) * nt, lambda j, i: j * nt + i


def _hy_conv(vx, x0, coef, bias, L, casts=()):
    n = vx.shape[0]
    tm = HY_CONV_TOKENS
    dt = HY_CONV_CHANNELS
    fwd, inv, _ = _dft_tables(L)
    tok = pl.BlockSpec((tm, dt), lambda j, i: (i, j))
    body = functools.partial(_hy_conv_kernel, L=L, nseq=tm // L, cn=256 if L <= 256 else dt, mt=L)
    return pl.pallas_call(
        _with_casts(body, 6, 1, casts),
        out_shape=(jax.ShapeDtypeStruct((n, D_MODEL), BF16),) + tuple(c.out_shape for c in casts),
        grid=(D_MODEL // dt, n // tm),
        in_specs=[
            tok, tok,
            pl.BlockSpec((3, L, dt), lambda j, i: (0, 0, j)),
            pl.BlockSpec((1, dt), lambda j, i: (0, j)),
            pl.BlockSpec((2 * L, L), lambda j, i: (0, 0)),
            pl.BlockSpec((L, 2 * L), lambda j, i: (0, 0)),
        ] + [c.in_spec for c in casts],
        out_specs=(tok,) + tuple(c.out_spec for c in casts),
        compiler_params=_params("arbitrary", "arbitrary"),
        name=f"hyena_conv_{L}",
    )(vx, x0, coef, bias, _bf(jnp.asarray(fwd)), _bf(jnp.asarray(inv)), *[c.w for c in casts])


def _post_kernel(x_ref, a_ref, mod_ref, wo_ref, ln1g_ref, ln1b_ref, win_ref, wout_ref, ln2g_ref, ln2b_ref,
                 o_ref, act_ref, *, cf, a_feature_major):
    d = D_MODEL
    g1 = mod_ref[:, 2 * d:3 * d]
    sh2 = mod_ref[:, 3 * d:4 * d]
    sc2 = mod_ref[:, 4 * d:5 * d]
    g2 = mod_ref[:, 5 * d:6 * d]
    mix = (_dot_tn if a_feature_major else _dot)(a_ref[...], wo_ref[...])
    x1 = _layer_norm(ALPHA * x_ref[...] + g1 * mix, ln1g_ref[...], ln1b_ref[...])
    h = _bf(x1 * (1.0 + sc2) + sh2)
    for c in range(D_FF // cf):
        gate = _dot(h, win_ref[:, c * cf:(c + 1) * cf])
        up = _dot(h, win_ref[:, D_FF + c * cf:D_FF + (c + 1) * cf])
        act_ref[:, c * cf:(c + 1) * cf] = _bf(gate * jax.nn.sigmoid(gate) * up)
    f = _dot(act_ref[...], wout_ref[...])
    o_ref[...] = _layer_norm(ALPHA * x1 + g2 * f, ln2g_ref[...], ln2b_ref[...])


def _post(x2d, a2d, mods, layer, row_fn, w_o, w_in, w_out, ln1_g, ln1_b, ln2_g, ln2_b, tm, name,
          a_feature_major=False, casts=()):
    n = x2d.shape[0]
    tok = pl.BlockSpec((tm, D_MODEL), lambda i: (i, 0))
    a_spec = pl.BlockSpec((D_MODEL, tm), lambda i: (0, i)) if a_feature_major else tok
    const = lambda shape: pl.BlockSpec(shape, lambda i: (0, 0), pipeline_mode=pl.Buffered(1))
    vec = pl.BlockSpec((None, 1, D_MODEL), lambda i: (layer, 0, 0))
    body = functools.partial(_post_kernel, cf=256, a_feature_major=a_feature_major)
    return pl.pallas_call(
        _with_casts(body, 10, 1, casts),
        out_shape=(jax.ShapeDtypeStruct((n, D_MODEL), F32),) + tuple(c.out_shape for c in casts),
        grid=(n // tm,),
        in_specs=[
            tok, a_spec,
            pl.BlockSpec((None, None, 1, 6 * D_MODEL), lambda i: (layer, row_fn(i), 0, 0)),
            const((D_MODEL, D_MODEL)), vec, vec,
            const((D_MODEL, 2 * D_FF)), const((D_FF, D_MODEL)), vec, vec,
        ] + [c.in_spec for c in casts],
        out_specs=(tok,) + tuple(c.out_spec for c in casts),
        scratch_shapes=[pltpu.VMEM((tm, D_FF), BF16)],
        compiler_params=_params("arbitrary"),
        name=name,
    )(x2d, a2d, mods, w_o, ln1_g, ln1_b, w_in, w_out, ln2_g, ln2_b, *[c.w for c in casts])


def _qkv_kernel(x_ref, mod_ref, wt_ref, qkvt_ref, *kv_refs):
    d = D_MODEL
    tm = x_ref.shape[0]
    h = _bf(x_ref[...] * (1.0 + mod_ref[:, d:2 * d]) + mod_ref[:, 0:d])
    qkvt_ref[0:d, :] = _bf(_dot_nt(wt_ref[0:d, :], h) * (HEAD_DIM ** -0.5 * LOG2E))
    kt = _dot_nt(wt_ref[d:2 * d, :], h)
    qkvt_ref[d:2 * d, :] = _bf(kt)
    vt = _dot_nt(wt_ref[2 * d:3 * d, :], h)
    qkvt_ref[2 * d:3 * d, :] = _bf(vt)
    if kv_refs:
        nk_ref, nv_ref = kv_refs
        nk_ref[0, 0] = kt.reshape(N_HEADS, HEAD_DIM, tm)
        nv_ref[0, 0] = vt.reshape(N_HEADS, HEAD_DIM, tm)


def _qkv(x2d, mods, layer, row_fn, w_qkv_t, tm, seq_kv):
    n = x2d.shape[0]
    out_shape = [jax.ShapeDtypeStruct((3 * D_MODEL, n), BF16)]
    out_specs = [pl.BlockSpec((3 * D_MODEL, tm), lambda i: (0, i))]
    if seq_kv:
        nb = n // tm
        kv_shape = jax.ShapeDtypeStruct((nb, 1, N_HEADS, HEAD_DIM, tm), F32)
        kv_spec = pl.BlockSpec((1, 1, N_HEADS, HEAD_DIM, tm), lambda i: (i, 0, 0, 0, 0))
        out_shape += [kv_shape, kv_shape]
        out_specs += [kv_spec, kv_spec]
    return pl.pallas_call(
        _qkv_kernel,
        out_shape=tuple(out_shape),
        grid=(n // tm,),
        in_specs=[
            pl.BlockSpec((tm, D_MODEL), lambda i: (i, 0)),
            pl.BlockSpec((None, None, 1, 6 * D_MODEL), lambda i: (layer, row_fn(i), 0, 0)),
            pl.BlockSpec((3 * D_MODEL, D_MODEL), lambda i: (0, 0)),
        ],
        out_specs=tuple(out_specs),
        compiler_params=_params("arbitrary"),
        name="qkv_ctx" if seq_kv else "qkv_lat",
    )(x2d, mods, w_qkv_t)


def _softmax_weights(*scores):
    m = functools.reduce(jnp.maximum, [jnp.max(s, axis=0, keepdims=True) for s in scores])
    return [_bf(jnp.exp2(s - m)) for s in scores]


def _weighted_values(vt, p):
    ones = jnp.ones((BF16_ROWS, vt.shape[1]), BF16)
    return _dot(jnp.concatenate([vt, ones], axis=0), p)


def _normalise(acc):
    return _bf(acc[0:HEAD_DIM] * (1.0 / acc[HEAD_DIM:HEAD_DIM + 1]))


def _ctx_attn_kernel(qt_ref, kt_ref, vt_ref, ot_ref):
    def rows(hh):
        return slice(hh * HEAD_DIM, (hh + 1) * HEAD_DIM)

    def scores(hh):
        return _dot_tn(kt_ref[rows(hh), :], qt_ref[rows(hh), :])

    pending = [scores(hh) for hh in range(ATTN_LOOKAHEAD)]
    for hh in range(N_HEADS):
        if hh + ATTN_LOOKAHEAD < N_HEADS:
            pending.append(scores(hh + ATTN_LOOKAHEAD))
        (p,) = _softmax_weights(pending.pop(0))
        ot_ref[rows(hh), :] = _normalise(_weighted_values(vt_ref[rows(hh), :], p))


def _ctx_attn(qkvt, seq):
    n = qkvt.shape[1]
    return pl.pallas_call(
        _ctx_attn_kernel,
        out_shape=jax.ShapeDtypeStruct((D_MODEL, n), BF16),
        grid=(n // seq,),
        in_specs=[pl.BlockSpec((D_MODEL, seq), lambda b, c=c: (c, b)) for c in range(3)],
        out_specs=pl.BlockSpec((D_MODEL, seq), lambda b: (0, b)),
        compiler_params=_params("arbitrary"),
        name="ctx_attn",
    )(qkvt, qkvt, qkvt)


LAT_ROWS = 16
HALF_ROWS = LAT_ROWS // 2
LOCAL_ROWS = 12
HALF_KEY_ROW0 = (0, LAT_ROWS - LOCAL_ROWS)


def _fill_bias_table(rpb_ref, tab_ref):
    shape = (GRID_W, LANES)
    kc = lax.broadcasted_iota(jnp.int32, shape, 0)
    lane = lax.broadcasted_iota(jnp.int32, shape, 1)
    qc = lane % GRID_W
    win0 = jnp.clip(qc - WIN_W // 2, 0, GRID_W - WIN_W)
    col_ok = (kc >= win0) & (kc < win0 + WIN_W)
    masked = jnp.full(shape, MASK_VALUE, F32)
    tiles = []
    for dr in range(2 * WIN_H - 1):
        row = jnp.broadcast_to(rpb_ref[dr:dr + 1, :] * LOG2E, shape)
        t = pltpu.roll(row, LANES - (WIN_W - 1), 1, stride=1, stride_axis=0)
        t = jnp.where(lane < GRID_W, t, pltpu.roll(t, GRID_W, 1))
        tiles.append(jnp.where(col_ok, t, MASK_VALUE))

    def tile(qr, kr):
        row0 = min(max(qr - WIN_H // 2, 0), LAT_ROWS - WIN_H)
        if row0 <= kr < row0 + WIN_H:
            return tiles[kr - qr + WIN_H - 1]
        return masked

    for qr in range(0, LAT_ROWS, 2):
        kr0 = HALF_KEY_ROW0[qr // HALF_ROWS]
        for i in range(LOCAL_ROWS):
            both = jnp.where(lane < GRID_W, tile(qr, kr0 + i), tile(qr + 1, kr0 + i))
            tab_ref[i * GRID_W:(i + 1) * GRID_W, qr * GRID_W:(qr + 2) * GRID_W] = both


def _lat_attn_kernel(qt_ref, kt_ref, vt_ref, ckt_ref, cvt_ref, rpb_ref, ot_ref, tab_ref):
    for hh in range(tab_ref.shape[0]):
        _fill_bias_table(rpb_ref.at[hh], tab_ref.at[hh])
    seq = LAT_ROWS * GRID_W
    nq = HALF_ROWS * GRID_W
    nk = LOCAL_ROWS * GRID_W
    heads = qt_ref.shape[0] // HEAD_DIM
    units = [(b, hh, half) for b in range(qt_ref.shape[1] // seq) for hh in range(heads) for half in range(2)]

    def rows(hh):
        return slice(hh * HEAD_DIM, (hh + 1) * HEAD_DIM)

    def scores(unit):
        b, hh, half = unit
        key0 = b * seq + HALF_KEY_ROW0[half] * GRID_W
        qt = qt_ref[rows(hh), b * seq + half * nq:b * seq + (half + 1) * nq]
        s_loc = _dot_tn(kt_ref[rows(hh), key0:key0 + nk], qt) + tab_ref[hh, :, half * nq:(half + 1) * nq]
        s_ctx = _dot_tn(_bf(ckt_ref[b, hh]), qt)
        return s_loc, s_ctx

    pending = [scores(u) for u in units[:ATTN_LOOKAHEAD]]
    for i, (b, hh, half) in enumerate(units):
        if i + ATTN_LOOKAHEAD < len(units):
            pending.append(scores(units[i + ATTN_LOOKAHEAD]))
        s_loc, s_ctx = pending.pop(0)
        key0 = b * seq + HALF_KEY_ROW0[half] * GRID_W
        p_loc, p_ctx = _softmax_weights(s_loc, s_ctx)
        acc = (_weighted_values(vt_ref[rows(hh), key0:key0 + nk], p_loc)
               + _weighted_values(_bf(cvt_ref[b, hh]), p_ctx))
        ot_ref[rows(hh), b * seq + half * nq:b * seq + (half + 1) * nq] = _normalise(acc)


def _lat_attn(qkvt, cache_kt, cache_vt, rpb, layer_idx):
    rpb = jnp.pad(rpb[:, :, ::-1], ((0, 0), (0, 1), (0, LANES - (2 * WIN_W - 1))))
    n = qkvt.shape[1]
    seq = LAT_ROWS * GRID_W
    nb = n // seq
    nhp = N_HEADS // 2
    past = cache_kt.shape[4]
    feat = lambda c: pl.BlockSpec((2 * HEAD_DIM, n), lambda hp, c=c: (c * nhp + hp, 0))
    cache = pl.BlockSpec((nb, None, 2, HEAD_DIM, past), lambda hp: (0, layer_idx, hp, 0, 0))
    return pl.pallas_call(
        _lat_attn_kernel,
        out_shape=jax.ShapeDtypeStruct((D_MODEL, n), BF16),
        grid=(nhp,),
        in_specs=[feat(0), feat(1), feat(2), cache, cache,
                  pl.BlockSpec((2, 2 * WIN_H, LANES), lambda hp: (hp, 0, 0))],
        out_specs=pl.BlockSpec((2 * HEAD_DIM, n), lambda hp: (hp, 0)),
        scratch_shapes=[pltpu.VMEM((2, LOCAL_ROWS * GRID_W, seq), F32)],
        compiler_params=_params("arbitrary"),
        name="lat_attn",
    )(qkvt, qkvt, qkvt, cache_kt, cache_vt, rpb)


def kernel(x_prompt, x_sample, cache_k, cache_v, c, c_ctx, ada_w, ada_b, ln1_g, ln1_b, ln2_g, ln2_b, ffn_w_in, ffn_w_out, hy_w_in, hy_sconv_w, hy_sconv_b, hy_pe_w1, hy_pe_b1, hy_pe_w2, hy_pe_b2, hy_pe_w3, hy_sin_freq, hy_decay, hy_bias, hy_w_out, na_w_qkv, na_rpb, na_w_out):
    nbp, lp, d = x_prompt.shape
    nbs, ls, _ = x_sample.shape
    assert d == D_MODEL and ls == LAT_ROWS * GRID_W and c.shape[0] == nbs
    xp = x_prompt.reshape(nbp * lp, d)
    xs = x_sample.reshape(nbs * ls, d)

    cvec = jnp.concatenate([c_ctx[None, :], c, jnp.zeros((8 - 1 - nbs, d), F32)], axis=0)
    mods = _ada(cvec, ada_w, ada_b).reshape(DEPTH, 8, 1, 6 * d)
    row_p = lambda i: 0
    tm = 512
    row_s = lambda i: 1 + (i * tm) // ls

    filt = (hy_pe_w1[0], hy_pe_b1[0], hy_pe_w2[0], hy_pe_b2[0], hy_pe_w3[0], hy_sin_freq[0], hy_decay[0])
    coef_p = _hyena_filter_spectrum(lp, *filt)
    coef_s = _hyena_filter_spectrum(ls, *filt)
    w_in = _bf(hy_w_in[0])
    sconv_b = hy_sconv_b[0].reshape(1, -1)
    hy_bias0 = hy_bias[0].reshape(1, -1)
    np_steps, ns_steps = xp.shape[0] // tm, xs.shape[0] // ls
    one = lambda i: i
    x0_p, vx_p, ffn_w_in0 = _hy_in(xp, mods, 0, row_p, w_in, hy_sconv_w[0], sconv_b, lp, tm,
                                   casts=[_Cast(ffn_w_in, 0, np_steps, one)])
    x0_s, vx_s, ffn_w_out0 = _hy_in(xs, mods, 0, lambda i: 1 + i, w_in, hy_sconv_w[0], sconv_b, ls, ls,
                                    casts=[_Cast(ffn_w_out, 0, ns_steps, one)])
    steps, lin = _hy_conv_steps(xp.shape[0])
    u_p, hy_w_out0 = _hy_conv(vx_p, x0_p, coef_p, hy_bias0, lp, casts=[_Cast(hy_w_out, 0, steps, lin)])
    steps, lin = _hy_conv_steps(xs.shape[0])
    u_s, w_qkv_t = _hy_conv(vx_s, x0_s, coef_s, hy_bias0, ls,
                            casts=[_Cast(na_w_qkv, 0, steps, lin, transpose=True)])
    ln = [a.reshape(DEPTH, 1, d) for a in (ln1_g, ln1_b, ln2_g, ln2_b)]
    np_steps, ns_steps = xp.shape[0] // tm, xs.shape[0] // tm
    xp, ffn_w_in1 = _post(xp, u_p, mods, 0, row_p, hy_w_out0, ffn_w_in0, ffn_w_out0, *ln, tm=tm,
                          name="post0_ctx", casts=[_Cast(ffn_w_in, 1, np_steps, one)])
    xs, ffn_w_out1, na_w_out0 = _post(xs, u_s, mods, 0, row_s, hy_w_out0, ffn_w_in0, ffn_w_out0, *ln, tm=tm,
                                      name="post0_lat", casts=[_Cast(ffn_w_out, 1, ns_steps, one),
                                                               _Cast(na_w_out, 0, ns_steps, one)])

    qkvt_p, new_kt, new_vt = _qkv(xp, mods, 1, row_p, w_qkv_t, lp, True)
    (qkvt_s,) = _qkv(xs, mods, 1, row_s, w_qkv_t, tm, False)
    at_p = _ctx_attn(qkvt_p, lp)
    at_s = _lat_attn(qkvt_s, jnp.swapaxes(cache_k, 3, 4), jnp.swapaxes(cache_v, 3, 4), na_rpb[0], 0)
    (xp,) = _post(xp, at_p, mods, 1, row_p, na_w_out0, ffn_w_in1, ffn_w_out1, *ln, tm=tm, name="post1_ctx",
                  a_feature_major=True)
    (xs,) = _post(xs, at_s, mods, 1, row_s, na_w_out0, ffn_w_in1, ffn_w_out1, *ln, tm=tm, name="post1_lat",
                  a_feature_major=True)

    return (xp.reshape(nbp, lp, d), xs.reshape(nbs, ls, d),
            jnp.swapaxes(new_kt, 3, 4), jnp.swapaxes(new_vt, 3, 4))
```

```python
import functools
import math

import jax
import jax.numpy as jnp
import numpy as np
from jax import lax
from jax.experimental import pallas as pl
from jax.experimental.pallas import tpu as pltpu

D_MODEL = 1024
N_HEADS = 16
HEAD_DIM = D_MODEL // N_HEADS
D_FF = 2816
GRID_W = 64
WIN_H = 8
WIN_W = 16
N_BANDS = 16
PE_DIM = 1 + 2 * N_BANDS
FILT_W = 64
MOD_SHIFT = 0.05
DEPTH = 2
ALPHA = (2 * DEPTH) ** 0.25
LN_EPS = 1e-5
MASK_VALUE = -1e30

LANES = 128
SUBLANES = 8
BF16_ROWS = 2 * SUBLANES
ATTN_LOOKAHEAD = 2
LOG2E = math.log2(math.e)
VMEM_LIMIT = 56 * 1024 * 1024

F32 = jnp.float32
BF16 = jnp.bfloat16


def _bf(x):
    return x.astype(BF16)


def _dot(a, b):
    return jnp.dot(a, b, preferred_element_type=F32)


def _dot_nt(a, b):
    return lax.dot_general(a, b, (((1,), (1,)), ((), ())), preferred_element_type=F32)


def _dot_tn(a, b):
    return lax.dot_general(a, b, (((0,), (0,)), ((), ())), preferred_element_type=F32)


def _layer_norm(x, g, b):
    mu = jnp.mean(x, axis=-1, keepdims=True)
    xc = x - mu
    var = jnp.mean(xc * xc, axis=-1, keepdims=True)
    return xc * lax.rsqrt(var + LN_EPS) * g + b


def _params(*sem):
    return pltpu.CompilerParams(dimension_semantics=sem, vmem_limit_bytes=VMEM_LIMIT)


class _Cast:
    def __init__(self, w, layer, steps, step_fn, transpose=False):
        _, r, c = w.shape
        assert r % steps == 0
        rows = r // steps
        self.w = w
        self.transpose = transpose
        self.in_spec = pl.BlockSpec((None, rows, c), lambda *g: (layer, step_fn(*g), 0))
        if transpose:
            self.out_shape = jax.ShapeDtypeStruct((c, r), BF16)
            self.out_spec = pl.BlockSpec((c, rows), lambda *g: (0, step_fn(*g)))
        else:
            self.out_shape = jax.ShapeDtypeStruct((r, c), BF16)
            self.out_spec = pl.BlockSpec((rows, c), lambda *g: (step_fn(*g), 0))


def _with_casts(body, n_in, n_out, casts):
    nc = len(casts)

    def kernel_fn(*refs):
        ins, src = refs[:n_in], refs[n_in:n_in + nc]
        outs, dst = refs[n_in + nc:n_in + nc + n_out], refs[n_in + nc + n_out:n_in + 2 * nc + n_out]
        for cast, s, d in zip(casts, src, dst):
            d[...] = _bf(s[...].T if cast.transpose else s[...])
        body(*ins, *outs, *refs[n_in + 2 * nc + n_out:])

    return kernel_fn


@functools.lru_cache(maxsize=None)
def _dft_tables(L):
    n = 2 * L
    f = np.arange(L)[:, None]
    s = np.arange(L)[None, :]
    ang = 2.0 * np.pi * ((f * s) % n) / n
    c = np.cos(ang)
    sn = np.sin(ang)
    sn[0, :] = np.where(np.arange(L) % 2 == 0, 1.0, -1.0)
    fwd = np.concatenate([c, sn], axis=0)
    w = np.full((2 * L, 1), 2.0 / n)
    w[0, 0] = 1.0 / n
    w[L, 0] = 1.0 / n
    inv = (fwd * w).T
    alt = np.where(np.arange(L) % 2 == 0, 1.0, -1.0).astype(np.float32)[:, None]
    return (np.ascontiguousarray(fwd).astype(np.float32), np.ascontiguousarray(inv).astype(np.float32), alt)


@functools.lru_cache(maxsize=None)
def _filter_features(L):
    t = np.linspace(0.0, 1.0, L, dtype=np.float32)[:, None]
    w = (np.float32(2.0 * math.pi / L) * np.arange(L, dtype=np.float32))[:, None]
    bands = np.linspace(1e-4, N_BANDS - 1, N_BANDS, dtype=np.float32)[None, :]
    arg = (bands * w).astype(np.float64)
    feats = np.concatenate([t.astype(np.float64), np.cos(arg), np.sin(arg)], axis=-1)
    out = np.zeros((L, LANES), np.float32)
    out[:, :PE_DIM] = feats
    return out, t


def _ada_kernel(c_ref, w_ref, b_ref, o_ref):
    c = c_ref[...]
    s = c * jax.nn.sigmoid(c)
    s_hi = _bf(s)
    s_lo = _bf(s - s_hi.astype(F32))
    w = w_ref[...]
    w_hi = _bf(w)
    w_lo = _bf(w - w_hi.astype(F32))
    o_ref[...] = _dot(s_hi, w_hi) + _dot(s_lo, w_hi) + _dot(s_hi, w_lo) + b_ref[...]


def _ada(cvec, ada_w, ada_b):
    tn = 1536
    nt = 6 * D_MODEL // tn
    return pl.pallas_call(
        _ada_kernel,
        out_shape=jax.ShapeDtypeStruct((DEPTH, 8, 6 * D_MODEL), F32),
        grid=(DEPTH, nt),
        in_specs=[
            pl.BlockSpec((8, D_MODEL), lambda l, j: (0, 0)),
            pl.BlockSpec((None, D_MODEL, tn), lambda l, j: (l, 0, j)),
            pl.BlockSpec((None, 1, tn), lambda l, j: (l, 0, j)),
        ],
        out_specs=pl.BlockSpec((None, 8, tn), lambda l, j: (l, 0, j)),
        compiler_params=_params("arbitrary", "arbitrary"),
        name="ada_mod",
    )(cvec, ada_w, ada_b.reshape(DEPTH, 1, 6 * D_MODEL))


def _filter_kernel(feats_ref, t_ref, alt_ref, w1_ref, b1_ref, w2_ref, b2_ref, w3f_ref, w3b_ref, freq_ref,
                   decf_ref, decb_ref, fwd_ref, out_ref, h_ref, *, L):
    hi = lax.Precision.HIGHEST

    @pl.when(pl.program_id(0) == 0)
    def _():
        h1 = jnp.sin(freq_ref[0:1, :] * (jnp.dot(feats_ref[...], w1_ref[...], precision=hi,
                                                 preferred_element_type=F32) + b1_ref[...]))
        h_ref[...] = jnp.sin(freq_ref[1:2, :] * (jnp.dot(h1, w2_ref[...], precision=hi,
                                                         preferred_element_type=F32) + b2_ref[...]))

    h = h_ref[...]
    t = t_ref[...]
    kf = jnp.dot(h, w3f_ref[...], precision=hi, preferred_element_type=F32) * (
        jnp.exp(-t * jnp.abs(decf_ref[...])) + MOD_SHIFT)
    kb = jnp.dot(h, w3b_ref[...], precision=hi, preferred_element_type=F32) * (
        jnp.exp(-t * jnp.abs(decb_ref[...])) + MOD_SHIFT)
    row = lax.broadcasted_iota(jnp.int32, kf.shape, 0)
    kb = jnp.where(row == 0, 0.0, kb)
    ksum = kf + kb
    kdiff = kf - kb
    ksum_hi = _bf(ksum)
    ksum_lo = _bf(ksum - ksum_hi.astype(F32))
    kdiff_hi = _bf(kdiff)
    kdiff_lo = _bf(kdiff - kdiff_hi.astype(F32))
    cos_m = fwd_ref[0:L, :]
    sin_m = fwd_ref[L:2 * L, :]
    k_cos = _dot(cos_m, ksum_hi) + _dot(cos_m, ksum_lo)
    k_sin = _dot(sin_m, kdiff_hi) + _dot(sin_m, kdiff_lo)
    k_nyq = jnp.sum(ksum * alt_ref[...], axis=0, keepdims=True)
    out_ref[0] = k_cos
    out_ref[1] = jnp.where(row == 0, 0.0, k_sin)
    out_ref[2] = jnp.where(row == 0, k_nyq, k_cos)


def _hyena_filter_spectrum(L, pe_w1, pe_b1, pe_w2, pe_b2, pe_w3, sin_freq, decay):
    feats, t = _filter_features(L)
    fwd, _, alt = _dft_tables(L)
    pad = LANES - FILT_W
    w1 = jnp.pad(pe_w1, ((0, LANES - PE_DIM), (0, pad)))
    b1 = jnp.pad(pe_b1, (0, pad)).reshape(1, LANES)
    w2 = jnp.pad(pe_w2, ((0, pad), (0, pad)))
    b2 = jnp.pad(pe_b2, (0, pad)).reshape(1, LANES)
    w3 = jnp.pad(pe_w3, ((0, pad), (0, 0)))
    freq = jnp.pad(sin_freq, ((0, 0), (0, pad)))
    dec = decay.reshape(1, 2 * D_MODEL)
    dt = 256
    nt = D_MODEL // dt
    full = lambda shape: pl.BlockSpec(shape, lambda j: (0,) * len(shape))
    return pl.pallas_call(
        functools.partial(_filter_kernel, L=L),
        out_shape=jax.ShapeDtypeStruct((3, L, D_MODEL), F32),
        grid=(nt,),
        in_specs=[
            full((L, LANES)), full((L, 1)), full((L, 1)),
            full((LANES, LANES)), full((1, LANES)), full((LANES, LANES)), full((1, LANES)),
            pl.BlockSpec((LANES, dt), lambda j: (0, j)),
            pl.BlockSpec((LANES, dt), lambda j: (0, nt + j)),
            full((2, LANES)),
            pl.BlockSpec((1, dt), lambda j: (0, j)),
            pl.BlockSpec((1, dt), lambda j: (0, nt + j)),
            full((2 * L, L)),
        ],
        out_specs=pl.BlockSpec((3, L, dt), lambda j: (0, 0, j)),
        scratch_shapes=[pltpu.VMEM((L, LANES), F32)],
        compiler_params=_params("arbitrary"),
        name=f"hyena_filter_{L}",
    )(feats, t, alt, w1, b1, w2, b2, w3, w3, freq, dec, dec, _bf(jnp.asarray(fwd)))


def _hy_in_kernel(x_ref, mod_ref, w_ref, cw_ref, cb_ref, x0_ref, vx_ref, *, L, cn):
    tm = x_ref.shape[0]
    x = x_ref[...]
    h = _bf(x * (1.0 + mod_ref[:, D_MODEL:2 * D_MODEL]) + mod_ref[:, 0:D_MODEL])
    row = lax.broadcasted_iota(jnp.int32, (SUBLANES, cn), 0)

    def conv(col):
        z = _dot(h, w_ref[:, col:col + cn])
        zm = pltpu.roll(z, 1, 0)
        zp = pltpu.roll(z, tm - 1, 0)
        w0 = cw_ref[0:1, col:col + cn]
        w1 = cw_ref[1:2, col:col + cn]
        w2 = cw_ref[2:3, col:col + cn]
        b = cb_ref[:, col:col + cn]
        out = zm * w0 + z * w1 + zp * w2 + b
        pieces = []
        for s in range(tm // L):
            lo = s * L
            hi = lo + L - SUBLANES
            head = (jnp.where(row == 0, 0.0, zm[lo:lo + SUBLANES]) * w0 + z[lo:lo + SUBLANES] * w1
                    + zp[lo:lo + SUBLANES] * w2 + b)
            tail = (zm[hi:hi + SUBLANES] * w0 + z[hi:hi + SUBLANES] * w1
                    + jnp.where(row == SUBLANES - 1, 0.0, zp[hi:hi + SUBLANES]) * w2 + b)
            pieces += [head, out[lo + SUBLANES:hi], tail]
        return jnp.concatenate(pieces, axis=0)

    for c in range(D_MODEL // cn):
        x0_ref[:, c * cn:(c + 1) * cn] = _bf(conv(c * cn))
        x1 = conv(D_MODEL + c * cn)
        v = conv(2 * D_MODEL + c * cn)
        vx_ref[:, c * cn:(c + 1) * cn] = _bf(v * x1)


def _hy_in(x2d, mods, layer, row_fn, w_in, sconv_w, sconv_b, L, tm, casts=()):
    n = x2d.shape[0]
    cn = 256
    tok = pl.BlockSpec((tm, D_MODEL), lambda i: (i, 0))
    act = jax.ShapeDtypeStruct((n, D_MODEL), BF16)
    return pl.pallas_call(
        _with_casts(functools.partial(_hy_in_kernel, L=L, cn=cn), 5, 2, casts),
        out_shape=(act, act) + tuple(c.out_shape for c in casts),
        grid=(n // tm,),
        in_specs=[
            tok,
            pl.BlockSpec((None, None, 1, 6 * D_MODEL), lambda i: (layer, row_fn(i), 0, 0)),
            pl.BlockSpec((D_MODEL, 3 * D_MODEL), lambda i: (0, 0)),
            pl.BlockSpec((3, 3 * D_MODEL), lambda i: (0, 0)),
            pl.BlockSpec((1, 3 * D_MODEL), lambda i: (0, 0)),
        ] + [c.in_spec for c in casts],
        out_specs=(tok, tok) + tuple(c.out_spec for c in casts),
        compiler_params=_params("arbitrary"),
        name=f"hyena_in_{L}",
    )(x2d, mods, w_in, sconv_w, sconv_b, *[c.w for c in casts])


def _hy_conv_kernel(vx_ref, x0_ref, coef_ref, bias_ref, fwd_ref, inv_ref, u_ref, *, L, nseq, cn, mt):
    for c in range(vx_ref.shape[1] // cn):
        cols = slice(c * cn, (c + 1) * cn)
        k_cos = coef_ref[0, :, cols]
        k_sin = coef_ref[1, :, cols]
        k_mix = coef_ref[2, :, cols]
        for s in range(nseq):
            vx = vx_ref[s * L:(s + 1) * L, cols]
            spec = _dot(fwd_ref[...], vx)
            v_cos = spec[0:L]
            v_sin = spec[L:2 * L]
            y_spec = _bf(jnp.concatenate([v_cos * k_cos - v_sin * k_sin, v_cos * k_sin + v_sin * k_mix],
                                         axis=0))
            for m in range(L // mt):
                rows = slice(s * L + m * mt, s * L + (m + 1) * mt)
                y = _dot(inv_ref[m * mt:(m + 1) * mt, :], y_spec)
                u = (y + bias_ref[:, cols] * vx_ref[rows, cols].astype(F32)) * x0_ref[rows, cols].astype(F32)
                u_ref[rows, cols] = _bf(u)


HY_CONV_TOKENS = 1024
HY_CONV_CHANNELS = 512


def _hy_conv_steps(n):
    nt = n // HY_CONV_TOKENS
    return (D_MODEL // HY_CONV_CHANNELS) * nt, lambda j, i: j * nt + i


def _hy_conv(vx, x0, coef, bias, L, casts=()):
    n = vx.shape[0]
    tm = HY_CONV_TOKENS
    dt = HY_CONV_CHANNELS
    fwd, inv, _ = _dft_tables(L)
    tok = pl.BlockSpec((tm, dt), lambda j, i: (i, j))
    body = functools.partial(_hy_conv_kernel, L=L, nseq=tm // L, cn=256 if L <= 256 else dt, mt=L)
    return pl.pallas_call(
        _with_casts(body, 6, 1, casts),
        out_shape=(jax.ShapeDtypeStruct((n, D_MODEL), BF16),) + tuple(c.out_shape for c in casts),
        grid=(D_MODEL // dt, n // tm),
        in_specs=[
            tok, tok,
            pl.BlockSpec((3, L, dt), lambda j, i: (0, 0, j)),
            pl.BlockSpec((1, dt), lambda j, i: (0, j)),
            pl.BlockSpec((2 * L, L), lambda j, i: (0, 0)),
            pl.BlockSpec((L, 2 * L), lambda j, i: (0, 0)),
        ] + [c.in_spec for c in casts],
        out_specs=(tok,) + tuple(c.out_spec for c in casts),
        compiler_params=_params("arbitrary", "arbitrary"),
        name=f"hyena_conv_{L}",
    )(vx, x0, coef, bias, _bf(jnp.asarray(fwd)), _bf(jnp.asarray(inv)), *[c.w for c in casts])


def _post_kernel(x_ref, a_ref, mod_ref, wo_ref, ln1g_ref, ln1b_ref, win_ref, wout_ref, ln2g_ref, ln2b_ref,
                 o_ref, act_ref, *, cf, a_feature_major, row_groups):
    d = D_MODEL
    g1 = mod_ref[:, 2 * d:3 * d]
    sh2 = mod_ref[:, 3 * d:4 * d]
    sc2 = mod_ref[:, 4 * d:5 * d]
    g2 = mod_ref[:, 5 * d:6 * d]
    tm = x_ref.shape[0]
    groups = [slice(r * (tm // row_groups), (r + 1) * (tm // row_groups)) for r in range(row_groups)]
    if a_feature_major:
        mix = [_dot_tn(a_ref[:, rows], wo_ref[...]) for rows in groups]
    else:
        mix = [_dot(a_ref[rows, :], wo_ref[...]) for rows in groups]
    x1 = []
    for rows, mx in zip(groups, mix):
        x1.append(_layer_norm(ALPHA * x_ref[rows, :] + g1 * mx, ln1g_ref[...], ln1b_ref[...]))
        h = _bf(x1[-1] * (1.0 + sc2) + sh2)
        for c in range(D_FF // cf):
            gate = _dot(h, win_ref[:, c * cf:(c + 1) * cf])
            up = _dot(h, win_ref[:, D_FF + c * cf:D_FF + (c + 1) * cf])
            act_ref[rows, c * cf:(c + 1) * cf] = _bf(gate * jax.nn.sigmoid(gate) * up)
    for rows, xr in zip(groups, x1):
        f = _dot(act_ref[rows, :], wout_ref[...])
        o_ref[rows, :] = _layer_norm(ALPHA * xr + g2 * f, ln2g_ref[...], ln2b_ref[...])


def _post(x2d, a2d, mods, layer, row_fn, w_o, w_in, w_out, ln1_g, ln1_b, ln2_g, ln2_b, tm, name,
          a_feature_major=False, casts=()):
    n = x2d.shape[0]
    tok = pl.BlockSpec((tm, D_MODEL), lambda i: (i, 0))
    a_spec = pl.BlockSpec((D_MODEL, tm), lambda i: (0, i)) if a_feature_major else tok
    const = lambda shape: pl.BlockSpec(shape, lambda i: (0, 0), pipeline_mode=pl.Buffered(1))
    vec = pl.BlockSpec((None, 1, D_MODEL), lambda i: (layer, 0, 0))
    body = functools.partial(_post_kernel, cf=256, a_feature_major=a_feature_major, row_groups=2)
    return pl.pallas_call(
        _with_casts(body, 10, 1, casts),
        out_shape=(jax.ShapeDtypeStruct((n, D_MODEL), F32),) + tuple(c.out_shape for c in casts),
        grid=(n // tm,),
        in_specs=[
            tok, a_spec,
            pl.BlockSpec((None, None, 1, 6 * D_MODEL), lambda i: (layer, row_fn(i), 0, 0)),
            const((D_MODEL, D_MODEL)), vec, vec,
            const((D_MODEL, 2 * D_FF)), const((D_FF, D_MODEL)), vec, vec,
        ] + [c.in_spec for c in casts],
        out_specs=(tok,) + tuple(c.out_spec for c in casts),
        scratch_shapes=[pltpu.VMEM((tm, D_FF), BF16)],
        compiler_params=_params("arbitrary"),
        name=name,
    )(x2d, a2d, mods, w_o, ln1_g, ln1_b, w_in, w_out, ln2_g, ln2_b, *[c.w for c in casts])


def _qkv_kernel(x_ref, mod_ref, wt_ref, qkvt_ref, *kv_refs):
    d = D_MODEL
    tm = x_ref.shape[0]
    h = _bf(x_ref[...] * (1.0 + mod_ref[:, d:2 * d]) + mod_ref[:, 0:d])
    qkvt_ref[0:d, :] = _bf(_dot_nt(wt_ref[0:d, :], h) * (HEAD_DIM ** -0.5 * LOG2E))
    kt = _dot_nt(wt_ref[d:2 * d, :], h)
    qkvt_ref[d:2 * d, :] = _bf(kt)
    vt = _dot_nt(wt_ref[2 * d:3 * d, :], h)
    qkvt_ref[2 * d:3 * d, :] = _bf(vt)
    if kv_refs:
        nk_ref, nv_ref = kv_refs
        nk_ref[0, 0] = kt.reshape(N_HEADS, HEAD_DIM, tm)
        nv_ref[0, 0] = vt.reshape(N_HEADS, HEAD_DIM, tm)


def _qkv(x2d, mods, layer, row_fn, w_qkv_t, tm, seq_kv):
    n = x2d.shape[0]
    out_shape = [jax.ShapeDtypeStruct((3 * D_MODEL, n), BF16)]
    out_specs = [pl.BlockSpec((3 * D_MODEL, tm), lambda i: (0, i))]
    if seq_kv:
        nb = n // tm
        kv_shape = jax.ShapeDtypeStruct((nb, 1, N_HEADS, HEAD_DIM, tm), F32)
        kv_spec = pl.BlockSpec((1, 1, N_HEADS, HEAD_DIM, tm), lambda i: (i, 0, 0, 0, 0))
        out_shape += [kv_shape, kv_shape]
        out_specs += [kv_spec, kv_spec]
    return pl.pallas_call(
        _qkv_kernel,
        out_shape=tuple(out_shape),
        grid=(n // tm,),
        in_specs=[
            pl.BlockSpec((tm, D_MODEL), lambda i: (i, 0)),
            pl.BlockSpec((None, None, 1, 6 * D_MODEL), lambda i: (layer, row_fn(i), 0, 0)),
            pl.BlockSpec((3 * D_MODEL, D_MODEL), lambda i: (0, 0)),
        ],
        out_specs=tuple(out_specs),
        compiler_params=_params("arbitrary"),
        name="qkv_ctx" if seq_kv else "qkv_lat",
    )(x2d, mods, w_qkv_t)


def _softmax_weights(*scores):
    m = functools.reduce(jnp.maximum, [jnp.max(s, axis=0, keepdims=True) for s in scores])
    return [_bf(jnp.exp2(s - m)) for s in scores]


def _weighted_values(vt, p):
    ones = jnp.ones((BF16_ROWS, vt.shape[1]), BF16)
    return _dot(jnp.concatenate([vt, ones], axis=0), p)


def _normalise(acc):
    return _bf(acc[0:HEAD_DIM] * (1.0 / acc[HEAD_DIM:HEAD_DIM + 1]))


def _ctx_attn_kernel(qt_ref, kt_ref, vt_ref, ot_ref):
    def rows(hh):
        return slice(hh * HEAD_DIM, (hh + 1) * HEAD_DIM)

    def scores(hh):
        return _dot_tn(kt_ref[rows(hh), :], qt_ref[rows(hh), :])

    pending = [scores(hh) for hh in range(ATTN_LOOKAHEAD)]
    for hh in range(N_HEADS):
        if hh + ATTN_LOOKAHEAD < N_HEADS:
            pending.append(scores(hh + ATTN_LOOKAHEAD))
        (p,) = _softmax_weights(pending.pop(0))
        ot_ref[rows(hh), :] = _normalise(_weighted_values(vt_ref[rows(hh), :], p))


def _ctx_attn(qkvt, seq):
    n = qkvt.shape[1]
    return pl.pallas_call(
        _ctx_attn_kernel,
        out_shape=jax.ShapeDtypeStruct((D_MODEL, n), BF16),
        grid=(n // seq,),
        in_specs=[pl.BlockSpec((D_MODEL, seq), lambda b, c=c: (c, b)) for c in range(3)],
        out_specs=pl.BlockSpec((D_MODEL, seq), lambda b: (0, b)),
        compiler_params=_params("arbitrary"),
        name="ctx_attn",
    )(qkvt, qkvt, qkvt)


LAT_ROWS = 16
HALF_ROWS = LAT_ROWS // 2
LOCAL_ROWS = 12
HALF_KEY_ROW0 = (0, LAT_ROWS - LOCAL_ROWS)


def _fill_bias_table(rpb_ref, tab_ref):
    shape = (GRID_W, LANES)
    kc = lax.broadcasted_iota(jnp.int32, shape, 0)
    lane = lax.broadcasted_iota(jnp.int32, shape, 1)
    qc = lane % GRID_W
    win0 = jnp.clip(qc - WIN_W // 2, 0, GRID_W - WIN_W)
    col_ok = (kc >= win0) & (kc < win0 + WIN_W)
    masked = jnp.full(shape, MASK_VALUE, F32)
    tiles = []
    for dr in range(2 * WIN_H - 1):
        row = jnp.broadcast_to(rpb_ref[dr:dr + 1, :] * LOG2E, shape)
        t = pltpu.roll(row, LANES - (WIN_W - 1), 1, stride=1, stride_axis=0)
        t = jnp.where(lane < GRID_W, t, pltpu.roll(t, GRID_W, 1))
        tiles.append(jnp.where(col_ok, t, MASK_VALUE))

    def tile(qr, kr):
        row0 = min(max(qr - WIN_H // 2, 0), LAT_ROWS - WIN_H)
        if row0 <= kr < row0 + WIN_H:
            return tiles[kr - qr + WIN_H - 1]
        return masked

    for qr in range(0, LAT_ROWS, 2):
        kr0 = HALF_KEY_ROW0[qr // HALF_ROWS]
        for i in range(LOCAL_ROWS):
            both = jnp.where(lane < GRID_W, tile(qr, kr0 + i), tile(qr + 1, kr0 + i))
            tab_ref[i * GRID_W:(i + 1) * GRID_W, qr * GRID_W:(qr + 2) * GRID_W] = both


def _lat_attn_kernel(qt_ref, kt_ref, vt_ref, ckt_ref, cvt_ref, rpb_ref, ot_ref, tab_ref):
    for hh in range(tab_ref.shape[0]):
        _fill_bias_table(rpb_ref.at[hh], tab_ref.at[hh])
    seq = LAT_ROWS * GRID_W
    nq = HALF_ROWS * GRID_W
    nk = LOCAL_ROWS * GRID_W
    heads = qt_ref.shape[0] // HEAD_DIM
    units = [(b, hh, half) for b in range(qt_ref.shape[1] // seq) for hh in range(heads) for half in range(2)]

    def rows(hh):
        return slice(hh * HEAD_DIM, (hh + 1) * HEAD_DIM)

    def scores(unit):
        b, hh, half = unit
        key0 = b * seq + HALF_KEY_ROW0[half] * GRID_W
        qt = qt_ref[rows(hh), b * seq + half * nq:b * seq + (half + 1) * nq]
        s_loc = _dot_tn(kt_ref[rows(hh), key0:key0 + nk], qt) + tab_ref[hh, :, half * nq:(half + 1) * nq]
        s_ctx = _dot_tn(_bf(ckt_ref[b, hh]), qt)
        return s_loc, s_ctx

    pending = [scores(u) for u in units[:ATTN_LOOKAHEAD]]
    for i, (b, hh, half) in enumerate(units):
        if i + ATTN_LOOKAHEAD < len(units):
            pending.append(scores(units[i + ATTN_LOOKAHEAD]))
        s_loc, s_ctx = pending.pop(0)
        key0 = b * seq + HALF_KEY_ROW0[half] * GRID_W
        p_loc, p_ctx = _softmax_weights(s_loc, s_ctx)
        acc = (_weighted_values(vt_ref[rows(hh), key0:key0 + nk], p_loc)
               + _weighted_values(_bf(cvt_ref[b, hh]), p_ctx))
        ot_ref[rows(hh), b * seq + half * nq:b * seq + (half + 1) * nq] = _normalise(acc)


def _lat_attn(qkvt, cache_kt, cache_vt, rpb, layer_idx):
    rpb = jnp.pad(rpb[:, :, ::-1], ((0, 0), (0, 1), (0, LANES - (2 * WIN_W - 1))))
    n = qkvt.shape[1]
    seq = LAT_ROWS * GRID_W
    nb = n // seq
    nhp = N_HEADS // 2
    past = cache_kt.shape[4]
    feat = lambda c: pl.BlockSpec((2 * HEAD_DIM, n), lambda hp, c=c: (c * nhp + hp, 0))
    cache = pl.BlockSpec((nb, None, 2, HEAD_DIM, past), lambda hp: (0, layer_idx, hp, 0, 0))
    return pl.pallas_call(
        _lat_attn_kernel,
        out_shape=jax.ShapeDtypeStruct((D_MODEL, n), BF16),
        grid=(nhp,),
        in_specs=[feat(0), feat(1), feat(2), cache, cache,
                  pl.BlockSpec((2, 2 * WIN_H, LANES), lambda hp: (hp, 0, 0))],
        out_specs=pl.BlockSpec((2 * HEAD_DIM, n), lambda hp: (hp, 0)),
        scratch_shapes=[pltpu.VMEM((2, LOCAL_ROWS * GRID_W, seq), F32)],
        compiler_params=_params("arbitrary"),
        name="lat_attn",
    )(qkvt, qkvt, qkvt, cache_kt, cache_vt, rpb)


def kernel(x_prompt, x_sample, cache_k, cache_v, c, c_ctx, ada_w, ada_b, ln1_g, ln1_b, ln2_g, ln2_b, ffn_w_in, ffn_w_out, hy_w_in, hy_sconv_w, hy_sconv_b, hy_pe_w1, hy_pe_b1, hy_pe_w2, hy_pe_b2, hy_pe_w3, hy_sin_freq, hy_decay, hy_bias, hy_w_out, na_w_qkv, na_rpb, na_w_out):
    nbp, lp, d = x_prompt.shape
    nbs, ls, _ = x_sample.shape
    assert d == D_MODEL and ls == LAT_ROWS * GRID_W and c.shape[0] == nbs
    xp = x_prompt.reshape(nbp * lp, d)
    xs = x_sample.reshape(nbs * ls, d)

    cvec = jnp.concatenate([c_ctx[None, :], c, jnp.zeros((8 - 1 - nbs, d), F32)], axis=0)
    mods = _ada(cvec, ada_w, ada_b).reshape(DEPTH, 8, 1, 6 * d)
    row_p = lambda i: 0
    tm = 512
    row_s = lambda i: 1 + (i * tm) // ls

    filt = (hy_pe_w1[0], hy_pe_b1[0], hy_pe_w2[0], hy_pe_b2[0], hy_pe_w3[0], hy_sin_freq[0], hy_decay[0])
    coef_p = _hyena_filter_spectrum(lp, *filt)
    coef_s = _hyena_filter_spectrum(ls, *filt)
    w_in = _bf(hy_w_in[0])
    sconv_b = hy_sconv_b[0].reshape(1, -1)
    hy_bias0 = hy_bias[0].reshape(1, -1)
    np_steps, ns_steps = xp.shape[0] // tm, xs.shape[0] // ls
    one = lambda i: i
    x0_p, vx_p, ffn_w_in0 = _hy_in(xp, mods, 0, row_p, w_in, hy_sconv_w[0], sconv_b, lp, tm,
                                   casts=[_Cast(ffn_w_in, 0, np_steps, one)])
    x0_s, vx_s, ffn_w_out0 = _hy_in(xs, mods, 0, lambda i: 1 + i, w_in, hy_sconv_w[0], sconv_b, ls, ls,
                                    casts=[_Cast(ffn_w_out, 0, ns_steps, one)])
    steps, lin = _hy_conv_steps(xp.shape[0])
    u_p, hy_w_out0 = _hy_conv(vx_p, x0_p, coef_p, hy_bias0, lp, casts=[_Cast(hy_w_out, 0, steps, lin)])
    steps, lin = _hy_conv_steps(xs.shape[0])
    u_s, w_qkv_t = _hy_conv(vx_s, x0_s, coef_s, hy_bias0, ls,
                            casts=[_Cast(na_w_qkv, 0, steps, lin, transpose=True)])
    ln = [a.reshape(DEPTH, 1, d) for a in (ln1_g, ln1_b, ln2_g, ln2_b)]
    np_steps, ns_steps = xp.shape[0] // tm, xs.shape[0] // tm
    xp, ffn_w_in1 = _post(xp, u_p, mods, 0, row_p, hy_w_out0, ffn_w_in0, ffn_w_out0, *ln, tm=tm,
                          name="post0_ctx", casts=[_Cast(ffn_w_in, 1, np_steps, one)])
    xs, ffn_w_out1, na_w_out0 = _post(xs, u_s, mods, 0, row_s, hy_w_out0, ffn_w_in0, ffn_w_out0, *ln, tm=tm,
                                      name="post0_lat", casts=[_Cast(ffn_w_out, 1, ns_steps, one),
                                                               _Cast(na_w_out, 0, ns_steps, one)])

    qkvt_p, new_kt, new_vt = _qkv(xp, mods, 1, row_p, w_qkv_t, lp, True)
    (qkvt_s,) = _qkv(xs, mods, 1, row_s, w_qkv_t, tm, False)
    at_p = _ctx_attn(qkvt_p, lp)
    at_s = _lat_attn(qkvt_s, jnp.swapaxes(cache_k, 3, 4), jnp.swapaxes(cache_v, 3, 4), na_rpb[0], 0)
    (xp,) = _post(xp, at_p, mods, 1, row_p, na_w_out0, ffn_w_in1, ffn_w_out1, *ln, tm=tm, name="post1_ctx",
                  a_feature_major=True)
    (xs,) = _post(xs, at_s, mods, 1, row_s, na_w_out0, ffn_w_in1, ffn_w_out1, *ln, tm=tm, name="post1_lat",
                  a_feature_major=True)

    return (xp.reshape(nbp, lp, d), xs.reshape(nbs, ls, d),
            jnp.swapaxes(new_kt, 3, 4), jnp.swapaxes(new_vt, 3, 4))
```

```python
import functools
import math

import jax
import jax.numpy as jnp
import numpy as np
from jax import lax
from jax.experimental import pallas as pl
from jax.experimental.pallas import tpu as pltpu

D_MODEL = 1024
N_HEADS = 16
HEAD_DIM = D_MODEL // N_HEADS
D_FF = 2816
GRID_W = 64
WIN_H = 8
WIN_W = 16
N_BANDS = 16
PE_DIM = 1 + 2 * N_BANDS
FILT_W = 64
MOD_SHIFT = 0.05
DEPTH = 2
ALPHA = (2 * DEPTH) ** 0.25
LN_EPS = 1e-5
MASK_VALUE = -1e30
MOD_ROWS = 8

LANES = 128
SUBLANES = 8
BF16_ROWS = 2 * SUBLANES
ATTN_LOOKAHEAD = 2
LOG2E = math.log2(math.e)
VMEM_LIMIT = 56 * 1024 * 1024

ADA_COLS = 1536
FILTER_CHANNELS = 256
HY_IN_COLS = 256
HY_CONV_TOKENS = 1024
HY_CONV_CHANNELS = 512
POST_TOKENS = 512
POST_ROW_GROUPS = 2
FFN_COLS = 256
QKV_TOKENS = 256

F32 = jnp.float32
BF16 = jnp.bfloat16


def _bf(x):
    return x.astype(BF16)


def _dot(a, b):
    return jnp.dot(a, b, preferred_element_type=F32)


def _dot_nt(a, b):
    return lax.dot_general(a, b, (((1,), (1,)), ((), ())), preferred_element_type=F32)


def _dot_tn(a, b):
    return lax.dot_general(a, b, (((0,), (0,)), ((), ())), preferred_element_type=F32)


def _layer_norm(x, g, b):
    mu = jnp.mean(x, axis=-1, keepdims=True)
    xc = x - mu
    var = jnp.mean(xc * xc, axis=-1, keepdims=True)
    return xc * lax.rsqrt(var + LN_EPS) * g + b


def _params(*sem):
    return pltpu.CompilerParams(dimension_semantics=sem, vmem_limit_bytes=VMEM_LIMIT)


class _Cast:
    def __init__(self, w, layer, steps, step_fn, transpose=False):
        _, r, c = w.shape
        assert r % (steps * BF16_ROWS) == 0
        rows = r // steps
        self.w = w
        self.transpose = transpose
        self.in_spec = pl.BlockSpec((None, rows, c), lambda *g: (layer, step_fn(*g), 0))
        if transpose:
            self.out_shape = jax.ShapeDtypeStruct((c, r), BF16)
            self.out_spec = pl.BlockSpec((c, rows), lambda *g: (0, step_fn(*g)))
        else:
            self.out_shape = jax.ShapeDtypeStruct((r, c), BF16)
            self.out_spec = pl.BlockSpec((rows, c), lambda *g: (step_fn(*g), 0))


def _with_casts(body, n_in, n_out, casts):
    nc = len(casts)

    def kernel_fn(*refs):
        ins, src = refs[:n_in], refs[n_in:n_in + nc]
        outs, dst = refs[n_in + nc:n_in + nc + n_out], refs[n_in + nc + n_out:n_in + 2 * nc + n_out]
        for cast, s, d in zip(casts, src, dst):
            d[...] = _bf(s[...].T if cast.transpose else s[...])
        body(*ins, *outs, *refs[n_in + 2 * nc + n_out:])

    return kernel_fn


def _step(i):
    return i


def _ctx_block(i, ctx_steps):
    return jnp.minimum(i, ctx_steps - 1)


def _lat_block(i, ctx_steps):
    return jnp.maximum(i - ctx_steps, 0)


def _mod_row(i, ctx_steps, tm, lat_len):
    return jnp.where(i < ctx_steps, 0, 1 + ((i - ctx_steps) * tm) // lat_len)


def _pick(is_ctx, refs, idx):
    if len(refs) == 1:
        return refs[0][idx]
    return jnp.where(is_ctx, refs[0][idx], refs[1][idx])


@functools.lru_cache(maxsize=None)
def _dft_tables(L):
    n = 2 * L
    f = np.arange(L)[:, None]
    s = np.arange(L)[None, :]
    ang = 2.0 * np.pi * ((f * s) % n) / n
    c = np.cos(ang)
    sn = np.sin(ang)
    sn[0, :] = np.where(np.arange(L) % 2 == 0, 1.0, -1.0)
    fwd = np.concatenate([c, sn], axis=0)
    w = np.full((2 * L, 1), 2.0 / n)
    w[0, 0] = 1.0 / n
    w[L, 0] = 1.0 / n
    inv = (fwd * w).T
    alt = np.where(np.arange(L) % 2 == 0, 1.0, -1.0).astype(np.float32)[:, None]
    return (np.ascontiguousarray(fwd).astype(np.float32), np.ascontiguousarray(inv).astype(np.float32), alt)


@functools.lru_cache(maxsize=None)
def _filter_features(L):
    t = np.linspace(0.0, 1.0, L, dtype=np.float32)[:, None]
    w = (np.float32(2.0 * math.pi / L) * np.arange(L, dtype=np.float32))[:, None]
    bands = np.linspace(1e-4, N_BANDS - 1, N_BANDS, dtype=np.float32)[None, :]
    arg = (bands * w).astype(np.float64)
    feats = np.concatenate([t.astype(np.float64), np.cos(arg), np.sin(arg)], axis=-1)
    out = np.zeros((L, LANES), np.float32)
    out[:, :PE_DIM] = feats
    return out, t


def _ada_kernel(cctx_ref, c_ref, w_ref, b_ref, o_ref, cvec_ref):
    nb = c_ref.shape[0]
    cvec_ref[...] = jnp.zeros_like(cvec_ref)
    cvec_ref[0:1, :] = cctx_ref[...]
    cvec_ref[1:1 + nb, :] = c_ref[...]
    c = cvec_ref[...]
    s = c * jax.nn.sigmoid(c)
    s_hi = _bf(s)
    s_lo = _bf(s - s_hi.astype(F32))
    w = w_ref[...]
    w_hi = _bf(w)
    w_lo = _bf(w - w_hi.astype(F32))
    bias = b_ref[pl.ds(pl.program_id(0), 1), :]
    o_ref[...] = _dot(s_hi, w_hi) + _dot(s_lo, w_hi) + _dot(s_hi, w_lo) + bias


def _ada(c_ctx, c, ada_w, ada_b):
    assert 1 + c.shape[0] <= MOD_ROWS
    tn = ADA_COLS
    nt = 6 * D_MODEL // tn
    return pl.pallas_call(
        _ada_kernel,
        out_shape=jax.ShapeDtypeStruct((DEPTH, MOD_ROWS, 6 * D_MODEL), F32),
        grid=(DEPTH, nt),
        in_specs=[
            pl.BlockSpec((1, D_MODEL), lambda l, j: (0, 0)),
            pl.BlockSpec(c.shape, lambda l, j: (0, 0)),
            pl.BlockSpec((None, D_MODEL, tn), lambda l, j: (l, 0, j)),
            pl.BlockSpec((DEPTH, tn), lambda l, j: (0, j)),
        ],
        out_specs=pl.BlockSpec((None, MOD_ROWS, tn), lambda l, j: (l, 0, j)),
        scratch_shapes=[pltpu.VMEM((MOD_ROWS, D_MODEL), F32)],
        compiler_params=_params("arbitrary", "arbitrary"),
        name="ada_mod",
    )(c_ctx.reshape(1, D_MODEL), c, ada_w, ada_b)


def _filter_kernel(feats_ref, t_ref, alt_ref, w1_ref, b1_ref, w2_ref, b2_ref, w3f_ref, w3b_ref, freq_ref,
                   decf_ref, decb_ref, fwd_ref, out_ref, h_ref, *, L):
    hi = lax.Precision.HIGHEST

    @pl.when(pl.program_id(0) == 0)
    def _():
        h1 = jnp.sin(freq_ref[0:1, :] * (jnp.dot(feats_ref[...], w1_ref[...], precision=hi,
                                                 preferred_element_type=F32) + b1_ref[...]))
        h_ref[...] = jnp.sin(freq_ref[1:2, :] * (jnp.dot(h1, w2_ref[...], precision=hi,
                                                         preferred_element_type=F32) + b2_ref[...]))

    h = h_ref[...]
    t = t_ref[...]
    kf = jnp.dot(h, w3f_ref[...], precision=hi, preferred_element_type=F32) * (
        jnp.exp(-t * jnp.abs(decf_ref[...])) + MOD_SHIFT)
    kb = jnp.dot(h, w3b_ref[...], precision=hi, preferred_element_type=F32) * (
        jnp.exp(-t * jnp.abs(decb_ref[...])) + MOD_SHIFT)
    row = lax.broadcasted_iota(jnp.int32, kf.shape, 0)
    kb = jnp.where(row == 0, 0.0, kb)
    ksum = kf + kb
    kdiff = kf - kb
    ksum_hi = _bf(ksum)
    ksum_lo = _bf(ksum - ksum_hi.astype(F32))
    kdiff_hi = _bf(kdiff)
    kdiff_lo = _bf(kdiff - kdiff_hi.astype(F32))
    cos_m = fwd_ref[0:L, :]
    sin_m = fwd_ref[L:2 * L, :]
    k_cos = _dot(cos_m, ksum_hi) + _dot(cos_m, ksum_lo)
    k_sin = _dot(sin_m, kdiff_hi) + _dot(sin_m, kdiff_lo)
    k_nyq = jnp.sum(ksum * alt_ref[...], axis=0, keepdims=True)
    out_ref[0] = k_cos
    out_ref[1] = jnp.where(row == 0, 0.0, k_sin)
    out_ref[2] = jnp.where(row == 0, k_nyq, k_cos)


def _hyena_filter_spectrum(L, pe_w1, pe_b1, pe_w2, pe_b2, pe_w3, sin_freq, decay):
    feats, t = _filter_features(L)
    fwd, _, alt = _dft_tables(L)
    pad = LANES - FILT_W
    w1 = jnp.pad(pe_w1, ((0, LANES - PE_DIM), (0, pad)))
    b1 = jnp.pad(pe_b1, (0, pad)).reshape(1, LANES)
    w2 = jnp.pad(pe_w2, ((0, pad), (0, pad)))
    b2 = jnp.pad(pe_b2, (0, pad)).reshape(1, LANES)
    w3 = jnp.pad(pe_w3, ((0, pad), (0, 0)))
    freq = jnp.pad(sin_freq, ((0, 0), (0, pad)))
    dec = decay.reshape(1, 2 * D_MODEL)
    dt = FILTER_CHANNELS
    nt = D_MODEL // dt
    full = lambda shape: pl.BlockSpec(shape, lambda j: (0,) * len(shape))
    return pl.pallas_call(
        functools.partial(_filter_kernel, L=L),
        out_shape=jax.ShapeDtypeStruct((3, L, D_MODEL), F32),
        grid=(nt,),
        in_specs=[
            full((L, LANES)), full((L, 1)), full((L, 1)),
            full((LANES, LANES)), full((1, LANES)), full((LANES, LANES)), full((1, LANES)),
            pl.BlockSpec((LANES, dt), lambda j: (0, j)),
            pl.BlockSpec((LANES, dt), lambda j: (0, nt + j)),
            full((2, LANES)),
            pl.BlockSpec((1, dt), lambda j: (0, j)),
            pl.BlockSpec((1, dt), lambda j: (0, nt + j)),
            full((2 * L, L)),
        ],
        out_specs=pl.BlockSpec((3, L, dt), lambda j: (0, 0, j)),
        scratch_shapes=[pltpu.VMEM((L, LANES), F32)],
        compiler_params=_params("arbitrary"),
        name=f"hyena_filter_{L}",
    )(feats, t, alt, w1, b1, w2, b2, w3, w3, freq, dec, dec, _bf(jnp.asarray(fwd)))


def _hy_in_kernel(xc_ref, xl_ref, mod_ref, w_ref, cw_ref, cb_ref, x0_ref, vx_ref, *, ctx_steps, ctx_len, cn):
    i = pl.program_id(0)
    is_ctx = i < ctx_steps
    tm = x0_ref.shape[0]
    mod = mod_ref[pl.ds(_mod_row(i, ctx_steps, tm, tm), 1), :]
    x = _pick(is_ctx, (xc_ref, xl_ref), (slice(None), slice(None)))
    h = _bf(x * (1.0 + mod[:, D_MODEL:2 * D_MODEL]) + mod[:, 0:D_MODEL])
    row = lax.broadcasted_iota(jnp.int32, (SUBLANES, cn), 0)

    def conv(col):
        z = _dot(h, w_ref[:, col:col + cn])
        zm = pltpu.roll(z, 1, 0)
        zp = pltpu.roll(z, tm - 1, 0)
        w0 = cw_ref[0:1, col:col + cn]
        w1 = cw_ref[1:2, col:col + cn]
        w2 = cw_ref[2:3, col:col + cn]
        b = cb_ref[:, col:col + cn]
        out = zm * w0 + z * w1 + zp * w2 + b
        pieces = []
        for s in range(tm // ctx_len):
            lo = s * ctx_len
            hi = lo + ctx_len - SUBLANES
            head = (jnp.where(row == 0, 0.0, zm[lo:lo + SUBLANES]) * w0 + z[lo:lo + SUBLANES] * w1
                    + zp[lo:lo + SUBLANES] * w2 + b)
            tail = (zm[hi:hi + SUBLANES] * w0 + z[hi:hi + SUBLANES] * w1
                    + jnp.where(row == SUBLANES - 1, 0.0, zp[hi:hi + SUBLANES]) * w2 + b)
            if lo > 0:
                head = jnp.where(is_ctx, head, out[lo:lo + SUBLANES])
            if hi + SUBLANES < tm:
                tail = jnp.where(is_ctx, tail, out[hi:hi + SUBLANES])
            pieces += [head, out[lo + SUBLANES:hi], tail]
        return jnp.concatenate(pieces, axis=0)

    for c in range(D_MODEL // cn):
        x0_ref[:, c * cn:(c + 1) * cn] = _bf(conv(c * cn))
        x1 = conv(D_MODEL + c * cn)
        v = conv(2 * D_MODEL + c * cn)
        vx_ref[:, c * cn:(c + 1) * cn] = _bf(v * x1)


def _hy_in(x_ctx, x_lat, ctx_len, lat_len, mods, layer, w_in, sconv_w, sconv_b, casts=()):
    tm = lat_len
    assert tm % ctx_len == 0 and x_ctx.shape[0] % tm == 0
    ctx_steps = x_ctx.shape[0] // tm
    n = x_ctx.shape[0] + x_lat.shape[0]
    tok = pl.BlockSpec((tm, D_MODEL), lambda i: (i, 0))
    const = lambda shape: pl.BlockSpec(shape, lambda i: (0, 0), pipeline_mode=pl.Buffered(1))
    act = jax.ShapeDtypeStruct((n, D_MODEL), BF16)
    body = functools.partial(_hy_in_kernel, ctx_steps=ctx_steps, ctx_len=ctx_len, cn=HY_IN_COLS)
    return pl.pallas_call(
        _with_casts(body, 6, 2, casts),
        out_shape=(act, act) + tuple(c.out_shape for c in casts),
        grid=(n // tm,),
        in_specs=[
            pl.BlockSpec((tm, D_MODEL), lambda i: (_ctx_block(i, ctx_steps), 0)),
            pl.BlockSpec((tm, D_MODEL), lambda i: (_lat_block(i, ctx_steps), 0)),
            pl.BlockSpec((None, MOD_ROWS, 6 * D_MODEL), lambda i: (layer, 0, 0)),
            const((D_MODEL, 3 * D_MODEL)),
            pl.BlockSpec((3, 3 * D_MODEL), lambda i: (0, 0)),
            pl.BlockSpec((1, 3 * D_MODEL), lambda i: (0, 0)),
        ] + [c.in_spec for c in casts],
        out_specs=(tok, tok) + tuple(c.out_spec for c in casts),
        compiler_params=_params("arbitrary"),
        name="hyena_in",
    )(x_ctx, x_lat, mods, w_in, sconv_w, sconv_b, *[c.w for c in casts])


def _hy_conv_kernel(vx_ref, x0_ref, coef_ref, bias_ref, fwd_ref, inv_ref, u_ref, *, L, nseq, cn):
    for c in range(vx_ref.shape[1] // cn):
        cols = slice(c * cn, (c + 1) * cn)
        k_cos = coef_ref[0, :, cols]
        k_sin = coef_ref[1, :, cols]
        k_mix = coef_ref[2, :, cols]
        for s in range(nseq):
            rows = slice(s * L, (s + 1) * L)
            vx = vx_ref[rows, cols]
            spec = _dot(fwd_ref[...], vx)
            v_cos = spec[0:L]
            v_sin = spec[L:2 * L]
            y_spec = _bf(jnp.concatenate([v_cos * k_cos - v_sin * k_sin, v_cos * k_sin + v_sin * k_mix],
                                         axis=0))
            y = _dot(inv_ref[...], y_spec)
            u = (y + bias_ref[:, cols] * vx.astype(F32)) * x0_ref[rows, cols].astype(F32)
            u_ref[rows, cols] = _bf(u)


def _hy_conv_steps(n):
    nt = n // HY_CONV_TOKENS
    return (D_MODEL // HY_CONV_CHANNELS) * nt, lambda j, i: j * nt + i


def _hy_conv(vx, x0, tok0, n, coef, bias, L, casts=()):
    tm = HY_CONV_TOKENS
    dt = HY_CONV_CHANNELS
    assert tok0 % tm == 0 and n % tm == 0 and tm % L == 0
    fwd, inv, _ = _dft_tables(L)
    src = pl.BlockSpec((tm, dt), lambda j, i: (tok0 // tm + i, j))
    dst = pl.BlockSpec((tm, dt), lambda j, i: (i, j))
    const = lambda shape: pl.BlockSpec(shape, lambda j, i: (0, 0), pipeline_mode=pl.Buffered(1))
    body = functools.partial(_hy_conv_kernel, L=L, nseq=tm // L, cn=256 if L <= 256 else dt)
    return pl.pallas_call(
        _with_casts(body, 6, 1, casts),
        out_shape=(jax.ShapeDtypeStruct((n, D_MODEL), BF16),) + tuple(c.out_shape for c in casts),
        grid=(D_MODEL // dt, n // tm),
        in_specs=[
            src, src,
            pl.BlockSpec((3, L, dt), lambda j, i: (0, 0, j)),
            pl.BlockSpec((1, dt), lambda j, i: (0, j)),
            const((2 * L, L)),
            const((L, 2 * L)),
        ] + [c.in_spec for c in casts],
        out_specs=(dst,) + tuple(c.out_spec for c in casts),
        compiler_params=_params("arbitrary", "arbitrary"),
        name=f"hyena_conv_{L}",
    )(vx, x0, coef, bias, _bf(jnp.asarray(fwd)), _bf(jnp.asarray(inv)), *[c.w for c in casts])


def _post_kernel(*refs, layer, ctx_steps, lat_len, n_x, n_a, a_feature_major):
    x_refs, a_refs = refs[:n_x], refs[n_x:n_x + n_a]
    (mod_ref, wo_ref, ln1g_ref, ln1b_ref, win_ref, wout_ref, ln2g_ref, ln2b_ref, o_ref,
     act_ref) = refs[n_x + n_a:]
    d = D_MODEL
    i = pl.program_id(0)
    is_ctx = i < ctx_steps
    tm = act_ref.shape[0]
    mod = mod_ref[pl.ds(_mod_row(i, ctx_steps, tm, lat_len), 1), :]
    g1 = mod[:, 2 * d:3 * d]
    sh2 = mod[:, 3 * d:4 * d]
    sc2 = mod[:, 4 * d:5 * d]
    g2 = mod[:, 5 * d:6 * d]
    ln1 = (ln1g_ref[layer:layer + 1, :], ln1b_ref[layer:layer + 1, :])
    ln2 = (ln2g_ref[layer:layer + 1, :], ln2b_ref[layer:layer + 1, :])

    gr = tm // POST_ROW_GROUPS
    groups = [slice(r * gr, (r + 1) * gr) for r in range(POST_ROW_GROUPS)]
    if a_feature_major:
        mix = [_dot_tn(_pick(is_ctx, a_refs, (slice(None), rows)), wo_ref[...]) for rows in groups]
    else:
        mix = [_dot(_pick(is_ctx, a_refs, (rows, slice(None))), wo_ref[...]) for rows in groups]
    x1 = []
    for rows, mx in zip(groups, mix):
        x = _pick(is_ctx, x_refs, (rows, slice(None)))
        x1.append(_layer_norm(ALPHA * x + g1 * mx, *ln1))
        h = _bf(x1[-1] * (1.0 + sc2) + sh2)
        for c in range(D_FF // FFN_COLS):
            cols = slice(c * FFN_COLS, (c + 1) * FFN_COLS)
            gate = _dot(h, win_ref[:, cols])
            up = _dot(h, win_ref[:, D_FF + c * FFN_COLS:D_FF + (c + 1) * FFN_COLS])
            act_ref[rows, cols] = _bf(gate * jax.nn.sigmoid(gate) * up)
    for rows, xr in zip(groups, x1):
        f = _dot(act_ref[rows, :], wout_ref[...])
        o_ref[rows, :] = _layer_norm(ALPHA * xr + g2 * f, *ln2)


def _post(xs, mixes, n, n_ctx, lat_len, mods, layer, w_o, w_in, w_out, ln1_g, ln1_b, ln2_g, ln2_b, name,
          x_tok0=0, a_feature_major=False, casts=()):
    tm = POST_TOKENS
    ctx_steps = n_ctx // tm
    ctx = lambda i: _ctx_block(i, ctx_steps)
    lat = lambda i: _lat_block(i, ctx_steps)
    blocks = lambda m, tok0: (lambda i: tok0 // tm + i,) if m == 1 else (ctx, lat)
    tok = lambda f: pl.BlockSpec((tm, D_MODEL), lambda i: (f(i), 0))
    feat = lambda f: pl.BlockSpec((D_MODEL, tm), lambda i: (0, f(i)))
    const = lambda shape: pl.BlockSpec(shape, lambda i: (0, 0), pipeline_mode=pl.Buffered(1))
    vec = pl.BlockSpec((DEPTH, D_MODEL), lambda i: (0, 0))
    body = functools.partial(_post_kernel, layer=layer, ctx_steps=ctx_steps, lat_len=lat_len, n_x=len(xs),
                             n_a=len(mixes), a_feature_major=a_feature_major)
    return pl.pallas_call(
        _with_casts(body, len(xs) + len(mixes) + 8, 1, casts),
        out_shape=(jax.ShapeDtypeStruct((n, D_MODEL), F32),) + tuple(c.out_shape for c in casts),
        grid=(n // tm,),
        in_specs=[tok(f) for f in blocks(len(xs), x_tok0)]
        + [(feat if a_feature_major else tok)(f) for f in blocks(len(mixes), 0)]
        + [pl.BlockSpec((None, MOD_ROWS, 6 * D_MODEL), lambda i: (layer, 0, 0)),
           const((D_MODEL, D_MODEL)), vec, vec,
           const((D_MODEL, 2 * D_FF)), const((D_FF, D_MODEL)), vec, vec]
        + [c.in_spec for c in casts],
        out_specs=(tok(lambda i: i),) + tuple(c.out_spec for c in casts),
        scratch_shapes=[pltpu.VMEM((tm, D_FF), BF16)],
        compiler_params=_params("arbitrary"),
        name=name,
    )(*xs, *mixes, mods, w_o, ln1_g, ln1_b, w_in, w_out, ln2_g, ln2_b, *[c.w for c in casts])


def _qkv_kernel(x_ref, mod_ref, wt_ref, qkvt_ref, nk_ref, nv_ref, *, ctx_steps, lat_len):
    d = D_MODEL
    i = pl.program_id(0)
    tm = x_ref.shape[0]
    mod = mod_ref[pl.ds(_mod_row(i, ctx_steps, tm, lat_len), 1), :]
    h = _bf(x_ref[...] * (1.0 + mod[:, d:2 * d]) + mod[:, 0:d])
    qkvt_ref[0:d, :] = _bf(_dot_nt(wt_ref[0:d, :], h) * (HEAD_DIM ** -0.5 * LOG2E))
    kt = _dot_nt(wt_ref[d:2 * d, :], h)
    qkvt_ref[d:2 * d, :] = _bf(kt)
    vt = _dot_nt(wt_ref[2 * d:3 * d, :], h)
    qkvt_ref[2 * d:3 * d, :] = _bf(vt)

    @pl.when(i < ctx_steps)
    def _():
        nk_ref[0, 0] = kt.reshape(N_HEADS, HEAD_DIM, tm)
        nv_ref[0, 0] = vt.reshape(N_HEADS, HEAD_DIM, tm)


def _qkv(x2d, n_ctx, lat_len, mods, layer, w_qkv_t):
    n = x2d.shape[0]
    tm = QKV_TOKENS
    ctx_steps = n_ctx // tm
    kv_shape = jax.ShapeDtypeStruct((ctx_steps, 1, N_HEADS, HEAD_DIM, tm), F32)
    kv_spec = pl.BlockSpec((1, 1, N_HEADS, HEAD_DIM, tm), lambda i: (_ctx_block(i, ctx_steps), 0, 0, 0, 0))
    return pl.pallas_call(
        functools.partial(_qkv_kernel, ctx_steps=ctx_steps, lat_len=lat_len),
        out_shape=(jax.ShapeDtypeStruct((3 * D_MODEL, n), BF16), kv_shape, kv_shape),
        grid=(n // tm,),
        in_specs=[
            pl.BlockSpec((tm, D_MODEL), lambda i: (i, 0)),
            pl.BlockSpec((None, MOD_ROWS, 6 * D_MODEL), lambda i: (layer, 0, 0)),
            pl.BlockSpec((3 * D_MODEL, D_MODEL), lambda i: (0, 0), pipeline_mode=pl.Buffered(1)),
        ],
        out_specs=(pl.BlockSpec((3 * D_MODEL, tm), lambda i: (0, i)), kv_spec, kv_spec),
        compiler_params=_params("arbitrary"),
        name="qkv",
    )(x2d, mods, w_qkv_t)


def _softmax_weights(*scores):
    m = functools.reduce(jnp.maximum, [jnp.max(s, axis=0, keepdims=True) for s in scores])
    return [_bf(jnp.exp2(s - m)) for s in scores]


def _weighted_values(vt, p):
    ones = jnp.ones((BF16_ROWS, vt.shape[1]), BF16)
    return _dot(jnp.concatenate([vt, ones], axis=0), p)


def _normalise(acc):
    return _bf(acc[0:HEAD_DIM] * (1.0 / acc[HEAD_DIM:HEAD_DIM + 1]))


def _ctx_attn_kernel(qt_ref, kt_ref, vt_ref, ot_ref):
    def rows(hh):
        return slice(hh * HEAD_DIM, (hh + 1) * HEAD_DIM)

    def scores(hh):
        return _dot_tn(kt_ref[rows(hh), :], qt_ref[rows(hh), :])

    pending = [scores(hh) for hh in range(ATTN_LOOKAHEAD)]
    for hh in range(N_HEADS):
        if hh + ATTN_LOOKAHEAD < N_HEADS:
            pending.append(scores(hh + ATTN_LOOKAHEAD))
        (p,) = _softmax_weights(pending.pop(0))
        ot_ref[rows(hh), :] = _normalise(_weighted_values(vt_ref[rows(hh), :], p))


def _ctx_attn(qkvt, n_ctx, seq):
    return pl.pallas_call(
        _ctx_attn_kernel,
        out_shape=jax.ShapeDtypeStruct((D_MODEL, n_ctx), BF16),
        grid=(n_ctx // seq,),
        in_specs=[pl.BlockSpec((D_MODEL, seq), lambda b, c=c: (c, b)) for c in range(3)],
        out_specs=pl.BlockSpec((D_MODEL, seq), lambda b: (0, b)),
        compiler_params=_params("arbitrary"),
        name="ctx_attn",
    )(qkvt, qkvt, qkvt)


LAT_ROWS = 16
HALF_ROWS = LAT_ROWS // 2
LOCAL_ROWS = 12
HALF_KEY_ROW0 = (0, LAT_ROWS - LOCAL_ROWS)


def _fill_bias_table(rpb_ref, tab_ref):
    shape = (GRID_W, LANES)
    kc = lax.broadcasted_iota(jnp.int32, shape, 0)
    lane = lax.broadcasted_iota(jnp.int32, shape, 1)
    qc = lane % GRID_W
    win0 = jnp.clip(qc - WIN_W // 2, 0, GRID_W - WIN_W)
    col_ok = (kc >= win0) & (kc < win0 + WIN_W)
    masked = jnp.full(shape, MASK_VALUE, F32)
    tiles = []
    for dr in range(2 * WIN_H - 1):
        row = jnp.broadcast_to(rpb_ref[dr:dr + 1, :] * LOG2E, shape)
        t = pltpu.roll(row, LANES - (WIN_W - 1), 1, stride=1, stride_axis=0)
        t = jnp.where(lane < GRID_W, t, pltpu.roll(t, GRID_W, 1))
        tiles.append(jnp.where(col_ok, t, MASK_VALUE))

    def tile(qr, kr):
        row0 = min(max(qr - WIN_H // 2, 0), LAT_ROWS - WIN_H)
        if row0 <= kr < row0 + WIN_H:
            return tiles[kr - qr + WIN_H - 1]
        return masked

    for qr in range(0, LAT_ROWS, 2):
        kr0 = HALF_KEY_ROW0[qr // HALF_ROWS]
        for i in range(LOCAL_ROWS):
            both = jnp.where(lane < GRID_W, tile(qr, kr0 + i), tile(qr + 1, kr0 + i))
            tab_ref[i * GRID_W:(i + 1) * GRID_W, qr * GRID_W:(qr + 2) * GRID_W] = both


def _lat_attn_kernel(qt_ref, kt_ref, vt_ref, ckt_ref, cvt_ref, rpb_ref, ot_ref, tab_ref):
    for hh in range(tab_ref.shape[0]):
        _fill_bias_table(rpb_ref.at[hh], tab_ref.at[hh])
    seq = LAT_ROWS * GRID_W
    nq = HALF_ROWS * GRID_W
    nk = LOCAL_ROWS * GRID_W
    heads = qt_ref.shape[0] // HEAD_DIM
    units = [(b, hh, half) for b in range(qt_ref.shape[1] // seq) for hh in range(heads) for half in range(2)]

    def rows(hh):
        return slice(hh * HEAD_DIM, (hh + 1) * HEAD_DIM)

    def scores(unit):
        b, hh, half = unit
        key0 = b * seq + HALF_KEY_ROW0[half] * GRID_W
        qt = qt_ref[rows(hh), b * seq + half * nq:b * seq + (half + 1) * nq]
        s_loc = _dot_tn(kt_ref[rows(hh), key0:key0 + nk], qt) + tab_ref[hh, :, half * nq:(half + 1) * nq]
        s_ctx = _dot_tn(_bf(ckt_ref[b, hh]), qt)
        return s_loc, s_ctx

    pending = [scores(u) for u in units[:ATTN_LOOKAHEAD]]
    for i, (b, hh, half) in enumerate(units):
        if i + ATTN_LOOKAHEAD < len(units):
            pending.append(scores(units[i + ATTN_LOOKAHEAD]))
        s_loc, s_ctx = pending.pop(0)
        key0 = b * seq + HALF_KEY_ROW0[half] * GRID_W
        p_loc, p_ctx = _softmax_weights(s_loc, s_ctx)
        acc = (_weighted_values(vt_ref[rows(hh), key0:key0 + nk], p_loc)
               + _weighted_values(_bf(cvt_ref[b, hh]), p_ctx))
        ot_ref[rows(hh), b * seq + half * nq:b * seq + (half + 1) * nq] = _normalise(acc)


def _lat_attn(qkvt, n_ctx, cache_kt, cache_vt, rpb, layer_idx):
    rpb = jnp.pad(rpb[:, :, ::-1], ((0, 0), (0, 1), (0, LANES - (2 * WIN_W - 1))))
    n = qkvt.shape[1] - n_ctx
    seq = LAT_ROWS * GRID_W
    assert n_ctx % n == 0 and n % seq == 0
    nb = n // seq
    nhp = N_HEADS // 2
    past = cache_kt.shape[4]
    feat = lambda c: pl.BlockSpec((2 * HEAD_DIM, n), lambda hp, c=c: (c * nhp + hp, n_ctx // n))
    cache = pl.BlockSpec((nb, None, 2, HEAD_DIM, past), lambda hp: (0, layer_idx, hp, 0, 0))
    return pl.pallas_call(
        _lat_attn_kernel,
        out_shape=jax.ShapeDtypeStruct((D_MODEL, n), BF16),
        grid=(nhp,),
        in_specs=[feat(0), feat(1), feat(2), cache, cache,
                  pl.BlockSpec((2, 2 * WIN_H, LANES), lambda hp: (hp, 0, 0))],
        out_specs=pl.BlockSpec((2 * HEAD_DIM, n), lambda hp: (hp, 0)),
        scratch_shapes=[pltpu.VMEM((2, LOCAL_ROWS * GRID_W, seq), F32)],
        compiler_params=_params("arbitrary"),
        name="lat_attn",
    )(qkvt, qkvt, qkvt, cache_kt, cache_vt, rpb)


def kernel(x_prompt, x_sample, cache_k, cache_v, c, c_ctx, ada_w, ada_b, ln1_g, ln1_b, ln2_g, ln2_b, ffn_w_in, ffn_w_out, hy_w_in, hy_sconv_w, hy_sconv_b, hy_pe_w1, hy_pe_b1, hy_pe_w2, hy_pe_b2, hy_pe_w3, hy_sin_freq, hy_decay, hy_bias, hy_w_out, na_w_qkv, na_rpb, na_w_out):
    nbp, lp, d = x_prompt.shape
    nbs, ls, _ = x_sample.shape
    assert d == D_MODEL and ls == LAT_ROWS * GRID_W and lp == QKV_TOKENS and c.shape[0] == nbs
    xp = x_prompt.reshape(nbp * lp, d)
    xs = x_sample.reshape(nbs * ls, d)
    n_ctx, n_lat = xp.shape[0], xs.shape[0]
    n = n_ctx + n_lat
    ln = (ln1_g, ln1_b, ln2_g, ln2_b)

    mods = _ada(c_ctx, c, ada_w, ada_b)

    filt = (hy_pe_w1[0], hy_pe_b1[0], hy_pe_w2[0], hy_pe_b2[0], hy_pe_w3[0], hy_sin_freq[0], hy_decay[0])
    coef_p = _hyena_filter_spectrum(lp, *filt)
    coef_s = _hyena_filter_spectrum(ls, *filt)
    sconv_b = hy_sconv_b[0].reshape(1, -1)
    hy_bias0 = hy_bias[0].reshape(1, -1)
    x0, vx, ffn_w_in0 = _hy_in(xp, xs, lp, ls, mods, 0, _bf(hy_w_in[0]), hy_sconv_w[0], sconv_b,
                               casts=[_Cast(ffn_w_in, 0, n // ls, _step)])
    steps, lin = _hy_conv_steps(n_ctx)
    u_p, hy_w_out0, ffn_w_out0 = _hy_conv(vx, x0, 0, n_ctx, coef_p, hy_bias0, lp,
                                          casts=[_Cast(hy_w_out, 0, steps, lin), _Cast(ffn_w_out, 0, steps, lin)])
    steps, lin = _hy_conv_steps(n_lat)
    u_s, w_qkv_t = _hy_conv(vx, x0, n_ctx, n_lat, coef_s, hy_bias0, ls,
                            casts=[_Cast(na_w_qkv, 0, steps, lin, transpose=True)])
    steps = n // POST_TOKENS
    x1, ffn_w_in1, ffn_w_out1, na_w_out0 = _post(
        (xp, xs), (u_p, u_s), n, n_ctx, ls, mods, 0, hy_w_out0, ffn_w_in0, ffn_w_out0, *ln, name="post0",
        casts=[_Cast(ffn_w_in, 1, steps, _step), _Cast(ffn_w_out, 1, steps, _step),
               _Cast(na_w_out, 0, steps, _step)])

    qkvt, new_kt, new_vt = _qkv(x1, n_ctx, ls, mods, 1, w_qkv_t)
    at_p = _ctx_attn(qkvt, n_ctx, lp)
    at_s = _lat_attn(qkvt, n_ctx, jnp.swapaxes(cache_k, 3, 4), jnp.swapaxes(cache_v, 3, 4), na_rpb[0], 0)
    ffn1 = (na_w_out0, ffn_w_in1, ffn_w_out1, *ln)
    (yp,) = _post((x1,), (at_p,), n_ctx, n_ctx, ls, mods, 1, *ffn1, name="post1_ctx", a_feature_major=True)
    (ys,) = _post((x1,), (at_s,), n_lat, 0, ls, mods, 1, *ffn1, name="post1_lat", x_tok0=n_ctx,
                  a_feature_major=True)

    return (yp.reshape(nbp, lp, d), ys.reshape(nbs, ls, d),
            jnp.swapaxes(new_kt, 3, 4), jnp.swapaxes(new_vt, 3, 4))
```

```python
import functools
import math

import jax
import jax.numpy as jnp
import numpy as np
from jax import lax
from jax.experimental import pallas as pl
from jax.experimental.pallas import tpu as pltpu

D_MODEL = 1024
N_HEADS = 16
HEAD_DIM = D_MODEL // N_HEADS
D_FF = 2816
GRID_W = 64
WIN_H = 8
WIN_W = 16
N_BANDS = 16
PE_DIM = 1 + 2 * N_BANDS
FILT_W = 64
MOD_SHIFT = 0.05
DEPTH = 2
ALPHA = (2 * DEPTH) ** 0.25
LN_EPS = 1e-5
MASK_VALUE = -1e30
MOD_ROWS = 8

LANES = 128
SUBLANES = 8
BF16_ROWS = 2 * SUBLANES
ATTN_LOOKAHEAD = 2
LOG2E = math.log2(math.e)
VMEM_LIMIT = 56 * 1024 * 1024

ADA_COLS = 1536
FILTER_CHANNELS = 256
HY_IN_COLS = 256
HY_CONV_TOKENS = 1024
HY_CONV_CHANNELS = 512
POST_TOKENS = 512
POST_ROW_GROUPS = 2
FFN_COLS = 256
QKV_TOKENS = 256

F32 = jnp.float32
BF16 = jnp.bfloat16


def _bf(x):
    return x.astype(BF16)


def _dot(a, b):
    return jnp.dot(a, b, preferred_element_type=F32)


def _dot_nt(a, b):
    return lax.dot_general(a, b, (((1,), (1,)), ((), ())), preferred_element_type=F32)


def _dot_tn(a, b):
    return lax.dot_general(a, b, (((0,), (0,)), ((), ())), preferred_element_type=F32)


def _layer_norm(x, g, b):
    mu = jnp.mean(x, axis=-1, keepdims=True)
    xc = x - mu
    var = jnp.mean(xc * xc, axis=-1, keepdims=True)
    return xc * lax.rsqrt(var + LN_EPS) * g + b


def _params(*sem):
    return pltpu.CompilerParams(dimension_semantics=sem, vmem_limit_bytes=VMEM_LIMIT)


class _Cast:
    def __init__(self, w, layer, steps, step_fn, transpose=False):
        _, r, c = w.shape
        assert r % (steps * BF16_ROWS) == 0
        rows = r // steps
        self.w = w
        self.transpose = transpose
        self.in_spec = pl.BlockSpec((None, rows, c), lambda *g: (layer, step_fn(*g), 0))
        if transpose:
            self.out_shape = jax.ShapeDtypeStruct((c, r), BF16)
            self.out_spec = pl.BlockSpec((c, rows), lambda *g: (0, step_fn(*g)))
        else:
            self.out_shape = jax.ShapeDtypeStruct((r, c), BF16)
            self.out_spec = pl.BlockSpec((rows, c), lambda *g: (step_fn(*g), 0))


def _with_casts(body, n_in, n_out, casts):
    nc = len(casts)

    def kernel_fn(*refs):
        ins, src = refs[:n_in], refs[n_in:n_in + nc]
        outs, dst = refs[n_in + nc:n_in + nc + n_out], refs[n_in + nc + n_out:n_in + 2 * nc + n_out]
        for cast, s, d in zip(casts, src, dst):
            d[...] = _bf(s[...].T if cast.transpose else s[...])
        body(*ins, *outs, *refs[n_in + 2 * nc + n_out:])

    return kernel_fn


def _step(i):
    return i


def _ctx_block(i, ctx_steps):
    return jnp.minimum(i, ctx_steps - 1)


def _lat_block(i, ctx_steps):
    return jnp.maximum(i - ctx_steps, 0)


def _mod_row(i, ctx_steps, tm, lat_len):
    return jnp.where(i < ctx_steps, 0, 1 + ((i - ctx_steps) * tm) // lat_len)


def _pick(is_ctx, refs, idx):
    if len(refs) == 1:
        return refs[0][idx]
    return jnp.where(is_ctx, refs[0][idx], refs[1][idx])


@functools.lru_cache(maxsize=None)
def _plain_dft_tables(L):
    n = 2 * L
    f = np.arange(L)[:, None]
    s = np.arange(L)[None, :]
    ang = 2.0 * np.pi * ((f * s) % n) / n
    c = np.cos(ang)
    sn = np.sin(ang)
    sn[0, :] = np.where(np.arange(L) % 2 == 0, 1.0, -1.0)
    fwd = np.concatenate([c, sn], axis=0)
    w = np.full((2 * L, 1), 2.0 / n)
    w[0, 0] = 1.0 / n
    w[L, 0] = 1.0 / n
    inv = (fwd * w).T
    alt = np.where(np.arange(L) % 2 == 0, 1.0, -1.0).astype(np.float32)[:, None]
    return (np.ascontiguousarray(fwd).astype(np.float32), np.ascontiguousarray(inv).astype(np.float32), alt)


@functools.lru_cache(maxsize=None)
def _dft_tables(L):
    n = 2 * L
    half = L // 2
    f = np.arange(half)[:, None]
    m = np.arange(half)[None, :]
    alt = np.where(np.arange(half) % 2 == 0, 1.0, -1.0)
    w = np.full((L, 1), 2.0 / n)
    w[0, 0] = 1.0 / n
    w[half, 0] = 1.0 / n
    fwd, inv = [], []
    for parity in range(2):
        ang = 2.0 * np.pi * ((f * (2 * m + parity)) % n) / n
        sn = np.sin(ang)
        sn[0, :] = alt
        table = np.concatenate([np.cos(ang), sn], axis=0)
        fwd.append(np.ascontiguousarray(table).astype(np.float32))
        inv.append(np.ascontiguousarray((table * w).T).astype(np.float32))
    return fwd, inv


@functools.lru_cache(maxsize=None)
def _filter_dft_tables(L):
    n = 2 * L
    f = np.arange(L // 2)[:, None]
    t = np.arange(L)[None, :]
    ang = 2.0 * np.pi * ((f * t) % n) / n
    table = np.concatenate([np.cos(ang), np.sin(ang)], axis=0).astype(np.float32)
    tt = np.arange(L)
    cols = np.stack([np.where(tt % 2 == 0, 1.0, -1.0), np.cos(np.pi * (tt % 4) / 2.0),
                     np.sin(np.pi * (tt % 4) / 2.0)], axis=1)
    return np.ascontiguousarray(table), np.round(cols).astype(np.float32)


@functools.lru_cache(maxsize=None)
def _filter_features(L):
    t = np.linspace(0.0, 1.0, L, dtype=np.float32)[:, None]
    w = (np.float32(2.0 * math.pi / L) * np.arange(L, dtype=np.float32))[:, None]
    bands = np.linspace(1e-4, N_BANDS - 1, N_BANDS, dtype=np.float32)[None, :]
    arg = (bands * w).astype(np.float64)
    feats = np.concatenate([t.astype(np.float64), np.cos(arg), np.sin(arg)], axis=-1)
    out = np.zeros((L, LANES), np.float32)
    out[:, :PE_DIM] = feats
    return out, t


def _ada_kernel(cctx_ref, c_ref, w_ref, b_ref, o_ref, cvec_ref):
    nb = c_ref.shape[0]
    cvec_ref[...] = jnp.zeros_like(cvec_ref)
    cvec_ref[0:1, :] = cctx_ref[...]
    cvec_ref[1:1 + nb, :] = c_ref[...]
    c = cvec_ref[...]
    s = c * jax.nn.sigmoid(c)
    s_hi = _bf(s)
    s_lo = _bf(s - s_hi.astype(F32))
    w = w_ref[...]
    w_hi = _bf(w)
    w_lo = _bf(w - w_hi.astype(F32))
    bias = b_ref[pl.ds(pl.program_id(0), 1), :]
    o_ref[...] = _dot(s_hi, w_hi) + _dot(s_lo, w_hi) + _dot(s_hi, w_lo) + bias


def _ada(c_ctx, c, ada_w, ada_b):
    assert 1 + c.shape[0] <= MOD_ROWS
    tn = ADA_COLS
    nt = 6 * D_MODEL // tn
    return pl.pallas_call(
        _ada_kernel,
        out_shape=jax.ShapeDtypeStruct((DEPTH, MOD_ROWS, 6 * D_MODEL), F32),
        grid=(DEPTH, nt),
        in_specs=[
            pl.BlockSpec((1, D_MODEL), lambda l, j: (0, 0)),
            pl.BlockSpec(c.shape, lambda l, j: (0, 0)),
            pl.BlockSpec((None, D_MODEL, tn), lambda l, j: (l, 0, j)),
            pl.BlockSpec((DEPTH, tn), lambda l, j: (0, j)),
        ],
        out_specs=pl.BlockSpec((None, MOD_ROWS, tn), lambda l, j: (l, 0, j)),
        scratch_shapes=[pltpu.VMEM((MOD_ROWS, D_MODEL), F32)],
        compiler_params=_params("arbitrary", "arbitrary"),
        name="ada_mod",
    )(c_ctx.reshape(1, D_MODEL), c, ada_w, ada_b)


def _filter_kernel(feats_ref, t_ref, cols_ref, w1_ref, b1_ref, w2_ref, b2_ref, w3f_ref, w3b_ref, freq_ref,
                   decf_ref, decb_ref, tab_ref, out_ref, h_ref, *, L, split):
    hi = lax.Precision.HIGHEST

    @pl.when(pl.program_id(0) == 0)
    def _():
        h1 = jnp.sin(freq_ref[0:1, :] * (jnp.dot(feats_ref[...], w1_ref[...], precision=hi,
                                                 preferred_element_type=F32) + b1_ref[...]))
        h_ref[...] = jnp.sin(freq_ref[1:2, :] * (jnp.dot(h1, w2_ref[...], precision=hi,
                                                         preferred_element_type=F32) + b2_ref[...]))

    h = h_ref[...]
    t = t_ref[...]
    kf = jnp.dot(h, w3f_ref[...], precision=hi, preferred_element_type=F32) * (
        jnp.exp(-t * jnp.abs(decf_ref[...])) + MOD_SHIFT)
    kb = jnp.dot(h, w3b_ref[...], precision=hi, preferred_element_type=F32) * (
        jnp.exp(-t * jnp.abs(decb_ref[...])) + MOD_SHIFT)
    row = lax.broadcasted_iota(jnp.int32, kf.shape, 0)
    kb = jnp.where(row == 0, 0.0, kb)
    ksum = kf + kb
    kdiff = kf - kb
    alt = cols_ref[:, 0:1]

    def spectrum(table, x):
        hi = _bf(x)
        return _dot(table, hi) + _dot(table, _bf(x - hi.astype(F32)))

    if not split:
        k_c = spectrum(tab_ref[0:L, :], ksum)
        k_s = spectrum(tab_ref[L:2 * L, :], kdiff)
        k_nyq = jnp.sum(ksum * alt, axis=0, keepdims=True)
        out_ref[0] = k_c
        out_ref[1] = jnp.where(row == 0, 0.0, k_s)
        out_ref[2] = jnp.where(row == 0, k_nyq, k_c)
        return

    half = L // 2
    cos_m = tab_ref[0:half, :]
    sin_m = tab_ref[half:L, :]
    k_c = spectrum(cos_m, ksum)
    k_s = spectrum(sin_m, kdiff)
    u_c = spectrum(cos_m, alt * ksum)
    u_s = -spectrum(sin_m, alt * kdiff)
    kc_h = jnp.sum(ksum * cols_ref[:, 1:2], axis=0, keepdims=True)
    ks_h = jnp.sum(kdiff * cols_ref[:, 2:3], axis=0, keepdims=True)
    first = lax.broadcasted_iota(jnp.int32, k_c.shape, 0) == 0
    out_ref[0] = k_c
    out_ref[1] = k_s
    out_ref[2] = u_c
    out_ref[3] = u_s
    out_ref[4] = jnp.where(first, kc_h - ks_h, k_c)
    out_ref[5] = jnp.where(first, kc_h + ks_h, u_c)
    out_ref[6] = jnp.where(first, kc_h + ks_h, k_c)
    out_ref[7] = jnp.where(first, kc_h - ks_h, u_c)


HY_SPLIT_MIN_LEN = 512


def _filter_coef_shape(L):
    return (8, L // 2) if L >= HY_SPLIT_MIN_LEN else (3, L)


def _hyena_filter_spectrum(L, pe_w1, pe_b1, pe_w2, pe_b2, pe_w3, sin_freq, decay):
    split = L >= HY_SPLIT_MIN_LEN
    feats, t = _filter_features(L)
    if split:
        fwd, alt = _filter_dft_tables(L)
    else:
        fwd, _, alt = _plain_dft_tables(L)
    nc, nf = _filter_coef_shape(L)
    pad = LANES - FILT_W
    w1 = jnp.pad(pe_w1, ((0, LANES - PE_DIM), (0, pad)))
    b1 = jnp.pad(pe_b1, (0, pad)).reshape(1, LANES)
    w2 = jnp.pad(pe_w2, ((0, pad), (0, pad)))
    b2 = jnp.pad(pe_b2, (0, pad)).reshape(1, LANES)
    w3 = jnp.pad(pe_w3, ((0, pad), (0, 0)))
    freq = jnp.pad(sin_freq, ((0, 0), (0, pad)))
    dec = decay.reshape(1, 2 * D_MODEL)
    dt = FILTER_CHANNELS
    nt = D_MODEL // dt
    full = lambda shape: pl.BlockSpec(shape, lambda j: (0,) * len(shape))
    return pl.pallas_call(
        functools.partial(_filter_kernel, L=L, split=split),
        out_shape=jax.ShapeDtypeStruct((nc, nf, D_MODEL), F32),
        grid=(nt,),
        in_specs=[
            full((L, LANES)), full((L, 1)), full(alt.shape),
            full((LANES, LANES)), full((1, LANES)), full((LANES, LANES)), full((1, LANES)),
            pl.BlockSpec((LANES, dt), lambda j: (0, j)),
            pl.BlockSpec((LANES, dt), lambda j: (0, nt + j)),
            full((2, LANES)),
            pl.BlockSpec((1, dt), lambda j: (0, j)),
            pl.BlockSpec((1, dt), lambda j: (0, nt + j)),
            full(fwd.shape),
        ],
        out_specs=pl.BlockSpec((nc, nf, dt), lambda j: (0, 0, j)),
        scratch_shapes=[pltpu.VMEM((L, LANES), F32)],
        compiler_params=_params("arbitrary"),
        name=f"hyena_filter_{L}",
    )(feats, t, alt, w1, b1, w2, b2, w3, w3, freq, dec, dec, _bf(jnp.asarray(fwd)))


def _hy_in_kernel(xc_ref, xl_ref, mod_ref, w_ref, cw_ref, cb_ref, x0_ref, vx_ref, *, ctx_steps, ctx_len, cn):
    i = pl.program_id(0)
    is_ctx = i < ctx_steps
    tm = x0_ref.shape[0]
    mod = mod_ref[pl.ds(_mod_row(i, ctx_steps, tm, tm), 1), :]
    x = _pick(is_ctx, (xc_ref, xl_ref), (slice(None), slice(None)))
    h = _bf(x * (1.0 + mod[:, D_MODEL:2 * D_MODEL]) + mod[:, 0:D_MODEL])
    row = lax.broadcasted_iota(jnp.int32, (SUBLANES, cn), 0)

    def conv(col):
        z = _dot(h, w_ref[:, col:col + cn])
        zm = pltpu.roll(z, 1, 0)
        zp = pltpu.roll(z, tm - 1, 0)
        w0 = cw_ref[0:1, col:col + cn]
        w1 = cw_ref[1:2, col:col + cn]
        w2 = cw_ref[2:3, col:col + cn]
        b = cb_ref[:, col:col + cn]
        out = zm * w0 + z * w1 + zp * w2 + b
        pieces = []
        for s in range(tm // ctx_len):
            lo = s * ctx_len
            hi = lo + ctx_len - SUBLANES
            head = (jnp.where(row == 0, 0.0, zm[lo:lo + SUBLANES]) * w0 + z[lo:lo + SUBLANES] * w1
                    + zp[lo:lo + SUBLANES] * w2 + b)
            tail = (zm[hi:hi + SUBLANES] * w0 + z[hi:hi + SUBLANES] * w1
                    + jnp.where(row == SUBLANES - 1, 0.0, zp[hi:hi + SUBLANES]) * w2 + b)
            if lo > 0:
                head = jnp.where(is_ctx, head, out[lo:lo + SUBLANES])
            if hi + SUBLANES < tm:
                tail = jnp.where(is_ctx, tail, out[hi:hi + SUBLANES])
            pieces += [head, out[lo + SUBLANES:hi], tail]
        return jnp.concatenate(pieces, axis=0)

    for c in range(D_MODEL // cn):
        x0_ref[:, c * cn:(c + 1) * cn] = _bf(conv(c * cn))
        x1 = conv(D_MODEL + c * cn)
        v = conv(2 * D_MODEL + c * cn)
        vx_ref[:, c * cn:(c + 1) * cn] = _bf(v * x1)


def _hy_in(x_ctx, x_lat, ctx_len, lat_len, mods, layer, w_in, sconv_w, sconv_b, casts=()):
    tm = lat_len
    assert tm % ctx_len == 0 and x_ctx.shape[0] % tm == 0
    ctx_steps = x_ctx.shape[0] // tm
    n = x_ctx.shape[0] + x_lat.shape[0]
    tok = pl.BlockSpec((tm, D_MODEL), lambda i: (i, 0))
    const = lambda shape: pl.BlockSpec(shape, lambda i: (0, 0), pipeline_mode=pl.Buffered(1))
    act = jax.ShapeDtypeStruct((n, D_MODEL), BF16)
    body = functools.partial(_hy_in_kernel, ctx_steps=ctx_steps, ctx_len=ctx_len, cn=HY_IN_COLS)
    return pl.pallas_call(
        _with_casts(body, 6, 2, casts),
        out_shape=(act, act) + tuple(c.out_shape for c in casts),
        grid=(n // tm,),
        in_specs=[
            pl.BlockSpec((tm, D_MODEL), lambda i: (_ctx_block(i, ctx_steps), 0)),
            pl.BlockSpec((tm, D_MODEL), lambda i: (_lat_block(i, ctx_steps), 0)),
            pl.BlockSpec((None, MOD_ROWS, 6 * D_MODEL), lambda i: (layer, 0, 0)),
            const((D_MODEL, 3 * D_MODEL)),
            pl.BlockSpec((3, 3 * D_MODEL), lambda i: (0, 0)),
            pl.BlockSpec((1, 3 * D_MODEL), lambda i: (0, 0)),
        ] + [c.in_spec for c in casts],
        out_specs=(tok, tok) + tuple(c.out_spec for c in casts),
        compiler_params=_params("arbitrary"),
        name="hyena_in",
    )(x_ctx, x_lat, mods, w_in, sconv_w, sconv_b, *[c.w for c in casts])


def _hy_conv_plain_kernel(vx_ref, x0_ref, coef_ref, bias_ref, fwd_ref, inv_ref, u_ref, *, L, nseq, cn):
    for c in range(vx_ref.shape[1] // cn):
        cols = slice(c * cn, (c + 1) * cn)
        k_cos = coef_ref[0, :, cols]
        k_sin = coef_ref[1, :, cols]
        k_mix = coef_ref[2, :, cols]
        for s in range(nseq):
            rows = slice(s * L, (s + 1) * L)
            vx = vx_ref[rows, cols]
            spec = _dot(fwd_ref[...], vx)
            v_cos = spec[0:L]
            v_sin = spec[L:2 * L]
            y_spec = _bf(jnp.concatenate([v_cos * k_cos - v_sin * k_sin, v_cos * k_sin + v_sin * k_mix],
                                         axis=0))
            y = _dot(inv_ref[...], y_spec)
            u = (y + bias_ref[:, cols] * vx.astype(F32)) * x0_ref[rows, cols].astype(F32)
            u_ref[rows, cols] = _bf(u)


def _hy_conv_split_kernel(vx_ref, x0_ref, coef_ref, bias_ref, fe_ref, fo_ref, ge_ref, go_ref, u_ref,
                          sv_ref, sx_ref, su_ref, *, L, nseq, cn):
    half = L // 2
    for g in range(vx_ref.shape[1] // LANES):
        lanes = slice(g * LANES, (g + 1) * LANES)
        sv_ref[g] = vx_ref[:, lanes].astype(F32)
        sx_ref[g] = x0_ref[:, lanes].astype(F32)

    def load(ref, start, c):
        return jnp.concatenate([ref[g, pl.ds(start, half, stride=2), :]
                                for g in range(c * cn // LANES, (c + 1) * cn // LANES)], axis=1)

    def store(ref, start, c, val):
        for k, g in enumerate(range(c * cn // LANES, (c + 1) * cn // LANES)):
            ref[g, pl.ds(start, half, stride=2), :] = val[:, k * LANES:(k + 1) * LANES]

    for c in range(vx_ref.shape[1] // cn):
        cols = slice(c * cn, (c + 1) * cn)
        k_c, k_s, u_c, u_s, e2, e3, f2, f3 = [coef_ref[k, :, cols] for k in range(coef_ref.shape[0])]
        for s in range(nseq):
            v_e = load(sv_ref, s * L, c)
            v_o = load(sv_ref, s * L + 1, c)
            a = _dot(fe_ref[...], _bf(v_e))
            b = _dot(fo_ref[...], _bf(v_o))
            s_c, s_s = a[0:half] + b[0:half], a[half:L] + b[half:L]
            d_c, d_s = a[0:half] - b[0:half], a[half:L] - b[half:L]
            t1 = s_c * k_c - s_s * k_s
            t2 = d_c * u_c + d_s * u_s
            q1 = s_c * k_s
            q4 = d_c * u_s
            p_spec = jnp.concatenate([t1 + t2, q1 + s_s * e2 + d_s * e3 - q4], axis=0)
            m_spec = jnp.concatenate([t1 - t2, q1 + s_s * f2 - d_s * f3 + q4], axis=0)
            y_e = _dot(ge_ref[...], _bf(p_spec))
            y_o = _dot(go_ref[...], _bf(m_spec))
            bias = bias_ref[:, cols]
            store(su_ref, s * L, c, (y_e + bias * v_e) * load(sx_ref, s * L, c))
            store(su_ref, s * L + 1, c, (y_o + bias * v_o) * load(sx_ref, s * L + 1, c))
            rows = slice(s * L, (s + 1) * L)
            for g in range(c * cn // LANES, (c + 1) * cn // LANES):
                u_ref[rows, g * LANES:(g + 1) * LANES] = _bf(su_ref[g, rows, :])


def _hy_conv_steps(n):
    nt = n // HY_CONV_TOKENS
    return (D_MODEL // HY_CONV_CHANNELS) * nt, lambda j, i: j * nt + i


def _hy_conv(vx, x0, tok0, n, coef, bias, L, casts=()):
    tm = HY_CONV_TOKENS
    dt = HY_CONV_CHANNELS
    assert tok0 % tm == 0 and n % tm == 0 and tm % L == 0
    src = pl.BlockSpec((tm, dt), lambda j, i: (tok0 // tm + i, j))
    dst = pl.BlockSpec((tm, dt), lambda j, i: (i, j))
    const = lambda shape: pl.BlockSpec(shape, lambda j, i: (0, 0), pipeline_mode=pl.Buffered(1))
    if L >= HY_SPLIT_MIN_LEN:
        fwd, inv = _dft_tables(L)
        tables = (*fwd, *inv)
        body = functools.partial(_hy_conv_split_kernel, L=L, nseq=tm // L, cn=dt)
        scratch = [pltpu.VMEM((dt // LANES, tm, LANES), F32)] * 3
    else:
        tables = _plain_dft_tables(L)[:2]
        body = functools.partial(_hy_conv_plain_kernel, L=L, nseq=tm // L, cn=256)
        scratch = []
    return pl.pallas_call(
        _with_casts(body, 4 + len(tables), 1, casts),
        out_shape=(jax.ShapeDtypeStruct((n, D_MODEL), BF16),) + tuple(c.out_shape for c in casts),
        grid=(D_MODEL // dt, n // tm),
        in_specs=[
            src, src,
            pl.BlockSpec(_filter_coef_shape(L) + (dt,), lambda j, i: (0, 0, j)),
            pl.BlockSpec((1, dt), lambda j, i: (0, j)),
        ] + [const(t.shape) for t in tables] + [c.in_spec for c in casts],
        out_specs=(dst,) + tuple(c.out_spec for c in casts),
        scratch_shapes=scratch,
        compiler_params=_params("arbitrary", "arbitrary"),
        name=f"hyena_conv_{L}",
    )(vx, x0, coef, bias, *[_bf(jnp.asarray(t)) for t in tables], *[c.w for c in casts])


def _post_kernel(*refs, layer, ctx_steps, lat_len, n_x, n_a, a_feature_major):
    x_refs, a_refs = refs[:n_x], refs[n_x:n_x + n_a]
    (mod_ref, wo_ref, ln1g_ref, ln1b_ref, win_ref, wout_ref, ln2g_ref, ln2b_ref, o_ref,
     act_ref) = refs[n_x + n_a:]
    d = D_MODEL
    i = pl.program_id(0)
    is_ctx = i < ctx_steps
    tm = act_ref.shape[0]
    mod = mod_ref[pl.ds(_mod_row(i, ctx_steps, tm, lat_len), 1), :]
    g1 = mod[:, 2 * d:3 * d]
    sh2 = mod[:, 3 * d:4 * d]
    sc2 = mod[:, 4 * d:5 * d]
    g2 = mod[:, 5 * d:6 * d]
    ln1 = (ln1g_ref[layer:layer + 1, :], ln1b_ref[layer:layer + 1, :])
    ln2 = (ln2g_ref[layer:layer + 1, :], ln2b_ref[layer:layer + 1, :])

    gr = tm // POST_ROW_GROUPS
    groups = [slice(r * gr, (r + 1) * gr) for r in range(POST_ROW_GROUPS)]
    if a_feature_major:
        mix = [_dot_tn(_pick(is_ctx, a_refs, (slice(None), rows)), wo_ref[...]) for rows in groups]
    else:
        mix = [_dot(_pick(is_ctx, a_refs, (rows, slice(None))), wo_ref[...]) for rows in groups]
    x1 = []
    for rows, mx in zip(groups, mix):
        x = _pick(is_ctx, x_refs, (rows, slice(None)))
        x1.append(_layer_norm(ALPHA * x + g1 * mx, *ln1))
        h = _bf(x1[-1] * (1.0 + sc2) + sh2)
        for c in range(D_FF // FFN_COLS):
            cols = slice(c * FFN_COLS, (c + 1) * FFN_COLS)
            gate = _dot(h, win_ref[:, cols])
            up = _dot(h, win_ref[:, D_FF + c * FFN_COLS:D_FF + (c + 1) * FFN_COLS])
            act_ref[rows, cols] = _bf(gate * jax.nn.sigmoid(gate) * up)
    for rows, xr in zip(groups, x1):
        f = _dot(act_ref[rows, :], wout_ref[...])
        o_ref[rows, :] = _layer_norm(ALPHA * xr + g2 * f, *ln2)


def _post(xs, mixes, n, n_ctx, lat_len, mods, layer, w_o, w_in, w_out, ln1_g, ln1_b, ln2_g, ln2_b, name,
          x_tok0=0, a_feature_major=False, casts=()):
    tm = POST_TOKENS
    ctx_steps = n_ctx // tm
    ctx = lambda i: _ctx_block(i, ctx_steps)
    lat = lambda i: _lat_block(i, ctx_steps)
    blocks = lambda m, tok0: (lambda i: tok0 // tm + i,) if m == 1 else (ctx, lat)
    tok = lambda f: pl.BlockSpec((tm, D_MODEL), lambda i: (f(i), 0))
    feat = lambda f: pl.BlockSpec((D_MODEL, tm), lambda i: (0, f(i)))
    const = lambda shape: pl.BlockSpec(shape, lambda i: (0, 0), pipeline_mode=pl.Buffered(1))
    vec = pl.BlockSpec((DEPTH, D_MODEL), lambda i: (0, 0))
    body = functools.partial(_post_kernel, layer=layer, ctx_steps=ctx_steps, lat_len=lat_len, n_x=len(xs),
                             n_a=len(mixes), a_feature_major=a_feature_major)
    return pl.pallas_call(
        _with_casts(body, len(xs) + len(mixes) + 8, 1, casts),
        out_shape=(jax.ShapeDtypeStruct((n, D_MODEL), F32),) + tuple(c.out_shape for c in casts),
        grid=(n // tm,),
        in_specs=[tok(f) for f in blocks(len(xs), x_tok0)]
        + [(feat if a_feature_major else tok)(f) for f in blocks(len(mixes), 0)]
        + [pl.BlockSpec((None, MOD_ROWS, 6 * D_MODEL), lambda i: (layer, 0, 0)),
           const((D_MODEL, D_MODEL)), vec, vec,
           const((D_MODEL, 2 * D_FF)), const((D_FF, D_MODEL)), vec, vec]
        + [c.in_spec for c in casts],
        out_specs=(tok(lambda i: i),) + tuple(c.out_spec for c in casts),
        scratch_shapes=[pltpu.VMEM((tm, D_FF), BF16)],
        compiler_params=_params("arbitrary"),
        name=name,
    )(*xs, *mixes, mods, w_o, ln1_g, ln1_b, w_in, w_out, ln2_g, ln2_b, *[c.w for c in casts])


def _qkv_kernel(x_ref, mod_ref, wt_ref, qkvt_ref, nk_ref, nv_ref, *, ctx_steps, lat_len):
    d = D_MODEL
    i = pl.program_id(0)
    tm = x_ref.shape[0]
    mod = mod_ref[pl.ds(_mod_row(i, ctx_steps, tm, lat_len), 1), :]
    h = _bf(x_ref[...] * (1.0 + mod[:, d:2 * d]) + mod[:, 0:d])
    qkvt_ref[0:d, :] = _bf(_dot_nt(wt_ref[0:d, :], h) * (HEAD_DIM ** -0.5 * LOG2E))
    kt = _dot_nt(wt_ref[d:2 * d, :], h)
    qkvt_ref[d:2 * d, :] = _bf(kt)
    vt = _dot_nt(wt_ref[2 * d:3 * d, :], h)
    qkvt_ref[2 * d:3 * d, :] = _bf(vt)

    @pl.when(i < ctx_steps)
    def _():
        nk_ref[0, 0] = kt.reshape(N_HEADS, HEAD_DIM, tm)
        nv_ref[0, 0] = vt.reshape(N_HEADS, HEAD_DIM, tm)


def _qkv(x2d, n_ctx, lat_len, mods, layer, w_qkv_t):
    n = x2d.shape[0]
    tm = QKV_TOKENS
    ctx_steps = n_ctx // tm
    kv_shape = jax.ShapeDtypeStruct((ctx_steps, 1, N_HEADS, HEAD_DIM, tm), F32)
    kv_spec = pl.BlockSpec((1, 1, N_HEADS, HEAD_DIM, tm), lambda i: (_ctx_block(i, ctx_steps), 0, 0, 0, 0))
    return pl.pallas_call(
        functools.partial(_qkv_kernel, ctx_steps=ctx_steps, lat_len=lat_len),
        out_shape=(jax.ShapeDtypeStruct((3 * D_MODEL, n), BF16), kv_shape, kv_shape),
        grid=(n // tm,),
        in_specs=[
            pl.BlockSpec((tm, D_MODEL), lambda i: (i, 0)),
            pl.BlockSpec((None, MOD_ROWS, 6 * D_MODEL), lambda i: (layer, 0, 0)),
            pl.BlockSpec((3 * D_MODEL, D_MODEL), lambda i: (0, 0), pipeline_mode=pl.Buffered(1)),
        ],
        out_specs=(pl.BlockSpec((3 * D_MODEL, tm), lambda i: (0, i)), kv_spec, kv_spec),
        compiler_params=_params("arbitrary"),
        name="qkv",
    )(x2d, mods, w_qkv_t)


def _softmax_weights(*scores):
    m = functools.reduce(jnp.maximum, [jnp.max(s, axis=0, keepdims=True) for s in scores])
    return [_bf(jnp.exp2(s - m)) for s in scores]


def _weighted_values(vt, p):
    ones = jnp.ones((BF16_ROWS, vt.shape[1]), BF16)
    return _dot(jnp.concatenate([vt, ones], axis=0), p)


def _normalise(acc):
    return _bf(acc[0:HEAD_DIM] * (1.0 / acc[HEAD_DIM:HEAD_DIM + 1]))


def _ctx_attn_kernel(qt_ref, kt_ref, vt_ref, ot_ref):
    def rows(hh):
        return slice(hh * HEAD_DIM, (hh + 1) * HEAD_DIM)

    def scores(hh):
        return _dot_tn(kt_ref[rows(hh), :], qt_ref[rows(hh), :])

    pending = [scores(hh) for hh in range(ATTN_LOOKAHEAD)]
    for hh in range(N_HEADS):
        if hh + ATTN_LOOKAHEAD < N_HEADS:
            pending.append(scores(hh + ATTN_LOOKAHEAD))
        (p,) = _softmax_weights(pending.pop(0))
        ot_ref[rows(hh), :] = _normalise(_weighted_values(vt_ref[rows(hh), :], p))


def _ctx_attn(qkvt, n_ctx, seq):
    return pl.pallas_call(
        _ctx_attn_kernel,
        out_shape=jax.ShapeDtypeStruct((D_MODEL, n_ctx), BF16),
        grid=(n_ctx // seq,),
        in_specs=[pl.BlockSpec((D_MODEL, seq), lambda b, c=c: (c, b)) for c in range(3)],
        out_specs=pl.BlockSpec((D_MODEL, seq), lambda b: (0, b)),
        compiler_params=_params("arbitrary"),
        name="ctx_attn",
    )(qkvt, qkvt, qkvt)


LAT_ROWS = 16
HALF_ROWS = LAT_ROWS // 2
LOCAL_ROWS = 12
HALF_KEY_ROW0 = (0, LAT_ROWS - LOCAL_ROWS)


def _fill_bias_table(rpb_ref, tab_ref):
    shape = (GRID_W, LANES)
    kc = lax.broadcasted_iota(jnp.int32, shape, 0)
    lane = lax.broadcasted_iota(jnp.int32, shape, 1)
    qc = lane % GRID_W
    win0 = jnp.clip(qc - WIN_W // 2, 0, GRID_W - WIN_W)
    col_ok = (kc >= win0) & (kc < win0 + WIN_W)
    masked = jnp.full(shape, MASK_VALUE, F32)
    tiles = []
    for dr in range(2 * WIN_H - 1):
        row = jnp.broadcast_to(rpb_ref[dr:dr + 1, :] * LOG2E, shape)
        t = pltpu.roll(row, LANES - (WIN_W - 1), 1, stride=1, stride_axis=0)
        t = jnp.where(lane < GRID_W, t, pltpu.roll(t, GRID_W, 1))
        tiles.append(jnp.where(col_ok, t, MASK_VALUE))

    def tile(qr, kr):
        row0 = min(max(qr - WIN_H // 2, 0), LAT_ROWS - WIN_H)
        if row0 <= kr < row0 + WIN_H:
            return tiles[kr - qr + WIN_H - 1]
        return masked

    for qr in range(0, LAT_ROWS, 2):
        kr0 = HALF_KEY_ROW0[qr // HALF_ROWS]
        for i in range(LOCAL_ROWS):
            both = jnp.where(lane < GRID_W, tile(qr, kr0 + i), tile(qr + 1, kr0 + i))
            tab_ref[i * GRID_W:(i + 1) * GRID_W, qr * GRID_W:(qr + 2) * GRID_W] = both


def _lat_attn_kernel(qt_ref, kt_ref, vt_ref, ckt_ref, cvt_ref, rpb_ref, ot_ref, tab_ref):
    for hh in range(tab_ref.shape[0]):
        _fill_bias_table(rpb_ref.at[hh], tab_ref.at[hh])
    seq = LAT_ROWS * GRID_W
    nq = HALF_ROWS * GRID_W
    nk = LOCAL_ROWS * GRID_W
    heads = qt_ref.shape[0] // HEAD_DIM
    units = [(b, hh, half) for b in range(qt_ref.shape[1] // seq) for hh in range(heads) for half in range(2)]

    def rows(hh):
        return slice(hh * HEAD_DIM, (hh + 1) * HEAD_DIM)

    def scores(unit):
        b, hh, half = unit
        key0 = b * seq + HALF_KEY_ROW0[half] * GRID_W
        qt = qt_ref[rows(hh), b * seq + half * nq:b * seq + (half + 1) * nq]
        s_loc = _dot_tn(kt_ref[rows(hh), key0:key0 + nk], qt) + tab_ref[hh, :, half * nq:(half + 1) * nq]
        s_ctx = _dot_tn(_bf(ckt_ref[b, hh]), qt)
        return s_loc, s_ctx

    pending = [scores(u) for u in units[:ATTN_LOOKAHEAD]]
    for i, (b, hh, half) in enumerate(units):
        if i + ATTN_LOOKAHEAD < len(units):
            pending.append(scores(units[i + ATTN_LOOKAHEAD]))
        s_loc, s_ctx = pending.pop(0)
        key0 = b * seq + HALF_KEY_ROW0[half] * GRID_W
        p_loc, p_ctx = _softmax_weights(s_loc, s_ctx)
        acc = (_weighted_values(vt_ref[rows(hh), key0:key0 + nk], p_loc)
               + _weighted_values(_bf(cvt_ref[b, hh]), p_ctx))
        ot_ref[rows(hh), b * seq + half * nq:b * seq + (half + 1) * nq] = _normalise(acc)


def _lat_attn(qkvt, n_ctx, cache_kt, cache_vt, rpb, layer_idx):
    rpb = jnp.pad(rpb[:, :, ::-1], ((0, 0), (0, 1), (0, LANES - (2 * WIN_W - 1))))
    n = qkvt.shape[1] - n_ctx
    seq = LAT_ROWS * GRID_W
    assert n_ctx % n == 0 and n % seq == 0
    nb = n // seq
    nhp = N_HEADS // 2
    past = cache_kt.shape[4]
    feat = lambda c: pl.BlockSpec((2 * HEAD_DIM, n), lambda hp, c=c: (c * nhp + hp, n_ctx // n))
    cache = pl.BlockSpec((nb, None, 2, HEAD_DIM, past), lambda hp: (0, layer_idx, hp, 0, 0))
    return pl.pallas_call(
        _lat_attn_kernel,
        out_shape=jax.ShapeDtypeStruct((D_MODEL, n), BF16),
        grid=(nhp,),
        in_specs=[feat(0), feat(1), feat(2), cache, cache,
                  pl.BlockSpec((2, 2 * WIN_H, LANES), lambda hp: (hp, 0, 0))],
        out_specs=pl.BlockSpec((2 * HEAD_DIM, n), lambda hp: (hp, 0)),
        scratch_shapes=[pltpu.VMEM((2, LOCAL_ROWS * GRID_W, seq), F32)],
        compiler_params=_params("arbitrary"),
        name="lat_attn",
    )(qkvt, qkvt, qkvt, cache_kt, cache_vt, rpb)


def kernel(x_prompt, x_sample, cache_k, cache_v, c, c_ctx, ada_w, ada_b, ln1_g, ln1_b, ln2_g, ln2_b, ffn_w_in, ffn_w_out, hy_w_in, hy_sconv_w, hy_sconv_b, hy_pe_w1, hy_pe_b1, hy_pe_w2, hy_pe_b2, hy_pe_w3, hy_sin_freq, hy_decay, hy_bias, hy_w_out, na_w_qkv, na_rpb, na_w_out):
    nbp, lp, d = x_prompt.shape
    nbs, ls, _ = x_sample.shape
    assert d == D_MODEL and ls == LAT_ROWS * GRID_W and lp == QKV_TOKENS and c.shape[0] == nbs
    xp = x_prompt.reshape(nbp * lp, d)
    xs = x_sample.reshape(nbs * ls, d)
    n_ctx, n_lat = xp.shape[0], xs.shape[0]
    n = n_ctx + n_lat
    ln = (ln1_g, ln1_b, ln2_g, ln2_b)

    mods = _ada(c_ctx, c, ada_w, ada_b)

    filt = (hy_pe_w1[0], hy_pe_b1[0], hy_pe_w2[0], hy_pe_b2[0], hy_pe_w3[0], hy_sin_freq[0], hy_decay[0])
    coef_p = _hyena_filter_spectrum(lp, *filt)
    coef_s = _hyena_filter_spectrum(ls, *filt)
    sconv_b = hy_sconv_b[0].reshape(1, -1)
    hy_bias0 = hy_bias[0].reshape(1, -1)
    x0, vx, ffn_w_in0 = _hy_in(xp, xs, lp, ls, mods, 0, _bf(hy_w_in[0]), hy_sconv_w[0], sconv_b,
                               casts=[_Cast(ffn_w_in, 0, n // ls, _step)])
    steps, lin = _hy_conv_steps(n_ctx)
    u_p, hy_w_out0 = _hy_conv(vx, x0, 0, n_ctx, coef_p, hy_bias0, lp, casts=[_Cast(hy_w_out, 0, steps, lin)])
    steps, lin = _hy_conv_steps(n_lat)
    u_s, w_qkv_t, ffn_w_out0 = _hy_conv(vx, x0, n_ctx, n_lat, coef_s, hy_bias0, ls,
                                        casts=[_Cast(na_w_qkv, 0, steps, lin, transpose=True),
                                               _Cast(ffn_w_out, 0, steps, lin)])
    steps = n // POST_TOKENS
    x1, ffn_w_in1, ffn_w_out1, na_w_out0 = _post(
        (xp, xs), (u_p, u_s), n, n_ctx, ls, mods, 0, hy_w_out0, ffn_w_in0, ffn_w_out0, *ln, name="post0",
        casts=[_Cast(ffn_w_in, 1, steps, _step), _Cast(ffn_w_out, 1, steps, _step),
               _Cast(na_w_out, 0, steps, _step)])

    qkvt, new_kt, new_vt = _qkv(x1, n_ctx, ls, mods, 1, w_qkv_t)
    at_p = _ctx_attn(qkvt, n_ctx, lp)
    at_s = _lat_attn(qkvt, n_ctx, jnp.swapaxes(cache_k, 3, 4), jnp.swapaxes(cache_v, 3, 4), na_rpb[0], 0)
    ffn1 = (na_w_out0, ffn_w_in1, ffn_w_out1, *ln)
    (yp,) = _post((x1,), (at_p,), n_ctx, n_ctx, ls, mods, 1, *ffn1, name="post1_ctx", a_feature_major=True)
    (ys,) = _post((x1,), (at_s,), n_lat, 0, ls, mods, 1, *ffn1, name="post1_lat", x_tok0=n_ctx,
                  a_feature_major=True)

    return (yp.reshape(nbp, lp, d), ys.reshape(nbs, ls, d),
            jnp.swapaxes(new_kt, 3, 4), jnp.swapaxes(new_vt, 3, 4))
```

```python
import functools
import math

import jax
import jax.numpy as jnp
import numpy as np
from jax import lax
from jax.experimental import pallas as pl
from jax.experimental.pallas import tpu as pltpu

D_MODEL = 1024
N_HEADS = 16
HEAD_DIM = D_MODEL // N_HEADS
D_FF = 2816
GRID_W = 64
WIN_H = 8
WIN_W = 16
N_BANDS = 16
PE_DIM = 1 + 2 * N_BANDS
FILT_W = 64
MOD_SHIFT = 0.05
DEPTH = 2
ALPHA = (2 * DEPTH) ** 0.25
LN_EPS = 1e-5
MASK_VALUE = -1e30
MOD_ROWS = 8

LANES = 128
SUBLANES = 8
BF16_ROWS = 2 * SUBLANES
ATTN_LOOKAHEAD = 2
LOG2E = math.log2(math.e)
VMEM_LIMIT = 56 * 1024 * 1024

ADA_COLS = 1536
FILTER_CHANNELS = 256
HY_IN_COLS = 256
HY_CONV_TOKENS = 1024
HY_CONV_CHANNELS = 512
POST_TOKENS = 512
POST_ROW_GROUPS = 2
FFN_COLS = 256
QKV_TOKENS = 512

F32 = jnp.float32
BF16 = jnp.bfloat16


def _bf(x):
    return x.astype(BF16)


def _dot(a, b):
    return jnp.dot(a, b, preferred_element_type=F32)


def _dot_nt(a, b):
    return lax.dot_general(a, b, (((1,), (1,)), ((), ())), preferred_element_type=F32)


def _dot_tn(a, b):
    return lax.dot_general(a, b, (((0,), (0,)), ((), ())), preferred_element_type=F32)


def _layer_norm(x, g, b):
    mu = jnp.mean(x, axis=-1, keepdims=True)
    xc = x - mu
    var = jnp.mean(xc * xc, axis=-1, keepdims=True)
    return xc * lax.rsqrt(var + LN_EPS) * g + b


def _params(*sem):
    return pltpu.CompilerParams(dimension_semantics=sem, vmem_limit_bytes=VMEM_LIMIT)


class _Cast:
    def __init__(self, w, layer, steps, step_fn, transpose=False):
        _, r, c = w.shape
        assert r % (steps * BF16_ROWS) == 0
        rows = r // steps
        self.w = w
        self.transpose = transpose
        self.in_spec = pl.BlockSpec((None, rows, c), lambda *g: (layer, step_fn(*g), 0))
        if transpose:
            self.out_shape = jax.ShapeDtypeStruct((c, r), BF16)
            self.out_spec = pl.BlockSpec((c, rows), lambda *g: (0, step_fn(*g)))
        else:
            self.out_shape = jax.ShapeDtypeStruct((r, c), BF16)
            self.out_spec = pl.BlockSpec((rows, c), lambda *g: (step_fn(*g), 0))


def _with_casts(body, n_in, n_out, casts):
    nc = len(casts)

    def kernel_fn(*refs):
        ins, src = refs[:n_in], refs[n_in:n_in + nc]
        outs, dst = refs[n_in + nc:n_in + nc + n_out], refs[n_in + nc + n_out:n_in + 2 * nc + n_out]
        for cast, s, d in zip(casts, src, dst):
            d[...] = _bf(s[...].T if cast.transpose else s[...])
        body(*ins, *outs, *refs[n_in + 2 * nc + n_out:])

    return kernel_fn


def _step(i):
    return i


def _ctx_block(i, ctx_steps):
    return jnp.minimum(i, ctx_steps - 1)


def _lat_block(i, ctx_steps):
    return jnp.maximum(i - ctx_steps, 0)


def _mod_row(i, ctx_steps, tm, lat_len):
    return jnp.where(i < ctx_steps, 0, 1 + ((i - ctx_steps) * tm) // lat_len)


def _pick(is_ctx, refs, idx):
    if len(refs) == 1:
        return refs[0][idx]
    return jnp.where(is_ctx, refs[0][idx], refs[1][idx])


@functools.lru_cache(maxsize=None)
def _plain_dft_tables(L):
    n = 2 * L
    f = np.arange(L)[:, None]
    s = np.arange(L)[None, :]
    ang = 2.0 * np.pi * ((f * s) % n) / n
    c = np.cos(ang)
    sn = np.sin(ang)
    sn[0, :] = np.where(np.arange(L) % 2 == 0, 1.0, -1.0)
    fwd = np.concatenate([c, sn], axis=0)
    w = np.full((2 * L, 1), 2.0 / n)
    w[0, 0] = 1.0 / n
    w[L, 0] = 1.0 / n
    inv = (fwd * w).T
    alt = np.where(np.arange(L) % 2 == 0, 1.0, -1.0).astype(np.float32)[:, None]
    return (np.ascontiguousarray(fwd).astype(np.float32), np.ascontiguousarray(inv).astype(np.float32), alt)


@functools.lru_cache(maxsize=None)
def _dft_tables(L):
    n = 2 * L
    half = L // 2
    f = np.arange(half)[:, None]
    m = np.arange(half)[None, :]
    alt = np.where(np.arange(half) % 2 == 0, 1.0, -1.0)
    w = np.full((L, 1), 2.0 / n)
    w[0, 0] = 1.0 / n
    w[half, 0] = 1.0 / n
    fwd, inv = [], []
    for parity in range(2):
        ang = 2.0 * np.pi * ((f * (2 * m + parity)) % n) / n
        sn = np.sin(ang)
        sn[0, :] = alt
        table = np.concatenate([np.cos(ang), sn], axis=0)
        fwd.append(np.ascontiguousarray(table).astype(np.float32))
        inv.append(np.ascontiguousarray((table * w).T).astype(np.float32))
    return fwd, inv


@functools.lru_cache(maxsize=None)
def _filter_dft_tables(L):
    n = 2 * L
    f = np.arange(L // 2)[:, None]
    t = np.arange(L)[None, :]
    ang = 2.0 * np.pi * ((f * t) % n) / n
    table = np.concatenate([np.cos(ang), np.sin(ang)], axis=0).astype(np.float32)
    tt = np.arange(L)
    cols = np.stack([np.where(tt % 2 == 0, 1.0, -1.0), np.cos(np.pi * (tt % 4) / 2.0),
                     np.sin(np.pi * (tt % 4) / 2.0)], axis=1)
    return np.ascontiguousarray(table), np.round(cols).astype(np.float32)


@functools.lru_cache(maxsize=None)
def _filter_features(L):
    t = np.linspace(0.0, 1.0, L, dtype=np.float32)[:, None]
    w = (np.float32(2.0 * math.pi / L) * np.arange(L, dtype=np.float32))[:, None]
    bands = np.linspace(1e-4, N_BANDS - 1, N_BANDS, dtype=np.float32)[None, :]
    arg = (bands * w).astype(np.float64)
    feats = np.concatenate([t.astype(np.float64), np.cos(arg), np.sin(arg)], axis=-1)
    out = np.zeros((L, LANES), np.float32)
    out[:, :PE_DIM] = feats
    return out, t


def _ada_kernel(cctx_ref, c_ref, w_ref, b_ref, o_ref, cvec_ref):
    nb = c_ref.shape[0]
    cvec_ref[...] = jnp.zeros_like(cvec_ref)
    cvec_ref[0:1, :] = cctx_ref[...]
    cvec_ref[1:1 + nb, :] = c_ref[...]
    c = cvec_ref[...]
    s = c * jax.nn.sigmoid(c)
    s_hi = _bf(s)
    s_lo = _bf(s - s_hi.astype(F32))
    w = w_ref[...]
    w_hi = _bf(w)
    w_lo = _bf(w - w_hi.astype(F32))
    bias = b_ref[pl.ds(pl.program_id(0), 1), :]
    o_ref[...] = _dot(s_hi, w_hi) + _dot(s_lo, w_hi) + _dot(s_hi, w_lo) + bias


def _ada(c_ctx, c, ada_w, ada_b):
    assert 1 + c.shape[0] <= MOD_ROWS
    tn = ADA_COLS
    nt = 6 * D_MODEL // tn
    return pl.pallas_call(
        _ada_kernel,
        out_shape=jax.ShapeDtypeStruct((DEPTH, MOD_ROWS, 6 * D_MODEL), F32),
        grid=(DEPTH, nt),
        in_specs=[
            pl.BlockSpec((1, D_MODEL), lambda l, j: (0, 0)),
            pl.BlockSpec(c.shape, lambda l, j: (0, 0)),
            pl.BlockSpec((None, D_MODEL, tn), lambda l, j: (l, 0, j)),
            pl.BlockSpec((DEPTH, tn), lambda l, j: (0, j)),
        ],
        out_specs=pl.BlockSpec((None, MOD_ROWS, tn), lambda l, j: (l, 0, j)),
        scratch_shapes=[pltpu.VMEM((MOD_ROWS, D_MODEL), F32)],
        compiler_params=_params("arbitrary", "arbitrary"),
        name="ada_mod",
    )(c_ctx.reshape(1, D_MODEL), c, ada_w, ada_b)


def _filter_kernel(feats_ref, t_ref, cols_ref, w1_ref, b1_ref, w2_ref, b2_ref, w3f_ref, w3b_ref, freq_ref,
                   decf_ref, decb_ref, tab_ref, out_ref, h_ref, *, L, split):
    hi = lax.Precision.HIGHEST

    @pl.when(pl.program_id(0) == 0)
    def _():
        h1 = jnp.sin(freq_ref[0:1, :] * (jnp.dot(feats_ref[...], w1_ref[...], precision=hi,
                                                 preferred_element_type=F32) + b1_ref[...]))
        h_ref[...] = jnp.sin(freq_ref[1:2, :] * (jnp.dot(h1, w2_ref[...], precision=hi,
                                                         preferred_element_type=F32) + b2_ref[...]))

    h = h_ref[...]
    t = t_ref[...]
    kf = jnp.dot(h, w3f_ref[...], precision=hi, preferred_element_type=F32) * (
        jnp.exp(-t * jnp.abs(decf_ref[...])) + MOD_SHIFT)
    kb = jnp.dot(h, w3b_ref[...], precision=hi, preferred_element_type=F32) * (
        jnp.exp(-t * jnp.abs(decb_ref[...])) + MOD_SHIFT)
    row = lax.broadcasted_iota(jnp.int32, kf.shape, 0)
    kb = jnp.where(row == 0, 0.0, kb)
    ksum = kf + kb
    kdiff = kf - kb
    alt = cols_ref[:, 0:1]

    def spectrum(table, x):
        hi = _bf(x)
        return _dot(table, hi) + _dot(table, _bf(x - hi.astype(F32)))

    if not split:
        k_c = spectrum(tab_ref[0:L, :], ksum)
        k_s = spectrum(tab_ref[L:2 * L, :], kdiff)
        k_nyq = jnp.sum(ksum * alt, axis=0, keepdims=True)
        out_ref[0] = k_c
        out_ref[1] = jnp.where(row == 0, 0.0, k_s)
        out_ref[2] = jnp.where(row == 0, k_nyq, k_c)
        return

    half = L // 2
    cos_m = tab_ref[0:half, :]
    sin_m = tab_ref[half:L, :]
    k_c = spectrum(cos_m, ksum)
    k_s = spectrum(sin_m, kdiff)
    u_c = spectrum(cos_m, alt * ksum)
    u_s = -spectrum(sin_m, alt * kdiff)
    kc_h = jnp.sum(ksum * cols_ref[:, 1:2], axis=0, keepdims=True)
    ks_h = jnp.sum(kdiff * cols_ref[:, 2:3], axis=0, keepdims=True)
    first = lax.broadcasted_iota(jnp.int32, k_c.shape, 0) == 0
    out_ref[0] = k_c
    out_ref[1] = k_s
    out_ref[2] = u_c
    out_ref[3] = u_s
    out_ref[4] = jnp.where(first, kc_h - ks_h, k_c)
    out_ref[5] = jnp.where(first, kc_h + ks_h, u_c)
    out_ref[6] = jnp.where(first, kc_h + ks_h, k_c)
    out_ref[7] = jnp.where(first, kc_h - ks_h, u_c)


HY_SPLIT_MIN_LEN = 512


def _filter_coef_shape(L):
    return (8, L // 2) if L >= HY_SPLIT_MIN_LEN else (3, L)


def _hyena_filter_spectrum(L, pe_w1, pe_b1, pe_w2, pe_b2, pe_w3, sin_freq, decay):
    split = L >= HY_SPLIT_MIN_LEN
    feats, t = _filter_features(L)
    if split:
        fwd, alt = _filter_dft_tables(L)
    else:
        fwd, _, alt = _plain_dft_tables(L)
    nc, nf = _filter_coef_shape(L)
    pad = LANES - FILT_W
    w1 = jnp.pad(pe_w1, ((0, LANES - PE_DIM), (0, pad)))
    b1 = jnp.pad(pe_b1, (0, pad)).reshape(1, LANES)
    w2 = jnp.pad(pe_w2, ((0, pad), (0, pad)))
    b2 = jnp.pad(pe_b2, (0, pad)).reshape(1, LANES)
    w3 = jnp.pad(pe_w3, ((0, pad), (0, 0)))
    freq = jnp.pad(sin_freq, ((0, 0), (0, pad)))
    dec = decay.reshape(1, 2 * D_MODEL)
    dt = FILTER_CHANNELS
    nt = D_MODEL // dt
    full = lambda shape: pl.BlockSpec(shape, lambda j: (0,) * len(shape))
    return pl.pallas_call(
        functools.partial(_filter_kernel, L=L, split=split),
        out_shape=jax.ShapeDtypeStruct((nc, nf, D_MODEL), F32),
        grid=(nt,),
        in_specs=[
            full((L, LANES)), full((L, 1)), full(alt.shape),
            full((LANES, LANES)), full((1, LANES)), full((LANES, LANES)), full((1, LANES)),
            pl.BlockSpec((LANES, dt), lambda j: (0, j)),
            pl.BlockSpec((LANES, dt), lambda j: (0, nt + j)),
            full((2, LANES)),
            pl.BlockSpec((1, dt), lambda j: (0, j)),
            pl.BlockSpec((1, dt), lambda j: (0, nt + j)),
            full(fwd.shape),
        ],
        out_specs=pl.BlockSpec((nc, nf, dt), lambda j: (0, 0, j)),
        scratch_shapes=[pltpu.VMEM((L, LANES), F32)],
        compiler_params=_params("arbitrary"),
        name=f"hyena_filter_{L}",
    )(feats, t, alt, w1, b1, w2, b2, w3, w3, freq, dec, dec, _bf(jnp.asarray(fwd)))


def _hy_in_kernel(xc_ref, xl_ref, mod_ref, w_ref, cw_ref, cb_ref, x0_ref, vx_ref, *, ctx_steps, ctx_len, cn):
    i = pl.program_id(0)
    is_ctx = i < ctx_steps
    tm = x0_ref.shape[0]
    mod = mod_ref[pl.ds(_mod_row(i, ctx_steps, tm, tm), 1), :]
    x = _pick(is_ctx, (xc_ref, xl_ref), (slice(None), slice(None)))
    h = _bf(x * (1.0 + mod[:, D_MODEL:2 * D_MODEL]) + mod[:, 0:D_MODEL])
    row = lax.broadcasted_iota(jnp.int32, (SUBLANES, cn), 0)

    def conv(col):
        z = _dot(h, w_ref[:, col:col + cn])
        zm = pltpu.roll(z, 1, 0)
        zp = pltpu.roll(z, tm - 1, 0)
        w0 = cw_ref[0:1, col:col + cn]
        w1 = cw_ref[1:2, col:col + cn]
        w2 = cw_ref[2:3, col:col + cn]
        b = cb_ref[:, col:col + cn]
        out = zm * w0 + z * w1 + zp * w2 + b
        pieces = []
        for s in range(tm // ctx_len):
            lo = s * ctx_len
            hi = lo + ctx_len - SUBLANES
            head = (jnp.where(row == 0, 0.0, zm[lo:lo + SUBLANES]) * w0 + z[lo:lo + SUBLANES] * w1
                    + zp[lo:lo + SUBLANES] * w2 + b)
            tail = (zm[hi:hi + SUBLANES] * w0 + z[hi:hi + SUBLANES] * w1
                    + jnp.where(row == SUBLANES - 1, 0.0, zp[hi:hi + SUBLANES]) * w2 + b)
            if lo > 0:
                head = jnp.where(is_ctx, head, out[lo:lo + SUBLANES])
            if hi + SUBLANES < tm:
                tail = jnp.where(is_ctx, tail, out[hi:hi + SUBLANES])
            pieces += [head, out[lo + SUBLANES:hi], tail]
        return jnp.concatenate(pieces, axis=0)

    for c in range(D_MODEL // cn):
        x0_ref[:, c * cn:(c + 1) * cn] = _bf(conv(c * cn))
        x1 = conv(D_MODEL + c * cn)
        v = conv(2 * D_MODEL + c * cn)
        vx_ref[:, c * cn:(c + 1) * cn] = _bf(v * x1)


def _hy_in(x_ctx, x_lat, ctx_len, lat_len, mods, layer, w_in, sconv_w, sconv_b, casts=()):
    tm = lat_len
    assert tm % ctx_len == 0 and x_ctx.shape[0] % tm == 0
    ctx_steps = x_ctx.shape[0] // tm
    n = x_ctx.shape[0] + x_lat.shape[0]
    tok = pl.BlockSpec((tm, D_MODEL), lambda i: (i, 0))
    const = lambda shape: pl.BlockSpec(shape, lambda i: (0, 0), pipeline_mode=pl.Buffered(1))
    act = jax.ShapeDtypeStruct((n, D_MODEL), BF16)
    body = functools.partial(_hy_in_kernel, ctx_steps=ctx_steps, ctx_len=ctx_len, cn=HY_IN_COLS)
    return pl.pallas_call(
        _with_casts(body, 6, 2, casts),
        out_shape=(act, act) + tuple(c.out_shape for c in casts),
        grid=(n // tm,),
        in_specs=[
            pl.BlockSpec((tm, D_MODEL), lambda i: (_ctx_block(i, ctx_steps), 0)),
            pl.BlockSpec((tm, D_MODEL), lambda i: (_lat_block(i, ctx_steps), 0)),
            pl.BlockSpec((None, MOD_ROWS, 6 * D_MODEL), lambda i: (layer, 0, 0)),
            const((D_MODEL, 3 * D_MODEL)),
            pl.BlockSpec((3, 3 * D_MODEL), lambda i: (0, 0)),
            pl.BlockSpec((1, 3 * D_MODEL), lambda i: (0, 0)),
        ] + [c.in_spec for c in casts],
        out_specs=(tok, tok) + tuple(c.out_spec for c in casts),
        compiler_params=_params("arbitrary"),
        name="hyena_in",
    )(x_ctx, x_lat, mods, w_in, sconv_w, sconv_b, *[c.w for c in casts])


def _hy_conv_plain_kernel(vx_ref, x0_ref, coef_ref, bias_ref, fwd_ref, inv_ref, u_ref, *, L, nseq, cn):
    for c in range(vx_ref.shape[1] // cn):
        cols = slice(c * cn, (c + 1) * cn)
        k_cos = coef_ref[0, :, cols]
        k_sin = coef_ref[1, :, cols]
        k_mix = coef_ref[2, :, cols]
        for s in range(nseq):
            rows = slice(s * L, (s + 1) * L)
            vx = vx_ref[rows, cols]
            spec = _dot(fwd_ref[...], vx)
            v_cos = spec[0:L]
            v_sin = spec[L:2 * L]
            y_spec = _bf(jnp.concatenate([v_cos * k_cos - v_sin * k_sin, v_cos * k_sin + v_sin * k_mix],
                                         axis=0))
            y = _dot(inv_ref[...], y_spec)
            u = (y + bias_ref[:, cols] * vx.astype(F32)) * x0_ref[rows, cols].astype(F32)
            u_ref[rows, cols] = _bf(u)


def _hy_conv_split_kernel(vx_ref, x0_ref, coef_ref, bias_ref, fe_ref, fo_ref, ge_ref, go_ref, u_ref,
                          sv_ref, sx_ref, su_ref, *, L, nseq, cn):
    half = L // 2
    for g in range(vx_ref.shape[1] // LANES):
        lanes = slice(g * LANES, (g + 1) * LANES)
        sv_ref[g] = vx_ref[:, lanes].astype(F32)
        sx_ref[g] = x0_ref[:, lanes].astype(F32)

    def load(ref, start, c):
        return jnp.concatenate([ref[g, pl.ds(start, half, stride=2), :]
                                for g in range(c * cn // LANES, (c + 1) * cn // LANES)], axis=1)

    def store(ref, start, c, val):
        for k, g in enumerate(range(c * cn // LANES, (c + 1) * cn // LANES)):
            ref[g, pl.ds(start, half, stride=2), :] = val[:, k * LANES:(k + 1) * LANES]

    for c in range(vx_ref.shape[1] // cn):
        cols = slice(c * cn, (c + 1) * cn)
        k_c, k_s, u_c, u_s, e2, e3, f2, f3 = [coef_ref[k, :, cols] for k in range(coef_ref.shape[0])]
        for s in range(nseq):
            v_e = load(sv_ref, s * L, c)
            v_o = load(sv_ref, s * L + 1, c)
            a = _dot(fe_ref[...], _bf(v_e))
            b = _dot(fo_ref[...], _bf(v_o))
            s_c, s_s = a[0:half] + b[0:half], a[half:L] + b[half:L]
            d_c, d_s = a[0:half] - b[0:half], a[half:L] - b[half:L]
            t1 = s_c * k_c - s_s * k_s
            t2 = d_c * u_c + d_s * u_s
            q1 = s_c * k_s
            q4 = d_c * u_s
            p_spec = jnp.concatenate([t1 + t2, q1 + s_s * e2 + d_s * e3 - q4], axis=0)
            m_spec = jnp.concatenate([t1 - t2, q1 + s_s * f2 - d_s * f3 + q4], axis=0)
            y_e = _dot(ge_ref[...], _bf(p_spec))
            y_o = _dot(go_ref[...], _bf(m_spec))
            bias = bias_ref[:, cols]
            store(su_ref, s * L, c, (y_e + bias * v_e) * load(sx_ref, s * L, c))
            store(su_ref, s * L + 1, c, (y_o + bias * v_o) * load(sx_ref, s * L + 1, c))
            rows = slice(s * L, (s + 1) * L)
            for g in range(c * cn // LANES, (c + 1) * cn // LANES):
                u_ref[rows, g * LANES:(g + 1) * LANES] = _bf(su_ref[g, rows, :])


def _hy_conv_steps(n):
    nt = n // HY_CONV_TOKENS
    return (D_MODEL // HY_CONV_CHANNELS) * nt, lambda j, i: j * nt + i


def _hy_conv(vx, x0, tok0, n, coef, bias, L, casts=()):
    tm = HY_CONV_TOKENS
    dt = HY_CONV_CHANNELS
    assert tok0 % tm == 0 and n % tm == 0 and tm % L == 0
    src = pl.BlockSpec((tm, dt), lambda j, i: (tok0 // tm + i, j))
    dst = pl.BlockSpec((tm, dt), lambda j, i: (i, j))
    const = lambda shape: pl.BlockSpec(shape, lambda j, i: (0, 0), pipeline_mode=pl.Buffered(1))
    if L >= HY_SPLIT_MIN_LEN:
        fwd, inv = _dft_tables(L)
        tables = (*fwd, *inv)
        body = functools.partial(_hy_conv_split_kernel, L=L, nseq=tm // L, cn=dt)
        scratch = [pltpu.VMEM((dt // LANES, tm, LANES), F32)] * 3
    else:
        tables = _plain_dft_tables(L)[:2]
        body = functools.partial(_hy_conv_plain_kernel, L=L, nseq=tm // L, cn=256)
        scratch = []
    return pl.pallas_call(
        _with_casts(body, 4 + len(tables), 1, casts),
        out_shape=(jax.ShapeDtypeStruct((n, D_MODEL), BF16),) + tuple(c.out_shape for c in casts),
        grid=(D_MODEL // dt, n // tm),
        in_specs=[
            src, src,
            pl.BlockSpec(_filter_coef_shape(L) + (dt,), lambda j, i: (0, 0, j)),
            pl.BlockSpec((1, dt), lambda j, i: (0, j)),
        ] + [const(t.shape) for t in tables] + [c.in_spec for c in casts],
        out_specs=(dst,) + tuple(c.out_spec for c in casts),
        scratch_shapes=scratch,
        compiler_params=_params("arbitrary", "arbitrary"),
        name=f"hyena_conv_{L}",
    )(vx, x0, coef, bias, *[_bf(jnp.asarray(t)) for t in tables], *[c.w for c in casts])


def _post_kernel(*refs, layer, ctx_steps, lat_len, n_x, n_a, a_feature_major):
    x_refs, a_refs = refs[:n_x], refs[n_x:n_x + n_a]
    (mod_ref, wo_ref, ln1g_ref, ln1b_ref, win_ref, wout_ref, ln2g_ref, ln2b_ref, o_ref,
     act_ref) = refs[n_x + n_a:]
    d = D_MODEL
    i = pl.program_id(0)
    is_ctx = i < ctx_steps
    tm = act_ref.shape[0]
    mod = mod_ref[pl.ds(_mod_row(i, ctx_steps, tm, lat_len), 1), :]
    g1 = mod[:, 2 * d:3 * d]
    sh2 = mod[:, 3 * d:4 * d]
    sc2 = mod[:, 4 * d:5 * d]
    g2 = mod[:, 5 * d:6 * d]
    ln1 = (ln1g_ref[layer:layer + 1, :], ln1b_ref[layer:layer + 1, :])
    ln2 = (ln2g_ref[layer:layer + 1, :], ln2b_ref[layer:layer + 1, :])

    gr = tm // POST_ROW_GROUPS
    groups = [slice(r * gr, (r + 1) * gr) for r in range(POST_ROW_GROUPS)]
    if a_feature_major:
        mix = [_dot_tn(_pick(is_ctx, a_refs, (slice(None), rows)), wo_ref[...]) for rows in groups]
    else:
        mix = [_dot(_pick(is_ctx, a_refs, (rows, slice(None))), wo_ref[...]) for rows in groups]
    x1 = []
    for rows, mx in zip(groups, mix):
        x = _pick(is_ctx, x_refs, (rows, slice(None)))
        x1.append(_layer_norm(ALPHA * x + g1 * mx, *ln1))
        h = _bf(x1[-1] * (1.0 + sc2) + sh2)
        for c in range(D_FF // FFN_COLS):
            cols = slice(c * FFN_COLS, (c + 1) * FFN_COLS)
            gate = _dot(h, win_ref[:, cols])
            up = _dot(h, win_ref[:, D_FF + c * FFN_COLS:D_FF + (c + 1) * FFN_COLS])
            act_ref[rows, cols] = _bf(gate * jax.nn.sigmoid(gate) * up)
    for rows, xr in zip(groups, x1):
        f = _dot(act_ref[rows, :], wout_ref[...])
        o_ref[rows, :] = _layer_norm(ALPHA * xr + g2 * f, *ln2)


def _post(xs, mixes, n, n_ctx, lat_len, mods, layer, w_o, w_in, w_out, ln1_g, ln1_b, ln2_g, ln2_b, name,
          x_tok0=0, a_feature_major=False, casts=()):
    tm = POST_TOKENS
    ctx_steps = n_ctx // tm
    ctx = lambda i: _ctx_block(i, ctx_steps)
    lat = lambda i: _lat_block(i, ctx_steps)
    blocks = lambda m, tok0: (lambda i: tok0 // tm + i,) if m == 1 else (ctx, lat)
    tok = lambda f: pl.BlockSpec((tm, D_MODEL), lambda i: (f(i), 0))
    feat = lambda f: pl.BlockSpec((D_MODEL, tm), lambda i: (0, f(i)))
    const = lambda shape: pl.BlockSpec(shape, lambda i: (0, 0), pipeline_mode=pl.Buffered(1))
    vec = pl.BlockSpec((DEPTH, D_MODEL), lambda i: (0, 0))
    body = functools.partial(_post_kernel, layer=layer, ctx_steps=ctx_steps, lat_len=lat_len, n_x=len(xs),
                             n_a=len(mixes), a_feature_major=a_feature_major)
    return pl.pallas_call(
        _with_casts(body, len(xs) + len(mixes) + 8, 1, casts),
        out_shape=(jax.ShapeDtypeStruct((n, D_MODEL), F32),) + tuple(c.out_shape for c in casts),
        grid=(n // tm,),
        in_specs=[tok(f) for f in blocks(len(xs), x_tok0)]
        + [(feat if a_feature_major else tok)(f) for f in blocks(len(mixes), 0)]
        + [pl.BlockSpec((None, MOD_ROWS, 6 * D_MODEL), lambda i: (layer, 0, 0)),
           const((D_MODEL, D_MODEL)), vec, vec,
           const((D_MODEL, 2 * D_FF)), const((D_FF, D_MODEL)), vec, vec]
        + [c.in_spec for c in casts],
        out_specs=(tok(lambda i: i),) + tuple(c.out_spec for c in casts),
        scratch_shapes=[pltpu.VMEM((tm, D_FF), BF16)],
        compiler_params=_params("arbitrary"),
        name=name,
    )(*xs, *mixes, mods, w_o, ln1_g, ln1_b, w_in, w_out, ln2_g, ln2_b, *[c.w for c in casts])


def _qkv_kernel(x_ref, mod_ref, wt_ref, qkvt_ref, nk_ref, nv_ref, *, ctx_steps, lat_len):
    d = D_MODEL
    i = pl.program_id(0)
    tm = x_ref.shape[0]
    mod = mod_ref[pl.ds(_mod_row(i, ctx_steps, tm, lat_len), 1), :]
    h = _bf(x_ref[...] * (1.0 + mod[:, d:2 * d]) + mod[:, 0:d])
    qkvt_ref[0:d, :] = _bf(_dot_nt(wt_ref[0:d, :], h) * (HEAD_DIM ** -0.5 * LOG2E))
    kt = _dot_nt(wt_ref[d:2 * d, :], h)
    qkvt_ref[d:2 * d, :] = _bf(kt)
    vt = _dot_nt(wt_ref[2 * d:3 * d, :], h)
    qkvt_ref[2 * d:3 * d, :] = _bf(vt)

    @pl.when(i < ctx_steps)
    def _():
        seq = nk_ref.shape[4]
        for s in range(tm // seq):
            nk_ref[s, 0] = kt[:, s * seq:(s + 1) * seq].reshape(N_HEADS, HEAD_DIM, seq)
            nv_ref[s, 0] = vt[:, s * seq:(s + 1) * seq].reshape(N_HEADS, HEAD_DIM, seq)


def _qkv(x2d, n_ctx, ctx_len, lat_len, mods, layer, w_qkv_t):
    n = x2d.shape[0]
    tm = QKV_TOKENS
    assert tm % ctx_len == 0 and n_ctx % tm == 0
    ctx_steps = n_ctx // tm
    kv_shape = jax.ShapeDtypeStruct((n_ctx // ctx_len, 1, N_HEADS, HEAD_DIM, ctx_len), F32)
    kv_spec = pl.BlockSpec((tm // ctx_len, 1, N_HEADS, HEAD_DIM, ctx_len),
                           lambda i: (_ctx_block(i, ctx_steps), 0, 0, 0, 0))
    return pl.pallas_call(
        functools.partial(_qkv_kernel, ctx_steps=ctx_steps, lat_len=lat_len),
        out_shape=(jax.ShapeDtypeStruct((3 * D_MODEL, n), BF16), kv_shape, kv_shape),
        grid=(n // tm,),
        in_specs=[
            pl.BlockSpec((tm, D_MODEL), lambda i: (i, 0)),
            pl.BlockSpec((None, MOD_ROWS, 6 * D_MODEL), lambda i: (layer, 0, 0)),
            pl.BlockSpec((3 * D_MODEL, D_MODEL), lambda i: (0, 0), pipeline_mode=pl.Buffered(1)),
        ],
        out_specs=(pl.BlockSpec((3 * D_MODEL, tm), lambda i: (0, i)), kv_spec, kv_spec),
        compiler_params=_params("arbitrary"),
        name="qkv",
    )(x2d, mods, w_qkv_t)


def _softmax_weights(*scores):
    m = functools.reduce(jnp.maximum, [jnp.max(s, axis=0, keepdims=True) for s in scores])
    return [_bf(jnp.exp2(s - m)) for s in scores]


def _weighted_values(vt, p):
    ones = jnp.ones((BF16_ROWS, vt.shape[1]), BF16)
    return _dot(jnp.concatenate([vt, ones], axis=0), p)


def _normalise(acc):
    return _bf(acc[0:HEAD_DIM] * (1.0 / acc[HEAD_DIM:HEAD_DIM + 1]))


def _ctx_attn_kernel(qt_ref, kt_ref, vt_ref, ot_ref):
    def rows(hh):
        return slice(hh * HEAD_DIM, (hh + 1) * HEAD_DIM)

    def scores(hh):
        return _dot_tn(kt_ref[rows(hh), :], qt_ref[rows(hh), :])

    pending = [scores(hh) for hh in range(ATTN_LOOKAHEAD)]
    for hh in range(N_HEADS):
        if hh + ATTN_LOOKAHEAD < N_HEADS:
            pending.append(scores(hh + ATTN_LOOKAHEAD))
        (p,) = _softmax_weights(pending.pop(0))
        ot_ref[rows(hh), :] = _normalise(_weighted_values(vt_ref[rows(hh), :], p))


def _ctx_attn(qkvt, n_ctx, seq):
    return pl.pallas_call(
        _ctx_attn_kernel,
        out_shape=jax.ShapeDtypeStruct((D_MODEL, n_ctx), BF16),
        grid=(n_ctx // seq,),
        in_specs=[pl.BlockSpec((D_MODEL, seq), lambda b, c=c: (c, b)) for c in range(3)],
        out_specs=pl.BlockSpec((D_MODEL, seq), lambda b: (0, b)),
        compiler_params=_params("arbitrary"),
        name="ctx_attn",
    )(qkvt, qkvt, qkvt)


LAT_ROWS = 16
HALF_ROWS = LAT_ROWS // 2
LOCAL_ROWS = 12
HALF_KEY_ROW0 = (0, LAT_ROWS - LOCAL_ROWS)


def _fill_bias_table(rpb_ref, tab_ref):
    shape = (GRID_W, LANES)
    kc = lax.broadcasted_iota(jnp.int32, shape, 0)
    lane = lax.broadcasted_iota(jnp.int32, shape, 1)
    qc = lane % GRID_W
    win0 = jnp.clip(qc - WIN_W // 2, 0, GRID_W - WIN_W)
    col_ok = (kc >= win0) & (kc < win0 + WIN_W)
    masked = jnp.full(shape, MASK_VALUE, F32)
    tiles = []
    for dr in range(2 * WIN_H - 1):
        row = jnp.broadcast_to(rpb_ref[dr:dr + 1, :] * LOG2E, shape)
        t = pltpu.roll(row, LANES - (WIN_W - 1), 1, stride=1, stride_axis=0)
        t = jnp.where(lane < GRID_W, t, pltpu.roll(t, GRID_W, 1))
        tiles.append(jnp.where(col_ok, t, MASK_VALUE))

    def tile(qr, kr):
        row0 = min(max(qr - WIN_H // 2, 0), LAT_ROWS - WIN_H)
        if row0 <= kr < row0 + WIN_H:
            return tiles[kr - qr + WIN_H - 1]
        return masked

    for qr in range(0, LAT_ROWS, 2):
        kr0 = HALF_KEY_ROW0[qr // HALF_ROWS]
        for i in range(LOCAL_ROWS):
            both = jnp.where(lane < GRID_W, tile(qr, kr0 + i), tile(qr + 1, kr0 + i))
            tab_ref[i * GRID_W:(i + 1) * GRID_W, qr * GRID_W:(qr + 2) * GRID_W] = both


def _lat_attn_kernel(qt_ref, kt_ref, vt_ref, ckt_ref, cvt_ref, rpb_ref, ot_ref, tab_ref):
    seq = LAT_ROWS * GRID_W
    nq = HALF_ROWS * GRID_W
    nk = LOCAL_ROWS * GRID_W
    heads = qt_ref.shape[0] // HEAD_DIM
    units = [(b, hh, half) for b in range(qt_ref.shape[1] // seq) for hh in range(heads) for half in range(2)]

    def rows(hh):
        return slice(hh * HEAD_DIM, (hh + 1) * HEAD_DIM)

    def scores(unit):
        b, hh, half = unit
        key0 = b * seq + HALF_KEY_ROW0[half] * GRID_W
        qt = qt_ref[rows(hh), b * seq + half * nq:b * seq + (half + 1) * nq]
        s_loc = _dot_tn(kt_ref[rows(hh), key0:key0 + nk], qt) + tab_ref[hh, :, half * nq:(half + 1) * nq]
        s_ctx = _dot_tn(_bf(ckt_ref[b, hh]), qt)
        return s_loc, s_ctx

    for hh in range(tab_ref.shape[0]):
        _fill_bias_table(rpb_ref.at[hh], tab_ref.at[hh])
    pending = [scores(u) for u in units[:ATTN_LOOKAHEAD]]
    for i, (b, hh, half) in enumerate(units):
        if i + ATTN_LOOKAHEAD < len(units):
            pending.append(scores(units[i + ATTN_LOOKAHEAD]))
        s_loc, s_ctx = pending.pop(0)
        key0 = b * seq + HALF_KEY_ROW0[half] * GRID_W
        p_loc, p_ctx = _softmax_weights(s_loc, s_ctx)
        acc = (_weighted_values(vt_ref[rows(hh), key0:key0 + nk], p_loc)
               + _weighted_values(_bf(cvt_ref[b, hh]), p_ctx))
        ot_ref[rows(hh), b * seq + half * nq:b * seq + (half + 1) * nq] = _normalise(acc)


def _lat_attn(qkvt, n_ctx, cache_kt, cache_vt, rpb, layer_idx):
    rpb = jnp.pad(rpb[:, :, ::-1], ((0, 0), (0, 1), (0, LANES - (2 * WIN_W - 1))))
    n = qkvt.shape[1] - n_ctx
    seq = LAT_ROWS * GRID_W
    assert n_ctx % n == 0 and n % seq == 0
    nb = n // seq
    nhp = N_HEADS // 2
    past = cache_kt.shape[4]
    feat = lambda c: pl.BlockSpec((2 * HEAD_DIM, n), lambda hp, c=c: (c * nhp + hp, n_ctx // n))
    cache = pl.BlockSpec((nb, None, 2, HEAD_DIM, past), lambda hp: (0, layer_idx, hp, 0, 0))
    return pl.pallas_call(
        _lat_attn_kernel,
        out_shape=jax.ShapeDtypeStruct((D_MODEL, n), BF16),
        grid=(nhp,),
        in_specs=[feat(0), feat(1), feat(2), cache, cache,
                  pl.BlockSpec((2, 2 * WIN_H, LANES), lambda hp: (hp, 0, 0))],
        out_specs=pl.BlockSpec((2 * HEAD_DIM, n), lambda hp: (hp, 0)),
        scratch_shapes=[pltpu.VMEM((2, LOCAL_ROWS * GRID_W, seq), F32)],
        compiler_params=_params("arbitrary"),
        name="lat_attn",
    )(qkvt, qkvt, qkvt, cache_kt, cache_vt, rpb)


def kernel(x_prompt, x_sample, cache_k, cache_v, c, c_ctx, ada_w, ada_b, ln1_g, ln1_b, ln2_g, ln2_b, ffn_w_in, ffn_w_out, hy_w_in, hy_sconv_w, hy_sconv_b, hy_pe_w1, hy_pe_b1, hy_pe_w2, hy_pe_b2, hy_pe_w3, hy_sin_freq, hy_decay, hy_bias, hy_w_out, na_w_qkv, na_rpb, na_w_out):
    nbp, lp, d = x_prompt.shape
    nbs, ls, _ = x_sample.shape
    assert d == D_MODEL and ls == LAT_ROWS * GRID_W and c.shape[0] == nbs
    xp = x_prompt.reshape(nbp * lp, d)
    xs = x_sample.reshape(nbs * ls, d)
    n_ctx, n_lat = xp.shape[0], xs.shape[0]
    n = n_ctx + n_lat
    ln = (ln1_g, ln1_b, ln2_g, ln2_b)

    mods = _ada(c_ctx, c, ada_w, ada_b)

    filt = (hy_pe_w1[0], hy_pe_b1[0], hy_pe_w2[0], hy_pe_b2[0], hy_pe_w3[0], hy_sin_freq[0], hy_decay[0])
    coef_p = _hyena_filter_spectrum(lp, *filt)
    coef_s = _hyena_filter_spectrum(ls, *filt)
    sconv_b = hy_sconv_b[0].reshape(1, -1)
    hy_bias0 = hy_bias[0].reshape(1, -1)
    x0, vx, ffn_w_in0 = _hy_in(xp, xs, lp, ls, mods, 0, _bf(hy_w_in[0]), hy_sconv_w[0], sconv_b,
                               casts=[_Cast(ffn_w_in, 0, n // ls, _step)])
    steps, lin = _hy_conv_steps(n_ctx)
    u_p, hy_w_out0 = _hy_conv(vx, x0, 0, n_ctx, coef_p, hy_bias0, lp, casts=[_Cast(hy_w_out, 0, steps, lin)])
    steps, lin = _hy_conv_steps(n_lat)
    u_s, w_qkv_t, ffn_w_out0 = _hy_conv(vx, x0, n_ctx, n_lat, coef_s, hy_bias0, ls,
                                        casts=[_Cast(na_w_qkv, 0, steps, lin, transpose=True),
                                               _Cast(ffn_w_out, 0, steps, lin)])
    steps = n // POST_TOKENS
    x1, ffn_w_in1, ffn_w_out1, na_w_out0 = _post(
        (xp, xs), (u_p, u_s), n, n_ctx, ls, mods, 0, hy_w_out0, ffn_w_in0, ffn_w_out0, *ln, name="post0",
        casts=[_Cast(ffn_w_in, 1, steps, _step), _Cast(ffn_w_out, 1, steps, _step),
               _Cast(na_w_out, 0, steps, _step)])

    qkvt, new_kt, new_vt = _qkv(x1, n_ctx, lp, ls, mods, 1, w_qkv_t)
    at_p = _ctx_attn(qkvt, n_ctx, lp)
    at_s = _lat_attn(qkvt, n_ctx, jnp.swapaxes(cache_k, 3, 4), jnp.swapaxes(cache_v, 3, 4), na_rpb[0], 0)
    ffn1 = (na_w_out0, ffn_w_in1, ffn_w_out1, *ln)
    (yp,) = _post((x1,), (at_p,), n_ctx, n_ctx, ls, mods, 1, *ffn1, name="post1_ctx", a_feature_major=True)
    (ys,) = _post((x1,), (at_s,), n_lat, 0, ls, mods, 1, *ffn1, name="post1_lat", x_tok0=n_ctx,
                  a_feature_major=True)

    return (yp.reshape(nbp, lp, d), ys.reshape(nbs, ls, d),
            jnp.swapaxes(new_kt, 3, 4), jnp.swapaxes(new_vt, 3, 4))
```

```python
import functools
import math

import jax
import jax.numpy as jnp
import numpy as np
from jax import lax
from jax.experimental import pallas as pl
from jax.experimental.pallas import tpu as pltpu

D_MODEL = 1024
N_HEADS = 16
HEAD_DIM = D_MODEL // N_HEADS
D_FF = 2816
GRID_W = 64
WIN_H = 8
WIN_W = 16
N_BANDS = 16
PE_DIM = 1 + 2 * N_BANDS
FILT_W = 64
MOD_SHIFT = 0.05
DEPTH = 2
ALPHA = (2 * DEPTH) ** 0.25
LN_EPS = 1e-5
MASK_VALUE = -1e30
MOD_ROWS = 8

LANES = 128
SUBLANES = 8
BF16_ROWS = 2 * SUBLANES
ATTN_LOOKAHEAD = 2
LOG2E = math.log2(math.e)
VMEM_LIMIT = 56 * 1024 * 1024

ADA_COLS = 1536
FILTER_CHANNELS = 256
HY_IN_COLS = 256
HY_CONV_TOKENS = 1024
HY_CONV_CHANNELS = 512
POST_TOKENS = 512
POST_ROW_GROUPS = 2
FFN_COLS = 256
QKV_TOKENS = 512

F32 = jnp.float32
BF16 = jnp.bfloat16


def _bf(x):
    return x.astype(BF16)


def _dot(a, b):
    return jnp.dot(a, b, preferred_element_type=F32)


def _dot_nt(a, b):
    return lax.dot_general(a, b, (((1,), (1,)), ((), ())), preferred_element_type=F32)


def _dot_tn(a, b):
    return lax.dot_general(a, b, (((0,), (0,)), ((), ())), preferred_element_type=F32)


def _layer_norm(x, g, b):
    mu = jnp.mean(x, axis=-1, keepdims=True)
    xc = x - mu
    var = jnp.mean(xc * xc, axis=-1, keepdims=True)
    return xc * lax.rsqrt(var + LN_EPS) * g + b


def _params(*sem):
    return pltpu.CompilerParams(dimension_semantics=sem, vmem_limit_bytes=VMEM_LIMIT)


class _Cast:
    def __init__(self, w, layer, steps, step_fn, transpose=False):
        _, r, c = w.shape
        assert r % (steps * BF16_ROWS) == 0
        rows = r // steps
        self.w = w
        self.transpose = transpose
        self.in_spec = pl.BlockSpec((None, rows, c), lambda *g: (layer, step_fn(*g), 0))
        if transpose:
            self.out_shape = jax.ShapeDtypeStruct((c, r), BF16)
            self.out_spec = pl.BlockSpec((c, rows), lambda *g: (0, step_fn(*g)))
        else:
            self.out_shape = jax.ShapeDtypeStruct((r, c), BF16)
            self.out_spec = pl.BlockSpec((rows, c), lambda *g: (step_fn(*g), 0))


def _with_casts(body, n_in, n_out, casts):
    nc = len(casts)

    def kernel_fn(*refs):
        ins, src = refs[:n_in], refs[n_in:n_in + nc]
        outs, dst = refs[n_in + nc:n_in + nc + n_out], refs[n_in + nc + n_out:n_in + 2 * nc + n_out]
        for cast, s, d in zip(casts, src, dst):
            d[...] = _bf(s[...].T if cast.transpose else s[...])
        body(*ins, *outs, *refs[n_in + 2 * nc + n_out:])

    return kernel_fn


def _step(i):
    return i


def _ctx_block(i, ctx_steps):
    return jnp.minimum(i, ctx_steps - 1)


def _lat_block(i, ctx_steps):
    return jnp.maximum(i - ctx_steps, 0)


def _mod_row(i, ctx_steps, tm, lat_len):
    return jnp.where(i < ctx_steps, 0, 1 + ((i - ctx_steps) * tm) // lat_len)


def _pick(is_ctx, refs, idx):
    if len(refs) == 1:
        return refs[0][idx]
    return jnp.where(is_ctx, refs[0][idx], refs[1][idx])


@functools.lru_cache(maxsize=None)
def _plain_dft_tables(L):
    n = 2 * L
    f = np.arange(L)[:, None]
    s = np.arange(L)[None, :]
    ang = 2.0 * np.pi * ((f * s) % n) / n
    c = np.cos(ang)
    sn = np.sin(ang)
    sn[0, :] = np.where(np.arange(L) % 2 == 0, 1.0, -1.0)
    fwd = np.concatenate([c, sn], axis=0)
    w = np.full((2 * L, 1), 2.0 / n)
    w[0, 0] = 1.0 / n
    w[L, 0] = 1.0 / n
    inv = (fwd * w).T
    alt = np.where(np.arange(L) % 2 == 0, 1.0, -1.0).astype(np.float32)[:, None]
    return (np.ascontiguousarray(fwd).astype(np.float32), np.ascontiguousarray(inv).astype(np.float32), alt)


@functools.lru_cache(maxsize=None)
def _dft_tables(L):
    n = 2 * L
    half = L // 2
    f = np.arange(half)[:, None]
    m = np.arange(half)[None, :]
    alt = np.where(np.arange(half) % 2 == 0, 1.0, -1.0)
    w = np.full((L, 1), 2.0 / n)
    w[0, 0] = 1.0 / n
    w[half, 0] = 1.0 / n
    fwd, inv = [], []
    for parity in range(2):
        ang = 2.0 * np.pi * ((f * (2 * m + parity)) % n) / n
        sn = np.sin(ang)
        sn[0, :] = alt
        table = np.concatenate([np.cos(ang), sn], axis=0)
        fwd.append(np.ascontiguousarray(table).astype(np.float32))
        inv.append(np.ascontiguousarray((table * w).T).astype(np.float32))
    return fwd, inv


@functools.lru_cache(maxsize=None)
def _filter_dft_tables(L):
    n = 2 * L
    f = np.arange(L // 2)[:, None]
    t = np.arange(L)[None, :]
    ang = 2.0 * np.pi * ((f * t) % n) / n
    table = np.concatenate([np.cos(ang), np.sin(ang)], axis=0).astype(np.float32)
    tt = np.arange(L)
    cols = np.stack([np.where(tt % 2 == 0, 1.0, -1.0), np.cos(np.pi * (tt % 4) / 2.0),
                     np.sin(np.pi * (tt % 4) / 2.0)], axis=1)
    return np.ascontiguousarray(table), np.round(cols).astype(np.float32)


@functools.lru_cache(maxsize=None)
def _filter_features(L):
    t = np.linspace(0.0, 1.0, L, dtype=np.float32)[:, None]
    w = (np.float32(2.0 * math.pi / L) * np.arange(L, dtype=np.float32))[:, None]
    bands = np.linspace(1e-4, N_BANDS - 1, N_BANDS, dtype=np.float32)[None, :]
    arg = (bands * w).astype(np.float64)
    feats = np.concatenate([t.astype(np.float64), np.cos(arg), np.sin(arg)], axis=-1)
    out = np.zeros((L, LANES), np.float32)
    out[:, :PE_DIM] = feats
    return out, t


HY_SPLIT_MIN_LEN = 512


def _filter_coef_shape(L):
    return (8, L // 2) if L >= HY_SPLIT_MIN_LEN else (3, L)


def _ada_tile(layer, cctx_ref, c_ref, w_ref, b_ref, o_ref, cvec_ref):
    nb = c_ref.shape[0]
    cvec_ref[...] = jnp.zeros_like(cvec_ref)
    cvec_ref[0:1, :] = cctx_ref[...]
    cvec_ref[1:1 + nb, :] = c_ref[...]
    c = cvec_ref[...]
    s = c * jax.nn.sigmoid(c)
    s_hi = _bf(s)
    s_lo = _bf(s - s_hi.astype(F32))
    w = w_ref[...]
    w_hi = _bf(w)
    w_lo = _bf(w - w_hi.astype(F32))
    o_ref[...] = _dot(s_hi, w_hi) + _dot(s_lo, w_hi) + _dot(s_hi, w_lo) + b_ref[pl.ds(layer, 1), :]


def _filter_tile(is_first, feats_ref, t_ref, cols_ref, tab_ref, w1_ref, b1_ref, w2_ref, b2_ref, freq_ref,
                 w3f_ref, w3b_ref, decf_ref, decb_ref, out_ref, h_ref, *, L):
    hi = lax.Precision.HIGHEST

    @pl.when(is_first)
    def _():
        h1 = jnp.sin(freq_ref[0:1, :] * (jnp.dot(feats_ref[...], w1_ref[...], precision=hi,
                                                 preferred_element_type=F32) + b1_ref[...]))
        h_ref[...] = jnp.sin(freq_ref[1:2, :] * (jnp.dot(h1, w2_ref[...], precision=hi,
                                                         preferred_element_type=F32) + b2_ref[...]))

    h = h_ref[...]
    t = t_ref[...]
    kf = jnp.dot(h, w3f_ref[...], precision=hi, preferred_element_type=F32) * (
        jnp.exp(-t * jnp.abs(decf_ref[...])) + MOD_SHIFT)
    kb = jnp.dot(h, w3b_ref[...], precision=hi, preferred_element_type=F32) * (
        jnp.exp(-t * jnp.abs(decb_ref[...])) + MOD_SHIFT)
    row = lax.broadcasted_iota(jnp.int32, kf.shape, 0)
    kb = jnp.where(row == 0, 0.0, kb)
    ksum = kf + kb
    kdiff = kf - kb
    alt = cols_ref[:, 0:1]

    def spectrum(table, x):
        hi = _bf(x)
        return _dot(table, hi) + _dot(table, _bf(x - hi.astype(F32)))

    if L < HY_SPLIT_MIN_LEN:
        k_c = spectrum(tab_ref[0:L, :], ksum)
        k_s = spectrum(tab_ref[L:2 * L, :], kdiff)
        k_nyq = jnp.sum(ksum * alt, axis=0, keepdims=True)
        out_ref[0] = k_c
        out_ref[1] = jnp.where(row == 0, 0.0, k_s)
        out_ref[2] = jnp.where(row == 0, k_nyq, k_c)
        return

    half = L // 2
    cos_m = tab_ref[0:half, :]
    sin_m = tab_ref[half:L, :]
    k_c = spectrum(cos_m, ksum)
    k_s = spectrum(sin_m, kdiff)
    u_c = spectrum(cos_m, alt * ksum)
    u_s = -spectrum(sin_m, alt * kdiff)
    kc_h = jnp.sum(ksum * cols_ref[:, 1:2], axis=0, keepdims=True)
    ks_h = jnp.sum(kdiff * cols_ref[:, 2:3], axis=0, keepdims=True)
    first = lax.broadcasted_iota(jnp.int32, k_c.shape, 0) == 0
    out_ref[0] = k_c
    out_ref[1] = k_s
    out_ref[2] = u_c
    out_ref[3] = u_s
    out_ref[4] = jnp.where(first, kc_h - ks_h, k_c)
    out_ref[5] = jnp.where(first, kc_h + ks_h, u_c)
    out_ref[6] = jnp.where(first, kc_h + ks_h, k_c)
    out_ref[7] = jnp.where(first, kc_h - ks_h, u_c)


def _prep_kernel(cctx_ref, c_ref, w_ref, b_ref, fa_ref, ta_ref, ca_ref, tba_ref, fb_ref, tb_ref, cb_ref,
                 tbb_ref, w1_ref, b1_ref, w2_ref, b2_ref, freq_ref, w3f_ref, w3b_ref, decf_ref, decb_ref,
                 mod_ref, coefa_ref, coefb_ref, cvec_ref, ha_ref, hb_ref, *, lens, tiles):
    s = pl.program_id(0)
    _ada_tile(s // tiles, cctx_ref, c_ref, w_ref, b_ref, mod_ref, cvec_ref)
    shared = (w1_ref, b1_ref, w2_ref, b2_ref, freq_ref, w3f_ref, w3b_ref, decf_ref, decb_ref)

    @pl.when(s < tiles)
    def _():
        _filter_tile(s == 0, fa_ref, ta_ref, ca_ref, tba_ref, *shared, coefa_ref, ha_ref, L=lens[0])

    @pl.when(s >= tiles)
    def _():
        _filter_tile(s == tiles, fb_ref, tb_ref, cb_ref, tbb_ref, *shared, coefb_ref, hb_ref, L=lens[1])


def _prep(c_ctx, c, ada_w, ada_b, lens, pe_w1, pe_b1, pe_w2, pe_b2, pe_w3, sin_freq, decay):
    assert 1 + c.shape[0] <= MOD_ROWS and DEPTH == 2
    tn = ADA_COLS
    dt = FILTER_CHANNELS
    tiles = D_MODEL // dt
    assert 6 * D_MODEL // tn == tiles
    full = lambda shape: pl.BlockSpec(shape, lambda s: (0,) * len(shape))
    consts, const_specs, out_shapes, out_specs, scratch = [], [], [], [], []
    for k, L in enumerate(lens):
        feats, t = _filter_features(L)
        if L >= HY_SPLIT_MIN_LEN:
            table, cols = _filter_dft_tables(L)
        else:
            table, _, cols = _plain_dft_tables(L)
        consts += [feats, t, cols, _bf(jnp.asarray(table))]
        const_specs += [full(feats.shape), full(t.shape), full(cols.shape), full(table.shape)]
        nc, nf = _filter_coef_shape(L)
        out_shapes.append(jax.ShapeDtypeStruct((nc, nf, D_MODEL), F32))
        block = (lambda s: (0, 0, jnp.minimum(s, tiles - 1))) if k == 0 else (
            lambda s: (0, 0, jnp.maximum(s - tiles, 0)))
        out_specs.append(pl.BlockSpec((nc, nf, dt), block))
        scratch.append(pltpu.VMEM((L, LANES), F32))
    pad = LANES - FILT_W
    w1 = jnp.pad(pe_w1, ((0, LANES - PE_DIM), (0, pad)))
    b1 = jnp.pad(pe_b1, (0, pad)).reshape(1, LANES)
    w2 = jnp.pad(pe_w2, ((0, pad), (0, pad)))
    b2 = jnp.pad(pe_b2, (0, pad)).reshape(1, LANES)
    w3 = jnp.pad(pe_w3, ((0, pad), (0, 0)))
    freq = jnp.pad(sin_freq, ((0, 0), (0, pad)))
    dec = decay.reshape(1, 2 * D_MODEL)
    return pl.pallas_call(
        functools.partial(_prep_kernel, lens=tuple(lens), tiles=tiles),
        out_shape=(jax.ShapeDtypeStruct((DEPTH, MOD_ROWS, 6 * D_MODEL), F32), *out_shapes),
        grid=(DEPTH * tiles,),
        in_specs=[
            full((1, D_MODEL)), full(c.shape),
            pl.BlockSpec((None, D_MODEL, tn), lambda s: (s // tiles, 0, s % tiles)),
            pl.BlockSpec((DEPTH, tn), lambda s: (0, s % tiles)),
            *const_specs,
            full((LANES, LANES)), full((1, LANES)), full((LANES, LANES)), full((1, LANES)), full((2, LANES)),
            pl.BlockSpec((LANES, dt), lambda s: (0, s % tiles)),
            pl.BlockSpec((LANES, dt), lambda s: (0, tiles + s % tiles)),
            pl.BlockSpec((1, dt), lambda s: (0, s % tiles)),
            pl.BlockSpec((1, dt), lambda s: (0, tiles + s % tiles)),
        ],
        out_specs=(pl.BlockSpec((None, MOD_ROWS, tn), lambda s: (s // tiles, 0, s % tiles)), *out_specs),
        scratch_shapes=[pltpu.VMEM((MOD_ROWS, D_MODEL), F32), *scratch],
        compiler_params=_params("arbitrary"),
        name="prep",
    )(c_ctx.reshape(1, D_MODEL), c, ada_w, ada_b, *consts, w1, b1, w2, b2, freq, w3, w3, dec, dec)


def _hy_in_kernel(xc_ref, xl_ref, mod_ref, w_ref, cw_ref, cb_ref, x0_ref, vx_ref, *, ctx_steps, ctx_len, cn):
    i = pl.program_id(0)
    is_ctx = i < ctx_steps
    tm = x0_ref.shape[0]
    mod = mod_ref[pl.ds(_mod_row(i, ctx_steps, tm, tm), 1), :]
    x = _pick(is_ctx, (xc_ref, xl_ref), (slice(None), slice(None)))
    h = _bf(x * (1.0 + mod[:, D_MODEL:2 * D_MODEL]) + mod[:, 0:D_MODEL])
    row = lax.broadcasted_iota(jnp.int32, (SUBLANES, cn), 0)

    def conv(col):
        z = _dot(h, w_ref[:, col:col + cn])
        zm = pltpu.roll(z, 1, 0)
        zp = pltpu.roll(z, tm - 1, 0)
        w0 = cw_ref[0:1, col:col + cn]
        w1 = cw_ref[1:2, col:col + cn]
        w2 = cw_ref[2:3, col:col + cn]
        b = cb_ref[:, col:col + cn]
        out = zm * w0 + z * w1 + zp * w2 + b
        pieces = []
        for s in range(tm // ctx_len):
            lo = s * ctx_len
            hi = lo + ctx_len - SUBLANES
            head = (jnp.where(row == 0, 0.0, zm[lo:lo + SUBLANES]) * w0 + z[lo:lo + SUBLANES] * w1
                    + zp[lo:lo + SUBLANES] * w2 + b)
            tail = (zm[hi:hi + SUBLANES] * w0 + z[hi:hi + SUBLANES] * w1
                    + jnp.where(row == SUBLANES - 1, 0.0, zp[hi:hi + SUBLANES]) * w2 + b)
            if lo > 0:
                head = jnp.where(is_ctx, head, out[lo:lo + SUBLANES])
            if hi + SUBLANES < tm:
                tail = jnp.where(is_ctx, tail, out[hi:hi + SUBLANES])
            pieces += [head, out[lo + SUBLANES:hi], tail]
        return jnp.concatenate(pieces, axis=0)

    for c in range(D_MODEL // cn):
        x0_ref[:, c * cn:(c + 1) * cn] = _bf(conv(c * cn))
        x1 = conv(D_MODEL + c * cn)
        v = conv(2 * D_MODEL + c * cn)
        vx_ref[:, c * cn:(c + 1) * cn] = _bf(v * x1)


def _hy_in(x_ctx, x_lat, ctx_len, lat_len, mods, layer, w_in, sconv_w, sconv_b, casts=()):
    tm = lat_len
    assert tm % ctx_len == 0 and x_ctx.shape[0] % tm == 0
    ctx_steps = x_ctx.shape[0] // tm
    n = x_ctx.shape[0] + x_lat.shape[0]
    tok = pl.BlockSpec((tm, D_MODEL), lambda i: (i, 0))
    const = lambda shape: pl.BlockSpec(shape, lambda i: (0, 0), pipeline_mode=pl.Buffered(1))
    act = jax.ShapeDtypeStruct((n, D_MODEL), BF16)
    body = functools.partial(_hy_in_kernel, ctx_steps=ctx_steps, ctx_len=ctx_len, cn=HY_IN_COLS)
    return pl.pallas_call(
        _with_casts(body, 6, 2, casts),
        out_shape=(act, act) + tuple(c.out_shape for c in casts),
        grid=(n // tm,),
        in_specs=[
            pl.BlockSpec((tm, D_MODEL), lambda i: (_ctx_block(i, ctx_steps), 0)),
            pl.BlockSpec((tm, D_MODEL), lambda i: (_lat_block(i, ctx_steps), 0)),
            pl.BlockSpec((None, MOD_ROWS, 6 * D_MODEL), lambda i: (layer, 0, 0)),
            const((D_MODEL, 3 * D_MODEL)),
            pl.BlockSpec((3, 3 * D_MODEL), lambda i: (0, 0)),
            pl.BlockSpec((1, 3 * D_MODEL), lambda i: (0, 0)),
        ] + [c.in_spec for c in casts],
        out_specs=(tok, tok) + tuple(c.out_spec for c in casts),
        compiler_params=_params("arbitrary"),
        name="hyena_in",
    )(x_ctx, x_lat, mods, w_in, sconv_w, sconv_b, *[c.w for c in casts])


def _hy_conv_plain_kernel(vx_ref, x0_ref, coef_ref, bias_ref, fwd_ref, inv_ref, u_ref, *, L, nseq, cn):
    for c in range(vx_ref.shape[1] // cn):
        cols = slice(c * cn, (c + 1) * cn)
        k_cos = coef_ref[0, :, cols]
        k_sin = coef_ref[1, :, cols]
        k_mix = coef_ref[2, :, cols]
        for s in range(nseq):
            rows = slice(s * L, (s + 1) * L)
            vx = vx_ref[rows, cols]
            spec = _dot(fwd_ref[...], vx)
            v_cos = spec[0:L]
            v_sin = spec[L:2 * L]
            y_spec = _bf(jnp.concatenate([v_cos * k_cos - v_sin * k_sin, v_cos * k_sin + v_sin * k_mix],
                                         axis=0))
            y = _dot(inv_ref[...], y_spec)
            u = (y + bias_ref[:, cols] * vx.astype(F32)) * x0_ref[rows, cols].astype(F32)
            u_ref[rows, cols] = _bf(u)


def _hy_conv_split_kernel(vx_ref, x0_ref, coef_ref, bias_ref, fe_ref, fo_ref, ge_ref, go_ref, u_ref,
                          sv_ref, sx_ref, su_ref, *, L, nseq, cn):
    half = L // 2
    for g in range(vx_ref.shape[1] // LANES):
        lanes = slice(g * LANES, (g + 1) * LANES)
        sv_ref[g] = vx_ref[:, lanes].astype(F32)
        sx_ref[g] = x0_ref[:, lanes].astype(F32)

    def load(ref, start, c):
        return jnp.concatenate([ref[g, pl.ds(start, half, stride=2), :]
                                for g in range(c * cn // LANES, (c + 1) * cn // LANES)], axis=1)

    def store(ref, start, c, val):
        for k, g in enumerate(range(c * cn // LANES, (c + 1) * cn // LANES)):
            ref[g, pl.ds(start, half, stride=2), :] = val[:, k * LANES:(k + 1) * LANES]

    for c in range(vx_ref.shape[1] // cn):
        cols = slice(c * cn, (c + 1) * cn)
        k_c, k_s, u_c, u_s, e2, e3, f2, f3 = [coef_ref[k, :, cols] for k in range(coef_ref.shape[0])]
        for s in range(nseq):
            v_e = load(sv_ref, s * L, c)
            v_o = load(sv_ref, s * L + 1, c)
            a = _dot(fe_ref[...], _bf(v_e))
            b = _dot(fo_ref[...], _bf(v_o))
            s_c, s_s = a[0:half] + b[0:half], a[half:L] + b[half:L]
            d_c, d_s = a[0:half] - b[0:half], a[half:L] - b[half:L]
            t1 = s_c * k_c - s_s * k_s
            t2 = d_c * u_c + d_s * u_s
            q1 = s_c * k_s
            q4 = d_c * u_s
            p_spec = jnp.concatenate([t1 + t2, q1 + s_s * e2 + d_s * e3 - q4], axis=0)
            m_spec = jnp.concatenate([t1 - t2, q1 + s_s * f2 - d_s * f3 + q4], axis=0)
            y_e = _dot(ge_ref[...], _bf(p_spec))
            y_o = _dot(go_ref[...], _bf(m_spec))
            bias = bias_ref[:, cols]
            store(su_ref, s * L, c, (y_e + bias * v_e) * load(sx_ref, s * L, c))
            store(su_ref, s * L + 1, c, (y_o + bias * v_o) * load(sx_ref, s * L + 1, c))
            rows = slice(s * L, (s + 1) * L)
            for g in range(c * cn // LANES, (c + 1) * cn // LANES):
                u_ref[rows, g * LANES:(g + 1) * LANES] = _bf(su_ref[g, rows, :])


def _hy_conv_steps(n):
    nt = n // HY_CONV_TOKENS
    return (D_MODEL // HY_CONV_CHANNELS) * nt, lambda j, i: j * nt + i


def _hy_conv(vx, x0, tok0, n, coef, bias, L, casts=()):
    tm = HY_CONV_TOKENS
    dt = HY_CONV_CHANNELS
    assert tok0 % tm == 0 and n % tm == 0 and tm % L == 0
    src = pl.BlockSpec((tm, dt), lambda j, i: (tok0 // tm + i, j))
    dst = pl.BlockSpec((tm, dt), lambda j, i: (i, j))
    const = lambda shape: pl.BlockSpec(shape, lambda j, i: (0, 0), pipeline_mode=pl.Buffered(1))
    if L >= HY_SPLIT_MIN_LEN:
        fwd, inv = _dft_tables(L)
        tables = (*fwd, *inv)
        body = functools.partial(_hy_conv_split_kernel, L=L, nseq=tm // L, cn=dt)
        scratch = [pltpu.VMEM((dt // LANES, tm, LANES), F32)] * 3
    else:
        tables = _plain_dft_tables(L)[:2]
        body = functools.partial(_hy_conv_plain_kernel, L=L, nseq=tm // L, cn=256)
        scratch = []
    return pl.pallas_call(
        _with_casts(body, 4 + len(tables), 1, casts),
        out_shape=(jax.ShapeDtypeStruct((n, D_MODEL), BF16),) + tuple(c.out_shape for c in casts),
        grid=(D_MODEL // dt, n // tm),
        in_specs=[
            src, src,
            pl.BlockSpec(_filter_coef_shape(L) + (dt,), lambda j, i: (0, 0, j)),
            pl.BlockSpec((1, dt), lambda j, i: (0, j)),
        ] + [const(t.shape) for t in tables] + [c.in_spec for c in casts],
        out_specs=(dst,) + tuple(c.out_spec for c in casts),
        scratch_shapes=scratch,
        compiler_params=_params("arbitrary", "arbitrary"),
        name=f"hyena_conv_{L}",
    )(vx, x0, coef, bias, *[_bf(jnp.asarray(t)) for t in tables], *[c.w for c in casts])


def _post_kernel(*refs, layer, ctx_steps, lat_len, n_x, n_a, a_feature_major):
    x_refs, a_refs = refs[:n_x], refs[n_x:n_x + n_a]
    (mod_ref, wo_ref, ln1g_ref, ln1b_ref, win_ref, wout_ref, ln2g_ref, ln2b_ref, o_ref,
     act_ref) = refs[n_x + n_a:]
    d = D_MODEL
    i = pl.program_id(0)
    is_ctx = i < ctx_steps
    tm = act_ref.shape[0]
    mod = mod_ref[pl.ds(_mod_row(i, ctx_steps, tm, lat_len), 1), :]
    g1 = mod[:, 2 * d:3 * d]
    sh2 = mod[:, 3 * d:4 * d]
    sc2 = mod[:, 4 * d:5 * d]
    g2 = mod[:, 5 * d:6 * d]
    ln1 = (ln1g_ref[layer:layer + 1, :], ln1b_ref[layer:layer + 1, :])
    ln2 = (ln2g_ref[layer:layer + 1, :], ln2b_ref[layer:layer + 1, :])

    gr = tm // POST_ROW_GROUPS
    groups = [slice(r * gr, (r + 1) * gr) for r in range(POST_ROW_GROUPS)]
    if a_feature_major:
        mix = [_dot_tn(_pick(is_ctx, a_refs, (slice(None), rows)), wo_ref[...]) for rows in groups]
    else:
        mix = [_dot(_pick(is_ctx, a_refs, (rows, slice(None))), wo_ref[...]) for rows in groups]
    x1 = []
    for rows, mx in zip(groups, mix):
        x = _pick(is_ctx, x_refs, (rows, slice(None)))
        x1.append(_layer_norm(ALPHA * x + g1 * mx, *ln1))
        h = _bf(x1[-1] * (1.0 + sc2) + sh2)
        for c in range(D_FF // FFN_COLS):
            cols = slice(c * FFN_COLS, (c + 1) * FFN_COLS)
            gate = _dot(h, win_ref[:, cols])
            up = _dot(h, win_ref[:, D_FF + c * FFN_COLS:D_FF + (c + 1) * FFN_COLS])
            act_ref[rows, cols] = _bf(gate * jax.nn.sigmoid(gate) * up)
    for rows, xr in zip(groups, x1):
        f = _dot(act_ref[rows, :], wout_ref[...])
        o_ref[rows, :] = _layer_norm(ALPHA * xr + g2 * f, *ln2)


def _post(xs, mixes, n, n_ctx, lat_len, mods, layer, w_o, w_in, w_out, ln1_g, ln1_b, ln2_g, ln2_b, name,
          x_tok0=0, a_feature_major=False, casts=()):
    tm = POST_TOKENS
    ctx_steps = n_ctx // tm
    ctx = lambda i: _ctx_block(i, ctx_steps)
    lat = lambda i: _lat_block(i, ctx_steps)
    blocks = lambda m, tok0: (lambda i: tok0 // tm + i,) if m == 1 else (ctx, lat)
    tok = lambda f: pl.BlockSpec((tm, D_MODEL), lambda i: (f(i), 0))
    feat = lambda f: pl.BlockSpec((D_MODEL, tm), lambda i: (0, f(i)))
    const = lambda shape: pl.BlockSpec(shape, lambda i: (0, 0), pipeline_mode=pl.Buffered(1))
    vec = pl.BlockSpec((DEPTH, D_MODEL), lambda i: (0, 0))
    body = functools.partial(_post_kernel, layer=layer, ctx_steps=ctx_steps, lat_len=lat_len, n_x=len(xs),
                             n_a=len(mixes), a_feature_major=a_feature_major)
    return pl.pallas_call(
        _with_casts(body, len(xs) + len(mixes) + 8, 1, casts),
        out_shape=(jax.ShapeDtypeStruct((n, D_MODEL), F32),) + tuple(c.out_shape for c in casts),
        grid=(n // tm,),
        in_specs=[tok(f) for f in blocks(len(xs), x_tok0)]
        + [(feat if a_feature_major else tok)(f) for f in blocks(len(mixes), 0)]
        + [pl.BlockSpec((None, MOD_ROWS, 6 * D_MODEL), lambda i: (layer, 0, 0)),
           const((D_MODEL, D_MODEL)), vec, vec,
           const((D_MODEL, 2 * D_FF)), const((D_FF, D_MODEL)), vec, vec]
        + [c.in_spec for c in casts],
        out_specs=(tok(lambda i: i),) + tuple(c.out_spec for c in casts),
        scratch_shapes=[pltpu.VMEM((tm, D_FF), BF16)],
        compiler_params=_params("arbitrary"),
        name=name,
    )(*xs, *mixes, mods, w_o, ln1_g, ln1_b, w_in, w_out, ln2_g, ln2_b, *[c.w for c in casts])


def _qkv_kernel(x_ref, mod_ref, wt_ref, qkvt_ref, nk_ref, nv_ref, *, ctx_steps, lat_len):
    d = D_MODEL
    i = pl.program_id(0)
    tm = x_ref.shape[0]
    mod = mod_ref[pl.ds(_mod_row(i, ctx_steps, tm, lat_len), 1), :]
    h = _bf(x_ref[...] * (1.0 + mod[:, d:2 * d]) + mod[:, 0:d])
    qkvt_ref[0:d, :] = _bf(_dot_nt(wt_ref[0:d, :], h) * (HEAD_DIM ** -0.5 * LOG2E))
    kt = _dot_nt(wt_ref[d:2 * d, :], h)
    qkvt_ref[d:2 * d, :] = _bf(kt)
    vt = _dot_nt(wt_ref[2 * d:3 * d, :], h)
    qkvt_ref[2 * d:3 * d, :] = _bf(vt)

    @pl.when(i < ctx_steps)
    def _():
        seq = nk_ref.shape[4]
        for s in range(tm // seq):
            nk_ref[s, 0] = kt[:, s * seq:(s + 1) * seq].reshape(N_HEADS, HEAD_DIM, seq)
            nv_ref[s, 0] = vt[:, s * seq:(s + 1) * seq].reshape(N_HEADS, HEAD_DIM, seq)


def _qkv(x2d, n_ctx, ctx_len, lat_len, mods, layer, w_qkv_t):
    n = x2d.shape[0]
    tm = QKV_TOKENS
    assert tm % ctx_len == 0 and n_ctx % tm == 0
    ctx_steps = n_ctx // tm
    kv_shape = jax.ShapeDtypeStruct((n_ctx // ctx_len, 1, N_HEADS, HEAD_DIM, ctx_len), F32)
    kv_spec = pl.BlockSpec((tm // ctx_len, 1, N_HEADS, HEAD_DIM, ctx_len),
                           lambda i: (_ctx_block(i, ctx_steps), 0, 0, 0, 0))
    return pl.pallas_call(
        functools.partial(_qkv_kernel, ctx_steps=ctx_steps, lat_len=lat_len),
        out_shape=(jax.ShapeDtypeStruct((3 * D_MODEL, n), BF16), kv_shape, kv_shape),
        grid=(n // tm,),
        in_specs=[
            pl.BlockSpec((tm, D_MODEL), lambda i: (i, 0)),
            pl.BlockSpec((None, MOD_ROWS, 6 * D_MODEL), lambda i: (layer, 0, 0)),
            pl.BlockSpec((3 * D_MODEL, D_MODEL), lambda i: (0, 0), pipeline_mode=pl.Buffered(1)),
        ],
        out_specs=(pl.BlockSpec((3 * D_MODEL, tm), lambda i: (0, i)), kv_spec, kv_spec),
        compiler_params=_params("arbitrary"),
        name="qkv",
    )(x2d, mods, w_qkv_t)


def _softmax_weights(*scores):
    m = functools.reduce(jnp.maximum, [jnp.max(s, axis=0, keepdims=True) for s in scores])
    return [_bf(jnp.exp2(s - m)) for s in scores]


def _weighted_values(vt, p):
    ones = jnp.ones((BF16_ROWS, vt.shape[1]), BF16)
    return _dot(jnp.concatenate([vt, ones], axis=0), p)


def _normalise(acc):
    return _bf(acc[0:HEAD_DIM] * (1.0 / acc[HEAD_DIM:HEAD_DIM + 1]))


def _ctx_attn_kernel(qt_ref, kt_ref, vt_ref, ot_ref):
    def rows(hh):
        return slice(hh * HEAD_DIM, (hh + 1) * HEAD_DIM)

    def scores(hh):
        return _dot_tn(kt_ref[rows(hh), :], qt_ref[rows(hh), :])

    pending = [scores(hh) for hh in range(ATTN_LOOKAHEAD)]
    for hh in range(N_HEADS):
        if hh + ATTN_LOOKAHEAD < N_HEADS:
            pending.append(scores(hh + ATTN_LOOKAHEAD))
        (p,) = _softmax_weights(pending.pop(0))
        ot_ref[rows(hh), :] = _normalise(_weighted_values(vt_ref[rows(hh), :], p))


def _ctx_attn(qkvt, n_ctx, seq):
    return pl.pallas_call(
        _ctx_attn_kernel,
        out_shape=jax.ShapeDtypeStruct((D_MODEL, n_ctx), BF16),
        grid=(n_ctx // seq,),
        in_specs=[pl.BlockSpec((D_MODEL, seq), lambda b, c=c: (c, b)) for c in range(3)],
        out_specs=pl.BlockSpec((D_MODEL, seq), lambda b: (0, b)),
        compiler_params=_params("arbitrary"),
        name="ctx_attn",
    )(qkvt, qkvt, qkvt)


LAT_ROWS = 16
HALF_ROWS = LAT_ROWS // 2
LOCAL_ROWS = 12
HALF_KEY_ROW0 = (0, LAT_ROWS - LOCAL_ROWS)


def _fill_bias_table(rpb_ref, tab_ref):
    shape = (GRID_W, LANES)
    kc = lax.broadcasted_iota(jnp.int32, shape, 0)
    lane = lax.broadcasted_iota(jnp.int32, shape, 1)
    qc = lane % GRID_W
    win0 = jnp.clip(qc - WIN_W // 2, 0, GRID_W - WIN_W)
    col_ok = (kc >= win0) & (kc < win0 + WIN_W)
    masked = jnp.full(shape, MASK_VALUE, F32)
    tiles = []
    for dr in range(2 * WIN_H - 1):
        row = jnp.broadcast_to(rpb_ref[dr:dr + 1, :] * LOG2E, shape)
        t = pltpu.roll(row, LANES - (WIN_W - 1), 1, stride=1, stride_axis=0)
        t = jnp.where(lane < GRID_W, t, pltpu.roll(t, GRID_W, 1))
        tiles.append(jnp.where(col_ok, t, MASK_VALUE))

    def tile(qr, kr):
        row0 = min(max(qr - WIN_H // 2, 0), LAT_ROWS - WIN_H)
        if row0 <= kr < row0 + WIN_H:
            return tiles[kr - qr + WIN_H - 1]
        return masked

    for qr in range(0, LAT_ROWS, 2):
        kr0 = HALF_KEY_ROW0[qr // HALF_ROWS]
        for i in range(LOCAL_ROWS):
            both = jnp.where(lane < GRID_W, tile(qr, kr0 + i), tile(qr + 1, kr0 + i))
            tab_ref[i * GRID_W:(i + 1) * GRID_W, qr * GRID_W:(qr + 2) * GRID_W] = both


def _lat_attn_kernel(qt_ref, kt_ref, vt_ref, ckt_ref, cvt_ref, rpb_ref, ot_ref, tab_ref):
    seq = LAT_ROWS * GRID_W
    nq = HALF_ROWS * GRID_W
    nk = LOCAL_ROWS * GRID_W
    heads = qt_ref.shape[0] // HEAD_DIM
    units = [(b, hh, half) for b in range(qt_ref.shape[1] // seq) for hh in range(heads) for half in range(2)]

    def rows(hh):
        return slice(hh * HEAD_DIM, (hh + 1) * HEAD_DIM)

    def scores(unit):
        b, hh, half = unit
        key0 = b * seq + HALF_KEY_ROW0[half] * GRID_W
        qt = qt_ref[rows(hh), b * seq + half * nq:b * seq + (half + 1) * nq]
        s_loc = _dot_tn(kt_ref[rows(hh), key0:key0 + nk], qt) + tab_ref[hh, :, half * nq:(half + 1) * nq]
        s_ctx = _dot_tn(_bf(ckt_ref[b, hh]), qt)
        return s_loc, s_ctx

    for hh in range(tab_ref.shape[0]):
        _fill_bias_table(rpb_ref.at[hh], tab_ref.at[hh])
    pending = [scores(u) for u in units[:ATTN_LOOKAHEAD]]
    for i, (b, hh, half) in enumerate(units):
        if i + ATTN_LOOKAHEAD < len(units):
            pending.append(scores(units[i + ATTN_LOOKAHEAD]))
        s_loc, s_ctx = pending.pop(0)
        key0 = b * seq + HALF_KEY_ROW0[half] * GRID_W
        p_loc, p_ctx = _softmax_weights(s_loc, s_ctx)
        acc = (_weighted_values(vt_ref[rows(hh), key0:key0 + nk], p_loc)
               + _weighted_values(_bf(cvt_ref[b, hh]), p_ctx))
        ot_ref[rows(hh), b * seq + half * nq:b * seq + (half + 1) * nq] = _normalise(acc)


def _lat_attn(qkvt, n_ctx, cache_kt, cache_vt, rpb, layer_idx):
    rpb = jnp.pad(rpb[:, :, ::-1], ((0, 0), (0, 1), (0, LANES - (2 * WIN_W - 1))))
    n = qkvt.shape[1] - n_ctx
    seq = LAT_ROWS * GRID_W
    assert n_ctx % n == 0 and n % seq == 0
    nb = n // seq
    nhp = N_HEADS // 2
    past = cache_kt.shape[4]
    feat = lambda c: pl.BlockSpec((2 * HEAD_DIM, n), lambda hp, c=c: (c * nhp + hp, n_ctx // n))
    cache = pl.BlockSpec((nb, None, 2, HEAD_DIM, past), lambda hp: (0, layer_idx, hp, 0, 0))
    return pl.pallas_call(
        _lat_attn_kernel,
        out_shape=jax.ShapeDtypeStruct((D_MODEL, n), BF16),
        grid=(nhp,),
        in_specs=[feat(0), feat(1), feat(2), cache, cache,
                  pl.BlockSpec((2, 2 * WIN_H, LANES), lambda hp: (hp, 0, 0))],
        out_specs=pl.BlockSpec((2 * HEAD_DIM, n), lambda hp: (hp, 0)),
        scratch_shapes=[pltpu.VMEM((2, LOCAL_ROWS * GRID_W, seq), F32)],
        compiler_params=_params("arbitrary"),
        name="lat_attn",
    )(qkvt, qkvt, qkvt, cache_kt, cache_vt, rpb)


def kernel(x_prompt, x_sample, cache_k, cache_v, c, c_ctx, ada_w, ada_b, ln1_g, ln1_b, ln2_g, ln2_b, ffn_w_in, ffn_w_out, hy_w_in, hy_sconv_w, hy_sconv_b, hy_pe_w1, hy_pe_b1, hy_pe_w2, hy_pe_b2, hy_pe_w3, hy_sin_freq, hy_decay, hy_bias, hy_w_out, na_w_qkv, na_rpb, na_w_out):
    nbp, lp, d = x_prompt.shape
    nbs, ls, _ = x_sample.shape
    assert d == D_MODEL and ls == LAT_ROWS * GRID_W and c.shape[0] == nbs
    xp = x_prompt.reshape(nbp * lp, d)
    xs = x_sample.reshape(nbs * ls, d)
    n_ctx, n_lat = xp.shape[0], xs.shape[0]
    n = n_ctx + n_lat
    ln = (ln1_g, ln1_b, ln2_g, ln2_b)

    filt = (hy_pe_w1[0], hy_pe_b1[0], hy_pe_w2[0], hy_pe_b2[0], hy_pe_w3[0], hy_sin_freq[0], hy_decay[0])
    mods, coef_p, coef_s = _prep(c_ctx, c, ada_w, ada_b, (lp, ls), *filt)

    sconv_b = hy_sconv_b[0].reshape(1, -1)
    hy_bias0 = hy_bias[0].reshape(1, -1)
    x0, vx, ffn_w_in0 = _hy_in(xp, xs, lp, ls, mods, 0, _bf(hy_w_in[0]), hy_sconv_w[0], sconv_b,
                               casts=[_Cast(ffn_w_in, 0, n // ls, _step)])
    steps, lin = _hy_conv_steps(n_ctx)
    u_p, hy_w_out0 = _hy_conv(vx, x0, 0, n_ctx, coef_p, hy_bias0, lp, casts=[_Cast(hy_w_out, 0, steps, lin)])
    steps, lin = _hy_conv_steps(n_lat)
    u_s, w_qkv_t, ffn_w_out0 = _hy_conv(vx, x0, n_ctx, n_lat, coef_s, hy_bias0, ls,
                                        casts=[_Cast(na_w_qkv, 0, steps, lin, transpose=True),
                                               _Cast(ffn_w_out, 0, steps, lin)])
    steps = n // POST_TOKENS
    x1, ffn_w_in1, ffn_w_out1, na_w_out0 = _post(
        (xp, xs), (u_p, u_s), n, n_ctx, ls, mods, 0, hy_w_out0, ffn_w_in0, ffn_w_out0, *ln, name="post0",
        casts=[_Cast(ffn_w_in, 1, steps, _step), _Cast(ffn_w_out, 1, steps, _step),
               _Cast(na_w_out, 0, steps, _step)])

    qkvt, new_kt, new_vt = _qkv(x1, n_ctx, lp, ls, mods, 1, w_qkv_t)
    at_p = _ctx_attn(qkvt, n_ctx, lp)
    at_s = _lat_attn(qkvt, n_ctx, jnp.swapaxes(cache_k, 3, 4), jnp.swapaxes(cache_v, 3, 4), na_rpb[0], 0)
    ffn1 = (na_w_out0, ffn_w_in1, ffn_w_out1, *ln)
    (yp,) = _post((x1,), (at_p,), n_ctx, n_ctx, ls, mods, 1, *ffn1, name="post1_ctx", a_feature_major=True)
    (ys,) = _post((x1,), (at_s,), n_lat, 0, ls, mods, 1, *ffn1, name="post1_lat", x_tok0=n_ctx,
                  a_feature_major=True)

    return (yp.reshape(nbp, lp, d), ys.reshape(nbs, ls, d),
            jnp.swapaxes(new_kt, 3, 4), jnp.swapaxes(new_vt, 3, 4))
```

```python
import functools
import math

import jax
import jax.numpy as jnp
import numpy as np
from jax import lax
from jax.experimental import pallas as pl
from jax.experimental.pallas import tpu as pltpu

D_MODEL = 1024
N_HEADS = 16
HEAD_DIM = D_MODEL // N_HEADS
D_FF = 2816
GRID_W = 64
WIN_H = 8
WIN_W = 16
N_BANDS = 16
PE_DIM = 1 + 2 * N_BANDS
FILT_W = 64
MOD_SHIFT = 0.05
DEPTH = 2
ALPHA = (2 * DEPTH) ** 0.25
LN_EPS = 1e-5
MASK_VALUE = -1e30
MOD_ROWS = 8

LANES = 128
SUBLANES = 8
BF16_ROWS = 2 * SUBLANES
CTX_ATTN_LOOKAHEAD = 4
LAT_ATTN_LOOKAHEAD = 2
LOG2E = math.log2(math.e)
VMEM_LIMIT = 56 * 1024 * 1024

ADA_COLS = 1536
FILTER_CHANNELS = 256
HY_IN_COLS = 256
HY_CONV_TOKENS = 1024
HY_CONV_CHANNELS = 512
POST_TOKENS = 512
POST_ROW_GROUPS = 2
FFN_COLS = 256
QKV_TOKENS = 512

F32 = jnp.float32
BF16 = jnp.bfloat16


def _bf(x):
    return x.astype(BF16)


def _dot(a, b):
    return jnp.dot(a, b, preferred_element_type=F32)


def _dot_nt(a, b):
    return lax.dot_general(a, b, (((1,), (1,)), ((), ())), preferred_element_type=F32)


def _dot_tn(a, b):
    return lax.dot_general(a, b, (((0,), (0,)), ((), ())), preferred_element_type=F32)


def _layer_norm(x, g, b):
    mu = jnp.mean(x, axis=-1, keepdims=True)
    xc = x - mu
    var = jnp.mean(xc * xc, axis=-1, keepdims=True)
    return xc * lax.rsqrt(var + LN_EPS) * g + b


def _params(*sem):
    return pltpu.CompilerParams(dimension_semantics=sem, vmem_limit_bytes=VMEM_LIMIT)


class _Cast:
    def __init__(self, w, layer, steps, step_fn, transpose=False):
        _, r, c = w.shape
        assert r % (steps * BF16_ROWS) == 0
        rows = r // steps
        self.w = w
        self.transpose = transpose
        self.in_spec = pl.BlockSpec((None, rows, c), lambda *g: (layer, step_fn(*g), 0))
        if transpose:
            self.out_shape = jax.ShapeDtypeStruct((c, r), BF16)
            self.out_spec = pl.BlockSpec((c, rows), lambda *g: (0, step_fn(*g)))
        else:
            self.out_shape = jax.ShapeDtypeStruct((r, c), BF16)
            self.out_spec = pl.BlockSpec((rows, c), lambda *g: (step_fn(*g), 0))


def _with_casts(body, n_in, n_out, casts):
    nc = len(casts)

    def kernel_fn(*refs):
        ins, src = refs[:n_in], refs[n_in:n_in + nc]
        outs, dst = refs[n_in + nc:n_in + nc + n_out], refs[n_in + nc + n_out:n_in + 2 * nc + n_out]
        for cast, s, d in zip(casts, src, dst):
            d[...] = _bf(s[...].T if cast.transpose else s[...])
        body(*ins, *outs, *refs[n_in + 2 * nc + n_out:])

    return kernel_fn


def _step(i):
    return i


def _ctx_block(i, ctx_steps):
    return jnp.minimum(i, ctx_steps - 1)


def _lat_block(i, ctx_steps):
    return jnp.maximum(i - ctx_steps, 0)


def _mod_row(i, ctx_steps, tm, lat_len):
    return jnp.where(i < ctx_steps, 0, 1 + ((i - ctx_steps) * tm) // lat_len)


def _pick(is_ctx, refs, idx):
    if len(refs) == 1:
        return refs[0][idx]
    return jnp.where(is_ctx, refs[0][idx], refs[1][idx])


@functools.lru_cache(maxsize=None)
def _plain_dft_tables(L):
    n = 2 * L
    f = np.arange(L)[:, None]
    s = np.arange(L)[None, :]
    ang = 2.0 * np.pi * ((f * s) % n) / n
    c = np.cos(ang)
    sn = np.sin(ang)
    sn[0, :] = np.where(np.arange(L) % 2 == 0, 1.0, -1.0)
    fwd = np.concatenate([c, sn], axis=0)
    w = np.full((2 * L, 1), 2.0 / n)
    w[0, 0] = 1.0 / n
    w[L, 0] = 1.0 / n
    inv = (fwd * w).T
    alt = np.where(np.arange(L) % 2 == 0, 1.0, -1.0).astype(np.float32)[:, None]
    return (np.ascontiguousarray(fwd).astype(np.float32), np.ascontiguousarray(inv).astype(np.float32), alt)


@functools.lru_cache(maxsize=None)
def _dft_tables(L):
    n = 2 * L
    half = L // 2
    f = np.arange(half)[:, None]
    m = np.arange(half)[None, :]
    alt = np.where(np.arange(half) % 2 == 0, 1.0, -1.0)
    w = np.full((L, 1), 2.0 / n)
    w[0, 0] = 1.0 / n
    w[half, 0] = 1.0 / n
    fwd, inv = [], []
    for parity in range(2):
        ang = 2.0 * np.pi * ((f * (2 * m + parity)) % n) / n
        sn = np.sin(ang)
        sn[0, :] = alt
        table = np.concatenate([np.cos(ang), sn], axis=0)
        fwd.append(np.ascontiguousarray(table).astype(np.float32))
        inv.append(np.ascontiguousarray((table * w).T).astype(np.float32))
    return fwd, inv


@functools.lru_cache(maxsize=None)
def _filter_dft_tables(L):
    n = 2 * L
    f = np.arange(L // 2)[:, None]
    t = np.arange(L)[None, :]
    ang = 2.0 * np.pi * ((f * t) % n) / n
    table = np.concatenate([np.cos(ang), np.sin(ang)], axis=0).astype(np.float32)
    tt = np.arange(L)
    cols = np.stack([np.where(tt % 2 == 0, 1.0, -1.0), np.cos(np.pi * (tt % 4) / 2.0),
                     np.sin(np.pi * (tt % 4) / 2.0)], axis=1)
    return np.ascontiguousarray(table), np.round(cols).astype(np.float32)


@functools.lru_cache(maxsize=None)
def _filter_features(L):
    t = np.linspace(0.0, 1.0, L, dtype=np.float32)[:, None]
    w = (np.float32(2.0 * math.pi / L) * np.arange(L, dtype=np.float32))[:, None]
    bands = np.linspace(1e-4, N_BANDS - 1, N_BANDS, dtype=np.float32)[None, :]
    arg = (bands * w).astype(np.float64)
    feats = np.concatenate([t.astype(np.float64), np.cos(arg), np.sin(arg)], axis=-1)
    out = np.zeros((L, LANES), np.float32)
    out[:, :PE_DIM] = feats
    return out, t


HY_SPLIT_MIN_LEN = 512


def _filter_coef_shape(L):
    return (8, L // 2) if L >= HY_SPLIT_MIN_LEN else (3, L)


def _ada_tile(layer, cctx_ref, c_ref, w_ref, b_ref, o_ref, cvec_ref):
    nb = c_ref.shape[0]
    cvec_ref[...] = jnp.zeros_like(cvec_ref)
    cvec_ref[0:1, :] = cctx_ref[...]
    cvec_ref[1:1 + nb, :] = c_ref[...]
    c = cvec_ref[...]
    s = c * jax.nn.sigmoid(c)
    s_hi = _bf(s)
    s_lo = _bf(s - s_hi.astype(F32))
    w = w_ref[...]
    w_hi = _bf(w)
    w_lo = _bf(w - w_hi.astype(F32))
    o_ref[...] = _dot(s_hi, w_hi) + _dot(s_lo, w_hi) + _dot(s_hi, w_lo) + b_ref[pl.ds(layer, 1), :]


def _filter_tile(is_first, feats_ref, t_ref, cols_ref, tab_ref, w1_ref, b1_ref, w2_ref, b2_ref, freq_ref,
                 w3f_ref, w3b_ref, decf_ref, decb_ref, out_ref, h_ref, *, L):
    hi = lax.Precision.HIGHEST

    @pl.when(is_first)
    def _():
        h1 = jnp.sin(freq_ref[0:1, :] * (jnp.dot(feats_ref[...], w1_ref[...], precision=hi,
                                                 preferred_element_type=F32) + b1_ref[...]))
        h_ref[...] = jnp.sin(freq_ref[1:2, :] * (jnp.dot(h1, w2_ref[...], precision=hi,
                                                         preferred_element_type=F32) + b2_ref[...]))

    h = h_ref[...]
    t = t_ref[...]
    kf = jnp.dot(h, w3f_ref[...], precision=hi, preferred_element_type=F32) * (
        jnp.exp(-t * jnp.abs(decf_ref[...])) + MOD_SHIFT)
    kb = jnp.dot(h, w3b_ref[...], precision=hi, preferred_element_type=F32) * (
        jnp.exp(-t * jnp.abs(decb_ref[...])) + MOD_SHIFT)
    row = lax.broadcasted_iota(jnp.int32, kf.shape, 0)
    kb = jnp.where(row == 0, 0.0, kb)
    ksum = kf + kb
    kdiff = kf - kb
    alt = cols_ref[:, 0:1]

    def spectrum(table, x):
        hi = _bf(x)
        return _dot(table, hi) + _dot(table, _bf(x - hi.astype(F32)))

    if L < HY_SPLIT_MIN_LEN:
        k_c = spectrum(tab_ref[0:L, :], ksum)
        k_s = spectrum(tab_ref[L:2 * L, :], kdiff)
        k_nyq = jnp.sum(ksum * alt, axis=0, keepdims=True)
        out_ref[0] = k_c
        out_ref[1] = jnp.where(row == 0, 0.0, k_s)
        out_ref[2] = jnp.where(row == 0, k_nyq, k_c)
        return

    half = L // 2
    cos_m = tab_ref[0:half, :]
    sin_m = tab_ref[half:L, :]
    k_c = spectrum(cos_m, ksum)
    k_s = spectrum(sin_m, kdiff)
    u_c = spectrum(cos_m, alt * ksum)
    u_s = -spectrum(sin_m, alt * kdiff)
    kc_h = jnp.sum(ksum * cols_ref[:, 1:2], axis=0, keepdims=True)
    ks_h = jnp.sum(kdiff * cols_ref[:, 2:3], axis=0, keepdims=True)
    first = lax.broadcasted_iota(jnp.int32, k_c.shape, 0) == 0
    out_ref[0] = k_c
    out_ref[1] = k_s
    out_ref[2] = u_c
    out_ref[3] = u_s
    out_ref[4] = jnp.where(first, kc_h - ks_h, k_c)
    out_ref[5] = jnp.where(first, kc_h + ks_h, u_c)
    out_ref[6] = jnp.where(first, kc_h + ks_h, k_c)
    out_ref[7] = jnp.where(first, kc_h - ks_h, u_c)


def _prep_kernel(cctx_ref, c_ref, w_ref, b_ref, fa_ref, ta_ref, ca_ref, tba_ref, fb_ref, tb_ref, cb_ref,
                 tbb_ref, w1_ref, b1_ref, w2_ref, b2_ref, freq_ref, w3f_ref, w3b_ref, decf_ref, decb_ref,
                 mod_ref, coefa_ref, coefb_ref, cvec_ref, ha_ref, hb_ref, *, lens, tiles):
    s = pl.program_id(0)
    _ada_tile(s // tiles, cctx_ref, c_ref, w_ref, b_ref, mod_ref, cvec_ref)
    shared = (w1_ref, b1_ref, w2_ref, b2_ref, freq_ref, w3f_ref, w3b_ref, decf_ref, decb_ref)

    @pl.when(s < tiles)
    def _():
        _filter_tile(s == 0, fa_ref, ta_ref, ca_ref, tba_ref, *shared, coefa_ref, ha_ref, L=lens[0])

    @pl.when(s >= tiles)
    def _():
        _filter_tile(s == tiles, fb_ref, tb_ref, cb_ref, tbb_ref, *shared, coefb_ref, hb_ref, L=lens[1])


def _prep(c_ctx, c, ada_w, ada_b, lens, pe_w1, pe_b1, pe_w2, pe_b2, pe_w3, sin_freq, decay, casts=()):
    assert 1 + c.shape[0] <= MOD_ROWS and DEPTH == 2
    tn = ADA_COLS
    dt = FILTER_CHANNELS
    tiles = D_MODEL // dt
    assert 6 * D_MODEL // tn == tiles
    full = lambda shape: pl.BlockSpec(shape, lambda s: (0,) * len(shape))
    consts, const_specs, out_shapes, out_specs, scratch = [], [], [], [], []
    for k, L in enumerate(lens):
        feats, t = _filter_features(L)
        if L >= HY_SPLIT_MIN_LEN:
            table, cols = _filter_dft_tables(L)
        else:
            table, _, cols = _plain_dft_tables(L)
        consts += [feats, t, cols, _bf(jnp.asarray(table))]
        const_specs += [full(feats.shape), full(t.shape), full(cols.shape), full(table.shape)]
        nc, nf = _filter_coef_shape(L)
        out_shapes.append(jax.ShapeDtypeStruct((nc, nf, D_MODEL), F32))
        block = (lambda s: (0, 0, jnp.minimum(s, tiles - 1))) if k == 0 else (
            lambda s: (0, 0, jnp.maximum(s - tiles, 0)))
        out_specs.append(pl.BlockSpec((nc, nf, dt), block))
        scratch.append(pltpu.VMEM((L, LANES), F32))
    pad = LANES - FILT_W
    w1 = jnp.pad(pe_w1, ((0, LANES - PE_DIM), (0, pad)))
    b1 = jnp.pad(pe_b1, (0, pad)).reshape(1, LANES)
    w2 = jnp.pad(pe_w2, ((0, pad), (0, pad)))
    b2 = jnp.pad(pe_b2, (0, pad)).reshape(1, LANES)
    w3 = jnp.pad(pe_w3, ((0, pad), (0, 0)))
    freq = jnp.pad(sin_freq, ((0, 0), (0, pad)))
    dec = decay.reshape(1, 2 * D_MODEL)
    return pl.pallas_call(
        _with_casts(functools.partial(_prep_kernel, lens=tuple(lens), tiles=tiles), 21, 3, casts),
        out_shape=(jax.ShapeDtypeStruct((DEPTH, MOD_ROWS, 6 * D_MODEL), F32), *out_shapes,
                   *[k.out_shape for k in casts]),
        grid=(DEPTH * tiles,),
        in_specs=[
            full((1, D_MODEL)), full(c.shape),
            pl.BlockSpec((None, D_MODEL, tn), lambda s: (s // tiles, 0, s % tiles)),
            pl.BlockSpec((DEPTH, tn), lambda s: (0, s % tiles)),
            *const_specs,
            full((LANES, LANES)), full((1, LANES)), full((LANES, LANES)), full((1, LANES)), full((2, LANES)),
            pl.BlockSpec((LANES, dt), lambda s: (0, s % tiles)),
            pl.BlockSpec((LANES, dt), lambda s: (0, tiles + s % tiles)),
            pl.BlockSpec((1, dt), lambda s: (0, s % tiles)),
            pl.BlockSpec((1, dt), lambda s: (0, tiles + s % tiles)),
            *[k.in_spec for k in casts],
        ],
        out_specs=(pl.BlockSpec((None, MOD_ROWS, tn), lambda s: (s // tiles, 0, s % tiles)), *out_specs,
                   *[k.out_spec for k in casts]),
        scratch_shapes=[pltpu.VMEM((MOD_ROWS, D_MODEL), F32), *scratch],
        compiler_params=_params("arbitrary"),
        name="prep",
    )(c_ctx.reshape(1, D_MODEL), c, ada_w, ada_b, *consts, w1, b1, w2, b2, freq, w3, w3, dec, dec,
      *[k.w for k in casts])


def _hy_in_kernel(xc_ref, xl_ref, mod_ref, w_ref, cw_ref, cb_ref, x0_ref, vx_ref, *, ctx_steps, ctx_len, cn):
    i = pl.program_id(0)
    is_ctx = i < ctx_steps
    tm = x0_ref.shape[0]
    mod = mod_ref[pl.ds(_mod_row(i, ctx_steps, tm, tm), 1), :]
    x = _pick(is_ctx, (xc_ref, xl_ref), (slice(None), slice(None)))
    h = _bf(x * (1.0 + mod[:, D_MODEL:2 * D_MODEL]) + mod[:, 0:D_MODEL])
    row = lax.broadcasted_iota(jnp.int32, (SUBLANES, cn), 0)

    def conv(col):
        z = _dot(h, w_ref[:, col:col + cn])
        zm = pltpu.roll(z, 1, 0)
        zp = pltpu.roll(z, tm - 1, 0)
        w0 = cw_ref[0:1, col:col + cn]
        w1 = cw_ref[1:2, col:col + cn]
        w2 = cw_ref[2:3, col:col + cn]
        b = cb_ref[:, col:col + cn]
        out = zm * w0 + z * w1 + zp * w2 + b
        pieces = []
        for s in range(tm // ctx_len):
            lo = s * ctx_len
            hi = lo + ctx_len - SUBLANES
            head = (jnp.where(row == 0, 0.0, zm[lo:lo + SUBLANES]) * w0 + z[lo:lo + SUBLANES] * w1
                    + zp[lo:lo + SUBLANES] * w2 + b)
            tail = (zm[hi:hi + SUBLANES] * w0 + z[hi:hi + SUBLANES] * w1
                    + jnp.where(row == SUBLANES - 1, 0.0, zp[hi:hi + SUBLANES]) * w2 + b)
            if lo > 0:
                head = jnp.where(is_ctx, head, out[lo:lo + SUBLANES])
            if hi + SUBLANES < tm:
                tail = jnp.where(is_ctx, tail, out[hi:hi + SUBLANES])
            pieces += [head, out[lo + SUBLANES:hi], tail]
        return jnp.concatenate(pieces, axis=0)

    for c in range(D_MODEL // cn):
        x0_ref[:, c * cn:(c + 1) * cn] = _bf(conv(c * cn))
        x1 = conv(D_MODEL + c * cn)
        v = conv(2 * D_MODEL + c * cn)
        vx_ref[:, c * cn:(c + 1) * cn] = _bf(v * x1)


def _hy_in(x_ctx, x_lat, ctx_len, lat_len, mods, layer, w_in, sconv_w, sconv_b, casts=()):
    tm = lat_len
    assert tm % ctx_len == 0 and x_ctx.shape[0] % tm == 0
    ctx_steps = x_ctx.shape[0] // tm
    n = x_ctx.shape[0] + x_lat.shape[0]
    tok = pl.BlockSpec((tm, D_MODEL), lambda i: (i, 0))
    const = lambda shape: pl.BlockSpec(shape, lambda i: (0, 0), pipeline_mode=pl.Buffered(1))
    act = jax.ShapeDtypeStruct((n, D_MODEL), BF16)
    body = functools.partial(_hy_in_kernel, ctx_steps=ctx_steps, ctx_len=ctx_len, cn=HY_IN_COLS)
    return pl.pallas_call(
        _with_casts(body, 6, 2, casts),
        out_shape=(act, act) + tuple(c.out_shape for c in casts),
        grid=(n // tm,),
        in_specs=[
            pl.BlockSpec((tm, D_MODEL), lambda i: (_ctx_block(i, ctx_steps), 0)),
            pl.BlockSpec((tm, D_MODEL), lambda i: (_lat_block(i, ctx_steps), 0)),
            pl.BlockSpec((None, MOD_ROWS, 6 * D_MODEL), lambda i: (layer, 0, 0)),
            const((D_MODEL, 3 * D_MODEL)),
            pl.BlockSpec((3, 3 * D_MODEL), lambda i: (0, 0)),
            pl.BlockSpec((1, 3 * D_MODEL), lambda i: (0, 0)),
        ] + [c.in_spec for c in casts],
        out_specs=(tok, tok) + tuple(c.out_spec for c in casts),
        compiler_params=_params("arbitrary"),
        name="hyena_in",
    )(x_ctx, x_lat, mods, w_in, sconv_w, sconv_b, *[c.w for c in casts])


def _hy_conv_plain_kernel(vx_ref, x0_ref, coef_ref, bias_ref, fwd_ref, inv_ref, u_ref, *, L, nseq, cn):
    for c in range(vx_ref.shape[1] // cn):
        cols = slice(c * cn, (c + 1) * cn)
        k_cos = coef_ref[0, :, cols]
        k_sin = coef_ref[1, :, cols]
        k_mix = coef_ref[2, :, cols]
        for s in range(nseq):
            rows = slice(s * L, (s + 1) * L)
            vx = vx_ref[rows, cols]
            spec = _dot(fwd_ref[...], vx)
            v_cos = spec[0:L]
            v_sin = spec[L:2 * L]
            y_spec = _bf(jnp.concatenate([v_cos * k_cos - v_sin * k_sin, v_cos * k_sin + v_sin * k_mix],
                                         axis=0))
            y = _dot(inv_ref[...], y_spec)
            u = (y + bias_ref[:, cols] * vx.astype(F32)) * x0_ref[rows, cols].astype(F32)
            u_ref[rows, cols] = _bf(u)


def _hy_conv_split_kernel(vx_ref, x0_ref, coef_ref, bias_ref, fe_ref, fo_ref, ge_ref, go_ref, u_ref,
                          sv_ref, sx_ref, su_ref, *, L, nseq, cn):
    half = L // 2
    for g in range(vx_ref.shape[1] // LANES):
        lanes = slice(g * LANES, (g + 1) * LANES)
        sv_ref[g] = vx_ref[:, lanes].astype(F32)
        sx_ref[g] = x0_ref[:, lanes].astype(F32)

    def load(ref, start, c):
        return jnp.concatenate([ref[g, pl.ds(start, half, stride=2), :]
                                for g in range(c * cn // LANES, (c + 1) * cn // LANES)], axis=1)

    def store(ref, start, c, val):
        for k, g in enumerate(range(c * cn // LANES, (c + 1) * cn // LANES)):
            ref[g, pl.ds(start, half, stride=2), :] = val[:, k * LANES:(k + 1) * LANES]

    for c in range(vx_ref.shape[1] // cn):
        cols = slice(c * cn, (c + 1) * cn)
        k_c, k_s, u_c, u_s, e2, e3, f2, f3 = [coef_ref[k, :, cols] for k in range(coef_ref.shape[0])]
        for s in range(nseq):
            v_e = load(sv_ref, s * L, c)
            v_o = load(sv_ref, s * L + 1, c)
            a = _dot(fe_ref[...], _bf(v_e))
            b = _dot(fo_ref[...], _bf(v_o))
            s_c, s_s = a[0:half] + b[0:half], a[half:L] + b[half:L]
            d_c, d_s = a[0:half] - b[0:half], a[half:L] - b[half:L]
            t1 = s_c * k_c - s_s * k_s
            t2 = d_c * u_c + d_s * u_s
            q1 = s_c * k_s
            q4 = d_c * u_s
            p_spec = jnp.concatenate([t1 + t2, q1 + s_s * e2 + d_s * e3 - q4], axis=0)
            m_spec = jnp.concatenate([t1 - t2, q1 + s_s * f2 - d_s * f3 + q4], axis=0)
            y_e = _dot(ge_ref[...], _bf(p_spec))
            y_o = _dot(go_ref[...], _bf(m_spec))
            bias = bias_ref[:, cols]
            store(su_ref, s * L, c, (y_e + bias * v_e) * load(sx_ref, s * L, c))
            store(su_ref, s * L + 1, c, (y_o + bias * v_o) * load(sx_ref, s * L + 1, c))
            rows = slice(s * L, (s + 1) * L)
            for g in range(c * cn // LANES, (c + 1) * cn // LANES):
                u_ref[rows, g * LANES:(g + 1) * LANES] = _bf(su_ref[g, rows, :])


def _hy_conv_steps(n):
    nt = n // HY_CONV_TOKENS
    return (D_MODEL // HY_CONV_CHANNELS) * nt, lambda j, i: j * nt + i


def _hy_conv(vx, x0, tok0, n, coef, bias, L, casts=()):
    tm = HY_CONV_TOKENS
    dt = HY_CONV_CHANNELS
    assert tok0 % tm == 0 and n % tm == 0 and tm % L == 0
    src = pl.BlockSpec((tm, dt), lambda j, i: (tok0 // tm + i, j))
    dst = pl.BlockSpec((tm, dt), lambda j, i: (i, j))
    const = lambda shape: pl.BlockSpec(shape, lambda j, i: (0, 0), pipeline_mode=pl.Buffered(1))
    if L >= HY_SPLIT_MIN_LEN:
        fwd, inv = _dft_tables(L)
        tables = (*fwd, *inv)
        body = functools.partial(_hy_conv_split_kernel, L=L, nseq=tm // L, cn=dt)
        scratch = [pltpu.VMEM((dt // LANES, tm, LANES), F32)] * 3
    else:
        tables = _plain_dft_tables(L)[:2]
        body = functools.partial(_hy_conv_plain_kernel, L=L, nseq=tm // L, cn=256)
        scratch = []
    return pl.pallas_call(
        _with_casts(body, 4 + len(tables), 1, casts),
        out_shape=(jax.ShapeDtypeStruct((n, D_MODEL), BF16),) + tuple(c.out_shape for c in casts),
        grid=(D_MODEL // dt, n // tm),
        in_specs=[
            src, src,
            pl.BlockSpec(_filter_coef_shape(L) + (dt,), lambda j, i: (0, 0, j)),
            pl.BlockSpec((1, dt), lambda j, i: (0, j)),
        ] + [const(t.shape) for t in tables] + [c.in_spec for c in casts],
        out_specs=(dst,) + tuple(c.out_spec for c in casts),
        scratch_shapes=scratch,
        compiler_params=_params("arbitrary", "arbitrary"),
        name=f"hyena_conv_{L}",
    )(vx, x0, coef, bias, *[_bf(jnp.asarray(t)) for t in tables], *[c.w for c in casts])


def _post_kernel(*refs, layer, ctx_steps, lat_len, n_x, n_a, a_feature_major):
    x_refs, a_refs = refs[:n_x], refs[n_x:n_x + n_a]
    (mod_ref, wo_ref, ln1g_ref, ln1b_ref, win_ref, wout_ref, ln2g_ref, ln2b_ref, o_ref,
     act_ref) = refs[n_x + n_a:]
    d = D_MODEL
    i = pl.program_id(0)
    is_ctx = i < ctx_steps
    tm = act_ref.shape[0]
    mod = mod_ref[pl.ds(_mod_row(i, ctx_steps, tm, lat_len), 1), :]
    g1 = mod[:, 2 * d:3 * d]
    sh2 = mod[:, 3 * d:4 * d]
    sc2 = mod[:, 4 * d:5 * d]
    g2 = mod[:, 5 * d:6 * d]
    ln1 = (ln1g_ref[layer:layer + 1, :], ln1b_ref[layer:layer + 1, :])
    ln2 = (ln2g_ref[layer:layer + 1, :], ln2b_ref[layer:layer + 1, :])

    gr = tm // POST_ROW_GROUPS
    groups = [slice(r * gr, (r + 1) * gr) for r in range(POST_ROW_GROUPS)]
    if a_feature_major:
        mix = [_dot_tn(_pick(is_ctx, a_refs, (slice(None), rows)), wo_ref[...]) for rows in groups]
    else:
        mix = [_dot(_pick(is_ctx, a_refs, (rows, slice(None))), wo_ref[...]) for rows in groups]
    x1 = []
    for rows, mx in zip(groups, mix):
        x = _pick(is_ctx, x_refs, (rows, slice(None)))
        x1.append(_layer_norm(ALPHA * x + g1 * mx, *ln1))
        h = _bf(x1[-1] * (1.0 + sc2) + sh2)
        for c in range(D_FF // FFN_COLS):
            cols = slice(c * FFN_COLS, (c + 1) * FFN_COLS)
            gate = _dot(h, win_ref[:, cols])
            up = _dot(h, win_ref[:, D_FF + c * FFN_COLS:D_FF + (c + 1) * FFN_COLS])
            act_ref[rows, cols] = _bf(gate * jax.nn.sigmoid(gate) * up)
    for rows, xr in zip(groups, x1):
        f = _dot(act_ref[rows, :], wout_ref[...])
        o_ref[rows, :] = _layer_norm(ALPHA * xr + g2 * f, *ln2)


def _post(xs, mixes, n, n_ctx, lat_len, mods, layer, w_o, w_in, w_out, ln1_g, ln1_b, ln2_g, ln2_b, name,
          x_tok0=0, a_feature_major=False, casts=()):
    tm = POST_TOKENS
    ctx_steps = n_ctx // tm
    ctx = lambda i: _ctx_block(i, ctx_steps)
    lat = lambda i: _lat_block(i, ctx_steps)
    blocks = lambda m, tok0: (lambda i: tok0 // tm + i,) if m == 1 else (ctx, lat)
    tok = lambda f: pl.BlockSpec((tm, D_MODEL), lambda i: (f(i), 0))
    feat = lambda f: pl.BlockSpec((D_MODEL, tm), lambda i: (0, f(i)))
    const = lambda shape: pl.BlockSpec(shape, lambda i: (0, 0), pipeline_mode=pl.Buffered(1))
    vec = pl.BlockSpec((DEPTH, D_MODEL), lambda i: (0, 0))
    body = functools.partial(_post_kernel, layer=layer, ctx_steps=ctx_steps, lat_len=lat_len, n_x=len(xs),
                             n_a=len(mixes), a_feature_major=a_feature_major)
    return pl.pallas_call(
        _with_casts(body, len(xs) + len(mixes) + 8, 1, casts),
        out_shape=(jax.ShapeDtypeStruct((n, D_MODEL), F32),) + tuple(c.out_shape for c in casts),
        grid=(n // tm,),
        in_specs=[tok(f) for f in blocks(len(xs), x_tok0)]
        + [(feat if a_feature_major else tok)(f) for f in blocks(len(mixes), 0)]
        + [pl.BlockSpec((None, MOD_ROWS, 6 * D_MODEL), lambda i: (layer, 0, 0)),
           const((D_MODEL, D_MODEL)), vec, vec,
           const((D_MODEL, 2 * D_FF)), const((D_FF, D_MODEL)), vec, vec]
        + [c.in_spec for c in casts],
        out_specs=(tok(lambda i: i),) + tuple(c.out_spec for c in casts),
        scratch_shapes=[pltpu.VMEM((tm, D_FF), BF16)],
        compiler_params=_params("arbitrary"),
        name=name,
    )(*xs, *mixes, mods, w_o, ln1_g, ln1_b, w_in, w_out, ln2_g, ln2_b, *[c.w for c in casts])


def _qkv_kernel(x_ref, mod_ref, wt_ref, qkvt_ref, nk_ref, nv_ref, *, ctx_steps, lat_len):
    d = D_MODEL
    i = pl.program_id(0)
    tm = x_ref.shape[0]
    mod = mod_ref[pl.ds(_mod_row(i, ctx_steps, tm, lat_len), 1), :]
    h = _bf(x_ref[...] * (1.0 + mod[:, d:2 * d]) + mod[:, 0:d])
    qkvt_ref[0:d, :] = _bf(_dot_nt(wt_ref[0:d, :], h) * (HEAD_DIM ** -0.5 * LOG2E))
    kt = _dot_nt(wt_ref[d:2 * d, :], h)
    qkvt_ref[d:2 * d, :] = _bf(kt)
    vt = _dot_nt(wt_ref[2 * d:3 * d, :], h)
    qkvt_ref[2 * d:3 * d, :] = _bf(vt)

    @pl.when(i < ctx_steps)
    def _():
        seq = nk_ref.shape[4]
        for s in range(tm // seq):
            nk_ref[s, 0] = kt[:, s * seq:(s + 1) * seq].reshape(N_HEADS, HEAD_DIM, seq)
            nv_ref[s, 0] = vt[:, s * seq:(s + 1) * seq].reshape(N_HEADS, HEAD_DIM, seq)


def _qkv(x2d, n_ctx, ctx_len, lat_len, mods, layer, w_qkv_t):
    n = x2d.shape[0]
    tm = QKV_TOKENS
    assert tm % ctx_len == 0 and n_ctx % tm == 0
    ctx_steps = n_ctx // tm
    kv_shape = jax.ShapeDtypeStruct((n_ctx // ctx_len, 1, N_HEADS, HEAD_DIM, ctx_len), F32)
    kv_spec = pl.BlockSpec((tm // ctx_len, 1, N_HEADS, HEAD_DIM, ctx_len),
                           lambda i: (_ctx_block(i, ctx_steps), 0, 0, 0, 0))
    return pl.pallas_call(
        functools.partial(_qkv_kernel, ctx_steps=ctx_steps, lat_len=lat_len),
        out_shape=(jax.ShapeDtypeStruct((3 * D_MODEL, n), BF16), kv_shape, kv_shape),
        grid=(n // tm,),
        in_specs=[
            pl.BlockSpec((tm, D_MODEL), lambda i: (i, 0)),
            pl.BlockSpec((None, MOD_ROWS, 6 * D_MODEL), lambda i: (layer, 0, 0)),
            pl.BlockSpec((3 * D_MODEL, D_MODEL), lambda i: (0, 0), pipeline_mode=pl.Buffered(1)),
        ],
        out_specs=(pl.BlockSpec((3 * D_MODEL, tm), lambda i: (0, i)), kv_spec, kv_spec),
        compiler_params=_params("arbitrary"),
        name="qkv",
    )(x2d, mods, w_qkv_t)


def _softmax_weights(*scores):
    m = functools.reduce(jnp.maximum, [jnp.max(s, axis=0, keepdims=True) for s in scores])
    return [_bf(jnp.exp2(s - m)) for s in scores]


def _weighted_values(vt, p):
    ones = jnp.ones((BF16_ROWS, vt.shape[1]), BF16)
    return _dot(jnp.concatenate([vt, ones], axis=0), p)


def _normalise(acc):
    return _bf(acc[0:HEAD_DIM] * (1.0 / acc[HEAD_DIM:HEAD_DIM + 1]))


def _ctx_attn_kernel(qt_ref, kt_ref, vt_ref, ot_ref):
    def rows(hh):
        return slice(hh * HEAD_DIM, (hh + 1) * HEAD_DIM)

    def scores(hh):
        return _dot_tn(kt_ref[rows(hh), :], qt_ref[rows(hh), :])

    pending = [scores(hh) for hh in range(CTX_ATTN_LOOKAHEAD)]
    for hh in range(N_HEADS):
        if hh + CTX_ATTN_LOOKAHEAD < N_HEADS:
            pending.append(scores(hh + CTX_ATTN_LOOKAHEAD))
        (p,) = _softmax_weights(pending.pop(0))
        ot_ref[rows(hh), :] = _normalise(_weighted_values(vt_ref[rows(hh), :], p))


def _ctx_attn(qkvt, n_ctx, seq):
    return pl.pallas_call(
        _ctx_attn_kernel,
        out_shape=jax.ShapeDtypeStruct((D_MODEL, n_ctx), BF16),
        grid=(n_ctx // seq,),
        in_specs=[pl.BlockSpec((D_MODEL, seq), lambda b, c=c: (c, b)) for c in range(3)],
        out_specs=pl.BlockSpec((D_MODEL, seq), lambda b: (0, b)),
        compiler_params=_params("arbitrary"),
        name="ctx_attn",
    )(qkvt, qkvt, qkvt)


LAT_ROWS = 16
Q_BLOCK_ROWS = 4
KEY_TILE_ROWS = 4


def _key_band(qb):
    row0 = [min(max(r - WIN_H // 2, 0), LAT_ROWS - WIN_H) for r in range(qb * Q_BLOCK_ROWS, (qb + 1) * Q_BLOCK_ROWS)]
    lo = min(row0) // KEY_TILE_ROWS * KEY_TILE_ROWS
    hi = -(-(max(row0) + WIN_H) // KEY_TILE_ROWS) * KEY_TILE_ROWS
    return lo, hi


KEY_BANDS = tuple(_key_band(qb) for qb in range(LAT_ROWS // Q_BLOCK_ROWS))
MAX_BAND_ROWS = max(hi - lo for lo, hi in KEY_BANDS)


def _fill_bias_table(rpb_ref, tab_ref):
    shape = (GRID_W, LANES)
    kc = lax.broadcasted_iota(jnp.int32, shape, 0)
    lane = lax.broadcasted_iota(jnp.int32, shape, 1)
    qc = lane % GRID_W
    win0 = jnp.clip(qc - WIN_W // 2, 0, GRID_W - WIN_W)
    col_ok = (kc >= win0) & (kc < win0 + WIN_W)
    masked = jnp.full(shape, MASK_VALUE, F32)
    tiles = []
    for dr in range(2 * WIN_H - 1):
        row = jnp.broadcast_to(rpb_ref[dr:dr + 1, :] * LOG2E, shape)
        t = pltpu.roll(row, LANES - (WIN_W - 1), 1, stride=1, stride_axis=0)
        t = jnp.where(lane < GRID_W, t, pltpu.roll(t, GRID_W, 1))
        tiles.append(jnp.where(col_ok, t, MASK_VALUE))

    def tile(qr, kr):
        row0 = min(max(qr - WIN_H // 2, 0), LAT_ROWS - WIN_H)
        if row0 <= kr < row0 + WIN_H:
            return tiles[kr - qr + WIN_H - 1]
        return masked

    for qr in range(0, LAT_ROWS, 2):
        lo, hi = KEY_BANDS[qr // Q_BLOCK_ROWS]
        for i in range(hi - lo):
            both = jnp.where(lane < GRID_W, tile(qr, lo + i), tile(qr + 1, lo + i))
            tab_ref[i * GRID_W:(i + 1) * GRID_W, qr * GRID_W:(qr + 2) * GRID_W] = both


def _lat_attn_kernel(qt_ref, kt_ref, vt_ref, ckt_ref, cvt_ref, rpb_ref, ot_ref, tab_ref):
    seq = LAT_ROWS * GRID_W
    nq = Q_BLOCK_ROWS * GRID_W
    heads = qt_ref.shape[0] // HEAD_DIM
    units = [(b, hh, qb) for b in range(qt_ref.shape[1] // seq) for hh in range(heads)
             for qb in range(len(KEY_BANDS))]

    def rows(hh):
        return slice(hh * HEAD_DIM, (hh + 1) * HEAD_DIM)

    def keys(b, qb):
        lo, hi = KEY_BANDS[qb]
        return slice(b * seq + lo * GRID_W, b * seq + hi * GRID_W)

    def queries(b, qb):
        return slice(b * seq + qb * nq, b * seq + (qb + 1) * nq)

    def scores(unit):
        b, hh, qb = unit
        lo, hi = KEY_BANDS[qb]
        qt = qt_ref[rows(hh), queries(b, qb)]
        s_loc = (_dot_tn(kt_ref[rows(hh), keys(b, qb)], qt)
                 + tab_ref[hh, 0:(hi - lo) * GRID_W, qb * nq:(qb + 1) * nq])
        s_ctx = _dot_tn(_bf(ckt_ref[b, hh]), qt)
        return s_loc, s_ctx

    for hh in range(tab_ref.shape[0]):
        _fill_bias_table(rpb_ref.at[hh], tab_ref.at[hh])
    pending = [scores(u) for u in units[:LAT_ATTN_LOOKAHEAD]]
    for i, (b, hh, qb) in enumerate(units):
        if i + LAT_ATTN_LOOKAHEAD < len(units):
            pending.append(scores(units[i + LAT_ATTN_LOOKAHEAD]))
        s_loc, s_ctx = pending.pop(0)
        p_loc, p_ctx = _softmax_weights(s_loc, s_ctx)
        acc = (_weighted_values(vt_ref[rows(hh), keys(b, qb)], p_loc)
               + _weighted_values(_bf(cvt_ref[b, hh]), p_ctx))
        ot_ref[rows(hh), queries(b, qb)] = _normalise(acc)


def _lat_attn(qkvt, n_ctx, cache_kt, cache_vt, rpb, layer_idx):
    rpb = jnp.pad(rpb[:, :, ::-1], ((0, 0), (0, 1), (0, LANES - (2 * WIN_W - 1))))
    n = qkvt.shape[1] - n_ctx
    seq = LAT_ROWS * GRID_W
    assert n_ctx % n == 0 and n % seq == 0
    nb = n // seq
    nhp = N_HEADS // 2
    past = cache_kt.shape[4]
    feat = lambda c: pl.BlockSpec((2 * HEAD_DIM, n), lambda hp, c=c: (c * nhp + hp, n_ctx // n))
    cache = pl.BlockSpec((nb, None, 2, HEAD_DIM, past), lambda hp: (0, layer_idx, hp, 0, 0))
    return pl.pallas_call(
        _lat_attn_kernel,
        out_shape=jax.ShapeDtypeStruct((D_MODEL, n), BF16),
        grid=(nhp,),
        in_specs=[feat(0), feat(1), feat(2), cache, cache,
                  pl.BlockSpec((2, 2 * WIN_H, LANES), lambda hp: (hp, 0, 0))],
        out_specs=pl.BlockSpec((2 * HEAD_DIM, n), lambda hp: (hp, 0)),
        scratch_shapes=[pltpu.VMEM((2, MAX_BAND_ROWS * GRID_W, seq), F32)],
        compiler_params=_params("arbitrary"),
        name="lat_attn",
    )(qkvt, qkvt, qkvt, cache_kt, cache_vt, rpb)


def kernel(x_prompt, x_sample, cache_k, cache_v, c, c_ctx, ada_w, ada_b, ln1_g, ln1_b, ln2_g, ln2_b, ffn_w_in, ffn_w_out, hy_w_in, hy_sconv_w, hy_sconv_b, hy_pe_w1, hy_pe_b1, hy_pe_w2, hy_pe_b2, hy_pe_w3, hy_sin_freq, hy_decay, hy_bias, hy_w_out, na_w_qkv, na_rpb, na_w_out):
    nbp, lp, d = x_prompt.shape
    nbs, ls, _ = x_sample.shape
    assert d == D_MODEL and ls == LAT_ROWS * GRID_W and c.shape[0] == nbs
    xp = x_prompt.reshape(nbp * lp, d)
    xs = x_sample.reshape(nbs * ls, d)
    n_ctx, n_lat = xp.shape[0], xs.shape[0]
    n = n_ctx + n_lat
    ln = (ln1_g, ln1_b, ln2_g, ln2_b)

    filt = (hy_pe_w1[0], hy_pe_b1[0], hy_pe_w2[0], hy_pe_b2[0], hy_pe_w3[0], hy_sin_freq[0], hy_decay[0])
    mods, coef_p, coef_s, hy_w_in0 = _prep(c_ctx, c, ada_w, ada_b, (lp, ls), *filt,
                                           casts=[_Cast(hy_w_in, 0, DEPTH * D_MODEL // FILTER_CHANNELS, _step)])

    sconv_b = hy_sconv_b[0].reshape(1, -1)
    hy_bias0 = hy_bias[0].reshape(1, -1)
    x0, vx, ffn_w_in0 = _hy_in(xp, xs, lp, ls, mods, 0, hy_w_in0, hy_sconv_w[0], sconv_b,
                               casts=[_Cast(ffn_w_in, 0, n // ls, _step)])
    steps, lin = _hy_conv_steps(n_ctx)
    u_p, hy_w_out0 = _hy_conv(vx, x0, 0, n_ctx, coef_p, hy_bias0, lp, casts=[_Cast(hy_w_out, 0, steps, lin)])
    steps, lin = _hy_conv_steps(n_lat)
    u_s, w_qkv_t, ffn_w_out0 = _hy_conv(vx, x0, n_ctx, n_lat, coef_s, hy_bias0, ls,
                                        casts=[_Cast(na_w_qkv, 0, steps, lin, transpose=True),
                                               _Cast(ffn_w_out, 0, steps, lin)])
    steps = n // POST_TOKENS
    x1, ffn_w_in1, ffn_w_out1, na_w_out0 = _post(
        (xp, xs), (u_p, u_s), n, n_ctx, ls, mods, 0, hy_w_out0, ffn_w_in0, ffn_w_out0, *ln, name="post0",
        casts=[_Cast(ffn_w_in, 1, steps, _step), _Cast(ffn_w_out, 1, steps, _step),
               _Cast(na_w_out, 0, steps, _step)])

    qkvt, new_kt, new_vt = _qkv(x1, n_ctx, lp, ls, mods, 1, w_qkv_t)
    at_p = _ctx_attn(qkvt, n_ctx, lp)
    at_s = _lat_attn(qkvt, n_ctx, jnp.swapaxes(cache_k, 3, 4), jnp.swapaxes(cache_v, 3, 4), na_rpb[0], 0)
    ffn1 = (na_w_out0, ffn_w_in1, ffn_w_out1, *ln)
    (yp,) = _post((x1,), (at_p,), n_ctx, n_ctx, ls, mods, 1, *ffn1, name="post1_ctx", a_feature_major=True)
    (ys,) = _post((x1,), (at_s,), n_lat, 0, ls, mods, 1, *ffn1, name="post1_lat", x_tok0=n_ctx,
                  a_feature_major=True)

    return (yp.reshape(nbp, lp, d), ys.reshape(nbs, ls, d),
            jnp.swapaxes(new_kt, 3, 4), jnp.swapaxes(new_vt, 3, 4))
```

```python
import functools
import math

import jax
import jax.numpy as jnp
import numpy as np
from jax import lax
from jax.experimental import pallas as pl
from jax.experimental.pallas import tpu as pltpu

D_MODEL = 1024
N_HEADS = 16
HEAD_DIM = D_MODEL // N_HEADS
D_FF = 2816
GRID_W = 64
WIN_H = 8
WIN_W = 16
N_BANDS = 16
PE_DIM = 1 + 2 * N_BANDS
FILT_W = 64
MOD_SHIFT = 0.05
DEPTH = 2
ALPHA = (2 * DEPTH) ** 0.25
LN_EPS = 1e-5
MASK_VALUE = -1e30
MOD_ROWS = 8

LANES = 128
SUBLANES = 8
BF16_ROWS = 2 * SUBLANES
CTX_ATTN_LOOKAHEAD = 6
LAT_ATTN_LOOKAHEAD = 2
LOG2E = math.log2(math.e)
VMEM_LIMIT = 56 * 1024 * 1024

ADA_COLS = 1536
FILTER_CHANNELS = 256
HY_IN_COLS = 256
HY_CONV_TOKENS = 1024
HY_CONV_CHANNELS = 512
HY_CONV_LOOKAHEAD = 2
POST_TOKENS = 512
POST_ROW_GROUPS = 2
FFN_COLS = 256
QKV_TOKENS = 512

F32 = jnp.float32
BF16 = jnp.bfloat16


def _bf(x):
    return x.astype(BF16)


def _dot(a, b):
    return jnp.dot(a, b, preferred_element_type=F32)


def _dot_nt(a, b):
    return lax.dot_general(a, b, (((1,), (1,)), ((), ())), preferred_element_type=F32)


def _dot_tn(a, b):
    return lax.dot_general(a, b, (((0,), (0,)), ((), ())), preferred_element_type=F32)


def _layer_norm(x, g, b):
    mu = jnp.mean(x, axis=-1, keepdims=True)
    xc = x - mu
    var = jnp.mean(xc * xc, axis=-1, keepdims=True)
    return xc * lax.rsqrt(var + LN_EPS) * g + b


def _params(*sem):
    return pltpu.CompilerParams(dimension_semantics=sem, vmem_limit_bytes=VMEM_LIMIT)


class _Cast:
    def __init__(self, w, layer, steps, step_fn, transpose=False):
        _, r, c = w.shape
        assert r % (steps * BF16_ROWS) == 0
        rows = r // steps
        self.w = w
        self.transpose = transpose
        self.in_spec = pl.BlockSpec((None, rows, c), lambda *g: (layer, step_fn(*g), 0))
        if transpose:
            self.out_shape = jax.ShapeDtypeStruct((c, r), BF16)
            self.out_spec = pl.BlockSpec((c, rows), lambda *g: (0, step_fn(*g)))
        else:
            self.out_shape = jax.ShapeDtypeStruct((r, c), BF16)
            self.out_spec = pl.BlockSpec((rows, c), lambda *g: (step_fn(*g), 0))


def _with_casts(body, n_in, n_out, casts):
    nc = len(casts)

    def kernel_fn(*refs):
        ins, src = refs[:n_in], refs[n_in:n_in + nc]
        outs, dst = refs[n_in + nc:n_in + nc + n_out], refs[n_in + nc + n_out:n_in + 2 * nc + n_out]
        for cast, s, d in zip(casts, src, dst):
            d[...] = _bf(s[...].T if cast.transpose else s[...])
        body(*ins, *outs, *refs[n_in + 2 * nc + n_out:])

    return kernel_fn


def _step(i):
    return i


def _ctx_block(i, ctx_steps):
    return jnp.minimum(i, ctx_steps - 1)


def _lat_block(i, ctx_steps):
    return jnp.maximum(i - ctx_steps, 0)


def _mod_row(i, ctx_steps, tm, lat_len):
    return jnp.where(i < ctx_steps, 0, 1 + ((i - ctx_steps) * tm) // lat_len)


def _pick(is_ctx, refs, idx):
    if len(refs) == 1:
        return refs[0][idx]
    return jnp.where(is_ctx, refs[0][idx], refs[1][idx])


@functools.lru_cache(maxsize=None)
def _plain_dft_tables(L):
    n = 2 * L
    f = np.arange(L)[:, None]
    s = np.arange(L)[None, :]
    ang = 2.0 * np.pi * ((f * s) % n) / n
    c = np.cos(ang)
    sn = np.sin(ang)
    sn[0, :] = np.where(np.arange(L) % 2 == 0, 1.0, -1.0)
    fwd = np.concatenate([c, sn], axis=0)
    w = np.full((2 * L, 1), 2.0 / n)
    w[0, 0] = 1.0 / n
    w[L, 0] = 1.0 / n
    inv = (fwd * w).T
    alt = np.where(np.arange(L) % 2 == 0, 1.0, -1.0).astype(np.float32)[:, None]
    return (np.ascontiguousarray(fwd).astype(np.float32), np.ascontiguousarray(inv).astype(np.float32), alt)


@functools.lru_cache(maxsize=None)
def _dft_tables(L):
    n = 2 * L
    half = L // 2
    f = np.arange(half)[:, None]
    m = np.arange(half)[None, :]
    alt = np.where(np.arange(half) % 2 == 0, 1.0, -1.0)
    w = np.full((L, 1), 2.0 / n)
    w[0, 0] = 1.0 / n
    w[half, 0] = 1.0 / n
    fwd, inv = [], []
    for parity in range(2):
        ang = 2.0 * np.pi * ((f * (2 * m + parity)) % n) / n
        sn = np.sin(ang)
        sn[0, :] = alt
        table = np.concatenate([np.cos(ang), sn], axis=0)
        fwd.append(np.ascontiguousarray(table).astype(np.float32))
        inv.append(np.ascontiguousarray((table * w).T).astype(np.float32))
    return fwd, inv


@functools.lru_cache(maxsize=None)
def _filter_dft_tables(L):
    n = 2 * L
    f = np.arange(L // 2)[:, None]
    t = np.arange(L)[None, :]
    ang = 2.0 * np.pi * ((f * t) % n) / n
    table = np.concatenate([np.cos(ang), np.sin(ang)], axis=0).astype(np.float32)
    tt = np.arange(L)
    cols = np.stack([np.where(tt % 2 == 0, 1.0, -1.0), np.cos(np.pi * (tt % 4) / 2.0),
                     np.sin(np.pi * (tt % 4) / 2.0)], axis=1)
    return np.ascontiguousarray(table), np.round(cols).astype(np.float32)


@functools.lru_cache(maxsize=None)
def _filter_features(L):
    t = np.linspace(0.0, 1.0, L, dtype=np.float32)[:, None]
    w = (np.float32(2.0 * math.pi / L) * np.arange(L, dtype=np.float32))[:, None]
    bands = np.linspace(1e-4, N_BANDS - 1, N_BANDS, dtype=np.float32)[None, :]
    arg = (bands * w).astype(np.float64)
    feats = np.concatenate([t.astype(np.float64), np.cos(arg), np.sin(arg)], axis=-1)
    out = np.zeros((L, LANES), np.float32)
    out[:, :PE_DIM] = feats
    return out, t


HY_SPLIT_MIN_LEN = 512


def _filter_coef_shape(L):
    return (8, L // 2) if L >= HY_SPLIT_MIN_LEN else (3, L)


def _ada_tile(layer, cctx_ref, c_ref, w_ref, b_ref, o_ref, cvec_ref):
    nb = c_ref.shape[0]
    cvec_ref[...] = jnp.zeros_like(cvec_ref)
    cvec_ref[0:1, :] = cctx_ref[...]
    cvec_ref[1:1 + nb, :] = c_ref[...]
    c = cvec_ref[...]
    s = c * jax.nn.sigmoid(c)
    s_hi = _bf(s)
    s_lo = _bf(s - s_hi.astype(F32))
    w = w_ref[...]
    w_hi = _bf(w)
    w_lo = _bf(w - w_hi.astype(F32))
    o_ref[...] = _dot(s_hi, w_hi) + _dot(s_lo, w_hi) + _dot(s_hi, w_lo) + b_ref[pl.ds(layer, 1), :]


def _filter_tile(is_first, feats_ref, t_ref, cols_ref, tab_ref, w1_ref, b1_ref, w2_ref, b2_ref, freq_ref,
                 w3f_ref, w3b_ref, decf_ref, decb_ref, out_ref, h_ref, *, L):
    hi = lax.Precision.HIGHEST

    @pl.when(is_first)
    def _():
        h1 = jnp.sin(freq_ref[0:1, :] * (jnp.dot(feats_ref[...], w1_ref[...], precision=hi,
                                                 preferred_element_type=F32) + b1_ref[...]))
        h_ref[...] = jnp.sin(freq_ref[1:2, :] * (jnp.dot(h1, w2_ref[...], precision=hi,
                                                         preferred_element_type=F32) + b2_ref[...]))

    h = h_ref[...]
    t = t_ref[...]
    kf = jnp.dot(h, w3f_ref[...], precision=hi, preferred_element_type=F32) * (
        jnp.exp(-t * jnp.abs(decf_ref[...])) + MOD_SHIFT)
    kb = jnp.dot(h, w3b_ref[...], precision=hi, preferred_element_type=F32) * (
        jnp.exp(-t * jnp.abs(decb_ref[...])) + MOD_SHIFT)
    row = lax.broadcasted_iota(jnp.int32, kf.shape, 0)
    kb = jnp.where(row == 0, 0.0, kb)
    ksum = kf + kb
    kdiff = kf - kb
    alt = cols_ref[:, 0:1]

    def spectrum(table, x):
        hi = _bf(x)
        return _dot(table, hi) + _dot(table, _bf(x - hi.astype(F32)))

    if L < HY_SPLIT_MIN_LEN:
        k_c = spectrum(tab_ref[0:L, :], ksum)
        k_s = spectrum(tab_ref[L:2 * L, :], kdiff)
        k_nyq = jnp.sum(ksum * alt, axis=0, keepdims=True)
        out_ref[0] = k_c
        out_ref[1] = jnp.where(row == 0, 0.0, k_s)
        out_ref[2] = jnp.where(row == 0, k_nyq, k_c)
        return

    half = L // 2
    cos_m = tab_ref[0:half, :]
    sin_m = tab_ref[half:L, :]
    k_c = spectrum(cos_m, ksum)
    k_s = spectrum(sin_m, kdiff)
    u_c = spectrum(cos_m, alt * ksum)
    u_s = -spectrum(sin_m, alt * kdiff)
    kc_h = jnp.sum(ksum * cols_ref[:, 1:2], axis=0, keepdims=True)
    ks_h = jnp.sum(kdiff * cols_ref[:, 2:3], axis=0, keepdims=True)
    first = lax.broadcasted_iota(jnp.int32, k_c.shape, 0) == 0
    out_ref[0] = k_c
    out_ref[1] = k_s
    out_ref[2] = u_c
    out_ref[3] = u_s
    out_ref[4] = jnp.where(first, kc_h - ks_h, k_c)
    out_ref[5] = jnp.where(first, kc_h + ks_h, u_c)
    out_ref[6] = jnp.where(first, kc_h + ks_h, k_c)
    out_ref[7] = jnp.where(first, kc_h - ks_h, u_c)


def _prep_kernel(cctx_ref, c_ref, w_ref, b_ref, fa_ref, ta_ref, ca_ref, tba_ref, fb_ref, tb_ref, cb_ref,
                 tbb_ref, w1_ref, b1_ref, w2_ref, b2_ref, freq_ref, w3f_ref, w3b_ref, decf_ref, decb_ref,
                 mod_ref, coefa_ref, coefb_ref, cvec_ref, ha_ref, hb_ref, *, lens, tiles):
    s = pl.program_id(0)
    _ada_tile(s // tiles, cctx_ref, c_ref, w_ref, b_ref, mod_ref, cvec_ref)
    shared = (w1_ref, b1_ref, w2_ref, b2_ref, freq_ref, w3f_ref, w3b_ref, decf_ref, decb_ref)

    @pl.when(s < tiles)
    def _():
        _filter_tile(s == 0, fa_ref, ta_ref, ca_ref, tba_ref, *shared, coefa_ref, ha_ref, L=lens[0])

    @pl.when(s >= tiles)
    def _():
        _filter_tile(s == tiles, fb_ref, tb_ref, cb_ref, tbb_ref, *shared, coefb_ref, hb_ref, L=lens[1])


def _prep(c_ctx, c, ada_w, ada_b, lens, pe_w1, pe_b1, pe_w2, pe_b2, pe_w3, sin_freq, decay, casts=()):
    assert 1 + c.shape[0] <= MOD_ROWS and DEPTH == 2
    tn = ADA_COLS
    dt = FILTER_CHANNELS
    tiles = D_MODEL // dt
    assert 6 * D_MODEL // tn == tiles
    full = lambda shape: pl.BlockSpec(shape, lambda s: (0,) * len(shape))
    consts, const_specs, out_shapes, out_specs, scratch = [], [], [], [], []
    for k, L in enumerate(lens):
        feats, t = _filter_features(L)
        if L >= HY_SPLIT_MIN_LEN:
            table, cols = _filter_dft_tables(L)
        else:
            table, _, cols = _plain_dft_tables(L)
        consts += [feats, t, cols, _bf(jnp.asarray(table))]
        const_specs += [full(feats.shape), full(t.shape), full(cols.shape), full(table.shape)]
        nc, nf = _filter_coef_shape(L)
        out_shapes.append(jax.ShapeDtypeStruct((nc, nf, D_MODEL), F32))
        block = (lambda s: (0, 0, jnp.minimum(s, tiles - 1))) if k == 0 else (
            lambda s: (0, 0, jnp.maximum(s - tiles, 0)))
        out_specs.append(pl.BlockSpec((nc, nf, dt), block))
        scratch.append(pltpu.VMEM((L, LANES), F32))
    pad = LANES - FILT_W
    w1 = jnp.pad(pe_w1, ((0, LANES - PE_DIM), (0, pad)))
    b1 = jnp.pad(pe_b1, (0, pad)).reshape(1, LANES)
    w2 = jnp.pad(pe_w2, ((0, pad), (0, pad)))
    b2 = jnp.pad(pe_b2, (0, pad)).reshape(1, LANES)
    w3 = jnp.pad(pe_w3, ((0, pad), (0, 0)))
    freq = jnp.pad(sin_freq, ((0, 0), (0, pad)))
    dec = decay.reshape(1, 2 * D_MODEL)
    return pl.pallas_call(
        _with_casts(functools.partial(_prep_kernel, lens=tuple(lens), tiles=tiles), 21, 3, casts),
        out_shape=(jax.ShapeDtypeStruct((DEPTH, MOD_ROWS, 6 * D_MODEL), F32), *out_shapes,
                   *[k.out_shape for k in casts]),
        grid=(DEPTH * tiles,),
        in_specs=[
            full((1, D_MODEL)), full(c.shape),
            pl.BlockSpec((None, D_MODEL, tn), lambda s: (s // tiles, 0, s % tiles)),
            pl.BlockSpec((DEPTH, tn), lambda s: (0, s % tiles)),
            *const_specs,
            full((LANES, LANES)), full((1, LANES)), full((LANES, LANES)), full((1, LANES)), full((2, LANES)),
            pl.BlockSpec((LANES, dt), lambda s: (0, s % tiles)),
            pl.BlockSpec((LANES, dt), lambda s: (0, tiles + s % tiles)),
            pl.BlockSpec((1, dt), lambda s: (0, s % tiles)),
            pl.BlockSpec((1, dt), lambda s: (0, tiles + s % tiles)),
            *[k.in_spec for k in casts],
        ],
        out_specs=(pl.BlockSpec((None, MOD_ROWS, tn), lambda s: (s // tiles, 0, s % tiles)), *out_specs,
                   *[k.out_spec for k in casts]),
        scratch_shapes=[pltpu.VMEM((MOD_ROWS, D_MODEL), F32), *scratch],
        compiler_params=_params("arbitrary"),
        name="prep",
    )(c_ctx.reshape(1, D_MODEL), c, ada_w, ada_b, *consts, w1, b1, w2, b2, freq, w3, w3, dec, dec,
      *[k.w for k in casts])


def _hy_in_kernel(xc_ref, xl_ref, mod_ref, w_ref, cw_ref, cb_ref, x0_ref, vx_ref, *, ctx_steps, ctx_len, cn):
    i = pl.program_id(0)
    is_ctx = i < ctx_steps
    tm = x0_ref.shape[0]
    mod = mod_ref[pl.ds(_mod_row(i, ctx_steps, tm, tm), 1), :]
    x = _pick(is_ctx, (xc_ref, xl_ref), (slice(None), slice(None)))
    h = _bf(x * (1.0 + mod[:, D_MODEL:2 * D_MODEL]) + mod[:, 0:D_MODEL])
    row = lax.broadcasted_iota(jnp.int32, (SUBLANES, cn), 0)

    def conv(col):
        z = _dot(h, w_ref[:, col:col + cn])
        zm = pltpu.roll(z, 1, 0)
        zp = pltpu.roll(z, tm - 1, 0)
        w0 = cw_ref[0:1, col:col + cn]
        w1 = cw_ref[1:2, col:col + cn]
        w2 = cw_ref[2:3, col:col + cn]
        b = cb_ref[:, col:col + cn]
        out = zm * w0 + z * w1 + zp * w2 + b
        pieces = []
        for s in range(tm // ctx_len):
            lo = s * ctx_len
            hi = lo + ctx_len - SUBLANES
            head = (jnp.where(row == 0, 0.0, zm[lo:lo + SUBLANES]) * w0 + z[lo:lo + SUBLANES] * w1
                    + zp[lo:lo + SUBLANES] * w2 + b)
            tail = (zm[hi:hi + SUBLANES] * w0 + z[hi:hi + SUBLANES] * w1
                    + jnp.where(row == SUBLANES - 1, 0.0, zp[hi:hi + SUBLANES]) * w2 + b)
            if lo > 0:
                head = jnp.where(is_ctx, head, out[lo:lo + SUBLANES])
            if hi + SUBLANES < tm:
                tail = jnp.where(is_ctx, tail, out[hi:hi + SUBLANES])
            pieces += [head, out[lo + SUBLANES:hi], tail]
        return jnp.concatenate(pieces, axis=0)

    for c in range(D_MODEL // cn):
        x0_ref[:, c * cn:(c + 1) * cn] = _bf(conv(c * cn))
        x1 = conv(D_MODEL + c * cn)
        v = conv(2 * D_MODEL + c * cn)
        vx_ref[:, c * cn:(c + 1) * cn] = _bf(v * x1)


def _hy_in(x_ctx, x_lat, ctx_len, lat_len, mods, layer, w_in, sconv_w, sconv_b, casts=()):
    tm = lat_len
    assert tm % ctx_len == 0 and x_ctx.shape[0] % tm == 0
    ctx_steps = x_ctx.shape[0] // tm
    n = x_ctx.shape[0] + x_lat.shape[0]
    tok = pl.BlockSpec((tm, D_MODEL), lambda i: (i, 0))
    const = lambda shape: pl.BlockSpec(shape, lambda i: (0, 0), pipeline_mode=pl.Buffered(1))
    act = jax.ShapeDtypeStruct((n, D_MODEL), BF16)
    body = functools.partial(_hy_in_kernel, ctx_steps=ctx_steps, ctx_len=ctx_len, cn=HY_IN_COLS)
    return pl.pallas_call(
        _with_casts(body, 6, 2, casts),
        out_shape=(act, act) + tuple(c.out_shape for c in casts),
        grid=(n // tm,),
        in_specs=[
            pl.BlockSpec((tm, D_MODEL), lambda i: (_ctx_block(i, ctx_steps), 0)),
            pl.BlockSpec((tm, D_MODEL), lambda i: (_lat_block(i, ctx_steps), 0)),
            pl.BlockSpec((None, MOD_ROWS, 6 * D_MODEL), lambda i: (layer, 0, 0)),
            const((D_MODEL, 3 * D_MODEL)),
            pl.BlockSpec((3, 3 * D_MODEL), lambda i: (0, 0)),
            pl.BlockSpec((1, 3 * D_MODEL), lambda i: (0, 0)),
        ] + [c.in_spec for c in casts],
        out_specs=(tok, tok) + tuple(c.out_spec for c in casts),
        compiler_params=_params("arbitrary"),
        name="hyena_in",
    )(x_ctx, x_lat, mods, w_in, sconv_w, sconv_b, *[c.w for c in casts])


def _hy_conv_plain_kernel(vx_ref, x0_ref, coef_ref, bias_ref, fwd_ref, inv_ref, u_ref, *, L, nseq, cn):
    pieces = [(c, s) for c in range(vx_ref.shape[1] // cn) for s in range(nseq)]

    def forward(piece):
        c, s = piece
        vx = vx_ref[s * L:(s + 1) * L, c * cn:(c + 1) * cn]
        return vx, _dot(fwd_ref[...], vx)

    pending = [forward(p) for p in pieces[:HY_CONV_LOOKAHEAD]]
    for i, (c, s) in enumerate(pieces):
        if i + HY_CONV_LOOKAHEAD < len(pieces):
            pending.append(forward(pieces[i + HY_CONV_LOOKAHEAD]))
        vx, spec = pending.pop(0)
        rows = slice(s * L, (s + 1) * L)
        cols = slice(c * cn, (c + 1) * cn)
        k_cos = coef_ref[0, :, cols]
        k_sin = coef_ref[1, :, cols]
        k_mix = coef_ref[2, :, cols]
        v_cos = spec[0:L]
        v_sin = spec[L:2 * L]
        y_spec = _bf(jnp.concatenate([v_cos * k_cos - v_sin * k_sin, v_cos * k_sin + v_sin * k_mix], axis=0))
        y = _dot(inv_ref[...], y_spec)
        u = (y + bias_ref[:, cols] * vx.astype(F32)) * x0_ref[rows, cols].astype(F32)
        u_ref[rows, cols] = _bf(u)


def _hy_conv_split_kernel(vx_ref, x0_ref, coef_ref, bias_ref, fe_ref, fo_ref, ge_ref, go_ref, u_ref,
                          sv_ref, sx_ref, su_ref, *, L, nseq, cn):
    half = L // 2
    for g in range(vx_ref.shape[1] // LANES):
        lanes = slice(g * LANES, (g + 1) * LANES)
        sv_ref[g] = vx_ref[:, lanes].astype(F32)
        sx_ref[g] = x0_ref[:, lanes].astype(F32)

    def load(ref, start, c):
        return jnp.concatenate([ref[g, pl.ds(start, half, stride=2), :]
                                for g in range(c * cn // LANES, (c + 1) * cn // LANES)], axis=1)

    def store(ref, start, c, val):
        for k, g in enumerate(range(c * cn // LANES, (c + 1) * cn // LANES)):
            ref[g, pl.ds(start, half, stride=2), :] = val[:, k * LANES:(k + 1) * LANES]

    pieces = [(c, s) for c in range(vx_ref.shape[1] // cn) for s in range(nseq)]

    def forward(piece):
        c, s = piece
        v_e = load(sv_ref, s * L, c)
        v_o = load(sv_ref, s * L + 1, c)
        return v_e, v_o, _dot(fe_ref[...], _bf(v_e)), _dot(fo_ref[...], _bf(v_o))

    pending = [forward(p) for p in pieces[:HY_CONV_LOOKAHEAD]]
    for i, (c, s) in enumerate(pieces):
        if i + HY_CONV_LOOKAHEAD < len(pieces):
            pending.append(forward(pieces[i + HY_CONV_LOOKAHEAD]))
        v_e, v_o, a, b = pending.pop(0)
        cols = slice(c * cn, (c + 1) * cn)
        k_c, k_s, u_c, u_s, e2, e3, f2, f3 = [coef_ref[k, :, cols] for k in range(coef_ref.shape[0])]
        s_c, s_s = a[0:half] + b[0:half], a[half:L] + b[half:L]
        d_c, d_s = a[0:half] - b[0:half], a[half:L] - b[half:L]
        t1 = s_c * k_c - s_s * k_s
        t2 = d_c * u_c + d_s * u_s
        q1 = s_c * k_s
        q4 = d_c * u_s
        p_spec = jnp.concatenate([t1 + t2, q1 + s_s * e2 + d_s * e3 - q4], axis=0)
        m_spec = jnp.concatenate([t1 - t2, q1 + s_s * f2 - d_s * f3 + q4], axis=0)
        y_e = _dot(ge_ref[...], _bf(p_spec))
        y_o = _dot(go_ref[...], _bf(m_spec))
        bias = bias_ref[:, cols]
        store(su_ref, s * L, c, (y_e + bias * v_e) * load(sx_ref, s * L, c))
        store(su_ref, s * L + 1, c, (y_o + bias * v_o) * load(sx_ref, s * L + 1, c))
        rows = slice(s * L, (s + 1) * L)
        for g in range(c * cn // LANES, (c + 1) * cn // LANES):
            u_ref[rows, g * LANES:(g + 1) * LANES] = _bf(su_ref[g, rows, :])


def _hy_conv_steps(n):
    nt = n // HY_CONV_TOKENS
    return (D_MODEL // HY_CONV_CHANNELS) * nt, lambda j, i: j * nt + i


def _hy_conv(vx, x0, tok0, n, coef, bias, L, casts=()):
    tm = HY_CONV_TOKENS
    dt = HY_CONV_CHANNELS
    assert tok0 % tm == 0 and n % tm == 0 and tm % L == 0
    src = pl.BlockSpec((tm, dt), lambda j, i: (tok0 // tm + i, j))
    dst = pl.BlockSpec((tm, dt), lambda j, i: (i, j))
    const = lambda shape: pl.BlockSpec(shape, lambda j, i: (0, 0), pipeline_mode=pl.Buffered(1))
    if L >= HY_SPLIT_MIN_LEN:
        fwd, inv = _dft_tables(L)
        tables = (*fwd, *inv)
        body = functools.partial(_hy_conv_split_kernel, L=L, nseq=tm // L, cn=256)
        scratch = [pltpu.VMEM((dt // LANES, tm, LANES), F32)] * 3
    else:
        tables = _plain_dft_tables(L)[:2]
        body = functools.partial(_hy_conv_plain_kernel, L=L, nseq=tm // L, cn=256)
        scratch = []
    return pl.pallas_call(
        _with_casts(body, 4 + len(tables), 1, casts),
        out_shape=(jax.ShapeDtypeStruct((n, D_MODEL), BF16),) + tuple(c.out_shape for c in casts),
        grid=(D_MODEL // dt, n // tm),
        in_specs=[
            src, src,
            pl.BlockSpec(_filter_coef_shape(L) + (dt,), lambda j, i: (0, 0, j)),
            pl.BlockSpec((1, dt), lambda j, i: (0, j)),
        ] + [const(t.shape) for t in tables] + [c.in_spec for c in casts],
        out_specs=(dst,) + tuple(c.out_spec for c in casts),
        scratch_shapes=scratch,
        compiler_params=_params("arbitrary", "arbitrary"),
        name=f"hyena_conv_{L}",
    )(vx, x0, coef, bias, *[_bf(jnp.asarray(t)) for t in tables], *[c.w for c in casts])


def _post_kernel(*refs, layer, ctx_steps, lat_len, n_x, n_a, a_feature_major):
    x_refs, a_refs = refs[:n_x], refs[n_x:n_x + n_a]
    (mod_ref, wo_ref, ln1g_ref, ln1b_ref, win_ref, wout_ref, ln2g_ref, ln2b_ref, o_ref,
     act_ref) = refs[n_x + n_a:]
    d = D_MODEL
    i = pl.program_id(0)
    is_ctx = i < ctx_steps
    tm = act_ref.shape[0]
    mod = mod_ref[pl.ds(_mod_row(i, ctx_steps, tm, lat_len), 1), :]
    g1 = mod[:, 2 * d:3 * d]
    sh2 = mod[:, 3 * d:4 * d]
    sc2 = mod[:, 4 * d:5 * d]
    g2 = mod[:, 5 * d:6 * d]
    ln1 = (ln1g_ref[layer:layer + 1, :], ln1b_ref[layer:layer + 1, :])
    ln2 = (ln2g_ref[layer:layer + 1, :], ln2b_ref[layer:layer + 1, :])

    gr = tm // POST_ROW_GROUPS
    groups = [slice(r * gr, (r + 1) * gr) for r in range(POST_ROW_GROUPS)]
    if a_feature_major:
        mix = [_dot_tn(_pick(is_ctx, a_refs, (slice(None), rows)), wo_ref[...]) for rows in groups]
    else:
        mix = [_dot(_pick(is_ctx, a_refs, (rows, slice(None))), wo_ref[...]) for rows in groups]
    x1 = []
    for rows, mx in zip(groups, mix):
        x = _pick(is_ctx, x_refs, (rows, slice(None)))
        x1.append(_layer_norm(ALPHA * x + g1 * mx, *ln1))
        h = _bf(x1[-1] * (1.0 + sc2) + sh2)
        for c in range(D_FF // FFN_COLS):
            cols = slice(c * FFN_COLS, (c + 1) * FFN_COLS)
            gate = _dot(h, win_ref[:, cols])
            up = _dot(h, win_ref[:, D_FF + c * FFN_COLS:D_FF + (c + 1) * FFN_COLS])
            act_ref[rows, cols] = _bf(gate * jax.nn.sigmoid(gate) * up)
    for rows, xr in zip(groups, x1):
        f = _dot(act_ref[rows, :], wout_ref[...])
        o_ref[rows, :] = _layer_norm(ALPHA * xr + g2 * f, *ln2)


def _post(xs, mixes, n, n_ctx, lat_len, mods, layer, w_o, w_in, w_out, ln1_g, ln1_b, ln2_g, ln2_b, name,
          x_tok0=0, a_feature_major=False, casts=()):
    tm = POST_TOKENS
    ctx_steps = n_ctx // tm
    ctx = lambda i: _ctx_block(i, ctx_steps)
    lat = lambda i: _lat_block(i, ctx_steps)
    blocks = lambda m, tok0: (lambda i: tok0 // tm + i,) if m == 1 else (ctx, lat)
    tok = lambda f: pl.BlockSpec((tm, D_MODEL), lambda i: (f(i), 0))
    feat = lambda f: pl.BlockSpec((D_MODEL, tm), lambda i: (0, f(i)))
    const = lambda shape: pl.BlockSpec(shape, lambda i: (0, 0), pipeline_mode=pl.Buffered(1))
    vec = pl.BlockSpec((DEPTH, D_MODEL), lambda i: (0, 0))
    body = functools.partial(_post_kernel, layer=layer, ctx_steps=ctx_steps, lat_len=lat_len, n_x=len(xs),
                             n_a=len(mixes), a_feature_major=a_feature_major)
    return pl.pallas_call(
        _with_casts(body, len(xs) + len(mixes) + 8, 1, casts),
        out_shape=(jax.ShapeDtypeStruct((n, D_MODEL), F32),) + tuple(c.out_shape for c in casts),
        grid=(n // tm,),
        in_specs=[tok(f) for f in blocks(len(xs), x_tok0)]
        + [(feat if a_feature_major else tok)(f) for f in blocks(len(mixes), 0)]
        + [pl.BlockSpec((None, MOD_ROWS, 6 * D_MODEL), lambda i: (layer, 0, 0)),
           const((D_MODEL, D_MODEL)), vec, vec,
           const((D_MODEL, 2 * D_FF)), const((D_FF, D_MODEL)), vec, vec]
        + [c.in_spec for c in casts],
        out_specs=(tok(lambda i: i),) + tuple(c.out_spec for c in casts),
        scratch_shapes=[pltpu.VMEM((tm, D_FF), BF16)],
        compiler_params=_params("arbitrary"),
        name=name,
    )(*xs, *mixes, mods, w_o, ln1_g, ln1_b, w_in, w_out, ln2_g, ln2_b, *[c.w for c in casts])


def _qkv_kernel(x_ref, mod_ref, wt_ref, qkvt_ref, nk_ref, nv_ref, *, ctx_steps, lat_len):
    d = D_MODEL
    i = pl.program_id(0)
    tm = x_ref.shape[0]
    mod = mod_ref[pl.ds(_mod_row(i, ctx_steps, tm, lat_len), 1), :]
    h = _bf(x_ref[...] * (1.0 + mod[:, d:2 * d]) + mod[:, 0:d])
    qkvt_ref[0:d, :] = _bf(_dot_nt(wt_ref[0:d, :], h) * (HEAD_DIM ** -0.5 * LOG2E))
    kt = _dot_nt(wt_ref[d:2 * d, :], h)
    qkvt_ref[d:2 * d, :] = _bf(kt)
    vt = _dot_nt(wt_ref[2 * d:3 * d, :], h)
    qkvt_ref[2 * d:3 * d, :] = _bf(vt)

    @pl.when(i < ctx_steps)
    def _():
        seq = nk_ref.shape[4]
        for s in range(tm // seq):
            nk_ref[s, 0] = kt[:, s * seq:(s + 1) * seq].reshape(N_HEADS, HEAD_DIM, seq)
            nv_ref[s, 0] = vt[:, s * seq:(s + 1) * seq].reshape(N_HEADS, HEAD_DIM, seq)


def _qkv(x2d, n_ctx, ctx_len, lat_len, mods, layer, w_qkv_t):
    n = x2d.shape[0]
    tm = QKV_TOKENS
    assert tm % ctx_len == 0 and n_ctx % tm == 0
    ctx_steps = n_ctx // tm
    kv_shape = jax.ShapeDtypeStruct((n_ctx // ctx_len, 1, N_HEADS, HEAD_DIM, ctx_len), F32)
    kv_spec = pl.BlockSpec((tm // ctx_len, 1, N_HEADS, HEAD_DIM, ctx_len),
                           lambda i: (_ctx_block(i, ctx_steps), 0, 0, 0, 0))
    return pl.pallas_call(
        functools.partial(_qkv_kernel, ctx_steps=ctx_steps, lat_len=lat_len),
        out_shape=(jax.ShapeDtypeStruct((3 * D_MODEL, n), BF16), kv_shape, kv_shape),
        grid=(n // tm,),
        in_specs=[
            pl.BlockSpec((tm, D_MODEL), lambda i: (i, 0)),
            pl.BlockSpec((None, MOD_ROWS, 6 * D_MODEL), lambda i: (layer, 0, 0)),
            pl.BlockSpec((3 * D_MODEL, D_MODEL), lambda i: (0, 0), pipeline_mode=pl.Buffered(1)),
        ],
        out_specs=(pl.BlockSpec((3 * D_MODEL, tm), lambda i: (0, i)), kv_spec, kv_spec),
        compiler_params=_params("arbitrary"),
        name="qkv",
    )(x2d, mods, w_qkv_t)


def _softmax_weights(*scores):
    m = functools.reduce(jnp.maximum, [jnp.max(s, axis=0, keepdims=True) for s in scores])
    return [_bf(jnp.exp2(s - m)) for s in scores]


def _weighted_values(vt, p):
    ones = jnp.ones((BF16_ROWS, vt.shape[1]), BF16)
    return _dot(jnp.concatenate([vt, ones], axis=0), p)


def _normalise(acc):
    return _bf(acc[0:HEAD_DIM] * (1.0 / acc[HEAD_DIM:HEAD_DIM + 1]))


def _ctx_attn_kernel(qt_ref, kt_ref, vt_ref, ot_ref):
    def rows(hh):
        return slice(hh * HEAD_DIM, (hh + 1) * HEAD_DIM)

    def scores(hh):
        return _dot_tn(kt_ref[rows(hh), :], qt_ref[rows(hh), :])

    pending = [scores(hh) for hh in range(CTX_ATTN_LOOKAHEAD)]
    for hh in range(N_HEADS):
        if hh + CTX_ATTN_LOOKAHEAD < N_HEADS:
            pending.append(scores(hh + CTX_ATTN_LOOKAHEAD))
        (p,) = _softmax_weights(pending.pop(0))
        ot_ref[rows(hh), :] = _normalise(_weighted_values(vt_ref[rows(hh), :], p))


def _ctx_attn(qkvt, n_ctx, seq):
    return pl.pallas_call(
        _ctx_attn_kernel,
        out_shape=jax.ShapeDtypeStruct((D_MODEL, n_ctx), BF16),
        grid=(n_ctx // seq,),
        in_specs=[pl.BlockSpec((D_MODEL, seq), lambda b, c=c: (c, b)) for c in range(3)],
        out_specs=pl.BlockSpec((D_MODEL, seq), lambda b: (0, b)),
        compiler_params=_params("arbitrary"),
        name="ctx_attn",
    )(qkvt, qkvt, qkvt)


LAT_ROWS = 16
Q_BLOCK_ROWS = 4
KEY_TILE_ROWS = 4


def _key_band(qb):
    row0 = [min(max(r - WIN_H // 2, 0), LAT_ROWS - WIN_H) for r in range(qb * Q_BLOCK_ROWS, (qb + 1) * Q_BLOCK_ROWS)]
    lo = min(row0) // KEY_TILE_ROWS * KEY_TILE_ROWS
    hi = -(-(max(row0) + WIN_H) // KEY_TILE_ROWS) * KEY_TILE_ROWS
    return lo, hi


KEY_BANDS = tuple(_key_band(qb) for qb in range(LAT_ROWS // Q_BLOCK_ROWS))
MAX_BAND_ROWS = max(hi - lo for lo, hi in KEY_BANDS)


def _fill_bias_table(rpb_ref, tab_ref):
    shape = (GRID_W, LANES)
    kc = lax.broadcasted_iota(jnp.int32, shape, 0)
    lane = lax.broadcasted_iota(jnp.int32, shape, 1)
    qc = lane % GRID_W
    win0 = jnp.clip(qc - WIN_W // 2, 0, GRID_W - WIN_W)
    col_ok = (kc >= win0) & (kc < win0 + WIN_W)
    masked = jnp.full(shape, MASK_VALUE, F32)
    tiles = []
    for dr in range(2 * WIN_H - 1):
        row = jnp.broadcast_to(rpb_ref[dr:dr + 1, :] * LOG2E, shape)
        t = pltpu.roll(row, LANES - (WIN_W - 1), 1, stride=1, stride_axis=0)
        t = jnp.where(lane < GRID_W, t, pltpu.roll(t, GRID_W, 1))
        tiles.append(jnp.where(col_ok, t, MASK_VALUE))

    def tile(qr, kr):
        row0 = min(max(qr - WIN_H // 2, 0), LAT_ROWS - WIN_H)
        if row0 <= kr < row0 + WIN_H:
            return tiles[kr - qr + WIN_H - 1]
        return masked

    for qr in range(0, LAT_ROWS, 2):
        lo, hi = KEY_BANDS[qr // Q_BLOCK_ROWS]
        for i in range(hi - lo):
            both = jnp.where(lane < GRID_W, tile(qr, lo + i), tile(qr + 1, lo + i))
            tab_ref[i * GRID_W:(i + 1) * GRID_W, qr * GRID_W:(qr + 2) * GRID_W] = both


def _lat_attn_kernel(qt_ref, kt_ref, vt_ref, ckt_ref, cvt_ref, rpb_ref, ot_ref, tab_ref):
    seq = LAT_ROWS * GRID_W
    nq = Q_BLOCK_ROWS * GRID_W
    heads = qt_ref.shape[0] // HEAD_DIM
    units = [(b, hh, qb) for b in range(qt_ref.shape[1] // seq) for hh in range(heads)
             for qb in range(len(KEY_BANDS))]

    def rows(hh):
        return slice(hh * HEAD_DIM, (hh + 1) * HEAD_DIM)

    def keys(b, qb):
        lo, hi = KEY_BANDS[qb]
        return slice(b * seq + lo * GRID_W, b * seq + hi * GRID_W)

    def queries(b, qb):
        return slice(b * seq + qb * nq, b * seq + (qb + 1) * nq)

    def scores(unit):
        b, hh, qb = unit
        lo, hi = KEY_BANDS[qb]
        qt = qt_ref[rows(hh), queries(b, qb)]
        s_loc = (_dot_tn(kt_ref[rows(hh), keys(b, qb)], qt)
                 + tab_ref[hh, 0:(hi - lo) * GRID_W, qb * nq:(qb + 1) * nq])
        s_ctx = _dot_tn(_bf(ckt_ref[b, hh]), qt)
        return s_loc, s_ctx

    for hh in range(tab_ref.shape[0]):
        _fill_bias_table(rpb_ref.at[hh], tab_ref.at[hh])
    pending = [scores(u) for u in units[:LAT_ATTN_LOOKAHEAD]]
    for i, (b, hh, qb) in enumerate(units):
        if i + LAT_ATTN_LOOKAHEAD < len(units):
            pending.append(scores(units[i + LAT_ATTN_LOOKAHEAD]))
        s_loc, s_ctx = pending.pop(0)
        p_loc, p_ctx = _softmax_weights(s_loc, s_ctx)
        acc = (_weighted_values(vt_ref[rows(hh), keys(b, qb)], p_loc)
               + _weighted_values(_bf(cvt_ref[b, hh]), p_ctx))
        ot_ref[rows(hh), queries(b, qb)] = _normalise(acc)


def _lat_attn(qkvt, n_ctx, cache_kt, cache_vt, rpb, layer_idx):
    rpb = jnp.pad(rpb[:, :, ::-1], ((0, 0), (0, 1), (0, LANES - (2 * WIN_W - 1))))
    n = qkvt.shape[1] - n_ctx
    seq = LAT_ROWS * GRID_W
    assert n_ctx % n == 0 and n % seq == 0
    nb = n // seq
    nhp = N_HEADS // 2
    past = cache_kt.shape[4]
    feat = lambda c: pl.BlockSpec((2 * HEAD_DIM, n), lambda hp, c=c: (c * nhp + hp, n_ctx // n))
    cache = pl.BlockSpec((nb, None, 2, HEAD_DIM, past), lambda hp: (0, layer_idx, hp, 0, 0))
    return pl.pallas_call(
        _lat_attn_kernel,
        out_shape=jax.ShapeDtypeStruct((D_MODEL, n), BF16),
        grid=(nhp,),
        in_specs=[feat(0), feat(1), feat(2), cache, cache,
                  pl.BlockSpec((2, 2 * WIN_H, LANES), lambda hp: (hp, 0, 0))],
        out_specs=pl.BlockSpec((2 * HEAD_DIM, n), lambda hp: (hp, 0)),
        scratch_shapes=[pltpu.VMEM((2, MAX_BAND_ROWS * GRID_W, seq), F32)],
        compiler_params=_params("arbitrary"),
        name="lat_attn",
    )(qkvt, qkvt, qkvt, cache_kt, cache_vt, rpb)


def kernel(x_prompt, x_sample, cache_k, cache_v, c, c_ctx, ada_w, ada_b, ln1_g, ln1_b, ln2_g, ln2_b, ffn_w_in, ffn_w_out, hy_w_in, hy_sconv_w, hy_sconv_b, hy_pe_w1, hy_pe_b1, hy_pe_w2, hy_pe_b2, hy_pe_w3, hy_sin_freq, hy_decay, hy_bias, hy_w_out, na_w_qkv, na_rpb, na_w_out):
    nbp, lp, d = x_prompt.shape
    nbs, ls, _ = x_sample.shape
    assert d == D_MODEL and ls == LAT_ROWS * GRID_W and c.shape[0] == nbs
    xp = x_prompt.reshape(nbp * lp, d)
    xs = x_sample.reshape(nbs * ls, d)
    n_ctx, n_lat = xp.shape[0], xs.shape[0]
    n = n_ctx + n_lat
    ln = (ln1_g, ln1_b, ln2_g, ln2_b)

    filt = (hy_pe_w1[0], hy_pe_b1[0], hy_pe_w2[0], hy_pe_b2[0], hy_pe_w3[0], hy_sin_freq[0], hy_decay[0])
    mods, coef_p, coef_s, hy_w_in0 = _prep(c_ctx, c, ada_w, ada_b, (lp, ls), *filt,
                                           casts=[_Cast(hy_w_in, 0, DEPTH * D_MODEL // FILTER_CHANNELS, _step)])

    sconv_b = hy_sconv_b[0].reshape(1, -1)
    hy_bias0 = hy_bias[0].reshape(1, -1)
    x0, vx, ffn_w_in0 = _hy_in(xp, xs, lp, ls, mods, 0, hy_w_in0, hy_sconv_w[0], sconv_b,
                               casts=[_Cast(ffn_w_in, 0, n // ls, _step)])
    steps, lin = _hy_conv_steps(n_ctx)
    u_p, hy_w_out0 = _hy_conv(vx, x0, 0, n_ctx, coef_p, hy_bias0, lp, casts=[_Cast(hy_w_out, 0, steps, lin)])
    steps, lin = _hy_conv_steps(n_lat)
    u_s, w_qkv_t, ffn_w_out0 = _hy_conv(vx, x0, n_ctx, n_lat, coef_s, hy_bias0, ls,
                                        casts=[_Cast(na_w_qkv, 0, steps, lin, transpose=True),
                                               _Cast(ffn_w_out, 0, steps, lin)])
    steps = n // POST_TOKENS
    x1, ffn_w_in1, ffn_w_out1, na_w_out0 = _post(
        (xp, xs), (u_p, u_s), n, n_ctx, ls, mods, 0, hy_w_out0, ffn_w_in0, ffn_w_out0, *ln, name="post0",
        casts=[_Cast(ffn_w_in, 1, steps, _step), _Cast(ffn_w_out, 1, steps, _step),
               _Cast(na_w_out, 0, steps, _step)])

    qkvt, new_kt, new_vt = _qkv(x1, n_ctx, lp, ls, mods, 1, w_qkv_t)
    at_p = _ctx_attn(qkvt, n_ctx, lp)
    at_s = _lat_attn(qkvt, n_ctx, jnp.swapaxes(cache_k, 3, 4), jnp.swapaxes(cache_v, 3, 4), na_rpb[0], 0)
    ffn1 = (na_w_out0, ffn_w_in1, ffn_w_out1, *ln)
    (yp,) = _post((x1,), (at_p,), n_ctx, n_ctx, ls, mods, 1, *ffn1, name="post1_ctx", a_feature_major=True)
    (ys,) = _post((x1,), (at_s,), n_lat, 0, ls, mods, 1, *ffn1, name="post1_lat", x_tok0=n_ctx,
                  a_feature_major=True)

    return (yp.reshape(nbp, lp, d), ys.reshape(nbs, ls, d),
            jnp.swapaxes(new_kt, 3, 4), jnp.swapaxes(new_vt, 3, 4))
```

```python
import functools
import math

import jax
import jax.numpy as jnp
import numpy as np
from jax import lax
from jax.experimental import pallas as pl
from jax.experimental.pallas import tpu as pltpu

D_MODEL = 1024
N_HEADS = 16
HEAD_DIM = D_MODEL // N_HEADS
D_FF = 2816
GRID_W = 64
WIN_H = 8
WIN_W = 16
N_BANDS = 16
PE_DIM = 1 + 2 * N_BANDS
FILT_W = 64
MOD_SHIFT = 0.05
DEPTH = 2
ALPHA = (2 * DEPTH) ** 0.25
LN_EPS = 1e-5
MASK_VALUE = -1e30
MOD_ROWS = 8

LANES = 128
SUBLANES = 8
BF16_ROWS = 2 * SUBLANES
CTX_ATTN_LOOKAHEAD = 6
LAT_ATTN_LOOKAHEAD = 2
LOG2E = math.log2(math.e)
VMEM_LIMIT = 56 * 1024 * 1024

ADA_COLS = 1536
FILTER_CHANNELS = 256
HY_IN_COLS = 256
HY_CONV_TOKENS = 1024
HY_CONV_CHANNELS = 512
HY_CONV_LOOKAHEAD = 2
POST_TOKENS = 512
POST_ROW_GROUPS = 2
FFN_COLS = 256
QKV_TOKENS = 512

F32 = jnp.float32
BF16 = jnp.bfloat16


def _bf(x):
    return x.astype(BF16)


def _dot(a, b):
    return jnp.dot(a, b, preferred_element_type=F32)


def _dot_nt(a, b):
    return lax.dot_general(a, b, (((1,), (1,)), ((), ())), preferred_element_type=F32)


def _dot_tn(a, b):
    return lax.dot_general(a, b, (((0,), (0,)), ((), ())), preferred_element_type=F32)


def _dot3(a, b):
    a_hi = _bf(a)
    a_lo = _bf(a - a_hi.astype(F32))
    b_hi = _bf(b)
    b_lo = _bf(b - b_hi.astype(F32))
    m = a.shape[0]
    both = _dot(jnp.concatenate([a_hi, a_lo], axis=0), b_hi)
    return both[0:m] + both[m:2 * m] + _dot(a_hi, b_lo)


def _layer_norm(x, g, b):
    mu = jnp.mean(x, axis=-1, keepdims=True)
    xc = x - mu
    var = jnp.mean(xc * xc, axis=-1, keepdims=True)
    return xc * lax.rsqrt(var + LN_EPS) * g + b


def _params(*sem):
    return pltpu.CompilerParams(dimension_semantics=sem, vmem_limit_bytes=VMEM_LIMIT)


class _Cast:
    def __init__(self, w, layer, steps, step_fn, transpose=False):
        _, r, c = w.shape
        assert r % (steps * BF16_ROWS) == 0
        rows = r // steps
        self.w = w
        self.transpose = transpose
        self.in_spec = pl.BlockSpec((None, rows, c), lambda *g: (layer, step_fn(*g), 0))
        if transpose:
            self.out_shape = jax.ShapeDtypeStruct((c, r), BF16)
            self.out_spec = pl.BlockSpec((c, rows), lambda *g: (0, step_fn(*g)))
        else:
            self.out_shape = jax.ShapeDtypeStruct((r, c), BF16)
            self.out_spec = pl.BlockSpec((rows, c), lambda *g: (step_fn(*g), 0))


def _with_casts(body, n_in, n_out, casts):
    nc = len(casts)

    def kernel_fn(*refs):
        ins, src = refs[:n_in], refs[n_in:n_in + nc]
        outs, dst = refs[n_in + nc:n_in + nc + n_out], refs[n_in + nc + n_out:n_in + 2 * nc + n_out]
        for cast, s, d in zip(casts, src, dst):
            d[...] = _bf(s[...].T if cast.transpose else s[...])
        body(*ins, *outs, *refs[n_in + 2 * nc + n_out:])

    return kernel_fn


def _step(i):
    return i


def _ctx_block(i, ctx_steps):
    return jnp.minimum(i, ctx_steps - 1)


def _lat_block(i, ctx_steps):
    return jnp.maximum(i - ctx_steps, 0)


def _mod_row(i, ctx_steps, tm, lat_len):
    return jnp.where(i < ctx_steps, 0, 1 + ((i - ctx_steps) * tm) // lat_len)


def _pick(is_ctx, refs, idx):
    if len(refs) == 1:
        return refs[0][idx]
    return jnp.where(is_ctx, refs[0][idx], refs[1][idx])


@functools.lru_cache(maxsize=None)
def _plain_dft_tables(L):
    n = 2 * L
    f = np.arange(L)[:, None]
    s = np.arange(L)[None, :]
    ang = 2.0 * np.pi * ((f * s) % n) / n
    c = np.cos(ang)
    sn = np.sin(ang)
    sn[0, :] = np.where(np.arange(L) % 2 == 0, 1.0, -1.0)
    fwd = np.concatenate([c, sn], axis=0)
    w = np.full((2 * L, 1), 2.0 / n)
    w[0, 0] = 1.0 / n
    w[L, 0] = 1.0 / n
    inv = (fwd * w).T
    alt = np.where(np.arange(L) % 2 == 0, 1.0, -1.0).astype(np.float32)[:, None]
    return (np.ascontiguousarray(fwd).astype(np.float32), np.ascontiguousarray(inv).astype(np.float32), alt)


@functools.lru_cache(maxsize=None)
def _dft_tables(L):
    n = 2 * L
    half = L // 2
    f = np.arange(half)[:, None]
    m = np.arange(half)[None, :]
    alt = np.where(np.arange(half) % 2 == 0, 1.0, -1.0)
    w = np.full((L, 1), 2.0 / n)
    w[0, 0] = 1.0 / n
    w[half, 0] = 1.0 / n
    fwd, inv = [], []
    for parity in range(2):
        ang = 2.0 * np.pi * ((f * (2 * m + parity)) % n) / n
        sn = np.sin(ang)
        sn[0, :] = alt
        table = np.concatenate([np.cos(ang), sn], axis=0)
        fwd.append(np.ascontiguousarray(table).astype(np.float32))
        inv.append(np.ascontiguousarray((table * w).T).astype(np.float32))
    return fwd, inv


@functools.lru_cache(maxsize=None)
def _filter_dft_tables(L):
    n = 2 * L
    f = np.arange(L // 2)[:, None]
    t = np.arange(L)[None, :]
    ang = 2.0 * np.pi * ((f * t) % n) / n
    table = np.concatenate([np.cos(ang), np.sin(ang)], axis=0).astype(np.float32)
    tt = np.arange(L)
    cols = np.stack([np.where(tt % 2 == 0, 1.0, -1.0), np.cos(np.pi * (tt % 4) / 2.0),
                     np.sin(np.pi * (tt % 4) / 2.0)], axis=1)
    return np.ascontiguousarray(table), np.round(cols).astype(np.float32)


@functools.lru_cache(maxsize=None)
def _filter_features(L):
    t = np.linspace(0.0, 1.0, L, dtype=np.float32)[:, None]
    w = (np.float32(2.0 * math.pi / L) * np.arange(L, dtype=np.float32))[:, None]
    bands = np.linspace(1e-4, N_BANDS - 1, N_BANDS, dtype=np.float32)[None, :]
    arg = (bands * w).astype(np.float64)
    feats = np.concatenate([t.astype(np.float64), np.cos(arg), np.sin(arg)], axis=-1)
    out = np.zeros((L, LANES), np.float32)
    out[:, :PE_DIM] = feats
    return out, t


HY_SPLIT_MIN_LEN = 512


def _filter_coef_shape(L):
    return (8, L // 2) if L >= HY_SPLIT_MIN_LEN else (3, L)


def _ada_tile(layer, cctx_ref, c_ref, w_ref, b_ref, o_ref, cvec_ref):
    nb = c_ref.shape[0]
    cvec_ref[...] = jnp.zeros_like(cvec_ref)
    cvec_ref[0:1, :] = cctx_ref[...]
    cvec_ref[1:1 + nb, :] = c_ref[...]
    c = cvec_ref[...]
    o_ref[...] = _dot3(c * jax.nn.sigmoid(c), w_ref[...]) + b_ref[pl.ds(layer, 1), :]


def _filter_tile(is_first, feats_ref, t_ref, cols_ref, tab_ref, w1_ref, b1_ref, w2_ref, b2_ref, freq_ref,
                 w3f_ref, w3b_ref, decf_ref, decb_ref, out_ref, h_ref, w1p_ref, *, L):
    hi = lax.Precision.HIGHEST

    @pl.when(is_first)
    def _():
        w1p_ref[...] = jnp.zeros_like(w1p_ref)
        w1p_ref[0:PE_DIM, :] = w1_ref[...]
        h1 = jnp.sin(freq_ref[0:1, :] * (jnp.dot(feats_ref[...], w1p_ref[...], precision=hi,
                                                 preferred_element_type=F32) + b1_ref[...]))
        h_ref[...] = jnp.sin(freq_ref[1:2, :] * (jnp.dot(h1, w2_ref[...], precision=hi,
                                                         preferred_element_type=F32) + b2_ref[...]))

    h = h_ref[...]
    t = t_ref[...]
    kf = _dot3(h, w3f_ref[...]) * (jnp.exp(-t * jnp.abs(decf_ref[...])) + MOD_SHIFT)
    kb = _dot3(h, w3b_ref[...]) * (jnp.exp(-t * jnp.abs(decb_ref[...])) + MOD_SHIFT)
    row = lax.broadcasted_iota(jnp.int32, kf.shape, 0)
    kb = jnp.where(row == 0, 0.0, kb)
    ksum = kf + kb
    kdiff = kf - kb
    alt = cols_ref[:, 0:1]

    def spectrum(table, x):
        hi = _bf(x)
        return _dot(table, hi) + _dot(table, _bf(x - hi.astype(F32)))

    if L < HY_SPLIT_MIN_LEN:
        k_c = spectrum(tab_ref[0:L, :], ksum)
        k_s = spectrum(tab_ref[L:2 * L, :], kdiff)
        k_nyq = jnp.sum(ksum * alt, axis=0, keepdims=True)
        out_ref[0] = k_c
        out_ref[1] = jnp.where(row == 0, 0.0, k_s)
        out_ref[2] = jnp.where(row == 0, k_nyq, k_c)
        return

    half = L // 2
    cos_m = tab_ref[0:half, :]
    sin_m = tab_ref[half:L, :]
    k_c = spectrum(cos_m, ksum)
    k_s = spectrum(sin_m, kdiff)
    u_c = spectrum(cos_m, alt * ksum)
    u_s = -spectrum(sin_m, alt * kdiff)
    kc_h = jnp.sum(ksum * cols_ref[:, 1:2], axis=0, keepdims=True)
    ks_h = jnp.sum(kdiff * cols_ref[:, 2:3], axis=0, keepdims=True)
    first = lax.broadcasted_iota(jnp.int32, k_c.shape, 0) == 0
    out_ref[0] = k_c
    out_ref[1] = k_s
    out_ref[2] = u_c
    out_ref[3] = u_s
    out_ref[4] = jnp.where(first, kc_h - ks_h, k_c)
    out_ref[5] = jnp.where(first, kc_h + ks_h, u_c)
    out_ref[6] = jnp.where(first, kc_h + ks_h, k_c)
    out_ref[7] = jnp.where(first, kc_h - ks_h, u_c)


def _prep_kernel(cctx_ref, c_ref, w_ref, b_ref, fa_ref, ta_ref, ca_ref, tba_ref, fb_ref, tb_ref, cb_ref,
                 tbb_ref, w1_ref, b1_ref, w2_ref, b2_ref, freq_ref, w3f_ref, w3b_ref, decf_ref, decb_ref,
                 mod_ref, coefa_ref, coefb_ref, cvec_ref, ha_ref, hb_ref, w1p_ref, *, lens, tiles):
    s = pl.program_id(0)
    _ada_tile(s // tiles, cctx_ref, c_ref, w_ref, b_ref, mod_ref, cvec_ref)
    shared = (w1_ref, b1_ref, w2_ref, b2_ref, freq_ref, w3f_ref, w3b_ref, decf_ref, decb_ref)

    @pl.when(s < tiles)
    def _():
        _filter_tile(s == 0, fa_ref, ta_ref, ca_ref, tba_ref, *shared, coefa_ref, ha_ref, w1p_ref, L=lens[0])

    @pl.when(s >= tiles)
    def _():
        _filter_tile(s == tiles, fb_ref, tb_ref, cb_ref, tbb_ref, *shared, coefb_ref, hb_ref, w1p_ref, L=lens[1])


def _prep(c_ctx, c, ada_w, ada_b, lens, pe_w1, pe_b1, pe_w2, pe_b2, pe_w3, sin_freq, decay, casts=()):
    assert 1 + c.shape[0] <= MOD_ROWS and DEPTH == 2
    tn = ADA_COLS
    dt = FILTER_CHANNELS
    tiles = D_MODEL // dt
    assert 6 * D_MODEL // tn == tiles
    full = lambda shape: pl.BlockSpec(shape, lambda s: (0,) * len(shape))
    consts, const_specs, out_shapes, out_specs, scratch = [], [], [], [], []
    for k, L in enumerate(lens):
        feats, t = _filter_features(L)
        if L >= HY_SPLIT_MIN_LEN:
            table, cols = _filter_dft_tables(L)
        else:
            table, _, cols = _plain_dft_tables(L)
        consts += [feats, t, cols, _bf(jnp.asarray(table))]
        const_specs += [full(feats.shape), full(t.shape), full(cols.shape), full(table.shape)]
        nc, nf = _filter_coef_shape(L)
        out_shapes.append(jax.ShapeDtypeStruct((nc, nf, D_MODEL), F32))
        block = (lambda s: (0, 0, jnp.minimum(s, tiles - 1))) if k == 0 else (
            lambda s: (0, 0, jnp.maximum(s - tiles, 0)))
        out_specs.append(pl.BlockSpec((nc, nf, dt), block))
        scratch.append(pltpu.VMEM((L, FILT_W), F32))
    b1 = pe_b1.reshape(1, FILT_W)
    b2 = pe_b2.reshape(1, FILT_W)
    dec = decay.reshape(1, 2 * D_MODEL)
    scratch.append(pltpu.VMEM((LANES, FILT_W), F32))
    return pl.pallas_call(
        _with_casts(functools.partial(_prep_kernel, lens=tuple(lens), tiles=tiles), 21, 3, casts),
        out_shape=(jax.ShapeDtypeStruct((DEPTH, MOD_ROWS, 6 * D_MODEL), F32), *out_shapes,
                   *[k.out_shape for k in casts]),
        grid=(DEPTH * tiles,),
        in_specs=[
            full((1, D_MODEL)), full(c.shape),
            pl.BlockSpec((None, D_MODEL, tn), lambda s: (s // tiles, 0, s % tiles)),
            pl.BlockSpec((DEPTH, tn), lambda s: (0, s % tiles)),
            *const_specs,
            full((PE_DIM, FILT_W)), full((1, FILT_W)), full((FILT_W, FILT_W)), full((1, FILT_W)),
            full((2, FILT_W)),
            pl.BlockSpec((FILT_W, dt), lambda s: (0, s % tiles)),
            pl.BlockSpec((FILT_W, dt), lambda s: (0, tiles + s % tiles)),
            pl.BlockSpec((1, dt), lambda s: (0, s % tiles)),
            pl.BlockSpec((1, dt), lambda s: (0, tiles + s % tiles)),
            *[k.in_spec for k in casts],
        ],
        out_specs=(pl.BlockSpec((None, MOD_ROWS, tn), lambda s: (s // tiles, 0, s % tiles)), *out_specs,
                   *[k.out_spec for k in casts]),
        scratch_shapes=[pltpu.VMEM((MOD_ROWS, D_MODEL), F32), *scratch],
        compiler_params=_params("arbitrary"),
        name="prep",
    )(c_ctx.reshape(1, D_MODEL), c, ada_w, ada_b, *consts, pe_w1, b1, pe_w2, b2, sin_freq, pe_w3, pe_w3, dec, dec,
      *[k.w for k in casts])


def _hy_in_kernel(xc_ref, xl_ref, mod_ref, w_ref, cw_ref, cb_ref, x0_ref, vx_ref, *, ctx_steps, ctx_len, cn):
    i = pl.program_id(0)
    is_ctx = i < ctx_steps
    tm = x0_ref.shape[0]
    mod = mod_ref[pl.ds(_mod_row(i, ctx_steps, tm, tm), 1), :]
    x = _pick(is_ctx, (xc_ref, xl_ref), (slice(None), slice(None)))
    h = _bf(x * (1.0 + mod[:, D_MODEL:2 * D_MODEL]) + mod[:, 0:D_MODEL])
    row = lax.broadcasted_iota(jnp.int32, (SUBLANES, cn), 0)

    def conv(col):
        z = _dot(h, w_ref[:, col:col + cn])
        zm = pltpu.roll(z, 1, 0)
        zp = pltpu.roll(z, tm - 1, 0)
        w0 = cw_ref[0:1, col:col + cn]
        w1 = cw_ref[1:2, col:col + cn]
        w2 = cw_ref[2:3, col:col + cn]
        b = cb_ref[:, col:col + cn]
        out = zm * w0 + z * w1 + zp * w2 + b
        pieces = []
        for s in range(tm // ctx_len):
            lo = s * ctx_len
            hi = lo + ctx_len - SUBLANES
            head = (jnp.where(row == 0, 0.0, zm[lo:lo + SUBLANES]) * w0 + z[lo:lo + SUBLANES] * w1
                    + zp[lo:lo + SUBLANES] * w2 + b)
            tail = (zm[hi:hi + SUBLANES] * w0 + z[hi:hi + SUBLANES] * w1
                    + jnp.where(row == SUBLANES - 1, 0.0, zp[hi:hi + SUBLANES]) * w2 + b)
            if lo > 0:
                head = jnp.where(is_ctx, head, out[lo:lo + SUBLANES])
            if hi + SUBLANES < tm:
                tail = jnp.where(is_ctx, tail, out[hi:hi + SUBLANES])
            pieces += [head, out[lo + SUBLANES:hi], tail]
        return jnp.concatenate(pieces, axis=0)

    for c in range(D_MODEL // cn):
        x0_ref[:, c * cn:(c + 1) * cn] = _bf(conv(c * cn))
        x1 = conv(D_MODEL + c * cn)
        v = conv(2 * D_MODEL + c * cn)
        vx_ref[:, c * cn:(c + 1) * cn] = _bf(v * x1)


def _hy_in(x_ctx, x_lat, ctx_len, lat_len, mods, layer, w_in, sconv_w, sconv_b, casts=()):
    tm = lat_len
    assert tm % ctx_len == 0 and x_ctx.shape[0] % tm == 0
    ctx_steps = x_ctx.shape[0] // tm
    n = x_ctx.shape[0] + x_lat.shape[0]
    tok = pl.BlockSpec((tm, D_MODEL), lambda i: (i, 0))
    const = lambda shape: pl.BlockSpec(shape, lambda i: (0, 0), pipeline_mode=pl.Buffered(1))
    act = jax.ShapeDtypeStruct((n, D_MODEL), BF16)
    body = functools.partial(_hy_in_kernel, ctx_steps=ctx_steps, ctx_len=ctx_len, cn=HY_IN_COLS)
    return pl.pallas_call(
        _with_casts(body, 6, 2, casts),
        out_shape=(act, act) + tuple(c.out_shape for c in casts),
        grid=(n // tm,),
        in_specs=[
            pl.BlockSpec((tm, D_MODEL), lambda i: (_ctx_block(i, ctx_steps), 0)),
            pl.BlockSpec((tm, D_MODEL), lambda i: (_lat_block(i, ctx_steps), 0)),
            pl.BlockSpec((None, MOD_ROWS, 6 * D_MODEL), lambda i: (layer, 0, 0)),
            const((D_MODEL, 3 * D_MODEL)),
            pl.BlockSpec((3, 3 * D_MODEL), lambda i: (0, 0)),
            pl.BlockSpec((1, 3 * D_MODEL), lambda i: (0, 0)),
        ] + [c.in_spec for c in casts],
        out_specs=(tok, tok) + tuple(c.out_spec for c in casts),
        compiler_params=_params("arbitrary"),
        name="hyena_in",
    )(x_ctx, x_lat, mods, w_in, sconv_w, sconv_b, *[c.w for c in casts])


def _hy_conv_plain_kernel(vx_ref, x0_ref, coef_ref, bias_ref, fwd_ref, inv_ref, u_ref, *, L, nseq, cn):
    pieces = [(c, s) for c in range(vx_ref.shape[1] // cn) for s in range(nseq)]

    def forward(piece):
        c, s = piece
        vx = vx_ref[s * L:(s + 1) * L, c * cn:(c + 1) * cn]
        return vx, _dot(fwd_ref[...], vx)

    pending = [forward(p) for p in pieces[:HY_CONV_LOOKAHEAD]]
    for i, (c, s) in enumerate(pieces):
        if i + HY_CONV_LOOKAHEAD < len(pieces):
            pending.append(forward(pieces[i + HY_CONV_LOOKAHEAD]))
        vx, spec = pending.pop(0)
        rows = slice(s * L, (s + 1) * L)
        cols = slice(c * cn, (c + 1) * cn)
        k_cos = coef_ref[0, :, cols]
        k_sin = coef_ref[1, :, cols]
        k_mix = coef_ref[2, :, cols]
        v_cos = spec[0:L]
        v_sin = spec[L:2 * L]
        y_spec = _bf(jnp.concatenate([v_cos * k_cos - v_sin * k_sin, v_cos * k_sin + v_sin * k_mix], axis=0))
        y = _dot(inv_ref[...], y_spec)
        u = (y + bias_ref[:, cols] * vx.astype(F32)) * x0_ref[rows, cols].astype(F32)
        u_ref[rows, cols] = _bf(u)


def _hy_conv_split_kernel(vx_ref, x0_ref, coef_ref, bias_ref, fe_ref, fo_ref, ge_ref, go_ref, u_ref,
                          sv_ref, sx_ref, su_ref, *, L, nseq, cn):
    half = L // 2
    for g in range(vx_ref.shape[1] // LANES):
        lanes = slice(g * LANES, (g + 1) * LANES)
        sv_ref[g] = vx_ref[:, lanes].astype(F32)
        sx_ref[g] = x0_ref[:, lanes].astype(F32)

    def load(ref, start, c):
        return jnp.concatenate([ref[g, pl.ds(start, half, stride=2), :]
                                for g in range(c * cn // LANES, (c + 1) * cn // LANES)], axis=1)

    def store(ref, start, c, val):
        for k, g in enumerate(range(c * cn // LANES, (c + 1) * cn // LANES)):
            ref[g, pl.ds(start, half, stride=2), :] = val[:, k * LANES:(k + 1) * LANES]

    pieces = [(c, s) for c in range(vx_ref.shape[1] // cn) for s in range(nseq)]

    def forward(piece):
        c, s = piece
        v_e = load(sv_ref, s * L, c)
        v_o = load(sv_ref, s * L + 1, c)
        return v_e, v_o, _dot(fe_ref[...], _bf(v_e)), _dot(fo_ref[...], _bf(v_o))

    pending = [forward(p) for p in pieces[:HY_CONV_LOOKAHEAD]]
    for i, (c, s) in enumerate(pieces):
        if i + HY_CONV_LOOKAHEAD < len(pieces):
            pending.append(forward(pieces[i + HY_CONV_LOOKAHEAD]))
        v_e, v_o, a, b = pending.pop(0)
        cols = slice(c * cn, (c + 1) * cn)
        k_c, k_s, u_c, u_s, e2, e3, f2, f3 = [coef_ref[k, :, cols] for k in range(coef_ref.shape[0])]
        s_c, s_s = a[0:half] + b[0:half], a[half:L] + b[half:L]
        d_c, d_s = a[0:half] - b[0:half], a[half:L] - b[half:L]
        t1 = s_c * k_c - s_s * k_s
        t2 = d_c * u_c + d_s * u_s
        q1 = s_c * k_s
        q4 = d_c * u_s
        p_spec = jnp.concatenate([t1 + t2, q1 + s_s * e2 + d_s * e3 - q4], axis=0)
        m_spec = jnp.concatenate([t1 - t2, q1 + s_s * f2 - d_s * f3 + q4], axis=0)
        y_e = _dot(ge_ref[...], _bf(p_spec))
        y_o = _dot(go_ref[...], _bf(m_spec))
        bias = bias_ref[:, cols]
        store(su_ref, s * L, c, (y_e + bias * v_e) * load(sx_ref, s * L, c))
        store(su_ref, s * L + 1, c, (y_o + bias * v_o) * load(sx_ref, s * L + 1, c))
        rows = slice(s * L, (s + 1) * L)
        for g in range(c * cn // LANES, (c + 1) * cn // LANES):
            u_ref[rows, g * LANES:(g + 1) * LANES] = _bf(su_ref[g, rows, :])


def _hy_conv_steps(n):
    nt = n // HY_CONV_TOKENS
    return (D_MODEL // HY_CONV_CHANNELS) * nt, lambda j, i: j * nt + i


def _hy_conv(vx, x0, tok0, n, coef, bias, L, casts=()):
    tm = HY_CONV_TOKENS
    dt = HY_CONV_CHANNELS
    assert tok0 % tm == 0 and n % tm == 0 and tm % L == 0
    src = pl.BlockSpec((tm, dt), lambda j, i: (tok0 // tm + i, j))
    dst = pl.BlockSpec((tm, dt), lambda j, i: (i, j))
    const = lambda shape: pl.BlockSpec(shape, lambda j, i: (0, 0), pipeline_mode=pl.Buffered(1))
    if L >= HY_SPLIT_MIN_LEN:
        fwd, inv = _dft_tables(L)
        tables = (*fwd, *inv)
        body = functools.partial(_hy_conv_split_kernel, L=L, nseq=tm // L, cn=256)
        scratch = [pltpu.VMEM((dt // LANES, tm, LANES), F32)] * 3
    else:
        tables = _plain_dft_tables(L)[:2]
        body = functools.partial(_hy_conv_plain_kernel, L=L, nseq=tm // L, cn=256)
        scratch = []
    return pl.pallas_call(
        _with_casts(body, 4 + len(tables), 1, casts),
        out_shape=(jax.ShapeDtypeStruct((n, D_MODEL), BF16),) + tuple(c.out_shape for c in casts),
        grid=(D_MODEL // dt, n // tm),
        in_specs=[
            src, src,
            pl.BlockSpec(_filter_coef_shape(L) + (dt,), lambda j, i: (0, 0, j)),
            pl.BlockSpec((1, dt), lambda j, i: (0, j)),
        ] + [const(t.shape) for t in tables] + [c.in_spec for c in casts],
        out_specs=(dst,) + tuple(c.out_spec for c in casts),
        scratch_shapes=scratch,
        compiler_params=_params("arbitrary", "arbitrary"),
        name=f"hyena_conv_{L}",
    )(vx, x0, coef, bias, *[_bf(jnp.asarray(t)) for t in tables], *[c.w for c in casts])


def _post_kernel(*refs, layer, ctx_steps, lat_len, n_x, n_a, a_feature_major):
    x_refs, a_refs = refs[:n_x], refs[n_x:n_x + n_a]
    (mod_ref, wo_ref, ln1g_ref, ln1b_ref, win_ref, wout_ref, ln2g_ref, ln2b_ref, o_ref,
     act_ref) = refs[n_x + n_a:]
    d = D_MODEL
    i = pl.program_id(0)
    is_ctx = i < ctx_steps
    tm = act_ref.shape[0]
    mod = mod_ref[pl.ds(_mod_row(i, ctx_steps, tm, lat_len), 1), :]
    g1 = mod[:, 2 * d:3 * d]
    sh2 = mod[:, 3 * d:4 * d]
    sc2 = mod[:, 4 * d:5 * d]
    g2 = mod[:, 5 * d:6 * d]
    ln1 = (ln1g_ref[layer:layer + 1, :], ln1b_ref[layer:layer + 1, :])
    ln2 = (ln2g_ref[layer:layer + 1, :], ln2b_ref[layer:layer + 1, :])

    gr = tm // POST_ROW_GROUPS
    groups = [slice(r * gr, (r + 1) * gr) for r in range(POST_ROW_GROUPS)]
    if a_feature_major:
        mix = [_dot_tn(_pick(is_ctx, a_refs, (slice(None), rows)), wo_ref[...]) for rows in groups]
    else:
        mix = [_dot(_pick(is_ctx, a_refs, (rows, slice(None))), wo_ref[...]) for rows in groups]
    x1 = []
    for rows, mx in zip(groups, mix):
        x = _pick(is_ctx, x_refs, (rows, slice(None)))
        x1.append(_layer_norm(ALPHA * x + g1 * mx, *ln1))
        h = _bf(x1[-1] * (1.0 + sc2) + sh2)
        for c in range(D_FF // FFN_COLS):
            cols = slice(c * FFN_COLS, (c + 1) * FFN_COLS)
            gate = _dot(h, win_ref[:, cols])
            up = _dot(h, win_ref[:, D_FF + c * FFN_COLS:D_FF + (c + 1) * FFN_COLS])
            act_ref[rows, cols] = _bf(gate * jax.nn.sigmoid(gate) * up)
    for rows, xr in zip(groups, x1):
        f = _dot(act_ref[rows, :], wout_ref[...])
        o_ref[rows, :] = _layer_norm(ALPHA * xr + g2 * f, *ln2)


def _post(xs, mixes, n, n_ctx, lat_len, mods, layer, w_o, w_in, w_out, ln1_g, ln1_b, ln2_g, ln2_b, name,
          x_tok0=0, a_feature_major=False, casts=()):
    tm = POST_TOKENS
    ctx_steps = n_ctx // tm
    ctx = lambda i: _ctx_block(i, ctx_steps)
    lat = lambda i: _lat_block(i, ctx_steps)
    blocks = lambda m, tok0: (lambda i: tok0 // tm + i,) if m == 1 else (ctx, lat)
    tok = lambda f: pl.BlockSpec((tm, D_MODEL), lambda i: (f(i), 0))
    feat = lambda f: pl.BlockSpec((D_MODEL, tm), lambda i: (0, f(i)))
    const = lambda shape: pl.BlockSpec(shape, lambda i: (0, 0), pipeline_mode=pl.Buffered(1))
    vec = pl.BlockSpec((DEPTH, D_MODEL), lambda i: (0, 0))
    body = functools.partial(_post_kernel, layer=layer, ctx_steps=ctx_steps, lat_len=lat_len, n_x=len(xs),
                             n_a=len(mixes), a_feature_major=a_feature_major)
    return pl.pallas_call(
        _with_casts(body, len(xs) + len(mixes) + 8, 1, casts),
        out_shape=(jax.ShapeDtypeStruct((n, D_MODEL), F32),) + tuple(c.out_shape for c in casts),
        grid=(n // tm,),
        in_specs=[tok(f) for f in blocks(len(xs), x_tok0)]
        + [(feat if a_feature_major else tok)(f) for f in blocks(len(mixes), 0)]
        + [pl.BlockSpec((None, MOD_ROWS, 6 * D_MODEL), lambda i: (layer, 0, 0)),
           const((D_MODEL, D_MODEL)), vec, vec,
           const((D_MODEL, 2 * D_FF)), const((D_FF, D_MODEL)), vec, vec]
        + [c.in_spec for c in casts],
        out_specs=(tok(lambda i: i),) + tuple(c.out_spec for c in casts),
        scratch_shapes=[pltpu.VMEM((tm, D_FF), BF16)],
        compiler_params=_params("arbitrary"),
        name=name,
    )(*xs, *mixes, mods, w_o, ln1_g, ln1_b, w_in, w_out, ln2_g, ln2_b, *[c.w for c in casts])


def _qkv_kernel(x_ref, mod_ref, wt_ref, qkvt_ref, nk_ref, nv_ref, *, ctx_steps, lat_len):
    d = D_MODEL
    i = pl.program_id(0)
    tm = x_ref.shape[0]
    mod = mod_ref[pl.ds(_mod_row(i, ctx_steps, tm, lat_len), 1), :]
    h = _bf(x_ref[...] * (1.0 + mod[:, d:2 * d]) + mod[:, 0:d])
    qkvt_ref[0:d, :] = _bf(_dot_nt(wt_ref[0:d, :], h) * (HEAD_DIM ** -0.5 * LOG2E))
    kt = _dot_nt(wt_ref[d:2 * d, :], h)
    qkvt_ref[d:2 * d, :] = _bf(kt)
    vt = _dot_nt(wt_ref[2 * d:3 * d, :], h)
    qkvt_ref[2 * d:3 * d, :] = _bf(vt)

    @pl.when(i < ctx_steps)
    def _():
        seq = nk_ref.shape[4]
        for s in range(tm // seq):
            nk_ref[s, 0] = kt[:, s * seq:(s + 1) * seq].reshape(N_HEADS, HEAD_DIM, seq)
            nv_ref[s, 0] = vt[:, s * seq:(s + 1) * seq].reshape(N_HEADS, HEAD_DIM, seq)


def _qkv(x2d, n_ctx, ctx_len, lat_len, mods, layer, w_qkv_t):
    n = x2d.shape[0]
    tm = QKV_TOKENS
    assert tm % ctx_len == 0 and n_ctx % tm == 0
    ctx_steps = n_ctx // tm
    kv_shape = jax.ShapeDtypeStruct((n_ctx // ctx_len, 1, N_HEADS, HEAD_DIM, ctx_len), F32)
    kv_spec = pl.BlockSpec((tm // ctx_len, 1, N_HEADS, HEAD_DIM, ctx_len),
                           lambda i: (_ctx_block(i, ctx_steps), 0, 0, 0, 0))
    return pl.pallas_call(
        functools.partial(_qkv_kernel, ctx_steps=ctx_steps, lat_len=lat_len),
        out_shape=(jax.ShapeDtypeStruct((3 * D_MODEL, n), BF16), kv_shape, kv_shape),
        grid=(n // tm,),
        in_specs=[
            pl.BlockSpec((tm, D_MODEL), lambda i: (i, 0)),
            pl.BlockSpec((None, MOD_ROWS, 6 * D_MODEL), lambda i: (layer, 0, 0)),
            pl.BlockSpec((3 * D_MODEL, D_MODEL), lambda i: (0, 0), pipeline_mode=pl.Buffered(1)),
        ],
        out_specs=(pl.BlockSpec((3 * D_MODEL, tm), lambda i: (0, i)), kv_spec, kv_spec),
        compiler_params=_params("arbitrary"),
        name="qkv",
    )(x2d, mods, w_qkv_t)


def _softmax_weights(*scores):
    m = functools.reduce(jnp.maximum, [jnp.max(s, axis=0, keepdims=True) for s in scores])
    return [_bf(jnp.exp2(s - m)) for s in scores]


def _weighted_values(vt, p):
    ones = jnp.ones((BF16_ROWS, vt.shape[1]), BF16)
    return _dot(jnp.concatenate([vt, ones], axis=0), p)


def _normalise(acc):
    return _bf(acc[0:HEAD_DIM] * (1.0 / acc[HEAD_DIM:HEAD_DIM + 1]))


def _ctx_attn_kernel(qt_ref, kt_ref, vt_ref, ot_ref):
    def rows(hh):
        return slice(hh * HEAD_DIM, (hh + 1) * HEAD_DIM)

    def scores(hh):
        return _dot_tn(kt_ref[rows(hh), :], qt_ref[rows(hh), :])

    pending = [scores(hh) for hh in range(CTX_ATTN_LOOKAHEAD)]
    for hh in range(N_HEADS):
        if hh + CTX_ATTN_LOOKAHEAD < N_HEADS:
            pending.append(scores(hh + CTX_ATTN_LOOKAHEAD))
        (p,) = _softmax_weights(pending.pop(0))
        ot_ref[rows(hh), :] = _normalise(_weighted_values(vt_ref[rows(hh), :], p))


def _ctx_attn(qkvt, n_ctx, seq):
    return pl.pallas_call(
        _ctx_attn_kernel,
        out_shape=jax.ShapeDtypeStruct((D_MODEL, n_ctx), BF16),
        grid=(n_ctx // seq,),
        in_specs=[pl.BlockSpec((D_MODEL, seq), lambda b, c=c: (c, b)) for c in range(3)],
        out_specs=pl.BlockSpec((D_MODEL, seq), lambda b: (0, b)),
        compiler_params=_params("arbitrary"),
        name="ctx_attn",
    )(qkvt, qkvt, qkvt)


LAT_ROWS = 16
Q_BLOCK_ROWS = 4
KEY_TILE_ROWS = 4


def _key_band(qb):
    row0 = [min(max(r - WIN_H // 2, 0), LAT_ROWS - WIN_H) for r in range(qb * Q_BLOCK_ROWS, (qb + 1) * Q_BLOCK_ROWS)]
    lo = min(row0) // KEY_TILE_ROWS * KEY_TILE_ROWS
    hi = -(-(max(row0) + WIN_H) // KEY_TILE_ROWS) * KEY_TILE_ROWS
    return lo, hi


KEY_BANDS = tuple(_key_band(qb) for qb in range(LAT_ROWS // Q_BLOCK_ROWS))
MAX_BAND_ROWS = max(hi - lo for lo, hi in KEY_BANDS)


def _fill_bias_table(rpb_ref, tab_ref):
    shape = (GRID_W, LANES)
    kc = lax.broadcasted_iota(jnp.int32, shape, 0)
    lane = lax.broadcasted_iota(jnp.int32, shape, 1)
    qc = lane % GRID_W
    win0 = jnp.clip(qc - WIN_W // 2, 0, GRID_W - WIN_W)
    col_ok = (kc >= win0) & (kc < win0 + WIN_W)
    masked = jnp.full(shape, MASK_VALUE, F32)
    tiles = []
    for dr in range(2 * WIN_H - 1):
        row = jnp.broadcast_to(rpb_ref[dr:dr + 1, :] * LOG2E, shape)
        t = pltpu.roll(row, LANES - (WIN_W - 1), 1, stride=1, stride_axis=0)
        t = jnp.where(lane < GRID_W, t, pltpu.roll(t, GRID_W, 1))
        tiles.append(jnp.where(col_ok, t, MASK_VALUE))

    def tile(qr, kr):
        row0 = min(max(qr - WIN_H // 2, 0), LAT_ROWS - WIN_H)
        if row0 <= kr < row0 + WIN_H:
            return tiles[kr - qr + WIN_H - 1]
        return masked

    for qr in range(0, LAT_ROWS, 2):
        lo, hi = KEY_BANDS[qr // Q_BLOCK_ROWS]
        for i in range(hi - lo):
            both = jnp.where(lane < GRID_W, tile(qr, lo + i), tile(qr + 1, lo + i))
            tab_ref[i * GRID_W:(i + 1) * GRID_W, qr * GRID_W:(qr + 2) * GRID_W] = both


def _lat_attn_kernel(qt_ref, kt_ref, vt_ref, ckt_ref, cvt_ref, rpb_ref, ot_ref, tab_ref):
    seq = LAT_ROWS * GRID_W
    nq = Q_BLOCK_ROWS * GRID_W
    heads = qt_ref.shape[0] // HEAD_DIM
    units = [(b, hh, qb) for b in range(qt_ref.shape[1] // seq) for hh in range(heads)
             for qb in range(len(KEY_BANDS))]

    def rows(hh):
        return slice(hh * HEAD_DIM, (hh + 1) * HEAD_DIM)

    def keys(b, qb):
        lo, hi = KEY_BANDS[qb]
        return slice(b * seq + lo * GRID_W, b * seq + hi * GRID_W)

    def queries(b, qb):
        return slice(b * seq + qb * nq, b * seq + (qb + 1) * nq)

    def scores(unit):
        b, hh, qb = unit
        lo, hi = KEY_BANDS[qb]
        qt = qt_ref[rows(hh), queries(b, qb)]
        s_loc = (_dot_tn(kt_ref[rows(hh), keys(b, qb)], qt)
                 + tab_ref[hh, 0:(hi - lo) * GRID_W, qb * nq:(qb + 1) * nq])
        s_ctx = _dot_tn(_bf(ckt_ref[b, hh]), qt)
        return s_loc, s_ctx

    for hh in range(tab_ref.shape[0]):
        _fill_bias_table(rpb_ref.at[hh], tab_ref.at[hh])
    pending = [scores(u) for u in units[:LAT_ATTN_LOOKAHEAD]]
    for i, (b, hh, qb) in enumerate(units):
        if i + LAT_ATTN_LOOKAHEAD < len(units):
            pending.append(scores(units[i + LAT_ATTN_LOOKAHEAD]))
        s_loc, s_ctx = pending.pop(0)
        p_loc, p_ctx = _softmax_weights(s_loc, s_ctx)
        acc = (_weighted_values(vt_ref[rows(hh), keys(b, qb)], p_loc)
               + _weighted_values(_bf(cvt_ref[b, hh]), p_ctx))
        ot_ref[rows(hh), queries(b, qb)] = _normalise(acc)


def _lat_attn(qkvt, n_ctx, cache_kt, cache_vt, rpb, layer_idx):
    rpb = jnp.pad(rpb[:, :, ::-1], ((0, 0), (0, 1), (0, LANES - (2 * WIN_W - 1))))
    n = qkvt.shape[1] - n_ctx
    seq = LAT_ROWS * GRID_W
    assert n_ctx % n == 0 and n % seq == 0
    nb = n // seq
    nhp = N_HEADS // 2
    past = cache_kt.shape[4]
    feat = lambda c: pl.BlockSpec((2 * HEAD_DIM, n), lambda hp, c=c: (c * nhp + hp, n_ctx // n))
    cache = pl.BlockSpec((nb, None, 2, HEAD_DIM, past), lambda hp: (0, layer_idx, hp, 0, 0))
    return pl.pallas_call(
        _lat_attn_kernel,
        out_shape=jax.ShapeDtypeStruct((D_MODEL, n), BF16),
        grid=(nhp,),
        in_specs=[feat(0), feat(1), feat(2), cache, cache,
                  pl.BlockSpec((2, 2 * WIN_H, LANES), lambda hp: (hp, 0, 0))],
        out_specs=pl.BlockSpec((2 * HEAD_DIM, n), lambda hp: (hp, 0)),
        scratch_shapes=[pltpu.VMEM((2, MAX_BAND_ROWS * GRID_W, seq), F32)],
        compiler_params=_params("arbitrary"),
        name="lat_attn",
    )(qkvt, qkvt, qkvt, cache_kt, cache_vt, rpb)


def kernel(x_prompt, x_sample, cache_k, cache_v, c, c_ctx, ada_w, ada_b, ln1_g, ln1_b, ln2_g, ln2_b, ffn_w_in, ffn_w_out, hy_w_in, hy_sconv_w, hy_sconv_b, hy_pe_w1, hy_pe_b1, hy_pe_w2, hy_pe_b2, hy_pe_w3, hy_sin_freq, hy_decay, hy_bias, hy_w_out, na_w_qkv, na_rpb, na_w_out):
    nbp, lp, d = x_prompt.shape
    nbs, ls, _ = x_sample.shape
    assert d == D_MODEL and ls == LAT_ROWS * GRID_W and c.shape[0] == nbs
    xp = x_prompt.reshape(nbp * lp, d)
    xs = x_sample.reshape(nbs * ls, d)
    n_ctx, n_lat = xp.shape[0], xs.shape[0]
    n = n_ctx + n_lat
    ln = (ln1_g, ln1_b, ln2_g, ln2_b)

    filt = (hy_pe_w1[0], hy_pe_b1[0], hy_pe_w2[0], hy_pe_b2[0], hy_pe_w3[0], hy_sin_freq[0], hy_decay[0])
    mods, coef_p, coef_s, hy_w_in0 = _prep(c_ctx, c, ada_w, ada_b, (lp, ls), *filt,
                                           casts=[_Cast(hy_w_in, 0, DEPTH * D_MODEL // FILTER_CHANNELS, _step)])

    sconv_b = hy_sconv_b[0].reshape(1, -1)
    hy_bias0 = hy_bias[0].reshape(1, -1)
    x0, vx, ffn_w_in0 = _hy_in(xp, xs, lp, ls, mods, 0, hy_w_in0, hy_sconv_w[0], sconv_b,
                               casts=[_Cast(ffn_w_in, 0, n // ls, _step)])
    steps, lin = _hy_conv_steps(n_ctx)
    u_p, hy_w_out0 = _hy_conv(vx, x0, 0, n_ctx, coef_p, hy_bias0, lp, casts=[_Cast(hy_w_out, 0, steps, lin)])
    steps, lin = _hy_conv_steps(n_lat)
    u_s, w_qkv_t, ffn_w_out0 = _hy_conv(vx, x0, n_ctx, n_lat, coef_s, hy_bias0, ls,
                                        casts=[_Cast(na_w_qkv, 0, steps, lin, transpose=True),
                                               _Cast(ffn_w_out, 0, steps, lin)])
    steps = n // POST_TOKENS
    x1, ffn_w_in1, ffn_w_out1, na_w_out0 = _post(
        (xp, xs), (u_p, u_s), n, n_ctx, ls, mods, 0, hy_w_out0, ffn_w_in0, ffn_w_out0, *ln, name="post0",
        casts=[_Cast(ffn_w_in, 1, steps, _step), _Cast(ffn_w_out, 1, steps, _step),
               _Cast(na_w_out, 0, steps, _step)])

    qkvt, new_kt, new_vt = _qkv(x1, n_ctx, lp, ls, mods, 1, w_qkv_t)
    at_p = _ctx_attn(qkvt, n_ctx, lp)
    at_s = _lat_attn(qkvt, n_ctx, jnp.swapaxes(cache_k, 3, 4), jnp.swapaxes(cache_v, 3, 4), na_rpb[0], 0)
    ffn1 = (na_w_out0, ffn_w_in1, ffn_w_out1, *ln)
    (yp,) = _post((x1,), (at_p,), n_ctx, n_ctx, ls, mods, 1, *ffn1, name="post1_ctx", a_feature_major=True)
    (ys,) = _post((x1,), (at_s,), n_lat, 0, ls, mods, 1, *ffn1, name="post1_lat", x_tok0=n_ctx,
                  a_feature_major=True)

    return (yp.reshape(nbp, lp, d), ys.reshape(nbs, ls, d),
            jnp.swapaxes(new_kt, 3, 4), jnp.swapaxes(new_vt, 3, 4))
```

```python
import functools
import math

import jax
import jax.numpy as jnp
import numpy as np
from jax import lax
from jax.experimental import pallas as pl
from jax.experimental.pallas import tpu as pltpu

D_MODEL = 1024
N_HEADS = 16
HEAD_DIM = D_MODEL // N_HEADS
D_FF = 2816
GRID_W = 64
WIN_H = 8
WIN_W = 16
N_BANDS = 16
PE_DIM = 1 + 2 * N_BANDS
FILT_W = 64
MOD_SHIFT = 0.05
DEPTH = 2
ALPHA = (2 * DEPTH) ** 0.25
LN_EPS = 1e-5
MASK_VALUE = -1e30
MOD_ROWS = 8

LANES = 128
SUBLANES = 8
BF16_ROWS = 2 * SUBLANES
CTX_ATTN_LOOKAHEAD = 6
CTX_ATTN_SEQS = 4
LAT_ATTN_LOOKAHEAD = 2
LOG2E = math.log2(math.e)
VMEM_LIMIT = 56 * 1024 * 1024

ADA_COLS = 1536
FILTER_CHANNELS = 256
HY_IN_COLS = 256
HY_CONV_TOKENS = 1024
HY_CONV_CHANNELS = 512
HY_CONV_LOOKAHEAD = 2
POST_TOKENS = 512
POST_ROW_GROUPS = 2
FFN_COLS = 256
QKV_TOKENS = 1024

F32 = jnp.float32
BF16 = jnp.bfloat16


def _bf(x):
    return x.astype(BF16)


def _dot(a, b):
    return jnp.dot(a, b, preferred_element_type=F32)


def _dot_nt(a, b):
    return lax.dot_general(a, b, (((1,), (1,)), ((), ())), preferred_element_type=F32)


def _dot_tn(a, b):
    return lax.dot_general(a, b, (((0,), (0,)), ((), ())), preferred_element_type=F32)


def _dot3(a, b):
    a_hi = _bf(a)
    a_lo = _bf(a - a_hi.astype(F32))
    b_hi = _bf(b)
    b_lo = _bf(b - b_hi.astype(F32))
    m = a.shape[0]
    both = _dot(jnp.concatenate([a_hi, a_lo], axis=0), b_hi)
    return both[0:m] + both[m:2 * m] + _dot(a_hi, b_lo)


def _layer_norm(x, g, b):
    mu = jnp.mean(x, axis=-1, keepdims=True)
    xc = x - mu
    var = jnp.mean(xc * xc, axis=-1, keepdims=True)
    return xc * lax.rsqrt(var + LN_EPS) * g + b


def _params(*sem):
    return pltpu.CompilerParams(dimension_semantics=sem, vmem_limit_bytes=VMEM_LIMIT)


class _Cast:
    def __init__(self, w, layer, steps, step_fn, transpose=False):
        _, r, c = w.shape
        assert r % (steps * BF16_ROWS) == 0
        rows = r // steps
        self.w = w
        self.transpose = transpose
        self.in_spec = pl.BlockSpec((None, rows, c), lambda *g: (layer, step_fn(*g), 0))
        if transpose:
            self.out_shape = jax.ShapeDtypeStruct((c, r), BF16)
            self.out_spec = pl.BlockSpec((c, rows), lambda *g: (0, step_fn(*g)))
        else:
            self.out_shape = jax.ShapeDtypeStruct((r, c), BF16)
            self.out_spec = pl.BlockSpec((rows, c), lambda *g: (step_fn(*g), 0))


def _with_casts(body, n_in, n_out, casts):
    nc = len(casts)

    def kernel_fn(*refs):
        ins, src = refs[:n_in], refs[n_in:n_in + nc]
        outs, dst = refs[n_in + nc:n_in + nc + n_out], refs[n_in + nc + n_out:n_in + 2 * nc + n_out]
        for cast, s, d in zip(casts, src, dst):
            d[...] = _bf(s[...].T if cast.transpose else s[...])
        body(*ins, *outs, *refs[n_in + 2 * nc + n_out:])

    return kernel_fn


def _step(i):
    return i


def _ctx_block(i, ctx_steps):
    return jnp.minimum(i, ctx_steps - 1)


def _lat_block(i, ctx_steps):
    return jnp.maximum(i - ctx_steps, 0)


def _mod_row(i, ctx_steps, tm, lat_len):
    return jnp.where(i < ctx_steps, 0, 1 + ((i - ctx_steps) * tm) // lat_len)


def _pick(is_ctx, refs, idx):
    if len(refs) == 1:
        return refs[0][idx]
    return jnp.where(is_ctx, refs[0][idx], refs[1][idx])


@functools.lru_cache(maxsize=None)
def _plain_dft_tables(L):
    n = 2 * L
    f = np.arange(L)[:, None]
    s = np.arange(L)[None, :]
    ang = 2.0 * np.pi * ((f * s) % n) / n
    c = np.cos(ang)
    sn = np.sin(ang)
    sn[0, :] = np.where(np.arange(L) % 2 == 0, 1.0, -1.0)
    fwd = np.concatenate([c, sn], axis=0)
    w = np.full((2 * L, 1), 2.0 / n)
    w[0, 0] = 1.0 / n
    w[L, 0] = 1.0 / n
    inv = (fwd * w).T
    alt = np.where(np.arange(L) % 2 == 0, 1.0, -1.0).astype(np.float32)[:, None]
    return (np.ascontiguousarray(fwd).astype(np.float32), np.ascontiguousarray(inv).astype(np.float32), alt)


@functools.lru_cache(maxsize=None)
def _dft_tables(L):
    n = 2 * L
    half = L // 2
    f = np.arange(half)[:, None]
    m = np.arange(half)[None, :]
    alt = np.where(np.arange(half) % 2 == 0, 1.0, -1.0)
    w = np.full((L, 1), 2.0 / n)
    w[0, 0] = 1.0 / n
    w[half, 0] = 1.0 / n
    fwd, inv = [], []
    for parity in range(2):
        ang = 2.0 * np.pi * ((f * (2 * m + parity)) % n) / n
        sn = np.sin(ang)
        sn[0, :] = alt
        table = np.concatenate([np.cos(ang), sn], axis=0)
        fwd.append(np.ascontiguousarray(table).astype(np.float32))
        inv.append(np.ascontiguousarray((table * w).T).astype(np.float32))
    return fwd, inv


@functools.lru_cache(maxsize=None)
def _filter_dft_tables(L):
    n = 2 * L
    f = np.arange(L // 2)[:, None]
    t = np.arange(L)[None, :]
    ang = 2.0 * np.pi * ((f * t) % n) / n
    table = np.concatenate([np.cos(ang), np.sin(ang)], axis=0).astype(np.float32)
    tt = np.arange(L)
    cols = np.stack([np.where(tt % 2 == 0, 1.0, -1.0), np.cos(np.pi * (tt % 4) / 2.0),
                     np.sin(np.pi * (tt % 4) / 2.0)], axis=1)
    return np.ascontiguousarray(table), np.round(cols).astype(np.float32)


@functools.lru_cache(maxsize=None)
def _filter_features(L):
    t = np.linspace(0.0, 1.0, L, dtype=np.float32)[:, None]
    w = (np.float32(2.0 * math.pi / L) * np.arange(L, dtype=np.float32))[:, None]
    bands = np.linspace(1e-4, N_BANDS - 1, N_BANDS, dtype=np.float32)[None, :]
    arg = (bands * w).astype(np.float64)
    feats = np.concatenate([t.astype(np.float64), np.cos(arg), np.sin(arg)], axis=-1)
    out = np.zeros((L, LANES), np.float32)
    out[:, :PE_DIM] = feats
    return out, t


HY_SPLIT_MIN_LEN = 512


def _filter_coef_shape(L):
    return (8, L // 2) if L >= HY_SPLIT_MIN_LEN else (3, L)


def _ada_tile(layer, cctx_ref, c_ref, w_ref, b_ref, o_ref, cvec_ref):
    nb = c_ref.shape[0]
    cvec_ref[...] = jnp.zeros_like(cvec_ref)
    cvec_ref[0:1, :] = cctx_ref[...]
    cvec_ref[1:1 + nb, :] = c_ref[...]
    c = cvec_ref[...]
    o_ref[...] = _dot3(c * jax.nn.sigmoid(c), w_ref[...]) + b_ref[pl.ds(layer, 1), :]


def _filter_tile(is_first, feats_ref, t_ref, cols_ref, tab_ref, w1_ref, b1_ref, w2_ref, b2_ref, freq_ref,
                 w3f_ref, w3b_ref, decf_ref, decb_ref, out_ref, h_ref, w1p_ref, *, L):
    hi = lax.Precision.HIGHEST

    @pl.when(is_first)
    def _():
        w1p_ref[...] = jnp.zeros_like(w1p_ref)
        w1p_ref[0:PE_DIM, :] = w1_ref[...]
        h1 = jnp.sin(freq_ref[0:1, :] * (jnp.dot(feats_ref[...], w1p_ref[...], precision=hi,
                                                 preferred_element_type=F32) + b1_ref[...]))
        h_ref[...] = jnp.sin(freq_ref[1:2, :] * (jnp.dot(h1, w2_ref[...], precision=hi,
                                                         preferred_element_type=F32) + b2_ref[...]))

    h = h_ref[...]
    t = t_ref[...]
    kf = _dot3(h, w3f_ref[...]) * (jnp.exp(-t * jnp.abs(decf_ref[...])) + MOD_SHIFT)
    kb = _dot3(h, w3b_ref[...]) * (jnp.exp(-t * jnp.abs(decb_ref[...])) + MOD_SHIFT)
    row = lax.broadcasted_iota(jnp.int32, kf.shape, 0)
    kb = jnp.where(row == 0, 0.0, kb)
    ksum = kf + kb
    kdiff = kf - kb
    alt = cols_ref[:, 0:1]

    def spectrum(table, x):
        hi = _bf(x)
        return _dot(table, hi) + _dot(table, _bf(x - hi.astype(F32)))

    if L < HY_SPLIT_MIN_LEN:
        k_c = spectrum(tab_ref[0:L, :], ksum)
        k_s = spectrum(tab_ref[L:2 * L, :], kdiff)
        k_nyq = jnp.sum(ksum * alt, axis=0, keepdims=True)
        out_ref[0] = k_c
        out_ref[1] = jnp.where(row == 0, 0.0, k_s)
        out_ref[2] = jnp.where(row == 0, k_nyq, k_c)
        return

    half = L // 2
    cos_m = tab_ref[0:half, :]
    sin_m = tab_ref[half:L, :]
    k_c = spectrum(cos_m, ksum)
    k_s = spectrum(sin_m, kdiff)
    u_c = spectrum(cos_m, alt * ksum)
    u_s = -spectrum(sin_m, alt * kdiff)
    kc_h = jnp.sum(ksum * cols_ref[:, 1:2], axis=0, keepdims=True)
    ks_h = jnp.sum(kdiff * cols_ref[:, 2:3], axis=0, keepdims=True)
    first = lax.broadcasted_iota(jnp.int32, k_c.shape, 0) == 0
    out_ref[0] = k_c
    out_ref[1] = k_s
    out_ref[2] = u_c
    out_ref[3] = u_s
    out_ref[4] = jnp.where(first, kc_h - ks_h, k_c)
    out_ref[5] = jnp.where(first, kc_h + ks_h, u_c)
    out_ref[6] = jnp.where(first, kc_h + ks_h, k_c)
    out_ref[7] = jnp.where(first, kc_h - ks_h, u_c)


def _prep_kernel(cctx_ref, c_ref, w_ref, b_ref, fa_ref, ta_ref, ca_ref, tba_ref, fb_ref, tb_ref, cb_ref,
                 tbb_ref, w1_ref, b1_ref, w2_ref, b2_ref, freq_ref, w3f_ref, w3b_ref, decf_ref, decb_ref,
                 mod_ref, coefa_ref, coefb_ref, cvec_ref, ha_ref, hb_ref, w1p_ref, *, lens, tiles):
    s = pl.program_id(0)
    _ada_tile(s // tiles, cctx_ref, c_ref, w_ref, b_ref, mod_ref, cvec_ref)
    shared = (w1_ref, b1_ref, w2_ref, b2_ref, freq_ref, w3f_ref, w3b_ref, decf_ref, decb_ref)

    @pl.when(s < tiles)
    def _():
        _filter_tile(s == 0, fa_ref, ta_ref, ca_ref, tba_ref, *shared, coefa_ref, ha_ref, w1p_ref, L=lens[0])

    @pl.when(s >= tiles)
    def _():
        _filter_tile(s == tiles, fb_ref, tb_ref, cb_ref, tbb_ref, *shared, coefb_ref, hb_ref, w1p_ref, L=lens[1])


def _prep(c_ctx, c, ada_w, ada_b, lens, pe_w1, pe_b1, pe_w2, pe_b2, pe_w3, sin_freq, decay, casts=()):
    assert 1 + c.shape[0] <= MOD_ROWS and DEPTH == 2
    tn = ADA_COLS
    dt = FILTER_CHANNELS
    tiles = D_MODEL // dt
    assert 6 * D_MODEL // tn == tiles
    full = lambda shape: pl.BlockSpec(shape, lambda s: (0,) * len(shape))
    consts, const_specs, out_shapes, out_specs, scratch = [], [], [], [], []
    for k, L in enumerate(lens):
        feats, t = _filter_features(L)
        if L >= HY_SPLIT_MIN_LEN:
            table, cols = _filter_dft_tables(L)
        else:
            table, _, cols = _plain_dft_tables(L)
        consts += [feats, t, cols, _bf(jnp.asarray(table))]
        const_specs += [full(feats.shape), full(t.shape), full(cols.shape), full(table.shape)]
        nc, nf = _filter_coef_shape(L)
        out_shapes.append(jax.ShapeDtypeStruct((nc, nf, D_MODEL), F32))
        block = (lambda s: (0, 0, jnp.minimum(s, tiles - 1))) if k == 0 else (
            lambda s: (0, 0, jnp.maximum(s - tiles, 0)))
        out_specs.append(pl.BlockSpec((nc, nf, dt), block))
        scratch.append(pltpu.VMEM((L, FILT_W), F32))
    b1 = pe_b1.reshape(1, FILT_W)
    b2 = pe_b2.reshape(1, FILT_W)
    dec = decay.reshape(1, 2 * D_MODEL)
    scratch.append(pltpu.VMEM((LANES, FILT_W), F32))
    return pl.pallas_call(
        _with_casts(functools.partial(_prep_kernel, lens=tuple(lens), tiles=tiles), 21, 3, casts),
        out_shape=(jax.ShapeDtypeStruct((DEPTH, MOD_ROWS, 6 * D_MODEL), F32), *out_shapes,
                   *[k.out_shape for k in casts]),
        grid=(DEPTH * tiles,),
        in_specs=[
            full((1, D_MODEL)), full(c.shape),
            pl.BlockSpec((None, D_MODEL, tn), lambda s: (s // tiles, 0, s % tiles)),
            pl.BlockSpec((DEPTH, tn), lambda s: (0, s % tiles)),
            *const_specs,
            full((PE_DIM, FILT_W)), full((1, FILT_W)), full((FILT_W, FILT_W)), full((1, FILT_W)),
            full((2, FILT_W)),
            pl.BlockSpec((FILT_W, dt), lambda s: (0, s % tiles)),
            pl.BlockSpec((FILT_W, dt), lambda s: (0, tiles + s % tiles)),
            pl.BlockSpec((1, dt), lambda s: (0, s % tiles)),
            pl.BlockSpec((1, dt), lambda s: (0, tiles + s % tiles)),
            *[k.in_spec for k in casts],
        ],
        out_specs=(pl.BlockSpec((None, MOD_ROWS, tn), lambda s: (s // tiles, 0, s % tiles)), *out_specs,
                   *[k.out_spec for k in casts]),
        scratch_shapes=[pltpu.VMEM((MOD_ROWS, D_MODEL), F32), *scratch],
        compiler_params=_params("arbitrary"),
        name="prep",
    )(c_ctx.reshape(1, D_MODEL), c, ada_w, ada_b, *consts, pe_w1, b1, pe_w2, b2, sin_freq, pe_w3, pe_w3, dec, dec,
      *[k.w for k in casts])


def _hy_in_kernel(xc_ref, xl_ref, mod_ref, w_ref, cw_ref, cb_ref, x0_ref, vx_ref, *, ctx_steps, ctx_len, cn):
    i = pl.program_id(0)
    is_ctx = i < ctx_steps
    tm = x0_ref.shape[0]
    mod = mod_ref[pl.ds(_mod_row(i, ctx_steps, tm, tm), 1), :]
    x = _pick(is_ctx, (xc_ref, xl_ref), (slice(None), slice(None)))
    h = _bf(x * (1.0 + mod[:, D_MODEL:2 * D_MODEL]) + mod[:, 0:D_MODEL])
    row = lax.broadcasted_iota(jnp.int32, (SUBLANES, cn), 0)

    def conv(col):
        z = _dot(h, w_ref[:, col:col + cn])
        zm = pltpu.roll(z, 1, 0)
        zp = pltpu.roll(z, tm - 1, 0)
        w0 = cw_ref[0:1, col:col + cn]
        w1 = cw_ref[1:2, col:col + cn]
        w2 = cw_ref[2:3, col:col + cn]
        b = cb_ref[:, col:col + cn]
        out = zm * w0 + z * w1 + zp * w2 + b
        pieces = []
        for s in range(tm // ctx_len):
            lo = s * ctx_len
            hi = lo + ctx_len - SUBLANES
            head = (jnp.where(row == 0, 0.0, zm[lo:lo + SUBLANES]) * w0 + z[lo:lo + SUBLANES] * w1
                    + zp[lo:lo + SUBLANES] * w2 + b)
            tail = (zm[hi:hi + SUBLANES] * w0 + z[hi:hi + SUBLANES] * w1
                    + jnp.where(row == SUBLANES - 1, 0.0, zp[hi:hi + SUBLANES]) * w2 + b)
            if lo > 0:
                head = jnp.where(is_ctx, head, out[lo:lo + SUBLANES])
            if hi + SUBLANES < tm:
                tail = jnp.where(is_ctx, tail, out[hi:hi + SUBLANES])
            pieces += [head, out[lo + SUBLANES:hi], tail]
        return jnp.concatenate(pieces, axis=0)

    for c in range(D_MODEL // cn):
        x0_ref[:, c * cn:(c + 1) * cn] = _bf(conv(c * cn))
        x1 = conv(D_MODEL + c * cn)
        v = conv(2 * D_MODEL + c * cn)
        vx_ref[:, c * cn:(c + 1) * cn] = _bf(v * x1)


def _hy_in(x_ctx, x_lat, ctx_len, lat_len, mods, layer, w_in, sconv_w, sconv_b, casts=()):
    tm = lat_len
    assert tm % ctx_len == 0 and x_ctx.shape[0] % tm == 0
    ctx_steps = x_ctx.shape[0] // tm
    n = x_ctx.shape[0] + x_lat.shape[0]
    tok = pl.BlockSpec((tm, D_MODEL), lambda i: (i, 0))
    const = lambda shape: pl.BlockSpec(shape, lambda i: (0, 0), pipeline_mode=pl.Buffered(1))
    act = jax.ShapeDtypeStruct((n, D_MODEL), BF16)
    body = functools.partial(_hy_in_kernel, ctx_steps=ctx_steps, ctx_len=ctx_len, cn=HY_IN_COLS)
    return pl.pallas_call(
        _with_casts(body, 6, 2, casts),
        out_shape=(act, act) + tuple(c.out_shape for c in casts),
        grid=(n // tm,),
        in_specs=[
            pl.BlockSpec((tm, D_MODEL), lambda i: (_ctx_block(i, ctx_steps), 0)),
            pl.BlockSpec((tm, D_MODEL), lambda i: (_lat_block(i, ctx_steps), 0)),
            pl.BlockSpec((None, MOD_ROWS, 6 * D_MODEL), lambda i: (layer, 0, 0)),
            const((D_MODEL, 3 * D_MODEL)),
            pl.BlockSpec((3, 3 * D_MODEL), lambda i: (0, 0)),
            pl.BlockSpec((1, 3 * D_MODEL), lambda i: (0, 0)),
        ] + [c.in_spec for c in casts],
        out_specs=(tok, tok) + tuple(c.out_spec for c in casts),
        compiler_params=_params("arbitrary"),
        name="hyena_in",
    )(x_ctx, x_lat, mods, w_in, sconv_w, sconv_b, *[c.w for c in casts])


def _hy_conv_plain_kernel(vx_ref, x0_ref, coef_ref, bias_ref, fwd_ref, inv_ref, u_ref, *, L, nseq, cn):
    pieces = [(c, s) for c in range(vx_ref.shape[1] // cn) for s in range(nseq)]

    def forward(piece):
        c, s = piece
        vx = vx_ref[s * L:(s + 1) * L, c * cn:(c + 1) * cn]
        return vx, _dot(fwd_ref[...], vx)

    pending = [forward(p) for p in pieces[:HY_CONV_LOOKAHEAD]]
    for i, (c, s) in enumerate(pieces):
        if i + HY_CONV_LOOKAHEAD < len(pieces):
            pending.append(forward(pieces[i + HY_CONV_LOOKAHEAD]))
        vx, spec = pending.pop(0)
        rows = slice(s * L, (s + 1) * L)
        cols = slice(c * cn, (c + 1) * cn)
        k_cos = coef_ref[0, :, cols]
        k_sin = coef_ref[1, :, cols]
        k_mix = coef_ref[2, :, cols]
        v_cos = spec[0:L]
        v_sin = spec[L:2 * L]
        y_spec = _bf(jnp.concatenate([v_cos * k_cos - v_sin * k_sin, v_cos * k_sin + v_sin * k_mix], axis=0))
        y = _dot(inv_ref[...], y_spec)
        u = (y + bias_ref[:, cols] * vx.astype(F32)) * x0_ref[rows, cols].astype(F32)
        u_ref[rows, cols] = _bf(u)


def _hy_conv_split_kernel(vx_ref, x0_ref, coef_ref, bias_ref, fe_ref, fo_ref, ge_ref, go_ref, u_ref,
                          sv_ref, sx_ref, su_ref, *, L, nseq, cn):
    half = L // 2
    for g in range(vx_ref.shape[1] // LANES):
        lanes = slice(g * LANES, (g + 1) * LANES)
        sv_ref[g] = vx_ref[:, lanes].astype(F32)
        sx_ref[g] = x0_ref[:, lanes].astype(F32)

    def load(ref, start, c):
        return jnp.concatenate([ref[g, pl.ds(start, half, stride=2), :]
                                for g in range(c * cn // LANES, (c + 1) * cn // LANES)], axis=1)

    def store(ref, start, c, val):
        for k, g in enumerate(range(c * cn // LANES, (c + 1) * cn // LANES)):
            ref[g, pl.ds(start, half, stride=2), :] = val[:, k * LANES:(k + 1) * LANES]

    pieces = [(c, s) for c in range(vx_ref.shape[1] // cn) for s in range(nseq)]

    def forward(piece):
        c, s = piece
        v_e = load(sv_ref, s * L, c)
        v_o = load(sv_ref, s * L + 1, c)
        return v_e, v_o, _dot(fe_ref[...], _bf(v_e)), _dot(fo_ref[...], _bf(v_o))

    pending = [forward(p) for p in pieces[:HY_CONV_LOOKAHEAD]]
    for i, (c, s) in enumerate(pieces):
        if i + HY_CONV_LOOKAHEAD < len(pieces):
            pending.append(forward(pieces[i + HY_CONV_LOOKAHEAD]))
        v_e, v_o, a, b = pending.pop(0)
        cols = slice(c * cn, (c + 1) * cn)
        k_c, k_s, u_c, u_s, e2, e3, f2, f3 = [coef_ref[k, :, cols] for k in range(coef_ref.shape[0])]
        s_c, s_s = a[0:half] + b[0:half], a[half:L] + b[half:L]
        d_c, d_s = a[0:half] - b[0:half], a[half:L] - b[half:L]
        t1 = s_c * k_c - s_s * k_s
        t2 = d_c * u_c + d_s * u_s
        q1 = s_c * k_s
        q4 = d_c * u_s
        p_spec = jnp.concatenate([t1 + t2, q1 + s_s * e2 + d_s * e3 - q4], axis=0)
        m_spec = jnp.concatenate([t1 - t2, q1 + s_s * f2 - d_s * f3 + q4], axis=0)
        y_e = _dot(ge_ref[...], _bf(p_spec))
        y_o = _dot(go_ref[...], _bf(m_spec))
        bias = bias_ref[:, cols]
        store(su_ref, s * L, c, (y_e + bias * v_e) * load(sx_ref, s * L, c))
        store(su_ref, s * L + 1, c, (y_o + bias * v_o) * load(sx_ref, s * L + 1, c))
        rows = slice(s * L, (s + 1) * L)
        for g in range(c * cn // LANES, (c + 1) * cn // LANES):
            u_ref[rows, g * LANES:(g + 1) * LANES] = _bf(su_ref[g, rows, :])


def _hy_conv_steps(n):
    nt = n // HY_CONV_TOKENS
    return (D_MODEL // HY_CONV_CHANNELS) * nt, lambda j, i: j * nt + i


def _hy_conv(vx, x0, tok0, n, coef, bias, L, casts=()):
    tm = HY_CONV_TOKENS
    dt = HY_CONV_CHANNELS
    assert tok0 % tm == 0 and n % tm == 0 and tm % L == 0
    src = pl.BlockSpec((tm, dt), lambda j, i: (tok0 // tm + i, j))
    dst = pl.BlockSpec((tm, dt), lambda j, i: (i, j))
    const = lambda shape: pl.BlockSpec(shape, lambda j, i: (0, 0), pipeline_mode=pl.Buffered(1))
    if L >= HY_SPLIT_MIN_LEN:
        fwd, inv = _dft_tables(L)
        tables = (*fwd, *inv)
        body = functools.partial(_hy_conv_split_kernel, L=L, nseq=tm // L, cn=256)
        scratch = [pltpu.VMEM((dt // LANES, tm, LANES), F32)] * 3
    else:
        tables = _plain_dft_tables(L)[:2]
        body = functools.partial(_hy_conv_plain_kernel, L=L, nseq=tm // L, cn=256)
        scratch = []
    return pl.pallas_call(
        _with_casts(body, 4 + len(tables), 1, casts),
        out_shape=(jax.ShapeDtypeStruct((n, D_MODEL), BF16),) + tuple(c.out_shape for c in casts),
        grid=(D_MODEL // dt, n // tm),
        in_specs=[
            src, src,
            pl.BlockSpec(_filter_coef_shape(L) + (dt,), lambda j, i: (0, 0, j)),
            pl.BlockSpec((1, dt), lambda j, i: (0, j)),
        ] + [const(t.shape) for t in tables] + [c.in_spec for c in casts],
        out_specs=(dst,) + tuple(c.out_spec for c in casts),
        scratch_shapes=scratch,
        compiler_params=_params("arbitrary", "arbitrary"),
        name=f"hyena_conv_{L}",
    )(vx, x0, coef, bias, *[_bf(jnp.asarray(t)) for t in tables], *[c.w for c in casts])


def _post_kernel(*refs, layer, ctx_steps, lat_len, n_x, n_a, a_feature_major):
    x_refs, a_refs = refs[:n_x], refs[n_x:n_x + n_a]
    (mod_ref, wo_ref, ln1g_ref, ln1b_ref, win_ref, wout_ref, ln2g_ref, ln2b_ref, o_ref,
     act_ref) = refs[n_x + n_a:]
    d = D_MODEL
    i = pl.program_id(0)
    is_ctx = i < ctx_steps
    tm = act_ref.shape[0]
    mod = mod_ref[pl.ds(_mod_row(i, ctx_steps, tm, lat_len), 1), :]
    g1 = mod[:, 2 * d:3 * d]
    sh2 = mod[:, 3 * d:4 * d]
    sc2 = mod[:, 4 * d:5 * d]
    g2 = mod[:, 5 * d:6 * d]
    ln1 = (ln1g_ref[layer:layer + 1, :], ln1b_ref[layer:layer + 1, :])
    ln2 = (ln2g_ref[layer:layer + 1, :], ln2b_ref[layer:layer + 1, :])

    gr = tm // POST_ROW_GROUPS
    groups = [slice(r * gr, (r + 1) * gr) for r in range(POST_ROW_GROUPS)]
    if a_feature_major:
        mix = [_dot_tn(_pick(is_ctx, a_refs, (slice(None), rows)), wo_ref[...]) for rows in groups]
    else:
        mix = [_dot(_pick(is_ctx, a_refs, (rows, slice(None))), wo_ref[...]) for rows in groups]
    x1 = []
    for rows, mx in zip(groups, mix):
        x = _pick(is_ctx, x_refs, (rows, slice(None)))
        x1.append(_layer_norm(ALPHA * x + g1 * mx, *ln1))
        h = _bf(x1[-1] * (1.0 + sc2) + sh2)
        for c in range(D_FF // FFN_COLS):
            cols = slice(c * FFN_COLS, (c + 1) * FFN_COLS)
            gate = _dot(h, win_ref[:, cols])
            up = _dot(h, win_ref[:, D_FF + c * FFN_COLS:D_FF + (c + 1) * FFN_COLS])
            act_ref[rows, cols] = _bf(gate * jax.nn.sigmoid(gate) * up)
    for rows, xr in zip(groups, x1):
        f = _dot(act_ref[rows, :], wout_ref[...])
        o_ref[rows, :] = _layer_norm(ALPHA * xr + g2 * f, *ln2)


def _post(xs, mixes, n, n_ctx, lat_len, mods, layer, w_o, w_in, w_out, ln1_g, ln1_b, ln2_g, ln2_b, name,
          x_tok0=0, a_feature_major=False, casts=()):
    tm = POST_TOKENS
    ctx_steps = n_ctx // tm
    ctx = lambda i: _ctx_block(i, ctx_steps)
    lat = lambda i: _lat_block(i, ctx_steps)
    blocks = lambda m, tok0: (lambda i: tok0 // tm + i,) if m == 1 else (ctx, lat)
    tok = lambda f: pl.BlockSpec((tm, D_MODEL), lambda i: (f(i), 0))
    feat = lambda f: pl.BlockSpec((D_MODEL, tm), lambda i: (0, f(i)))
    const = lambda shape: pl.BlockSpec(shape, lambda i: (0, 0), pipeline_mode=pl.Buffered(1))
    vec = pl.BlockSpec((DEPTH, D_MODEL), lambda i: (0, 0))
    body = functools.partial(_post_kernel, layer=layer, ctx_steps=ctx_steps, lat_len=lat_len, n_x=len(xs),
                             n_a=len(mixes), a_feature_major=a_feature_major)
    return pl.pallas_call(
        _with_casts(body, len(xs) + len(mixes) + 8, 1, casts),
        out_shape=(jax.ShapeDtypeStruct((n, D_MODEL), F32),) + tuple(c.out_shape for c in casts),
        grid=(n // tm,),
        in_specs=[tok(f) for f in blocks(len(xs), x_tok0)]
        + [(feat if a_feature_major else tok)(f) for f in blocks(len(mixes), 0)]
        + [pl.BlockSpec((None, MOD_ROWS, 6 * D_MODEL), lambda i: (layer, 0, 0)),
           const((D_MODEL, D_MODEL)), vec, vec,
           const((D_MODEL, 2 * D_FF)), const((D_FF, D_MODEL)), vec, vec]
        + [c.in_spec for c in casts],
        out_specs=(tok(lambda i: i),) + tuple(c.out_spec for c in casts),
        scratch_shapes=[pltpu.VMEM((tm, D_FF), BF16)],
        compiler_params=_params("arbitrary"),
        name=name,
    )(*xs, *mixes, mods, w_o, ln1_g, ln1_b, w_in, w_out, ln2_g, ln2_b, *[c.w for c in casts])


def _qkv_kernel(x_ref, mod_ref, wt_ref, qkvt_ref, nk_ref, nv_ref, *, ctx_steps, lat_len):
    d = D_MODEL
    i = pl.program_id(0)
    tm = x_ref.shape[0]
    mod = mod_ref[pl.ds(_mod_row(i, ctx_steps, tm, lat_len), 1), :]
    h = _bf(x_ref[...] * (1.0 + mod[:, d:2 * d]) + mod[:, 0:d])
    qkvt_ref[0:d, :] = _bf(_dot_nt(wt_ref[0:d, :], h) * (HEAD_DIM ** -0.5 * LOG2E))
    kt = _dot_nt(wt_ref[d:2 * d, :], h)
    qkvt_ref[d:2 * d, :] = _bf(kt)
    vt = _dot_nt(wt_ref[2 * d:3 * d, :], h)
    qkvt_ref[2 * d:3 * d, :] = _bf(vt)

    @pl.when(i < ctx_steps)
    def _():
        seq = nk_ref.shape[4]
        for s in range(tm // seq):
            nk_ref[s, 0] = kt[:, s * seq:(s + 1) * seq].reshape(N_HEADS, HEAD_DIM, seq)
            nv_ref[s, 0] = vt[:, s * seq:(s + 1) * seq].reshape(N_HEADS, HEAD_DIM, seq)


def _qkv(x2d, n_ctx, ctx_len, lat_len, mods, layer, w_qkv_t):
    n = x2d.shape[0]
    tm = QKV_TOKENS
    assert tm % ctx_len == 0 and n_ctx % tm == 0
    ctx_steps = n_ctx // tm
    kv_shape = jax.ShapeDtypeStruct((n_ctx // ctx_len, 1, N_HEADS, HEAD_DIM, ctx_len), F32)
    kv_spec = pl.BlockSpec((tm // ctx_len, 1, N_HEADS, HEAD_DIM, ctx_len),
                           lambda i: (_ctx_block(i, ctx_steps), 0, 0, 0, 0))
    return pl.pallas_call(
        functools.partial(_qkv_kernel, ctx_steps=ctx_steps, lat_len=lat_len),
        out_shape=(jax.ShapeDtypeStruct((3 * D_MODEL, n), BF16), kv_shape, kv_shape),
        grid=(n // tm,),
        in_specs=[
            pl.BlockSpec((tm, D_MODEL), lambda i: (i, 0)),
            pl.BlockSpec((None, MOD_ROWS, 6 * D_MODEL), lambda i: (layer, 0, 0)),
            pl.BlockSpec((3 * D_MODEL, D_MODEL), lambda i: (0, 0), pipeline_mode=pl.Buffered(1)),
        ],
        out_specs=(pl.BlockSpec((3 * D_MODEL, tm), lambda i: (0, i)), kv_spec, kv_spec),
        compiler_params=_params("arbitrary"),
        name="qkv",
    )(x2d, mods, w_qkv_t)


def _softmax_weights(*scores):
    m = functools.reduce(jnp.maximum, [jnp.max(s, axis=0, keepdims=True) for s in scores])
    return [_bf(jnp.exp2(s - m)) for s in scores]


def _weighted_values(vt, p):
    ones = jnp.ones((BF16_ROWS, vt.shape[1]), BF16)
    return _dot(jnp.concatenate([vt, ones], axis=0), p)


def _normalise(acc):
    return _bf(acc[0:HEAD_DIM] * (1.0 / acc[HEAD_DIM:HEAD_DIM + 1]))


def _ctx_attn_kernel(qt_ref, kt_ref, vt_ref, ot_ref, *, seq):
    units = [(s, hh) for s in range(qt_ref.shape[1] // seq) for hh in range(N_HEADS)]

    def block(unit):
        s, hh = unit
        return slice(hh * HEAD_DIM, (hh + 1) * HEAD_DIM), slice(s * seq, (s + 1) * seq)

    def scores(unit):
        return _dot_tn(kt_ref[block(unit)], qt_ref[block(unit)])

    pending = [scores(u) for u in units[:CTX_ATTN_LOOKAHEAD]]
    for i, unit in enumerate(units):
        if i + CTX_ATTN_LOOKAHEAD < len(units):
            pending.append(scores(units[i + CTX_ATTN_LOOKAHEAD]))
        (p,) = _softmax_weights(pending.pop(0))
        ot_ref[block(unit)] = _normalise(_weighted_values(vt_ref[block(unit)], p))


def _ctx_attn(qkvt, n_ctx, seq):
    tm = CTX_ATTN_SEQS * seq
    return pl.pallas_call(
        functools.partial(_ctx_attn_kernel, seq=seq),
        out_shape=jax.ShapeDtypeStruct((D_MODEL, n_ctx), BF16),
        grid=(n_ctx // tm,),
        in_specs=[pl.BlockSpec((D_MODEL, tm), lambda b, c=c: (c, b)) for c in range(3)],
        out_specs=pl.BlockSpec((D_MODEL, tm), lambda b: (0, b)),
        compiler_params=_params("arbitrary"),
        name="ctx_attn",
    )(qkvt, qkvt, qkvt)


LAT_ROWS = 16
Q_BLOCK_ROWS = 4
KEY_TILE_ROWS = 4


def _key_band(qb):
    row0 = [min(max(r - WIN_H // 2, 0), LAT_ROWS - WIN_H) for r in range(qb * Q_BLOCK_ROWS, (qb + 1) * Q_BLOCK_ROWS)]
    lo = min(row0) // KEY_TILE_ROWS * KEY_TILE_ROWS
    hi = -(-(max(row0) + WIN_H) // KEY_TILE_ROWS) * KEY_TILE_ROWS
    return lo, hi


KEY_BANDS = tuple(_key_band(qb) for qb in range(LAT_ROWS // Q_BLOCK_ROWS))
MAX_BAND_ROWS = max(hi - lo for lo, hi in KEY_BANDS)


def _fill_bias_table(rpb_ref, tab_ref):
    shape = (GRID_W, LANES)
    kc = lax.broadcasted_iota(jnp.int32, shape, 0)
    lane = lax.broadcasted_iota(jnp.int32, shape, 1)
    qc = lane % GRID_W
    win0 = jnp.clip(qc - WIN_W // 2, 0, GRID_W - WIN_W)
    col_ok = (kc >= win0) & (kc < win0 + WIN_W)
    masked = jnp.full(shape, MASK_VALUE, F32)
    tiles = []
    for dr in range(2 * WIN_H - 1):
        row = jnp.broadcast_to(rpb_ref[dr:dr + 1, :] * LOG2E, shape)
        t = pltpu.roll(row, LANES - (WIN_W - 1), 1, stride=1, stride_axis=0)
        t = jnp.where(lane < GRID_W, t, pltpu.roll(t, GRID_W, 1))
        tiles.append(jnp.where(col_ok, t, MASK_VALUE))

    def tile(qr, kr):
        row0 = min(max(qr - WIN_H // 2, 0), LAT_ROWS - WIN_H)
        if row0 <= kr < row0 + WIN_H:
            return tiles[kr - qr + WIN_H - 1]
        return masked

    for qr in range(0, LAT_ROWS, 2):
        lo, hi = KEY_BANDS[qr // Q_BLOCK_ROWS]
        for i in range(hi - lo):
            both = jnp.where(lane < GRID_W, tile(qr, lo + i), tile(qr + 1, lo + i))
            tab_ref[i * GRID_W:(i + 1) * GRID_W, qr * GRID_W:(qr + 2) * GRID_W] = both


def _lat_attn_kernel(qt_ref, kt_ref, vt_ref, ckt_ref, cvt_ref, rpb_ref, ot_ref, tab_ref):
    seq = LAT_ROWS * GRID_W
    nq = Q_BLOCK_ROWS * GRID_W
    heads = qt_ref.shape[0] // HEAD_DIM
    units = [(b, hh, qb) for b in range(qt_ref.shape[1] // seq) for hh in range(heads)
             for qb in range(len(KEY_BANDS))]

    def rows(hh):
        return slice(hh * HEAD_DIM, (hh + 1) * HEAD_DIM)

    def keys(b, qb):
        lo, hi = KEY_BANDS[qb]
        return slice(b * seq + lo * GRID_W, b * seq + hi * GRID_W)

    def queries(b, qb):
        return slice(b * seq + qb * nq, b * seq + (qb + 1) * nq)

    def scores(unit):
        b, hh, qb = unit
        lo, hi = KEY_BANDS[qb]
        qt = qt_ref[rows(hh), queries(b, qb)]
        s_loc = (_dot_tn(kt_ref[rows(hh), keys(b, qb)], qt)
                 + tab_ref[hh, 0:(hi - lo) * GRID_W, qb * nq:(qb + 1) * nq])
        s_ctx = _dot_tn(_bf(ckt_ref[b, hh]), qt)
        return s_loc, s_ctx

    for hh in range(tab_ref.shape[0]):
        _fill_bias_table(rpb_ref.at[hh], tab_ref.at[hh])
    pending = [scores(u) for u in units[:LAT_ATTN_LOOKAHEAD]]
    for i, (b, hh, qb) in enumerate(units):
        if i + LAT_ATTN_LOOKAHEAD < len(units):
            pending.append(scores(units[i + LAT_ATTN_LOOKAHEAD]))
        s_loc, s_ctx = pending.pop(0)
        p_loc, p_ctx = _softmax_weights(s_loc, s_ctx)
        acc = (_weighted_values(vt_ref[rows(hh), keys(b, qb)], p_loc)
               + _weighted_values(_bf(cvt_ref[b, hh]), p_ctx))
        ot_ref[rows(hh), queries(b, qb)] = _normalise(acc)


def _lat_attn(qkvt, n_ctx, cache_kt, cache_vt, rpb, layer_idx):
    rpb = jnp.pad(rpb[:, :, ::-1], ((0, 0), (0, 1), (0, LANES - (2 * WIN_W - 1))))
    n = qkvt.shape[1] - n_ctx
    seq = LAT_ROWS * GRID_W
    assert n_ctx % n == 0 and n % seq == 0
    nb = n // seq
    nhp = N_HEADS // 2
    past = cache_kt.shape[4]
    feat = lambda c: pl.BlockSpec((2 * HEAD_DIM, n), lambda hp, c=c: (c * nhp + hp, n_ctx // n))
    cache = pl.BlockSpec((nb, None, 2, HEAD_DIM, past), lambda hp: (0, layer_idx, hp, 0, 0))
    return pl.pallas_call(
        _lat_attn_kernel,
        out_shape=jax.ShapeDtypeStruct((D_MODEL, n), BF16),
        grid=(nhp,),
        in_specs=[feat(0), feat(1), feat(2), cache, cache,
                  pl.BlockSpec((2, 2 * WIN_H, LANES), lambda hp: (hp, 0, 0))],
        out_specs=pl.BlockSpec((2 * HEAD_DIM, n), lambda hp: (hp, 0)),
        scratch_shapes=[pltpu.VMEM((2, MAX_BAND_ROWS * GRID_W, seq), F32)],
        compiler_params=_params("arbitrary"),
        name="lat_attn",
    )(qkvt, qkvt, qkvt, cache_kt, cache_vt, rpb)


def kernel(x_prompt, x_sample, cache_k, cache_v, c, c_ctx, ada_w, ada_b, ln1_g, ln1_b, ln2_g, ln2_b, ffn_w_in, ffn_w_out, hy_w_in, hy_sconv_w, hy_sconv_b, hy_pe_w1, hy_pe_b1, hy_pe_w2, hy_pe_b2, hy_pe_w3, hy_sin_freq, hy_decay, hy_bias, hy_w_out, na_w_qkv, na_rpb, na_w_out):
    nbp, lp, d = x_prompt.shape
    nbs, ls, _ = x_sample.shape
    assert d == D_MODEL and ls == LAT_ROWS * GRID_W and c.shape[0] == nbs
    xp = x_prompt.reshape(nbp * lp, d)
    xs = x_sample.reshape(nbs * ls, d)
    n_ctx, n_lat = xp.shape[0], xs.shape[0]
    n = n_ctx + n_lat
    ln = (ln1_g, ln1_b, ln2_g, ln2_b)

    filt = (hy_pe_w1[0], hy_pe_b1[0], hy_pe_w2[0], hy_pe_b2[0], hy_pe_w3[0], hy_sin_freq[0], hy_decay[0])
    mods, coef_p, coef_s, hy_w_in0 = _prep(c_ctx, c, ada_w, ada_b, (lp, ls), *filt,
                                           casts=[_Cast(hy_w_in, 0, DEPTH * D_MODEL // FILTER_CHANNELS, _step)])

    sconv_b = hy_sconv_b[0].reshape(1, -1)
    hy_bias0 = hy_bias[0].reshape(1, -1)
    x0, vx, ffn_w_in0 = _hy_in(xp, xs, lp, ls, mods, 0, hy_w_in0, hy_sconv_w[0], sconv_b,
                               casts=[_Cast(ffn_w_in, 0, n // ls, _step)])
    steps, lin = _hy_conv_steps(n_ctx)
    u_p, hy_w_out0 = _hy_conv(vx, x0, 0, n_ctx, coef_p, hy_bias0, lp, casts=[_Cast(hy_w_out, 0, steps, lin)])
    steps, lin = _hy_conv_steps(n_lat)
    u_s, w_qkv_t, ffn_w_out0 = _hy_conv(vx, x0, n_ctx, n_lat, coef_s, hy_bias0, ls,
                                        casts=[_Cast(na_w_qkv, 0, steps, lin, transpose=True),
                                               _Cast(ffn_w_out, 0, steps, lin)])
    steps = n // POST_TOKENS
    x1, ffn_w_in1, ffn_w_out1, na_w_out0 = _post(
        (xp, xs), (u_p, u_s), n, n_ctx, ls, mods, 0, hy_w_out0, ffn_w_in0, ffn_w_out0, *ln, name="post0",
        casts=[_Cast(ffn_w_in, 1, steps, _step), _Cast(ffn_w_out, 1, steps, _step),
               _Cast(na_w_out, 0, steps, _step)])

    qkvt, new_kt, new_vt = _qkv(x1, n_ctx, lp, ls, mods, 1, w_qkv_t)
    at_p = _ctx_attn(qkvt, n_ctx, lp)
    at_s = _lat_attn(qkvt, n_ctx, jnp.swapaxes(cache_k, 3, 4), jnp.swapaxes(cache_v, 3, 4), na_rpb[0], 0)
    ffn1 = (na_w_out0, ffn_w_in1, ffn_w_out1, *ln)
    (yp,) = _post((x1,), (at_p,), n_ctx, n_ctx, ls, mods, 1, *ffn1, name="post1_ctx", a_feature_major=True)
    (ys,) = _post((x1,), (at_s,), n_lat, 0, ls, mods, 1, *ffn1, name="post1_lat", x_tok0=n_ctx,
                  a_feature_major=True)

    return (yp.reshape(nbp, lp, d), ys.reshape(nbs, ls, d),
            jnp.swapaxes(new_kt, 3, 4), jnp.swapaxes(new_vt, 3, 4))
```

```python
import functools
import math

import jax
import jax.numpy as jnp
import numpy as np
from jax import lax
from jax.experimental import pallas as pl
from jax.experimental.pallas import tpu as pltpu

D_MODEL = 1024
N_HEADS = 16
HEAD_DIM = D_MODEL // N_HEADS
D_FF = 2816
GRID_W = 64
WIN_H = 8
WIN_W = 16
N_BANDS = 16
PE_DIM = 1 + 2 * N_BANDS
FILT_W = 64
MOD_SHIFT = 0.05
DEPTH = 2
ALPHA = (2 * DEPTH) ** 0.25
LN_EPS = 1e-5
MASK_VALUE = -1e30
MOD_ROWS = 8

LANES = 128
SUBLANES = 8
BF16_ROWS = 2 * SUBLANES
CTX_ATTN_LOOKAHEAD = 6
CTX_ATTN_SEQS = 4
LAT_ATTN_LOOKAHEAD = 2
LOG2E = math.log2(math.e)
VMEM_LIMIT = 56 * 1024 * 1024

ADA_COLS = 1536
FILTER_CHANNELS = 256
HY_IN_COLS = 256
HY_CONV_TOKENS = 1024
HY_CONV_CHANNELS = 512
HY_CONV_LOOKAHEAD = 2
POST_TOKENS = 512
POST_ROW_GROUPS = 2
FFN_COLS = 256
QKV_TOKENS = 1024

F32 = jnp.float32
BF16 = jnp.bfloat16


def _bf(x):
    return x.astype(BF16)


def _dot(a, b):
    return jnp.dot(a, b, preferred_element_type=F32)


def _dot_nt(a, b):
    return lax.dot_general(a, b, (((1,), (1,)), ((), ())), preferred_element_type=F32)


def _dot_tn(a, b):
    return lax.dot_general(a, b, (((0,), (0,)), ((), ())), preferred_element_type=F32)


def _dot3(a, b):
    a_hi = _bf(a)
    a_lo = _bf(a - a_hi.astype(F32))
    b_hi = _bf(b)
    b_lo = _bf(b - b_hi.astype(F32))
    m = a.shape[0]
    both = _dot(jnp.concatenate([a_hi, a_lo], axis=0), b_hi)
    return both[0:m] + both[m:2 * m] + _dot(a_hi, b_lo)


def _layer_norm(x, g, b):
    mu = jnp.mean(x, axis=-1, keepdims=True)
    xc = x - mu
    var = jnp.mean(xc * xc, axis=-1, keepdims=True)
    return xc * lax.rsqrt(var + LN_EPS) * g + b


def _params(*sem):
    return pltpu.CompilerParams(dimension_semantics=sem, vmem_limit_bytes=VMEM_LIMIT)


class _Cast:
    def __init__(self, w, layer, steps, step_fn, transpose=False):
        _, r, c = w.shape
        assert r % (steps * BF16_ROWS) == 0
        rows = r // steps
        self.w = w
        self.transpose = transpose
        self.in_spec = pl.BlockSpec((None, rows, c), lambda *g: (layer, step_fn(*g), 0))
        if transpose:
            self.out_shape = jax.ShapeDtypeStruct((c, r), BF16)
            self.out_spec = pl.BlockSpec((c, rows), lambda *g: (0, step_fn(*g)))
        else:
            self.out_shape = jax.ShapeDtypeStruct((r, c), BF16)
            self.out_spec = pl.BlockSpec((rows, c), lambda *g: (step_fn(*g), 0))


def _with_casts(body, n_in, n_out, casts):
    nc = len(casts)

    def kernel_fn(*refs):
        ins, src = refs[:n_in], refs[n_in:n_in + nc]
        outs, dst = refs[n_in + nc:n_in + nc + n_out], refs[n_in + nc + n_out:n_in + 2 * nc + n_out]
        for cast, s, d in zip(casts, src, dst):
            d[...] = _bf(s[...].T if cast.transpose else s[...])
        body(*ins, *outs, *refs[n_in + 2 * nc + n_out:])

    return kernel_fn


def _step(i):
    return i


def _ctx_block(i, ctx_steps):
    return jnp.minimum(i, ctx_steps - 1)


def _lat_block(i, ctx_steps):
    return jnp.maximum(i - ctx_steps, 0)


def _mod_row(i, ctx_steps, tm, lat_len):
    return jnp.where(i < ctx_steps, 0, 1 + ((i - ctx_steps) * tm) // lat_len)


def _pick(is_ctx, refs, idx):
    if len(refs) == 1:
        return refs[0][idx]
    return jnp.where(is_ctx, refs[0][idx], refs[1][idx])


@functools.lru_cache(maxsize=None)
def _plain_dft_tables(L):
    n = 2 * L
    f = np.arange(L)[:, None]
    s = np.arange(L)[None, :]
    ang = 2.0 * np.pi * ((f * s) % n) / n
    c = np.cos(ang)
    sn = np.sin(ang)
    sn[0, :] = np.where(np.arange(L) % 2 == 0, 1.0, -1.0)
    fwd = np.concatenate([c, sn], axis=0)
    w = np.full((2 * L, 1), 2.0 / n)
    w[0, 0] = 1.0 / n
    w[L, 0] = 1.0 / n
    inv = (fwd * w).T
    alt = np.where(np.arange(L) % 2 == 0, 1.0, -1.0).astype(np.float32)[:, None]
    return (np.ascontiguousarray(fwd).astype(np.float32), np.ascontiguousarray(inv).astype(np.float32), alt)


@functools.lru_cache(maxsize=None)
def _dft_tables(L):
    n = 2 * L
    half = L // 2
    f = np.arange(half)[:, None]
    m = np.arange(half)[None, :]
    alt = np.where(np.arange(half) % 2 == 0, 1.0, -1.0)
    w = np.full((L, 1), 2.0 / n)
    w[0, 0] = 1.0 / n
    w[half, 0] = 1.0 / n
    fwd, inv = [], []
    for parity in range(2):
        ang = 2.0 * np.pi * ((f * (2 * m + parity)) % n) / n
        sn = np.sin(ang)
        sn[0, :] = alt
        table = np.concatenate([np.cos(ang), sn], axis=0)
        fwd.append(np.ascontiguousarray(table).astype(np.float32))
        inv.append(np.ascontiguousarray((table * w).T).astype(np.float32))
    return fwd, inv


@functools.lru_cache(maxsize=None)
def _filter_dft_tables(L):
    n = 2 * L
    f = np.arange(L // 2)[:, None]
    t = np.arange(L)[None, :]
    ang = 2.0 * np.pi * ((f * t) % n) / n
    table = np.concatenate([np.cos(ang), np.sin(ang)], axis=0).astype(np.float32)
    tt = np.arange(L)
    cols = np.stack([np.where(tt % 2 == 0, 1.0, -1.0), np.cos(np.pi * (tt % 4) / 2.0),
                     np.sin(np.pi * (tt % 4) / 2.0)], axis=1)
    return np.ascontiguousarray(table), np.round(cols).astype(np.float32)


@functools.lru_cache(maxsize=None)
def _filter_features(L):
    t = np.linspace(0.0, 1.0, L, dtype=np.float32)[:, None]
    w = (np.float32(2.0 * math.pi / L) * np.arange(L, dtype=np.float32))[:, None]
    bands = np.linspace(1e-4, N_BANDS - 1, N_BANDS, dtype=np.float32)[None, :]
    arg = (bands * w).astype(np.float64)
    feats = np.concatenate([t.astype(np.float64), np.cos(arg), np.sin(arg)], axis=-1)
    out = np.zeros((L, LANES), np.float32)
    out[:, :PE_DIM] = feats
    return out, t


HY_SPLIT_MIN_LEN = 512


def _filter_coef_shape(L):
    return (8, L // 2) if L >= HY_SPLIT_MIN_LEN else (3, L)


def _ada_tile(layer, cctx_ref, c_ref, w_ref, b_ref, o_ref, cvec_ref):
    nb = c_ref.shape[0]
    cvec_ref[...] = jnp.zeros_like(cvec_ref)
    cvec_ref[0:1, :] = cctx_ref[...]
    cvec_ref[1:1 + nb, :] = c_ref[...]
    c = cvec_ref[...]
    o_ref[...] = _dot3(c * jax.nn.sigmoid(c), w_ref[...]) + b_ref[pl.ds(layer, 1), :]


def _filter_tile(is_first, feats_ref, t_ref, cols_ref, tab_ref, w1_ref, b1_ref, w2_ref, b2_ref, freq_ref,
                 w3f_ref, w3b_ref, decf_ref, decb_ref, out_ref, h_ref, w1p_ref, *, L):
    hi = lax.Precision.HIGHEST

    @pl.when(is_first)
    def _():
        w1p_ref[...] = jnp.zeros_like(w1p_ref)
        w1p_ref[0:PE_DIM, :] = w1_ref[...]
        h1 = jnp.sin(freq_ref[0:1, :] * (jnp.dot(feats_ref[...], w1p_ref[...], precision=hi,
                                                 preferred_element_type=F32) + b1_ref[...]))
        h_ref[...] = jnp.sin(freq_ref[1:2, :] * (jnp.dot(h1, w2_ref[...], precision=hi,
                                                         preferred_element_type=F32) + b2_ref[...]))

    h = h_ref[...]
    t = t_ref[...]
    kf = _dot3(h, w3f_ref[...]) * (jnp.exp(-t * jnp.abs(decf_ref[...])) + MOD_SHIFT)
    kb = _dot3(h, w3b_ref[...]) * (jnp.exp(-t * jnp.abs(decb_ref[...])) + MOD_SHIFT)
    row = lax.broadcasted_iota(jnp.int32, kf.shape, 0)
    kb = jnp.where(row == 0, 0.0, kb)
    ksum = kf + kb
    kdiff = kf - kb
    alt = cols_ref[:, 0:1]

    def spectrum(table, x):
        hi = _bf(x)
        return _dot(table, hi) + _dot(table, _bf(x - hi.astype(F32)))

    if L < HY_SPLIT_MIN_LEN:
        k_c = spectrum(tab_ref[0:L, :], ksum)
        k_s = spectrum(tab_ref[L:2 * L, :], kdiff)
        k_nyq = jnp.sum(ksum * alt, axis=0, keepdims=True)
        out_ref[0] = k_c
        out_ref[1] = jnp.where(row == 0, 0.0, k_s)
        out_ref[2] = jnp.where(row == 0, k_nyq, k_c)
        return

    half = L // 2
    cos_m = tab_ref[0:half, :]
    sin_m = tab_ref[half:L, :]
    k_c = spectrum(cos_m, ksum)
    k_s = spectrum(sin_m, kdiff)
    u_c = spectrum(cos_m, alt * ksum)
    u_s = -spectrum(sin_m, alt * kdiff)
    kc_h = jnp.sum(ksum * cols_ref[:, 1:2], axis=0, keepdims=True)
    ks_h = jnp.sum(kdiff * cols_ref[:, 2:3], axis=0, keepdims=True)
    first = lax.broadcasted_iota(jnp.int32, k_c.shape, 0) == 0
    out_ref[0] = k_c
    out_ref[1] = k_s
    out_ref[2] = u_c
    out_ref[3] = u_s
    out_ref[4] = jnp.where(first, kc_h - ks_h, k_c)
    out_ref[5] = jnp.where(first, kc_h + ks_h, u_c)
    out_ref[6] = jnp.where(first, kc_h + ks_h, k_c)
    out_ref[7] = jnp.where(first, kc_h - ks_h, u_c)


def _prep_kernel(cctx_ref, c_ref, w_ref, b_ref, fa_ref, ta_ref, ca_ref, tba_ref, fb_ref, tb_ref, cb_ref,
                 tbb_ref, w1_ref, b1_ref, w2_ref, b2_ref, freq_ref, w3f_ref, w3b_ref, decf_ref, decb_ref,
                 mod_ref, coefa_ref, coefb_ref, cvec_ref, ha_ref, hb_ref, w1p_ref, *, lens, tiles):
    s = pl.program_id(0)
    _ada_tile(s // tiles, cctx_ref, c_ref, w_ref, b_ref, mod_ref, cvec_ref)
    shared = (w1_ref, b1_ref, w2_ref, b2_ref, freq_ref, w3f_ref, w3b_ref, decf_ref, decb_ref)

    @pl.when(s < tiles)
    def _():
        _filter_tile(s == 0, fa_ref, ta_ref, ca_ref, tba_ref, *shared, coefa_ref, ha_ref, w1p_ref, L=lens[0])

    @pl.when(s >= tiles)
    def _():
        _filter_tile(s == tiles, fb_ref, tb_ref, cb_ref, tbb_ref, *shared, coefb_ref, hb_ref, w1p_ref, L=lens[1])


def _prep(c_ctx, c, ada_w, ada_b, lens, pe_w1, pe_b1, pe_w2, pe_b2, pe_w3, sin_freq, decay, casts=()):
    assert 1 + c.shape[0] <= MOD_ROWS and DEPTH == 2
    tn = ADA_COLS
    dt = FILTER_CHANNELS
    tiles = D_MODEL // dt
    assert 6 * D_MODEL // tn == tiles
    full = lambda shape: pl.BlockSpec(shape, lambda s: (0,) * len(shape))
    consts, const_specs, out_shapes, out_specs, scratch = [], [], [], [], []
    for k, L in enumerate(lens):
        feats, t = _filter_features(L)
        if L >= HY_SPLIT_MIN_LEN:
            table, cols = _filter_dft_tables(L)
        else:
            table, _, cols = _plain_dft_tables(L)
        consts += [feats, t, cols, _bf(jnp.asarray(table))]
        const_specs += [full(feats.shape), full(t.shape), full(cols.shape), full(table.shape)]
        nc, nf = _filter_coef_shape(L)
        out_shapes.append(jax.ShapeDtypeStruct((nc, nf, D_MODEL), F32))
        block = (lambda s: (0, 0, jnp.minimum(s, tiles - 1))) if k == 0 else (
            lambda s: (0, 0, jnp.maximum(s - tiles, 0)))
        out_specs.append(pl.BlockSpec((nc, nf, dt), block))
        scratch.append(pltpu.VMEM((L, FILT_W), F32))
    b1 = pe_b1.reshape(1, FILT_W)
    b2 = pe_b2.reshape(1, FILT_W)
    dec = decay.reshape(1, 2 * D_MODEL)
    scratch.append(pltpu.VMEM((LANES, FILT_W), F32))
    return pl.pallas_call(
        _with_casts(functools.partial(_prep_kernel, lens=tuple(lens), tiles=tiles), 21, 3, casts),
        out_shape=(jax.ShapeDtypeStruct((DEPTH, MOD_ROWS, 6 * D_MODEL), F32), *out_shapes,
                   *[k.out_shape for k in casts]),
        grid=(DEPTH * tiles,),
        in_specs=[
            full((1, D_MODEL)), full(c.shape),
            pl.BlockSpec((None, D_MODEL, tn), lambda s: (s // tiles, 0, s % tiles)),
            pl.BlockSpec((DEPTH, tn), lambda s: (0, s % tiles)),
            *const_specs,
            full((PE_DIM, FILT_W)), full((1, FILT_W)), full((FILT_W, FILT_W)), full((1, FILT_W)),
            full((2, FILT_W)),
            pl.BlockSpec((FILT_W, dt), lambda s: (0, s % tiles)),
            pl.BlockSpec((FILT_W, dt), lambda s: (0, tiles + s % tiles)),
            pl.BlockSpec((1, dt), lambda s: (0, s % tiles)),
            pl.BlockSpec((1, dt), lambda s: (0, tiles + s % tiles)),
            *[k.in_spec for k in casts],
        ],
        out_specs=(pl.BlockSpec((None, MOD_ROWS, tn), lambda s: (s // tiles, 0, s % tiles)), *out_specs,
                   *[k.out_spec for k in casts]),
        scratch_shapes=[pltpu.VMEM((MOD_ROWS, D_MODEL), F32), *scratch],
        compiler_params=_params("arbitrary"),
        name="prep",
    )(c_ctx.reshape(1, D_MODEL), c, ada_w, ada_b, *consts, pe_w1, b1, pe_w2, b2, sin_freq, pe_w3, pe_w3, dec, dec,
      *[k.w for k in casts])


def _hy_in_kernel(xc_ref, xl_ref, mod_ref, w_ref, cw_ref, cb_ref, x0_ref, vx_ref, *, ctx_steps, ctx_len, cn):
    i = pl.program_id(0)
    is_ctx = i < ctx_steps
    tm = x0_ref.shape[0]
    mod = mod_ref[pl.ds(_mod_row(i, ctx_steps, tm, tm), 1), :]
    x = _pick(is_ctx, (xc_ref, xl_ref), (slice(None), slice(None)))
    h = _bf(x * (1.0 + mod[:, D_MODEL:2 * D_MODEL]) + mod[:, 0:D_MODEL])
    row = lax.broadcasted_iota(jnp.int32, (SUBLANES, cn), 0)

    def conv(col):
        z = _dot(h, w_ref[:, col:col + cn])
        zm = pltpu.roll(z, 1, 0)
        zp = pltpu.roll(z, tm - 1, 0)
        w0 = cw_ref[0:1, col:col + cn]
        w1 = cw_ref[1:2, col:col + cn]
        w2 = cw_ref[2:3, col:col + cn]
        b = cb_ref[:, col:col + cn]
        out = zm * w0 + z * w1 + zp * w2 + b
        pieces = []
        for s in range(tm // ctx_len):
            lo = s * ctx_len
            hi = lo + ctx_len - SUBLANES
            head = (jnp.where(row == 0, 0.0, zm[lo:lo + SUBLANES]) * w0 + z[lo:lo + SUBLANES] * w1
                    + zp[lo:lo + SUBLANES] * w2 + b)
            tail = (zm[hi:hi + SUBLANES] * w0 + z[hi:hi + SUBLANES] * w1
                    + jnp.where(row == SUBLANES - 1, 0.0, zp[hi:hi + SUBLANES]) * w2 + b)
            if lo > 0:
                head = jnp.where(is_ctx, head, out[lo:lo + SUBLANES])
            if hi + SUBLANES < tm:
                tail = jnp.where(is_ctx, tail, out[hi:hi + SUBLANES])
            pieces += [head, out[lo + SUBLANES:hi], tail]
        return jnp.concatenate(pieces, axis=0)

    for c in range(D_MODEL // cn):
        x0_ref[:, c * cn:(c + 1) * cn] = _bf(conv(c * cn))
        x1 = conv(D_MODEL + c * cn)
        v = conv(2 * D_MODEL + c * cn)
        vx_ref[:, c * cn:(c + 1) * cn] = _bf(v * x1)


def _hy_in(x_ctx, x_lat, ctx_len, lat_len, mods, layer, w_in, sconv_w, sconv_b, casts=()):
    tm = lat_len
    assert tm % ctx_len == 0 and x_ctx.shape[0] % tm == 0
    ctx_steps = x_ctx.shape[0] // tm
    n = x_ctx.shape[0] + x_lat.shape[0]
    tok = pl.BlockSpec((tm, D_MODEL), lambda i: (i, 0))
    const = lambda shape: pl.BlockSpec(shape, lambda i: (0, 0), pipeline_mode=pl.Buffered(1))
    act = jax.ShapeDtypeStruct((n, D_MODEL), BF16)
    body = functools.partial(_hy_in_kernel, ctx_steps=ctx_steps, ctx_len=ctx_len, cn=HY_IN_COLS)
    return pl.pallas_call(
        _with_casts(body, 6, 2, casts),
        out_shape=(act, act) + tuple(c.out_shape for c in casts),
        grid=(n // tm,),
        in_specs=[
            pl.BlockSpec((tm, D_MODEL), lambda i: (_ctx_block(i, ctx_steps), 0)),
            pl.BlockSpec((tm, D_MODEL), lambda i: (_lat_block(i, ctx_steps), 0)),
            pl.BlockSpec((None, MOD_ROWS, 6 * D_MODEL), lambda i: (layer, 0, 0)),
            const((D_MODEL, 3 * D_MODEL)),
            pl.BlockSpec((3, 3 * D_MODEL), lambda i: (0, 0)),
            pl.BlockSpec((1, 3 * D_MODEL), lambda i: (0, 0)),
        ] + [c.in_spec for c in casts],
        out_specs=(tok, tok) + tuple(c.out_spec for c in casts),
        compiler_params=_params("arbitrary"),
        name="hyena_in",
    )(x_ctx, x_lat, mods, w_in, sconv_w, sconv_b, *[c.w for c in casts])


def _hy_conv_plain_kernel(vx_ref, x0_ref, coef_ref, bias_ref, fwd_ref, inv_ref, u_ref, *, L, nseq, cn):
    pieces = [(c, s) for c in range(vx_ref.shape[1] // cn) for s in range(nseq)]

    def forward(piece):
        c, s = piece
        vx = vx_ref[s * L:(s + 1) * L, c * cn:(c + 1) * cn]
        return vx, _dot(fwd_ref[...], vx)

    pending = [forward(p) for p in pieces[:HY_CONV_LOOKAHEAD]]
    for i, (c, s) in enumerate(pieces):
        if i + HY_CONV_LOOKAHEAD < len(pieces):
            pending.append(forward(pieces[i + HY_CONV_LOOKAHEAD]))
        vx, spec = pending.pop(0)
        rows = slice(s * L, (s + 1) * L)
        cols = slice(c * cn, (c + 1) * cn)
        k_cos = coef_ref[0, :, cols]
        k_sin = coef_ref[1, :, cols]
        k_mix = coef_ref[2, :, cols]
        v_cos = spec[0:L]
        v_sin = spec[L:2 * L]
        y_spec = _bf(jnp.concatenate([v_cos * k_cos - v_sin * k_sin, v_cos * k_sin + v_sin * k_mix], axis=0))
        y = _dot(inv_ref[...], y_spec)
        u = (y + bias_ref[:, cols] * vx.astype(F32)) * x0_ref[rows, cols].astype(F32)
        u_ref[rows, cols] = _bf(u)


def _hy_conv_split_kernel(vx_ref, x0_ref, coef_ref, bias_ref, fe_ref, fo_ref, ge_ref, go_ref, u_ref,
                          sv_ref, sx_ref, su_ref, *, L, nseq, cn):
    half = L // 2
    for g in range(vx_ref.shape[1] // LANES):
        lanes = slice(g * LANES, (g + 1) * LANES)
        sv_ref[g] = vx_ref[:, lanes].astype(F32)
        sx_ref[g] = x0_ref[:, lanes].astype(F32)

    def load(ref, start, c):
        return jnp.concatenate([ref[g, pl.ds(start, half, stride=2), :]
                                for g in range(c * cn // LANES, (c + 1) * cn // LANES)], axis=1)

    def store(ref, start, c, val):
        for k, g in enumerate(range(c * cn // LANES, (c + 1) * cn // LANES)):
            ref[g, pl.ds(start, half, stride=2), :] = val[:, k * LANES:(k + 1) * LANES]

    pieces = [(c, s) for c in range(vx_ref.shape[1] // cn) for s in range(nseq)]

    def forward(piece):
        c, s = piece
        v_e = load(sv_ref, s * L, c)
        v_o = load(sv_ref, s * L + 1, c)
        return v_e, v_o, _dot(fe_ref[...], _bf(v_e)), _dot(fo_ref[...], _bf(v_o))

    pending = [forward(p) for p in pieces[:HY_CONV_LOOKAHEAD]]
    for i, (c, s) in enumerate(pieces):
        if i + HY_CONV_LOOKAHEAD < len(pieces):
            pending.append(forward(pieces[i + HY_CONV_LOOKAHEAD]))
        v_e, v_o, a, b = pending.pop(0)
        cols = slice(c * cn, (c + 1) * cn)
        k_c, k_s, u_c, u_s, e2, e3, f2, f3 = [coef_ref[k, :, cols] for k in range(coef_ref.shape[0])]
        s_c, s_s = a[0:half] + b[0:half], a[half:L] + b[half:L]
        d_c, d_s = a[0:half] - b[0:half], a[half:L] - b[half:L]
        t1 = s_c * k_c - s_s * k_s
        t2 = d_c * u_c + d_s * u_s
        q1 = s_c * k_s
        q4 = d_c * u_s
        p_spec = jnp.concatenate([t1 + t2, q1 + s_s * e2 + d_s * e3 - q4], axis=0)
        m_spec = jnp.concatenate([t1 - t2, q1 + s_s * f2 - d_s * f3 + q4], axis=0)
        y_e = _dot(ge_ref[...], _bf(p_spec))
        y_o = _dot(go_ref[...], _bf(m_spec))
        bias = bias_ref[:, cols]
        store(su_ref, s * L, c, (y_e + bias * v_e) * load(sx_ref, s * L, c))
        store(su_ref, s * L + 1, c, (y_o + bias * v_o) * load(sx_ref, s * L + 1, c))
        rows = slice(s * L, (s + 1) * L)
        for g in range(c * cn // LANES, (c + 1) * cn // LANES):
            u_ref[rows, g * LANES:(g + 1) * LANES] = _bf(su_ref[g, rows, :])


def _hy_conv_steps(n):
    nt = n // HY_CONV_TOKENS
    return (D_MODEL // HY_CONV_CHANNELS) * nt, lambda j, i: j * nt + i


def _hy_conv(vx, x0, tok0, n, coef, bias, L, casts=()):
    tm = HY_CONV_TOKENS
    dt = HY_CONV_CHANNELS
    assert tok0 % tm == 0 and n % tm == 0 and tm % L == 0
    src = pl.BlockSpec((tm, dt), lambda j, i: (tok0 // tm + i, j))
    dst = pl.BlockSpec((tm, dt), lambda j, i: (i, j))
    const = lambda shape: pl.BlockSpec(shape, lambda j, i: (0, 0), pipeline_mode=pl.Buffered(1))
    if L >= HY_SPLIT_MIN_LEN:
        fwd, inv = _dft_tables(L)
        tables = (*fwd, *inv)
        body = functools.partial(_hy_conv_split_kernel, L=L, nseq=tm // L, cn=256)
        scratch = [pltpu.VMEM((dt // LANES, tm, LANES), F32)] * 3
    else:
        tables = _plain_dft_tables(L)[:2]
        body = functools.partial(_hy_conv_plain_kernel, L=L, nseq=tm // L, cn=256)
        scratch = []
    return pl.pallas_call(
        _with_casts(body, 4 + len(tables), 1, casts),
        out_shape=(jax.ShapeDtypeStruct((n, D_MODEL), BF16),) + tuple(c.out_shape for c in casts),
        grid=(D_MODEL // dt, n // tm),
        in_specs=[
            src, src,
            pl.BlockSpec(_filter_coef_shape(L) + (dt,), lambda j, i: (0, 0, j)),
            pl.BlockSpec((1, dt), lambda j, i: (0, j)),
        ] + [const(t.shape) for t in tables] + [c.in_spec for c in casts],
        out_specs=(dst,) + tuple(c.out_spec for c in casts),
        scratch_shapes=scratch,
        compiler_params=_params("arbitrary", "arbitrary"),
        name=f"hyena_conv_{L}",
    )(vx, x0, coef, bias, *[_bf(jnp.asarray(t)) for t in tables], *[c.w for c in casts])


def _post_kernel(*refs, layer, ctx_steps, lat_len, n_x, n_a, a_feature_major):
    x_refs, a_refs = refs[:n_x], refs[n_x:n_x + n_a]
    (mod_ref, wo_ref, ln1g_ref, ln1b_ref, win_ref, wout_ref, ln2g_ref, ln2b_ref, o_ref,
     act_ref) = refs[n_x + n_a:]
    d = D_MODEL
    i = pl.program_id(0)
    is_ctx = i < ctx_steps
    tm = act_ref.shape[0]
    mod = mod_ref[pl.ds(_mod_row(i, ctx_steps, tm, lat_len), 1), :]
    g1 = mod[:, 2 * d:3 * d]
    sh2 = mod[:, 3 * d:4 * d]
    sc2 = mod[:, 4 * d:5 * d]
    g2 = mod[:, 5 * d:6 * d]
    ln1 = (ln1g_ref[layer:layer + 1, :], ln1b_ref[layer:layer + 1, :])
    ln2 = (ln2g_ref[layer:layer + 1, :], ln2b_ref[layer:layer + 1, :])

    gr = tm // POST_ROW_GROUPS
    groups = [slice(r * gr, (r + 1) * gr) for r in range(POST_ROW_GROUPS)]
    if a_feature_major:
        mix = [_dot_tn(_pick(is_ctx, a_refs, (slice(None), rows)), wo_ref[...]) for rows in groups]
    else:
        mix = [_dot(_pick(is_ctx, a_refs, (rows, slice(None))), wo_ref[...]) for rows in groups]
    x1 = []
    for rows, mx in zip(groups, mix):
        x = _pick(is_ctx, x_refs, (rows, slice(None)))
        x1.append(_layer_norm(ALPHA * x + g1 * mx, *ln1))
        h = _bf(x1[-1] * (1.0 + sc2) + sh2)
        for c in range(D_FF // FFN_COLS):
            cols = slice(c * FFN_COLS, (c + 1) * FFN_COLS)
            gate = _dot(h, win_ref[:, cols])
            up = _dot(h, win_ref[:, D_FF + c * FFN_COLS:D_FF + (c + 1) * FFN_COLS])
            act_ref[rows, cols] = _bf(gate * jax.nn.sigmoid(gate) * up)
    for rows, xr in zip(groups, x1):
        f = _dot(act_ref[rows, :], wout_ref[...])
        o_ref[rows, :] = _layer_norm(ALPHA * xr + g2 * f, *ln2)


def _post(xs, mixes, n, n_ctx, lat_len, mods, layer, w_o, w_in, w_out, ln1_g, ln1_b, ln2_g, ln2_b, name,
          x_tok0=0, a_feature_major=False, casts=()):
    tm = POST_TOKENS
    ctx_steps = n_ctx // tm
    ctx = lambda i: _ctx_block(i, ctx_steps)
    lat = lambda i: _lat_block(i, ctx_steps)
    blocks = lambda m, tok0: (lambda i: tok0 // tm + i,) if m == 1 else (ctx, lat)
    tok = lambda f: pl.BlockSpec((tm, D_MODEL), lambda i: (f(i), 0))
    feat = lambda f: pl.BlockSpec((D_MODEL, tm), lambda i: (0, f(i)))
    const = lambda shape: pl.BlockSpec(shape, lambda i: (0, 0), pipeline_mode=pl.Buffered(1))
    vec = pl.BlockSpec((DEPTH, D_MODEL), lambda i: (0, 0))
    body = functools.partial(_post_kernel, layer=layer, ctx_steps=ctx_steps, lat_len=lat_len, n_x=len(xs),
                             n_a=len(mixes), a_feature_major=a_feature_major)
    return pl.pallas_call(
        _with_casts(body, len(xs) + len(mixes) + 8, 1, casts),
        out_shape=(jax.ShapeDtypeStruct((n, D_MODEL), F32),) + tuple(c.out_shape for c in casts),
        grid=(n // tm,),
        in_specs=[tok(f) for f in blocks(len(xs), x_tok0)]
        + [(feat if a_feature_major else tok)(f) for f in blocks(len(mixes), 0)]
        + [pl.BlockSpec((None, MOD_ROWS, 6 * D_MODEL), lambda i: (layer, 0, 0)),
           const((D_MODEL, D_MODEL)), vec, vec,
           const((D_MODEL, 2 * D_FF)), const((D_FF, D_MODEL)), vec, vec]
        + [c.in_spec for c in casts],
        out_specs=(tok(lambda i: i),) + tuple(c.out_spec for c in casts),
        scratch_shapes=[pltpu.VMEM((tm, D_FF), BF16)],
        compiler_params=_params("arbitrary"),
        name=name,
    )(*xs, *mixes, mods, w_o, ln1_g, ln1_b, w_in, w_out, ln2_g, ln2_b, *[c.w for c in casts])


def _qkv_kernel(x_ref, mod_ref, wt_ref, qkvt_ref, nk_ref, nv_ref, *, ctx_steps, lat_len):
    d = D_MODEL
    i = pl.program_id(0)
    tm = x_ref.shape[0]
    mod = mod_ref[pl.ds(_mod_row(i, ctx_steps, tm, lat_len), 1), :]
    h = _bf(x_ref[...] * (1.0 + mod[:, d:2 * d]) + mod[:, 0:d])
    qkvt_ref[0:d, :] = _bf(_dot_nt(wt_ref[0:d, :], h) * (HEAD_DIM ** -0.5 * LOG2E))
    kt = _dot_nt(wt_ref[d:2 * d, :], h)
    qkvt_ref[d:2 * d, :] = _bf(kt)
    vt = _dot_nt(wt_ref[2 * d:3 * d, :], h)
    qkvt_ref[2 * d:3 * d, :] = _bf(vt)

    @pl.when(i < ctx_steps)
    def _():
        seq = nk_ref.shape[4]
        for s in range(tm // seq):
            nk_ref[s, 0] = kt[:, s * seq:(s + 1) * seq].reshape(N_HEADS, HEAD_DIM, seq)
            nv_ref[s, 0] = vt[:, s * seq:(s + 1) * seq].reshape(N_HEADS, HEAD_DIM, seq)


def _qkv(x2d, n_ctx, ctx_len, lat_len, mods, layer, w_qkv_t):
    n = x2d.shape[0]
    tm = QKV_TOKENS
    assert tm % ctx_len == 0 and n_ctx % tm == 0
    ctx_steps = n_ctx // tm
    kv_shape = jax.ShapeDtypeStruct((n_ctx // ctx_len, 1, N_HEADS, HEAD_DIM, ctx_len), F32)
    kv_spec = pl.BlockSpec((tm // ctx_len, 1, N_HEADS, HEAD_DIM, ctx_len),
                           lambda i: (_ctx_block(i, ctx_steps), 0, 0, 0, 0))
    return pl.pallas_call(
        functools.partial(_qkv_kernel, ctx_steps=ctx_steps, lat_len=lat_len),
        out_shape=(jax.ShapeDtypeStruct((3 * D_MODEL, n), BF16), kv_shape, kv_shape),
        grid=(n // tm,),
        in_specs=[
            pl.BlockSpec((tm, D_MODEL), lambda i: (i, 0)),
            pl.BlockSpec((None, MOD_ROWS, 6 * D_MODEL), lambda i: (layer, 0, 0)),
            pl.BlockSpec((3 * D_MODEL, D_MODEL), lambda i: (0, 0), pipeline_mode=pl.Buffered(1)),
        ],
        out_specs=(pl.BlockSpec((3 * D_MODEL, tm), lambda i: (0, i)), kv_spec, kv_spec),
        compiler_params=_params("arbitrary"),
        name="qkv",
    )(x2d, mods, w_qkv_t)


def _softmax_weights(*scores):
    m = functools.reduce(jnp.maximum, [jnp.max(s, axis=0, keepdims=True) for s in scores])
    return [_bf(jnp.exp2(s - m)) for s in scores]


def _weighted_values(vt, p):
    ones = jnp.ones((BF16_ROWS, vt.shape[1]), BF16)
    return _dot(jnp.concatenate([vt, ones], axis=0), p)


def _normalise(acc):
    return _bf(acc[0:HEAD_DIM] * (1.0 / acc[HEAD_DIM:HEAD_DIM + 1]))


def _ctx_attn_kernel(qt_ref, kt_ref, vt_ref, ot_ref, *, seq):
    units = [(s, hh) for s in range(qt_ref.shape[1] // seq) for hh in range(N_HEADS)]

    def block(unit):
        s, hh = unit
        return slice(hh * HEAD_DIM, (hh + 1) * HEAD_DIM), slice(s * seq, (s + 1) * seq)

    def scores(unit):
        return _dot_tn(kt_ref[block(unit)], qt_ref[block(unit)])

    pending = [scores(u) for u in units[:CTX_ATTN_LOOKAHEAD]]
    for i, unit in enumerate(units):
        if i + CTX_ATTN_LOOKAHEAD < len(units):
            pending.append(scores(units[i + CTX_ATTN_LOOKAHEAD]))
        (p,) = _softmax_weights(pending.pop(0))
        ot_ref[block(unit)] = _normalise(_weighted_values(vt_ref[block(unit)], p))


def _ctx_attn(qkvt, n_ctx, seq):
    tm = CTX_ATTN_SEQS * seq
    return pl.pallas_call(
        functools.partial(_ctx_attn_kernel, seq=seq),
        out_shape=jax.ShapeDtypeStruct((D_MODEL, n_ctx), BF16),
        grid=(n_ctx // tm,),
        in_specs=[pl.BlockSpec((D_MODEL, tm), lambda b, c=c: (c, b)) for c in range(3)],
        out_specs=pl.BlockSpec((D_MODEL, tm), lambda b: (0, b)),
        compiler_params=_params("arbitrary"),
        name="ctx_attn",
    )(qkvt, qkvt, qkvt)


LAT_ROWS = 16
Q_BLOCK_ROWS = 4
KEY_TILE_ROWS = 4


def _key_band(qb):
    row0 = [min(max(r - WIN_H // 2, 0), LAT_ROWS - WIN_H) for r in range(qb * Q_BLOCK_ROWS, (qb + 1) * Q_BLOCK_ROWS)]
    lo = min(row0) // KEY_TILE_ROWS * KEY_TILE_ROWS
    hi = -(-(max(row0) + WIN_H) // KEY_TILE_ROWS) * KEY_TILE_ROWS
    return lo, hi


KEY_BANDS = tuple(_key_band(qb) for qb in range(LAT_ROWS // Q_BLOCK_ROWS))
MAX_BAND_ROWS = max(hi - lo for lo, hi in KEY_BANDS)


def _fill_bias_table(rpb_ref, tab_ref):
    shape = (GRID_W, LANES)
    kc = lax.broadcasted_iota(jnp.int32, shape, 0)
    lane = lax.broadcasted_iota(jnp.int32, shape, 1)
    qc = lane % GRID_W
    win0 = jnp.clip(qc - WIN_W // 2, 0, GRID_W - WIN_W)
    col_ok = (kc >= win0) & (kc < win0 + WIN_W)
    masked = jnp.full(shape, MASK_VALUE, F32)
    tiles = []
    for dr in range(2 * WIN_H - 1):
        row = jnp.broadcast_to(rpb_ref[dr:dr + 1, :] * LOG2E, shape)
        t = pltpu.roll(row, LANES - (WIN_W - 1), 1, stride=1, stride_axis=0)
        t = jnp.where(lane < GRID_W, t, pltpu.roll(t, GRID_W, 1))
        tiles.append(jnp.where(col_ok, t, MASK_VALUE))

    def tile(qr, kr):
        row0 = min(max(qr - WIN_H // 2, 0), LAT_ROWS - WIN_H)
        if row0 <= kr < row0 + WIN_H:
            return tiles[kr - qr + WIN_H - 1]
        return masked

    for qr in range(0, LAT_ROWS, 2):
        lo, hi = KEY_BANDS[qr // Q_BLOCK_ROWS]
        for i in range(hi - lo):
            both = jnp.where(lane < GRID_W, tile(qr, lo + i), tile(qr + 1, lo + i))
            tab_ref[i * GRID_W:(i + 1) * GRID_W, qr * GRID_W:(qr + 2) * GRID_W] = both


def _lat_attn_kernel(qt_ref, kt_ref, vt_ref, ckt_ref, cvt_ref, rpb_ref, ot_ref, tab_ref):
    seq = LAT_ROWS * GRID_W
    nq = Q_BLOCK_ROWS * GRID_W
    heads = qt_ref.shape[0] // HEAD_DIM
    units = [(b, hh, qb) for b in range(qt_ref.shape[1] // seq) for hh in range(heads)
             for qb in range(len(KEY_BANDS))]

    def rows(hh):
        return slice(hh * HEAD_DIM, (hh + 1) * HEAD_DIM)

    def keys(b, qb):
        lo, hi = KEY_BANDS[qb]
        return slice(b * seq + lo * GRID_W, b * seq + hi * GRID_W)

    def queries(b, qb):
        return slice(b * seq + qb * nq, b * seq + (qb + 1) * nq)

    def scores(unit):
        b, hh, qb = unit
        lo, hi = KEY_BANDS[qb]
        qt = qt_ref[rows(hh), queries(b, qb)]
        s_loc = (_dot_tn(kt_ref[rows(hh), keys(b, qb)], qt)
                 + tab_ref[hh, 0:(hi - lo) * GRID_W, qb * nq:(qb + 1) * nq])
        s_ctx = _dot_tn(_bf(ckt_ref[b, hh]), qt)
        return s_loc, s_ctx

    for hh in range(tab_ref.shape[0]):
        _fill_bias_table(rpb_ref.at[hh], tab_ref.at[hh])
    pending = [scores(u) for u in units[:LAT_ATTN_LOOKAHEAD]]
    for i, (b, hh, qb) in enumerate(units):
        if i + LAT_ATTN_LOOKAHEAD < len(units):
            pending.append(scores(units[i + LAT_ATTN_LOOKAHEAD]))
        s_loc, s_ctx = pending.pop(0)
        p_loc, p_ctx = _softmax_weights(s_loc, s_ctx)
        acc = (_weighted_values(vt_ref[rows(hh), keys(b, qb)], p_loc)
               + _weighted_values(_bf(cvt_ref[b, hh]), p_ctx))
        ot_ref[rows(hh), queries(b, qb)] = _normalise(acc)


def _lat_attn(qkvt, n_ctx, cache_kt, cache_vt, rpb, layer_idx):
    rpb = jnp.pad(rpb[:, :, ::-1], ((0, 0), (0, 1), (0, LANES - (2 * WIN_W - 1))))
    n = qkvt.shape[1] - n_ctx
    seq = LAT_ROWS * GRID_W
    assert n_ctx % n == 0 and n % seq == 0
    nb = n // seq
    nhp = N_HEADS // 2
    past = cache_kt.shape[4]
    feat = lambda c: pl.BlockSpec((2 * HEAD_DIM, n), lambda hp, c=c: (c * nhp + hp, n_ctx // n))
    cache = pl.BlockSpec((nb, None, 2, HEAD_DIM, past), lambda hp: (0, layer_idx, hp, 0, 0))
    return pl.pallas_call(
        _lat_attn_kernel,
        out_shape=jax.ShapeDtypeStruct((D_MODEL, n), BF16),
        grid=(nhp,),
        in_specs=[feat(0), feat(1), feat(2), cache, cache,
                  pl.BlockSpec((2, 2 * WIN_H, LANES), lambda hp: (hp, 0, 0))],
        out_specs=pl.BlockSpec((2 * HEAD_DIM, n), lambda hp: (hp, 0)),
        scratch_shapes=[pltpu.VMEM((2, MAX_BAND_ROWS * GRID_W, seq), F32)],
        compiler_params=_params("arbitrary"),
        name="lat_attn",
    )(qkvt, qkvt, qkvt, cache_kt, cache_vt, rpb)


def kernel(x_prompt, x_sample, cache_k, cache_v, c, c_ctx, ada_w, ada_b, ln1_g, ln1_b, ln2_g, ln2_b, ffn_w_in, ffn_w_out, hy_w_in, hy_sconv_w, hy_sconv_b, hy_pe_w1, hy_pe_b1, hy_pe_w2, hy_pe_b2, hy_pe_w3, hy_sin_freq, hy_decay, hy_bias, hy_w_out, na_w_qkv, na_rpb, na_w_out):
    nbp, lp, d = x_prompt.shape
    nbs, ls, _ = x_sample.shape
    assert d == D_MODEL and ls == LAT_ROWS * GRID_W and c.shape[0] == nbs
    xp = x_prompt.reshape(nbp * lp, d)
    xs = x_sample.reshape(nbs * ls, d)
    n_ctx, n_lat = xp.shape[0], xs.shape[0]
    n = n_ctx + n_lat
    ln = (ln1_g, ln1_b, ln2_g, ln2_b)

    filt = (hy_pe_w1[0], hy_pe_b1[0], hy_pe_w2[0], hy_pe_b2[0], hy_pe_w3[0], hy_sin_freq[0], hy_decay[0])
    steps = DEPTH * D_MODEL // FILTER_CHANNELS
    mods, coef_p, coef_s, hy_w_in0, ffn_w_out0 = _prep(
        c_ctx, c, ada_w, ada_b, (lp, ls), *filt,
        casts=[_Cast(hy_w_in, 0, steps, _step), _Cast(ffn_w_out, 0, steps, _step)])

    sconv_b = hy_sconv_b[0].reshape(1, -1)
    hy_bias0 = hy_bias[0].reshape(1, -1)
    x0, vx, ffn_w_in0 = _hy_in(xp, xs, lp, ls, mods, 0, hy_w_in0, hy_sconv_w[0], sconv_b,
                               casts=[_Cast(ffn_w_in, 0, n // ls, _step)])
    steps, lin = _hy_conv_steps(n_ctx)
    u_p, hy_w_out0 = _hy_conv(vx, x0, 0, n_ctx, coef_p, hy_bias0, lp, casts=[_Cast(hy_w_out, 0, steps, lin)])
    (u_s,) = _hy_conv(vx, x0, n_ctx, n_lat, coef_s, hy_bias0, ls)
    steps = n // POST_TOKENS
    x1, ffn_w_in1, ffn_w_out1, na_w_out0, w_qkv_t = _post(
        (xp, xs), (u_p, u_s), n, n_ctx, ls, mods, 0, hy_w_out0, ffn_w_in0, ffn_w_out0, *ln, name="post0",
        casts=[_Cast(ffn_w_in, 1, steps, _step), _Cast(ffn_w_out, 1, steps, _step),
               _Cast(na_w_out, 0, steps, _step),
               _Cast(na_w_qkv, 0, steps // 2, lambda i: i // 2, transpose=True)])

    qkvt, new_kt, new_vt = _qkv(x1, n_ctx, lp, ls, mods, 1, w_qkv_t)
    at_p = _ctx_attn(qkvt, n_ctx, lp)
    at_s = _lat_attn(qkvt, n_ctx, jnp.swapaxes(cache_k, 3, 4), jnp.swapaxes(cache_v, 3, 4), na_rpb[0], 0)
    ffn1 = (na_w_out0, ffn_w_in1, ffn_w_out1, *ln)
    (yp,) = _post((x1,), (at_p,), n_ctx, n_ctx, ls, mods, 1, *ffn1, name="post1_ctx", a_feature_major=True)
    (ys,) = _post((x1,), (at_s,), n_lat, 0, ls, mods, 1, *ffn1, name="post1_lat", x_tok0=n_ctx,
                  a_feature_major=True)

    return (yp.reshape(nbp, lp, d), ys.reshape(nbs, ls, d),
            jnp.swapaxes(new_kt, 3, 4), jnp.swapaxes(new_vt, 3, 4))
```

```python
import functools
import math

import jax
import jax.numpy as jnp
import numpy as np
from jax import lax
from jax.experimental import pallas as pl
from jax.experimental.pallas import tpu as pltpu

D_MODEL = 1024
N_HEADS = 16
HEAD_DIM = D_MODEL // N_HEADS
D_FF = 2816
GRID_W = 64
WIN_H = 8
WIN_W = 16
N_BANDS = 16
PE_DIM = 1 + 2 * N_BANDS
FILT_W = 64
MOD_SHIFT = 0.05
DEPTH = 2
ALPHA = (2 * DEPTH) ** 0.25
LN_EPS = 1e-5
MASK_VALUE = -1e30
MOD_ROWS = 8

LANES = 128
SUBLANES = 8
BF16_ROWS = 2 * SUBLANES
CTX_ATTN_LOOKAHEAD = 6
CTX_ATTN_SEQS = 4
LAT_ATTN_LOOKAHEAD = 2
LOG2E = math.log2(math.e)
VMEM_LIMIT = 56 * 1024 * 1024

ADA_COLS = 1536
FILTER_CHANNELS = 256
HY_IN_COLS = 256
HY_CONV_TOKENS = 1024
HY_CONV_CHANNELS = 512
HY_CONV_LOOKAHEAD = 2
POST_TOKENS = 512
POST_TOKENS_NO_CASTS = 1024
POST_GROUP_ROWS = 256
FFN_COLS = 256
QKV_TOKENS = 1024

F32 = jnp.float32
BF16 = jnp.bfloat16


def _bf(x):
    return x.astype(BF16)


def _dot(a, b):
    return jnp.dot(a, b, preferred_element_type=F32)


def _dot_nt(a, b):
    return lax.dot_general(a, b, (((1,), (1,)), ((), ())), preferred_element_type=F32)


def _dot_tn(a, b):
    return lax.dot_general(a, b, (((0,), (0,)), ((), ())), preferred_element_type=F32)


def _dot3(a, b):
    a_hi = _bf(a)
    a_lo = _bf(a - a_hi.astype(F32))
    b_hi = _bf(b)
    b_lo = _bf(b - b_hi.astype(F32))
    m = a.shape[0]
    both = _dot(jnp.concatenate([a_hi, a_lo], axis=0), b_hi)
    return both[0:m] + both[m:2 * m] + _dot(a_hi, b_lo)


def _layer_norm(x, g, b):
    mu = jnp.mean(x, axis=-1, keepdims=True)
    xc = x - mu
    var = jnp.mean(xc * xc, axis=-1, keepdims=True)
    return xc * lax.rsqrt(var + LN_EPS) * g + b


def _params(*sem):
    return pltpu.CompilerParams(dimension_semantics=sem, vmem_limit_bytes=VMEM_LIMIT)


class _Cast:
    def __init__(self, w, layer, steps, step_fn, transpose=False):
        _, r, c = w.shape
        assert r % (steps * BF16_ROWS) == 0
        rows = r // steps
        self.w = w
        self.transpose = transpose
        self.in_spec = pl.BlockSpec((None, rows, c), lambda *g: (layer, step_fn(*g), 0))
        if transpose:
            self.out_shape = jax.ShapeDtypeStruct((c, r), BF16)
            self.out_spec = pl.BlockSpec((c, rows), lambda *g: (0, step_fn(*g)))
        else:
            self.out_shape = jax.ShapeDtypeStruct((r, c), BF16)
            self.out_spec = pl.BlockSpec((rows, c), lambda *g: (step_fn(*g), 0))


def _with_casts(body, n_in, n_out, casts):
    nc = len(casts)

    def kernel_fn(*refs):
        ins, src = refs[:n_in], refs[n_in:n_in + nc]
        outs, dst = refs[n_in + nc:n_in + nc + n_out], refs[n_in + nc + n_out:n_in + 2 * nc + n_out]
        for cast, s, d in zip(casts, src, dst):
            d[...] = _bf(s[...].T if cast.transpose else s[...])
        body(*ins, *outs, *refs[n_in + 2 * nc + n_out:])

    return kernel_fn


def _step(i):
    return i


def _ctx_block(i, ctx_steps):
    return jnp.minimum(i, ctx_steps - 1)


def _lat_block(i, ctx_steps):
    return jnp.maximum(i - ctx_steps, 0)


def _mod_row(i, ctx_steps, tm, lat_len):
    return jnp.where(i < ctx_steps, 0, 1 + ((i - ctx_steps) * tm) // lat_len)


def _pick(is_ctx, refs, idx):
    if len(refs) == 1:
        return refs[0][idx]
    return jnp.where(is_ctx, refs[0][idx], refs[1][idx])


@functools.lru_cache(maxsize=None)
def _plain_dft_tables(L):
    n = 2 * L
    f = np.arange(L)[:, None]
    s = np.arange(L)[None, :]
    ang = 2.0 * np.pi * ((f * s) % n) / n
    c = np.cos(ang)
    sn = np.sin(ang)
    sn[0, :] = np.where(np.arange(L) % 2 == 0, 1.0, -1.0)
    fwd = np.concatenate([c, sn], axis=0)
    w = np.full((2 * L, 1), 2.0 / n)
    w[0, 0] = 1.0 / n
    w[L, 0] = 1.0 / n
    inv = (fwd * w).T
    alt = np.where(np.arange(L) % 2 == 0, 1.0, -1.0).astype(np.float32)[:, None]
    return (np.ascontiguousarray(fwd).astype(np.float32), np.ascontiguousarray(inv).astype(np.float32), alt)


@functools.lru_cache(maxsize=None)
def _dft_tables(L):
    n = 2 * L
    half = L // 2
    f = np.arange(half)[:, None]
    m = np.arange(half)[None, :]
    alt = np.where(np.arange(half) % 2 == 0, 1.0, -1.0)
    w = np.full((L, 1), 2.0 / n)
    w[0, 0] = 1.0 / n
    w[half, 0] = 1.0 / n
    fwd, inv = [], []
    for parity in range(2):
        ang = 2.0 * np.pi * ((f * (2 * m + parity)) % n) / n
        sn = np.sin(ang)
        sn[0, :] = alt
        table = np.concatenate([np.cos(ang), sn], axis=0)
        fwd.append(np.ascontiguousarray(table).astype(np.float32))
        inv.append(np.ascontiguousarray((table * w).T).astype(np.float32))
    return fwd, inv


@functools.lru_cache(maxsize=None)
def _filter_dft_tables(L):
    n = 2 * L
    f = np.arange(L // 2)[:, None]
    t = np.arange(L)[None, :]
    ang = 2.0 * np.pi * ((f * t) % n) / n
    table = np.concatenate([np.cos(ang), np.sin(ang)], axis=0).astype(np.float32)
    tt = np.arange(L)
    cols = np.stack([np.where(tt % 2 == 0, 1.0, -1.0), np.cos(np.pi * (tt % 4) / 2.0),
                     np.sin(np.pi * (tt % 4) / 2.0)], axis=1)
    return np.ascontiguousarray(table), np.round(cols).astype(np.float32)


@functools.lru_cache(maxsize=None)
def _filter_features(L):
    t = np.linspace(0.0, 1.0, L, dtype=np.float32)[:, None]
    w = (np.float32(2.0 * math.pi / L) * np.arange(L, dtype=np.float32))[:, None]
    bands = np.linspace(1e-4, N_BANDS - 1, N_BANDS, dtype=np.float32)[None, :]
    arg = (bands * w).astype(np.float64)
    feats = np.concatenate([t.astype(np.float64), np.cos(arg), np.sin(arg)], axis=-1)
    out = np.zeros((L, LANES), np.float32)
    out[:, :PE_DIM] = feats
    return out, t


HY_SPLIT_MIN_LEN = 512


def _filter_coef_shape(L):
    return (8, L // 2) if L >= HY_SPLIT_MIN_LEN else (3, L)


def _ada_tile(layer, cctx_ref, c_ref, w_ref, b_ref, o_ref, cvec_ref):
    nb = c_ref.shape[0]
    cvec_ref[...] = jnp.zeros_like(cvec_ref)
    cvec_ref[0:1, :] = cctx_ref[...]
    cvec_ref[1:1 + nb, :] = c_ref[...]
    c = cvec_ref[...]
    o_ref[...] = _dot3(c * jax.nn.sigmoid(c), w_ref[...]) + b_ref[pl.ds(layer, 1), :]


def _filter_tile(is_first, feats_ref, t_ref, cols_ref, tab_ref, w1_ref, b1_ref, w2_ref, b2_ref, freq_ref,
                 w3f_ref, w3b_ref, decf_ref, decb_ref, out_ref, h_ref, w1p_ref, *, L):
    hi = lax.Precision.HIGHEST

    @pl.when(is_first)
    def _():
        w1p_ref[...] = jnp.zeros_like(w1p_ref)
        w1p_ref[0:PE_DIM, :] = w1_ref[...]
        h1 = jnp.sin(freq_ref[0:1, :] * (jnp.dot(feats_ref[...], w1p_ref[...], precision=hi,
                                                 preferred_element_type=F32) + b1_ref[...]))
        h_ref[...] = jnp.sin(freq_ref[1:2, :] * (jnp.dot(h1, w2_ref[...], precision=hi,
                                                         preferred_element_type=F32) + b2_ref[...]))

    h = h_ref[...]
    t = t_ref[...]
    kf = _dot3(h, w3f_ref[...]) * (jnp.exp(-t * jnp.abs(decf_ref[...])) + MOD_SHIFT)
    kb = _dot3(h, w3b_ref[...]) * (jnp.exp(-t * jnp.abs(decb_ref[...])) + MOD_SHIFT)
    row = lax.broadcasted_iota(jnp.int32, kf.shape, 0)
    kb = jnp.where(row == 0, 0.0, kb)
    ksum = kf + kb
    kdiff = kf - kb
    alt = cols_ref[:, 0:1]

    def spectrum(table, x):
        hi = _bf(x)
        return _dot(table, hi) + _dot(table, _bf(x - hi.astype(F32)))

    if L < HY_SPLIT_MIN_LEN:
        k_c = spectrum(tab_ref[0:L, :], ksum)
        k_s = spectrum(tab_ref[L:2 * L, :], kdiff)
        k_nyq = jnp.sum(ksum * alt, axis=0, keepdims=True)
        out_ref[0] = k_c
        out_ref[1] = jnp.where(row == 0, 0.0, k_s)
        out_ref[2] = jnp.where(row == 0, k_nyq, k_c)
        return

    half = L // 2
    cos_m = tab_ref[0:half, :]
    sin_m = tab_ref[half:L, :]
    k_c = spectrum(cos_m, ksum)
    k_s = spectrum(sin_m, kdiff)
    u_c = spectrum(cos_m, alt * ksum)
    u_s = -spectrum(sin_m, alt * kdiff)
    kc_h = jnp.sum(ksum * cols_ref[:, 1:2], axis=0, keepdims=True)
    ks_h = jnp.sum(kdiff * cols_ref[:, 2:3], axis=0, keepdims=True)
    first = lax.broadcasted_iota(jnp.int32, k_c.shape, 0) == 0
    out_ref[0] = k_c
    out_ref[1] = k_s
    out_ref[2] = u_c
    out_ref[3] = u_s
    out_ref[4] = jnp.where(first, kc_h - ks_h, k_c)
    out_ref[5] = jnp.where(first, kc_h + ks_h, u_c)
    out_ref[6] = jnp.where(first, kc_h + ks_h, k_c)
    out_ref[7] = jnp.where(first, kc_h - ks_h, u_c)


def _prep_kernel(cctx_ref, c_ref, w_ref, b_ref, fa_ref, ta_ref, ca_ref, tba_ref, fb_ref, tb_ref, cb_ref,
                 tbb_ref, w1_ref, b1_ref, w2_ref, b2_ref, freq_ref, w3f_ref, w3b_ref, decf_ref, decb_ref,
                 mod_ref, coefa_ref, coefb_ref, cvec_ref, ha_ref, hb_ref, w1p_ref, *, lens, tiles):
    s = pl.program_id(0)
    _ada_tile(s // tiles, cctx_ref, c_ref, w_ref, b_ref, mod_ref, cvec_ref)
    shared = (w1_ref, b1_ref, w2_ref, b2_ref, freq_ref, w3f_ref, w3b_ref, decf_ref, decb_ref)

    @pl.when(s < tiles)
    def _():
        _filter_tile(s == 0, fa_ref, ta_ref, ca_ref, tba_ref, *shared, coefa_ref, ha_ref, w1p_ref, L=lens[0])

    @pl.when(s >= tiles)
    def _():
        _filter_tile(s == tiles, fb_ref, tb_ref, cb_ref, tbb_ref, *shared, coefb_ref, hb_ref, w1p_ref, L=lens[1])


def _prep(c_ctx, c, ada_w, ada_b, lens, pe_w1, pe_b1, pe_w2, pe_b2, pe_w3, sin_freq, decay, casts=()):
    assert 1 + c.shape[0] <= MOD_ROWS and DEPTH == 2
    tn = ADA_COLS
    dt = FILTER_CHANNELS
    tiles = D_MODEL // dt
    assert 6 * D_MODEL // tn == tiles
    full = lambda shape: pl.BlockSpec(shape, lambda s: (0,) * len(shape))
    consts, const_specs, out_shapes, out_specs, scratch = [], [], [], [], []
    for k, L in enumerate(lens):
        feats, t = _filter_features(L)
        if L >= HY_SPLIT_MIN_LEN:
            table, cols = _filter_dft_tables(L)
        else:
            table, _, cols = _plain_dft_tables(L)
        consts += [feats, t, cols, _bf(jnp.asarray(table))]
        const_specs += [full(feats.shape), full(t.shape), full(cols.shape), full(table.shape)]
        nc, nf = _filter_coef_shape(L)
        out_shapes.append(jax.ShapeDtypeStruct((nc, nf, D_MODEL), F32))
        block = (lambda s: (0, 0, jnp.minimum(s, tiles - 1))) if k == 0 else (
            lambda s: (0, 0, jnp.maximum(s - tiles, 0)))
        out_specs.append(pl.BlockSpec((nc, nf, dt), block))
        scratch.append(pltpu.VMEM((L, FILT_W), F32))
    b1 = pe_b1.reshape(1, FILT_W)
    b2 = pe_b2.reshape(1, FILT_W)
    dec = decay.reshape(1, 2 * D_MODEL)
    scratch.append(pltpu.VMEM((LANES, FILT_W), F32))
    return pl.pallas_call(
        _with_casts(functools.partial(_prep_kernel, lens=tuple(lens), tiles=tiles), 21, 3, casts),
        out_shape=(jax.ShapeDtypeStruct((DEPTH, MOD_ROWS, 6 * D_MODEL), F32), *out_shapes,
                   *[k.out_shape for k in casts]),
        grid=(DEPTH * tiles,),
        in_specs=[
            full((1, D_MODEL)), full(c.shape),
            pl.BlockSpec((None, D_MODEL, tn), lambda s: (s // tiles, 0, s % tiles)),
            pl.BlockSpec((DEPTH, tn), lambda s: (0, s % tiles)),
            *const_specs,
            full((PE_DIM, FILT_W)), full((1, FILT_W)), full((FILT_W, FILT_W)), full((1, FILT_W)),
            full((2, FILT_W)),
            pl.BlockSpec((FILT_W, dt), lambda s: (0, s % tiles)),
            pl.BlockSpec((FILT_W, dt), lambda s: (0, tiles + s % tiles)),
            pl.BlockSpec((1, dt), lambda s: (0, s % tiles)),
            pl.BlockSpec((1, dt), lambda s: (0, tiles + s % tiles)),
            *[k.in_spec for k in casts],
        ],
        out_specs=(pl.BlockSpec((None, MOD_ROWS, tn), lambda s: (s // tiles, 0, s % tiles)), *out_specs,
                   *[k.out_spec for k in casts]),
        scratch_shapes=[pltpu.VMEM((MOD_ROWS, D_MODEL), F32), *scratch],
        compiler_params=_params("arbitrary"),
        name="prep",
    )(c_ctx.reshape(1, D_MODEL), c, ada_w, ada_b, *consts, pe_w1, b1, pe_w2, b2, sin_freq, pe_w3, pe_w3, dec, dec,
      *[k.w for k in casts])


def _hy_in_kernel(xc_ref, xl_ref, mod_ref, w_ref, cw_ref, cb_ref, x0_ref, vx_ref, *, ctx_steps, ctx_len, cn):
    i = pl.program_id(0)
    is_ctx = i < ctx_steps
    tm = x0_ref.shape[0]
    mod = mod_ref[pl.ds(_mod_row(i, ctx_steps, tm, tm), 1), :]
    x = _pick(is_ctx, (xc_ref, xl_ref), (slice(None), slice(None)))
    h = _bf(x * (1.0 + mod[:, D_MODEL:2 * D_MODEL]) + mod[:, 0:D_MODEL])
    row = lax.broadcasted_iota(jnp.int32, (SUBLANES, cn), 0)

    def conv(col):
        z = _dot(h, w_ref[:, col:col + cn])
        zm = pltpu.roll(z, 1, 0)
        zp = pltpu.roll(z, tm - 1, 0)
        w0 = cw_ref[0:1, col:col + cn]
        w1 = cw_ref[1:2, col:col + cn]
        w2 = cw_ref[2:3, col:col + cn]
        b = cb_ref[:, col:col + cn]
        out = zm * w0 + z * w1 + zp * w2 + b
        pieces = []
        for s in range(tm // ctx_len):
            lo = s * ctx_len
            hi = lo + ctx_len - SUBLANES
            head = (jnp.where(row == 0, 0.0, zm[lo:lo + SUBLANES]) * w0 + z[lo:lo + SUBLANES] * w1
                    + zp[lo:lo + SUBLANES] * w2 + b)
            tail = (zm[hi:hi + SUBLANES] * w0 + z[hi:hi + SUBLANES] * w1
                    + jnp.where(row == SUBLANES - 1, 0.0, zp[hi:hi + SUBLANES]) * w2 + b)
            if lo > 0:
                head = jnp.where(is_ctx, head, out[lo:lo + SUBLANES])
            if hi + SUBLANES < tm:
                tail = jnp.where(is_ctx, tail, out[hi:hi + SUBLANES])
            pieces += [head, out[lo + SUBLANES:hi], tail]
        return jnp.concatenate(pieces, axis=0)

    for c in range(D_MODEL // cn):
        x0_ref[:, c * cn:(c + 1) * cn] = _bf(conv(c * cn))
        x1 = conv(D_MODEL + c * cn)
        v = conv(2 * D_MODEL + c * cn)
        vx_ref[:, c * cn:(c + 1) * cn] = _bf(v * x1)


def _hy_in(x_ctx, x_lat, ctx_len, lat_len, mods, layer, w_in, sconv_w, sconv_b, casts=()):
    tm = lat_len
    assert tm % ctx_len == 0 and x_ctx.shape[0] % tm == 0
    ctx_steps = x_ctx.shape[0] // tm
    n = x_ctx.shape[0] + x_lat.shape[0]
    tok = pl.BlockSpec((tm, D_MODEL), lambda i: (i, 0))
    const = lambda shape: pl.BlockSpec(shape, lambda i: (0, 0), pipeline_mode=pl.Buffered(1))
    act = jax.ShapeDtypeStruct((n, D_MODEL), BF16)
    body = functools.partial(_hy_in_kernel, ctx_steps=ctx_steps, ctx_len=ctx_len, cn=HY_IN_COLS)
    return pl.pallas_call(
        _with_casts(body, 6, 2, casts),
        out_shape=(act, act) + tuple(c.out_shape for c in casts),
        grid=(n // tm,),
        in_specs=[
            pl.BlockSpec((tm, D_MODEL), lambda i: (_ctx_block(i, ctx_steps), 0)),
            pl.BlockSpec((tm, D_MODEL), lambda i: (_lat_block(i, ctx_steps), 0)),
            pl.BlockSpec((None, MOD_ROWS, 6 * D_MODEL), lambda i: (layer, 0, 0)),
            const((D_MODEL, 3 * D_MODEL)),
            pl.BlockSpec((3, 3 * D_MODEL), lambda i: (0, 0)),
            pl.BlockSpec((1, 3 * D_MODEL), lambda i: (0, 0)),
        ] + [c.in_spec for c in casts],
        out_specs=(tok, tok) + tuple(c.out_spec for c in casts),
        compiler_params=_params("arbitrary"),
        name="hyena_in",
    )(x_ctx, x_lat, mods, w_in, sconv_w, sconv_b, *[c.w for c in casts])


def _hy_conv_plain_kernel(vx_ref, x0_ref, coef_ref, bias_ref, fwd_ref, inv_ref, u_ref, *, L, nseq, cn):
    pieces = [(c, s) for c in range(vx_ref.shape[1] // cn) for s in range(nseq)]

    def forward(piece):
        c, s = piece
        vx = vx_ref[s * L:(s + 1) * L, c * cn:(c + 1) * cn]
        return vx, _dot(fwd_ref[...], vx)

    pending = [forward(p) for p in pieces[:HY_CONV_LOOKAHEAD]]
    for i, (c, s) in enumerate(pieces):
        if i + HY_CONV_LOOKAHEAD < len(pieces):
            pending.append(forward(pieces[i + HY_CONV_LOOKAHEAD]))
        vx, spec = pending.pop(0)
        rows = slice(s * L, (s + 1) * L)
        cols = slice(c * cn, (c + 1) * cn)
        k_cos = coef_ref[0, :, cols]
        k_sin = coef_ref[1, :, cols]
        k_mix = coef_ref[2, :, cols]
        v_cos = spec[0:L]
        v_sin = spec[L:2 * L]
        y_spec = _bf(jnp.concatenate([v_cos * k_cos - v_sin * k_sin, v_cos * k_sin + v_sin * k_mix], axis=0))
        y = _dot(inv_ref[...], y_spec)
        u = (y + bias_ref[:, cols] * vx.astype(F32)) * x0_ref[rows, cols].astype(F32)
        u_ref[rows, cols] = _bf(u)


def _hy_conv_split_kernel(vx_ref, x0_ref, coef_ref, bias_ref, fe_ref, fo_ref, ge_ref, go_ref, u_ref,
                          sv_ref, sx_ref, su_ref, *, L, nseq, cn):
    half = L // 2
    for g in range(vx_ref.shape[1] // LANES):
        lanes = slice(g * LANES, (g + 1) * LANES)
        sv_ref[g] = vx_ref[:, lanes].astype(F32)
        sx_ref[g] = x0_ref[:, lanes].astype(F32)

    def load(ref, start, c):
        return jnp.concatenate([ref[g, pl.ds(start, half, stride=2), :]
                                for g in range(c * cn // LANES, (c + 1) * cn // LANES)], axis=1)

    def store(ref, start, c, val):
        for k, g in enumerate(range(c * cn // LANES, (c + 1) * cn // LANES)):
            ref[g, pl.ds(start, half, stride=2), :] = val[:, k * LANES:(k + 1) * LANES]

    pieces = [(c, s) for c in range(vx_ref.shape[1] // cn) for s in range(nseq)]

    def forward(piece):
        c, s = piece
        v_e = load(sv_ref, s * L, c)
        v_o = load(sv_ref, s * L + 1, c)
        return v_e, v_o, _dot(fe_ref[...], _bf(v_e)), _dot(fo_ref[...], _bf(v_o))

    pending = [forward(p) for p in pieces[:HY_CONV_LOOKAHEAD]]
    for i, (c, s) in enumerate(pieces):
        if i + HY_CONV_LOOKAHEAD < len(pieces):
            pending.append(forward(pieces[i + HY_CONV_LOOKAHEAD]))
        v_e, v_o, a, b = pending.pop(0)
        cols = slice(c * cn, (c + 1) * cn)
        k_c, k_s, u_c, u_s, e2, e3, f2, f3 = [coef_ref[k, :, cols] for k in range(coef_ref.shape[0])]
        s_c, s_s = a[0:half] + b[0:half], a[half:L] + b[half:L]
        d_c, d_s = a[0:half] - b[0:half], a[half:L] - b[half:L]
        t1 = s_c * k_c - s_s * k_s
        t2 = d_c * u_c + d_s * u_s
        q1 = s_c * k_s
        q4 = d_c * u_s
        p_spec = jnp.concatenate([t1 + t2, q1 + s_s * e2 + d_s * e3 - q4], axis=0)
        m_spec = jnp.concatenate([t1 - t2, q1 + s_s * f2 - d_s * f3 + q4], axis=0)
        y_e = _dot(ge_ref[...], _bf(p_spec))
        y_o = _dot(go_ref[...], _bf(m_spec))
        bias = bias_ref[:, cols]
        store(su_ref, s * L, c, (y_e + bias * v_e) * load(sx_ref, s * L, c))
        store(su_ref, s * L + 1, c, (y_o + bias * v_o) * load(sx_ref, s * L + 1, c))
        rows = slice(s * L, (s + 1) * L)
        for g in range(c * cn // LANES, (c + 1) * cn // LANES):
            u_ref[rows, g * LANES:(g + 1) * LANES] = _bf(su_ref[g, rows, :])


def _hy_conv_steps(n):
    nt = n // HY_CONV_TOKENS
    return (D_MODEL // HY_CONV_CHANNELS) * nt, lambda j, i: j * nt + i


def _hy_conv(vx, x0, tok0, n, coef, bias, L, casts=()):
    tm = HY_CONV_TOKENS
    dt = HY_CONV_CHANNELS
    assert tok0 % tm == 0 and n % tm == 0 and tm % L == 0
    src = pl.BlockSpec((tm, dt), lambda j, i: (tok0 // tm + i, j))
    dst = pl.BlockSpec((tm, dt), lambda j, i: (i, j))
    const = lambda shape: pl.BlockSpec(shape, lambda j, i: (0, 0), pipeline_mode=pl.Buffered(1))
    if L >= HY_SPLIT_MIN_LEN:
        fwd, inv = _dft_tables(L)
        tables = (*fwd, *inv)
        body = functools.partial(_hy_conv_split_kernel, L=L, nseq=tm // L, cn=256)
        scratch = [pltpu.VMEM((dt // LANES, tm, LANES), F32)] * 3
    else:
        tables = _plain_dft_tables(L)[:2]
        body = functools.partial(_hy_conv_plain_kernel, L=L, nseq=tm // L, cn=256)
        scratch = []
    return pl.pallas_call(
        _with_casts(body, 4 + len(tables), 1, casts),
        out_shape=(jax.ShapeDtypeStruct((n, D_MODEL), BF16),) + tuple(c.out_shape for c in casts),
        grid=(D_MODEL // dt, n // tm),
        in_specs=[
            src, src,
            pl.BlockSpec(_filter_coef_shape(L) + (dt,), lambda j, i: (0, 0, j)),
            pl.BlockSpec((1, dt), lambda j, i: (0, j)),
        ] + [const(t.shape) for t in tables] + [c.in_spec for c in casts],
        out_specs=(dst,) + tuple(c.out_spec for c in casts),
        scratch_shapes=scratch,
        compiler_params=_params("arbitrary", "arbitrary"),
        name=f"hyena_conv_{L}",
    )(vx, x0, coef, bias, *[_bf(jnp.asarray(t)) for t in tables], *[c.w for c in casts])


def _post_kernel(*refs, layer, ctx_steps, lat_len, n_x, n_a, a_feature_major):
    x_refs, a_refs = refs[:n_x], refs[n_x:n_x + n_a]
    (mod_ref, wo_ref, ln1g_ref, ln1b_ref, win_ref, wout_ref, ln2g_ref, ln2b_ref, o_ref,
     act_ref) = refs[n_x + n_a:]
    d = D_MODEL
    i = pl.program_id(0)
    is_ctx = i < ctx_steps
    tm = act_ref.shape[0]
    mod = mod_ref[pl.ds(_mod_row(i, ctx_steps, tm, lat_len), 1), :]
    g1 = mod[:, 2 * d:3 * d]
    sh2 = mod[:, 3 * d:4 * d]
    sc2 = mod[:, 4 * d:5 * d]
    g2 = mod[:, 5 * d:6 * d]
    ln1 = (ln1g_ref[layer:layer + 1, :], ln1b_ref[layer:layer + 1, :])
    ln2 = (ln2g_ref[layer:layer + 1, :], ln2b_ref[layer:layer + 1, :])

    groups = [slice(r, r + POST_GROUP_ROWS) for r in range(0, tm, POST_GROUP_ROWS)]
    if a_feature_major:
        mix = [_dot_tn(_pick(is_ctx, a_refs, (slice(None), rows)), wo_ref[...]) for rows in groups]
    else:
        mix = [_dot(_pick(is_ctx, a_refs, (rows, slice(None))), wo_ref[...]) for rows in groups]
    x1 = []
    for rows, mx in zip(groups, mix):
        x = _pick(is_ctx, x_refs, (rows, slice(None)))
        x1.append(_layer_norm(ALPHA * x + g1 * mx, *ln1))
        h = _bf(x1[-1] * (1.0 + sc2) + sh2)
        for c in range(D_FF // FFN_COLS):
            cols = slice(c * FFN_COLS, (c + 1) * FFN_COLS)
            gate = _dot(h, win_ref[:, cols])
            up = _dot(h, win_ref[:, D_FF + c * FFN_COLS:D_FF + (c + 1) * FFN_COLS])
            act_ref[rows, cols] = _bf(gate * jax.nn.sigmoid(gate) * up)
    for rows, xr in zip(groups, x1):
        f = _dot(act_ref[rows, :], wout_ref[...])
        o_ref[rows, :] = _layer_norm(ALPHA * xr + g2 * f, *ln2)


def _post(xs, mixes, n, n_ctx, lat_len, mods, layer, w_o, w_in, w_out, ln1_g, ln1_b, ln2_g, ln2_b, name,
          x_tok0=0, a_feature_major=False, casts=(), tm=POST_TOKENS):
    ctx_steps = n_ctx // tm
    ctx = lambda i: _ctx_block(i, ctx_steps)
    lat = lambda i: _lat_block(i, ctx_steps)
    blocks = lambda m, tok0: (lambda i: tok0 // tm + i,) if m == 1 else (ctx, lat)
    tok = lambda f: pl.BlockSpec((tm, D_MODEL), lambda i: (f(i), 0))
    feat = lambda f: pl.BlockSpec((D_MODEL, tm), lambda i: (0, f(i)))
    const = lambda shape: pl.BlockSpec(shape, lambda i: (0, 0), pipeline_mode=pl.Buffered(1))
    vec = pl.BlockSpec((DEPTH, D_MODEL), lambda i: (0, 0))
    body = functools.partial(_post_kernel, layer=layer, ctx_steps=ctx_steps, lat_len=lat_len, n_x=len(xs),
                             n_a=len(mixes), a_feature_major=a_feature_major)
    return pl.pallas_call(
        _with_casts(body, len(xs) + len(mixes) + 8, 1, casts),
        out_shape=(jax.ShapeDtypeStruct((n, D_MODEL), F32),) + tuple(c.out_shape for c in casts),
        grid=(n // tm,),
        in_specs=[tok(f) for f in blocks(len(xs), x_tok0)]
        + [(feat if a_feature_major else tok)(f) for f in blocks(len(mixes), 0)]
        + [pl.BlockSpec((None, MOD_ROWS, 6 * D_MODEL), lambda i: (layer, 0, 0)),
           const((D_MODEL, D_MODEL)), vec, vec,
           const((D_MODEL, 2 * D_FF)), const((D_FF, D_MODEL)), vec, vec]
        + [c.in_spec for c in casts],
        out_specs=(tok(lambda i: i),) + tuple(c.out_spec for c in casts),
        scratch_shapes=[pltpu.VMEM((tm, D_FF), BF16)],
        compiler_params=_params("arbitrary"),
        name=name,
    )(*xs, *mixes, mods, w_o, ln1_g, ln1_b, w_in, w_out, ln2_g, ln2_b, *[c.w for c in casts])


def _qkv_kernel(x_ref, mod_ref, wt_ref, qkvt_ref, nk_ref, nv_ref, *, ctx_steps, lat_len):
    d = D_MODEL
    i = pl.program_id(0)
    tm = x_ref.shape[0]
    mod = mod_ref[pl.ds(_mod_row(i, ctx_steps, tm, lat_len), 1), :]
    h = _bf(x_ref[...] * (1.0 + mod[:, d:2 * d]) + mod[:, 0:d])
    qkvt_ref[0:d, :] = _bf(_dot_nt(wt_ref[0:d, :], h) * (HEAD_DIM ** -0.5 * LOG2E))
    kt = _dot_nt(wt_ref[d:2 * d, :], h)
    qkvt_ref[d:2 * d, :] = _bf(kt)
    vt = _dot_nt(wt_ref[2 * d:3 * d, :], h)
    qkvt_ref[2 * d:3 * d, :] = _bf(vt)

    @pl.when(i < ctx_steps)
    def _():
        seq = nk_ref.shape[4]
        for s in range(tm // seq):
            nk_ref[s, 0] = kt[:, s * seq:(s + 1) * seq].reshape(N_HEADS, HEAD_DIM, seq)
            nv_ref[s, 0] = vt[:, s * seq:(s + 1) * seq].reshape(N_HEADS, HEAD_DIM, seq)


def _qkv(x2d, n_ctx, ctx_len, lat_len, mods, layer, w_qkv_t):
    n = x2d.shape[0]
    tm = QKV_TOKENS
    assert tm % ctx_len == 0 and n_ctx % tm == 0
    ctx_steps = n_ctx // tm
    kv_shape = jax.ShapeDtypeStruct((n_ctx // ctx_len, 1, N_HEADS, HEAD_DIM, ctx_len), F32)
    kv_spec = pl.BlockSpec((tm // ctx_len, 1, N_HEADS, HEAD_DIM, ctx_len),
                           lambda i: (_ctx_block(i, ctx_steps), 0, 0, 0, 0))
    return pl.pallas_call(
        functools.partial(_qkv_kernel, ctx_steps=ctx_steps, lat_len=lat_len),
        out_shape=(jax.ShapeDtypeStruct((3 * D_MODEL, n), BF16), kv_shape, kv_shape),
        grid=(n // tm,),
        in_specs=[
            pl.BlockSpec((tm, D_MODEL), lambda i: (i, 0)),
            pl.BlockSpec((None, MOD_ROWS, 6 * D_MODEL), lambda i: (layer, 0, 0)),
            pl.BlockSpec((3 * D_MODEL, D_MODEL), lambda i: (0, 0), pipeline_mode=pl.Buffered(1)),
        ],
        out_specs=(pl.BlockSpec((3 * D_MODEL, tm), lambda i: (0, i)), kv_spec, kv_spec),
        compiler_params=_params("arbitrary"),
        name="qkv",
    )(x2d, mods, w_qkv_t)


def _softmax_weights(*scores):
    m = functools.reduce(jnp.maximum, [jnp.max(s, axis=0, keepdims=True) for s in scores])
    return [_bf(jnp.exp2(s - m)) for s in scores]


def _weighted_values(vt, p):
    ones = jnp.ones((BF16_ROWS, vt.shape[1]), BF16)
    return _dot(jnp.concatenate([vt, ones], axis=0), p)


def _normalise(acc):
    return _bf(acc[0:HEAD_DIM] * (1.0 / acc[HEAD_DIM:HEAD_DIM + 1]))


def _ctx_attn_kernel(qt_ref, kt_ref, vt_ref, ot_ref, *, seq):
    units = [(s, hh) for s in range(qt_ref.shape[1] // seq) for hh in range(N_HEADS)]

    def block(unit):
        s, hh = unit
        return slice(hh * HEAD_DIM, (hh + 1) * HEAD_DIM), slice(s * seq, (s + 1) * seq)

    def scores(unit):
        return _dot_tn(kt_ref[block(unit)], qt_ref[block(unit)])

    pending = [scores(u) for u in units[:CTX_ATTN_LOOKAHEAD]]
    for i, unit in enumerate(units):
        if i + CTX_ATTN_LOOKAHEAD < len(units):
            pending.append(scores(units[i + CTX_ATTN_LOOKAHEAD]))
        (p,) = _softmax_weights(pending.pop(0))
        ot_ref[block(unit)] = _normalise(_weighted_values(vt_ref[block(unit)], p))


def _ctx_attn(qkvt, n_ctx, seq):
    tm = CTX_ATTN_SEQS * seq
    return pl.pallas_call(
        functools.partial(_ctx_attn_kernel, seq=seq),
        out_shape=jax.ShapeDtypeStruct((D_MODEL, n_ctx), BF16),
        grid=(n_ctx // tm,),
        in_specs=[pl.BlockSpec((D_MODEL, tm), lambda b, c=c: (c, b)) for c in range(3)],
        out_specs=pl.BlockSpec((D_MODEL, tm), lambda b: (0, b)),
        compiler_params=_params("arbitrary"),
        name="ctx_attn",
    )(qkvt, qkvt, qkvt)


LAT_ROWS = 16
Q_BLOCK_ROWS = 4
KEY_TILE_ROWS = 4


def _key_band(qb):
    row0 = [min(max(r - WIN_H // 2, 0), LAT_ROWS - WIN_H) for r in range(qb * Q_BLOCK_ROWS, (qb + 1) * Q_BLOCK_ROWS)]
    lo = min(row0) // KEY_TILE_ROWS * KEY_TILE_ROWS
    hi = -(-(max(row0) + WIN_H) // KEY_TILE_ROWS) * KEY_TILE_ROWS
    return lo, hi


KEY_BANDS = tuple(_key_band(qb) for qb in range(LAT_ROWS // Q_BLOCK_ROWS))
MAX_BAND_ROWS = max(hi - lo for lo, hi in KEY_BANDS)


def _fill_bias_table(rpb_ref, tab_ref):
    shape = (GRID_W, LANES)
    kc = lax.broadcasted_iota(jnp.int32, shape, 0)
    lane = lax.broadcasted_iota(jnp.int32, shape, 1)
    qc = lane % GRID_W
    win0 = jnp.clip(qc - WIN_W // 2, 0, GRID_W - WIN_W)
    col_ok = (kc >= win0) & (kc < win0 + WIN_W)
    masked = jnp.full(shape, MASK_VALUE, F32)
    tiles = []
    for dr in range(2 * WIN_H - 1):
        row = jnp.broadcast_to(rpb_ref[dr:dr + 1, :] * LOG2E, shape)
        t = pltpu.roll(row, LANES - (WIN_W - 1), 1, stride=1, stride_axis=0)
        t = jnp.where(lane < GRID_W, t, pltpu.roll(t, GRID_W, 1))
        tiles.append(jnp.where(col_ok, t, MASK_VALUE))

    def tile(qr, kr):
        row0 = min(max(qr - WIN_H // 2, 0), LAT_ROWS - WIN_H)
        if row0 <= kr < row0 + WIN_H:
            return tiles[kr - qr + WIN_H - 1]
        return masked

    for qr in range(0, LAT_ROWS, 2):
        lo, hi = KEY_BANDS[qr // Q_BLOCK_ROWS]
        for i in range(hi - lo):
            both = jnp.where(lane < GRID_W, tile(qr, lo + i), tile(qr + 1, lo + i))
            tab_ref[i * GRID_W:(i + 1) * GRID_W, qr * GRID_W:(qr + 2) * GRID_W] = both


def _lat_attn_kernel(qt_ref, kt_ref, vt_ref, ckt_ref, cvt_ref, rpb_ref, ot_ref, tab_ref):
    seq = LAT_ROWS * GRID_W
    nq = Q_BLOCK_ROWS * GRID_W
    heads = qt_ref.shape[0] // HEAD_DIM
    units = [(b, hh, qb) for b in range(qt_ref.shape[1] // seq) for hh in range(heads)
             for qb in range(len(KEY_BANDS))]

    def rows(hh):
        return slice(hh * HEAD_DIM, (hh + 1) * HEAD_DIM)

    def keys(b, qb):
        lo, hi = KEY_BANDS[qb]
        return slice(b * seq + lo * GRID_W, b * seq + hi * GRID_W)

    def queries(b, qb):
        return slice(b * seq + qb * nq, b * seq + (qb + 1) * nq)

    def scores(unit):
        b, hh, qb = unit
        lo, hi = KEY_BANDS[qb]
        qt = qt_ref[rows(hh), queries(b, qb)]
        s_loc = (_dot_tn(kt_ref[rows(hh), keys(b, qb)], qt)
                 + tab_ref[hh, 0:(hi - lo) * GRID_W, qb * nq:(qb + 1) * nq])
        s_ctx = _dot_tn(_bf(ckt_ref[b, hh]), qt)
        return s_loc, s_ctx

    for hh in range(tab_ref.shape[0]):
        _fill_bias_table(rpb_ref.at[hh], tab_ref.at[hh])
    pending = [scores(u) for u in units[:LAT_ATTN_LOOKAHEAD]]
    for i, (b, hh, qb) in enumerate(units):
        if i + LAT_ATTN_LOOKAHEAD < len(units):
            pending.append(scores(units[i + LAT_ATTN_LOOKAHEAD]))
        s_loc, s_ctx = pending.pop(0)
        p_loc, p_ctx = _softmax_weights(s_loc, s_ctx)
        acc = (_weighted_values(vt_ref[rows(hh), keys(b, qb)], p_loc)
               + _weighted_values(_bf(cvt_ref[b, hh]), p_ctx))
        ot_ref[rows(hh), queries(b, qb)] = _normalise(acc)


def _lat_attn(qkvt, n_ctx, cache_kt, cache_vt, rpb, layer_idx):
    rpb = jnp.pad(rpb[:, :, ::-1], ((0, 0), (0, 1), (0, LANES - (2 * WIN_W - 1))))
    n = qkvt.shape[1] - n_ctx
    seq = LAT_ROWS * GRID_W
    assert n_ctx % n == 0 and n % seq == 0
    nb = n // seq
    nhp = N_HEADS // 2
    past = cache_kt.shape[4]
    feat = lambda c: pl.BlockSpec((2 * HEAD_DIM, n), lambda hp, c=c: (c * nhp + hp, n_ctx // n))
    cache = pl.BlockSpec((nb, None, 2, HEAD_DIM, past), lambda hp: (0, layer_idx, hp, 0, 0))
    return pl.pallas_call(
        _lat_attn_kernel,
        out_shape=jax.ShapeDtypeStruct((D_MODEL, n), BF16),
        grid=(nhp,),
        in_specs=[feat(0), feat(1), feat(2), cache, cache,
                  pl.BlockSpec((2, 2 * WIN_H, LANES), lambda hp: (hp, 0, 0))],
        out_specs=pl.BlockSpec((2 * HEAD_DIM, n), lambda hp: (hp, 0)),
        scratch_shapes=[pltpu.VMEM((2, MAX_BAND_ROWS * GRID_W, seq), F32)],
        compiler_params=_params("arbitrary"),
        name="lat_attn",
    )(qkvt, qkvt, qkvt, cache_kt, cache_vt, rpb)


def kernel(x_prompt, x_sample, cache_k, cache_v, c, c_ctx, ada_w, ada_b, ln1_g, ln1_b, ln2_g, ln2_b, ffn_w_in, ffn_w_out, hy_w_in, hy_sconv_w, hy_sconv_b, hy_pe_w1, hy_pe_b1, hy_pe_w2, hy_pe_b2, hy_pe_w3, hy_sin_freq, hy_decay, hy_bias, hy_w_out, na_w_qkv, na_rpb, na_w_out):
    nbp, lp, d = x_prompt.shape
    nbs, ls, _ = x_sample.shape
    assert d == D_MODEL and ls == LAT_ROWS * GRID_W and c.shape[0] == nbs
    xp = x_prompt.reshape(nbp * lp, d)
    xs = x_sample.reshape(nbs * ls, d)
    n_ctx, n_lat = xp.shape[0], xs.shape[0]
    n = n_ctx + n_lat
    ln = (ln1_g, ln1_b, ln2_g, ln2_b)

    filt = (hy_pe_w1[0], hy_pe_b1[0], hy_pe_w2[0], hy_pe_b2[0], hy_pe_w3[0], hy_sin_freq[0], hy_decay[0])
    steps = DEPTH * D_MODEL // FILTER_CHANNELS
    mods, coef_p, coef_s, hy_w_in0 = _prep(c_ctx, c, ada_w, ada_b, (lp, ls), *filt,
                                           casts=[_Cast(hy_w_in, 0, steps, _step)])

    sconv_b = hy_sconv_b[0].reshape(1, -1)
    hy_bias0 = hy_bias[0].reshape(1, -1)
    x0, vx, ffn_w_in0, ffn_w_out0 = _hy_in(xp, xs, lp, ls, mods, 0, hy_w_in0, hy_sconv_w[0], sconv_b,
                                           casts=[_Cast(ffn_w_in, 0, n // ls, _step),
                                                  _Cast(ffn_w_out, 0, n // ls, _step)])
    steps, lin = _hy_conv_steps(n_ctx)
    u_p, hy_w_out0 = _hy_conv(vx, x0, 0, n_ctx, coef_p, hy_bias0, lp, casts=[_Cast(hy_w_out, 0, steps, lin)])
    (u_s,) = _hy_conv(vx, x0, n_ctx, n_lat, coef_s, hy_bias0, ls)
    steps = n // POST_TOKENS
    x1, ffn_w_in1, ffn_w_out1, na_w_out0, w_qkv_t = _post(
        (xp, xs), (u_p, u_s), n, n_ctx, ls, mods, 0, hy_w_out0, ffn_w_in0, ffn_w_out0, *ln, name="post0",
        casts=[_Cast(ffn_w_in, 1, steps, _step), _Cast(ffn_w_out, 1, steps, _step),
               _Cast(na_w_out, 0, steps, _step),
               _Cast(na_w_qkv, 0, steps // 2, lambda i: i // 2, transpose=True)])

    qkvt, new_kt, new_vt = _qkv(x1, n_ctx, lp, ls, mods, 1, w_qkv_t)
    at_p = _ctx_attn(qkvt, n_ctx, lp)
    at_s = _lat_attn(qkvt, n_ctx, jnp.swapaxes(cache_k, 3, 4), jnp.swapaxes(cache_v, 3, 4), na_rpb[0], 0)
    ffn1 = (na_w_out0, ffn_w_in1, ffn_w_out1, *ln)
    (yp,) = _post((x1,), (at_p,), n_ctx, n_ctx, ls, mods, 1, *ffn1, name="post1_ctx", a_feature_major=True,
                  tm=POST_TOKENS_NO_CASTS)
    (ys,) = _post((x1,), (at_s,), n_lat, 0, ls, mods, 1, *ffn1, name="post1_lat", x_tok0=n_ctx,
                  a_feature_major=True, tm=POST_TOKENS_NO_CASTS)

    return (yp.reshape(nbp, lp, d), ys.reshape(nbs, ls, d),
            jnp.swapaxes(new_kt, 3, 4), jnp.swapaxes(new_vt, 3, 4))
```

```python
import functools
import math

import jax
import jax.numpy as jnp
import numpy as np
from jax import lax
from jax.experimental import pallas as pl
from jax.experimental.pallas import tpu as pltpu

D_MODEL = 1024
N_HEADS = 16
HEAD_DIM = D_MODEL // N_HEADS
D_FF = 2816
GRID_W = 64
WIN_H = 8
WIN_W = 16
N_BANDS = 16
PE_DIM = 1 + 2 * N_BANDS
FILT_W = 64
MOD_SHIFT = 0.05
DEPTH = 2
ALPHA = (2 * DEPTH) ** 0.25
LN_EPS = 1e-5
MASK_VALUE = -1e30
MOD_ROWS = 8

LANES = 128
SUBLANES = 8
BF16_ROWS = 2 * SUBLANES
CTX_ATTN_LOOKAHEAD = 6
CTX_ATTN_SEQS = 4
LAT_ATTN_LOOKAHEAD = 2
LOG2E = math.log2(math.e)
VMEM_LIMIT = 56 * 1024 * 1024

ADA_COLS = 1536
FILTER_CHANNELS = 256
HY_IN_COLS = 256
HY_CONV_TOKENS = 1024
HY_CONV_CHANNELS = 512
HY_CONV_LOOKAHEAD = 2
POST_TOKENS = 512
POST_GROUP_ROWS = 256
FFN_COLS = 256
QKV_TOKENS = 1024

F32 = jnp.float32
BF16 = jnp.bfloat16


def _bf(x):
    return x.astype(BF16)


def _dot(a, b):
    return jnp.dot(a, b, preferred_element_type=F32)


def _dot_nt(a, b):
    return lax.dot_general(a, b, (((1,), (1,)), ((), ())), preferred_element_type=F32)


def _dot_tn(a, b):
    return lax.dot_general(a, b, (((0,), (0,)), ((), ())), preferred_element_type=F32)


def _dot3(a, b):
    a_hi = _bf(a)
    a_lo = _bf(a - a_hi.astype(F32))
    b_hi = _bf(b)
    b_lo = _bf(b - b_hi.astype(F32))
    m = a.shape[0]
    both = _dot(jnp.concatenate([a_hi, a_lo], axis=0), b_hi)
    return both[0:m] + both[m:2 * m] + _dot(a_hi, b_lo)


def _layer_norm(x, g, b):
    mu = jnp.mean(x, axis=-1, keepdims=True)
    xc = x - mu
    var = jnp.mean(xc * xc, axis=-1, keepdims=True)
    return xc * lax.rsqrt(var + LN_EPS) * g + b


def _params(*sem):
    return pltpu.CompilerParams(dimension_semantics=sem, vmem_limit_bytes=VMEM_LIMIT)


class _Cast:
    def __init__(self, w, layer, steps, step_fn, transpose=False):
        _, r, c = w.shape
        assert r % (steps * BF16_ROWS) == 0
        rows = r // steps
        self.w = w
        self.transpose = transpose
        self.in_spec = pl.BlockSpec((None, rows, c), lambda *g: (layer, step_fn(*g), 0))
        if transpose:
            self.out_shape = jax.ShapeDtypeStruct((c, r), BF16)
            self.out_spec = pl.BlockSpec((c, rows), lambda *g: (0, step_fn(*g)))
        else:
            self.out_shape = jax.ShapeDtypeStruct((r, c), BF16)
            self.out_spec = pl.BlockSpec((rows, c), lambda *g: (step_fn(*g), 0))


def _with_casts(body, n_in, n_out, casts):
    nc = len(casts)

    def kernel_fn(*refs):
        ins, src = refs[:n_in], refs[n_in:n_in + nc]
        outs, dst = refs[n_in + nc:n_in + nc + n_out], refs[n_in + nc + n_out:n_in + 2 * nc + n_out]
        for cast, s, d in zip(casts, src, dst):
            d[...] = _bf(s[...].T if cast.transpose else s[...])
        body(*ins, *outs, *refs[n_in + 2 * nc + n_out:])

    return kernel_fn


def _step(i):
    return i


def _ctx_block(i, ctx_steps):
    return jnp.minimum(i, ctx_steps - 1)


def _lat_block(i, ctx_steps):
    return jnp.maximum(i - ctx_steps, 0)


def _mod_row(i, ctx_steps, tm, lat_len):
    return jnp.where(i < ctx_steps, 0, 1 + ((i - ctx_steps) * tm) // lat_len)


def _pick(is_ctx, refs, idx):
    if len(refs) == 1:
        return refs[0][idx]
    return jnp.where(is_ctx, refs[0][idx], refs[1][idx])


@functools.lru_cache(maxsize=None)
def _plain_dft_tables(L):
    n = 2 * L
    f = np.arange(L)[:, None]
    s = np.arange(L)[None, :]
    ang = 2.0 * np.pi * ((f * s) % n) / n
    c = np.cos(ang)
    sn = np.sin(ang)
    sn[0, :] = np.where(np.arange(L) % 2 == 0, 1.0, -1.0)
    fwd = np.concatenate([c, sn], axis=0)
    w = np.full((2 * L, 1), 2.0 / n)
    w[0, 0] = 1.0 / n
    w[L, 0] = 1.0 / n
    inv = (fwd * w).T
    alt = np.where(np.arange(L) % 2 == 0, 1.0, -1.0).astype(np.float32)[:, None]
    return (np.ascontiguousarray(fwd).astype(np.float32), np.ascontiguousarray(inv).astype(np.float32), alt)


@functools.lru_cache(maxsize=None)
def _dft_tables(L):
    n = 2 * L
    half = L // 2
    f = np.arange(half)[:, None]
    m = np.arange(half)[None, :]
    alt = np.where(np.arange(half) % 2 == 0, 1.0, -1.0)
    w = np.full((L, 1), 2.0 / n)
    w[0, 0] = 1.0 / n
    w[half, 0] = 1.0 / n
    fwd, inv = [], []
    for parity in range(2):
        ang = 2.0 * np.pi * ((f * (2 * m + parity)) % n) / n
        sn = np.sin(ang)
        sn[0, :] = alt
        table = np.concatenate([np.cos(ang), sn], axis=0)
        fwd.append(np.ascontiguousarray(table).astype(np.float32))
        inv.append(np.ascontiguousarray((table * w).T).astype(np.float32))
    return fwd, inv


@functools.lru_cache(maxsize=None)
def _filter_dft_tables(L):
    n = 2 * L
    f = np.arange(L // 2)[:, None]
    t = np.arange(L)[None, :]
    ang = 2.0 * np.pi * ((f * t) % n) / n
    table = np.concatenate([np.cos(ang), np.sin(ang)], axis=0).astype(np.float32)
    tt = np.arange(L)
    cols = np.stack([np.where(tt % 2 == 0, 1.0, -1.0), np.cos(np.pi * (tt % 4) / 2.0),
                     np.sin(np.pi * (tt % 4) / 2.0)], axis=1)
    return np.ascontiguousarray(table), np.round(cols).astype(np.float32)


@functools.lru_cache(maxsize=None)
def _filter_features(L):
    t = np.linspace(0.0, 1.0, L, dtype=np.float32)[:, None]
    w = (np.float32(2.0 * math.pi / L) * np.arange(L, dtype=np.float32))[:, None]
    bands = np.linspace(1e-4, N_BANDS - 1, N_BANDS, dtype=np.float32)[None, :]
    arg = (bands * w).astype(np.float64)
    feats = np.concatenate([t.astype(np.float64), np.cos(arg), np.sin(arg)], axis=-1)
    out = np.zeros((L, LANES), np.float32)
    out[:, :PE_DIM] = feats
    return out, t


HY_SPLIT_MIN_LEN = 512


def _filter_coef_shape(L):
    return (8, L // 2) if L >= HY_SPLIT_MIN_LEN else (3, L)


def _ada_tile(layer, cctx_ref, c_ref, w_ref, b_ref, o_ref, cvec_ref):
    nb = c_ref.shape[0]
    cvec_ref[...] = jnp.zeros_like(cvec_ref)
    cvec_ref[0:1, :] = cctx_ref[...]
    cvec_ref[1:1 + nb, :] = c_ref[...]
    c = cvec_ref[...]
    o_ref[...] = _dot3(c * jax.nn.sigmoid(c), w_ref[...]) + b_ref[pl.ds(layer, 1), :]


def _filter_tile(is_first, feats_ref, t_ref, cols_ref, tab_ref, w1_ref, b1_ref, w2_ref, b2_ref, freq_ref,
                 w3f_ref, w3b_ref, decf_ref, decb_ref, out_ref, h_ref, w1p_ref, *, L):
    hi = lax.Precision.HIGHEST

    @pl.when(is_first)
    def _():
        w1p_ref[...] = jnp.zeros_like(w1p_ref)
        w1p_ref[0:PE_DIM, :] = w1_ref[...]
        h1 = jnp.sin(freq_ref[0:1, :] * (jnp.dot(feats_ref[...], w1p_ref[...], precision=hi,
                                                 preferred_element_type=F32) + b1_ref[...]))
        h_ref[...] = jnp.sin(freq_ref[1:2, :] * (jnp.dot(h1, w2_ref[...], precision=hi,
                                                         preferred_element_type=F32) + b2_ref[...]))

    h = h_ref[...]
    t = t_ref[...]
    kf = _dot3(h, w3f_ref[...]) * (jnp.exp(-t * jnp.abs(decf_ref[...])) + MOD_SHIFT)
    kb = _dot3(h, w3b_ref[...]) * (jnp.exp(-t * jnp.abs(decb_ref[...])) + MOD_SHIFT)
    row = lax.broadcasted_iota(jnp.int32, kf.shape, 0)
    kb = jnp.where(row == 0, 0.0, kb)
    ksum = kf + kb
    kdiff = kf - kb
    alt = cols_ref[:, 0:1]

    def spectrum(table, x):
        hi = _bf(x)
        return _dot(table, hi) + _dot(table, _bf(x - hi.astype(F32)))

    if L < HY_SPLIT_MIN_LEN:
        k_c = spectrum(tab_ref[0:L, :], ksum)
        k_s = spectrum(tab_ref[L:2 * L, :], kdiff)
        k_nyq = jnp.sum(ksum * alt, axis=0, keepdims=True)
        out_ref[0] = k_c
        out_ref[1] = jnp.where(row == 0, 0.0, k_s)
        out_ref[2] = jnp.where(row == 0, k_nyq, k_c)
        return

    half = L // 2
    cos_m = tab_ref[0:half, :]
    sin_m = tab_ref[half:L, :]
    k_c = spectrum(cos_m, ksum)
    k_s = spectrum(sin_m, kdiff)
    u_c = spectrum(cos_m, alt * ksum)
    u_s = -spectrum(sin_m, alt * kdiff)
    kc_h = jnp.sum(ksum * cols_ref[:, 1:2], axis=0, keepdims=True)
    ks_h = jnp.sum(kdiff * cols_ref[:, 2:3], axis=0, keepdims=True)
    first = lax.broadcasted_iota(jnp.int32, k_c.shape, 0) == 0
    out_ref[0] = k_c
    out_ref[1] = k_s
    out_ref[2] = u_c
    out_ref[3] = u_s
    out_ref[4] = jnp.where(first, kc_h - ks_h, k_c)
    out_ref[5] = jnp.where(first, kc_h + ks_h, u_c)
    out_ref[6] = jnp.where(first, kc_h + ks_h, k_c)
    out_ref[7] = jnp.where(first, kc_h - ks_h, u_c)


def _prep_kernel(cctx_ref, c_ref, w_ref, b_ref, fa_ref, ta_ref, ca_ref, tba_ref, fb_ref, tb_ref, cb_ref,
                 tbb_ref, w1_ref, b1_ref, w2_ref, b2_ref, freq_ref, w3f_ref, w3b_ref, decf_ref, decb_ref,
                 mod_ref, coefa_ref, coefb_ref, cvec_ref, ha_ref, hb_ref, w1p_ref, *, lens, tiles):
    s = pl.program_id(0)
    _ada_tile(s // tiles, cctx_ref, c_ref, w_ref, b_ref, mod_ref, cvec_ref)
    shared = (w1_ref, b1_ref, w2_ref, b2_ref, freq_ref, w3f_ref, w3b_ref, decf_ref, decb_ref)

    @pl.when(s < tiles)
    def _():
        _filter_tile(s == 0, fa_ref, ta_ref, ca_ref, tba_ref, *shared, coefa_ref, ha_ref, w1p_ref, L=lens[0])

    @pl.when(s >= tiles)
    def _():
        _filter_tile(s == tiles, fb_ref, tb_ref, cb_ref, tbb_ref, *shared, coefb_ref, hb_ref, w1p_ref, L=lens[1])


def _prep(c_ctx, c, ada_w, ada_b, lens, pe_w1, pe_b1, pe_w2, pe_b2, pe_w3, sin_freq, decay, casts=()):
    assert 1 + c.shape[0] <= MOD_ROWS and DEPTH == 2
    tn = ADA_COLS
    dt = FILTER_CHANNELS
    tiles = D_MODEL // dt
    assert 6 * D_MODEL // tn == tiles
    full = lambda shape: pl.BlockSpec(shape, lambda s: (0,) * len(shape))
    consts, const_specs, out_shapes, out_specs, scratch = [], [], [], [], []
    for k, L in enumerate(lens):
        feats, t = _filter_features(L)
        if L >= HY_SPLIT_MIN_LEN:
            table, cols = _filter_dft_tables(L)
        else:
            table, _, cols = _plain_dft_tables(L)
        consts += [feats, t, cols, _bf(jnp.asarray(table))]
        const_specs += [full(feats.shape), full(t.shape), full(cols.shape), full(table.shape)]
        nc, nf = _filter_coef_shape(L)
        out_shapes.append(jax.ShapeDtypeStruct((nc, nf, D_MODEL), F32))
        block = (lambda s: (0, 0, jnp.minimum(s, tiles - 1))) if k == 0 else (
            lambda s: (0, 0, jnp.maximum(s - tiles, 0)))
        out_specs.append(pl.BlockSpec((nc, nf, dt), block))
        scratch.append(pltpu.VMEM((L, FILT_W), F32))
    b1 = pe_b1.reshape(1, FILT_W)
    b2 = pe_b2.reshape(1, FILT_W)
    dec = decay.reshape(1, 2 * D_MODEL)
    scratch.append(pltpu.VMEM((LANES, FILT_W), F32))
    return pl.pallas_call(
        _with_casts(functools.partial(_prep_kernel, lens=tuple(lens), tiles=tiles), 21, 3, casts),
        out_shape=(jax.ShapeDtypeStruct((DEPTH, MOD_ROWS, 6 * D_MODEL), F32), *out_shapes,
                   *[k.out_shape for k in casts]),
        grid=(DEPTH * tiles,),
        in_specs=[
            full((1, D_MODEL)), full(c.shape),
            pl.BlockSpec((None, D_MODEL, tn), lambda s: (s // tiles, 0, s % tiles)),
            pl.BlockSpec((DEPTH, tn), lambda s: (0, s % tiles)),
            *const_specs,
            full((PE_DIM, FILT_W)), full((1, FILT_W)), full((FILT_W, FILT_W)), full((1, FILT_W)),
            full((2, FILT_W)),
            pl.BlockSpec((FILT_W, dt), lambda s: (0, s % tiles)),
            pl.BlockSpec((FILT_W, dt), lambda s: (0, tiles + s % tiles)),
            pl.BlockSpec((1, dt), lambda s: (0, s % tiles)),
            pl.BlockSpec((1, dt), lambda s: (0, tiles + s % tiles)),
            *[k.in_spec for k in casts],
        ],
        out_specs=(pl.BlockSpec((None, MOD_ROWS, tn), lambda s: (s // tiles, 0, s % tiles)), *out_specs,
                   *[k.out_spec for k in casts]),
        scratch_shapes=[pltpu.VMEM((MOD_ROWS, D_MODEL), F32), *scratch],
        compiler_params=_params("arbitrary"),
        name="prep",
    )(c_ctx.reshape(1, D_MODEL), c, ada_w, ada_b, *consts, pe_w1, b1, pe_w2, b2, sin_freq, pe_w3, pe_w3, dec, dec,
      *[k.w for k in casts])


def _hy_in_kernel(xc_ref, xl_ref, mod_ref, w_ref, cw_ref, cb_ref, x0_ref, vx_ref, *, ctx_steps, ctx_len, cn):
    i = pl.program_id(0)
    is_ctx = i < ctx_steps
    tm = x0_ref.shape[0]
    mod = mod_ref[pl.ds(_mod_row(i, ctx_steps, tm, tm), 1), :]
    x = _pick(is_ctx, (xc_ref, xl_ref), (slice(None), slice(None)))
    h = _bf(x * (1.0 + mod[:, D_MODEL:2 * D_MODEL]) + mod[:, 0:D_MODEL])
    row = lax.broadcasted_iota(jnp.int32, (SUBLANES, cn), 0)

    def conv(col):
        z = _dot(h, w_ref[:, col:col + cn])
        zm = pltpu.roll(z, 1, 0)
        zp = pltpu.roll(z, tm - 1, 0)
        w0 = cw_ref[0:1, col:col + cn]
        w1 = cw_ref[1:2, col:col + cn]
        w2 = cw_ref[2:3, col:col + cn]
        b = cb_ref[:, col:col + cn]
        out = zm * w0 + z * w1 + zp * w2 + b
        pieces = []
        for s in range(tm // ctx_len):
            lo = s * ctx_len
            hi = lo + ctx_len - SUBLANES
            head = (jnp.where(row == 0, 0.0, zm[lo:lo + SUBLANES]) * w0 + z[lo:lo + SUBLANES] * w1
                    + zp[lo:lo + SUBLANES] * w2 + b)
            tail = (zm[hi:hi + SUBLANES] * w0 + z[hi:hi + SUBLANES] * w1
                    + jnp.where(row == SUBLANES - 1, 0.0, zp[hi:hi + SUBLANES]) * w2 + b)
            if lo > 0:
                head = jnp.where(is_ctx, head, out[lo:lo + SUBLANES])
            if hi + SUBLANES < tm:
                tail = jnp.where(is_ctx, tail, out[hi:hi + SUBLANES])
            pieces += [head, out[lo + SUBLANES:hi], tail]
        return jnp.concatenate(pieces, axis=0)

    for c in range(D_MODEL // cn):
        x0_ref[:, c * cn:(c + 1) * cn] = _bf(conv(c * cn))
        x1 = conv(D_MODEL + c * cn)
        v = conv(2 * D_MODEL + c * cn)
        vx_ref[:, c * cn:(c + 1) * cn] = _bf(v * x1)


def _hy_in(x_ctx, x_lat, ctx_len, lat_len, mods, layer, w_in, sconv_w, sconv_b, casts=()):
    tm = lat_len
    assert tm % ctx_len == 0 and x_ctx.shape[0] % tm == 0
    ctx_steps = x_ctx.shape[0] // tm
    n = x_ctx.shape[0] + x_lat.shape[0]
    tok = pl.BlockSpec((tm, D_MODEL), lambda i: (i, 0))
    const = lambda shape: pl.BlockSpec(shape, lambda i: (0, 0), pipeline_mode=pl.Buffered(1))
    act = jax.ShapeDtypeStruct((n, D_MODEL), BF16)
    body = functools.partial(_hy_in_kernel, ctx_steps=ctx_steps, ctx_len=ctx_len, cn=HY_IN_COLS)
    return pl.pallas_call(
        _with_casts(body, 6, 2, casts),
        out_shape=(act, act) + tuple(c.out_shape for c in casts),
        grid=(n // tm,),
        in_specs=[
            pl.BlockSpec((tm, D_MODEL), lambda i: (_ctx_block(i, ctx_steps), 0)),
            pl.BlockSpec((tm, D_MODEL), lambda i: (_lat_block(i, ctx_steps), 0)),
            pl.BlockSpec((None, MOD_ROWS, 6 * D_MODEL), lambda i: (layer, 0, 0)),
            const((D_MODEL, 3 * D_MODEL)),
            pl.BlockSpec((3, 3 * D_MODEL), lambda i: (0, 0)),
            pl.BlockSpec((1, 3 * D_MODEL), lambda i: (0, 0)),
        ] + [c.in_spec for c in casts],
        out_specs=(tok, tok) + tuple(c.out_spec for c in casts),
        compiler_params=_params("arbitrary"),
        name="hyena_in",
    )(x_ctx, x_lat, mods, w_in, sconv_w, sconv_b, *[c.w for c in casts])


def _hy_conv_plain_kernel(vx_ref, x0_ref, coef_ref, bias_ref, fwd_ref, inv_ref, u_ref, *, L, nseq, cn):
    pieces = [(c, s) for c in range(vx_ref.shape[1] // cn) for s in range(nseq)]

    def forward(piece):
        c, s = piece
        vx = vx_ref[s * L:(s + 1) * L, c * cn:(c + 1) * cn]
        return vx, _dot(fwd_ref[...], vx)

    pending = [forward(p) for p in pieces[:HY_CONV_LOOKAHEAD]]
    for i, (c, s) in enumerate(pieces):
        if i + HY_CONV_LOOKAHEAD < len(pieces):
            pending.append(forward(pieces[i + HY_CONV_LOOKAHEAD]))
        vx, spec = pending.pop(0)
        rows = slice(s * L, (s + 1) * L)
        cols = slice(c * cn, (c + 1) * cn)
        k_cos = coef_ref[0, :, cols]
        k_sin = coef_ref[1, :, cols]
        k_mix = coef_ref[2, :, cols]
        v_cos = spec[0:L]
        v_sin = spec[L:2 * L]
        y_spec = _bf(jnp.concatenate([v_cos * k_cos - v_sin * k_sin, v_cos * k_sin + v_sin * k_mix], axis=0))
        y = _dot(inv_ref[...], y_spec)
        u = (y + bias_ref[:, cols] * vx.astype(F32)) * x0_ref[rows, cols].astype(F32)
        u_ref[rows, cols] = _bf(u)


def _hy_conv_split_kernel(vx_ref, x0_ref, coef_ref, bias_ref, fe_ref, fo_ref, ge_ref, go_ref, u_ref,
                          sv_ref, sx_ref, su_ref, *, L, nseq, cn):
    half = L // 2
    for g in range(vx_ref.shape[1] // LANES):
        lanes = slice(g * LANES, (g + 1) * LANES)
        sv_ref[g] = vx_ref[:, lanes].astype(F32)
        sx_ref[g] = x0_ref[:, lanes].astype(F32)

    def load(ref, start, c):
        return jnp.concatenate([ref[g, pl.ds(start, half, stride=2), :]
                                for g in range(c * cn // LANES, (c + 1) * cn // LANES)], axis=1)

    def store(ref, start, c, val):
        for k, g in enumerate(range(c * cn // LANES, (c + 1) * cn // LANES)):
            ref[g, pl.ds(start, half, stride=2), :] = val[:, k * LANES:(k + 1) * LANES]

    pieces = [(c, s) for c in range(vx_ref.shape[1] // cn) for s in range(nseq)]

    def forward(piece):
        c, s = piece
        v_e = load(sv_ref, s * L, c)
        v_o = load(sv_ref, s * L + 1, c)
        return v_e, v_o, _dot(fe_ref[...], _bf(v_e)), _dot(fo_ref[...], _bf(v_o))

    pending = [forward(p) for p in pieces[:HY_CONV_LOOKAHEAD]]
    for i, (c, s) in enumerate(pieces):
        if i + HY_CONV_LOOKAHEAD < len(pieces):
            pending.append(forward(pieces[i + HY_CONV_LOOKAHEAD]))
        v_e, v_o, a, b = pending.pop(0)
        cols = slice(c * cn, (c + 1) * cn)
        k_c, k_s, u_c, u_s, e2, e3, f2, f3 = [coef_ref[k, :, cols] for k in range(coef_ref.shape[0])]
        s_c, s_s = a[0:half] + b[0:half], a[half:L] + b[half:L]
        d_c, d_s = a[0:half] - b[0:half], a[half:L] - b[half:L]
        t1 = s_c * k_c - s_s * k_s
        t2 = d_c * u_c + d_s * u_s
        q1 = s_c * k_s
        q4 = d_c * u_s
        p_spec = jnp.concatenate([t1 + t2, q1 + s_s * e2 + d_s * e3 - q4], axis=0)
        m_spec = jnp.concatenate([t1 - t2, q1 + s_s * f2 - d_s * f3 + q4], axis=0)
        y_e = _dot(ge_ref[...], _bf(p_spec))
        y_o = _dot(go_ref[...], _bf(m_spec))
        bias = bias_ref[:, cols]
        store(su_ref, s * L, c, (y_e + bias * v_e) * load(sx_ref, s * L, c))
        store(su_ref, s * L + 1, c, (y_o + bias * v_o) * load(sx_ref, s * L + 1, c))
        rows = slice(s * L, (s + 1) * L)
        for g in range(c * cn // LANES, (c + 1) * cn // LANES):
            u_ref[rows, g * LANES:(g + 1) * LANES] = _bf(su_ref[g, rows, :])


def _hy_conv_steps(n):
    nt = n // HY_CONV_TOKENS
    return (D_MODEL // HY_CONV_CHANNELS) * nt, lambda j, i: j * nt + i


def _hy_conv(vx, x0, tok0, n, coef, bias, L, casts=()):
    tm = HY_CONV_TOKENS
    dt = HY_CONV_CHANNELS
    assert tok0 % tm == 0 and n % tm == 0 and tm % L == 0
    src = pl.BlockSpec((tm, dt), lambda j, i: (tok0 // tm + i, j))
    dst = pl.BlockSpec((tm, dt), lambda j, i: (i, j))
    const = lambda shape: pl.BlockSpec(shape, lambda j, i: (0, 0), pipeline_mode=pl.Buffered(1))
    if L >= HY_SPLIT_MIN_LEN:
        fwd, inv = _dft_tables(L)
        tables = (*fwd, *inv)
        body = functools.partial(_hy_conv_split_kernel, L=L, nseq=tm // L, cn=256)
        scratch = [pltpu.VMEM((dt // LANES, tm, LANES), F32)] * 3
    else:
        tables = _plain_dft_tables(L)[:2]
        body = functools.partial(_hy_conv_plain_kernel, L=L, nseq=tm // L, cn=256)
        scratch = []
    return pl.pallas_call(
        _with_casts(body, 4 + len(tables), 1, casts),
        out_shape=(jax.ShapeDtypeStruct((n, D_MODEL), BF16),) + tuple(c.out_shape for c in casts),
        grid=(D_MODEL // dt, n // tm),
        in_specs=[
            src, src,
            pl.BlockSpec(_filter_coef_shape(L) + (dt,), lambda j, i: (0, 0, j)),
            pl.BlockSpec((1, dt), lambda j, i: (0, j)),
        ] + [const(t.shape) for t in tables] + [c.in_spec for c in casts],
        out_specs=(dst,) + tuple(c.out_spec for c in casts),
        scratch_shapes=scratch,
        compiler_params=_params("arbitrary", "arbitrary"),
        name=f"hyena_conv_{L}",
    )(vx, x0, coef, bias, *[_bf(jnp.asarray(t)) for t in tables], *[c.w for c in casts])


def _post_kernel(*refs, layer, ctx_steps, lat_len, n_x, n_a, a_feature_major):
    x_refs, a_refs = refs[:n_x], refs[n_x:n_x + n_a]
    (mod_ref, wo_ref, ln1g_ref, ln1b_ref, win_ref, wout_ref, ln2g_ref, ln2b_ref, o_ref,
     act_ref) = refs[n_x + n_a:]
    d = D_MODEL
    i = pl.program_id(0)
    is_ctx = i < ctx_steps
    tm = act_ref.shape[0]
    mod = mod_ref[pl.ds(_mod_row(i, ctx_steps, tm, lat_len), 1), :]
    g1 = mod[:, 2 * d:3 * d]
    sh2 = mod[:, 3 * d:4 * d]
    sc2 = mod[:, 4 * d:5 * d]
    g2 = mod[:, 5 * d:6 * d]
    ln1 = (ln1g_ref[layer:layer + 1, :], ln1b_ref[layer:layer + 1, :])
    ln2 = (ln2g_ref[layer:layer + 1, :], ln2b_ref[layer:layer + 1, :])

    groups = [slice(r, r + POST_GROUP_ROWS) for r in range(0, tm, POST_GROUP_ROWS)]
    if a_feature_major:
        mix = [_dot_tn(_pick(is_ctx, a_refs, (slice(None), rows)), wo_ref[...]) for rows in groups]
    else:
        mix = [_dot(_pick(is_ctx, a_refs, (rows, slice(None))), wo_ref[...]) for rows in groups]
    x1 = []
    for rows, mx in zip(groups, mix):
        x = _pick(is_ctx, x_refs, (rows, slice(None)))
        x1.append(_layer_norm(ALPHA * x + g1 * mx, *ln1))
        h = _bf(x1[-1] * (1.0 + sc2) + sh2)
        for c in range(D_FF // FFN_COLS):
            cols = slice(c * FFN_COLS, (c + 1) * FFN_COLS)
            gate = _dot(h, win_ref[:, cols])
            up = _dot(h, win_ref[:, D_FF + c * FFN_COLS:D_FF + (c + 1) * FFN_COLS])
            act_ref[rows, cols] = _bf(gate * jax.nn.sigmoid(gate) * up)
    for rows, xr in zip(groups, x1):
        f = _dot(act_ref[rows, :], wout_ref[...])
        o_ref[rows, :] = _layer_norm(ALPHA * xr + g2 * f, *ln2)


def _post(xs, mixes, n, n_ctx, lat_len, mods, layer, w_o, w_in, w_out, ln1_g, ln1_b, ln2_g, ln2_b, name,
          x_tok0=0, a_feature_major=False, casts=(), tm=POST_TOKENS):
    ctx_steps = n_ctx // tm
    ctx = lambda i: _ctx_block(i, ctx_steps)
    lat = lambda i: _lat_block(i, ctx_steps)
    blocks = lambda m, tok0: (lambda i: tok0 // tm + i,) if m == 1 else (ctx, lat)
    tok = lambda f: pl.BlockSpec((tm, D_MODEL), lambda i: (f(i), 0))
    feat = lambda f: pl.BlockSpec((D_MODEL, tm), lambda i: (0, f(i)))
    const = lambda shape: pl.BlockSpec(shape, lambda i: (0, 0), pipeline_mode=pl.Buffered(1))
    vec = pl.BlockSpec((DEPTH, D_MODEL), lambda i: (0, 0))
    body = functools.partial(_post_kernel, layer=layer, ctx_steps=ctx_steps, lat_len=lat_len, n_x=len(xs),
                             n_a=len(mixes), a_feature_major=a_feature_major)
    return pl.pallas_call(
        _with_casts(body, len(xs) + len(mixes) + 8, 1, casts),
        out_shape=(jax.ShapeDtypeStruct((n, D_MODEL), F32),) + tuple(c.out_shape for c in casts),
        grid=(n // tm,),
        in_specs=[tok(f) for f in blocks(len(xs), x_tok0)]
        + [(feat if a_feature_major else tok)(f) for f in blocks(len(mixes), 0)]
        + [pl.BlockSpec((None, MOD_ROWS, 6 * D_MODEL), lambda i: (layer, 0, 0)),
           const((D_MODEL, D_MODEL)), vec, vec,
           const((D_MODEL, 2 * D_FF)), const((D_FF, D_MODEL)), vec, vec]
        + [c.in_spec for c in casts],
        out_specs=(tok(lambda i: i),) + tuple(c.out_spec for c in casts),
        scratch_shapes=[pltpu.VMEM((tm, D_FF), BF16)],
        compiler_params=_params("arbitrary"),
        name=name,
    )(*xs, *mixes, mods, w_o, ln1_g, ln1_b, w_in, w_out, ln2_g, ln2_b, *[c.w for c in casts])


def _qkv_kernel(x_ref, mod_ref, wt_ref, qkvt_ref, nk_ref, nv_ref, *, ctx_steps, lat_len):
    d = D_MODEL
    i = pl.program_id(0)
    tm = x_ref.shape[0]
    mod = mod_ref[pl.ds(_mod_row(i, ctx_steps, tm, lat_len), 1), :]
    h = _bf(x_ref[...] * (1.0 + mod[:, d:2 * d]) + mod[:, 0:d])
    qkvt_ref[0:d, :] = _bf(_dot_nt(wt_ref[0:d, :], h) * (HEAD_DIM ** -0.5 * LOG2E))
    kt = _dot_nt(wt_ref[d:2 * d, :], h)
    qkvt_ref[d:2 * d, :] = _bf(kt)
    vt = _dot_nt(wt_ref[2 * d:3 * d, :], h)
    qkvt_ref[2 * d:3 * d, :] = _bf(vt)

    @pl.when(i < ctx_steps)
    def _():
        seq = nk_ref.shape[4]
        for s in range(tm // seq):
            nk_ref[s, 0] = kt[:, s * seq:(s + 1) * seq].reshape(N_HEADS, HEAD_DIM, seq)
            nv_ref[s, 0] = vt[:, s * seq:(s + 1) * seq].reshape(N_HEADS, HEAD_DIM, seq)


def _qkv(x2d, n_ctx, ctx_len, lat_len, mods, layer, w_qkv_t):
    n = x2d.shape[0]
    tm = QKV_TOKENS
    assert tm % ctx_len == 0 and n_ctx % tm == 0
    ctx_steps = n_ctx // tm
    kv_shape = jax.ShapeDtypeStruct((n_ctx // ctx_len, 1, N_HEADS, HEAD_DIM, ctx_len), F32)
    kv_spec = pl.BlockSpec((tm // ctx_len, 1, N_HEADS, HEAD_DIM, ctx_len),
                           lambda i: (_ctx_block(i, ctx_steps), 0, 0, 0, 0))
    return pl.pallas_call(
        functools.partial(_qkv_kernel, ctx_steps=ctx_steps, lat_len=lat_len),
        out_shape=(jax.ShapeDtypeStruct((3 * D_MODEL, n), BF16), kv_shape, kv_shape),
        grid=(n // tm,),
        in_specs=[
            pl.BlockSpec((tm, D_MODEL), lambda i: (i, 0)),
            pl.BlockSpec((None, MOD_ROWS, 6 * D_MODEL), lambda i: (layer, 0, 0)),
            pl.BlockSpec((3 * D_MODEL, D_MODEL), lambda i: (0, 0), pipeline_mode=pl.Buffered(1)),
        ],
        out_specs=(pl.BlockSpec((3 * D_MODEL, tm), lambda i: (0, i)), kv_spec, kv_spec),
        compiler_params=_params("arbitrary"),
        name="qkv",
    )(x2d, mods, w_qkv_t)


def _softmax_weights(*scores):
    m = functools.reduce(jnp.maximum, [jnp.max(s, axis=0, keepdims=True) for s in scores])
    return [_bf(jnp.exp2(s - m)) for s in scores]


def _weighted_values(vt, p):
    ones = jnp.ones((BF16_ROWS, vt.shape[1]), BF16)
    return _dot(jnp.concatenate([vt, ones], axis=0), p)


def _normalise(acc):
    return _bf(acc[0:HEAD_DIM] * (1.0 / acc[HEAD_DIM:HEAD_DIM + 1]))


def _ctx_attn_kernel(qt_ref, kt_ref, vt_ref, ot_ref, *, seq):
    units = [(s, hh) for s in range(qt_ref.shape[1] // seq) for hh in range(N_HEADS)]

    def block(unit):
        s, hh = unit
        return slice(hh * HEAD_DIM, (hh + 1) * HEAD_DIM), slice(s * seq, (s + 1) * seq)

    def scores(unit):
        return _dot_tn(kt_ref[block(unit)], qt_ref[block(unit)])

    pending = [scores(u) for u in units[:CTX_ATTN_LOOKAHEAD]]
    for i, unit in enumerate(units):
        if i + CTX_ATTN_LOOKAHEAD < len(units):
            pending.append(scores(units[i + CTX_ATTN_LOOKAHEAD]))
        (p,) = _softmax_weights(pending.pop(0))
        ot_ref[block(unit)] = _normalise(_weighted_values(vt_ref[block(unit)], p))


def _ctx_attn(qkvt, n_ctx, seq):
    tm = CTX_ATTN_SEQS * seq
    return pl.pallas_call(
        functools.partial(_ctx_attn_kernel, seq=seq),
        out_shape=jax.ShapeDtypeStruct((D_MODEL, n_ctx), BF16),
        grid=(n_ctx // tm,),
        in_specs=[pl.BlockSpec((D_MODEL, tm), lambda b, c=c: (c, b)) for c in range(3)],
        out_specs=pl.BlockSpec((D_MODEL, tm), lambda b: (0, b)),
        compiler_params=_params("arbitrary"),
        name="ctx_attn",
    )(qkvt, qkvt, qkvt)


LAT_ROWS = 16
Q_BLOCK_ROWS = 4
KEY_TILE_ROWS = 4


def _key_band(qb):
    row0 = [min(max(r - WIN_H // 2, 0), LAT_ROWS - WIN_H) for r in range(qb * Q_BLOCK_ROWS, (qb + 1) * Q_BLOCK_ROWS)]
    lo = min(row0) // KEY_TILE_ROWS * KEY_TILE_ROWS
    hi = -(-(max(row0) + WIN_H) // KEY_TILE_ROWS) * KEY_TILE_ROWS
    return lo, hi


KEY_BANDS = tuple(_key_band(qb) for qb in range(LAT_ROWS // Q_BLOCK_ROWS))
MAX_BAND_ROWS = max(hi - lo for lo, hi in KEY_BANDS)


def _fill_bias_table(rpb_ref, tab_ref):
    shape = (GRID_W, LANES)
    kc = lax.broadcasted_iota(jnp.int32, shape, 0)
    lane = lax.broadcasted_iota(jnp.int32, shape, 1)
    qc = lane % GRID_W
    win0 = jnp.clip(qc - WIN_W // 2, 0, GRID_W - WIN_W)
    col_ok = (kc >= win0) & (kc < win0 + WIN_W)
    masked = jnp.full(shape, MASK_VALUE, F32)
    tiles = []
    for dr in range(2 * WIN_H - 1):
        row = jnp.broadcast_to(rpb_ref[dr:dr + 1, :] * LOG2E, shape)
        t = pltpu.roll(row, LANES - (WIN_W - 1), 1, stride=1, stride_axis=0)
        t = jnp.where(lane < GRID_W, t, pltpu.roll(t, GRID_W, 1))
        tiles.append(jnp.where(col_ok, t, MASK_VALUE))

    def tile(qr, kr):
        row0 = min(max(qr - WIN_H // 2, 0), LAT_ROWS - WIN_H)
        if row0 <= kr < row0 + WIN_H:
            return tiles[kr - qr + WIN_H - 1]
        return masked

    for qr in range(0, LAT_ROWS, 2):
        lo, hi = KEY_BANDS[qr // Q_BLOCK_ROWS]
        for i in range(hi - lo):
            both = jnp.where(lane < GRID_W, tile(qr, lo + i), tile(qr + 1, lo + i))
            tab_ref[i * GRID_W:(i + 1) * GRID_W, qr * GRID_W:(qr + 2) * GRID_W] = both


def _lat_attn_kernel(qt_ref, kt_ref, vt_ref, ckt_ref, cvt_ref, rpb_ref, ot_ref, tab_ref):
    seq = LAT_ROWS * GRID_W
    nq = Q_BLOCK_ROWS * GRID_W
    heads = qt_ref.shape[0] // HEAD_DIM
    units = [(b, hh, qb) for b in range(qt_ref.shape[1] // seq) for hh in range(heads)
             for qb in range(len(KEY_BANDS))]

    def rows(hh):
        return slice(hh * HEAD_DIM, (hh + 1) * HEAD_DIM)

    def keys(b, qb):
        lo, hi = KEY_BANDS[qb]
        return slice(b * seq + lo * GRID_W, b * seq + hi * GRID_W)

    def queries(b, qb):
        return slice(b * seq + qb * nq, b * seq + (qb + 1) * nq)

    def scores(unit):
        b, hh, qb = unit
        lo, hi = KEY_BANDS[qb]
        qt = qt_ref[rows(hh), queries(b, qb)]
        s_loc = (_dot_tn(kt_ref[rows(hh), keys(b, qb)], qt)
                 + tab_ref[hh, 0:(hi - lo) * GRID_W, qb * nq:(qb + 1) * nq])
        s_ctx = _dot_tn(_bf(ckt_ref[b, hh]), qt)
        return s_loc, s_ctx

    for hh in range(tab_ref.shape[0]):
        _fill_bias_table(rpb_ref.at[hh], tab_ref.at[hh])
    pending = [scores(u) for u in units[:LAT_ATTN_LOOKAHEAD]]
    for i, (b, hh, qb) in enumerate(units):
        if i + LAT_ATTN_LOOKAHEAD < len(units):
            pending.append(scores(units[i + LAT_ATTN_LOOKAHEAD]))
        s_loc, s_ctx = pending.pop(0)
        p_loc, p_ctx = _softmax_weights(s_loc, s_ctx)
        acc = (_weighted_values(vt_ref[rows(hh), keys(b, qb)], p_loc)
               + _weighted_values(_bf(cvt_ref[b, hh]), p_ctx))
        ot_ref[rows(hh), queries(b, qb)] = _normalise(acc)


def _lat_attn(qkvt, n_ctx, cache_kt, cache_vt, rpb, layer_idx):
    rpb = jnp.pad(rpb[:, :, ::-1], ((0, 0), (0, 1), (0, LANES - (2 * WIN_W - 1))))
    n = qkvt.shape[1] - n_ctx
    seq = LAT_ROWS * GRID_W
    assert n_ctx % n == 0 and n % seq == 0
    nb = n // seq
    nhp = N_HEADS // 2
    past = cache_kt.shape[4]
    feat = lambda c: pl.BlockSpec((2 * HEAD_DIM, n), lambda hp, c=c: (c * nhp + hp, n_ctx // n))
    cache = pl.BlockSpec((nb, None, 2, HEAD_DIM, past), lambda hp: (0, layer_idx, hp, 0, 0))
    return pl.pallas_call(
        _lat_attn_kernel,
        out_shape=jax.ShapeDtypeStruct((D_MODEL, n), BF16),
        grid=(nhp,),
        in_specs=[feat(0), feat(1), feat(2), cache, cache,
                  pl.BlockSpec((2, 2 * WIN_H, LANES), lambda hp: (hp, 0, 0))],
        out_specs=pl.BlockSpec((2 * HEAD_DIM, n), lambda hp: (hp, 0)),
        scratch_shapes=[pltpu.VMEM((2, MAX_BAND_ROWS * GRID_W, seq), F32)],
        compiler_params=_params("arbitrary"),
        name="lat_attn",
    )(qkvt, qkvt, qkvt, cache_kt, cache_vt, rpb)


def kernel(x_prompt, x_sample, cache_k, cache_v, c, c_ctx, ada_w, ada_b, ln1_g, ln1_b, ln2_g, ln2_b, ffn_w_in, ffn_w_out, hy_w_in, hy_sconv_w, hy_sconv_b, hy_pe_w1, hy_pe_b1, hy_pe_w2, hy_pe_b2, hy_pe_w3, hy_sin_freq, hy_decay, hy_bias, hy_w_out, na_w_qkv, na_rpb, na_w_out):
    nbp, lp, d = x_prompt.shape
    nbs, ls, _ = x_sample.shape
    assert d == D_MODEL and ls == LAT_ROWS * GRID_W and c.shape[0] == nbs
    xp = x_prompt.reshape(nbp * lp, d)
    xs = x_sample.reshape(nbs * ls, d)
    n_ctx, n_lat = xp.shape[0], xs.shape[0]
    n = n_ctx + n_lat
    ln = (ln1_g, ln1_b, ln2_g, ln2_b)

    filt = (hy_pe_w1[0], hy_pe_b1[0], hy_pe_w2[0], hy_pe_b2[0], hy_pe_w3[0], hy_sin_freq[0], hy_decay[0])
    steps = DEPTH * D_MODEL // FILTER_CHANNELS
    mods, coef_p, coef_s, hy_w_in0 = _prep(c_ctx, c, ada_w, ada_b, (lp, ls), *filt,
                                           casts=[_Cast(hy_w_in, 0, steps, _step)])

    sconv_b = hy_sconv_b[0].reshape(1, -1)
    hy_bias0 = hy_bias[0].reshape(1, -1)
    x0, vx, ffn_w_in0, ffn_w_out0 = _hy_in(xp, xs, lp, ls, mods, 0, hy_w_in0, hy_sconv_w[0], sconv_b,
                                           casts=[_Cast(ffn_w_in, 0, n // ls, _step),
                                                  _Cast(ffn_w_out, 0, n // ls, _step)])
    steps, lin = _hy_conv_steps(n_ctx)
    u_p, hy_w_out0 = _hy_conv(vx, x0, 0, n_ctx, coef_p, hy_bias0, lp, casts=[_Cast(hy_w_out, 0, steps, lin)])
    (u_s,) = _hy_conv(vx, x0, n_ctx, n_lat, coef_s, hy_bias0, ls)
    steps = n // POST_TOKENS
    x1, ffn_w_in1, ffn_w_out1, na_w_out0, w_qkv_t = _post(
        (xp, xs), (u_p, u_s), n, n_ctx, ls, mods, 0, hy_w_out0, ffn_w_in0, ffn_w_out0, *ln, name="post0",
        casts=[_Cast(ffn_w_in, 1, steps, _step), _Cast(ffn_w_out, 1, steps, _step),
               _Cast(na_w_out, 0, steps, _step),
               _Cast(na_w_qkv, 0, steps // 2, lambda i: i // 2, transpose=True)])

    qkvt, new_kt, new_vt = _qkv(x1, n_ctx, lp, ls, mods, 1, w_qkv_t)
    at_p = _ctx_attn(qkvt, n_ctx, lp)
    at_s = _lat_attn(qkvt, n_ctx, jnp.swapaxes(cache_k, 3, 4), jnp.swapaxes(cache_v, 3, 4), na_rpb[0], 0)
    ffn1 = (na_w_out0, ffn_w_in1, ffn_w_out1, *ln)
    (yp,) = _post((x1,), (at_p,), n_ctx, n_ctx, ls, mods, 1, *ffn1, name="post1_ctx", a_feature_major=True)
    (ys,) = _post((x1,), (at_s,), n_lat, 0, ls, mods, 1, *ffn1, name="post1_lat", x_tok0=n_ctx,
                  a_feature_major=True)

    return (yp.reshape(nbp, lp, d), ys.reshape(nbs, ls, d),
            jnp.swapaxes(new_kt, 3, 4), jnp.swapaxes(new_vt, 3, 4))
```

```python
import functools
import math

import jax
import jax.numpy as jnp
import numpy as np
from jax import lax
from jax.experimental import pallas as pl
from jax.experimental.pallas import tpu as pltpu

D_MODEL = 1024
N_HEADS = 16
HEAD_DIM = D_MODEL // N_HEADS
D_FF = 2816
GRID_W = 64
WIN_H = 8
WIN_W = 16
N_BANDS = 16
PE_DIM = 1 + 2 * N_BANDS
FILT_W = 64
MOD_SHIFT = 0.05
DEPTH = 2
ALPHA = (2 * DEPTH) ** 0.25
LN_EPS = 1e-5
MASK_VALUE = -1e30
MOD_ROWS = 8

LANES = 128
SUBLANES = 8
BF16_ROWS = 2 * SUBLANES
CTX_ATTN_LOOKAHEAD = 6
CTX_ATTN_SEQS = 4
LAT_ATTN_HEADS = 4
LAT_ATTN_LOOKAHEAD = 2
LOG2E = math.log2(math.e)
VMEM_LIMIT = 56 * 1024 * 1024

ADA_COLS = 1536
FILTER_CHANNELS = 256
HY_IN_COLS = 256
HY_CONV_TOKENS = 1024
HY_CONV_CHANNELS = 512
HY_CONV_LOOKAHEAD = 2
POST_TOKENS = 512
POST_GROUP_ROWS = 256
FFN_COLS = 256
QKV_TOKENS = 1024

F32 = jnp.float32
BF16 = jnp.bfloat16


def _bf(x):
    return x.astype(BF16)


def _dot(a, b):
    return jnp.dot(a, b, preferred_element_type=F32)


def _dot_nt(a, b):
    return lax.dot_general(a, b, (((1,), (1,)), ((), ())), preferred_element_type=F32)


def _dot_tn(a, b):
    return lax.dot_general(a, b, (((0,), (0,)), ((), ())), preferred_element_type=F32)


def _dot3(a, b):
    a_hi = _bf(a)
    a_lo = _bf(a - a_hi.astype(F32))
    b_hi = _bf(b)
    b_lo = _bf(b - b_hi.astype(F32))
    m = a.shape[0]
    both = _dot(jnp.concatenate([a_hi, a_lo], axis=0), b_hi)
    return both[0:m] + both[m:2 * m] + _dot(a_hi, b_lo)


def _layer_norm(x, g, b):
    mu = jnp.mean(x, axis=-1, keepdims=True)
    xc = x - mu
    var = jnp.mean(xc * xc, axis=-1, keepdims=True)
    return xc * lax.rsqrt(var + LN_EPS) * g + b


def _params(*sem):
    return pltpu.CompilerParams(dimension_semantics=sem, vmem_limit_bytes=VMEM_LIMIT)


class _Cast:
    def __init__(self, w, layer, steps, step_fn, transpose=False):
        _, r, c = w.shape
        assert r % (steps * BF16_ROWS) == 0
        rows = r // steps
        self.w = w
        self.transpose = transpose
        self.in_spec = pl.BlockSpec((None, rows, c), lambda *g: (layer, step_fn(*g), 0))
        if transpose:
            self.out_shape = jax.ShapeDtypeStruct((c, r), BF16)
            self.out_spec = pl.BlockSpec((c, rows), lambda *g: (0, step_fn(*g)))
        else:
            self.out_shape = jax.ShapeDtypeStruct((r, c), BF16)
            self.out_spec = pl.BlockSpec((rows, c), lambda *g: (step_fn(*g), 0))


def _with_casts(body, n_in, n_out, casts):
    nc = len(casts)

    def kernel_fn(*refs):
        ins, src = refs[:n_in], refs[n_in:n_in + nc]
        outs, dst = refs[n_in + nc:n_in + nc + n_out], refs[n_in + nc + n_out:n_in + 2 * nc + n_out]
        for cast, s, d in zip(casts, src, dst):
            d[...] = _bf(s[...].T if cast.transpose else s[...])
        body(*ins, *outs, *refs[n_in + 2 * nc + n_out:])

    return kernel_fn


def _step(i):
    return i


def _ctx_block(i, ctx_steps):
    return jnp.minimum(i, ctx_steps - 1)


def _lat_block(i, ctx_steps):
    return jnp.maximum(i - ctx_steps, 0)


def _mod_row(i, ctx_steps, tm, lat_len):
    return jnp.where(i < ctx_steps, 0, 1 + ((i - ctx_steps) * tm) // lat_len)


def _pick(is_ctx, refs, idx):
    if len(refs) == 1:
        return refs[0][idx]
    return jnp.where(is_ctx, refs[0][idx], refs[1][idx])


@functools.lru_cache(maxsize=None)
def _plain_dft_tables(L):
    n = 2 * L
    f = np.arange(L)[:, None]
    s = np.arange(L)[None, :]
    ang = 2.0 * np.pi * ((f * s) % n) / n
    c = np.cos(ang)
    sn = np.sin(ang)
    sn[0, :] = np.where(np.arange(L) % 2 == 0, 1.0, -1.0)
    fwd = np.concatenate([c, sn], axis=0)
    w = np.full((2 * L, 1), 2.0 / n)
    w[0, 0] = 1.0 / n
    w[L, 0] = 1.0 / n
    inv = (fwd * w).T
    alt = np.where(np.arange(L) % 2 == 0, 1.0, -1.0).astype(np.float32)[:, None]
    return (np.ascontiguousarray(fwd).astype(np.float32), np.ascontiguousarray(inv).astype(np.float32), alt)


@functools.lru_cache(maxsize=None)
def _dft_tables(L):
    n = 2 * L
    half = L // 2
    f = np.arange(half)[:, None]
    m = np.arange(half)[None, :]
    alt = np.where(np.arange(half) % 2 == 0, 1.0, -1.0)
    w = np.full((L, 1), 2.0 / n)
    w[0, 0] = 1.0 / n
    w[half, 0] = 1.0 / n
    fwd, inv = [], []
    for parity in range(2):
        ang = 2.0 * np.pi * ((f * (2 * m + parity)) % n) / n
        sn = np.sin(ang)
        sn[0, :] = alt
        table = np.concatenate([np.cos(ang), sn], axis=0)
        fwd.append(np.ascontiguousarray(table).astype(np.float32))
        inv.append(np.ascontiguousarray((table * w).T).astype(np.float32))
    return fwd, inv


@functools.lru_cache(maxsize=None)
def _filter_dft_tables(L):
    n = 2 * L
    f = np.arange(L // 2)[:, None]
    t = np.arange(L)[None, :]
    ang = 2.0 * np.pi * ((f * t) % n) / n
    table = np.concatenate([np.cos(ang), np.sin(ang)], axis=0).astype(np.float32)
    tt = np.arange(L)
    cols = np.stack([np.where(tt % 2 == 0, 1.0, -1.0), np.cos(np.pi * (tt % 4) / 2.0),
                     np.sin(np.pi * (tt % 4) / 2.0)], axis=1)
    return np.ascontiguousarray(table), np.round(cols).astype(np.float32)


@functools.lru_cache(maxsize=None)
def _filter_features(L):
    t = np.linspace(0.0, 1.0, L, dtype=np.float32)[:, None]
    w = (np.float32(2.0 * math.pi / L) * np.arange(L, dtype=np.float32))[:, None]
    bands = np.linspace(1e-4, N_BANDS - 1, N_BANDS, dtype=np.float32)[None, :]
    arg = (bands * w).astype(np.float64)
    feats = np.concatenate([t.astype(np.float64), np.cos(arg), np.sin(arg)], axis=-1)
    out = np.zeros((L, LANES), np.float32)
    out[:, :PE_DIM] = feats
    return out, t


HY_SPLIT_MIN_LEN = 512


def _filter_coef_shape(L):
    return (8, L // 2) if L >= HY_SPLIT_MIN_LEN else (3, L)


def _ada_tile(layer, cctx_ref, c_ref, w_ref, b_ref, o_ref, cvec_ref):
    nb = c_ref.shape[0]
    cvec_ref[...] = jnp.zeros_like(cvec_ref)
    cvec_ref[0:1, :] = cctx_ref[...]
    cvec_ref[1:1 + nb, :] = c_ref[...]
    c = cvec_ref[...]
    o_ref[...] = _dot3(c * jax.nn.sigmoid(c), w_ref[...]) + b_ref[pl.ds(layer, 1), :]


def _filter_tile(is_first, feats_ref, t_ref, cols_ref, tab_ref, w1_ref, b1_ref, w2_ref, b2_ref, freq_ref,
                 w3f_ref, w3b_ref, decf_ref, decb_ref, out_ref, h_ref, w1p_ref, *, L):
    hi = lax.Precision.HIGHEST

    @pl.when(is_first)
    def _():
        w1p_ref[...] = jnp.zeros_like(w1p_ref)
        w1p_ref[0:PE_DIM, :] = w1_ref[...]
        h1 = jnp.sin(freq_ref[0:1, :] * (jnp.dot(feats_ref[...], w1p_ref[...], precision=hi,
                                                 preferred_element_type=F32) + b1_ref[...]))
        h_ref[...] = jnp.sin(freq_ref[1:2, :] * (jnp.dot(h1, w2_ref[...], precision=hi,
                                                         preferred_element_type=F32) + b2_ref[...]))

    h = h_ref[...]
    t = t_ref[...]
    kf = _dot3(h, w3f_ref[...]) * (jnp.exp(-t * jnp.abs(decf_ref[...])) + MOD_SHIFT)
    kb = _dot3(h, w3b_ref[...]) * (jnp.exp(-t * jnp.abs(decb_ref[...])) + MOD_SHIFT)
    row = lax.broadcasted_iota(jnp.int32, kf.shape, 0)
    kb = jnp.where(row == 0, 0.0, kb)
    ksum = kf + kb
    kdiff = kf - kb
    alt = cols_ref[:, 0:1]

    def spectrum(table, x):
        hi = _bf(x)
        return _dot(table, hi) + _dot(table, _bf(x - hi.astype(F32)))

    if L < HY_SPLIT_MIN_LEN:
        k_c = spectrum(tab_ref[0:L, :], ksum)
        k_s = spectrum(tab_ref[L:2 * L, :], kdiff)
        k_nyq = jnp.sum(ksum * alt, axis=0, keepdims=True)
        out_ref[0] = k_c
        out_ref[1] = jnp.where(row == 0, 0.0, k_s)
        out_ref[2] = jnp.where(row == 0, k_nyq, k_c)
        return

    half = L // 2
    cos_m = tab_ref[0:half, :]
    sin_m = tab_ref[half:L, :]
    k_c = spectrum(cos_m, ksum)
    k_s = spectrum(sin_m, kdiff)
    u_c = spectrum(cos_m, alt * ksum)
    u_s = -spectrum(sin_m, alt * kdiff)
    kc_h = jnp.sum(ksum * cols_ref[:, 1:2], axis=0, keepdims=True)
    ks_h = jnp.sum(kdiff * cols_ref[:, 2:3], axis=0, keepdims=True)
    first = lax.broadcasted_iota(jnp.int32, k_c.shape, 0) == 0
    out_ref[0] = k_c
    out_ref[1] = k_s
    out_ref[2] = u_c
    out_ref[3] = u_s
    out_ref[4] = jnp.where(first, kc_h - ks_h, k_c)
    out_ref[5] = jnp.where(first, kc_h + ks_h, u_c)
    out_ref[6] = jnp.where(first, kc_h + ks_h, k_c)
    out_ref[7] = jnp.where(first, kc_h - ks_h, u_c)


def _prep_kernel(cctx_ref, c_ref, w_ref, b_ref, fa_ref, ta_ref, ca_ref, tba_ref, fb_ref, tb_ref, cb_ref,
                 tbb_ref, w1_ref, b1_ref, w2_ref, b2_ref, freq_ref, w3f_ref, w3b_ref, decf_ref, decb_ref,
                 mod_ref, coefa_ref, coefb_ref, cvec_ref, ha_ref, hb_ref, w1p_ref, *, lens, tiles):
    s = pl.program_id(0)
    _ada_tile(s // tiles, cctx_ref, c_ref, w_ref, b_ref, mod_ref, cvec_ref)
    shared = (w1_ref, b1_ref, w2_ref, b2_ref, freq_ref, w3f_ref, w3b_ref, decf_ref, decb_ref)

    @pl.when(s < tiles)
    def _():
        _filter_tile(s == 0, fa_ref, ta_ref, ca_ref, tba_ref, *shared, coefa_ref, ha_ref, w1p_ref, L=lens[0])

    @pl.when(s >= tiles)
    def _():
        _filter_tile(s == tiles, fb_ref, tb_ref, cb_ref, tbb_ref, *shared, coefb_ref, hb_ref, w1p_ref, L=lens[1])


def _prep(c_ctx, c, ada_w, ada_b, lens, pe_w1, pe_b1, pe_w2, pe_b2, pe_w3, sin_freq, decay, casts=()):
    assert 1 + c.shape[0] <= MOD_ROWS and DEPTH == 2
    tn = ADA_COLS
    dt = FILTER_CHANNELS
    tiles = D_MODEL // dt
    assert 6 * D_MODEL // tn == tiles
    full = lambda shape: pl.BlockSpec(shape, lambda s: (0,) * len(shape))
    consts, const_specs, out_shapes, out_specs, scratch = [], [], [], [], []
    for k, L in enumerate(lens):
        feats, t = _filter_features(L)
        if L >= HY_SPLIT_MIN_LEN:
            table, cols = _filter_dft_tables(L)
        else:
            table, _, cols = _plain_dft_tables(L)
        consts += [feats, t, cols, _bf(jnp.asarray(table))]
        const_specs += [full(feats.shape), full(t.shape), full(cols.shape), full(table.shape)]
        nc, nf = _filter_coef_shape(L)
        out_shapes.append(jax.ShapeDtypeStruct((nc, nf, D_MODEL), F32))
        block = (lambda s: (0, 0, jnp.minimum(s, tiles - 1))) if k == 0 else (
            lambda s: (0, 0, jnp.maximum(s - tiles, 0)))
        out_specs.append(pl.BlockSpec((nc, nf, dt), block))
        scratch.append(pltpu.VMEM((L, FILT_W), F32))
    b1 = pe_b1.reshape(1, FILT_W)
    b2 = pe_b2.reshape(1, FILT_W)
    dec = decay.reshape(1, 2 * D_MODEL)
    scratch.append(pltpu.VMEM((LANES, FILT_W), F32))
    return pl.pallas_call(
        _with_casts(functools.partial(_prep_kernel, lens=tuple(lens), tiles=tiles), 21, 3, casts),
        out_shape=(jax.ShapeDtypeStruct((DEPTH, MOD_ROWS, 6 * D_MODEL), F32), *out_shapes,
                   *[k.out_shape for k in casts]),
        grid=(DEPTH * tiles,),
        in_specs=[
            full((1, D_MODEL)), full(c.shape),
            pl.BlockSpec((None, D_MODEL, tn), lambda s: (s // tiles, 0, s % tiles)),
            pl.BlockSpec((DEPTH, tn), lambda s: (0, s % tiles)),
            *const_specs,
            full((PE_DIM, FILT_W)), full((1, FILT_W)), full((FILT_W, FILT_W)), full((1, FILT_W)),
            full((2, FILT_W)),
            pl.BlockSpec((FILT_W, dt), lambda s: (0, s % tiles)),
            pl.BlockSpec((FILT_W, dt), lambda s: (0, tiles + s % tiles)),
            pl.BlockSpec((1, dt), lambda s: (0, s % tiles)),
            pl.BlockSpec((1, dt), lambda s: (0, tiles + s % tiles)),
            *[k.in_spec for k in casts],
        ],
        out_specs=(pl.BlockSpec((None, MOD_ROWS, tn), lambda s: (s // tiles, 0, s % tiles)), *out_specs,
                   *[k.out_spec for k in casts]),
        scratch_shapes=[pltpu.VMEM((MOD_ROWS, D_MODEL), F32), *scratch],
        compiler_params=_params("arbitrary"),
        name="prep",
    )(c_ctx.reshape(1, D_MODEL), c, ada_w, ada_b, *consts, pe_w1, b1, pe_w2, b2, sin_freq, pe_w3, pe_w3, dec, dec,
      *[k.w for k in casts])


def _hy_in_kernel(xc_ref, xl_ref, mod_ref, w_ref, cw_ref, cb_ref, x0_ref, vx_ref, *, ctx_steps, ctx_len, cn):
    i = pl.program_id(0)
    is_ctx = i < ctx_steps
    tm = x0_ref.shape[0]
    mod = mod_ref[pl.ds(_mod_row(i, ctx_steps, tm, tm), 1), :]
    x = _pick(is_ctx, (xc_ref, xl_ref), (slice(None), slice(None)))
    h = _bf(x * (1.0 + mod[:, D_MODEL:2 * D_MODEL]) + mod[:, 0:D_MODEL])
    row = lax.broadcasted_iota(jnp.int32, (SUBLANES, cn), 0)

    def conv(col):
        z = _dot(h, w_ref[:, col:col + cn])
        zm = pltpu.roll(z, 1, 0)
        zp = pltpu.roll(z, tm - 1, 0)
        w0 = cw_ref[0:1, col:col + cn]
        w1 = cw_ref[1:2, col:col + cn]
        w2 = cw_ref[2:3, col:col + cn]
        b = cb_ref[:, col:col + cn]
        out = zm * w0 + z * w1 + zp * w2 + b
        pieces = []
        for s in range(tm // ctx_len):
            lo = s * ctx_len
            hi = lo + ctx_len - SUBLANES
            head = (jnp.where(row == 0, 0.0, zm[lo:lo + SUBLANES]) * w0 + z[lo:lo + SUBLANES] * w1
                    + zp[lo:lo + SUBLANES] * w2 + b)
            tail = (zm[hi:hi + SUBLANES] * w0 + z[hi:hi + SUBLANES] * w1
                    + jnp.where(row == SUBLANES - 1, 0.0, zp[hi:hi + SUBLANES]) * w2 + b)
            if lo > 0:
                head = jnp.where(is_ctx, head, out[lo:lo + SUBLANES])
            if hi + SUBLANES < tm:
                tail = jnp.where(is_ctx, tail, out[hi:hi + SUBLANES])
            pieces += [head, out[lo + SUBLANES:hi], tail]
        return jnp.concatenate(pieces, axis=0)

    for c in range(D_MODEL // cn):
        x0_ref[:, c * cn:(c + 1) * cn] = _bf(conv(c * cn))
        x1 = conv(D_MODEL + c * cn)
        v = conv(2 * D_MODEL + c * cn)
        vx_ref[:, c * cn:(c + 1) * cn] = _bf(v * x1)


def _hy_in(x_ctx, x_lat, ctx_len, lat_len, mods, layer, w_in, sconv_w, sconv_b, casts=()):
    tm = lat_len
    assert tm % ctx_len == 0 and x_ctx.shape[0] % tm == 0
    ctx_steps = x_ctx.shape[0] // tm
    n = x_ctx.shape[0] + x_lat.shape[0]
    tok = pl.BlockSpec((tm, D_MODEL), lambda i: (i, 0))
    const = lambda shape: pl.BlockSpec(shape, lambda i: (0, 0), pipeline_mode=pl.Buffered(1))
    act = jax.ShapeDtypeStruct((n, D_MODEL), BF16)
    body = functools.partial(_hy_in_kernel, ctx_steps=ctx_steps, ctx_len=ctx_len, cn=HY_IN_COLS)
    return pl.pallas_call(
        _with_casts(body, 6, 2, casts),
        out_shape=(act, act) + tuple(c.out_shape for c in casts),
        grid=(n // tm,),
        in_specs=[
            pl.BlockSpec((tm, D_MODEL), lambda i: (_ctx_block(i, ctx_steps), 0)),
            pl.BlockSpec((tm, D_MODEL), lambda i: (_lat_block(i, ctx_steps), 0)),
            pl.BlockSpec((None, MOD_ROWS, 6 * D_MODEL), lambda i: (layer, 0, 0)),
            const((D_MODEL, 3 * D_MODEL)),
            pl.BlockSpec((3, 3 * D_MODEL), lambda i: (0, 0)),
            pl.BlockSpec((1, 3 * D_MODEL), lambda i: (0, 0)),
        ] + [c.in_spec for c in casts],
        out_specs=(tok, tok) + tuple(c.out_spec for c in casts),
        compiler_params=_params("arbitrary"),
        name="hyena_in",
    )(x_ctx, x_lat, mods, w_in, sconv_w, sconv_b, *[c.w for c in casts])


def _hy_conv_plain_kernel(vx_ref, x0_ref, coef_ref, bias_ref, fwd_ref, inv_ref, u_ref, *, L, nseq, cn):
    pieces = [(c, s) for c in range(vx_ref.shape[1] // cn) for s in range(nseq)]

    def forward(piece):
        c, s = piece
        vx = vx_ref[s * L:(s + 1) * L, c * cn:(c + 1) * cn]
        return vx, _dot(fwd_ref[...], vx)

    pending = [forward(p) for p in pieces[:HY_CONV_LOOKAHEAD]]
    for i, (c, s) in enumerate(pieces):
        if i + HY_CONV_LOOKAHEAD < len(pieces):
            pending.append(forward(pieces[i + HY_CONV_LOOKAHEAD]))
        vx, spec = pending.pop(0)
        rows = slice(s * L, (s + 1) * L)
        cols = slice(c * cn, (c + 1) * cn)
        k_cos = coef_ref[0, :, cols]
        k_sin = coef_ref[1, :, cols]
        k_mix = coef_ref[2, :, cols]
        v_cos = spec[0:L]
        v_sin = spec[L:2 * L]
        y_spec = _bf(jnp.concatenate([v_cos * k_cos - v_sin * k_sin, v_cos * k_sin + v_sin * k_mix], axis=0))
        y = _dot(inv_ref[...], y_spec)
        u = (y + bias_ref[:, cols] * vx.astype(F32)) * x0_ref[rows, cols].astype(F32)
        u_ref[rows, cols] = _bf(u)


def _hy_conv_split_kernel(vx_ref, x0_ref, coef_ref, bias_ref, fe_ref, fo_ref, ge_ref, go_ref, u_ref,
                          sv_ref, sx_ref, su_ref, *, L, nseq, cn):
    half = L // 2
    for g in range(vx_ref.shape[1] // LANES):
        lanes = slice(g * LANES, (g + 1) * LANES)
        sv_ref[g] = vx_ref[:, lanes].astype(F32)
        sx_ref[g] = x0_ref[:, lanes].astype(F32)

    def load(ref, start, c):
        return jnp.concatenate([ref[g, pl.ds(start, half, stride=2), :]
                                for g in range(c * cn // LANES, (c + 1) * cn // LANES)], axis=1)

    def store(ref, start, c, val):
        for k, g in enumerate(range(c * cn // LANES, (c + 1) * cn // LANES)):
            ref[g, pl.ds(start, half, stride=2), :] = val[:, k * LANES:(k + 1) * LANES]

    pieces = [(c, s) for c in range(vx_ref.shape[1] // cn) for s in range(nseq)]

    def forward(piece):
        c, s = piece
        v_e = load(sv_ref, s * L, c)
        v_o = load(sv_ref, s * L + 1, c)
        return v_e, v_o, _dot(fe_ref[...], _bf(v_e)), _dot(fo_ref[...], _bf(v_o))

    pending = [forward(p) for p in pieces[:HY_CONV_LOOKAHEAD]]
    for i, (c, s) in enumerate(pieces):
        if i + HY_CONV_LOOKAHEAD < len(pieces):
            pending.append(forward(pieces[i + HY_CONV_LOOKAHEAD]))
        v_e, v_o, a, b = pending.pop(0)
        cols = slice(c * cn, (c + 1) * cn)
        k_c, k_s, u_c, u_s, e2, e3, f2, f3 = [coef_ref[k, :, cols] for k in range(coef_ref.shape[0])]
        s_c, s_s = a[0:half] + b[0:half], a[half:L] + b[half:L]
        d_c, d_s = a[0:half] - b[0:half], a[half:L] - b[half:L]
        t1 = s_c * k_c - s_s * k_s
        t2 = d_c * u_c + d_s * u_s
        q1 = s_c * k_s
        q4 = d_c * u_s
        p_spec = jnp.concatenate([t1 + t2, q1 + s_s * e2 + d_s * e3 - q4], axis=0)
        m_spec = jnp.concatenate([t1 - t2, q1 + s_s * f2 - d_s * f3 + q4], axis=0)
        y_e = _dot(ge_ref[...], _bf(p_spec))
        y_o = _dot(go_ref[...], _bf(m_spec))
        bias = bias_ref[:, cols]
        store(su_ref, s * L, c, (y_e + bias * v_e) * load(sx_ref, s * L, c))
        store(su_ref, s * L + 1, c, (y_o + bias * v_o) * load(sx_ref, s * L + 1, c))
        rows = slice(s * L, (s + 1) * L)
        for g in range(c * cn // LANES, (c + 1) * cn // LANES):
            u_ref[rows, g * LANES:(g + 1) * LANES] = _bf(su_ref[g, rows, :])


def _hy_conv_steps(n):
    nt = n // HY_CONV_TOKENS
    return (D_MODEL // HY_CONV_CHANNELS) * nt, lambda j, i: j * nt + i


def _hy_conv(vx, x0, tok0, n, coef, bias, L, casts=()):
    tm = HY_CONV_TOKENS
    dt = HY_CONV_CHANNELS
    assert tok0 % tm == 0 and n % tm == 0 and tm % L == 0
    src = pl.BlockSpec((tm, dt), lambda j, i: (tok0 // tm + i, j))
    dst = pl.BlockSpec((tm, dt), lambda j, i: (i, j))
    const = lambda shape: pl.BlockSpec(shape, lambda j, i: (0, 0), pipeline_mode=pl.Buffered(1))
    if L >= HY_SPLIT_MIN_LEN:
        fwd, inv = _dft_tables(L)
        tables = (*fwd, *inv)
        body = functools.partial(_hy_conv_split_kernel, L=L, nseq=tm // L, cn=256)
        scratch = [pltpu.VMEM((dt // LANES, tm, LANES), F32)] * 3
    else:
        tables = _plain_dft_tables(L)[:2]
        body = functools.partial(_hy_conv_plain_kernel, L=L, nseq=tm // L, cn=256)
        scratch = []
    return pl.pallas_call(
        _with_casts(body, 4 + len(tables), 1, casts),
        out_shape=(jax.ShapeDtypeStruct((n, D_MODEL), BF16),) + tuple(c.out_shape for c in casts),
        grid=(D_MODEL // dt, n // tm),
        in_specs=[
            src, src,
            pl.BlockSpec(_filter_coef_shape(L) + (dt,), lambda j, i: (0, 0, j)),
            pl.BlockSpec((1, dt), lambda j, i: (0, j)),
        ] + [const(t.shape) for t in tables] + [c.in_spec for c in casts],
        out_specs=(dst,) + tuple(c.out_spec for c in casts),
        scratch_shapes=scratch,
        compiler_params=_params("arbitrary", "arbitrary"),
        name=f"hyena_conv_{L}",
    )(vx, x0, coef, bias, *[_bf(jnp.asarray(t)) for t in tables], *[c.w for c in casts])


def _post_kernel(*refs, layer, ctx_steps, lat_len, n_x, n_a, a_feature_major):
    x_refs, a_refs = refs[:n_x], refs[n_x:n_x + n_a]
    (mod_ref, wo_ref, ln1g_ref, ln1b_ref, win_ref, wout_ref, ln2g_ref, ln2b_ref, o_ref,
     act_ref) = refs[n_x + n_a:]
    d = D_MODEL
    i = pl.program_id(0)
    is_ctx = i < ctx_steps
    tm = act_ref.shape[0]
    mod = mod_ref[pl.ds(_mod_row(i, ctx_steps, tm, lat_len), 1), :]
    g1 = mod[:, 2 * d:3 * d]
    sh2 = mod[:, 3 * d:4 * d]
    sc2 = mod[:, 4 * d:5 * d]
    g2 = mod[:, 5 * d:6 * d]
    ln1 = (ln1g_ref[layer:layer + 1, :], ln1b_ref[layer:layer + 1, :])
    ln2 = (ln2g_ref[layer:layer + 1, :], ln2b_ref[layer:layer + 1, :])

    groups = [slice(r, r + POST_GROUP_ROWS) for r in range(0, tm, POST_GROUP_ROWS)]
    if a_feature_major:
        mix = [_dot_tn(_pick(is_ctx, a_refs, (slice(None), rows)), wo_ref[...]) for rows in groups]
    else:
        mix = [_dot(_pick(is_ctx, a_refs, (rows, slice(None))), wo_ref[...]) for rows in groups]
    x1 = []
    for rows, mx in zip(groups, mix):
        x = _pick(is_ctx, x_refs, (rows, slice(None)))
        x1.append(_layer_norm(ALPHA * x + g1 * mx, *ln1))
        h = _bf(x1[-1] * (1.0 + sc2) + sh2)
        for c in range(D_FF // FFN_COLS):
            cols = slice(c * FFN_COLS, (c + 1) * FFN_COLS)
            gate = _dot(h, win_ref[:, cols])
            up = _dot(h, win_ref[:, D_FF + c * FFN_COLS:D_FF + (c + 1) * FFN_COLS])
            act_ref[rows, cols] = _bf(gate * jax.nn.sigmoid(gate) * up)
    for rows, xr in zip(groups, x1):
        f = _dot(act_ref[rows, :], wout_ref[...])
        o_ref[rows, :] = _layer_norm(ALPHA * xr + g2 * f, *ln2)


def _post(xs, mixes, n, n_ctx, lat_len, mods, layer, w_o, w_in, w_out, ln1_g, ln1_b, ln2_g, ln2_b, name,
          x_tok0=0, a_feature_major=False, casts=(), tm=POST_TOKENS):
    ctx_steps = n_ctx // tm
    ctx = lambda i: _ctx_block(i, ctx_steps)
    lat = lambda i: _lat_block(i, ctx_steps)
    blocks = lambda m, tok0: (lambda i: tok0 // tm + i,) if m == 1 else (ctx, lat)
    tok = lambda f: pl.BlockSpec((tm, D_MODEL), lambda i: (f(i), 0))
    feat = lambda f: pl.BlockSpec((D_MODEL, tm), lambda i: (0, f(i)))
    const = lambda shape: pl.BlockSpec(shape, lambda i: (0, 0), pipeline_mode=pl.Buffered(1))
    vec = pl.BlockSpec((DEPTH, D_MODEL), lambda i: (0, 0))
    body = functools.partial(_post_kernel, layer=layer, ctx_steps=ctx_steps, lat_len=lat_len, n_x=len(xs),
                             n_a=len(mixes), a_feature_major=a_feature_major)
    return pl.pallas_call(
        _with_casts(body, len(xs) + len(mixes) + 8, 1, casts),
        out_shape=(jax.ShapeDtypeStruct((n, D_MODEL), F32),) + tuple(c.out_shape for c in casts),
        grid=(n // tm,),
        in_specs=[tok(f) for f in blocks(len(xs), x_tok0)]
        + [(feat if a_feature_major else tok)(f) for f in blocks(len(mixes), 0)]
        + [pl.BlockSpec((None, MOD_ROWS, 6 * D_MODEL), lambda i: (layer, 0, 0)),
           const((D_MODEL, D_MODEL)), vec, vec,
           const((D_MODEL, 2 * D_FF)), const((D_FF, D_MODEL)), vec, vec]
        + [c.in_spec for c in casts],
        out_specs=(tok(lambda i: i),) + tuple(c.out_spec for c in casts),
        scratch_shapes=[pltpu.VMEM((tm, D_FF), BF16)],
        compiler_params=_params("arbitrary"),
        name=name,
    )(*xs, *mixes, mods, w_o, ln1_g, ln1_b, w_in, w_out, ln2_g, ln2_b, *[c.w for c in casts])


def _qkv_kernel(x_ref, mod_ref, wt_ref, qkvt_ref, nk_ref, nv_ref, *, ctx_steps, lat_len):
    d = D_MODEL
    i = pl.program_id(0)
    tm = x_ref.shape[0]
    mod = mod_ref[pl.ds(_mod_row(i, ctx_steps, tm, lat_len), 1), :]
    h = _bf(x_ref[...] * (1.0 + mod[:, d:2 * d]) + mod[:, 0:d])
    qkvt_ref[0:d, :] = _bf(_dot_nt(wt_ref[0:d, :], h) * (HEAD_DIM ** -0.5 * LOG2E))
    kt = _dot_nt(wt_ref[d:2 * d, :], h)
    qkvt_ref[d:2 * d, :] = _bf(kt)
    vt = _dot_nt(wt_ref[2 * d:3 * d, :], h)
    qkvt_ref[2 * d:3 * d, :] = _bf(vt)

    @pl.when(i < ctx_steps)
    def _():
        seq = nk_ref.shape[4]
        for s in range(tm // seq):
            nk_ref[s, 0] = kt[:, s * seq:(s + 1) * seq].reshape(N_HEADS, HEAD_DIM, seq)
            nv_ref[s, 0] = vt[:, s * seq:(s + 1) * seq].reshape(N_HEADS, HEAD_DIM, seq)


def _qkv(x2d, n_ctx, ctx_len, lat_len, mods, layer, w_qkv_t):
    n = x2d.shape[0]
    tm = QKV_TOKENS
    assert tm % ctx_len == 0 and n_ctx % tm == 0
    ctx_steps = n_ctx // tm
    kv_shape = jax.ShapeDtypeStruct((n_ctx // ctx_len, 1, N_HEADS, HEAD_DIM, ctx_len), F32)
    kv_spec = pl.BlockSpec((tm // ctx_len, 1, N_HEADS, HEAD_DIM, ctx_len),
                           lambda i: (_ctx_block(i, ctx_steps), 0, 0, 0, 0))
    return pl.pallas_call(
        functools.partial(_qkv_kernel, ctx_steps=ctx_steps, lat_len=lat_len),
        out_shape=(jax.ShapeDtypeStruct((3 * D_MODEL, n), BF16), kv_shape, kv_shape),
        grid=(n // tm,),
        in_specs=[
            pl.BlockSpec((tm, D_MODEL), lambda i: (i, 0)),
            pl.BlockSpec((None, MOD_ROWS, 6 * D_MODEL), lambda i: (layer, 0, 0)),
            pl.BlockSpec((3 * D_MODEL, D_MODEL), lambda i: (0, 0), pipeline_mode=pl.Buffered(1)),
        ],
        out_specs=(pl.BlockSpec((3 * D_MODEL, tm), lambda i: (0, i)), kv_spec, kv_spec),
        compiler_params=_params("arbitrary"),
        name="qkv",
    )(x2d, mods, w_qkv_t)


def _softmax_weights(*scores):
    m = functools.reduce(jnp.maximum, [jnp.max(s, axis=0, keepdims=True) for s in scores])
    return [_bf(jnp.exp2(s - m)) for s in scores]


def _weighted_values(vt, p):
    ones = jnp.ones((BF16_ROWS, vt.shape[1]), BF16)
    return _dot(jnp.concatenate([vt, ones], axis=0), p)


def _normalise(acc):
    return _bf(acc[0:HEAD_DIM] * (1.0 / acc[HEAD_DIM:HEAD_DIM + 1]))


def _ctx_attn_kernel(qt_ref, kt_ref, vt_ref, ot_ref, *, seq):
    units = [(s, hh) for s in range(qt_ref.shape[1] // seq) for hh in range(N_HEADS)]

    def block(unit):
        s, hh = unit
        return slice(hh * HEAD_DIM, (hh + 1) * HEAD_DIM), slice(s * seq, (s + 1) * seq)

    def scores(unit):
        return _dot_tn(kt_ref[block(unit)], qt_ref[block(unit)])

    pending = [scores(u) for u in units[:CTX_ATTN_LOOKAHEAD]]
    for i, unit in enumerate(units):
        if i + CTX_ATTN_LOOKAHEAD < len(units):
            pending.append(scores(units[i + CTX_ATTN_LOOKAHEAD]))
        (p,) = _softmax_weights(pending.pop(0))
        ot_ref[block(unit)] = _normalise(_weighted_values(vt_ref[block(unit)], p))


def _ctx_attn(qkvt, n_ctx, seq):
    tm = CTX_ATTN_SEQS * seq
    return pl.pallas_call(
        functools.partial(_ctx_attn_kernel, seq=seq),
        out_shape=jax.ShapeDtypeStruct((D_MODEL, n_ctx), BF16),
        grid=(n_ctx // tm,),
        in_specs=[pl.BlockSpec((D_MODEL, tm), lambda b, c=c: (c, b)) for c in range(3)],
        out_specs=pl.BlockSpec((D_MODEL, tm), lambda b: (0, b)),
        compiler_params=_params("arbitrary"),
        name="ctx_attn",
    )(qkvt, qkvt, qkvt)


LAT_ROWS = 16
Q_BLOCK_ROWS = 4
KEY_TILE_ROWS = 4


def _key_band(qb):
    row0 = [min(max(r - WIN_H // 2, 0), LAT_ROWS - WIN_H) for r in range(qb * Q_BLOCK_ROWS, (qb + 1) * Q_BLOCK_ROWS)]
    lo = min(row0) // KEY_TILE_ROWS * KEY_TILE_ROWS
    hi = -(-(max(row0) + WIN_H) // KEY_TILE_ROWS) * KEY_TILE_ROWS
    return lo, hi


KEY_BANDS = tuple(_key_band(qb) for qb in range(LAT_ROWS // Q_BLOCK_ROWS))
MAX_BAND_ROWS = max(hi - lo for lo, hi in KEY_BANDS)


def _fill_bias_table(rpb_ref, tab_ref):
    shape = (GRID_W, LANES)
    kc = lax.broadcasted_iota(jnp.int32, shape, 0)
    lane = lax.broadcasted_iota(jnp.int32, shape, 1)
    qc = lane % GRID_W
    win0 = jnp.clip(qc - WIN_W // 2, 0, GRID_W - WIN_W)
    col_ok = (kc >= win0) & (kc < win0 + WIN_W)
    masked = jnp.full(shape, MASK_VALUE, F32)
    tiles = []
    for dr in range(2 * WIN_H - 1):
        row = jnp.broadcast_to(rpb_ref[dr:dr + 1, :] * LOG2E, shape)
        t = pltpu.roll(row, LANES - (WIN_W - 1), 1, stride=1, stride_axis=0)
        t = jnp.where(lane < GRID_W, t, pltpu.roll(t, GRID_W, 1))
        tiles.append(jnp.where(col_ok, t, MASK_VALUE))

    def tile(qr, kr):
        row0 = min(max(qr - WIN_H // 2, 0), LAT_ROWS - WIN_H)
        if row0 <= kr < row0 + WIN_H:
            return tiles[kr - qr + WIN_H - 1]
        return masked

    for qr in range(0, LAT_ROWS, 2):
        lo, hi = KEY_BANDS[qr // Q_BLOCK_ROWS]
        for i in range(hi - lo):
            both = jnp.where(lane < GRID_W, tile(qr, lo + i), tile(qr + 1, lo + i))
            tab_ref[i * GRID_W:(i + 1) * GRID_W, qr * GRID_W:(qr + 2) * GRID_W] = both


def _lat_attn_kernel(qt_ref, kt_ref, vt_ref, ckt_ref, cvt_ref, rpb_ref, ot_ref, tab_ref):
    seq = LAT_ROWS * GRID_W
    nq = Q_BLOCK_ROWS * GRID_W
    heads = qt_ref.shape[0] // HEAD_DIM
    units = [(b, hh, qb) for b in range(qt_ref.shape[1] // seq) for hh in range(heads)
             for qb in range(len(KEY_BANDS))]

    def rows(hh):
        return slice(hh * HEAD_DIM, (hh + 1) * HEAD_DIM)

    def keys(b, qb):
        lo, hi = KEY_BANDS[qb]
        return slice(b * seq + lo * GRID_W, b * seq + hi * GRID_W)

    def queries(b, qb):
        return slice(b * seq + qb * nq, b * seq + (qb + 1) * nq)

    def scores(unit):
        b, hh, qb = unit
        lo, hi = KEY_BANDS[qb]
        qt = qt_ref[rows(hh), queries(b, qb)]
        s_loc = (_dot_tn(kt_ref[rows(hh), keys(b, qb)], qt)
                 + tab_ref[hh, 0:(hi - lo) * GRID_W, qb * nq:(qb + 1) * nq])
        s_ctx = _dot_tn(_bf(ckt_ref[b, hh]), qt)
        return s_loc, s_ctx

    for hh in range(tab_ref.shape[0]):
        _fill_bias_table(rpb_ref.at[hh], tab_ref.at[hh])
    pending = [scores(u) for u in units[:LAT_ATTN_LOOKAHEAD]]
    for i, (b, hh, qb) in enumerate(units):
        if i + LAT_ATTN_LOOKAHEAD < len(units):
            pending.append(scores(units[i + LAT_ATTN_LOOKAHEAD]))
        s_loc, s_ctx = pending.pop(0)
        p_loc, p_ctx = _softmax_weights(s_loc, s_ctx)
        acc = (_weighted_values(vt_ref[rows(hh), keys(b, qb)], p_loc)
               + _weighted_values(_bf(cvt_ref[b, hh]), p_ctx))
        ot_ref[rows(hh), queries(b, qb)] = _normalise(acc)


def _lat_attn(qkvt, n_ctx, cache_kt, cache_vt, rpb, layer_idx):
    rpb = jnp.pad(rpb[:, :, ::-1], ((0, 0), (0, 1), (0, LANES - (2 * WIN_W - 1))))
    n = qkvt.shape[1] - n_ctx
    seq = LAT_ROWS * GRID_W
    assert n_ctx % n == 0 and n % seq == 0
    nb = n // seq
    hs = LAT_ATTN_HEADS
    nhp = N_HEADS // hs
    past = cache_kt.shape[4]
    feat = lambda c: pl.BlockSpec((hs * HEAD_DIM, n), lambda hp, c=c: (c * nhp + hp, n_ctx // n))
    cache = pl.BlockSpec((nb, None, hs, HEAD_DIM, past), lambda hp: (0, layer_idx, hp, 0, 0))
    return pl.pallas_call(
        _lat_attn_kernel,
        out_shape=jax.ShapeDtypeStruct((D_MODEL, n), BF16),
        grid=(nhp,),
        in_specs=[feat(0), feat(1), feat(2), cache, cache,
                  pl.BlockSpec((hs, 2 * WIN_H, LANES), lambda hp: (hp, 0, 0))],
        out_specs=pl.BlockSpec((hs * HEAD_DIM, n), lambda hp: (hp, 0)),
        scratch_shapes=[pltpu.VMEM((hs, MAX_BAND_ROWS * GRID_W, seq), F32)],
        compiler_params=_params("arbitrary"),
        name="lat_attn",
    )(qkvt, qkvt, qkvt, cache_kt, cache_vt, rpb)


def kernel(x_prompt, x_sample, cache_k, cache_v, c, c_ctx, ada_w, ada_b, ln1_g, ln1_b, ln2_g, ln2_b, ffn_w_in, ffn_w_out, hy_w_in, hy_sconv_w, hy_sconv_b, hy_pe_w1, hy_pe_b1, hy_pe_w2, hy_pe_b2, hy_pe_w3, hy_sin_freq, hy_decay, hy_bias, hy_w_out, na_w_qkv, na_rpb, na_w_out):
    nbp, lp, d = x_prompt.shape
    nbs, ls, _ = x_sample.shape
    assert d == D_MODEL and ls == LAT_ROWS * GRID_W and c.shape[0] == nbs
    xp = x_prompt.reshape(nbp * lp, d)
    xs = x_sample.reshape(nbs * ls, d)
    n_ctx, n_lat = xp.shape[0], xs.shape[0]
    n = n_ctx + n_lat
    ln = (ln1_g, ln1_b, ln2_g, ln2_b)

    filt = (hy_pe_w1[0], hy_pe_b1[0], hy_pe_w2[0], hy_pe_b2[0], hy_pe_w3[0], hy_sin_freq[0], hy_decay[0])
    steps = DEPTH * D_MODEL // FILTER_CHANNELS
    mods, coef_p, coef_s, hy_w_in0 = _prep(c_ctx, c, ada_w, ada_b, (lp, ls), *filt,
                                           casts=[_Cast(hy_w_in, 0, steps, _step)])

    sconv_b = hy_sconv_b[0].reshape(1, -1)
    hy_bias0 = hy_bias[0].reshape(1, -1)
    x0, vx, ffn_w_in0, ffn_w_out0 = _hy_in(xp, xs, lp, ls, mods, 0, hy_w_in0, hy_sconv_w[0], sconv_b,
                                           casts=[_Cast(ffn_w_in, 0, n // ls, _step),
                                                  _Cast(ffn_w_out, 0, n // ls, _step)])
    steps, lin = _hy_conv_steps(n_ctx)
    u_p, hy_w_out0 = _hy_conv(vx, x0, 0, n_ctx, coef_p, hy_bias0, lp, casts=[_Cast(hy_w_out, 0, steps, lin)])
    (u_s,) = _hy_conv(vx, x0, n_ctx, n_lat, coef_s, hy_bias0, ls)
    steps = n // POST_TOKENS
    x1, ffn_w_in1, ffn_w_out1, na_w_out0, w_qkv_t = _post(
        (xp, xs), (u_p, u_s), n, n_ctx, ls, mods, 0, hy_w_out0, ffn_w_in0, ffn_w_out0, *ln, name="post0",
        casts=[_Cast(ffn_w_in, 1, steps, _step), _Cast(ffn_w_out, 1, steps, _step),
               _Cast(na_w_out, 0, steps, _step),
               _Cast(na_w_qkv, 0, steps // 2, lambda i: i // 2, transpose=True)])

    qkvt, new_kt, new_vt = _qkv(x1, n_ctx, lp, ls, mods, 1, w_qkv_t)
    at_p = _ctx_attn(qkvt, n_ctx, lp)
    at_s = _lat_attn(qkvt, n_ctx, jnp.swapaxes(cache_k, 3, 4), jnp.swapaxes(cache_v, 3, 4), na_rpb[0], 0)
    ffn1 = (na_w_out0, ffn_w_in1, ffn_w_out1, *ln)
    (yp,) = _post((x1,), (at_p,), n_ctx, n_ctx, ls, mods, 1, *ffn1, name="post1_ctx", a_feature_major=True)
    (ys,) = _post((x1,), (at_s,), n_lat, 0, ls, mods, 1, *ffn1, name="post1_lat", x_tok0=n_ctx,
                  a_feature_major=True)

    return (yp.reshape(nbp, lp, d), ys.reshape(nbs, ls, d),
            jnp.swapaxes(new_kt, 3, 4), jnp.swapaxes(new_vt, 3, 4))
```

```python
import functools
import math

import jax
import jax.numpy as jnp
import numpy as np
from jax import lax
from jax.experimental import pallas as pl
from jax.experimental.pallas import tpu as pltpu

D_MODEL = 1024
N_HEADS = 16
HEAD_DIM = D_MODEL // N_HEADS
D_FF = 2816
GRID_W = 64
WIN_H = 8
WIN_W = 16
N_BANDS = 16
PE_DIM = 1 + 2 * N_BANDS
FILT_W = 64
MOD_SHIFT = 0.05
DEPTH = 2
ALPHA = (2 * DEPTH) ** 0.25
LN_EPS = 1e-5
MASK_VALUE = -1e30
MOD_ROWS = 8

LANES = 128
SUBLANES = 8
BF16_ROWS = 2 * SUBLANES
CTX_ATTN_LOOKAHEAD = 6
CTX_ATTN_SEQS = 4
LAT_ATTN_HEADS = 4
LAT_ATTN_LOOKAHEAD = 2
LOG2E = math.log2(math.e)
VMEM_LIMIT = 56 * 1024 * 1024

ADA_COLS = 1536
FILTER_CHANNELS = 256
HY_IN_COLS = 256
HY_CONV_TOKENS = 1024
HY_CONV_CHANNELS = 512
HY_CONV_LOOKAHEAD = 2
POST_TOKENS = 512
POST_GROUP_ROWS = 256
FFN_COLS = 256
QKV_TOKENS = 1024

F32 = jnp.float32
BF16 = jnp.bfloat16


def _bf(x):
    return x.astype(BF16)


def _dot(a, b):
    return jnp.dot(a, b, preferred_element_type=F32)


def _dot_nt(a, b):
    return lax.dot_general(a, b, (((1,), (1,)), ((), ())), preferred_element_type=F32)


def _dot_tn(a, b):
    return lax.dot_general(a, b, (((0,), (0,)), ((), ())), preferred_element_type=F32)


def _dot3(a, b):
    a_hi = _bf(a)
    a_lo = _bf(a - a_hi.astype(F32))
    b_hi = _bf(b)
    b_lo = _bf(b - b_hi.astype(F32))
    m = a.shape[0]
    both = _dot(jnp.concatenate([a_hi, a_lo], axis=0), b_hi)
    return both[0:m] + both[m:2 * m] + _dot(a_hi, b_lo)


def _layer_norm(x, g, b):
    mu = jnp.mean(x, axis=-1, keepdims=True)
    xc = x - mu
    var = jnp.mean(xc * xc, axis=-1, keepdims=True)
    return xc * lax.rsqrt(var + LN_EPS) * g + b


def _params(*sem):
    return pltpu.CompilerParams(dimension_semantics=sem, vmem_limit_bytes=VMEM_LIMIT)


class _Cast:
    def __init__(self, w, layer, steps, step_fn, transpose=False):
        _, r, c = w.shape
        assert r % (steps * BF16_ROWS) == 0
        rows = r // steps
        self.w = w
        self.transpose = transpose
        self.in_spec = pl.BlockSpec((None, rows, c), lambda *g: (layer, step_fn(*g), 0))
        if transpose:
            self.out_shape = jax.ShapeDtypeStruct((c, r), BF16)
            self.out_spec = pl.BlockSpec((c, rows), lambda *g: (0, step_fn(*g)))
        else:
            self.out_shape = jax.ShapeDtypeStruct((r, c), BF16)
            self.out_spec = pl.BlockSpec((rows, c), lambda *g: (step_fn(*g), 0))


def _with_casts(body, n_in, n_out, casts):
    nc = len(casts)

    def kernel_fn(*refs):
        ins, src = refs[:n_in], refs[n_in:n_in + nc]
        outs, dst = refs[n_in + nc:n_in + nc + n_out], refs[n_in + nc + n_out:n_in + 2 * nc + n_out]
        for cast, s, d in zip(casts, src, dst):
            d[...] = _bf(s[...].T if cast.transpose else s[...])
        body(*ins, *outs, *refs[n_in + 2 * nc + n_out:])

    return kernel_fn


def _step(i):
    return i


def _ctx_block(i, ctx_steps):
    return jnp.minimum(i, ctx_steps - 1)


def _lat_block(i, ctx_steps):
    return jnp.maximum(i - ctx_steps, 0)


def _mod_row(i, ctx_steps, tm, lat_len):
    return jnp.where(i < ctx_steps, 0, 1 + ((i - ctx_steps) * tm) // lat_len)


def _pick(is_ctx, refs, idx):
    if len(refs) == 1:
        return refs[0][idx]
    return jnp.where(is_ctx, refs[0][idx], refs[1][idx])


@functools.lru_cache(maxsize=None)
def _plain_dft_tables(L):
    n = 2 * L
    f = np.arange(L)[:, None]
    s = np.arange(L)[None, :]
    ang = 2.0 * np.pi * ((f * s) % n) / n
    c = np.cos(ang)
    sn = np.sin(ang)
    sn[0, :] = np.where(np.arange(L) % 2 == 0, 1.0, -1.0)
    fwd = np.concatenate([c, sn], axis=0)
    w = np.full((2 * L, 1), 2.0 / n)
    w[0, 0] = 1.0 / n
    w[L, 0] = 1.0 / n
    inv = (fwd * w).T
    alt = np.where(np.arange(L) % 2 == 0, 1.0, -1.0).astype(np.float32)[:, None]
    return (np.ascontiguousarray(fwd).astype(np.float32), np.ascontiguousarray(inv).astype(np.float32), alt)


@functools.lru_cache(maxsize=None)
def _dft_tables(L):
    n = 2 * L
    half = L // 2
    f = np.arange(half)[:, None]
    m = np.arange(half)[None, :]
    alt = np.where(np.arange(half) % 2 == 0, 1.0, -1.0)
    w = np.full((L, 1), 2.0 / n)
    w[0, 0] = 1.0 / n
    w[half, 0] = 1.0 / n
    fwd, inv = [], []
    for parity in range(2):
        ang = 2.0 * np.pi * ((f * (2 * m + parity)) % n) / n
        sn = np.sin(ang)
        sn[0, :] = alt
        table = np.concatenate([np.cos(ang), sn], axis=0)
        fwd.append(np.ascontiguousarray(table).astype(np.float32))
        inv.append(np.ascontiguousarray((table * w).T).astype(np.float32))
    return fwd, inv


@functools.lru_cache(maxsize=None)
def _filter_dft_tables(L):
    n = 2 * L
    f = np.arange(L // 2)[:, None]
    t = np.arange(L)[None, :]
    ang = 2.0 * np.pi * ((f * t) % n) / n
    table = np.concatenate([np.cos(ang), np.sin(ang)], axis=0).astype(np.float32)
    tt = np.arange(L)
    cols = np.stack([np.where(tt % 2 == 0, 1.0, -1.0), np.cos(np.pi * (tt % 4) / 2.0),
                     np.sin(np.pi * (tt % 4) / 2.0)], axis=1)
    return np.ascontiguousarray(table), np.round(cols).astype(np.float32)


@functools.lru_cache(maxsize=None)
def _filter_features(L):
    t = np.linspace(0.0, 1.0, L, dtype=np.float32)[:, None]
    w = (np.float32(2.0 * math.pi / L) * np.arange(L, dtype=np.float32))[:, None]
    bands = np.linspace(1e-4, N_BANDS - 1, N_BANDS, dtype=np.float32)[None, :]
    arg = (bands * w).astype(np.float64)
    feats = np.concatenate([t.astype(np.float64), np.cos(arg), np.sin(arg)], axis=-1)
    out = np.zeros((L, LANES), np.float32)
    out[:, :PE_DIM] = feats
    return out, t


HY_SPLIT_MIN_LEN = 512


def _filter_coef_shape(L):
    return (8, L // 2) if L >= HY_SPLIT_MIN_LEN else (3, L)


def _ada_tile(layer, cctx_ref, c_ref, w_ref, b_ref, o_ref, cvec_ref):
    nb = c_ref.shape[0]
    cvec_ref[...] = jnp.zeros_like(cvec_ref)
    cvec_ref[0:1, :] = cctx_ref[...]
    cvec_ref[1:1 + nb, :] = c_ref[...]
    c = cvec_ref[...]
    o_ref[...] = _dot3(c * jax.nn.sigmoid(c), w_ref[...]) + b_ref[pl.ds(layer, 1), :]


def _filter_tile(is_first, feats_ref, t_ref, cols_ref, tab_ref, w1_ref, b1_ref, w2_ref, b2_ref, freq_ref,
                 w3f_ref, w3b_ref, decf_ref, decb_ref, out_ref, h_ref, w1p_ref, *, L):
    hi = lax.Precision.HIGHEST

    @pl.when(is_first)
    def _():
        w1p_ref[...] = jnp.zeros_like(w1p_ref)
        w1p_ref[0:PE_DIM, :] = w1_ref[...]
        h1 = jnp.sin(freq_ref[0:1, :] * (jnp.dot(feats_ref[...], w1p_ref[...], precision=hi,
                                                 preferred_element_type=F32) + b1_ref[...]))
        h_ref[...] = jnp.sin(freq_ref[1:2, :] * (jnp.dot(h1, w2_ref[...], precision=hi,
                                                         preferred_element_type=F32) + b2_ref[...]))

    h = h_ref[...]
    t = t_ref[...]
    kf = _dot3(h, w3f_ref[...]) * (jnp.exp(-t * jnp.abs(decf_ref[...])) + MOD_SHIFT)
    kb = _dot3(h, w3b_ref[...]) * (jnp.exp(-t * jnp.abs(decb_ref[...])) + MOD_SHIFT)
    row = lax.broadcasted_iota(jnp.int32, kf.shape, 0)
    kb = jnp.where(row == 0, 0.0, kb)
    ksum = kf + kb
    kdiff = kf - kb
    alt = cols_ref[:, 0:1]

    def spectrum(table, x):
        hi = _bf(x)
        return _dot(table, hi) + _dot(table, _bf(x - hi.astype(F32)))

    if L < HY_SPLIT_MIN_LEN:
        k_c = spectrum(tab_ref[0:L, :], ksum)
        k_s = spectrum(tab_ref[L:2 * L, :], kdiff)
        k_nyq = jnp.sum(ksum * alt, axis=0, keepdims=True)
        out_ref[0] = k_c
        out_ref[1] = jnp.where(row == 0, 0.0, k_s)
        out_ref[2] = jnp.where(row == 0, k_nyq, k_c)
        return

    half = L // 2
    cos_m = tab_ref[0:half, :]
    sin_m = tab_ref[half:L, :]
    k_c = spectrum(cos_m, ksum)
    k_s = spectrum(sin_m, kdiff)
    u_c = spectrum(cos_m, alt * ksum)
    u_s = -spectrum(sin_m, alt * kdiff)
    kc_h = jnp.sum(ksum * cols_ref[:, 1:2], axis=0, keepdims=True)
    ks_h = jnp.sum(kdiff * cols_ref[:, 2:3], axis=0, keepdims=True)
    first = lax.broadcasted_iota(jnp.int32, k_c.shape, 0) == 0
    out_ref[0] = k_c
    out_ref[1] = k_s
    out_ref[2] = u_c
    out_ref[3] = u_s
    out_ref[4] = jnp.where(first, kc_h - ks_h, k_c)
    out_ref[5] = jnp.where(first, kc_h + ks_h, u_c)
    out_ref[6] = jnp.where(first, kc_h + ks_h, k_c)
    out_ref[7] = jnp.where(first, kc_h - ks_h, u_c)


def _prep_kernel(cctx_ref, c_ref, w_ref, b_ref, fa_ref, ta_ref, ca_ref, tba_ref, fb_ref, tb_ref, cb_ref,
                 tbb_ref, w1_ref, b1_ref, w2_ref, b2_ref, freq_ref, w3f_ref, w3b_ref, decf_ref, decb_ref,
                 mod_ref, coefa_ref, coefb_ref, cvec_ref, ha_ref, hb_ref, w1p_ref, *, lens, tiles):
    s = pl.program_id(0)
    _ada_tile(s // tiles, cctx_ref, c_ref, w_ref, b_ref, mod_ref, cvec_ref)
    shared = (w1_ref, b1_ref, w2_ref, b2_ref, freq_ref, w3f_ref, w3b_ref, decf_ref, decb_ref)

    @pl.when(s < tiles)
    def _():
        _filter_tile(s == 0, fa_ref, ta_ref, ca_ref, tba_ref, *shared, coefa_ref, ha_ref, w1p_ref, L=lens[0])

    @pl.when(s >= tiles)
    def _():
        _filter_tile(s == tiles, fb_ref, tb_ref, cb_ref, tbb_ref, *shared, coefb_ref, hb_ref, w1p_ref, L=lens[1])


def _prep(c_ctx, c, ada_w, ada_b, lens, pe_w1, pe_b1, pe_w2, pe_b2, pe_w3, sin_freq, decay, casts=()):
    assert 1 + c.shape[0] <= MOD_ROWS and DEPTH == 2
    tn = ADA_COLS
    dt = FILTER_CHANNELS
    tiles = D_MODEL // dt
    assert 6 * D_MODEL // tn == tiles
    full = lambda shape: pl.BlockSpec(shape, lambda s: (0,) * len(shape))
    consts, const_specs, out_shapes, out_specs, scratch = [], [], [], [], []
    for k, L in enumerate(lens):
        feats, t = _filter_features(L)
        if L >= HY_SPLIT_MIN_LEN:
            table, cols = _filter_dft_tables(L)
        else:
            table, _, cols = _plain_dft_tables(L)
        consts += [feats, t, cols, _bf(jnp.asarray(table))]
        const_specs += [full(feats.shape), full(t.shape), full(cols.shape), full(table.shape)]
        nc, nf = _filter_coef_shape(L)
        out_shapes.append(jax.ShapeDtypeStruct((nc, nf, D_MODEL), F32))
        block = (lambda s: (0, 0, jnp.minimum(s, tiles - 1))) if k == 0 else (
            lambda s: (0, 0, jnp.maximum(s - tiles, 0)))
        out_specs.append(pl.BlockSpec((nc, nf, dt), block))
        scratch.append(pltpu.VMEM((L, FILT_W), F32))
    b1 = pe_b1.reshape(1, FILT_W)
    b2 = pe_b2.reshape(1, FILT_W)
    dec = decay.reshape(1, 2 * D_MODEL)
    scratch.append(pltpu.VMEM((LANES, FILT_W), F32))
    return pl.pallas_call(
        _with_casts(functools.partial(_prep_kernel, lens=tuple(lens), tiles=tiles), 21, 3, casts),
        out_shape=(jax.ShapeDtypeStruct((DEPTH, MOD_ROWS, 6 * D_MODEL), F32), *out_shapes,
                   *[k.out_shape for k in casts]),
        grid=(DEPTH * tiles,),
        in_specs=[
            full((1, D_MODEL)), full(c.shape),
            pl.BlockSpec((None, D_MODEL, tn), lambda s: (s // tiles, 0, s % tiles)),
            pl.BlockSpec((DEPTH, tn), lambda s: (0, s % tiles)),
            *const_specs,
            full((PE_DIM, FILT_W)), full((1, FILT_W)), full((FILT_W, FILT_W)), full((1, FILT_W)),
            full((2, FILT_W)),
            pl.BlockSpec((FILT_W, dt), lambda s: (0, s % tiles)),
            pl.BlockSpec((FILT_W, dt), lambda s: (0, tiles + s % tiles)),
            pl.BlockSpec((1, dt), lambda s: (0, s % tiles)),
            pl.BlockSpec((1, dt), lambda s: (0, tiles + s % tiles)),
            *[k.in_spec for k in casts],
        ],
        out_specs=(pl.BlockSpec((None, MOD_ROWS, tn), lambda s: (s // tiles, 0, s % tiles)), *out_specs,
                   *[k.out_spec for k in casts]),
        scratch_shapes=[pltpu.VMEM((MOD_ROWS, D_MODEL), F32), *scratch],
        compiler_params=_params("arbitrary"),
        name="prep",
    )(c_ctx.reshape(1, D_MODEL), c, ada_w, ada_b, *consts, pe_w1, b1, pe_w2, b2, sin_freq, pe_w3, pe_w3, dec, dec,
      *[k.w for k in casts])


def _conv3(z, cw_ref, cb_ref, cols, seq_len, interior=True):
    n, cn = z.shape
    row = lax.broadcasted_iota(jnp.int32, (SUBLANES, cn), 0)
    zm = pltpu.roll(z, 1, 0)
    zp = pltpu.roll(z, n - 1, 0)
    w0 = cw_ref[0:1, cols]
    w1 = cw_ref[1:2, cols]
    w2 = cw_ref[2:3, cols]
    b = cb_ref[:, cols]
    out = zm * w0 + z * w1 + zp * w2 + b
    pieces = []
    for s in range(n // seq_len):
        lo = s * seq_len
        hi = lo + seq_len - SUBLANES
        head = (jnp.where(row == 0, 0.0, zm[lo:lo + SUBLANES]) * w0 + z[lo:lo + SUBLANES] * w1
                + zp[lo:lo + SUBLANES] * w2 + b)
        tail = (zm[hi:hi + SUBLANES] * w0 + z[hi:hi + SUBLANES] * w1
                + jnp.where(row == SUBLANES - 1, 0.0, zp[hi:hi + SUBLANES]) * w2 + b)
        if lo > 0 and interior is not True:
            head = jnp.where(interior, head, out[lo:lo + SUBLANES])
        if hi + SUBLANES < n and interior is not True:
            tail = jnp.where(interior, tail, out[hi:hi + SUBLANES])
        pieces += [head, out[lo + SUBLANES:hi], tail]
    return jnp.concatenate(pieces, axis=0)


def _hy_in_kernel(xc_ref, xl_ref, mod_ref, w_ref, cw_ref, cb_ref, z0_ref, vx_ref, *, ctx_steps, ctx_len, cn):
    i = pl.program_id(0)
    is_ctx = i < ctx_steps
    tm = z0_ref.shape[0]
    mod = mod_ref[pl.ds(_mod_row(i, ctx_steps, tm, tm), 1), :]
    x = _pick(is_ctx, (xc_ref, xl_ref), (slice(None), slice(None)))
    h = _bf(x * (1.0 + mod[:, D_MODEL:2 * D_MODEL]) + mod[:, 0:D_MODEL])

    def conv(col):
        cols = slice(col, col + cn)
        return _conv3(_dot(h, w_ref[:, cols]), cw_ref, cb_ref, cols, ctx_len, interior=is_ctx)

    for c in range(D_MODEL // cn):
        z0_ref[:, c * cn:(c + 1) * cn] = _bf(_dot(h, w_ref[:, c * cn:(c + 1) * cn]))
        x1 = conv(D_MODEL + c * cn)
        v = conv(2 * D_MODEL + c * cn)
        vx_ref[:, c * cn:(c + 1) * cn] = _bf(v * x1)


def _hy_in(x_ctx, x_lat, ctx_len, lat_len, mods, layer, w_in, sconv_w, sconv_b, casts=()):
    tm = lat_len
    assert tm % ctx_len == 0 and x_ctx.shape[0] % tm == 0
    ctx_steps = x_ctx.shape[0] // tm
    n = x_ctx.shape[0] + x_lat.shape[0]
    tok = pl.BlockSpec((tm, D_MODEL), lambda i: (i, 0))
    const = lambda shape: pl.BlockSpec(shape, lambda i: (0, 0), pipeline_mode=pl.Buffered(1))
    act = jax.ShapeDtypeStruct((n, D_MODEL), BF16)
    body = functools.partial(_hy_in_kernel, ctx_steps=ctx_steps, ctx_len=ctx_len, cn=HY_IN_COLS)
    return pl.pallas_call(
        _with_casts(body, 6, 2, casts),
        out_shape=(act, act) + tuple(c.out_shape for c in casts),
        grid=(n // tm,),
        in_specs=[
            pl.BlockSpec((tm, D_MODEL), lambda i: (_ctx_block(i, ctx_steps), 0)),
            pl.BlockSpec((tm, D_MODEL), lambda i: (_lat_block(i, ctx_steps), 0)),
            pl.BlockSpec((None, MOD_ROWS, 6 * D_MODEL), lambda i: (layer, 0, 0)),
            const((D_MODEL, 3 * D_MODEL)),
            pl.BlockSpec((3, 3 * D_MODEL), lambda i: (0, 0)),
            pl.BlockSpec((1, 3 * D_MODEL), lambda i: (0, 0)),
        ] + [c.in_spec for c in casts],
        out_specs=(tok, tok) + tuple(c.out_spec for c in casts),
        compiler_params=_params("arbitrary"),
        name="hyena_in",
    )(x_ctx, x_lat, mods, w_in, sconv_w, sconv_b, *[c.w for c in casts])


def _hy_conv_plain_kernel(vx_ref, z0_ref, cw_ref, cb_ref, coef_ref, bias_ref, fwd_ref, inv_ref, u_ref, *,
                          L, nseq, cn):
    pieces = [(c, s) for c in range(vx_ref.shape[1] // cn) for s in range(nseq)]

    def forward(piece):
        c, s = piece
        vx = vx_ref[s * L:(s + 1) * L, c * cn:(c + 1) * cn]
        return vx, _dot(fwd_ref[...], vx)

    pending = [forward(p) for p in pieces[:HY_CONV_LOOKAHEAD]]
    for i, (c, s) in enumerate(pieces):
        if i + HY_CONV_LOOKAHEAD < len(pieces):
            pending.append(forward(pieces[i + HY_CONV_LOOKAHEAD]))
        vx, spec = pending.pop(0)
        rows = slice(s * L, (s + 1) * L)
        cols = slice(c * cn, (c + 1) * cn)
        k_cos = coef_ref[0, :, cols]
        k_sin = coef_ref[1, :, cols]
        k_mix = coef_ref[2, :, cols]
        v_cos = spec[0:L]
        v_sin = spec[L:2 * L]
        y_spec = _bf(jnp.concatenate([v_cos * k_cos - v_sin * k_sin, v_cos * k_sin + v_sin * k_mix], axis=0))
        y = _dot(inv_ref[...], y_spec)
        x0 = _conv3(z0_ref[rows, cols].astype(F32), cw_ref, cb_ref, cols, L)
        u_ref[rows, cols] = _bf((y + bias_ref[:, cols] * vx.astype(F32)) * x0)


def _hy_conv_split_kernel(vx_ref, z0_ref, cw_ref, cb_ref, coef_ref, bias_ref, fe_ref, fo_ref, ge_ref, go_ref,
                          u_ref, sv_ref, sx_ref, su_ref, *, L, nseq, cn):
    half = L // 2
    for g in range(vx_ref.shape[1] // LANES):
        lanes = slice(g * LANES, (g + 1) * LANES)
        sv_ref[g] = vx_ref[:, lanes].astype(F32)
        sx_ref[g] = _conv3(z0_ref[:, lanes].astype(F32), cw_ref, cb_ref, lanes, L)

    def load(ref, start, c):
        return jnp.concatenate([ref[g, pl.ds(start, half, stride=2), :]
                                for g in range(c * cn // LANES, (c + 1) * cn // LANES)], axis=1)

    def store(ref, start, c, val):
        for k, g in enumerate(range(c * cn // LANES, (c + 1) * cn // LANES)):
            ref[g, pl.ds(start, half, stride=2), :] = val[:, k * LANES:(k + 1) * LANES]

    pieces = [(c, s) for c in range(vx_ref.shape[1] // cn) for s in range(nseq)]

    def forward(piece):
        c, s = piece
        v_e = load(sv_ref, s * L, c)
        v_o = load(sv_ref, s * L + 1, c)
        return v_e, v_o, _dot(fe_ref[...], _bf(v_e)), _dot(fo_ref[...], _bf(v_o))

    pending = [forward(p) for p in pieces[:HY_CONV_LOOKAHEAD]]
    for i, (c, s) in enumerate(pieces):
        if i + HY_CONV_LOOKAHEAD < len(pieces):
            pending.append(forward(pieces[i + HY_CONV_LOOKAHEAD]))
        v_e, v_o, a, b = pending.pop(0)
        cols = slice(c * cn, (c + 1) * cn)
        k_c, k_s, u_c, u_s, e2, e3, f2, f3 = [coef_ref[k, :, cols] for k in range(coef_ref.shape[0])]
        s_c, s_s = a[0:half] + b[0:half], a[half:L] + b[half:L]
        d_c, d_s = a[0:half] - b[0:half], a[half:L] - b[half:L]
        t1 = s_c * k_c - s_s * k_s
        t2 = d_c * u_c + d_s * u_s
        q1 = s_c * k_s
        q4 = d_c * u_s
        p_spec = jnp.concatenate([t1 + t2, q1 + s_s * e2 + d_s * e3 - q4], axis=0)
        m_spec = jnp.concatenate([t1 - t2, q1 + s_s * f2 - d_s * f3 + q4], axis=0)
        y_e = _dot(ge_ref[...], _bf(p_spec))
        y_o = _dot(go_ref[...], _bf(m_spec))
        bias = bias_ref[:, cols]
        store(su_ref, s * L, c, (y_e + bias * v_e) * load(sx_ref, s * L, c))
        store(su_ref, s * L + 1, c, (y_o + bias * v_o) * load(sx_ref, s * L + 1, c))
        rows = slice(s * L, (s + 1) * L)
        for g in range(c * cn // LANES, (c + 1) * cn // LANES):
            u_ref[rows, g * LANES:(g + 1) * LANES] = _bf(su_ref[g, rows, :])


def _hy_conv_steps(n):
    nt = n // HY_CONV_TOKENS
    return (D_MODEL // HY_CONV_CHANNELS) * nt, lambda j, i: j * nt + i


def _hy_conv(vx, z0, sconv_w, sconv_b, tok0, n, coef, bias, L, casts=()):
    tm = HY_CONV_TOKENS
    dt = HY_CONV_CHANNELS
    assert tok0 % tm == 0 and n % tm == 0 and tm % L == 0
    src = pl.BlockSpec((tm, dt), lambda j, i: (tok0 // tm + i, j))
    dst = pl.BlockSpec((tm, dt), lambda j, i: (i, j))
    const = lambda shape: pl.BlockSpec(shape, lambda j, i: (0, 0), pipeline_mode=pl.Buffered(1))
    if L >= HY_SPLIT_MIN_LEN:
        fwd, inv = _dft_tables(L)
        tables = (*fwd, *inv)
        body = functools.partial(_hy_conv_split_kernel, L=L, nseq=tm // L, cn=256)
        scratch = [pltpu.VMEM((dt // LANES, tm, LANES), F32)] * 3
    else:
        tables = _plain_dft_tables(L)[:2]
        body = functools.partial(_hy_conv_plain_kernel, L=L, nseq=tm // L, cn=256)
        scratch = []
    return pl.pallas_call(
        _with_casts(body, 6 + len(tables), 1, casts),
        out_shape=(jax.ShapeDtypeStruct((n, D_MODEL), BF16),) + tuple(c.out_shape for c in casts),
        grid=(D_MODEL // dt, n // tm),
        in_specs=[
            src, src,
            pl.BlockSpec((3, dt), lambda j, i: (0, j)),
            pl.BlockSpec((1, dt), lambda j, i: (0, j)),
            pl.BlockSpec(_filter_coef_shape(L) + (dt,), lambda j, i: (0, 0, j)),
            pl.BlockSpec((1, dt), lambda j, i: (0, j)),
        ] + [const(t.shape) for t in tables] + [c.in_spec for c in casts],
        out_specs=(dst,) + tuple(c.out_spec for c in casts),
        scratch_shapes=scratch,
        compiler_params=_params("arbitrary", "arbitrary"),
        name=f"hyena_conv_{L}",
    )(vx, z0, sconv_w, sconv_b, coef, bias, *[_bf(jnp.asarray(t)) for t in tables], *[c.w for c in casts])


def _post_kernel(*refs, layer, ctx_steps, lat_len, n_x, n_a, a_feature_major):
    x_refs, a_refs = refs[:n_x], refs[n_x:n_x + n_a]
    (mod_ref, wo_ref, ln1g_ref, ln1b_ref, win_ref, wout_ref, ln2g_ref, ln2b_ref, o_ref,
     act_ref) = refs[n_x + n_a:]
    d = D_MODEL
    i = pl.program_id(0)
    is_ctx = i < ctx_steps
    tm = act_ref.shape[0]
    mod = mod_ref[pl.ds(_mod_row(i, ctx_steps, tm, lat_len), 1), :]
    g1 = mod[:, 2 * d:3 * d]
    sh2 = mod[:, 3 * d:4 * d]
    sc2 = mod[:, 4 * d:5 * d]
    g2 = mod[:, 5 * d:6 * d]
    ln1 = (ln1g_ref[layer:layer + 1, :], ln1b_ref[layer:layer + 1, :])
    ln2 = (ln2g_ref[layer:layer + 1, :], ln2b_ref[layer:layer + 1, :])

    groups = [slice(r, r + POST_GROUP_ROWS) for r in range(0, tm, POST_GROUP_ROWS)]
    if a_feature_major:
        mix = [_dot_tn(_pick(is_ctx, a_refs, (slice(None), rows)), wo_ref[...]) for rows in groups]
    else:
        mix = [_dot(_pick(is_ctx, a_refs, (rows, slice(None))), wo_ref[...]) for rows in groups]
    x1 = []
    for rows, mx in zip(groups, mix):
        x = _pick(is_ctx, x_refs, (rows, slice(None)))
        x1.append(_layer_norm(ALPHA * x + g1 * mx, *ln1))
        h = _bf(x1[-1] * (1.0 + sc2) + sh2)
        for c in range(D_FF // FFN_COLS):
            cols = slice(c * FFN_COLS, (c + 1) * FFN_COLS)
            gate = _dot(h, win_ref[:, cols])
            up = _dot(h, win_ref[:, D_FF + c * FFN_COLS:D_FF + (c + 1) * FFN_COLS])
            act_ref[rows, cols] = _bf(gate * jax.nn.sigmoid(gate) * up)
    for rows, xr in zip(groups, x1):
        f = _dot(act_ref[rows, :], wout_ref[...])
        o_ref[rows, :] = _layer_norm(ALPHA * xr + g2 * f, *ln2)


def _post(xs, mixes, n, n_ctx, lat_len, mods, layer, w_o, w_in, w_out, ln1_g, ln1_b, ln2_g, ln2_b, name,
          x_tok0=0, a_feature_major=False, casts=(), tm=POST_TOKENS):
    ctx_steps = n_ctx // tm
    ctx = lambda i: _ctx_block(i, ctx_steps)
    lat = lambda i: _lat_block(i, ctx_steps)
    blocks = lambda m, tok0: (lambda i: tok0 // tm + i,) if m == 1 else (ctx, lat)
    tok = lambda f: pl.BlockSpec((tm, D_MODEL), lambda i: (f(i), 0))
    feat = lambda f: pl.BlockSpec((D_MODEL, tm), lambda i: (0, f(i)))
    const = lambda shape: pl.BlockSpec(shape, lambda i: (0, 0), pipeline_mode=pl.Buffered(1))
    vec = pl.BlockSpec((DEPTH, D_MODEL), lambda i: (0, 0))
    body = functools.partial(_post_kernel, layer=layer, ctx_steps=ctx_steps, lat_len=lat_len, n_x=len(xs),
                             n_a=len(mixes), a_feature_major=a_feature_major)
    return pl.pallas_call(
        _with_casts(body, len(xs) + len(mixes) + 8, 1, casts),
        out_shape=(jax.ShapeDtypeStruct((n, D_MODEL), F32),) + tuple(c.out_shape for c in casts),
        grid=(n // tm,),
        in_specs=[tok(f) for f in blocks(len(xs), x_tok0)]
        + [(feat if a_feature_major else tok)(f) for f in blocks(len(mixes), 0)]
        + [pl.BlockSpec((None, MOD_ROWS, 6 * D_MODEL), lambda i: (layer, 0, 0)),
           const((D_MODEL, D_MODEL)), vec, vec,
           const((D_MODEL, 2 * D_FF)), const((D_FF, D_MODEL)), vec, vec]
        + [c.in_spec for c in casts],
        out_specs=(tok(lambda i: i),) + tuple(c.out_spec for c in casts),
        scratch_shapes=[pltpu.VMEM((tm, D_FF), BF16)],
        compiler_params=_params("arbitrary"),
        name=name,
    )(*xs, *mixes, mods, w_o, ln1_g, ln1_b, w_in, w_out, ln2_g, ln2_b, *[c.w for c in casts])


def _qkv_kernel(x_ref, mod_ref, wt_ref, qkvt_ref, nk_ref, nv_ref, *, ctx_steps, lat_len):
    d = D_MODEL
    i = pl.program_id(0)
    tm = x_ref.shape[0]
    mod = mod_ref[pl.ds(_mod_row(i, ctx_steps, tm, lat_len), 1), :]
    h = _bf(x_ref[...] * (1.0 + mod[:, d:2 * d]) + mod[:, 0:d])
    qkvt_ref[0:d, :] = _bf(_dot_nt(wt_ref[0:d, :], h) * (HEAD_DIM ** -0.5 * LOG2E))
    kt = _dot_nt(wt_ref[d:2 * d, :], h)
    qkvt_ref[d:2 * d, :] = _bf(kt)
    vt = _dot_nt(wt_ref[2 * d:3 * d, :], h)
    qkvt_ref[2 * d:3 * d, :] = _bf(vt)

    @pl.when(i < ctx_steps)
    def _():
        seq = nk_ref.shape[4]
        for s in range(tm // seq):
            nk_ref[s, 0] = kt[:, s * seq:(s + 1) * seq].reshape(N_HEADS, HEAD_DIM, seq)
            nv_ref[s, 0] = vt[:, s * seq:(s + 1) * seq].reshape(N_HEADS, HEAD_DIM, seq)


def _qkv(x2d, n_ctx, ctx_len, lat_len, mods, layer, w_qkv_t):
    n = x2d.shape[0]
    tm = QKV_TOKENS
    assert tm % ctx_len == 0 and n_ctx % tm == 0
    ctx_steps = n_ctx // tm
    kv_shape = jax.ShapeDtypeStruct((n_ctx // ctx_len, 1, N_HEADS, HEAD_DIM, ctx_len), F32)
    kv_spec = pl.BlockSpec((tm // ctx_len, 1, N_HEADS, HEAD_DIM, ctx_len),
                           lambda i: (_ctx_block(i, ctx_steps), 0, 0, 0, 0))
    return pl.pallas_call(
        functools.partial(_qkv_kernel, ctx_steps=ctx_steps, lat_len=lat_len),
        out_shape=(jax.ShapeDtypeStruct((3 * D_MODEL, n), BF16), kv_shape, kv_shape),
        grid=(n // tm,),
        in_specs=[
            pl.BlockSpec((tm, D_MODEL), lambda i: (i, 0)),
            pl.BlockSpec((None, MOD_ROWS, 6 * D_MODEL), lambda i: (layer, 0, 0)),
            pl.BlockSpec((3 * D_MODEL, D_MODEL), lambda i: (0, 0), pipeline_mode=pl.Buffered(1)),
        ],
        out_specs=(pl.BlockSpec((3 * D_MODEL, tm), lambda i: (0, i)), kv_spec, kv_spec),
        compiler_params=_params("arbitrary"),
        name="qkv",
    )(x2d, mods, w_qkv_t)


def _softmax_weights(*scores):
    m = functools.reduce(jnp.maximum, [jnp.max(s, axis=0, keepdims=True) for s in scores])
    return [_bf(jnp.exp2(s - m)) for s in scores]


def _weighted_values(vt, p):
    ones = jnp.ones((BF16_ROWS, vt.shape[1]), BF16)
    return _dot(jnp.concatenate([vt, ones], axis=0), p)


def _normalise(acc):
    return _bf(acc[0:HEAD_DIM] * (1.0 / acc[HEAD_DIM:HEAD_DIM + 1]))


def _ctx_attn_kernel(qt_ref, kt_ref, vt_ref, ot_ref, *, seq):
    units = [(s, hh) for s in range(qt_ref.shape[1] // seq) for hh in range(N_HEADS)]

    def block(unit):
        s, hh = unit
        return slice(hh * HEAD_DIM, (hh + 1) * HEAD_DIM), slice(s * seq, (s + 1) * seq)

    def scores(unit):
        return _dot_tn(kt_ref[block(unit)], qt_ref[block(unit)])

    pending = [scores(u) for u in units[:CTX_ATTN_LOOKAHEAD]]
    for i, unit in enumerate(units):
        if i + CTX_ATTN_LOOKAHEAD < len(units):
            pending.append(scores(units[i + CTX_ATTN_LOOKAHEAD]))
        (p,) = _softmax_weights(pending.pop(0))
        ot_ref[block(unit)] = _normalise(_weighted_values(vt_ref[block(unit)], p))


def _ctx_attn(qkvt, n_ctx, seq):
    tm = CTX_ATTN_SEQS * seq
    return pl.pallas_call(
        functools.partial(_ctx_attn_kernel, seq=seq),
        out_shape=jax.ShapeDtypeStruct((D_MODEL, n_ctx), BF16),
        grid=(n_ctx // tm,),
        in_specs=[pl.BlockSpec((D_MODEL, tm), lambda b, c=c: (c, b)) for c in range(3)],
        out_specs=pl.BlockSpec((D_MODEL, tm), lambda b: (0, b)),
        compiler_params=_params("arbitrary"),
        name="ctx_attn",
    )(qkvt, qkvt, qkvt)


LAT_ROWS = 16
Q_BLOCK_ROWS = 4
KEY_TILE_ROWS = 4


def _key_band(qb):
    row0 = [min(max(r - WIN_H // 2, 0), LAT_ROWS - WIN_H) for r in range(qb * Q_BLOCK_ROWS, (qb + 1) * Q_BLOCK_ROWS)]
    lo = min(row0) // KEY_TILE_ROWS * KEY_TILE_ROWS
    hi = -(-(max(row0) + WIN_H) // KEY_TILE_ROWS) * KEY_TILE_ROWS
    return lo, hi


KEY_BANDS = tuple(_key_band(qb) for qb in range(LAT_ROWS // Q_BLOCK_ROWS))
MAX_BAND_ROWS = max(hi - lo for lo, hi in KEY_BANDS)


def _fill_bias_table(rpb_ref, tab_ref):
    shape = (GRID_W, LANES)
    kc = lax.broadcasted_iota(jnp.int32, shape, 0)
    lane = lax.broadcasted_iota(jnp.int32, shape, 1)
    qc = lane % GRID_W
    win0 = jnp.clip(qc - WIN_W // 2, 0, GRID_W - WIN_W)
    col_ok = (kc >= win0) & (kc < win0 + WIN_W)
    masked = jnp.full(shape, MASK_VALUE, F32)
    tiles = []
    for dr in range(2 * WIN_H - 1):
        row = jnp.broadcast_to(rpb_ref[dr:dr + 1, :] * LOG2E, shape)
        t = pltpu.roll(row, LANES - (WIN_W - 1), 1, stride=1, stride_axis=0)
        t = jnp.where(lane < GRID_W, t, pltpu.roll(t, GRID_W, 1))
        tiles.append(jnp.where(col_ok, t, MASK_VALUE))

    def tile(qr, kr):
        row0 = min(max(qr - WIN_H // 2, 0), LAT_ROWS - WIN_H)
        if row0 <= kr < row0 + WIN_H:
            return tiles[kr - qr + WIN_H - 1]
        return masked

    for qr in range(0, LAT_ROWS, 2):
        lo, hi = KEY_BANDS[qr // Q_BLOCK_ROWS]
        for i in range(hi - lo):
            both = jnp.where(lane < GRID_W, tile(qr, lo + i), tile(qr + 1, lo + i))
            tab_ref[i * GRID_W:(i + 1) * GRID_W, qr * GRID_W:(qr + 2) * GRID_W] = both


def _lat_attn_kernel(qt_ref, kt_ref, vt_ref, ckt_ref, cvt_ref, rpb_ref, ot_ref, tab_ref):
    seq = LAT_ROWS * GRID_W
    nq = Q_BLOCK_ROWS * GRID_W
    heads = qt_ref.shape[0] // HEAD_DIM
    units = [(b, hh, qb) for b in range(qt_ref.shape[1] // seq) for hh in range(heads)
             for qb in range(len(KEY_BANDS))]

    def rows(hh):
        return slice(hh * HEAD_DIM, (hh + 1) * HEAD_DIM)

    def keys(b, qb):
        lo, hi = KEY_BANDS[qb]
        return slice(b * seq + lo * GRID_W, b * seq + hi * GRID_W)

    def queries(b, qb):
        return slice(b * seq + qb * nq, b * seq + (qb + 1) * nq)

    def scores(unit):
        b, hh, qb = unit
        lo, hi = KEY_BANDS[qb]
        qt = qt_ref[rows(hh), queries(b, qb)]
        s_loc = (_dot_tn(kt_ref[rows(hh), keys(b, qb)], qt)
                 + tab_ref[hh, 0:(hi - lo) * GRID_W, qb * nq:(qb + 1) * nq])
        s_ctx = _dot_tn(_bf(ckt_ref[b, hh]), qt)
        return s_loc, s_ctx

    for hh in range(tab_ref.shape[0]):
        _fill_bias_table(rpb_ref.at[hh], tab_ref.at[hh])
    pending = [scores(u) for u in units[:LAT_ATTN_LOOKAHEAD]]
    for i, (b, hh, qb) in enumerate(units):
        if i + LAT_ATTN_LOOKAHEAD < len(units):
            pending.append(scores(units[i + LAT_ATTN_LOOKAHEAD]))
        s_loc, s_ctx = pending.pop(0)
        p_loc, p_ctx = _softmax_weights(s_loc, s_ctx)
        acc = (_weighted_values(vt_ref[rows(hh), keys(b, qb)], p_loc)
               + _weighted_values(_bf(cvt_ref[b, hh]), p_ctx))
        ot_ref[rows(hh), queries(b, qb)] = _normalise(acc)


def _lat_attn(qkvt, n_ctx, cache_kt, cache_vt, rpb, layer_idx):
    rpb = jnp.pad(rpb[:, :, ::-1], ((0, 0), (0, 1), (0, LANES - (2 * WIN_W - 1))))
    n = qkvt.shape[1] - n_ctx
    seq = LAT_ROWS * GRID_W
    assert n_ctx % n == 0 and n % seq == 0
    nb = n // seq
    hs = LAT_ATTN_HEADS
    nhp = N_HEADS // hs
    past = cache_kt.shape[4]
    feat = lambda c: pl.BlockSpec((hs * HEAD_DIM, n), lambda hp, c=c: (c * nhp + hp, n_ctx // n))
    cache = pl.BlockSpec((nb, None, hs, HEAD_DIM, past), lambda hp: (0, layer_idx, hp, 0, 0))
    return pl.pallas_call(
        _lat_attn_kernel,
        out_shape=jax.ShapeDtypeStruct((D_MODEL, n), BF16),
        grid=(nhp,),
        in_specs=[feat(0), feat(1), feat(2), cache, cache,
                  pl.BlockSpec((hs, 2 * WIN_H, LANES), lambda hp: (hp, 0, 0))],
        out_specs=pl.BlockSpec((hs * HEAD_DIM, n), lambda hp: (hp, 0)),
        scratch_shapes=[pltpu.VMEM((hs, MAX_BAND_ROWS * GRID_W, seq), F32)],
        compiler_params=_params("arbitrary"),
        name="lat_attn",
    )(qkvt, qkvt, qkvt, cache_kt, cache_vt, rpb)


def kernel(x_prompt, x_sample, cache_k, cache_v, c, c_ctx, ada_w, ada_b, ln1_g, ln1_b, ln2_g, ln2_b, ffn_w_in, ffn_w_out, hy_w_in, hy_sconv_w, hy_sconv_b, hy_pe_w1, hy_pe_b1, hy_pe_w2, hy_pe_b2, hy_pe_w3, hy_sin_freq, hy_decay, hy_bias, hy_w_out, na_w_qkv, na_rpb, na_w_out):
    nbp, lp, d = x_prompt.shape
    nbs, ls, _ = x_sample.shape
    assert d == D_MODEL and ls == LAT_ROWS * GRID_W and c.shape[0] == nbs
    xp = x_prompt.reshape(nbp * lp, d)
    xs = x_sample.reshape(nbs * ls, d)
    n_ctx, n_lat = xp.shape[0], xs.shape[0]
    n = n_ctx + n_lat
    ln = (ln1_g, ln1_b, ln2_g, ln2_b)

    filt = (hy_pe_w1[0], hy_pe_b1[0], hy_pe_w2[0], hy_pe_b2[0], hy_pe_w3[0], hy_sin_freq[0], hy_decay[0])
    steps = DEPTH * D_MODEL // FILTER_CHANNELS
    mods, coef_p, coef_s, hy_w_in0 = _prep(c_ctx, c, ada_w, ada_b, (lp, ls), *filt,
                                           casts=[_Cast(hy_w_in, 0, steps, _step)])

    sconv_b = hy_sconv_b[0].reshape(1, -1)
    hy_bias0 = hy_bias[0].reshape(1, -1)
    sconv = (hy_sconv_w[0], sconv_b)
    z0, vx, ffn_w_in0, ffn_w_out0 = _hy_in(xp, xs, lp, ls, mods, 0, hy_w_in0, *sconv,
                                           casts=[_Cast(ffn_w_in, 0, n // ls, _step),
                                                  _Cast(ffn_w_out, 0, n // ls, _step)])
    steps, lin = _hy_conv_steps(n_ctx)
    u_p, hy_w_out0 = _hy_conv(vx, z0, *sconv, 0, n_ctx, coef_p, hy_bias0, lp,
                              casts=[_Cast(hy_w_out, 0, steps, lin)])
    (u_s,) = _hy_conv(vx, z0, *sconv, n_ctx, n_lat, coef_s, hy_bias0, ls)
    steps = n // POST_TOKENS
    x1, ffn_w_in1, ffn_w_out1, na_w_out0, w_qkv_t = _post(
        (xp, xs), (u_p, u_s), n, n_ctx, ls, mods, 0, hy_w_out0, ffn_w_in0, ffn_w_out0, *ln, name="post0",
        casts=[_Cast(ffn_w_in, 1, steps, _step), _Cast(ffn_w_out, 1, steps, _step),
               _Cast(na_w_out, 0, steps, _step),
               _Cast(na_w_qkv, 0, steps // 2, lambda i: i // 2, transpose=True)])

    qkvt, new_kt, new_vt = _qkv(x1, n_ctx, lp, ls, mods, 1, w_qkv_t)
    at_p = _ctx_attn(qkvt, n_ctx, lp)
    at_s = _lat_attn(qkvt, n_ctx, jnp.swapaxes(cache_k, 3, 4), jnp.swapaxes(cache_v, 3, 4), na_rpb[0], 0)
    ffn1 = (na_w_out0, ffn_w_in1, ffn_w_out1, *ln)
    (yp,) = _post((x1,), (at_p,), n_ctx, n_ctx, ls, mods, 1, *ffn1, name="post1_ctx", a_feature_major=True)
    (ys,) = _post((x1,), (at_s,), n_lat, 0, ls, mods, 1, *ffn1, name="post1_lat", x_tok0=n_ctx,
                  a_feature_major=True)

    return (yp.reshape(nbp, lp, d), ys.reshape(nbs, ls, d),
            jnp.swapaxes(new_kt, 3, 4), jnp.swapaxes(new_vt, 3, 4))
```

```python
import functools
import math

import jax
import jax.numpy as jnp
import numpy as np
from jax import lax
from jax.experimental import pallas as pl
from jax.experimental.pallas import tpu as pltpu

D_MODEL = 1024
N_HEADS = 16
HEAD_DIM = D_MODEL // N_HEADS
D_FF = 2816
GRID_W = 64
WIN_H = 8
WIN_W = 16
N_BANDS = 16
PE_DIM = 1 + 2 * N_BANDS
FILT_W = 64
MOD_SHIFT = 0.05
DEPTH = 2
ALPHA = (2 * DEPTH) ** 0.25
LN_EPS = 1e-5
MASK_VALUE = -1e30
MOD_ROWS = 8

LANES = 128
SUBLANES = 8
BF16_ROWS = 2 * SUBLANES
CTX_ATTN_LOOKAHEAD = 6
CTX_ATTN_SEQS = 4
LAT_ATTN_HEADS = 4
LAT_ATTN_LOOKAHEAD = 2
LOG2E = math.log2(math.e)
VMEM_LIMIT = 56 * 1024 * 1024

ADA_COLS = 1536
FILTER_CHANNELS = 256
HY_IN_COLS = 256
HY_CONV_TOKENS = 1024
HY_CONV_CHANNELS = 512
HY_CONV_PIECE_CHANNELS = 256
HY_CONV_LOOKAHEAD = 2
POST_TOKENS = 512
POST_GROUP_ROWS = 256
FFN_COLS = 256
QKV_TOKENS = 1024

F32 = jnp.float32
BF16 = jnp.bfloat16


def _bf(x):
    return x.astype(BF16)


def _dot(a, b):
    return jnp.dot(a, b, preferred_element_type=F32)


def _dot_nt(a, b):
    return lax.dot_general(a, b, (((1,), (1,)), ((), ())), preferred_element_type=F32)


def _dot_tn(a, b):
    return lax.dot_general(a, b, (((0,), (0,)), ((), ())), preferred_element_type=F32)


def _dot3(a, b):
    a_hi = _bf(a)
    a_lo = _bf(a - a_hi.astype(F32))
    b_hi = _bf(b)
    b_lo = _bf(b - b_hi.astype(F32))
    m = a.shape[0]
    both = _dot(jnp.concatenate([a_hi, a_lo], axis=0), b_hi)
    return both[0:m] + both[m:2 * m] + _dot(a_hi, b_lo)


def _layer_norm(x, g, b):
    mu = jnp.mean(x, axis=-1, keepdims=True)
    xc = x - mu
    var = jnp.mean(xc * xc, axis=-1, keepdims=True)
    return xc * lax.rsqrt(var + LN_EPS) * g + b


def _params(*sem):
    return pltpu.CompilerParams(dimension_semantics=sem, vmem_limit_bytes=VMEM_LIMIT)


class _Cast:
    def __init__(self, w, layer, steps, step_fn, transpose=False):
        _, r, c = w.shape
        assert r % (steps * BF16_ROWS) == 0
        rows = r // steps
        self.w = w
        self.transpose = transpose
        self.in_spec = pl.BlockSpec((None, rows, c), lambda *g: (layer, step_fn(*g), 0))
        if transpose:
            self.out_shape = jax.ShapeDtypeStruct((c, r), BF16)
            self.out_spec = pl.BlockSpec((c, rows), lambda *g: (0, step_fn(*g)))
        else:
            self.out_shape = jax.ShapeDtypeStruct((r, c), BF16)
            self.out_spec = pl.BlockSpec((rows, c), lambda *g: (step_fn(*g), 0))


def _with_casts(body, n_in, n_out, casts):
    nc = len(casts)

    def kernel_fn(*refs):
        ins, src = refs[:n_in], refs[n_in:n_in + nc]
        outs, dst = refs[n_in + nc:n_in + nc + n_out], refs[n_in + nc + n_out:n_in + 2 * nc + n_out]
        for cast, s, d in zip(casts, src, dst):
            d[...] = _bf(s[...].T if cast.transpose else s[...])
        body(*ins, *outs, *refs[n_in + 2 * nc + n_out:])

    return kernel_fn


def _step(i):
    return i


def _ctx_block(i, ctx_steps):
    return jnp.minimum(i, ctx_steps - 1)


def _lat_block(i, ctx_steps):
    return jnp.maximum(i - ctx_steps, 0)


def _mod_row(i, ctx_steps, tm, lat_len):
    return jnp.where(i < ctx_steps, 0, 1 + ((i - ctx_steps) * tm) // lat_len)


def _pick(is_ctx, refs, idx):
    if len(refs) == 1:
        return refs[0][idx]
    return jnp.where(is_ctx, refs[0][idx], refs[1][idx])


@functools.lru_cache(maxsize=None)
def _plain_dft_tables(L):
    n = 2 * L
    f = np.arange(L)[:, None]
    s = np.arange(L)[None, :]
    ang = 2.0 * np.pi * ((f * s) % n) / n
    c = np.cos(ang)
    sn = np.sin(ang)
    sn[0, :] = np.where(np.arange(L) % 2 == 0, 1.0, -1.0)
    fwd = np.concatenate([c, sn], axis=0)
    w = np.full((2 * L, 1), 2.0 / n)
    w[0, 0] = 1.0 / n
    w[L, 0] = 1.0 / n
    inv = (fwd * w).T
    alt = np.where(np.arange(L) % 2 == 0, 1.0, -1.0).astype(np.float32)[:, None]
    return (np.ascontiguousarray(fwd).astype(np.float32), np.ascontiguousarray(inv).astype(np.float32), alt)


@functools.lru_cache(maxsize=None)
def _dft_tables(L):
    n = 2 * L
    half = L // 2
    f = np.arange(half)[:, None]
    m = np.arange(half)[None, :]
    alt = np.where(np.arange(half) % 2 == 0, 1.0, -1.0)
    w = np.full((L, 1), 2.0 / n)
    w[0, 0] = 1.0 / n
    w[half, 0] = 1.0 / n
    fwd, inv = [], []
    for parity in range(2):
        ang = 2.0 * np.pi * ((f * (2 * m + parity)) % n) / n
        sn = np.sin(ang)
        sn[0, :] = alt
        table = np.concatenate([np.cos(ang), sn], axis=0)
        fwd.append(np.ascontiguousarray(table).astype(np.float32))
        inv.append(np.ascontiguousarray((table * w).T).astype(np.float32))
    return fwd, inv


@functools.lru_cache(maxsize=None)
def _filter_dft_tables(L):
    n = 2 * L
    f = np.arange(L // 2)[:, None]
    t = np.arange(L)[None, :]
    ang = 2.0 * np.pi * ((f * t) % n) / n
    table = np.concatenate([np.cos(ang), np.sin(ang)], axis=0).astype(np.float32)
    tt = np.arange(L)
    cols = np.stack([np.where(tt % 2 == 0, 1.0, -1.0), np.cos(np.pi * (tt % 4) / 2.0),
                     np.sin(np.pi * (tt % 4) / 2.0)], axis=1)
    return np.ascontiguousarray(table), np.round(cols).astype(np.float32)


@functools.lru_cache(maxsize=None)
def _filter_features(L):
    t = np.linspace(0.0, 1.0, L, dtype=np.float32)[:, None]
    w = (np.float32(2.0 * math.pi / L) * np.arange(L, dtype=np.float32))[:, None]
    bands = np.linspace(1e-4, N_BANDS - 1, N_BANDS, dtype=np.float32)[None, :]
    arg = (bands * w).astype(np.float64)
    feats = np.concatenate([t.astype(np.float64), np.cos(arg), np.sin(arg)], axis=-1)
    out = np.zeros((L, LANES), np.float32)
    out[:, :PE_DIM] = feats
    return out, t


HY_SPLIT_MIN_LEN = 512


def _filter_coef_shape(L):
    return (8, L // 2) if L >= HY_SPLIT_MIN_LEN else (3, L)


def _ada_tile(layer, cctx_ref, c_ref, w_ref, b_ref, o_ref, cvec_ref):
    nb = c_ref.shape[0]
    cvec_ref[...] = jnp.zeros_like(cvec_ref)
    cvec_ref[0:1, :] = cctx_ref[...]
    cvec_ref[1:1 + nb, :] = c_ref[...]
    c = cvec_ref[...]
    o_ref[...] = _dot3(c * jax.nn.sigmoid(c), w_ref[...]) + b_ref[pl.ds(layer, 1), :]


def _filter_tile(is_first, feats_ref, t_ref, cols_ref, tab_ref, w1_ref, b1_ref, w2_ref, b2_ref, freq_ref,
                 w3f_ref, w3b_ref, decf_ref, decb_ref, out_ref, h_ref, w1p_ref, *, L):
    hi = lax.Precision.HIGHEST

    @pl.when(is_first)
    def _():
        w1p_ref[...] = jnp.zeros_like(w1p_ref)
        w1p_ref[0:PE_DIM, :] = w1_ref[...]
        h1 = jnp.sin(freq_ref[0:1, :] * (jnp.dot(feats_ref[...], w1p_ref[...], precision=hi,
                                                 preferred_element_type=F32) + b1_ref[...]))
        h_ref[...] = jnp.sin(freq_ref[1:2, :] * (jnp.dot(h1, w2_ref[...], precision=hi,
                                                         preferred_element_type=F32) + b2_ref[...]))

    h = h_ref[...]
    t = t_ref[...]
    kf = _dot3(h, w3f_ref[...]) * (jnp.exp(-t * jnp.abs(decf_ref[...])) + MOD_SHIFT)
    kb = _dot3(h, w3b_ref[...]) * (jnp.exp(-t * jnp.abs(decb_ref[...])) + MOD_SHIFT)
    row = lax.broadcasted_iota(jnp.int32, kf.shape, 0)
    kb = jnp.where(row == 0, 0.0, kb)
    ksum = kf + kb
    kdiff = kf - kb
    alt = cols_ref[:, 0:1]

    def spectrum(table, x):
        hi = _bf(x)
        return _dot(table, hi) + _dot(table, _bf(x - hi.astype(F32)))

    if L < HY_SPLIT_MIN_LEN:
        k_c = spectrum(tab_ref[0:L, :], ksum)
        k_s = spectrum(tab_ref[L:2 * L, :], kdiff)
        k_nyq = jnp.sum(ksum * alt, axis=0, keepdims=True)
        out_ref[0] = k_c
        out_ref[1] = jnp.where(row == 0, 0.0, k_s)
        out_ref[2] = jnp.where(row == 0, k_nyq, k_c)
        return

    half = L // 2
    cos_m = tab_ref[0:half, :]
    sin_m = tab_ref[half:L, :]
    k_c = spectrum(cos_m, ksum)
    k_s = spectrum(sin_m, kdiff)
    u_c = spectrum(cos_m, alt * ksum)
    u_s = -spectrum(sin_m, alt * kdiff)
    kc_h = jnp.sum(ksum * cols_ref[:, 1:2], axis=0, keepdims=True)
    ks_h = jnp.sum(kdiff * cols_ref[:, 2:3], axis=0, keepdims=True)
    first = lax.broadcasted_iota(jnp.int32, k_c.shape, 0) == 0
    out_ref[0] = k_c
    out_ref[1] = k_s
    out_ref[2] = u_c
    out_ref[3] = u_s
    out_ref[4] = jnp.where(first, kc_h - ks_h, k_c)
    out_ref[5] = jnp.where(first, kc_h + ks_h, u_c)
    out_ref[6] = jnp.where(first, kc_h + ks_h, k_c)
    out_ref[7] = jnp.where(first, kc_h - ks_h, u_c)


def _prep_kernel(cctx_ref, c_ref, w_ref, b_ref, fa_ref, ta_ref, ca_ref, tba_ref, fb_ref, tb_ref, cb_ref,
                 tbb_ref, w1_ref, b1_ref, w2_ref, b2_ref, freq_ref, w3f_ref, w3b_ref, decf_ref, decb_ref,
                 mod_ref, coefa_ref, coefb_ref, cvec_ref, ha_ref, hb_ref, w1p_ref, *, lens, tiles):
    s = pl.program_id(0)
    _ada_tile(s // tiles, cctx_ref, c_ref, w_ref, b_ref, mod_ref, cvec_ref)
    shared = (w1_ref, b1_ref, w2_ref, b2_ref, freq_ref, w3f_ref, w3b_ref, decf_ref, decb_ref)

    @pl.when(s < tiles)
    def _():
        _filter_tile(s == 0, fa_ref, ta_ref, ca_ref, tba_ref, *shared, coefa_ref, ha_ref, w1p_ref, L=lens[0])

    @pl.when(s >= tiles)
    def _():
        _filter_tile(s == tiles, fb_ref, tb_ref, cb_ref, tbb_ref, *shared, coefb_ref, hb_ref, w1p_ref, L=lens[1])


def _prep(c_ctx, c, ada_w, ada_b, lens, pe_w1, pe_b1, pe_w2, pe_b2, pe_w3, sin_freq, decay, casts=()):
    assert 1 + c.shape[0] <= MOD_ROWS and DEPTH == 2
    tn = ADA_COLS
    dt = FILTER_CHANNELS
    tiles = D_MODEL // dt
    assert 6 * D_MODEL // tn == tiles
    full = lambda shape: pl.BlockSpec(shape, lambda s: (0,) * len(shape))
    consts, const_specs, out_shapes, out_specs, scratch = [], [], [], [], []
    for k, L in enumerate(lens):
        feats, t = _filter_features(L)
        if L >= HY_SPLIT_MIN_LEN:
            table, cols = _filter_dft_tables(L)
        else:
            table, _, cols = _plain_dft_tables(L)
        consts += [feats, t, cols, _bf(jnp.asarray(table))]
        const_specs += [full(feats.shape), full(t.shape), full(cols.shape), full(table.shape)]
        nc, nf = _filter_coef_shape(L)
        out_shapes.append(jax.ShapeDtypeStruct((nc, nf, D_MODEL), F32))
        block = (lambda s: (0, 0, jnp.minimum(s, tiles - 1))) if k == 0 else (
            lambda s: (0, 0, jnp.maximum(s - tiles, 0)))
        out_specs.append(pl.BlockSpec((nc, nf, dt), block))
        scratch.append(pltpu.VMEM((L, FILT_W), F32))
    b1 = pe_b1.reshape(1, FILT_W)
    b2 = pe_b2.reshape(1, FILT_W)
    dec = decay.reshape(1, 2 * D_MODEL)
    scratch.append(pltpu.VMEM((LANES, FILT_W), F32))
    return pl.pallas_call(
        _with_casts(functools.partial(_prep_kernel, lens=tuple(lens), tiles=tiles), 21, 3, casts),
        out_shape=(jax.ShapeDtypeStruct((DEPTH, MOD_ROWS, 6 * D_MODEL), F32), *out_shapes,
                   *[k.out_shape for k in casts]),
        grid=(DEPTH * tiles,),
        in_specs=[
            full((1, D_MODEL)), full(c.shape),
            pl.BlockSpec((None, D_MODEL, tn), lambda s: (s // tiles, 0, s % tiles)),
            pl.BlockSpec((DEPTH, tn), lambda s: (0, s % tiles)),
            *const_specs,
            full((PE_DIM, FILT_W)), full((1, FILT_W)), full((FILT_W, FILT_W)), full((1, FILT_W)),
            full((2, FILT_W)),
            pl.BlockSpec((FILT_W, dt), lambda s: (0, s % tiles)),
            pl.BlockSpec((FILT_W, dt), lambda s: (0, tiles + s % tiles)),
            pl.BlockSpec((1, dt), lambda s: (0, s % tiles)),
            pl.BlockSpec((1, dt), lambda s: (0, tiles + s % tiles)),
            *[k.in_spec for k in casts],
        ],
        out_specs=(pl.BlockSpec((None, MOD_ROWS, tn), lambda s: (s // tiles, 0, s % tiles)), *out_specs,
                   *[k.out_spec for k in casts]),
        scratch_shapes=[pltpu.VMEM((MOD_ROWS, D_MODEL), F32), *scratch],
        compiler_params=_params("arbitrary"),
        name="prep",
    )(c_ctx.reshape(1, D_MODEL), c, ada_w, ada_b, *consts, pe_w1, b1, pe_w2, b2, sin_freq, pe_w3, pe_w3, dec, dec,
      *[k.w for k in casts])


def _conv3(z, cw_ref, cb_ref, cols, seq_len, interior=True):
    n, cn = z.shape
    row = lax.broadcasted_iota(jnp.int32, (SUBLANES, cn), 0)
    zm = pltpu.roll(z, 1, 0)
    zp = pltpu.roll(z, n - 1, 0)
    w0 = cw_ref[0:1, cols]
    w1 = cw_ref[1:2, cols]
    w2 = cw_ref[2:3, cols]
    b = cb_ref[:, cols]
    out = zm * w0 + z * w1 + zp * w2 + b
    pieces = []
    for s in range(n // seq_len):
        lo = s * seq_len
        hi = lo + seq_len - SUBLANES
        head = (jnp.where(row == 0, 0.0, zm[lo:lo + SUBLANES]) * w0 + z[lo:lo + SUBLANES] * w1
                + zp[lo:lo + SUBLANES] * w2 + b)
        tail = (zm[hi:hi + SUBLANES] * w0 + z[hi:hi + SUBLANES] * w1
                + jnp.where(row == SUBLANES - 1, 0.0, zp[hi:hi + SUBLANES]) * w2 + b)
        if lo > 0 and interior is not True:
            head = jnp.where(interior, head, out[lo:lo + SUBLANES])
        if hi + SUBLANES < n and interior is not True:
            tail = jnp.where(interior, tail, out[hi:hi + SUBLANES])
        pieces += [head, out[lo + SUBLANES:hi], tail]
    return jnp.concatenate(pieces, axis=0)


def _hy_in_kernel(xc_ref, xl_ref, mod_ref, w_ref, cw_ref, cb_ref, z0_ref, vx_ref, *, ctx_steps, ctx_len, cn):
    i = pl.program_id(0)
    is_ctx = i < ctx_steps
    tm = z0_ref.shape[0]
    mod = mod_ref[pl.ds(_mod_row(i, ctx_steps, tm, tm), 1), :]
    x = _pick(is_ctx, (xc_ref, xl_ref), (slice(None), slice(None)))
    h = _bf(x * (1.0 + mod[:, D_MODEL:2 * D_MODEL]) + mod[:, 0:D_MODEL])

    def conv(col):
        cols = slice(col, col + cn)
        return _conv3(_dot(h, w_ref[:, cols]), cw_ref, cb_ref, cols, ctx_len, interior=is_ctx)

    for c in range(D_MODEL // cn):
        z0_ref[:, c * cn:(c + 1) * cn] = _bf(_dot(h, w_ref[:, c * cn:(c + 1) * cn]))
        x1 = conv(D_MODEL + c * cn)
        v = conv(2 * D_MODEL + c * cn)
        vx_ref[:, c * cn:(c + 1) * cn] = _bf(v * x1)


def _hy_in(x_ctx, x_lat, ctx_len, lat_len, mods, layer, w_in, sconv_w, sconv_b, casts=()):
    tm = lat_len
    assert tm % ctx_len == 0 and x_ctx.shape[0] % tm == 0
    ctx_steps = x_ctx.shape[0] // tm
    n = x_ctx.shape[0] + x_lat.shape[0]
    tok = pl.BlockSpec((tm, D_MODEL), lambda i: (i, 0))
    const = lambda shape: pl.BlockSpec(shape, lambda i: (0, 0), pipeline_mode=pl.Buffered(1))
    act = jax.ShapeDtypeStruct((n, D_MODEL), BF16)
    body = functools.partial(_hy_in_kernel, ctx_steps=ctx_steps, ctx_len=ctx_len, cn=HY_IN_COLS)
    return pl.pallas_call(
        _with_casts(body, 6, 2, casts),
        out_shape=(act, act) + tuple(c.out_shape for c in casts),
        grid=(n // tm,),
        in_specs=[
            pl.BlockSpec((tm, D_MODEL), lambda i: (_ctx_block(i, ctx_steps), 0)),
            pl.BlockSpec((tm, D_MODEL), lambda i: (_lat_block(i, ctx_steps), 0)),
            pl.BlockSpec((None, MOD_ROWS, 6 * D_MODEL), lambda i: (layer, 0, 0)),
            const((D_MODEL, 3 * D_MODEL)),
            pl.BlockSpec((3, 3 * D_MODEL), lambda i: (0, 0)),
            pl.BlockSpec((1, 3 * D_MODEL), lambda i: (0, 0)),
        ] + [c.in_spec for c in casts],
        out_specs=(tok, tok) + tuple(c.out_spec for c in casts),
        compiler_params=_params("arbitrary"),
        name="hyena_in",
    )(x_ctx, x_lat, mods, w_in, sconv_w, sconv_b, *[c.w for c in casts])


def _hy_conv_plain_kernel(vx_ref, z0_ref, cw_ref, cb_ref, coef_ref, bias_ref, fwd_ref, inv_ref, u_ref, *,
                          L, nseq, cn):
    pieces = [(c, s) for c in range(vx_ref.shape[1] // cn) for s in range(nseq)]

    def forward(piece):
        c, s = piece
        vx = vx_ref[s * L:(s + 1) * L, c * cn:(c + 1) * cn]
        return vx, _dot(fwd_ref[...], vx)

    pending = [forward(p) for p in pieces[:HY_CONV_LOOKAHEAD]]
    for i, (c, s) in enumerate(pieces):
        if i + HY_CONV_LOOKAHEAD < len(pieces):
            pending.append(forward(pieces[i + HY_CONV_LOOKAHEAD]))
        vx, spec = pending.pop(0)
        rows = slice(s * L, (s + 1) * L)
        cols = slice(c * cn, (c + 1) * cn)
        k_cos = coef_ref[0, :, cols]
        k_sin = coef_ref[1, :, cols]
        k_mix = coef_ref[2, :, cols]
        v_cos = spec[0:L]
        v_sin = spec[L:2 * L]
        y_spec = _bf(jnp.concatenate([v_cos * k_cos - v_sin * k_sin, v_cos * k_sin + v_sin * k_mix], axis=0))
        y = _dot(inv_ref[...], y_spec)
        x0 = _conv3(z0_ref[rows, cols].astype(F32), cw_ref, cb_ref, cols, L)
        u_ref[rows, cols] = _bf((y + bias_ref[:, cols] * vx.astype(F32)) * x0)


def _hy_conv_split_kernel(vx_ref, z0_ref, cw_ref, cb_ref, coef_ref, bias_ref, fe_ref, fo_ref, ge_ref, go_ref,
                          u_ref, sv_ref, sx_ref, su_ref, *, L, nseq, cn):
    half = L // 2
    for g in range(vx_ref.shape[1] // LANES):
        lanes = slice(g * LANES, (g + 1) * LANES)
        sv_ref[g] = vx_ref[:, lanes].astype(F32)
        sx_ref[g] = _conv3(z0_ref[:, lanes].astype(F32), cw_ref, cb_ref, lanes, L)

    def load(ref, start, c):
        return jnp.concatenate([ref[g, pl.ds(start, half, stride=2), :]
                                for g in range(c * cn // LANES, (c + 1) * cn // LANES)], axis=1)

    def store(ref, start, c, val):
        for k, g in enumerate(range(c * cn // LANES, (c + 1) * cn // LANES)):
            ref[g, pl.ds(start, half, stride=2), :] = val[:, k * LANES:(k + 1) * LANES]

    pieces = [(c, s) for c in range(vx_ref.shape[1] // cn) for s in range(nseq)]

    def forward(piece):
        c, s = piece
        v_e = load(sv_ref, s * L, c)
        v_o = load(sv_ref, s * L + 1, c)
        return v_e, v_o, _dot(fe_ref[...], _bf(v_e)), _dot(fo_ref[...], _bf(v_o))

    pending = [forward(p) for p in pieces[:HY_CONV_LOOKAHEAD]]
    for i, (c, s) in enumerate(pieces):
        if i + HY_CONV_LOOKAHEAD < len(pieces):
            pending.append(forward(pieces[i + HY_CONV_LOOKAHEAD]))
        v_e, v_o, a, b = pending.pop(0)
        cols = slice(c * cn, (c + 1) * cn)
        k_c, k_s, u_c, u_s, e2, e3, f2, f3 = [coef_ref[k, :, cols] for k in range(coef_ref.shape[0])]
        s_c, s_s = a[0:half] + b[0:half], a[half:L] + b[half:L]
        d_c, d_s = a[0:half] - b[0:half], a[half:L] - b[half:L]
        t1 = s_c * k_c - s_s * k_s
        t2 = d_c * u_c + d_s * u_s
        q1 = s_c * k_s
        q4 = d_c * u_s
        p_spec = jnp.concatenate([t1 + t2, q1 + s_s * e2 + d_s * e3 - q4], axis=0)
        m_spec = jnp.concatenate([t1 - t2, q1 + s_s * f2 - d_s * f3 + q4], axis=0)
        y_e = _dot(ge_ref[...], _bf(p_spec))
        y_o = _dot(go_ref[...], _bf(m_spec))
        bias = bias_ref[:, cols]
        store(su_ref, s * L, c, (y_e + bias * v_e) * load(sx_ref, s * L, c))
        store(su_ref, s * L + 1, c, (y_o + bias * v_o) * load(sx_ref, s * L + 1, c))
        rows = slice(s * L, (s + 1) * L)
        for g in range(c * cn // LANES, (c + 1) * cn // LANES):
            u_ref[rows, g * LANES:(g + 1) * LANES] = _bf(su_ref[g, rows, :])


def _hy_conv_steps(n):
    nt = n // HY_CONV_TOKENS
    return (D_MODEL // HY_CONV_CHANNELS) * nt, lambda j, i: j * nt + i


def _hy_conv(vx, z0, sconv_w, sconv_b, tok0, n, coef, bias, L, casts=()):
    tm = HY_CONV_TOKENS
    dt = HY_CONV_CHANNELS
    assert tok0 % tm == 0 and n % tm == 0 and tm % L == 0
    src = pl.BlockSpec((tm, dt), lambda j, i: (tok0 // tm + i, j))
    dst = pl.BlockSpec((tm, dt), lambda j, i: (i, j))
    const = lambda shape: pl.BlockSpec(shape, lambda j, i: (0, 0), pipeline_mode=pl.Buffered(1))
    if L >= HY_SPLIT_MIN_LEN:
        fwd, inv = _dft_tables(L)
        tables = (*fwd, *inv)
        body = functools.partial(_hy_conv_split_kernel, L=L, nseq=tm // L, cn=HY_CONV_PIECE_CHANNELS)
        scratch = [pltpu.VMEM((dt // LANES, tm, LANES), F32)] * 3
    else:
        tables = _plain_dft_tables(L)[:2]
        body = functools.partial(_hy_conv_plain_kernel, L=L, nseq=tm // L, cn=HY_CONV_PIECE_CHANNELS)
        scratch = []
    return pl.pallas_call(
        _with_casts(body, 6 + len(tables), 1, casts),
        out_shape=(jax.ShapeDtypeStruct((n, D_MODEL), BF16),) + tuple(c.out_shape for c in casts),
        grid=(D_MODEL // dt, n // tm),
        in_specs=[
            src, src,
            pl.BlockSpec((3, dt), lambda j, i: (0, j)),
            pl.BlockSpec((1, dt), lambda j, i: (0, j)),
            pl.BlockSpec(_filter_coef_shape(L) + (dt,), lambda j, i: (0, 0, j)),
            pl.BlockSpec((1, dt), lambda j, i: (0, j)),
        ] + [const(t.shape) for t in tables] + [c.in_spec for c in casts],
        out_specs=(dst,) + tuple(c.out_spec for c in casts),
        scratch_shapes=scratch,
        compiler_params=_params("arbitrary", "arbitrary"),
        name=f"hyena_conv_{L}",
    )(vx, z0, sconv_w, sconv_b, coef, bias, *[_bf(jnp.asarray(t)) for t in tables], *[c.w for c in casts])


def _post_kernel(*refs, layer, ctx_steps, lat_len, n_x, n_a, a_feature_major):
    x_refs, a_refs = refs[:n_x], refs[n_x:n_x + n_a]
    (mod_ref, wo_ref, ln1g_ref, ln1b_ref, win_ref, wout_ref, ln2g_ref, ln2b_ref, o_ref,
     act_ref) = refs[n_x + n_a:]
    d = D_MODEL
    i = pl.program_id(0)
    is_ctx = i < ctx_steps
    tm = act_ref.shape[0]
    mod = mod_ref[pl.ds(_mod_row(i, ctx_steps, tm, lat_len), 1), :]
    g1 = mod[:, 2 * d:3 * d]
    sh2 = mod[:, 3 * d:4 * d]
    sc2 = mod[:, 4 * d:5 * d]
    g2 = mod[:, 5 * d:6 * d]
    ln1 = (ln1g_ref[layer:layer + 1, :], ln1b_ref[layer:layer + 1, :])
    ln2 = (ln2g_ref[layer:layer + 1, :], ln2b_ref[layer:layer + 1, :])

    groups = [slice(r, r + POST_GROUP_ROWS) for r in range(0, tm, POST_GROUP_ROWS)]
    if a_feature_major:
        mix = [_dot_tn(_pick(is_ctx, a_refs, (slice(None), rows)), wo_ref[...]) for rows in groups]
    else:
        mix = [_dot(_pick(is_ctx, a_refs, (rows, slice(None))), wo_ref[...]) for rows in groups]
    x1 = []
    for rows, mx in zip(groups, mix):
        x = _pick(is_ctx, x_refs, (rows, slice(None)))
        x1.append(_layer_norm(ALPHA * x + g1 * mx, *ln1))
        h = _bf(x1[-1] * (1.0 + sc2) + sh2)
        for c in range(D_FF // FFN_COLS):
            cols = slice(c * FFN_COLS, (c + 1) * FFN_COLS)
            gate = _dot(h, win_ref[:, cols])
            up = _dot(h, win_ref[:, D_FF + c * FFN_COLS:D_FF + (c + 1) * FFN_COLS])
            act_ref[rows, cols] = _bf(gate * jax.nn.sigmoid(gate) * up)
    for rows, xr in zip(groups, x1):
        f = _dot(act_ref[rows, :], wout_ref[...])
        o_ref[rows, :] = _layer_norm(ALPHA * xr + g2 * f, *ln2)


def _post(xs, mixes, n, n_ctx, lat_len, mods, layer, w_o, w_in, w_out, ln1_g, ln1_b, ln2_g, ln2_b, name,
          x_tok0=0, a_feature_major=False, casts=(), tm=POST_TOKENS):
    ctx_steps = n_ctx // tm
    ctx = lambda i: _ctx_block(i, ctx_steps)
    lat = lambda i: _lat_block(i, ctx_steps)
    blocks = lambda m, tok0: (lambda i: tok0 // tm + i,) if m == 1 else (ctx, lat)
    tok = lambda f: pl.BlockSpec((tm, D_MODEL), lambda i: (f(i), 0))
    feat = lambda f: pl.BlockSpec((D_MODEL, tm), lambda i: (0, f(i)))
    const = lambda shape: pl.BlockSpec(shape, lambda i: (0, 0), pipeline_mode=pl.Buffered(1))
    vec = pl.BlockSpec((DEPTH, D_MODEL), lambda i: (0, 0))
    body = functools.partial(_post_kernel, layer=layer, ctx_steps=ctx_steps, lat_len=lat_len, n_x=len(xs),
                             n_a=len(mixes), a_feature_major=a_feature_major)
    return pl.pallas_call(
        _with_casts(body, len(xs) + len(mixes) + 8, 1, casts),
        out_shape=(jax.ShapeDtypeStruct((n, D_MODEL), F32),) + tuple(c.out_shape for c in casts),
        grid=(n // tm,),
        in_specs=[tok(f) for f in blocks(len(xs), x_tok0)]
        + [(feat if a_feature_major else tok)(f) for f in blocks(len(mixes), 0)]
        + [pl.BlockSpec((None, MOD_ROWS, 6 * D_MODEL), lambda i: (layer, 0, 0)),
           const((D_MODEL, D_MODEL)), vec, vec,
           const((D_MODEL, 2 * D_FF)), const((D_FF, D_MODEL)), vec, vec]
        + [c.in_spec for c in casts],
        out_specs=(tok(lambda i: i),) + tuple(c.out_spec for c in casts),
        scratch_shapes=[pltpu.VMEM((tm, D_FF), BF16)],
        compiler_params=_params("arbitrary"),
        name=name,
    )(*xs, *mixes, mods, w_o, ln1_g, ln1_b, w_in, w_out, ln2_g, ln2_b, *[c.w for c in casts])


def _qkv_kernel(x_ref, mod_ref, wt_ref, qkvt_ref, nk_ref, nv_ref, *, ctx_steps, lat_len):
    d = D_MODEL
    i = pl.program_id(0)
    tm = x_ref.shape[0]
    mod = mod_ref[pl.ds(_mod_row(i, ctx_steps, tm, lat_len), 1), :]
    h = _bf(x_ref[...] * (1.0 + mod[:, d:2 * d]) + mod[:, 0:d])
    qkvt_ref[0:d, :] = _bf(_dot_nt(wt_ref[0:d, :], h) * (HEAD_DIM ** -0.5 * LOG2E))
    kt = _dot_nt(wt_ref[d:2 * d, :], h)
    qkvt_ref[d:2 * d, :] = _bf(kt)
    vt = _dot_nt(wt_ref[2 * d:3 * d, :], h)
    qkvt_ref[2 * d:3 * d, :] = _bf(vt)

    @pl.when(i < ctx_steps)
    def _():
        seq = nk_ref.shape[4]
        for s in range(tm // seq):
            nk_ref[s, 0] = kt[:, s * seq:(s + 1) * seq].reshape(N_HEADS, HEAD_DIM, seq)
            nv_ref[s, 0] = vt[:, s * seq:(s + 1) * seq].reshape(N_HEADS, HEAD_DIM, seq)


def _qkv(x2d, n_ctx, ctx_len, lat_len, mods, layer, w_qkv_t):
    n = x2d.shape[0]
    tm = QKV_TOKENS
    assert tm % ctx_len == 0 and n_ctx % tm == 0
    ctx_steps = n_ctx // tm
    kv_shape = jax.ShapeDtypeStruct((n_ctx // ctx_len, 1, N_HEADS, HEAD_DIM, ctx_len), F32)
    kv_spec = pl.BlockSpec((tm // ctx_len, 1, N_HEADS, HEAD_DIM, ctx_len),
                           lambda i: (_ctx_block(i, ctx_steps), 0, 0, 0, 0))
    return pl.pallas_call(
        functools.partial(_qkv_kernel, ctx_steps=ctx_steps, lat_len=lat_len),
        out_shape=(jax.ShapeDtypeStruct((3 * D_MODEL, n), BF16), kv_shape, kv_shape),
        grid=(n // tm,),
        in_specs=[
            pl.BlockSpec((tm, D_MODEL), lambda i: (i, 0)),
            pl.BlockSpec((None, MOD_ROWS, 6 * D_MODEL), lambda i: (layer, 0, 0)),
            pl.BlockSpec((3 * D_MODEL, D_MODEL), lambda i: (0, 0), pipeline_mode=pl.Buffered(1)),
        ],
        out_specs=(pl.BlockSpec((3 * D_MODEL, tm), lambda i: (0, i)), kv_spec, kv_spec),
        compiler_params=_params("arbitrary"),
        name="qkv",
    )(x2d, mods, w_qkv_t)


def _softmax_weights(*scores):
    m = functools.reduce(jnp.maximum, [jnp.max(s, axis=0, keepdims=True) for s in scores])
    return [_bf(jnp.exp2(s - m)) for s in scores]


def _weighted_values(vt, p):
    ones = jnp.ones((BF16_ROWS, vt.shape[1]), BF16)
    return _dot(jnp.concatenate([vt, ones], axis=0), p)


def _normalise(acc):
    return _bf(acc[0:HEAD_DIM] * (1.0 / acc[HEAD_DIM:HEAD_DIM + 1]))


def _ctx_attn_kernel(qt_ref, kt_ref, vt_ref, ot_ref, *, seq):
    units = [(s, hh) for s in range(qt_ref.shape[1] // seq) for hh in range(N_HEADS)]

    def block(unit):
        s, hh = unit
        return slice(hh * HEAD_DIM, (hh + 1) * HEAD_DIM), slice(s * seq, (s + 1) * seq)

    def scores(unit):
        return _dot_tn(kt_ref[block(unit)], qt_ref[block(unit)])

    pending = [scores(u) for u in units[:CTX_ATTN_LOOKAHEAD]]
    for i, unit in enumerate(units):
        if i + CTX_ATTN_LOOKAHEAD < len(units):
            pending.append(scores(units[i + CTX_ATTN_LOOKAHEAD]))
        (p,) = _softmax_weights(pending.pop(0))
        ot_ref[block(unit)] = _normalise(_weighted_values(vt_ref[block(unit)], p))


def _ctx_attn(qkvt, n_ctx, seq):
    tm = CTX_ATTN_SEQS * seq
    return pl.pallas_call(
        functools.partial(_ctx_attn_kernel, seq=seq),
        out_shape=jax.ShapeDtypeStruct((D_MODEL, n_ctx), BF16),
        grid=(n_ctx // tm,),
        in_specs=[pl.BlockSpec((D_MODEL, tm), lambda b, c=c: (c, b)) for c in range(3)],
        out_specs=pl.BlockSpec((D_MODEL, tm), lambda b: (0, b)),
        compiler_params=_params("arbitrary"),
        name="ctx_attn",
    )(qkvt, qkvt, qkvt)


LAT_ROWS = 16
Q_BLOCK_ROWS = 4
KEY_TILE_ROWS = 4


def _key_band(qb):
    row0 = [min(max(r - WIN_H // 2, 0), LAT_ROWS - WIN_H) for r in range(qb * Q_BLOCK_ROWS, (qb + 1) * Q_BLOCK_ROWS)]
    lo = min(row0) // KEY_TILE_ROWS * KEY_TILE_ROWS
    hi = -(-(max(row0) + WIN_H) // KEY_TILE_ROWS) * KEY_TILE_ROWS
    return lo, hi


KEY_BANDS = tuple(_key_band(qb) for qb in range(LAT_ROWS // Q_BLOCK_ROWS))
MAX_BAND_ROWS = max(hi - lo for lo, hi in KEY_BANDS)


def _fill_bias_table(rpb_ref, tab_ref):
    shape = (GRID_W, LANES)
    kc = lax.broadcasted_iota(jnp.int32, shape, 0)
    lane = lax.broadcasted_iota(jnp.int32, shape, 1)
    qc = lane % GRID_W
    win0 = jnp.clip(qc - WIN_W // 2, 0, GRID_W - WIN_W)
    col_ok = (kc >= win0) & (kc < win0 + WIN_W)
    masked = jnp.full(shape, MASK_VALUE, F32)
    tiles = []
    for dr in range(2 * WIN_H - 1):
        row = jnp.broadcast_to(rpb_ref[dr:dr + 1, :] * LOG2E, shape)
        t = pltpu.roll(row, LANES - (WIN_W - 1), 1, stride=1, stride_axis=0)
        t = jnp.where(lane < GRID_W, t, pltpu.roll(t, GRID_W, 1))
        tiles.append(jnp.where(col_ok, t, MASK_VALUE))

    def tile(qr, kr):
        row0 = min(max(qr - WIN_H // 2, 0), LAT_ROWS - WIN_H)
        if row0 <= kr < row0 + WIN_H:
            return tiles[kr - qr + WIN_H - 1]
        return masked

    for qr in range(0, LAT_ROWS, 2):
        lo, hi = KEY_BANDS[qr // Q_BLOCK_ROWS]
        for i in range(hi - lo):
            both = jnp.where(lane < GRID_W, tile(qr, lo + i), tile(qr + 1, lo + i))
            tab_ref[i * GRID_W:(i + 1) * GRID_W, qr * GRID_W:(qr + 2) * GRID_W] = both


def _lat_attn_kernel(qt_ref, kt_ref, vt_ref, ckt_ref, cvt_ref, rpb_ref, ot_ref, tab_ref):
    seq = LAT_ROWS * GRID_W
    nq = Q_BLOCK_ROWS * GRID_W
    heads = qt_ref.shape[0] // HEAD_DIM
    units = [(b, hh, qb) for b in range(qt_ref.shape[1] // seq) for hh in range(heads)
             for qb in range(len(KEY_BANDS))]

    def rows(hh):
        return slice(hh * HEAD_DIM, (hh + 1) * HEAD_DIM)

    def keys(b, qb):
        lo, hi = KEY_BANDS[qb]
        return slice(b * seq + lo * GRID_W, b * seq + hi * GRID_W)

    def queries(b, qb):
        return slice(b * seq + qb * nq, b * seq + (qb + 1) * nq)

    def scores(unit):
        b, hh, qb = unit
        lo, hi = KEY_BANDS[qb]
        qt = qt_ref[rows(hh), queries(b, qb)]
        s_loc = (_dot_tn(kt_ref[rows(hh), keys(b, qb)], qt)
                 + tab_ref[hh, 0:(hi - lo) * GRID_W, qb * nq:(qb + 1) * nq])
        s_ctx = _dot_tn(_bf(ckt_ref[b, hh]), qt)
        return s_loc, s_ctx

    for hh in range(tab_ref.shape[0]):
        _fill_bias_table(rpb_ref.at[hh], tab_ref.at[hh])
    pending = [scores(u) for u in units[:LAT_ATTN_LOOKAHEAD]]
    for i, (b, hh, qb) in enumerate(units):
        if i + LAT_ATTN_LOOKAHEAD < len(units):
            pending.append(scores(units[i + LAT_ATTN_LOOKAHEAD]))
        s_loc, s_ctx = pending.pop(0)
        p_loc, p_ctx = _softmax_weights(s_loc, s_ctx)
        acc = (_weighted_values(vt_ref[rows(hh), keys(b, qb)], p_loc)
               + _weighted_values(_bf(cvt_ref[b, hh]), p_ctx))
        ot_ref[rows(hh), queries(b, qb)] = _normalise(acc)


def _lat_attn(qkvt, n_ctx, cache_kt, cache_vt, rpb, layer_idx):
    rpb = jnp.pad(rpb[:, :, ::-1], ((0, 0), (0, 1), (0, LANES - (2 * WIN_W - 1))))
    n = qkvt.shape[1] - n_ctx
    seq = LAT_ROWS * GRID_W
    assert n_ctx % n == 0 and n % seq == 0
    nb = n // seq
    hs = LAT_ATTN_HEADS
    nhp = N_HEADS // hs
    past = cache_kt.shape[4]
    feat = lambda c: pl.BlockSpec((hs * HEAD_DIM, n), lambda hp, c=c: (c * nhp + hp, n_ctx // n))
    cache = pl.BlockSpec((nb, None, hs, HEAD_DIM, past), lambda hp: (0, layer_idx, hp, 0, 0))
    return pl.pallas_call(
        _lat_attn_kernel,
        out_shape=jax.ShapeDtypeStruct((D_MODEL, n), BF16),
        grid=(nhp,),
        in_specs=[feat(0), feat(1), feat(2), cache, cache,
                  pl.BlockSpec((hs, 2 * WIN_H, LANES), lambda hp: (hp, 0, 0))],
        out_specs=pl.BlockSpec((hs * HEAD_DIM, n), lambda hp: (hp, 0)),
        scratch_shapes=[pltpu.VMEM((hs, MAX_BAND_ROWS * GRID_W, seq), F32)],
        compiler_params=_params("arbitrary"),
        name="lat_attn",
    )(qkvt, qkvt, qkvt, cache_kt, cache_vt, rpb)


def kernel(x_prompt, x_sample, cache_k, cache_v, c, c_ctx, ada_w, ada_b, ln1_g, ln1_b, ln2_g, ln2_b, ffn_w_in, ffn_w_out, hy_w_in, hy_sconv_w, hy_sconv_b, hy_pe_w1, hy_pe_b1, hy_pe_w2, hy_pe_b2, hy_pe_w3, hy_sin_freq, hy_decay, hy_bias, hy_w_out, na_w_qkv, na_rpb, na_w_out):
    nbp, lp, d = x_prompt.shape
    nbs, ls, _ = x_sample.shape
    assert d == D_MODEL and ls == LAT_ROWS * GRID_W and c.shape[0] == nbs
    xp = x_prompt.reshape(nbp * lp, d)
    xs = x_sample.reshape(nbs * ls, d)
    n_ctx, n_lat = xp.shape[0], xs.shape[0]
    n = n_ctx + n_lat
    ln = (ln1_g, ln1_b, ln2_g, ln2_b)

    filt = (hy_pe_w1[0], hy_pe_b1[0], hy_pe_w2[0], hy_pe_b2[0], hy_pe_w3[0], hy_sin_freq[0], hy_decay[0])
    steps = DEPTH * D_MODEL // FILTER_CHANNELS
    mods, coef_p, coef_s, hy_w_in0 = _prep(c_ctx, c, ada_w, ada_b, (lp, ls), *filt,
                                           casts=[_Cast(hy_w_in, 0, steps, _step)])

    sconv_b = hy_sconv_b[0].reshape(1, -1)
    hy_bias0 = hy_bias[0].reshape(1, -1)
    sconv = (hy_sconv_w[0], sconv_b)
    z0, vx, ffn_w_in0, ffn_w_out0 = _hy_in(xp, xs, lp, ls, mods, 0, hy_w_in0, *sconv,
                                           casts=[_Cast(ffn_w_in, 0, n // ls, _step),
                                                  _Cast(ffn_w_out, 0, n // ls, _step)])
    steps, lin = _hy_conv_steps(n_ctx)
    u_p, hy_w_out0 = _hy_conv(vx, z0, *sconv, 0, n_ctx, coef_p, hy_bias0, lp,
                              casts=[_Cast(hy_w_out, 0, steps, lin)])
    (u_s,) = _hy_conv(vx, z0, *sconv, n_ctx, n_lat, coef_s, hy_bias0, ls)
    steps = n // POST_TOKENS
    x1, ffn_w_in1, ffn_w_out1, na_w_out0, w_qkv_t = _post(
        (xp, xs), (u_p, u_s), n, n_ctx, ls, mods, 0, hy_w_out0, ffn_w_in0, ffn_w_out0, *ln, name="post0",
        casts=[_Cast(ffn_w_in, 1, steps, _step), _Cast(ffn_w_out, 1, steps, _step),
               _Cast(na_w_out, 0, steps, _step),
               _Cast(na_w_qkv, 0, steps // 2, lambda i: i // 2, transpose=True)])

    qkvt, new_kt, new_vt = _qkv(x1, n_ctx, lp, ls, mods, 1, w_qkv_t)
    at_p = _ctx_attn(qkvt, n_ctx, lp)
    at_s = _lat_attn(qkvt, n_ctx, jnp.swapaxes(cache_k, 3, 4), jnp.swapaxes(cache_v, 3, 4), na_rpb[0], 0)
    ffn1 = (na_w_out0, ffn_w_in1, ffn_w_out1, *ln)
    (yp,) = _post((x1,), (at_p,), n_ctx, n_ctx, ls, mods, 1, *ffn1, name="post1_ctx", a_feature_major=True)
    (ys,) = _post((x1,), (at_s,), n_lat, 0, ls, mods, 1, *ffn1, name="post1_lat", x_tok0=n_ctx,
                  a_feature_major=True)

    return (yp.reshape(nbp, lp, d), ys.reshape(nbs, ls, d),
            jnp.swapaxes(new_kt, 3, 4), jnp.swapaxes(new_vt, 3, 4))
```

```python
import functools
import math

import jax
import jax.numpy as jnp
import numpy as np
from jax import lax
from jax.experimental import pallas as pl
from jax.experimental.pallas import tpu as pltpu

D_MODEL = 1024
N_HEADS = 16
HEAD_DIM = D_MODEL // N_HEADS
D_FF = 2816
GRID_W = 64
WIN_H = 8
WIN_W = 16
N_BANDS = 16
PE_DIM = 1 + 2 * N_BANDS
FILT_W = 64
MOD_SHIFT = 0.05
DEPTH = 2
ALPHA = (2 * DEPTH) ** 0.25
LN_EPS = 1e-5
MASK_VALUE = -1e30
MOD_ROWS = 8

LANES = 128
SUBLANES = 8
BF16_ROWS = 2 * SUBLANES
CTX_ATTN_LOOKAHEAD = 6
CTX_ATTN_SEQS = 4
LAT_ATTN_HEADS = 4
LAT_ATTN_LOOKAHEAD = 2
LOG2E = math.log2(math.e)
VMEM_LIMIT = 56 * 1024 * 1024

ADA_COLS = 1536
FILTER_CHANNELS = 256
HY_IN_COLS = 256
HY_CONV_TOKENS = 1024
HY_CONV_CHANNELS = 512
HY_CONV_PIECE_CHANNELS = 256
HY_CONV_LOOKAHEAD = 2
POST_TOKENS = 512
POST_GROUP_ROWS = 256
FFN_COLS = 256
QKV_TOKENS = 1024

F32 = jnp.float32
BF16 = jnp.bfloat16


def _bf(x):
    return x.astype(BF16)


def _dot(a, b):
    return jnp.dot(a, b, preferred_element_type=F32)


def _dot_nt(a, b):
    return lax.dot_general(a, b, (((1,), (1,)), ((), ())), preferred_element_type=F32)


def _dot_tn(a, b):
    return lax.dot_general(a, b, (((0,), (0,)), ((), ())), preferred_element_type=F32)


def _dot3(a, b):
    a_hi = _bf(a)
    a_lo = _bf(a - a_hi.astype(F32))
    b_hi = _bf(b)
    b_lo = _bf(b - b_hi.astype(F32))
    m = a.shape[0]
    both = _dot(jnp.concatenate([a_hi, a_lo], axis=0), b_hi)
    return both[0:m] + both[m:2 * m] + _dot(a_hi, b_lo)


def _layer_norm(x, g, b):
    mu = jnp.mean(x, axis=-1, keepdims=True)
    xc = x - mu
    var = jnp.mean(xc * xc, axis=-1, keepdims=True)
    return xc * lax.rsqrt(var + LN_EPS) * g + b


def _params(*sem):
    return pltpu.CompilerParams(dimension_semantics=sem, vmem_limit_bytes=VMEM_LIMIT)


class _Cast:
    def __init__(self, w, layer, steps, step_fn, transpose=False):
        _, r, c = w.shape
        assert r % (steps * BF16_ROWS) == 0
        rows = r // steps
        self.w = w
        self.transpose = transpose
        self.in_spec = pl.BlockSpec((None, rows, c), lambda *g: (layer, step_fn(*g), 0))
        if transpose:
            self.out_shape = jax.ShapeDtypeStruct((c, r), BF16)
            self.out_spec = pl.BlockSpec((c, rows), lambda *g: (0, step_fn(*g)))
        else:
            self.out_shape = jax.ShapeDtypeStruct((r, c), BF16)
            self.out_spec = pl.BlockSpec((rows, c), lambda *g: (step_fn(*g), 0))


def _with_casts(body, n_in, n_out, casts):
    nc = len(casts)

    def kernel_fn(*refs):
        ins, src = refs[:n_in], refs[n_in:n_in + nc]
        outs, dst = refs[n_in + nc:n_in + nc + n_out], refs[n_in + nc + n_out:n_in + 2 * nc + n_out]
        for cast, s, d in zip(casts, src, dst):
            d[...] = _bf(s[...].T if cast.transpose else s[...])
        body(*ins, *outs, *refs[n_in + 2 * nc + n_out:])

    return kernel_fn


def _step(i):
    return i


def _ctx_block(i, ctx_steps):
    return jnp.minimum(i, ctx_steps - 1)


def _lat_block(i, ctx_steps):
    return jnp.maximum(i - ctx_steps, 0)


def _mod_row(i, ctx_steps, tm, lat_len):
    return jnp.where(i < ctx_steps, 0, 1 + ((i - ctx_steps) * tm) // lat_len)


def _pick(is_ctx, refs, idx):
    if len(refs) == 1:
        return refs[0][idx]
    return jnp.where(is_ctx, refs[0][idx], refs[1][idx])


@functools.lru_cache(maxsize=None)
def _plain_dft_tables(L):
    n = 2 * L
    f = np.arange(L)[:, None]
    s = np.arange(L)[None, :]
    ang = 2.0 * np.pi * ((f * s) % n) / n
    c = np.cos(ang)
    sn = np.sin(ang)
    sn[0, :] = np.where(np.arange(L) % 2 == 0, 1.0, -1.0)
    fwd = np.concatenate([c, sn], axis=0)
    w = np.full((2 * L, 1), 2.0 / n)
    w[0, 0] = 1.0 / n
    w[L, 0] = 1.0 / n
    inv = (fwd * w).T
    alt = np.where(np.arange(L) % 2 == 0, 1.0, -1.0).astype(np.float32)[:, None]
    return (np.ascontiguousarray(fwd).astype(np.float32), np.ascontiguousarray(inv).astype(np.float32), alt)


@functools.lru_cache(maxsize=None)
def _dft_tables(L):
    n = 2 * L
    half = L // 2
    f = np.arange(half)[:, None]
    m = np.arange(half)[None, :]
    alt = np.where(np.arange(half) % 2 == 0, 1.0, -1.0)
    w = np.full((L, 1), 2.0 / n)
    w[0, 0] = 1.0 / n
    w[half, 0] = 1.0 / n
    fwd, inv = [], []
    for parity in range(2):
        ang = 2.0 * np.pi * ((f * (2 * m + parity)) % n) / n
        sn = np.sin(ang)
        sn[0, :] = alt
        table = np.concatenate([np.cos(ang), sn], axis=0)
        fwd.append(np.ascontiguousarray(table).astype(np.float32))
        inv.append(np.ascontiguousarray((table * w).T).astype(np.float32))
    return fwd, inv


@functools.lru_cache(maxsize=None)
def _filter_dft_tables(L):
    n = 2 * L
    f = np.arange(L // 2)[:, None]
    t = np.arange(L)[None, :]
    ang = 2.0 * np.pi * ((f * t) % n) / n
    table = np.concatenate([np.cos(ang), np.sin(ang)], axis=0).astype(np.float32)
    tt = np.arange(L)
    cols = np.stack([np.where(tt % 2 == 0, 1.0, -1.0), np.cos(np.pi * (tt % 4) / 2.0),
                     np.sin(np.pi * (tt % 4) / 2.0)], axis=1)
    return np.ascontiguousarray(table), np.round(cols).astype(np.float32)


@functools.lru_cache(maxsize=None)
def _filter_features(L):
    t = np.linspace(0.0, 1.0, L, dtype=np.float32)[:, None]
    w = (np.float32(2.0 * math.pi / L) * np.arange(L, dtype=np.float32))[:, None]
    bands = np.linspace(1e-4, N_BANDS - 1, N_BANDS, dtype=np.float32)[None, :]
    arg = (bands * w).astype(np.float64)
    feats = np.concatenate([t.astype(np.float64), np.cos(arg), np.sin(arg)], axis=-1)
    out = np.zeros((L, LANES), np.float32)
    out[:, :PE_DIM] = feats
    return out, t


HY_SPLIT_MIN_LEN = 512


def _filter_coef_shape(L):
    return (8, L // 2) if L >= HY_SPLIT_MIN_LEN else (3, L)


def _ada_tile(layer, cctx_ref, c_ref, w_ref, b_ref, o_ref, cvec_ref):
    nb = c_ref.shape[0]
    cvec_ref[...] = jnp.zeros_like(cvec_ref)
    cvec_ref[0:1, :] = cctx_ref[...]
    cvec_ref[1:1 + nb, :] = c_ref[...]
    c = cvec_ref[...]
    o_ref[...] = _dot3(c * jax.nn.sigmoid(c), w_ref[...]) + b_ref[pl.ds(layer, 1), :]


def _filter_tile(is_first, feats_ref, t_ref, cols_ref, tab_ref, w1_ref, b1_ref, w2_ref, b2_ref, freq_ref,
                 w3f_ref, w3b_ref, decf_ref, decb_ref, out_ref, h_ref, w1p_ref, *, L):
    hi = lax.Precision.HIGHEST

    @pl.when(is_first)
    def _():
        w1p_ref[...] = jnp.zeros_like(w1p_ref)
        w1p_ref[0:PE_DIM, :] = w1_ref[...]
        h1 = jnp.sin(freq_ref[0:1, :] * (jnp.dot(feats_ref[...], w1p_ref[...], precision=hi,
                                                 preferred_element_type=F32) + b1_ref[...]))
        h_ref[...] = jnp.sin(freq_ref[1:2, :] * (jnp.dot(h1, w2_ref[...], precision=hi,
                                                         preferred_element_type=F32) + b2_ref[...]))

    h = h_ref[...]
    t = t_ref[...]
    kf = _dot3(h, w3f_ref[...]) * (jnp.exp(-t * jnp.abs(decf_ref[...])) + MOD_SHIFT)
    kb = _dot3(h, w3b_ref[...]) * (jnp.exp(-t * jnp.abs(decb_ref[...])) + MOD_SHIFT)
    row = lax.broadcasted_iota(jnp.int32, kf.shape, 0)
    kb = jnp.where(row == 0, 0.0, kb)
    ksum = kf + kb
    kdiff = kf - kb
    alt = cols_ref[:, 0:1]

    def spectrum(table, x):
        hi = _bf(x)
        return _dot(table, hi) + _dot(table, _bf(x - hi.astype(F32)))

    if L < HY_SPLIT_MIN_LEN:
        k_c = spectrum(tab_ref[0:L, :], ksum)
        k_s = spectrum(tab_ref[L:2 * L, :], kdiff)
        k_nyq = jnp.sum(ksum * alt, axis=0, keepdims=True)
        out_ref[0] = k_c
        out_ref[1] = jnp.where(row == 0, 0.0, k_s)
        out_ref[2] = jnp.where(row == 0, k_nyq, k_c)
        return

    half = L // 2
    cos_m = tab_ref[0:half, :]
    sin_m = tab_ref[half:L, :]
    k_c = spectrum(cos_m, ksum)
    k_s = spectrum(sin_m, kdiff)
    u_c = spectrum(cos_m, alt * ksum)
    u_s = -spectrum(sin_m, alt * kdiff)
    kc_h = jnp.sum(ksum * cols_ref[:, 1:2], axis=0, keepdims=True)
    ks_h = jnp.sum(kdiff * cols_ref[:, 2:3], axis=0, keepdims=True)
    first = lax.broadcasted_iota(jnp.int32, k_c.shape, 0) == 0
    out_ref[0] = k_c
    out_ref[1] = k_s
    out_ref[2] = u_c
    out_ref[3] = u_s
    out_ref[4] = jnp.where(first, kc_h - ks_h, k_c)
    out_ref[5] = jnp.where(first, kc_h + ks_h, u_c)
    out_ref[6] = jnp.where(first, kc_h + ks_h, k_c)
    out_ref[7] = jnp.where(first, kc_h - ks_h, u_c)


def _prep_kernel(cctx_ref, c_ref, w_ref, b_ref, fa_ref, ta_ref, ca_ref, tba_ref, fb_ref, tb_ref, cb_ref,
                 tbb_ref, w1_ref, b1_ref, w2_ref, b2_ref, freq_ref, w3f_ref, w3b_ref, decf_ref, decb_ref,
                 mod_ref, coefa_ref, coefb_ref, cvec_ref, ha_ref, hb_ref, w1p_ref, *, lens, tiles):
    s = pl.program_id(0)
    _ada_tile(s // tiles, cctx_ref, c_ref, w_ref, b_ref, mod_ref, cvec_ref)
    shared = (w1_ref, b1_ref, w2_ref, b2_ref, freq_ref, w3f_ref, w3b_ref, decf_ref, decb_ref)

    @pl.when(s < tiles)
    def _():
        _filter_tile(s == 0, fa_ref, ta_ref, ca_ref, tba_ref, *shared, coefa_ref, ha_ref, w1p_ref, L=lens[0])

    @pl.when(s >= tiles)
    def _():
        _filter_tile(s == tiles, fb_ref, tb_ref, cb_ref, tbb_ref, *shared, coefb_ref, hb_ref, w1p_ref, L=lens[1])


def _prep(c_ctx, c, ada_w, ada_b, lens, pe_w1, pe_b1, pe_w2, pe_b2, pe_w3, sin_freq, decay, casts=()):
    assert 1 + c.shape[0] <= MOD_ROWS and DEPTH == 2
    tn = ADA_COLS
    dt = FILTER_CHANNELS
    tiles = D_MODEL // dt
    assert 6 * D_MODEL // tn == tiles
    full = lambda shape: pl.BlockSpec(shape, lambda s: (0,) * len(shape))
    consts, const_specs, out_shapes, out_specs, scratch = [], [], [], [], []
    for k, L in enumerate(lens):
        feats, t = _filter_features(L)
        if L >= HY_SPLIT_MIN_LEN:
            table, cols = _filter_dft_tables(L)
        else:
            table, _, cols = _plain_dft_tables(L)
        consts += [feats, t, cols, _bf(jnp.asarray(table))]
        const_specs += [full(feats.shape), full(t.shape), full(cols.shape), full(table.shape)]
        nc, nf = _filter_coef_shape(L)
        out_shapes.append(jax.ShapeDtypeStruct((nc, nf, D_MODEL), F32))
        block = (lambda s: (0, 0, jnp.minimum(s, tiles - 1))) if k == 0 else (
            lambda s: (0, 0, jnp.maximum(s - tiles, 0)))
        out_specs.append(pl.BlockSpec((nc, nf, dt), block))
        scratch.append(pltpu.VMEM((L, FILT_W), F32))
    b1 = pe_b1.reshape(1, FILT_W)
    b2 = pe_b2.reshape(1, FILT_W)
    dec = decay.reshape(1, 2 * D_MODEL)
    scratch.append(pltpu.VMEM((LANES, FILT_W), F32))
    return pl.pallas_call(
        _with_casts(functools.partial(_prep_kernel, lens=tuple(lens), tiles=tiles), 21, 3, casts),
        out_shape=(jax.ShapeDtypeStruct((DEPTH, MOD_ROWS, 6 * D_MODEL), F32), *out_shapes,
                   *[k.out_shape for k in casts]),
        grid=(DEPTH * tiles,),
        in_specs=[
            full((1, D_MODEL)), full(c.shape),
            pl.BlockSpec((None, D_MODEL, tn), lambda s: (s // tiles, 0, s % tiles)),
            pl.BlockSpec((DEPTH, tn), lambda s: (0, s % tiles)),
            *const_specs,
            full((PE_DIM, FILT_W)), full((1, FILT_W)), full((FILT_W, FILT_W)), full((1, FILT_W)),
            full((2, FILT_W)),
            pl.BlockSpec((FILT_W, dt), lambda s: (0, s % tiles)),
            pl.BlockSpec((FILT_W, dt), lambda s: (0, tiles + s % tiles)),
            pl.BlockSpec((1, dt), lambda s: (0, s % tiles)),
            pl.BlockSpec((1, dt), lambda s: (0, tiles + s % tiles)),
            *[k.in_spec for k in casts],
        ],
        out_specs=(pl.BlockSpec((None, MOD_ROWS, tn), lambda s: (s // tiles, 0, s % tiles)), *out_specs,
                   *[k.out_spec for k in casts]),
        scratch_shapes=[pltpu.VMEM((MOD_ROWS, D_MODEL), F32), *scratch],
        compiler_params=_params("arbitrary"),
        name="prep",
    )(c_ctx.reshape(1, D_MODEL), c, ada_w, ada_b, *consts, pe_w1, b1, pe_w2, b2, sin_freq, pe_w3, pe_w3, dec, dec,
      *[k.w for k in casts])


def _conv3(z, cw_ref, cb_ref, cols, seq_len, interior=True):
    n, cn = z.shape
    row = lax.broadcasted_iota(jnp.int32, (SUBLANES, cn), 0)
    zm = pltpu.roll(z, 1, 0)
    zp = pltpu.roll(z, n - 1, 0)
    w0 = cw_ref[0:1, cols]
    w1 = cw_ref[1:2, cols]
    w2 = cw_ref[2:3, cols]
    b = cb_ref[:, cols]
    out = zm * w0 + z * w1 + zp * w2 + b
    pieces = []
    for s in range(n // seq_len):
        lo = s * seq_len
        hi = lo + seq_len - SUBLANES
        head = (jnp.where(row == 0, 0.0, zm[lo:lo + SUBLANES]) * w0 + z[lo:lo + SUBLANES] * w1
                + zp[lo:lo + SUBLANES] * w2 + b)
        tail = (zm[hi:hi + SUBLANES] * w0 + z[hi:hi + SUBLANES] * w1
                + jnp.where(row == SUBLANES - 1, 0.0, zp[hi:hi + SUBLANES]) * w2 + b)
        if lo > 0 and interior is not True:
            head = jnp.where(interior, head, out[lo:lo + SUBLANES])
        if hi + SUBLANES < n and interior is not True:
            tail = jnp.where(interior, tail, out[hi:hi + SUBLANES])
        pieces += [head, out[lo + SUBLANES:hi], tail]
    return jnp.concatenate(pieces, axis=0)


def _hy_in_kernel(xc_ref, xl_ref, mod_ref, w_ref, cw_ref, cb_ref, z0_ref, vx_ref, *, ctx_steps, ctx_len, cn):
    i = pl.program_id(0)
    is_ctx = i < ctx_steps
    tm = z0_ref.shape[0]
    mod = mod_ref[pl.ds(_mod_row(i, ctx_steps, tm, tm), 1), :]
    x = _pick(is_ctx, (xc_ref, xl_ref), (slice(None), slice(None)))
    h = _bf(x * (1.0 + mod[:, D_MODEL:2 * D_MODEL]) + mod[:, 0:D_MODEL])

    def conv(col):
        cols = slice(col, col + cn)
        return _conv3(_dot(h, w_ref[:, cols]), cw_ref, cb_ref, cols, ctx_len, interior=is_ctx)

    for c in range(D_MODEL // cn):
        z0_ref[:, c * cn:(c + 1) * cn] = _bf(_dot(h, w_ref[:, c * cn:(c + 1) * cn]))
        x1 = conv(D_MODEL + c * cn)
        v = conv(2 * D_MODEL + c * cn)
        vx_ref[:, c * cn:(c + 1) * cn] = _bf(v * x1)


def _hy_in(x_ctx, x_lat, ctx_len, lat_len, mods, layer, w_in, sconv_w, sconv_b, casts=()):
    tm = lat_len
    assert tm % ctx_len == 0 and x_ctx.shape[0] % tm == 0
    ctx_steps = x_ctx.shape[0] // tm
    n = x_ctx.shape[0] + x_lat.shape[0]
    tok = pl.BlockSpec((tm, D_MODEL), lambda i: (i, 0))
    const = lambda shape: pl.BlockSpec(shape, lambda i: (0, 0), pipeline_mode=pl.Buffered(1))
    act = jax.ShapeDtypeStruct((n, D_MODEL), BF16)
    body = functools.partial(_hy_in_kernel, ctx_steps=ctx_steps, ctx_len=ctx_len, cn=HY_IN_COLS)
    return pl.pallas_call(
        _with_casts(body, 6, 2, casts),
        out_shape=(act, act) + tuple(c.out_shape for c in casts),
        grid=(n // tm,),
        in_specs=[
            pl.BlockSpec((tm, D_MODEL), lambda i: (_ctx_block(i, ctx_steps), 0)),
            pl.BlockSpec((tm, D_MODEL), lambda i: (_lat_block(i, ctx_steps), 0)),
            pl.BlockSpec((None, MOD_ROWS, 6 * D_MODEL), lambda i: (layer, 0, 0)),
            const((D_MODEL, 3 * D_MODEL)),
            pl.BlockSpec((3, 3 * D_MODEL), lambda i: (0, 0)),
            pl.BlockSpec((1, 3 * D_MODEL), lambda i: (0, 0)),
        ] + [c.in_spec for c in casts],
        out_specs=(tok, tok) + tuple(c.out_spec for c in casts),
        compiler_params=_params("arbitrary"),
        name="hyena_in",
    )(x_ctx, x_lat, mods, w_in, sconv_w, sconv_b, *[c.w for c in casts])


def _hy_conv_plain_kernel(vx_ref, z0_ref, cw_ref, cb_ref, coef_ref, bias_ref, fwd_ref, inv_ref, u_ref, *,
                          L, nseq, cn):
    pieces = [(c, s) for c in range(vx_ref.shape[1] // cn) for s in range(nseq)]

    def forward(piece):
        c, s = piece
        vx = vx_ref[s * L:(s + 1) * L, c * cn:(c + 1) * cn]
        return vx, _dot(fwd_ref[...], vx)

    pending = [forward(p) for p in pieces[:HY_CONV_LOOKAHEAD]]
    for i, (c, s) in enumerate(pieces):
        if i + HY_CONV_LOOKAHEAD < len(pieces):
            pending.append(forward(pieces[i + HY_CONV_LOOKAHEAD]))
        vx, spec = pending.pop(0)
        rows = slice(s * L, (s + 1) * L)
        cols = slice(c * cn, (c + 1) * cn)
        k_cos = coef_ref[0, :, cols]
        k_sin = coef_ref[1, :, cols]
        k_mix = coef_ref[2, :, cols]
        v_cos = spec[0:L]
        v_sin = spec[L:2 * L]
        y_spec = _bf(jnp.concatenate([v_cos * k_cos - v_sin * k_sin, v_cos * k_sin + v_sin * k_mix], axis=0))
        y = _dot(inv_ref[...], y_spec)
        x0 = _conv3(z0_ref[rows, cols].astype(F32), cw_ref, cb_ref, cols, L)
        u_ref[rows, cols] = _bf((y + bias_ref[:, cols] * vx.astype(F32)) * x0)


def _hy_conv_split_kernel(vx_ref, z0_ref, cw_ref, cb_ref, coef_ref, bias_ref, fe_ref, fo_ref, ge_ref, go_ref,
                          u_ref, sv_ref, sx_ref, su_ref, *, L, nseq, cn):
    half = L // 2
    for g in range(vx_ref.shape[1] // LANES):
        lanes = slice(g * LANES, (g + 1) * LANES)
        sv_ref[g] = vx_ref[:, lanes].astype(F32)
        sx_ref[g] = _conv3(z0_ref[:, lanes].astype(F32), cw_ref, cb_ref, lanes, L)

    def load(ref, start, c):
        return jnp.concatenate([ref[g, pl.ds(start, half, stride=2), :]
                                for g in range(c * cn // LANES, (c + 1) * cn // LANES)], axis=1)

    def store(ref, start, c, val):
        for k, g in enumerate(range(c * cn // LANES, (c + 1) * cn // LANES)):
            ref[g, pl.ds(start, half, stride=2), :] = val[:, k * LANES:(k + 1) * LANES]

    pieces = [(c, s) for c in range(vx_ref.shape[1] // cn) for s in range(nseq)]

    def forward(piece):
        c, s = piece
        v_e = load(sv_ref, s * L, c)
        v_o = load(sv_ref, s * L + 1, c)
        return v_e, v_o, _dot(fe_ref[...], _bf(v_e)), _dot(fo_ref[...], _bf(v_o))

    pending = [forward(p) for p in pieces[:HY_CONV_LOOKAHEAD]]
    for i, (c, s) in enumerate(pieces):
        if i + HY_CONV_LOOKAHEAD < len(pieces):
            pending.append(forward(pieces[i + HY_CONV_LOOKAHEAD]))
        v_e, v_o, a, b = pending.pop(0)
        cols = slice(c * cn, (c + 1) * cn)
        k_c, k_s, u_c, u_s, e2, e3, f2, f3 = [coef_ref[k, :, cols] for k in range(coef_ref.shape[0])]
        s_c, s_s = a[0:half] + b[0:half], a[half:L] + b[half:L]
        d_c, d_s = a[0:half] - b[0:half], a[half:L] - b[half:L]
        t1 = s_c * k_c - s_s * k_s
        t2 = d_c * u_c + d_s * u_s
        q1 = s_c * k_s
        q4 = d_c * u_s
        p_spec = jnp.concatenate([t1 + t2, q1 + s_s * e2 + d_s * e3 - q4], axis=0)
        m_spec = jnp.concatenate([t1 - t2, q1 + s_s * f2 - d_s * f3 + q4], axis=0)
        y_e = _dot(ge_ref[...], _bf(p_spec))
        y_o = _dot(go_ref[...], _bf(m_spec))
        bias = bias_ref[:, cols]
        store(su_ref, s * L, c, (y_e + bias * v_e) * load(sx_ref, s * L, c))
        store(su_ref, s * L + 1, c, (y_o + bias * v_o) * load(sx_ref, s * L + 1, c))
        rows = slice(s * L, (s + 1) * L)
        for g in range(c * cn // LANES, (c + 1) * cn // LANES):
            u_ref[rows, g * LANES:(g + 1) * LANES] = _bf(su_ref[g, rows, :])


def _hy_conv_steps(n):
    nt = n // HY_CONV_TOKENS
    return (D_MODEL // HY_CONV_CHANNELS) * nt, lambda j, i: j * nt + i


def _hy_conv(vx, z0, sconv_w, sconv_b, tok0, n, coef, bias, L, casts=()):
    tm = HY_CONV_TOKENS
    dt = HY_CONV_CHANNELS
    assert tok0 % tm == 0 and n % tm == 0 and tm % L == 0
    src = pl.BlockSpec((tm, dt), lambda j, i: (tok0 // tm + i, j))
    dst = pl.BlockSpec((tm, dt), lambda j, i: (i, j))
    const = lambda shape: pl.BlockSpec(shape, lambda j, i: (0, 0), pipeline_mode=pl.Buffered(1))
    if L >= HY_SPLIT_MIN_LEN:
        fwd, inv = _dft_tables(L)
        tables = (*fwd, *inv)
        body = functools.partial(_hy_conv_split_kernel, L=L, nseq=tm // L, cn=HY_CONV_PIECE_CHANNELS)
        scratch = [pltpu.VMEM((dt // LANES, tm, LANES), F32)] * 3
    else:
        tables = _plain_dft_tables(L)[:2]
        body = functools.partial(_hy_conv_plain_kernel, L=L, nseq=tm // L, cn=HY_CONV_PIECE_CHANNELS)
        scratch = []
    return pl.pallas_call(
        _with_casts(body, 6 + len(tables), 1, casts),
        out_shape=(jax.ShapeDtypeStruct((n, D_MODEL), BF16),) + tuple(c.out_shape for c in casts),
        grid=(D_MODEL // dt, n // tm),
        in_specs=[
            src, src,
            pl.BlockSpec((3, dt), lambda j, i: (0, j)),
            pl.BlockSpec((1, dt), lambda j, i: (0, j)),
            pl.BlockSpec(_filter_coef_shape(L) + (dt,), lambda j, i: (0, 0, j)),
            pl.BlockSpec((1, dt), lambda j, i: (0, j)),
        ] + [const(t.shape) for t in tables] + [c.in_spec for c in casts],
        out_specs=(dst,) + tuple(c.out_spec for c in casts),
        scratch_shapes=scratch,
        compiler_params=_params("arbitrary", "arbitrary"),
        name=f"hyena_conv_{L}",
    )(vx, z0, sconv_w, sconv_b, coef, bias, *[_bf(jnp.asarray(t)) for t in tables], *[c.w for c in casts])


def _post_kernel(*refs, layer, ctx_steps, lat_len, n_x, n_a, a_feature_major):
    x_refs, a_refs = refs[:n_x], refs[n_x:n_x + n_a]
    (mod_ref, wo_ref, ln1g_ref, ln1b_ref, win_ref, wout_ref, ln2g_ref, ln2b_ref, o_ref,
     act_ref) = refs[n_x + n_a:]
    d = D_MODEL
    i = pl.program_id(0)
    is_ctx = i < ctx_steps
    tm = act_ref.shape[0]
    mod = mod_ref[pl.ds(_mod_row(i, ctx_steps, tm, lat_len), 1), :]
    g1 = mod[:, 2 * d:3 * d]
    sh2 = mod[:, 3 * d:4 * d]
    sc2 = mod[:, 4 * d:5 * d]
    g2 = mod[:, 5 * d:6 * d]
    ln1 = (ln1g_ref[layer:layer + 1, :], ln1b_ref[layer:layer + 1, :])
    ln2 = (ln2g_ref[layer:layer + 1, :], ln2b_ref[layer:layer + 1, :])

    groups = [slice(r, r + POST_GROUP_ROWS) for r in range(0, tm, POST_GROUP_ROWS)]
    if a_feature_major:
        mix = [_dot_tn(_pick(is_ctx, a_refs, (slice(None), rows)), wo_ref[...]) for rows in groups]
    else:
        mix = [_dot(_pick(is_ctx, a_refs, (rows, slice(None))), wo_ref[...]) for rows in groups]
    x1 = []
    for rows, mx in zip(groups, mix):
        x = _pick(is_ctx, x_refs, (rows, slice(None)))
        x1.append(_layer_norm(ALPHA * x + g1 * mx, *ln1))
        h = _bf(x1[-1] * (1.0 + sc2) + sh2)
        for c in range(D_FF // FFN_COLS):
            cols = slice(c * FFN_COLS, (c + 1) * FFN_COLS)
            gate = _dot(h, win_ref[:, cols])
            up = _dot(h, win_ref[:, D_FF + c * FFN_COLS:D_FF + (c + 1) * FFN_COLS])
            act_ref[rows, cols] = _bf(gate * jax.nn.sigmoid(gate) * up)
    for rows, xr in zip(groups, x1):
        f = _dot(act_ref[rows, :], wout_ref[...])
        o_ref[rows, :] = _layer_norm(ALPHA * xr + g2 * f, *ln2)


def _post(xs, mixes, n, n_ctx, lat_len, mods, layer, w_o, w_in, w_out, ln1_g, ln1_b, ln2_g, ln2_b, name,
          x_tok0=0, a_feature_major=False, casts=(), tm=POST_TOKENS):
    ctx_steps = n_ctx // tm
    ctx = lambda i: _ctx_block(i, ctx_steps)
    lat = lambda i: _lat_block(i, ctx_steps)
    blocks = lambda m, tok0: (lambda i: tok0 // tm + i,) if m == 1 else (ctx, lat)
    tok = lambda f: pl.BlockSpec((tm, D_MODEL), lambda i: (f(i), 0))
    feat = lambda f: pl.BlockSpec((D_MODEL, tm), lambda i: (0, f(i)))
    const = lambda shape: pl.BlockSpec(shape, lambda i: (0, 0), pipeline_mode=pl.Buffered(1))
    vec = pl.BlockSpec((DEPTH, D_MODEL), lambda i: (0, 0))
    body = functools.partial(_post_kernel, layer=layer, ctx_steps=ctx_steps, lat_len=lat_len, n_x=len(xs),
                             n_a=len(mixes), a_feature_major=a_feature_major)
    return pl.pallas_call(
        _with_casts(body, len(xs) + len(mixes) + 8, 1, casts),
        out_shape=(jax.ShapeDtypeStruct((n, D_MODEL), F32),) + tuple(c.out_shape for c in casts),
        grid=(n // tm,),
        in_specs=[tok(f) for f in blocks(len(xs), x_tok0)]
        + [(feat if a_feature_major else tok)(f) for f in blocks(len(mixes), 0)]
        + [pl.BlockSpec((None, MOD_ROWS, 6 * D_MODEL), lambda i: (layer, 0, 0)),
           const((D_MODEL, D_MODEL)), vec, vec,
           const((D_MODEL, 2 * D_FF)), const((D_FF, D_MODEL)), vec, vec]
        + [c.in_spec for c in casts],
        out_specs=(tok(lambda i: i),) + tuple(c.out_spec for c in casts),
        scratch_shapes=[pltpu.VMEM((tm, D_FF), BF16)],
        compiler_params=_params("arbitrary"),
        name=name,
    )(*xs, *mixes, mods, w_o, ln1_g, ln1_b, w_in, w_out, ln2_g, ln2_b, *[c.w for c in casts])


def _qkv_kernel(x_ref, mod_ref, wt_ref, qkvt_ref, nk_ref, nv_ref, *, ctx_steps, lat_len):
    d = D_MODEL
    i = pl.program_id(0)
    tm = x_ref.shape[0]
    mod = mod_ref[pl.ds(_mod_row(i, ctx_steps, tm, lat_len), 1), :]
    h = _bf(x_ref[...] * (1.0 + mod[:, d:2 * d]) + mod[:, 0:d])
    qkvt_ref[0:d, :] = _bf(_dot_nt(wt_ref[0:d, :], h) * (HEAD_DIM ** -0.5 * LOG2E))
    kt = _dot_nt(wt_ref[d:2 * d, :], h)
    qkvt_ref[d:2 * d, :] = _bf(kt)
    vt = _dot_nt(wt_ref[2 * d:3 * d, :], h)
    qkvt_ref[2 * d:3 * d, :] = _bf(vt)

    @pl.when(i < ctx_steps)
    def _():
        seq = nk_ref.shape[4]
        for s in range(tm // seq):
            nk_ref[s, 0] = kt[:, s * seq:(s + 1) * seq].reshape(N_HEADS, HEAD_DIM, seq)
            nv_ref[s, 0] = vt[:, s * seq:(s + 1) * seq].reshape(N_HEADS, HEAD_DIM, seq)


def _qkv(x2d, n_ctx, ctx_len, lat_len, mods, layer, w_qkv_t):
    n = x2d.shape[0]
    tm = QKV_TOKENS
    assert tm % ctx_len == 0 and n_ctx % tm == 0
    ctx_steps = n_ctx // tm
    kv_shape = jax.ShapeDtypeStruct((n_ctx // ctx_len, 1, N_HEADS, HEAD_DIM, ctx_len), F32)
    kv_spec = pl.BlockSpec((tm // ctx_len, 1, N_HEADS, HEAD_DIM, ctx_len),
                           lambda i: (_ctx_block(i, ctx_steps), 0, 0, 0, 0))
    return pl.pallas_call(
        functools.partial(_qkv_kernel, ctx_steps=ctx_steps, lat_len=lat_len),
        out_shape=(jax.ShapeDtypeStruct((3 * D_MODEL, n), BF16), kv_shape, kv_shape),
        grid=(n // tm,),
        in_specs=[
            pl.BlockSpec((tm, D_MODEL), lambda i: (i, 0)),
            pl.BlockSpec((None, MOD_ROWS, 6 * D_MODEL), lambda i: (layer, 0, 0)),
            pl.BlockSpec((3 * D_MODEL, D_MODEL), lambda i: (0, 0), pipeline_mode=pl.Buffered(1)),
        ],
        out_specs=(pl.BlockSpec((3 * D_MODEL, tm), lambda i: (0, i)), kv_spec, kv_spec),
        compiler_params=_params("arbitrary"),
        name="qkv",
    )(x2d, mods, w_qkv_t)


def _softmax_weights(*scores):
    m = functools.reduce(jnp.maximum, [jnp.max(s, axis=0, keepdims=True) for s in scores])
    return [_bf(jnp.exp2(s - m)) for s in scores]


def _weighted_values(vt, p):
    ones = jnp.ones((BF16_ROWS, vt.shape[1]), BF16)
    return _dot(jnp.concatenate([vt, ones], axis=0), p)


def _normalise(acc):
    return _bf(acc[0:HEAD_DIM] * (1.0 / acc[HEAD_DIM:HEAD_DIM + 1]))


def _ctx_attn_kernel(qt_ref, kt_ref, vt_ref, ot_ref, *, seq):
    units = [(s, hh) for s in range(qt_ref.shape[1] // seq) for hh in range(N_HEADS)]

    def block(unit):
        s, hh = unit
        return slice(hh * HEAD_DIM, (hh + 1) * HEAD_DIM), slice(s * seq, (s + 1) * seq)

    def scores(unit):
        return _dot_tn(kt_ref[block(unit)], qt_ref[block(unit)])

    pending = [scores(u) for u in units[:CTX_ATTN_LOOKAHEAD]]
    for i, unit in enumerate(units):
        if i + CTX_ATTN_LOOKAHEAD < len(units):
            pending.append(scores(units[i + CTX_ATTN_LOOKAHEAD]))
        (p,) = _softmax_weights(pending.pop(0))
        ot_ref[block(unit)] = _normalise(_weighted_values(vt_ref[block(unit)], p))


def _ctx_attn(qkvt, n_ctx, seq):
    tm = CTX_ATTN_SEQS * seq
    return pl.pallas_call(
        functools.partial(_ctx_attn_kernel, seq=seq),
        out_shape=jax.ShapeDtypeStruct((D_MODEL, n_ctx), BF16),
        grid=(n_ctx // tm,),
        in_specs=[pl.BlockSpec((D_MODEL, tm), lambda b, c=c: (c, b)) for c in range(3)],
        out_specs=pl.BlockSpec((D_MODEL, tm), lambda b: (0, b)),
        compiler_params=_params("arbitrary"),
        name="ctx_attn",
    )(qkvt, qkvt, qkvt)


LAT_ROWS = 16
Q_BLOCK_ROWS = 4


def _key_band(qb):
    row0 = [min(max(r - WIN_H // 2, 0), LAT_ROWS - WIN_H) for r in range(qb * Q_BLOCK_ROWS, (qb + 1) * Q_BLOCK_ROWS)]
    return min(row0), max(row0) + WIN_H


KEY_BANDS = tuple(_key_band(qb) for qb in range(LAT_ROWS // Q_BLOCK_ROWS))
MAX_BAND_ROWS = max(hi - lo for lo, hi in KEY_BANDS)


def _fill_bias_table(rpb_ref, tab_ref):
    shape = (GRID_W, LANES)
    kc = lax.broadcasted_iota(jnp.int32, shape, 0)
    lane = lax.broadcasted_iota(jnp.int32, shape, 1)
    qc = lane % GRID_W
    win0 = jnp.clip(qc - WIN_W // 2, 0, GRID_W - WIN_W)
    col_ok = (kc >= win0) & (kc < win0 + WIN_W)
    masked = jnp.full(shape, MASK_VALUE, F32)
    tiles = []
    for dr in range(2 * WIN_H - 1):
        row = jnp.broadcast_to(rpb_ref[dr:dr + 1, :] * LOG2E, shape)
        t = pltpu.roll(row, LANES - (WIN_W - 1), 1, stride=1, stride_axis=0)
        t = jnp.where(lane < GRID_W, t, pltpu.roll(t, GRID_W, 1))
        tiles.append(jnp.where(col_ok, t, MASK_VALUE))

    def tile(qr, kr):
        row0 = min(max(qr - WIN_H // 2, 0), LAT_ROWS - WIN_H)
        if row0 <= kr < row0 + WIN_H:
            return tiles[kr - qr + WIN_H - 1]
        return masked

    for qr in range(0, LAT_ROWS, 2):
        lo, hi = KEY_BANDS[qr // Q_BLOCK_ROWS]
        for i in range(hi - lo):
            both = jnp.where(lane < GRID_W, tile(qr, lo + i), tile(qr + 1, lo + i))
            tab_ref[i * GRID_W:(i + 1) * GRID_W, qr * GRID_W:(qr + 2) * GRID_W] = both


def _lat_attn_kernel(qt_ref, kt_ref, vt_ref, ckt_ref, cvt_ref, rpb_ref, ot_ref, tab_ref):
    seq = LAT_ROWS * GRID_W
    nq = Q_BLOCK_ROWS * GRID_W
    heads = qt_ref.shape[0] // HEAD_DIM
    units = [(b, hh, qb) for b in range(qt_ref.shape[1] // seq) for hh in range(heads)
             for qb in range(len(KEY_BANDS))]

    def rows(hh):
        return slice(hh * HEAD_DIM, (hh + 1) * HEAD_DIM)

    def keys(b, qb):
        lo, hi = KEY_BANDS[qb]
        return slice(b * seq + lo * GRID_W, b * seq + hi * GRID_W)

    def queries(b, qb):
        return slice(b * seq + qb * nq, b * seq + (qb + 1) * nq)

    def scores(unit):
        b, hh, qb = unit
        lo, hi = KEY_BANDS[qb]
        qt = qt_ref[rows(hh), queries(b, qb)]
        s_loc = (_dot_tn(kt_ref[rows(hh), keys(b, qb)], qt)
                 + tab_ref[hh, 0:(hi - lo) * GRID_W, qb * nq:(qb + 1) * nq])
        s_ctx = _dot_tn(_bf(ckt_ref[b, hh]), qt)
        return s_loc, s_ctx

    for hh in range(tab_ref.shape[0]):
        _fill_bias_table(rpb_ref.at[hh], tab_ref.at[hh])
    pending = [scores(u) for u in units[:LAT_ATTN_LOOKAHEAD]]
    for i, (b, hh, qb) in enumerate(units):
        if i + LAT_ATTN_LOOKAHEAD < len(units):
            pending.append(scores(units[i + LAT_ATTN_LOOKAHEAD]))
        s_loc, s_ctx = pending.pop(0)
        p_loc, p_ctx = _softmax_weights(s_loc, s_ctx)
        acc = (_weighted_values(vt_ref[rows(hh), keys(b, qb)], p_loc)
               + _weighted_values(_bf(cvt_ref[b, hh]), p_ctx))
        ot_ref[rows(hh), queries(b, qb)] = _normalise(acc)


def _lat_attn(qkvt, n_ctx, cache_kt, cache_vt, rpb, layer_idx):
    rpb = jnp.pad(rpb[:, :, ::-1], ((0, 0), (0, 1), (0, LANES - (2 * WIN_W - 1))))
    n = qkvt.shape[1] - n_ctx
    seq = LAT_ROWS * GRID_W
    assert n_ctx % n == 0 and n % seq == 0
    nb = n // seq
    hs = LAT_ATTN_HEADS
    nhp = N_HEADS // hs
    past = cache_kt.shape[4]
    feat = lambda c: pl.BlockSpec((hs * HEAD_DIM, n), lambda hp, c=c: (c * nhp + hp, n_ctx // n))
    cache = pl.BlockSpec((nb, None, hs, HEAD_DIM, past), lambda hp: (0, layer_idx, hp, 0, 0))
    return pl.pallas_call(
        _lat_attn_kernel,
        out_shape=jax.ShapeDtypeStruct((D_MODEL, n), BF16),
        grid=(nhp,),
        in_specs=[feat(0), feat(1), feat(2), cache, cache,
                  pl.BlockSpec((hs, 2 * WIN_H, LANES), lambda hp: (hp, 0, 0))],
        out_specs=pl.BlockSpec((hs * HEAD_DIM, n), lambda hp: (hp, 0)),
        scratch_shapes=[pltpu.VMEM((hs, MAX_BAND_ROWS * GRID_W, seq), F32)],
        compiler_params=_params("arbitrary"),
        name="lat_attn",
    )(qkvt, qkvt, qkvt, cache_kt, cache_vt, rpb)


def kernel(x_prompt, x_sample, cache_k, cache_v, c, c_ctx, ada_w, ada_b, ln1_g, ln1_b, ln2_g, ln2_b, ffn_w_in, ffn_w_out, hy_w_in, hy_sconv_w, hy_sconv_b, hy_pe_w1, hy_pe_b1, hy_pe_w2, hy_pe_b2, hy_pe_w3, hy_sin_freq, hy_decay, hy_bias, hy_w_out, na_w_qkv, na_rpb, na_w_out):
    nbp, lp, d = x_prompt.shape
    nbs, ls, _ = x_sample.shape
    assert d == D_MODEL and ls == LAT_ROWS * GRID_W and c.shape[0] == nbs
    xp = x_prompt.reshape(nbp * lp, d)
    xs = x_sample.reshape(nbs * ls, d)
    n_ctx, n_lat = xp.shape[0], xs.shape[0]
    n = n_ctx + n_lat
    ln = (ln1_g, ln1_b, ln2_g, ln2_b)

    filt = (hy_pe_w1[0], hy_pe_b1[0], hy_pe_w2[0], hy_pe_b2[0], hy_pe_w3[0], hy_sin_freq[0], hy_decay[0])
    steps = DEPTH * D_MODEL // FILTER_CHANNELS
    mods, coef_p, coef_s, hy_w_in0 = _prep(c_ctx, c, ada_w, ada_b, (lp, ls), *filt,
                                           casts=[_Cast(hy_w_in, 0, steps, _step)])

    sconv_b = hy_sconv_b[0].reshape(1, -1)
    hy_bias0 = hy_bias[0].reshape(1, -1)
    sconv = (hy_sconv_w[0], sconv_b)
    z0, vx, ffn_w_in0, ffn_w_out0 = _hy_in(xp, xs, lp, ls, mods, 0, hy_w_in0, *sconv,
                                           casts=[_Cast(ffn_w_in, 0, n // ls, _step),
                                                  _Cast(ffn_w_out, 0, n // ls, _step)])
    steps, lin = _hy_conv_steps(n_ctx)
    u_p, hy_w_out0 = _hy_conv(vx, z0, *sconv, 0, n_ctx, coef_p, hy_bias0, lp,
                              casts=[_Cast(hy_w_out, 0, steps, lin)])
    (u_s,) = _hy_conv(vx, z0, *sconv, n_ctx, n_lat, coef_s, hy_bias0, ls)
    steps = n // POST_TOKENS
    x1, ffn_w_in1, ffn_w_out1, na_w_out0, w_qkv_t = _post(
        (xp, xs), (u_p, u_s), n, n_ctx, ls, mods, 0, hy_w_out0, ffn_w_in0, ffn_w_out0, *ln, name="post0",
        casts=[_Cast(ffn_w_in, 1, steps, _step), _Cast(ffn_w_out, 1, steps, _step),
               _Cast(na_w_out, 0, steps, _step),
               _Cast(na_w_qkv, 0, steps // 2, lambda i: i // 2, transpose=True)])

    qkvt, new_kt, new_vt = _qkv(x1, n_ctx, lp, ls, mods, 1, w_qkv_t)
    at_p = _ctx_attn(qkvt, n_ctx, lp)
    at_s = _lat_attn(qkvt, n_ctx, jnp.swapaxes(cache_k, 3, 4), jnp.swapaxes(cache_v, 3, 4), na_rpb[0], 0)
    ffn1 = (na_w_out0, ffn_w_in1, ffn_w_out1, *ln)
    (yp,) = _post((x1,), (at_p,), n_ctx, n_ctx, ls, mods, 1, *ffn1, name="post1_ctx", a_feature_major=True)
    (ys,) = _post((x1,), (at_s,), n_lat, 0, ls, mods, 1, *ffn1, name="post1_lat", x_tok0=n_ctx,
                  a_feature_major=True)

    return (yp.reshape(nbp, lp, d), ys.reshape(nbs, ls, d),
            jnp.swapaxes(new_kt, 3, 4), jnp.swapaxes(new_vt, 3, 4))
```

```python
import functools
import math

import jax
import jax.numpy as jnp
import numpy as np
from jax import lax
from jax.experimental import pallas as pl
from jax.experimental.pallas import tpu as pltpu

D_MODEL = 1024
N_HEADS = 16
HEAD_DIM = D_MODEL // N_HEADS
D_FF = 2816
GRID_W = 64
WIN_H = 8
WIN_W = 16
N_BANDS = 16
PE_DIM = 1 + 2 * N_BANDS
FILT_W = 64
MOD_SHIFT = 0.05
DEPTH = 2
ALPHA = (2 * DEPTH) ** 0.25
LN_EPS = 1e-5
MASK_VALUE = -1e30
MOD_ROWS = 8

LANES = 128
SUBLANES = 8
BF16_ROWS = 2 * SUBLANES
CTX_ATTN_LOOKAHEAD = 6
CTX_ATTN_SEQS = 4
LAT_ATTN_HEADS = 4
LAT_ATTN_LOOKAHEAD = 2
LOG2E = math.log2(math.e)
VMEM_LIMIT = 56 * 1024 * 1024

ADA_COLS = 1536
FILTER_CHANNELS = 256
HY_IN_COLS = 256
HY_CONV_TOKENS = 2048
HY_CONV_CHANNELS = 512
HY_CONV_PIECE_CHANNELS = 256
HY_CONV_LOOKAHEAD = 2
POST_TOKENS = 512
POST_GROUP_ROWS = 256
FFN_COLS = 256
QKV_TOKENS = 1024

F32 = jnp.float32
BF16 = jnp.bfloat16


def _bf(x):
    return x.astype(BF16)


def _dot(a, b):
    return jnp.dot(a, b, preferred_element_type=F32)


def _dot_nt(a, b):
    return lax.dot_general(a, b, (((1,), (1,)), ((), ())), preferred_element_type=F32)


def _dot_tn(a, b):
    return lax.dot_general(a, b, (((0,), (0,)), ((), ())), preferred_element_type=F32)


def _dot3(a, b):
    a_hi = _bf(a)
    a_lo = _bf(a - a_hi.astype(F32))
    b_hi = _bf(b)
    b_lo = _bf(b - b_hi.astype(F32))
    m = a.shape[0]
    both = _dot(jnp.concatenate([a_hi, a_lo], axis=0), b_hi)
    return both[0:m] + both[m:2 * m] + _dot(a_hi, b_lo)


def _layer_norm(x, g, b):
    mu = jnp.mean(x, axis=-1, keepdims=True)
    xc = x - mu
    var = jnp.mean(xc * xc, axis=-1, keepdims=True)
    return xc * lax.rsqrt(var + LN_EPS) * g + b


def _params(*sem):
    return pltpu.CompilerParams(dimension_semantics=sem, vmem_limit_bytes=VMEM_LIMIT)


class _Cast:
    def __init__(self, w, layer, steps, step_fn, transpose=False):
        _, r, c = w.shape
        assert r % (steps * BF16_ROWS) == 0
        rows = r // steps
        self.w = w
        self.transpose = transpose
        self.in_spec = pl.BlockSpec((None, rows, c), lambda *g: (layer, step_fn(*g), 0))
        if transpose:
            self.out_shape = jax.ShapeDtypeStruct((c, r), BF16)
            self.out_spec = pl.BlockSpec((c, rows), lambda *g: (0, step_fn(*g)))
        else:
            self.out_shape = jax.ShapeDtypeStruct((r, c), BF16)
            self.out_spec = pl.BlockSpec((rows, c), lambda *g: (step_fn(*g), 0))


def _with_casts(body, n_in, n_out, casts):
    nc = len(casts)

    def kernel_fn(*refs):
        ins, src = refs[:n_in], refs[n_in:n_in + nc]
        outs, dst = refs[n_in + nc:n_in + nc + n_out], refs[n_in + nc + n_out:n_in + 2 * nc + n_out]
        for cast, s, d in zip(casts, src, dst):
            d[...] = _bf(s[...].T if cast.transpose else s[...])
        body(*ins, *outs, *refs[n_in + 2 * nc + n_out:])

    return kernel_fn


def _step(i):
    return i


def _ctx_block(i, ctx_steps):
    return jnp.minimum(i, ctx_steps - 1)


def _lat_block(i, ctx_steps):
    return jnp.maximum(i - ctx_steps, 0)


def _mod_row(i, ctx_steps, tm, lat_len):
    return jnp.where(i < ctx_steps, 0, 1 + ((i - ctx_steps) * tm) // lat_len)


def _pick(is_ctx, refs, idx):
    if len(refs) == 1:
        return refs[0][idx]
    return jnp.where(is_ctx, refs[0][idx], refs[1][idx])


@functools.lru_cache(maxsize=None)
def _plain_dft_tables(L):
    n = 2 * L
    f = np.arange(L)[:, None]
    s = np.arange(L)[None, :]
    ang = 2.0 * np.pi * ((f * s) % n) / n
    c = np.cos(ang)
    sn = np.sin(ang)
    sn[0, :] = np.where(np.arange(L) % 2 == 0, 1.0, -1.0)
    fwd = np.concatenate([c, sn], axis=0)
    w = np.full((2 * L, 1), 2.0 / n)
    w[0, 0] = 1.0 / n
    w[L, 0] = 1.0 / n
    inv = (fwd * w).T
    alt = np.where(np.arange(L) % 2 == 0, 1.0, -1.0).astype(np.float32)[:, None]
    return (np.ascontiguousarray(fwd).astype(np.float32), np.ascontiguousarray(inv).astype(np.float32), alt)


@functools.lru_cache(maxsize=None)
def _dft_tables(L):
    n = 2 * L
    half = L // 2
    f = np.arange(half)[:, None]
    m = np.arange(half)[None, :]
    alt = np.where(np.arange(half) % 2 == 0, 1.0, -1.0)
    w = np.full((L, 1), 2.0 / n)
    w[0, 0] = 1.0 / n
    w[half, 0] = 1.0 / n
    fwd, inv = [], []
    for parity in range(2):
        ang = 2.0 * np.pi * ((f * (2 * m + parity)) % n) / n
        sn = np.sin(ang)
        sn[0, :] = alt
        table = np.concatenate([np.cos(ang), sn], axis=0)
        fwd.append(np.ascontiguousarray(table).astype(np.float32))
        inv.append(np.ascontiguousarray((table * w).T).astype(np.float32))
    return fwd, inv


@functools.lru_cache(maxsize=None)
def _filter_dft_tables(L):
    n = 2 * L
    f = np.arange(L // 2)[:, None]
    t = np.arange(L)[None, :]
    ang = 2.0 * np.pi * ((f * t) % n) / n
    table = np.concatenate([np.cos(ang), np.sin(ang)], axis=0).astype(np.float32)
    tt = np.arange(L)
    cols = np.stack([np.where(tt % 2 == 0, 1.0, -1.0), np.cos(np.pi * (tt % 4) / 2.0),
                     np.sin(np.pi * (tt % 4) / 2.0)], axis=1)
    return np.ascontiguousarray(table), np.round(cols).astype(np.float32)


@functools.lru_cache(maxsize=None)
def _filter_features(L):
    t = np.linspace(0.0, 1.0, L, dtype=np.float32)[:, None]
    w = (np.float32(2.0 * math.pi / L) * np.arange(L, dtype=np.float32))[:, None]
    bands = np.linspace(1e-4, N_BANDS - 1, N_BANDS, dtype=np.float32)[None, :]
    arg = (bands * w).astype(np.float64)
    feats = np.concatenate([t.astype(np.float64), np.cos(arg), np.sin(arg)], axis=-1)
    out = np.zeros((L, LANES), np.float32)
    out[:, :PE_DIM] = feats
    return out, t


HY_SPLIT_MIN_LEN = 512


def _filter_coef_shape(L):
    return (8, L // 2) if L >= HY_SPLIT_MIN_LEN else (3, L)


def _ada_tile(layer, cctx_ref, c_ref, w_ref, b_ref, o_ref, cvec_ref):
    nb = c_ref.shape[0]
    cvec_ref[...] = jnp.zeros_like(cvec_ref)
    cvec_ref[0:1, :] = cctx_ref[...]
    cvec_ref[1:1 + nb, :] = c_ref[...]
    c = cvec_ref[...]
    o_ref[...] = _dot3(c * jax.nn.sigmoid(c), w_ref[...]) + b_ref[pl.ds(layer, 1), :]


def _filter_tile(is_first, feats_ref, t_ref, cols_ref, tab_ref, w1_ref, b1_ref, w2_ref, b2_ref, freq_ref,
                 w3f_ref, w3b_ref, decf_ref, decb_ref, out_ref, h_ref, w1p_ref, *, L):
    hi = lax.Precision.HIGHEST

    @pl.when(is_first)
    def _():
        w1p_ref[...] = jnp.zeros_like(w1p_ref)
        w1p_ref[0:PE_DIM, :] = w1_ref[...]
        h1 = jnp.sin(freq_ref[0:1, :] * (jnp.dot(feats_ref[...], w1p_ref[...], precision=hi,
                                                 preferred_element_type=F32) + b1_ref[...]))
        h_ref[...] = jnp.sin(freq_ref[1:2, :] * (jnp.dot(h1, w2_ref[...], precision=hi,
                                                         preferred_element_type=F32) + b2_ref[...]))

    h = h_ref[...]
    t = t_ref[...]
    kf = _dot3(h, w3f_ref[...]) * (jnp.exp(-t * jnp.abs(decf_ref[...])) + MOD_SHIFT)
    kb = _dot3(h, w3b_ref[...]) * (jnp.exp(-t * jnp.abs(decb_ref[...])) + MOD_SHIFT)
    row = lax.broadcasted_iota(jnp.int32, kf.shape, 0)
    kb = jnp.where(row == 0, 0.0, kb)
    ksum = kf + kb
    kdiff = kf - kb
    alt = cols_ref[:, 0:1]

    def spectrum(table, x):
        hi = _bf(x)
        return _dot(table, hi) + _dot(table, _bf(x - hi.astype(F32)))

    if L < HY_SPLIT_MIN_LEN:
        k_c = spectrum(tab_ref[0:L, :], ksum)
        k_s = spectrum(tab_ref[L:2 * L, :], kdiff)
        k_nyq = jnp.sum(ksum * alt, axis=0, keepdims=True)
        out_ref[0] = k_c
        out_ref[1] = jnp.where(row == 0, 0.0, k_s)
        out_ref[2] = jnp.where(row == 0, k_nyq, k_c)
        return

    half = L // 2
    cos_m = tab_ref[0:half, :]
    sin_m = tab_ref[half:L, :]
    k_c = spectrum(cos_m, ksum)
    k_s = spectrum(sin_m, kdiff)
    u_c = spectrum(cos_m, alt * ksum)
    u_s = -spectrum(sin_m, alt * kdiff)
    kc_h = jnp.sum(ksum * cols_ref[:, 1:2], axis=0, keepdims=True)
    ks_h = jnp.sum(kdiff * cols_ref[:, 2:3], axis=0, keepdims=True)
    first = lax.broadcasted_iota(jnp.int32, k_c.shape, 0) == 0
    out_ref[0] = k_c
    out_ref[1] = k_s
    out_ref[2] = u_c
    out_ref[3] = u_s
    out_ref[4] = jnp.where(first, kc_h - ks_h, k_c)
    out_ref[5] = jnp.where(first, kc_h + ks_h, u_c)
    out_ref[6] = jnp.where(first, kc_h + ks_h, k_c)
    out_ref[7] = jnp.where(first, kc_h - ks_h, u_c)


def _prep_kernel(cctx_ref, c_ref, w_ref, b_ref, fa_ref, ta_ref, ca_ref, tba_ref, fb_ref, tb_ref, cb_ref,
                 tbb_ref, w1_ref, b1_ref, w2_ref, b2_ref, freq_ref, w3f_ref, w3b_ref, decf_ref, decb_ref,
                 mod_ref, coefa_ref, coefb_ref, cvec_ref, ha_ref, hb_ref, w1p_ref, *, lens, tiles):
    s = pl.program_id(0)
    _ada_tile(s // tiles, cctx_ref, c_ref, w_ref, b_ref, mod_ref, cvec_ref)
    shared = (w1_ref, b1_ref, w2_ref, b2_ref, freq_ref, w3f_ref, w3b_ref, decf_ref, decb_ref)

    @pl.when(s < tiles)
    def _():
        _filter_tile(s == 0, fa_ref, ta_ref, ca_ref, tba_ref, *shared, coefa_ref, ha_ref, w1p_ref, L=lens[0])

    @pl.when(s >= tiles)
    def _():
        _filter_tile(s == tiles, fb_ref, tb_ref, cb_ref, tbb_ref, *shared, coefb_ref, hb_ref, w1p_ref, L=lens[1])


def _prep(c_ctx, c, ada_w, ada_b, lens, pe_w1, pe_b1, pe_w2, pe_b2, pe_w3, sin_freq, decay, casts=()):
    assert 1 + c.shape[0] <= MOD_ROWS and DEPTH == 2
    tn = ADA_COLS
    dt = FILTER_CHANNELS
    tiles = D_MODEL // dt
    assert 6 * D_MODEL // tn == tiles
    full = lambda shape: pl.BlockSpec(shape, lambda s: (0,) * len(shape))
    consts, const_specs, out_shapes, out_specs, scratch = [], [], [], [], []
    for k, L in enumerate(lens):
        feats, t = _filter_features(L)
        if L >= HY_SPLIT_MIN_LEN:
            table, cols = _filter_dft_tables(L)
        else:
            table, _, cols = _plain_dft_tables(L)
        consts += [feats, t, cols, _bf(jnp.asarray(table))]
        const_specs += [full(feats.shape), full(t.shape), full(cols.shape), full(table.shape)]
        nc, nf = _filter_coef_shape(L)
        out_shapes.append(jax.ShapeDtypeStruct((nc, nf, D_MODEL), F32))
        block = (lambda s: (0, 0, jnp.minimum(s, tiles - 1))) if k == 0 else (
            lambda s: (0, 0, jnp.maximum(s - tiles, 0)))
        out_specs.append(pl.BlockSpec((nc, nf, dt), block))
        scratch.append(pltpu.VMEM((L, FILT_W), F32))
    b1 = pe_b1.reshape(1, FILT_W)
    b2 = pe_b2.reshape(1, FILT_W)
    dec = decay.reshape(1, 2 * D_MODEL)
    scratch.append(pltpu.VMEM((LANES, FILT_W), F32))
    return pl.pallas_call(
        _with_casts(functools.partial(_prep_kernel, lens=tuple(lens), tiles=tiles), 21, 3, casts),
        out_shape=(jax.ShapeDtypeStruct((DEPTH, MOD_ROWS, 6 * D_MODEL), F32), *out_shapes,
                   *[k.out_shape for k in casts]),
        grid=(DEPTH * tiles,),
        in_specs=[
            full((1, D_MODEL)), full(c.shape),
            pl.BlockSpec((None, D_MODEL, tn), lambda s: (s // tiles, 0, s % tiles)),
            pl.BlockSpec((DEPTH, tn), lambda s: (0, s % tiles)),
            *const_specs,
            full((PE_DIM, FILT_W)), full((1, FILT_W)), full((FILT_W, FILT_W)), full((1, FILT_W)),
            full((2, FILT_W)),
            pl.BlockSpec((FILT_W, dt), lambda s: (0, s % tiles)),
            pl.BlockSpec((FILT_W, dt), lambda s: (0, tiles + s % tiles)),
            pl.BlockSpec((1, dt), lambda s: (0, s % tiles)),
            pl.BlockSpec((1, dt), lambda s: (0, tiles + s % tiles)),
            *[k.in_spec for k in casts],
        ],
        out_specs=(pl.BlockSpec((None, MOD_ROWS, tn), lambda s: (s // tiles, 0, s % tiles)), *out_specs,
                   *[k.out_spec for k in casts]),
        scratch_shapes=[pltpu.VMEM((MOD_ROWS, D_MODEL), F32), *scratch],
        compiler_params=_params("arbitrary"),
        name="prep",
    )(c_ctx.reshape(1, D_MODEL), c, ada_w, ada_b, *consts, pe_w1, b1, pe_w2, b2, sin_freq, pe_w3, pe_w3, dec, dec,
      *[k.w for k in casts])


def _conv3(z, cw_ref, cb_ref, cols, seq_len, interior=True):
    n, cn = z.shape
    row = lax.broadcasted_iota(jnp.int32, (SUBLANES, cn), 0)
    zm = pltpu.roll(z, 1, 0)
    zp = pltpu.roll(z, n - 1, 0)
    w0 = cw_ref[0:1, cols]
    w1 = cw_ref[1:2, cols]
    w2 = cw_ref[2:3, cols]
    b = cb_ref[:, cols]
    out = zm * w0 + z * w1 + zp * w2 + b
    pieces = []
    for s in range(n // seq_len):
        lo = s * seq_len
        hi = lo + seq_len - SUBLANES
        head = (jnp.where(row == 0, 0.0, zm[lo:lo + SUBLANES]) * w0 + z[lo:lo + SUBLANES] * w1
                + zp[lo:lo + SUBLANES] * w2 + b)
        tail = (zm[hi:hi + SUBLANES] * w0 + z[hi:hi + SUBLANES] * w1
                + jnp.where(row == SUBLANES - 1, 0.0, zp[hi:hi + SUBLANES]) * w2 + b)
        if lo > 0 and interior is not True:
            head = jnp.where(interior, head, out[lo:lo + SUBLANES])
        if hi + SUBLANES < n and interior is not True:
            tail = jnp.where(interior, tail, out[hi:hi + SUBLANES])
        pieces += [head, out[lo + SUBLANES:hi], tail]
    return jnp.concatenate(pieces, axis=0)


def _hy_in_kernel(xc_ref, xl_ref, mod_ref, w_ref, cw_ref, cb_ref, z0_ref, vx_ref, *, ctx_steps, ctx_len, cn):
    i = pl.program_id(0)
    is_ctx = i < ctx_steps
    tm = z0_ref.shape[0]
    mod = mod_ref[pl.ds(_mod_row(i, ctx_steps, tm, tm), 1), :]
    x = _pick(is_ctx, (xc_ref, xl_ref), (slice(None), slice(None)))
    h = _bf(x * (1.0 + mod[:, D_MODEL:2 * D_MODEL]) + mod[:, 0:D_MODEL])

    def conv(col):
        cols = slice(col, col + cn)
        return _conv3(_dot(h, w_ref[:, cols]), cw_ref, cb_ref, cols, ctx_len, interior=is_ctx)

    for c in range(D_MODEL // cn):
        z0_ref[:, c * cn:(c + 1) * cn] = _bf(_dot(h, w_ref[:, c * cn:(c + 1) * cn]))
        x1 = conv(D_MODEL + c * cn)
        v = conv(2 * D_MODEL + c * cn)
        vx_ref[:, c * cn:(c + 1) * cn] = _bf(v * x1)


def _hy_in(x_ctx, x_lat, ctx_len, lat_len, mods, layer, w_in, sconv_w, sconv_b, casts=()):
    tm = lat_len
    assert tm % ctx_len == 0 and x_ctx.shape[0] % tm == 0
    ctx_steps = x_ctx.shape[0] // tm
    n = x_ctx.shape[0] + x_lat.shape[0]
    tok = pl.BlockSpec((tm, D_MODEL), lambda i: (i, 0))
    const = lambda shape: pl.BlockSpec(shape, lambda i: (0, 0), pipeline_mode=pl.Buffered(1))
    act = jax.ShapeDtypeStruct((n, D_MODEL), BF16)
    body = functools.partial(_hy_in_kernel, ctx_steps=ctx_steps, ctx_len=ctx_len, cn=HY_IN_COLS)
    return pl.pallas_call(
        _with_casts(body, 6, 2, casts),
        out_shape=(act, act) + tuple(c.out_shape for c in casts),
        grid=(n // tm,),
        in_specs=[
            pl.BlockSpec((tm, D_MODEL), lambda i: (_ctx_block(i, ctx_steps), 0)),
            pl.BlockSpec((tm, D_MODEL), lambda i: (_lat_block(i, ctx_steps), 0)),
            pl.BlockSpec((None, MOD_ROWS, 6 * D_MODEL), lambda i: (layer, 0, 0)),
            const((D_MODEL, 3 * D_MODEL)),
            pl.BlockSpec((3, 3 * D_MODEL), lambda i: (0, 0)),
            pl.BlockSpec((1, 3 * D_MODEL), lambda i: (0, 0)),
        ] + [c.in_spec for c in casts],
        out_specs=(tok, tok) + tuple(c.out_spec for c in casts),
        compiler_params=_params("arbitrary"),
        name="hyena_in",
    )(x_ctx, x_lat, mods, w_in, sconv_w, sconv_b, *[c.w for c in casts])


def _hy_conv_plain_kernel(vx_ref, z0_ref, cw_ref, cb_ref, coef_ref, bias_ref, fwd_ref, inv_ref, u_ref, *,
                          L, nseq, cn):
    pieces = [(c, s) for c in range(vx_ref.shape[1] // cn) for s in range(nseq)]

    def forward(piece):
        c, s = piece
        vx = vx_ref[s * L:(s + 1) * L, c * cn:(c + 1) * cn]
        return vx, _dot(fwd_ref[...], vx)

    pending = [forward(p) for p in pieces[:HY_CONV_LOOKAHEAD]]
    for i, (c, s) in enumerate(pieces):
        if i + HY_CONV_LOOKAHEAD < len(pieces):
            pending.append(forward(pieces[i + HY_CONV_LOOKAHEAD]))
        vx, spec = pending.pop(0)
        rows = slice(s * L, (s + 1) * L)
        cols = slice(c * cn, (c + 1) * cn)
        k_cos = coef_ref[0, :, cols]
        k_sin = coef_ref[1, :, cols]
        k_mix = coef_ref[2, :, cols]
        v_cos = spec[0:L]
        v_sin = spec[L:2 * L]
        y_spec = _bf(jnp.concatenate([v_cos * k_cos - v_sin * k_sin, v_cos * k_sin + v_sin * k_mix], axis=0))
        y = _dot(inv_ref[...], y_spec)
        x0 = _conv3(z0_ref[rows, cols].astype(F32), cw_ref, cb_ref, cols, L)
        u_ref[rows, cols] = _bf((y + bias_ref[:, cols] * vx.astype(F32)) * x0)


def _hy_conv_split_kernel(vx_ref, z0_ref, cw_ref, cb_ref, coef_ref, bias_ref, fe_ref, fo_ref, ge_ref, go_ref,
                          u_ref, sv_ref, sx_ref, su_ref, *, L, nseq, cn):
    half = L // 2
    for g in range(vx_ref.shape[1] // LANES):
        lanes = slice(g * LANES, (g + 1) * LANES)
        sv_ref[g] = vx_ref[:, lanes].astype(F32)
        sx_ref[g] = _conv3(z0_ref[:, lanes].astype(F32), cw_ref, cb_ref, lanes, L)

    def load(ref, start, c):
        return jnp.concatenate([ref[g, pl.ds(start, half, stride=2), :]
                                for g in range(c * cn // LANES, (c + 1) * cn // LANES)], axis=1)

    def store(ref, start, c, val):
        for k, g in enumerate(range(c * cn // LANES, (c + 1) * cn // LANES)):
            ref[g, pl.ds(start, half, stride=2), :] = val[:, k * LANES:(k + 1) * LANES]

    pieces = [(c, s) for c in range(vx_ref.shape[1] // cn) for s in range(nseq)]

    def forward(piece):
        c, s = piece
        v_e = load(sv_ref, s * L, c)
        v_o = load(sv_ref, s * L + 1, c)
        return v_e, v_o, _dot(fe_ref[...], _bf(v_e)), _dot(fo_ref[...], _bf(v_o))

    pending = [forward(p) for p in pieces[:HY_CONV_LOOKAHEAD]]
    for i, (c, s) in enumerate(pieces):
        if i + HY_CONV_LOOKAHEAD < len(pieces):
            pending.append(forward(pieces[i + HY_CONV_LOOKAHEAD]))
        v_e, v_o, a, b = pending.pop(0)
        cols = slice(c * cn, (c + 1) * cn)
        k_c, k_s, u_c, u_s, e2, e3, f2, f3 = [coef_ref[k, :, cols] for k in range(coef_ref.shape[0])]
        s_c, s_s = a[0:half] + b[0:half], a[half:L] + b[half:L]
        d_c, d_s = a[0:half] - b[0:half], a[half:L] - b[half:L]
        t1 = s_c * k_c - s_s * k_s
        t2 = d_c * u_c + d_s * u_s
        q1 = s_c * k_s
        q4 = d_c * u_s
        p_spec = jnp.concatenate([t1 + t2, q1 + s_s * e2 + d_s * e3 - q4], axis=0)
        m_spec = jnp.concatenate([t1 - t2, q1 + s_s * f2 - d_s * f3 + q4], axis=0)
        y_e = _dot(ge_ref[...], _bf(p_spec))
        y_o = _dot(go_ref[...], _bf(m_spec))
        bias = bias_ref[:, cols]
        store(su_ref, s * L, c, (y_e + bias * v_e) * load(sx_ref, s * L, c))
        store(su_ref, s * L + 1, c, (y_o + bias * v_o) * load(sx_ref, s * L + 1, c))
        rows = slice(s * L, (s + 1) * L)
        for g in range(c * cn // LANES, (c + 1) * cn // LANES):
            u_ref[rows, g * LANES:(g + 1) * LANES] = _bf(su_ref[g, rows, :])


def _hy_conv_steps(n):
    nt = n // HY_CONV_TOKENS
    return (D_MODEL // HY_CONV_CHANNELS) * nt, lambda j, i: j * nt + i


def _hy_conv(vx, z0, sconv_w, sconv_b, tok0, n, coef, bias, L, casts=()):
    tm = HY_CONV_TOKENS
    dt = HY_CONV_CHANNELS
    assert tok0 % tm == 0 and n % tm == 0 and tm % L == 0
    src = pl.BlockSpec((tm, dt), lambda j, i: (tok0 // tm + i, j))
    dst = pl.BlockSpec((tm, dt), lambda j, i: (i, j))
    const = lambda shape: pl.BlockSpec(shape, lambda j, i: (0, 0), pipeline_mode=pl.Buffered(1))
    if L >= HY_SPLIT_MIN_LEN:
        fwd, inv = _dft_tables(L)
        tables = (*fwd, *inv)
        body = functools.partial(_hy_conv_split_kernel, L=L, nseq=tm // L, cn=HY_CONV_PIECE_CHANNELS)
        scratch = [pltpu.VMEM((dt // LANES, tm, LANES), F32)] * 3
    else:
        tables = _plain_dft_tables(L)[:2]
        body = functools.partial(_hy_conv_plain_kernel, L=L, nseq=tm // L, cn=HY_CONV_PIECE_CHANNELS)
        scratch = []
    return pl.pallas_call(
        _with_casts(body, 6 + len(tables), 1, casts),
        out_shape=(jax.ShapeDtypeStruct((n, D_MODEL), BF16),) + tuple(c.out_shape for c in casts),
        grid=(D_MODEL // dt, n // tm),
        in_specs=[
            src, src,
            pl.BlockSpec((3, dt), lambda j, i: (0, j)),
            pl.BlockSpec((1, dt), lambda j, i: (0, j)),
            pl.BlockSpec(_filter_coef_shape(L) + (dt,), lambda j, i: (0, 0, j)),
            pl.BlockSpec((1, dt), lambda j, i: (0, j)),
        ] + [const(t.shape) for t in tables] + [c.in_spec for c in casts],
        out_specs=(dst,) + tuple(c.out_spec for c in casts),
        scratch_shapes=scratch,
        compiler_params=_params("arbitrary", "arbitrary"),
        name=f"hyena_conv_{L}",
    )(vx, z0, sconv_w, sconv_b, coef, bias, *[_bf(jnp.asarray(t)) for t in tables], *[c.w for c in casts])


def _post_kernel(*refs, layer, ctx_steps, lat_len, n_x, n_a, a_feature_major):
    x_refs, a_refs = refs[:n_x], refs[n_x:n_x + n_a]
    (mod_ref, wo_ref, ln1g_ref, ln1b_ref, win_ref, wout_ref, ln2g_ref, ln2b_ref, o_ref,
     act_ref) = refs[n_x + n_a:]
    d = D_MODEL
    i = pl.program_id(0)
    is_ctx = i < ctx_steps
    tm = act_ref.shape[0]
    mod = mod_ref[pl.ds(_mod_row(i, ctx_steps, tm, lat_len), 1), :]
    g1 = mod[:, 2 * d:3 * d]
    sh2 = mod[:, 3 * d:4 * d]
    sc2 = mod[:, 4 * d:5 * d]
    g2 = mod[:, 5 * d:6 * d]
    ln1 = (ln1g_ref[layer:layer + 1, :], ln1b_ref[layer:layer + 1, :])
    ln2 = (ln2g_ref[layer:layer + 1, :], ln2b_ref[layer:layer + 1, :])

    groups = [slice(r, r + POST_GROUP_ROWS) for r in range(0, tm, POST_GROUP_ROWS)]
    if a_feature_major:
        mix = [_dot_tn(_pick(is_ctx, a_refs, (slice(None), rows)), wo_ref[...]) for rows in groups]
    else:
        mix = [_dot(_pick(is_ctx, a_refs, (rows, slice(None))), wo_ref[...]) for rows in groups]
    x1 = []
    for rows, mx in zip(groups, mix):
        x = _pick(is_ctx, x_refs, (rows, slice(None)))
        x1.append(_layer_norm(ALPHA * x + g1 * mx, *ln1))
        h = _bf(x1[-1] * (1.0 + sc2) + sh2)
        for c in range(D_FF // FFN_COLS):
            cols = slice(c * FFN_COLS, (c + 1) * FFN_COLS)
            gate = _dot(h, win_ref[:, cols])
            up = _dot(h, win_ref[:, D_FF + c * FFN_COLS:D_FF + (c + 1) * FFN_COLS])
            act_ref[rows, cols] = _bf(gate * jax.nn.sigmoid(gate) * up)
    for rows, xr in zip(groups, x1):
        f = _dot(act_ref[rows, :], wout_ref[...])
        o_ref[rows, :] = _layer_norm(ALPHA * xr + g2 * f, *ln2)


def _post(xs, mixes, n, n_ctx, lat_len, mods, layer, w_o, w_in, w_out, ln1_g, ln1_b, ln2_g, ln2_b, name,
          x_tok0=0, a_feature_major=False, casts=(), tm=POST_TOKENS):
    ctx_steps = n_ctx // tm
    ctx = lambda i: _ctx_block(i, ctx_steps)
    lat = lambda i: _lat_block(i, ctx_steps)
    blocks = lambda m, tok0: (lambda i: tok0 // tm + i,) if m == 1 else (ctx, lat)
    tok = lambda f: pl.BlockSpec((tm, D_MODEL), lambda i: (f(i), 0))
    feat = lambda f: pl.BlockSpec((D_MODEL, tm), lambda i: (0, f(i)))
    const = lambda shape: pl.BlockSpec(shape, lambda i: (0, 0), pipeline_mode=pl.Buffered(1))
    vec = pl.BlockSpec((DEPTH, D_MODEL), lambda i: (0, 0))
    body = functools.partial(_post_kernel, layer=layer, ctx_steps=ctx_steps, lat_len=lat_len, n_x=len(xs),
                             n_a=len(mixes), a_feature_major=a_feature_major)
    return pl.pallas_call(
        _with_casts(body, len(xs) + len(mixes) + 8, 1, casts),
        out_shape=(jax.ShapeDtypeStruct((n, D_MODEL), F32),) + tuple(c.out_shape for c in casts),
        grid=(n // tm,),
        in_specs=[tok(f) for f in blocks(len(xs), x_tok0)]
        + [(feat if a_feature_major else tok)(f) for f in blocks(len(mixes), 0)]
        + [pl.BlockSpec((None, MOD_ROWS, 6 * D_MODEL), lambda i: (layer, 0, 0)),
           const((D_MODEL, D_MODEL)), vec, vec,
           const((D_MODEL, 2 * D_FF)), const((D_FF, D_MODEL)), vec, vec]
        + [c.in_spec for c in casts],
        out_specs=(tok(lambda i: i),) + tuple(c.out_spec for c in casts),
        scratch_shapes=[pltpu.VMEM((tm, D_FF), BF16)],
        compiler_params=_params("arbitrary"),
        name=name,
    )(*xs, *mixes, mods, w_o, ln1_g, ln1_b, w_in, w_out, ln2_g, ln2_b, *[c.w for c in casts])


def _qkv_kernel(x_ref, mod_ref, wt_ref, qkvt_ref, nk_ref, nv_ref, *, ctx_steps, lat_len):
    d = D_MODEL
    i = pl.program_id(0)
    tm = x_ref.shape[0]
    mod = mod_ref[pl.ds(_mod_row(i, ctx_steps, tm, lat_len), 1), :]
    h = _bf(x_ref[...] * (1.0 + mod[:, d:2 * d]) + mod[:, 0:d])
    qkvt_ref[0:d, :] = _bf(_dot_nt(wt_ref[0:d, :], h) * (HEAD_DIM ** -0.5 * LOG2E))
    kt = _dot_nt(wt_ref[d:2 * d, :], h)
    qkvt_ref[d:2 * d, :] = _bf(kt)
    vt = _dot_nt(wt_ref[2 * d:3 * d, :], h)
    qkvt_ref[2 * d:3 * d, :] = _bf(vt)

    @pl.when(i < ctx_steps)
    def _():
        seq = nk_ref.shape[4]
        for s in range(tm // seq):
            nk_ref[s, 0] = kt[:, s * seq:(s + 1) * seq].reshape(N_HEADS, HEAD_DIM, seq)
            nv_ref[s, 0] = vt[:, s * seq:(s + 1) * seq].reshape(N_HEADS, HEAD_DIM, seq)


def _qkv(x2d, n_ctx, ctx_len, lat_len, mods, layer, w_qkv_t):
    n = x2d.shape[0]
    tm = QKV_TOKENS
    assert tm % ctx_len == 0 and n_ctx % tm == 0
    ctx_steps = n_ctx // tm
    kv_shape = jax.ShapeDtypeStruct((n_ctx // ctx_len, 1, N_HEADS, HEAD_DIM, ctx_len), F32)
    kv_spec = pl.BlockSpec((tm // ctx_len, 1, N_HEADS, HEAD_DIM, ctx_len),
                           lambda i: (_ctx_block(i, ctx_steps), 0, 0, 0, 0))
    return pl.pallas_call(
        functools.partial(_qkv_kernel, ctx_steps=ctx_steps, lat_len=lat_len),
        out_shape=(jax.ShapeDtypeStruct((3 * D_MODEL, n), BF16), kv_shape, kv_shape),
        grid=(n // tm,),
        in_specs=[
            pl.BlockSpec((tm, D_MODEL), lambda i: (i, 0)),
            pl.BlockSpec((None, MOD_ROWS, 6 * D_MODEL), lambda i: (layer, 0, 0)),
            pl.BlockSpec((3 * D_MODEL, D_MODEL), lambda i: (0, 0), pipeline_mode=pl.Buffered(1)),
        ],
        out_specs=(pl.BlockSpec((3 * D_MODEL, tm), lambda i: (0, i)), kv_spec, kv_spec),
        compiler_params=_params("arbitrary"),
        name="qkv",
    )(x2d, mods, w_qkv_t)


def _softmax_weights(*scores):
    m = functools.reduce(jnp.maximum, [jnp.max(s, axis=0, keepdims=True) for s in scores])
    return [_bf(jnp.exp2(s - m)) for s in scores]


def _weighted_values(vt, p):
    ones = jnp.ones((BF16_ROWS, vt.shape[1]), BF16)
    return _dot(jnp.concatenate([vt, ones], axis=0), p)


def _normalise(acc):
    return _bf(acc[0:HEAD_DIM] * (1.0 / acc[HEAD_DIM:HEAD_DIM + 1]))


def _ctx_attn_kernel(qt_ref, kt_ref, vt_ref, ot_ref, *, seq):
    units = [(s, hh) for s in range(qt_ref.shape[1] // seq) for hh in range(N_HEADS)]

    def block(unit):
        s, hh = unit
        return slice(hh * HEAD_DIM, (hh + 1) * HEAD_DIM), slice(s * seq, (s + 1) * seq)

    def scores(unit):
        return _dot_tn(kt_ref[block(unit)], qt_ref[block(unit)])

    pending = [scores(u) for u in units[:CTX_ATTN_LOOKAHEAD]]
    for i, unit in enumerate(units):
        if i + CTX_ATTN_LOOKAHEAD < len(units):
            pending.append(scores(units[i + CTX_ATTN_LOOKAHEAD]))
        (p,) = _softmax_weights(pending.pop(0))
        ot_ref[block(unit)] = _normalise(_weighted_values(vt_ref[block(unit)], p))


def _ctx_attn(qkvt, n_ctx, seq):
    tm = CTX_ATTN_SEQS * seq
    return pl.pallas_call(
        functools.partial(_ctx_attn_kernel, seq=seq),
        out_shape=jax.ShapeDtypeStruct((D_MODEL, n_ctx), BF16),
        grid=(n_ctx // tm,),
        in_specs=[pl.BlockSpec((D_MODEL, tm), lambda b, c=c: (c, b)) for c in range(3)],
        out_specs=pl.BlockSpec((D_MODEL, tm), lambda b: (0, b)),
        compiler_params=_params("arbitrary"),
        name="ctx_attn",
    )(qkvt, qkvt, qkvt)


LAT_ROWS = 16
Q_BLOCK_ROWS = 4


def _key_band(qb):
    row0 = [min(max(r - WIN_H // 2, 0), LAT_ROWS - WIN_H) for r in range(qb * Q_BLOCK_ROWS, (qb + 1) * Q_BLOCK_ROWS)]
    return min(row0), max(row0) + WIN_H


KEY_BANDS = tuple(_key_band(qb) for qb in range(LAT_ROWS // Q_BLOCK_ROWS))
MAX_BAND_ROWS = max(hi - lo for lo, hi in KEY_BANDS)


def _fill_bias_table(rpb_ref, tab_ref):
    shape = (GRID_W, LANES)
    kc = lax.broadcasted_iota(jnp.int32, shape, 0)
    lane = lax.broadcasted_iota(jnp.int32, shape, 1)
    qc = lane % GRID_W
    win0 = jnp.clip(qc - WIN_W // 2, 0, GRID_W - WIN_W)
    col_ok = (kc >= win0) & (kc < win0 + WIN_W)
    masked = jnp.full(shape, MASK_VALUE, F32)
    tiles = []
    for dr in range(2 * WIN_H - 1):
        row = jnp.broadcast_to(rpb_ref[dr:dr + 1, :] * LOG2E, shape)
        t = pltpu.roll(row, LANES - (WIN_W - 1), 1, stride=1, stride_axis=0)
        t = jnp.where(lane < GRID_W, t, pltpu.roll(t, GRID_W, 1))
        tiles.append(jnp.where(col_ok, t, MASK_VALUE))

    def tile(qr, kr):
        row0 = min(max(qr - WIN_H // 2, 0), LAT_ROWS - WIN_H)
        if row0 <= kr < row0 + WIN_H:
            return tiles[kr - qr + WIN_H - 1]
        return masked

    for qr in range(0, LAT_ROWS, 2):
        lo, hi = KEY_BANDS[qr // Q_BLOCK_ROWS]
        for i in range(hi - lo):
            both = jnp.where(lane < GRID_W, tile(qr, lo + i), tile(qr + 1, lo + i))
            tab_ref[i * GRID_W:(i + 1) * GRID_W, qr * GRID_W:(qr + 2) * GRID_W] = both


def _lat_attn_kernel(qt_ref, kt_ref, vt_ref, ckt_ref, cvt_ref, rpb_ref, ot_ref, tab_ref):
    seq = LAT_ROWS * GRID_W
    nq = Q_BLOCK_ROWS * GRID_W
    heads = qt_ref.shape[0] // HEAD_DIM
    units = [(b, hh, qb) for b in range(qt_ref.shape[1] // seq) for hh in range(heads)
             for qb in range(len(KEY_BANDS))]

    def rows(hh):
        return slice(hh * HEAD_DIM, (hh + 1) * HEAD_DIM)

    def keys(b, qb):
        lo, hi = KEY_BANDS[qb]
        return slice(b * seq + lo * GRID_W, b * seq + hi * GRID_W)

    def queries(b, qb):
        return slice(b * seq + qb * nq, b * seq + (qb + 1) * nq)

    def scores(unit):
        b, hh, qb = unit
        lo, hi = KEY_BANDS[qb]
        qt = qt_ref[rows(hh), queries(b, qb)]
        s_loc = (_dot_tn(kt_ref[rows(hh), keys(b, qb)], qt)
                 + tab_ref[hh, 0:(hi - lo) * GRID_W, qb * nq:(qb + 1) * nq])
        s_ctx = _dot_tn(_bf(ckt_ref[b, hh]), qt)
        return s_loc, s_ctx

    for hh in range(tab_ref.shape[0]):
        _fill_bias_table(rpb_ref.at[hh], tab_ref.at[hh])
    pending = [scores(u) for u in units[:LAT_ATTN_LOOKAHEAD]]
    for i, (b, hh, qb) in enumerate(units):
        if i + LAT_ATTN_LOOKAHEAD < len(units):
            pending.append(scores(units[i + LAT_ATTN_LOOKAHEAD]))
        s_loc, s_ctx = pending.pop(0)
        p_loc, p_ctx = _softmax_weights(s_loc, s_ctx)
        acc = (_weighted_values(vt_ref[rows(hh), keys(b, qb)], p_loc)
               + _weighted_values(_bf(cvt_ref[b, hh]), p_ctx))
        ot_ref[rows(hh), queries(b, qb)] = _normalise(acc)


def _lat_attn(qkvt, n_ctx, cache_kt, cache_vt, rpb, layer_idx):
    rpb = jnp.pad(rpb[:, :, ::-1], ((0, 0), (0, 1), (0, LANES - (2 * WIN_W - 1))))
    n = qkvt.shape[1] - n_ctx
    seq = LAT_ROWS * GRID_W
    assert n_ctx % n == 0 and n % seq == 0
    nb = n // seq
    hs = LAT_ATTN_HEADS
    nhp = N_HEADS // hs
    past = cache_kt.shape[4]
    feat = lambda c: pl.BlockSpec((hs * HEAD_DIM, n), lambda hp, c=c: (c * nhp + hp, n_ctx // n))
    cache = pl.BlockSpec((nb, None, hs, HEAD_DIM, past), lambda hp: (0, layer_idx, hp, 0, 0))
    return pl.pallas_call(
        _lat_attn_kernel,
        out_shape=jax.ShapeDtypeStruct((D_MODEL, n), BF16),
        grid=(nhp,),
        in_specs=[feat(0), feat(1), feat(2), cache, cache,
                  pl.BlockSpec((hs, 2 * WIN_H, LANES), lambda hp: (hp, 0, 0))],
        out_specs=pl.BlockSpec((hs * HEAD_DIM, n), lambda hp: (hp, 0)),
        scratch_shapes=[pltpu.VMEM((hs, MAX_BAND_ROWS * GRID_W, seq), F32)],
        compiler_params=_params("arbitrary"),
        name="lat_attn",
    )(qkvt, qkvt, qkvt, cache_kt, cache_vt, rpb)


def kernel(x_prompt, x_sample, cache_k, cache_v, c, c_ctx, ada_w, ada_b, ln1_g, ln1_b, ln2_g, ln2_b, ffn_w_in, ffn_w_out, hy_w_in, hy_sconv_w, hy_sconv_b, hy_pe_w1, hy_pe_b1, hy_pe_w2, hy_pe_b2, hy_pe_w3, hy_sin_freq, hy_decay, hy_bias, hy_w_out, na_w_qkv, na_rpb, na_w_out):
    nbp, lp, d = x_prompt.shape
    nbs, ls, _ = x_sample.shape
    assert d == D_MODEL and ls == LAT_ROWS * GRID_W and c.shape[0] == nbs
    xp = x_prompt.reshape(nbp * lp, d)
    xs = x_sample.reshape(nbs * ls, d)
    n_ctx, n_lat = xp.shape[0], xs.shape[0]
    n = n_ctx + n_lat
    ln = (ln1_g, ln1_b, ln2_g, ln2_b)

    filt = (hy_pe_w1[0], hy_pe_b1[0], hy_pe_w2[0], hy_pe_b2[0], hy_pe_w3[0], hy_sin_freq[0], hy_decay[0])
    steps = DEPTH * D_MODEL // FILTER_CHANNELS
    mods, coef_p, coef_s, hy_w_in0 = _prep(c_ctx, c, ada_w, ada_b, (lp, ls), *filt,
                                           casts=[_Cast(hy_w_in, 0, steps, _step)])

    sconv_b = hy_sconv_b[0].reshape(1, -1)
    hy_bias0 = hy_bias[0].reshape(1, -1)
    sconv = (hy_sconv_w[0], sconv_b)
    z0, vx, ffn_w_in0, ffn_w_out0 = _hy_in(xp, xs, lp, ls, mods, 0, hy_w_in0, *sconv,
                                           casts=[_Cast(ffn_w_in, 0, n // ls, _step),
                                                  _Cast(ffn_w_out, 0, n // ls, _step)])
    steps, lin = _hy_conv_steps(n_ctx)
    u_p, hy_w_out0 = _hy_conv(vx, z0, *sconv, 0, n_ctx, coef_p, hy_bias0, lp,
                              casts=[_Cast(hy_w_out, 0, steps, lin)])
    (u_s,) = _hy_conv(vx, z0, *sconv, n_ctx, n_lat, coef_s, hy_bias0, ls)
    steps = n // POST_TOKENS
    x1, ffn_w_in1, ffn_w_out1, na_w_out0, w_qkv_t = _post(
        (xp, xs), (u_p, u_s), n, n_ctx, ls, mods, 0, hy_w_out0, ffn_w_in0, ffn_w_out0, *ln, name="post0",
        casts=[_Cast(ffn_w_in, 1, steps, _step), _Cast(ffn_w_out, 1, steps, _step),
               _Cast(na_w_out, 0, steps, _step),
               _Cast(na_w_qkv, 0, steps // 2, lambda i: i // 2, transpose=True)])

    qkvt, new_kt, new_vt = _qkv(x1, n_ctx, lp, ls, mods, 1, w_qkv_t)
    at_p = _ctx_attn(qkvt, n_ctx, lp)
    at_s = _lat_attn(qkvt, n_ctx, jnp.swapaxes(cache_k, 3, 4), jnp.swapaxes(cache_v, 3, 4), na_rpb[0], 0)
    ffn1 = (na_w_out0, ffn_w_in1, ffn_w_out1, *ln)
    (yp,) = _post((x1,), (at_p,), n_ctx, n_ctx, ls, mods, 1, *ffn1, name="post1_ctx", a_feature_major=True)
    (ys,) = _post((x1,), (at_s,), n_lat, 0, ls, mods, 1, *ffn1, name="post1_lat", x_tok0=n_ctx,
                  a_feature_major=True)

    return (yp.reshape(nbp, lp, d), ys.reshape(nbs, ls, d),
            jnp.swapaxes(new_kt, 3, 4), jnp.swapaxes(new_vt, 3, 4))
```
